```python
import jax, jax.numpy as jnp
from jax import lax
import numpy as np

D_MODEL = 1024
BATCH = 8
SEQ = 8192
DEPTH = 4

N_EVEN = (DEPTH + 1) // 2
N_ODD = DEPTH // 2

MLA_HEADS = 8
MLA_Q_RANK = 384
MLA_KV_RANK = 256
MLA_NOPE = 64
MLA_ROPE = 32
MLA_V = 64
MLA_QK = MLA_NOPE + MLA_ROPE
Q_BLOCK = 128

RET_HEADS = 8
RET_DK = 64
RET_DV = 64
RET_CHUNK = 128
RET_DECAY_BASE = 5.0

GLA_HEADS = 4
GLA_DK = 128
GLA_DV = 256
GLA_GATE_RANK = 16
GLA_TAU = 16.0
GLA_CHUNK = 64

D_FF = 2816
CONV_W = 3

ROPE_THETA = 10000.0
EPS = 1e-6

EVEN_SPLIT = (MLA_Q_RANK, MLA_KV_RANK, MLA_ROPE,
              RET_HEADS * RET_DK, RET_HEADS * RET_DK, RET_HEADS * RET_DV, RET_HEADS * RET_DV)
EVEN_IN = MLA_Q_RANK + MLA_KV_RANK + MLA_ROPE + 2 * RET_HEADS * RET_DK + 2 * RET_HEADS * RET_DV
EVEN_OUT = MLA_HEADS * MLA_V + RET_HEADS * RET_DV
ODD_SPLIT = (GLA_HEADS * GLA_DK, GLA_HEADS * GLA_DK, GLA_HEADS * GLA_DV, GLA_HEADS * GLA_DV,
             GLA_GATE_RANK, GLA_GATE_RANK)
ODD_IN = 2 * GLA_HEADS * GLA_DK + 2 * GLA_HEADS * GLA_DV + 2 * GLA_GATE_RANK
ODD_OUT = GLA_HEADS * GLA_DV

kernel_name = "hybrid_mla_retention_gla_convffn_encoder"


def _split(p, sizes):
    outs, s = [], 0
    for n in sizes:
        outs.append(p[..., s:s + n])
        s += n
    return outs


def _rmsnorm(x, g):
    xf = x.astype(jnp.float32)
    y = xf * lax.rsqrt(jnp.mean(xf * xf, axis=-1, keepdims=True) + EPS)
    return (y * g.astype(jnp.float32)).astype(x.dtype)


def _rope(x, positions):
    half = x.shape[-1] // 2
    inv = ROPE_THETA ** (-jnp.arange(half, dtype=jnp.float32) / half)
    ang = positions.astype(jnp.float32)[:, :, None] * inv
    cos = jnp.cos(ang)[:, :, None, :]
    sin = jnp.sin(ang)[:, :, None, :]
    x1 = x[..., :half].astype(jnp.float32)
    x2 = x[..., half:].astype(jnp.float32)
    return jnp.concatenate([x1 * cos - x2 * sin, x2 * cos + x1 * sin], axis=-1).astype(x.dtype)


def _mla(cq, ckv, k_rope, positions, q_norm, kv_norm, w_uq, w_ukv, q_head_norm, k_head_norm):
    B, S, _ = cq.shape
    H = MLA_HEADS
    q = (_rmsnorm(cq, q_norm) @ w_uq).reshape(B, S, H, MLA_QK)
    kv = (_rmsnorm(ckv, kv_norm) @ w_ukv).reshape(B, S, H, MLA_NOPE + MLA_V)
    k_nope, v = kv[..., :MLA_NOPE], kv[..., MLA_NOPE:]
    k = jnp.concatenate([k_nope, jnp.broadcast_to(k_rope[:, :, None, :], (B, S, H, MLA_ROPE))], axis=-1)
    q = _rmsnorm(q, q_head_norm)
    k = _rmsnorm(k, k_head_norm)
    q = jnp.concatenate([q[..., :MLA_NOPE], _rope(q[..., MLA_NOPE:], positions)], axis=-1)
    k = jnp.concatenate([k[..., :MLA_NOPE], _rope(k[..., MLA_NOPE:], positions)], axis=-1)
    scale = MLA_QK ** -0.5
    nb = S // Q_BLOCK
    qb = q.reshape(B, nb, Q_BLOCK, H, MLA_QK).transpose(1, 0, 2, 3, 4)

    def block(qi):
        s = jnp.einsum('bqhd,bkhd->bhqk', qi, k).astype(jnp.float32) * scale
        p = jax.nn.softmax(s, axis=-1).astype(v.dtype)
        return jnp.einsum('bhqk,bkhe->bqhe', p, v)

    o = lax.map(block, qb)
    return o.transpose(1, 0, 2, 3, 4).reshape(B, S, H * MLA_V)


def _retention_dir(q, k, v, log_gamma, include_diag):
    B, S, H, dk = q.shape
    dv = v.shape[-1]
    C = RET_CHUNK
    n = S // C
    q = q.reshape(B, n, C, H, dk)
    k = k.reshape(B, n, C, H, dk)
    v = v.reshape(B, n, C, H, dv)
    lg = log_gamma.astype(jnp.float32)
    pos = jnp.arange(C, dtype=jnp.float32)
    rel = pos[:, None] - pos[None, :]
    mask = (rel >= 0) if include_diag else (rel > 0)
    decay = jnp.where(mask[None], jnp.exp(lg[:, None, None] * jnp.maximum(rel, 0.0)[None]), 0.0)
    scores = jnp.einsum('bnihd,bnjhd->bnhij', q, k) * decay
    intra = jnp.einsum('bnhij,bnjhe->bnihe', scores, v)
    zeta = jnp.exp(lg[:, None] * (C - 1 - pos)[None])
    chunk_state = jnp.einsum('bnjhd,bnjhe,hj->bnhde', k, v, zeta)
    chunk_decay = jnp.exp(lg * C)[None, :, None, None]

    def step(R, s):
        return chunk_decay * R + s, R

    R0 = jnp.zeros((B, H, dk, dv), chunk_state.dtype)
    _, R_prev = lax.scan(step, R0, jnp.moveaxis(chunk_state, 1, 0))
    R_prev = jnp.moveaxis(R_prev, 0, 1)
    xi = jnp.exp(lg[:, None] * (pos + 1.0)[None]).T
    cross = jnp.einsum('bnihd,bnhde->bnihe', q, R_prev) * xi[None, None, :, :, None]
    return (intra + cross).reshape(B, S, H, dv)


def _retention(rq, rk, rv, rg, positions, theta_fwd, theta_bwd, out_norm):
    B, S, _ = rq.shape
    H = RET_HEADS
    q = _rope(rq.reshape(B, S, H, RET_DK), positions)
    k = _rope(rk.reshape(B, S, H, RET_DK), positions) * (RET_DK ** -0.5)
    v = rv.reshape(B, S, H, RET_DV)
    lg_f = jnp.log1p(-jnp.exp2(-theta_fwd.astype(jnp.float32)))
    lg_b = jnp.log1p(-jnp.exp2(-theta_bwd.astype(jnp.float32)))
    o_f = _retention_dir(q, k, v, lg_f, True)
    o_b = jnp.flip(_retention_dir(jnp.flip(q, 1), jnp.flip(k, 1), jnp.flip(v, 1), lg_b, False), 1)
    o = _rmsnorm(o_f + o_b, out_norm)
    return jax.nn.silu(rg) * o.reshape(B, S, H * RET_DV)


def _gla_dir(q, k, v, log_a, include_diag):
    B, S, H, dk = q.shape
    dv = v.shape[-1]
    C = GLA_CHUNK
    n = S // C
    q = q.reshape(B, n, C, H, dk)
    k = k.reshape(B, n, C, H, dk)
    v = v.reshape(B, n, C, H, dv)
    b = jnp.cumsum(log_a.astype(jnp.float32).reshape(B, n, C, H, dk), axis=2)
    b_mid = b[:, :, C // 2:C // 2 + 1]
    b_last = b[:, :, -1]
    qc = q * jnp.exp(b - b_mid)
    kc = k * jnp.exp(b_mid - b)
    A = jnp.einsum('bnihd,bnjhd->bnhij', qc, kc)
    pos = jnp.arange(C)
    mask = (pos[:, None] >= pos[None, :]) if include_diag else (pos[:, None] > pos[None, :])
    A = jnp.where(mask, A, 0.0)
    intra = jnp.einsum('bnhij,bnjhe->bnihe', A, v)
    k_dec = k * jnp.exp(b_last[:, :, None] - b)
    chunk_state = jnp.einsum('bnjhd,bnjhe->bnhde', k_dec, v)
    chunk_decay = jnp.exp(b_last)

    def step(Sm, xs):
        dcy, st = xs
        return dcy[..., None] * Sm + st, Sm

    S0 = jnp.zeros((B, H, dk, dv), chunk_state.dtype)
    _, S_prev = lax.scan(step, S0, (jnp.moveaxis(chunk_decay, 1, 0), jnp.moveaxis(chunk_state, 1, 0)))
    S_prev = jnp.moveaxis(S_prev, 0, 1)
    inter = jnp.einsum('bnihd,bnhde->bnihe', q * jnp.exp(b), S_prev)
    return (intra + inter).reshape(B, S, H, dv)


def _gla(gq, gk, gv, gr, ga_f, ga_b, w_gate_fwd, b_gate_fwd, w_gate_bwd, b_gate_bwd, out_norm):
    B, S, _ = gq.shape
    H = GLA_HEADS
    q = gq.reshape(B, S, H, GLA_DK) * (GLA_DK ** -0.5)
    k = gk.reshape(B, S, H, GLA_DK)
    v = gv.reshape(B, S, H, GLA_DV)
    la_f = (jax.nn.log_sigmoid((ga_f @ w_gate_fwd + b_gate_fwd).astype(jnp.float32)) / GLA_TAU).reshape(B, S, H, GLA_DK)
    la_b = (jax.nn.log_sigmoid((ga_b @ w_gate_bwd + b_gate_bwd).astype(jnp.float32)) / GLA_TAU).reshape(B, S, H, GLA_DK)
    o_f = _gla_dir(q, k, v, la_f, True)
    o_b = jnp.flip(_gla_dir(jnp.flip(q, 1), jnp.flip(k, 1), jnp.flip(v, 1), jnp.flip(la_b, 1), False), 1)
    o = _rmsnorm(o_f + o_b, out_norm)
    return jax.nn.silu(gr) * o.reshape(B, S, H * GLA_DV)


def _conv_ffn(x, norm_g, w_up, conv_w, conv_b, w_down):
    h = _rmsnorm(x, norm_g)
    up = h @ w_up
    gate, val = up[..., :D_FF], up[..., D_FF:]
    gate = lax.conv_general_dilated(gate, conv_w[:, None, :].astype(gate.dtype), window_strides=(1,),
                                    padding='SAME', dimension_numbers=('NWC', 'WIO', 'NWC'),
                                    feature_group_count=D_FF) + conv_b
    return (jax.nn.silu(gate) * val) @ w_down


def _fwd_setup_inputs(seed: int = 0) -> dict:
    key = jax.random.key(seed)
    ks = iter(jax.random.split(key, 40))

    def dense(shape, fan_in):
        return jax.random.normal(next(ks), shape, jnp.float32) * (fan_in ** -0.5)

    def gain(shape):
        return 1.0 + 0.02 * jax.random.normal(next(ks), shape, jnp.float32)

    def small(shape, s):
        return s * jax.random.normal(next(ks), shape, jnp.float32)

    x = jax.random.normal(next(ks), (BATCH, SEQ, D_MODEL), jnp.float32)
    start = jax.random.randint(next(ks), (BATCH, 1), 0, 4096)
    positions = (start + jnp.arange(SEQ)[None, :]).astype(jnp.int32)
    ret_base = RET_DECAY_BASE + jnp.arange(RET_HEADS, dtype=jnp.float32)
    return {
        "x": x,
        "positions": positions,
        "mix_norm_even": gain((N_EVEN, D_MODEL)),
        "w_in_even": dense((N_EVEN, D_MODEL, EVEN_IN), D_MODEL),
        "mla_q_norm": gain((N_EVEN, MLA_Q_RANK)),
        "mla_kv_norm": gain((N_EVEN, MLA_KV_RANK)),
        "mla_w_uq": dense((N_EVEN, MLA_Q_RANK, MLA_HEADS * MLA_QK), MLA_Q_RANK),
        "mla_w_ukv": dense((N_EVEN, MLA_KV_RANK, MLA_HEADS * (MLA_NOPE + MLA_V)), MLA_KV_RANK),
        "mla_q_head_norm": gain((N_EVEN, MLA_QK)),
        "mla_k_head_norm": gain((N_EVEN, MLA_QK)),
        "ret_theta_fwd": ret_base + small((N_EVEN, RET_HEADS), 0.1),
        "ret_theta_bwd": ret_base + small((N_EVEN, RET_HEADS), 0.1),
        "ret_out_norm": gain((N_EVEN, RET_HEADS, RET_DV)),
        "w_out_even": dense((N_EVEN, EVEN_OUT, D_MODEL), EVEN_OUT),
        "mix_norm_odd": gain((N_ODD, D_MODEL)),
        "w_in_odd": dense((N_ODD, D_MODEL, ODD_IN), D_MODEL),
        "gla_w_gate_fwd": dense((N_ODD, GLA_GATE_RANK, GLA_HEADS * GLA_DK), GLA_GATE_RANK),
        "gla_b_gate_fwd": small((N_ODD, GLA_HEADS * GLA_DK), 0.1),
        "gla_w_gate_bwd": dense((N_ODD, GLA_GATE_RANK, GLA_HEADS * GLA_DK), GLA_GATE_RANK),
        "gla_b_gate_bwd": small((N_ODD, GLA_HEADS * GLA_DK), 0.1),
        "gla_out_norm": gain((N_ODD, GLA_HEADS, GLA_DV)),
        "w_out_odd": dense((N_ODD, ODD_OUT, D_MODEL), ODD_OUT),
        "ffn_norm": gain((DEPTH, D_MODEL)),
        "ffn_w_up": dense((DEPTH, D_MODEL, 2 * D_FF), D_MODEL),
        "ffn_conv_w": dense((DEPTH, CONV_W, D_FF), CONV_W),
        "ffn_conv_b": small((DEPTH, D_FF), 0.02),
        "ffn_w_down": dense((DEPTH, D_FF, D_MODEL), D_FF),
    }


def _fwd_reference(x, positions, mix_norm_even, w_in_even, mla_q_norm, mla_kv_norm, mla_w_uq, mla_w_ukv,
              mla_q_head_norm, mla_k_head_norm, ret_theta_fwd, ret_theta_bwd, ret_out_norm, w_out_even,
              mix_norm_odd, w_in_odd, gla_w_gate_fwd, gla_b_gate_fwd, gla_w_gate_bwd, gla_b_gate_bwd,
              gla_out_norm, w_out_odd, ffn_norm, ffn_w_up, ffn_conv_w, ffn_conv_b, ffn_w_down):
    for layer in range(DEPTH):
        i = layer // 2
        if layer % 2 == 0:
            h = _rmsnorm(x, mix_norm_even[i])
            cq, ckv, k_rope, rq, rk, rv, rg = _split(h @ w_in_even[i], EVEN_SPLIT)
            a = _mla(cq, ckv, k_rope, positions, mla_q_norm[i], mla_kv_norm[i], mla_w_uq[i], mla_w_ukv[i],
                     mla_q_head_norm[i], mla_k_head_norm[i])
            r = _retention(rq, rk, rv, rg, positions, ret_theta_fwd[i], ret_theta_bwd[i], ret_out_norm[i])
            x = x + jnp.concatenate([a, r], axis=-1) @ w_out_even[i]
        else:
            h = _rmsnorm(x, mix_norm_odd[i])
            gq, gk, gv, gr, ga_f, ga_b = _split(h @ w_in_odd[i], ODD_SPLIT)
            g = _gla(gq, gk, gv, gr, ga_f, ga_b, gla_w_gate_fwd[i], gla_b_gate_fwd[i],
                     gla_w_gate_bwd[i], gla_b_gate_bwd[i], gla_out_norm[i])
            x = x + g @ w_out_odd[i]
        x = x + _conv_ffn(x, ffn_norm[layer], ffn_w_up[layer], ffn_conv_w[layer], ffn_conv_b[layer],
                          ffn_w_down[layer])
    return x


import jax as _jax
import jax.numpy as _jnp

TWIN_FORMAT = 'train_step'
FWD_PARAMS = ['x', 'positions', 'mix_norm_even', 'w_in_even', 'mla_q_norm', 'mla_kv_norm', 'mla_w_uq', 'mla_w_ukv', 'mla_q_head_norm', 'mla_k_head_norm', 'ret_theta_fwd', 'ret_theta_bwd', 'ret_out_norm', 'w_out_even', 'mix_norm_odd', 'w_in_odd', 'gla_w_gate_fwd', 'gla_b_gate_fwd', 'gla_w_gate_bwd', 'gla_b_gate_bwd', 'gla_out_norm', 'w_out_odd', 'ffn_norm', 'ffn_w_up', 'ffn_conv_w', 'ffn_conv_b', 'ffn_w_down']
TWIN_WEIGHTS = ['mix_norm_even', 'w_in_even', 'mla_q_norm', 'mla_kv_norm', 'mla_w_uq', 'mla_w_ukv', 'mla_q_head_norm', 'mla_k_head_norm', 'ret_theta_fwd', 'ret_theta_bwd', 'ret_out_norm', 'w_out_even', 'mix_norm_odd', 'w_in_odd', 'gla_w_gate_fwd', 'gla_b_gate_fwd', 'gla_w_gate_bwd', 'gla_b_gate_bwd', 'gla_out_norm', 'w_out_odd', 'ffn_norm', 'ffn_w_up', 'ffn_conv_w', 'ffn_conv_b', 'ffn_w_down']
TWIN_DIFF_INPUT = 'x'
TWIN_INPUTS = ['x', 'positions', 'mix_norm_even', 'w_in_even', 'mla_q_norm', 'mla_kv_norm', 'mla_w_uq', 'mla_w_ukv', 'mla_q_head_norm', 'mla_k_head_norm', 'ret_theta_fwd', 'ret_theta_bwd', 'ret_out_norm', 'w_out_even', 'mix_norm_odd', 'w_in_odd', 'gla_w_gate_fwd', 'gla_b_gate_fwd', 'gla_w_gate_bwd', 'gla_b_gate_bwd', 'gla_out_norm', 'w_out_odd', 'ffn_norm', 'ffn_w_up', 'ffn_conv_w', 'ffn_conv_b', 'ffn_w_down', 'loss_target', 'm_mix_norm_even', 'm_w_in_even', 'm_mla_q_norm', 'm_mla_kv_norm', 'm_mla_w_uq', 'm_mla_w_ukv', 'm_mla_q_head_norm', 'm_mla_k_head_norm', 'm_ret_theta_fwd', 'm_ret_theta_bwd', 'm_ret_out_norm', 'm_w_out_even', 'm_mix_norm_odd', 'm_w_in_odd', 'm_gla_w_gate_fwd', 'm_gla_b_gate_fwd', 'm_gla_w_gate_bwd', 'm_gla_b_gate_bwd', 'm_gla_out_norm', 'm_w_out_odd', 'm_ffn_norm', 'm_ffn_w_up', 'm_ffn_conv_w', 'm_ffn_conv_b', 'm_ffn_w_down', 'v_mix_norm_even', 'v_w_in_even', 'v_mla_q_norm', 'v_mla_kv_norm', 'v_mla_w_uq', 'v_mla_w_ukv', 'v_mla_q_head_norm', 'v_mla_k_head_norm', 'v_ret_theta_fwd', 'v_ret_theta_bwd', 'v_ret_out_norm', 'v_w_out_even', 'v_mix_norm_odd', 'v_w_in_odd', 'v_gla_w_gate_fwd', 'v_gla_b_gate_fwd', 'v_gla_w_gate_bwd', 'v_gla_b_gate_bwd', 'v_gla_out_norm', 'v_w_out_odd', 'v_ffn_norm', 'v_ffn_w_up', 'v_ffn_conv_w', 'v_ffn_conv_b', 'v_ffn_w_down']
TWIN_OUTPUTS = ['loss', 'grad_x', 'grad_mix_norm_even', 'grad_w_in_even', 'grad_mla_q_norm', 'grad_mla_kv_norm', 'grad_mla_w_uq', 'grad_mla_w_ukv', 'grad_mla_q_head_norm', 'grad_mla_k_head_norm', 'grad_ret_theta_fwd', 'grad_ret_theta_bwd', 'grad_ret_out_norm', 'grad_w_out_even', 'grad_mix_norm_odd', 'grad_w_in_odd', 'grad_gla_w_gate_fwd', 'grad_gla_b_gate_fwd', 'grad_gla_w_gate_bwd', 'grad_gla_b_gate_bwd', 'grad_gla_out_norm', 'grad_w_out_odd', 'grad_ffn_norm', 'grad_ffn_w_up', 'grad_ffn_conv_w', 'grad_ffn_conv_b', 'grad_ffn_w_down', 'delta_mix_norm_even', 'delta_w_in_even', 'delta_mla_q_norm', 'delta_mla_kv_norm', 'delta_mla_w_uq', 'delta_mla_w_ukv', 'delta_mla_q_head_norm', 'delta_mla_k_head_norm', 'delta_ret_theta_fwd', 'delta_ret_theta_bwd', 'delta_ret_out_norm', 'delta_w_out_even', 'delta_mix_norm_odd', 'delta_w_in_odd', 'delta_gla_w_gate_fwd', 'delta_gla_b_gate_fwd', 'delta_gla_w_gate_bwd', 'delta_gla_b_gate_bwd', 'delta_gla_out_norm', 'delta_w_out_odd', 'delta_ffn_norm', 'delta_ffn_w_up', 'delta_ffn_conv_w', 'delta_ffn_conv_b', 'delta_ffn_w_down', 'new_m_mix_norm_even', 'new_m_w_in_even', 'new_m_mla_q_norm', 'new_m_mla_kv_norm', 'new_m_mla_w_uq', 'new_m_mla_w_ukv', 'new_m_mla_q_head_norm', 'new_m_mla_k_head_norm', 'new_m_ret_theta_fwd', 'new_m_ret_theta_bwd', 'new_m_ret_out_norm', 'new_m_w_out_even', 'new_m_mix_norm_odd', 'new_m_w_in_odd', 'new_m_gla_w_gate_fwd', 'new_m_gla_b_gate_fwd', 'new_m_gla_w_gate_bwd', 'new_m_gla_b_gate_bwd', 'new_m_gla_out_norm', 'new_m_w_out_odd', 'new_m_ffn_norm', 'new_m_ffn_w_up', 'new_m_ffn_conv_w', 'new_m_ffn_conv_b', 'new_m_ffn_w_down', 'new_v_mix_norm_even', 'new_v_w_in_even', 'new_v_mla_q_norm', 'new_v_mla_kv_norm', 'new_v_mla_w_uq', 'new_v_mla_w_ukv', 'new_v_mla_q_head_norm', 'new_v_mla_k_head_norm', 'new_v_ret_theta_fwd', 'new_v_ret_theta_bwd', 'new_v_ret_out_norm', 'new_v_w_out_even', 'new_v_mix_norm_odd', 'new_v_w_in_odd', 'new_v_gla_w_gate_fwd', 'new_v_gla_b_gate_fwd', 'new_v_gla_w_gate_bwd', 'new_v_gla_b_gate_bwd', 'new_v_gla_out_norm', 'new_v_w_out_odd', 'new_v_ffn_norm', 'new_v_ffn_w_up', 'new_v_ffn_conv_w', 'new_v_ffn_conv_b', 'new_v_ffn_w_down']
TWIN_LEAF_KINDS = {'loss': 'loss', 'grad_x': 'grad_x', 'grad_mix_norm_even': 'grad_w', 'grad_w_in_even': 'grad_w', 'grad_mla_q_norm': 'grad_w', 'grad_mla_kv_norm': 'grad_w', 'grad_mla_w_uq': 'grad_w', 'grad_mla_w_ukv': 'grad_w', 'grad_mla_q_head_norm': 'grad_w', 'grad_mla_k_head_norm': 'grad_w', 'grad_ret_theta_fwd': 'grad_w', 'grad_ret_theta_bwd': 'grad_w', 'grad_ret_out_norm': 'grad_w', 'grad_w_out_even': 'grad_w', 'grad_mix_norm_odd': 'grad_w', 'grad_w_in_odd': 'grad_w', 'grad_gla_w_gate_fwd': 'grad_w', 'grad_gla_b_gate_fwd': 'grad_w', 'grad_gla_w_gate_bwd': 'grad_w', 'grad_gla_b_gate_bwd': 'grad_w', 'grad_gla_out_norm': 'grad_w', 'grad_w_out_odd': 'grad_w', 'grad_ffn_norm': 'grad_w', 'grad_ffn_w_up': 'grad_w', 'grad_ffn_conv_w': 'grad_w', 'grad_ffn_conv_b': 'grad_w', 'grad_ffn_w_down': 'grad_w', 'delta_mix_norm_even': 'delta_w', 'delta_w_in_even': 'delta_w', 'delta_mla_q_norm': 'delta_w', 'delta_mla_kv_norm': 'delta_w', 'delta_mla_w_uq': 'delta_w', 'delta_mla_w_ukv': 'delta_w', 'delta_mla_q_head_norm': 'delta_w', 'delta_mla_k_head_norm': 'delta_w', 'delta_ret_theta_fwd': 'delta_w', 'delta_ret_theta_bwd': 'delta_w', 'delta_ret_out_norm': 'delta_w', 'delta_w_out_even': 'delta_w', 'delta_mix_norm_odd': 'delta_w', 'delta_w_in_odd': 'delta_w', 'delta_gla_w_gate_fwd': 'delta_w', 'delta_gla_b_gate_fwd': 'delta_w', 'delta_gla_w_gate_bwd': 'delta_w', 'delta_gla_b_gate_bwd': 'delta_w', 'delta_gla_out_norm': 'delta_w', 'delta_w_out_odd': 'delta_w', 'delta_ffn_norm': 'delta_w', 'delta_ffn_w_up': 'delta_w', 'delta_ffn_conv_w': 'delta_w', 'delta_ffn_conv_b': 'delta_w', 'delta_ffn_w_down': 'delta_w', 'new_m_mix_norm_even': 'new_m', 'new_m_w_in_even': 'new_m', 'new_m_mla_q_norm': 'new_m', 'new_m_mla_kv_norm': 'new_m', 'new_m_mla_w_uq': 'new_m', 'new_m_mla_w_ukv': 'new_m', 'new_m_mla_q_head_norm': 'new_m', 'new_m_mla_k_head_norm': 'new_m', 'new_m_ret_theta_fwd': 'new_m', 'new_m_ret_theta_bwd': 'new_m', 'new_m_ret_out_norm': 'new_m', 'new_m_w_out_even': 'new_m', 'new_m_mix_norm_odd': 'new_m', 'new_m_w_in_odd': 'new_m', 'new_m_gla_w_gate_fwd': 'new_m', 'new_m_gla_b_gate_fwd': 'new_m', 'new_m_gla_w_gate_bwd': 'new_m', 'new_m_gla_b_gate_bwd': 'new_m', 'new_m_gla_out_norm': 'new_m', 'new_m_w_out_odd': 'new_m', 'new_m_ffn_norm': 'new_m', 'new_m_ffn_w_up': 'new_m', 'new_m_ffn_conv_w': 'new_m', 'new_m_ffn_conv_b': 'new_m', 'new_m_ffn_w_down': 'new_m', 'new_v_mix_norm_even': 'new_v', 'new_v_w_in_even': 'new_v', 'new_v_mla_q_norm': 'new_v', 'new_v_mla_kv_norm': 'new_v', 'new_v_mla_w_uq': 'new_v', 'new_v_mla_w_ukv': 'new_v', 'new_v_mla_q_head_norm': 'new_v', 'new_v_mla_k_head_norm': 'new_v', 'new_v_ret_theta_fwd': 'new_v', 'new_v_ret_theta_bwd': 'new_v', 'new_v_ret_out_norm': 'new_v', 'new_v_w_out_even': 'new_v', 'new_v_mix_norm_odd': 'new_v', 'new_v_w_in_odd': 'new_v', 'new_v_gla_w_gate_fwd': 'new_v', 'new_v_gla_b_gate_fwd': 'new_v', 'new_v_gla_w_gate_bwd': 'new_v', 'new_v_gla_b_gate_bwd': 'new_v', 'new_v_gla_out_norm': 'new_v', 'new_v_w_out_odd': 'new_v', 'new_v_ffn_norm': 'new_v', 'new_v_ffn_w_up': 'new_v', 'new_v_ffn_conv_w': 'new_v', 'new_v_ffn_conv_b': 'new_v', 'new_v_ffn_w_down': 'new_v'}


def _forward(args):
    return _fwd_reference(*[args[k] for k in FWD_PARAMS])


def _output_shape():
    def fwd():
        inp = _fwd_setup_inputs(0)
        return _fwd_reference(*[inp[k] for k in FWD_PARAMS])
    out = _jax.eval_shape(fwd)
    return out.shape, out.dtype

N_MICROBATCH = 1
ADAM_LR = 0.001
ADAM_B1 = 0.9
ADAM_B2 = 0.999
ADAM_EPS = 1e-08
ADAM_WD = 0.01
ADAM_STEP = 10
PER_EXAMPLE_BATCH_AXIS = {'x': 0, 'positions': 0, 'loss_target': 0}
SHARED_INPUTS = []
_WEIGHT_DTYPES = {'mix_norm_even': _jnp.float32, 'w_in_even': _jnp.float32, 'mla_q_norm': _jnp.float32, 'mla_kv_norm': _jnp.float32, 'mla_w_uq': _jnp.float32, 'mla_w_ukv': _jnp.float32, 'mla_q_head_norm': _jnp.float32, 'mla_k_head_norm': _jnp.float32, 'ret_theta_fwd': _jnp.float32, 'ret_theta_bwd': _jnp.float32, 'ret_out_norm': _jnp.float32, 'w_out_even': _jnp.float32, 'mix_norm_odd': _jnp.float32, 'w_in_odd': _jnp.float32, 'gla_w_gate_fwd': _jnp.float32, 'gla_b_gate_fwd': _jnp.float32, 'gla_w_gate_bwd': _jnp.float32, 'gla_b_gate_bwd': _jnp.float32, 'gla_out_norm': _jnp.float32, 'w_out_odd': _jnp.float32, 'ffn_norm': _jnp.float32, 'ffn_w_up': _jnp.float32, 'ffn_conv_w': _jnp.float32, 'ffn_conv_b': _jnp.float32, 'ffn_w_down': _jnp.float32}
MOMENT_SCALE = {'mix_norm_even': 1.129207e+01, 'w_in_even': 1.014436e+00, 'mla_q_norm': 2.483268e-01, 'mla_kv_norm': 5.824738e-01, 'mla_w_uq': 1.727182e-01, 'mla_w_ukv': 2.143568e-01, 'mla_q_head_norm': 8.369228e-01, 'mla_k_head_norm': 8.389455e-01, 'ret_theta_fwd': 3.164217e+00, 'ret_theta_bwd': 5.391163e+00, 'ret_out_norm': 2.221213e+01, 'w_out_even': 8.502266e-01, 'mix_norm_odd': 2.611582e+01, 'w_in_odd': 8.207212e-01, 'gla_w_gate_fwd': 7.471812e-02, 'gla_b_gate_fwd': 3.323759e-01, 'gla_w_gate_bwd': 7.365222e-02, 'gla_b_gate_bwd': 2.963041e-01, 'gla_out_norm': 2.270272e+01, 'w_out_odd': 7.552979e-01, 'ffn_norm': 5.203505e+01, 'ffn_w_up': 5.525127e-01, 'ffn_conv_w': 5.502707e+00, 'ffn_conv_b': 6.832808e+00, 'ffn_w_down': 9.278417e-01}


def _to_microbatches(a, axis):
    t = _jnp.moveaxis(a, axis, 0)
    t = t.reshape((N_MICROBATCH, t.shape[0] // N_MICROBATCH) + t.shape[1:])
    return _jnp.moveaxis(t, 1, axis + 1)


def setup_inputs(seed: int = 0) -> dict:
    inp = _fwd_setup_inputs(seed)
    key = _jax.random.fold_in(_jax.random.key(seed), 7919)
    shape, _ = _output_shape()
    out = dict(inp)
    out["loss_target"] = _jax.random.normal(_jax.random.fold_in(key, 0), shape, _jnp.float32)
    for i, name in enumerate(TWIN_WEIGHTS):
        w = inp[name].astype(_jnp.float32)
        if MOMENT_SCALE is None:
            s = _jnp.sqrt(_jnp.mean(_jnp.square(w)) + 1e-30)
        else:
            s = MOMENT_SCALE[name]
        km, kv = _jax.random.split(_jax.random.fold_in(key, i + 1))
        out[name] = w
        out["m_" + name] = s * _jax.random.normal(km, w.shape, _jnp.float32)
        out["v_" + name] = (s * s) * _jax.random.uniform(kv, w.shape, _jnp.float32, 0.5, 1.5)
    if N_MICROBATCH > 1:
        for name, axis in PER_EXAMPLE_BATCH_AXIS.items():
            out[name] = _to_microbatches(out[name], axis)
    return {'x': out['x'], 'positions': out['positions'], 'mix_norm_even': out['mix_norm_even'], 'w_in_even': out['w_in_even'], 'mla_q_norm': out['mla_q_norm'], 'mla_kv_norm': out['mla_kv_norm'], 'mla_w_uq': out['mla_w_uq'], 'mla_w_ukv': out['mla_w_ukv'], 'mla_q_head_norm': out['mla_q_head_norm'], 'mla_k_head_norm': out['mla_k_head_norm'], 'ret_theta_fwd': out['ret_theta_fwd'], 'ret_theta_bwd': out['ret_theta_bwd'], 'ret_out_norm': out['ret_out_norm'], 'w_out_even': out['w_out_even'], 'mix_norm_odd': out['mix_norm_odd'], 'w_in_odd': out['w_in_odd'], 'gla_w_gate_fwd': out['gla_w_gate_fwd'], 'gla_b_gate_fwd': out['gla_b_gate_fwd'], 'gla_w_gate_bwd': out['gla_w_gate_bwd'], 'gla_b_gate_bwd': out['gla_b_gate_bwd'], 'gla_out_norm': out['gla_out_norm'], 'w_out_odd': out['w_out_odd'], 'ffn_norm': out['ffn_norm'], 'ffn_w_up': out['ffn_w_up'], 'ffn_conv_w': out['ffn_conv_w'], 'ffn_conv_b': out['ffn_conv_b'], 'ffn_w_down': out['ffn_w_down'], 'loss_target': out['loss_target'], 'm_mix_norm_even': out['m_mix_norm_even'], 'm_w_in_even': out['m_w_in_even'], 'm_mla_q_norm': out['m_mla_q_norm'], 'm_mla_kv_norm': out['m_mla_kv_norm'], 'm_mla_w_uq': out['m_mla_w_uq'], 'm_mla_w_ukv': out['m_mla_w_ukv'], 'm_mla_q_head_norm': out['m_mla_q_head_norm'], 'm_mla_k_head_norm': out['m_mla_k_head_norm'], 'm_ret_theta_fwd': out['m_ret_theta_fwd'], 'm_ret_theta_bwd': out['m_ret_theta_bwd'], 'm_ret_out_norm': out['m_ret_out_norm'], 'm_w_out_even': out['m_w_out_even'], 'm_mix_norm_odd': out['m_mix_norm_odd'], 'm_w_in_odd': out['m_w_in_odd'], 'm_gla_w_gate_fwd': out['m_gla_w_gate_fwd'], 'm_gla_b_gate_fwd': out['m_gla_b_gate_fwd'], 'm_gla_w_gate_bwd': out['m_gla_w_gate_bwd'], 'm_gla_b_gate_bwd': out['m_gla_b_gate_bwd'], 'm_gla_out_norm': out['m_gla_out_norm'], 'm_w_out_odd': out['m_w_out_odd'], 'm_ffn_norm': out['m_ffn_norm'], 'm_ffn_w_up': out['m_ffn_w_up'], 'm_ffn_conv_w': out['m_ffn_conv_w'], 'm_ffn_conv_b': out['m_ffn_conv_b'], 'm_ffn_w_down': out['m_ffn_w_down'], 'v_mix_norm_even': out['v_mix_norm_even'], 'v_w_in_even': out['v_w_in_even'], 'v_mla_q_norm': out['v_mla_q_norm'], 'v_mla_kv_norm': out['v_mla_kv_norm'], 'v_mla_w_uq': out['v_mla_w_uq'], 'v_mla_w_ukv': out['v_mla_w_ukv'], 'v_mla_q_head_norm': out['v_mla_q_head_norm'], 'v_mla_k_head_norm': out['v_mla_k_head_norm'], 'v_ret_theta_fwd': out['v_ret_theta_fwd'], 'v_ret_theta_bwd': out['v_ret_theta_bwd'], 'v_ret_out_norm': out['v_ret_out_norm'], 'v_w_out_even': out['v_w_out_even'], 'v_mix_norm_odd': out['v_mix_norm_odd'], 'v_w_in_odd': out['v_w_in_odd'], 'v_gla_w_gate_fwd': out['v_gla_w_gate_fwd'], 'v_gla_b_gate_fwd': out['v_gla_b_gate_fwd'], 'v_gla_w_gate_bwd': out['v_gla_w_gate_bwd'], 'v_gla_b_gate_bwd': out['v_gla_b_gate_bwd'], 'v_gla_out_norm': out['v_gla_out_norm'], 'v_w_out_odd': out['v_w_out_odd'], 'v_ffn_norm': out['v_ffn_norm'], 'v_ffn_w_up': out['v_ffn_w_up'], 'v_ffn_conv_w': out['v_ffn_conv_w'], 'v_ffn_conv_b': out['v_ffn_conv_b'], 'v_ffn_w_down': out['v_ffn_w_down']}


def _loss(weights, diff, rest, loss_target):
    with _jax.named_scope("forward"):
        args = {**rest, TWIN_DIFF_INPUT: diff, **{k: w.astype(_WEIGHT_DTYPES[k]) for k, w in weights.items()}}
        y = _forward(args)
    with _jax.named_scope("loss_head"):
        err = _jnp.square(y.astype(_jnp.float32) - loss_target)
        return 0.5 * _jnp.sum(_jnp.mean(err, axis=-1)) if err.ndim else 0.5 * err


def _adamw(w, g, m, v):
    m = ADAM_B1 * m + (1.0 - ADAM_B1) * g
    v = ADAM_B2 * v + (1.0 - ADAM_B2) * _jnp.square(g)
    m_hat = m / (1.0 - ADAM_B1 ** ADAM_STEP)
    v_hat = v / (1.0 - ADAM_B2 ** ADAM_STEP)
    delta = -ADAM_LR * (m_hat / (_jnp.sqrt(v_hat) + ADAM_EPS) + ADAM_WD * w)
    return delta, m, v


def reference(x, positions, mix_norm_even, w_in_even, mla_q_norm, mla_kv_norm, mla_w_uq, mla_w_ukv, mla_q_head_norm, mla_k_head_norm, ret_theta_fwd, ret_theta_bwd, ret_out_norm, w_out_even, mix_norm_odd, w_in_odd, gla_w_gate_fwd, gla_b_gate_fwd, gla_w_gate_bwd, gla_b_gate_bwd, gla_out_norm, w_out_odd, ffn_norm, ffn_w_up, ffn_conv_w, ffn_conv_b, ffn_w_down, loss_target, m_mix_norm_even, m_w_in_even, m_mla_q_norm, m_mla_kv_norm, m_mla_w_uq, m_mla_w_ukv, m_mla_q_head_norm, m_mla_k_head_norm, m_ret_theta_fwd, m_ret_theta_bwd, m_ret_out_norm, m_w_out_even, m_mix_norm_odd, m_w_in_odd, m_gla_w_gate_fwd, m_gla_b_gate_fwd, m_gla_w_gate_bwd, m_gla_b_gate_bwd, m_gla_out_norm, m_w_out_odd, m_ffn_norm, m_ffn_w_up, m_ffn_conv_w, m_ffn_conv_b, m_ffn_w_down, v_mix_norm_even, v_w_in_even, v_mla_q_norm, v_mla_kv_norm, v_mla_w_uq, v_mla_w_ukv, v_mla_q_head_norm, v_mla_k_head_norm, v_ret_theta_fwd, v_ret_theta_bwd, v_ret_out_norm, v_w_out_even, v_mix_norm_odd, v_w_in_odd, v_gla_w_gate_fwd, v_gla_b_gate_fwd, v_gla_w_gate_bwd, v_gla_b_gate_bwd, v_gla_out_norm, v_w_out_odd, v_ffn_norm, v_ffn_w_up, v_ffn_conv_w, v_ffn_conv_b, v_ffn_w_down):
    given = dict(x=x, positions=positions, mix_norm_even=mix_norm_even, w_in_even=w_in_even, mla_q_norm=mla_q_norm, mla_kv_norm=mla_kv_norm, mla_w_uq=mla_w_uq, mla_w_ukv=mla_w_ukv, mla_q_head_norm=mla_q_head_norm, mla_k_head_norm=mla_k_head_norm, ret_theta_fwd=ret_theta_fwd, ret_theta_bwd=ret_theta_bwd, ret_out_norm=ret_out_norm, w_out_even=w_out_even, mix_norm_odd=mix_norm_odd, w_in_odd=w_in_odd, gla_w_gate_fwd=gla_w_gate_fwd, gla_b_gate_fwd=gla_b_gate_fwd, gla_w_gate_bwd=gla_w_gate_bwd, gla_b_gate_bwd=gla_b_gate_bwd, gla_out_norm=gla_out_norm, w_out_odd=w_out_odd, ffn_norm=ffn_norm, ffn_w_up=ffn_w_up, ffn_conv_w=ffn_conv_w, ffn_conv_b=ffn_conv_b, ffn_w_down=ffn_w_down, loss_target=loss_target, m_mix_norm_even=m_mix_norm_even, m_w_in_even=m_w_in_even, m_mla_q_norm=m_mla_q_norm, m_mla_kv_norm=m_mla_kv_norm, m_mla_w_uq=m_mla_w_uq, m_mla_w_ukv=m_mla_w_ukv, m_mla_q_head_norm=m_mla_q_head_norm, m_mla_k_head_norm=m_mla_k_head_norm, m_ret_theta_fwd=m_ret_theta_fwd, m_ret_theta_bwd=m_ret_theta_bwd, m_ret_out_norm=m_ret_out_norm, m_w_out_even=m_w_out_even, m_mix_norm_odd=m_mix_norm_odd, m_w_in_odd=m_w_in_odd, m_gla_w_gate_fwd=m_gla_w_gate_fwd, m_gla_b_gate_fwd=m_gla_b_gate_fwd, m_gla_w_gate_bwd=m_gla_w_gate_bwd, m_gla_b_gate_bwd=m_gla_b_gate_bwd, m_gla_out_norm=m_gla_out_norm, m_w_out_odd=m_w_out_odd, m_ffn_norm=m_ffn_norm, m_ffn_w_up=m_ffn_w_up, m_ffn_conv_w=m_ffn_conv_w, m_ffn_conv_b=m_ffn_conv_b, m_ffn_w_down=m_ffn_w_down, v_mix_norm_even=v_mix_norm_even, v_w_in_even=v_w_in_even, v_mla_q_norm=v_mla_q_norm, v_mla_kv_norm=v_mla_kv_norm, v_mla_w_uq=v_mla_w_uq, v_mla_w_ukv=v_mla_w_ukv, v_mla_q_head_norm=v_mla_q_head_norm, v_mla_k_head_norm=v_mla_k_head_norm, v_ret_theta_fwd=v_ret_theta_fwd, v_ret_theta_bwd=v_ret_theta_bwd, v_ret_out_norm=v_ret_out_norm, v_w_out_even=v_w_out_even, v_mix_norm_odd=v_mix_norm_odd, v_w_in_odd=v_w_in_odd, v_gla_w_gate_fwd=v_gla_w_gate_fwd, v_gla_b_gate_fwd=v_gla_b_gate_fwd, v_gla_w_gate_bwd=v_gla_w_gate_bwd, v_gla_b_gate_bwd=v_gla_b_gate_bwd, v_gla_out_norm=v_gla_out_norm, v_w_out_odd=v_w_out_odd, v_ffn_norm=v_ffn_norm, v_ffn_w_up=v_ffn_w_up, v_ffn_conv_w=v_ffn_conv_w, v_ffn_conv_b=v_ffn_conv_b, v_ffn_w_down=v_ffn_w_down)
    weights = {n: given[n] for n in TWIN_WEIGHTS}
    shared = {n: given[n] for n in SHARED_INPUTS}
    per_example = {n: given[n] for n in ['x', 'positions']}
    grad_fn = _jax.value_and_grad(_loss, argnums=(0, 1))

    def one_microbatch(ex, loss_target):
        ex = dict(ex)
        diff = ex.pop(TWIN_DIFF_INPUT)
        return grad_fn(weights, diff, {**shared, **ex}, loss_target)

    if N_MICROBATCH == 1:
        loss, (grad_w, grad_x) = one_microbatch(per_example, given["loss_target"])
    else:
        def body(carry, xs):
            loss_sum, grad_sum = carry
            l_k, (gw_k, gx_k) = one_microbatch(xs[0], xs[1])
            with _jax.named_scope("update"):
                return (loss_sum + l_k, _jax.tree.map(_jnp.add, grad_sum, gw_k)), gx_k

        init = (_jnp.zeros((), _jnp.float32), _jax.tree.map(_jnp.zeros_like, weights))
        (loss, grad_w), grad_x = _jax.lax.scan(body, init, (per_example, given["loss_target"]))
    with _jax.named_scope("update"):
        delta_w, new_m, new_v = {}, {}, {}
        for n in TWIN_WEIGHTS:
            delta_w[n], new_m[n], new_v[n] = _adamw(weights[n], grad_w[n], given["m_" + n], given["v_" + n])
    return (loss, grad_x, *[grad_w[n] for n in TWIN_WEIGHTS], *[delta_w[n] for n in TWIN_WEIGHTS],
            *[new_m[n] for n in TWIN_WEIGHTS], *[new_v[n] for n in TWIN_WEIGHTS])
```

```python
import math

import jax
import jax.numpy as jnp
from jax import lax
from jax.experimental import pallas as pl
from jax.experimental.pallas import tpu as pltpu

F32 = jnp.float32
BF16 = jnp.bfloat16

D_MODEL = 1024
DEPTH = 4
N_DEV = 8
MESH_AXES = ("x", "y", "c")

MLA_HEADS = 8
MLA_Q_RANK = 384
MLA_KV_RANK = 256
MLA_NOPE = 64
MLA_ROPE = 32
MLA_V = 64
MLA_QK = MLA_NOPE + MLA_ROPE
RET_HEADS = 8
RET_DK = 64
RET_DV = 64
RET_CHUNK = 128
GLA_HEADS = 4
GLA_DK = 128
GLA_DV = 256
GLA_GATE_RANK = 16
GLA_TAU = 16.0
GLA_CHUNK = 64
D_FF = 2816
ROPE_THETA = 10000.0
EPS = 1e-6

ADAM_LR = 0.001
ADAM_B1 = 0.9
ADAM_B2 = 0.999
ADAM_EPS = 1e-08
ADAM_WD = 0.01
ADAM_STEP = 10

LANE = 128
SUBLANE = 8
ROW_TILE = 512
VMEM_LIMIT = 56 * 1024 * 1024
WEIGHT_TILE_BYTES = 6 * 1024 * 1024

EV_CQ, EV_CKV, EV_KR, EV_RQ, EV_RK, EV_RV, EV_RG, EV_IN = 0, 384, 640, 768, 1280, 1792, 2304, 2816
OD_Q, OD_K, OD_V, OD_R, OD_GA, OD_IN = 0, 512, 1024, 2048, 3072, 3200
N_GROUPS = 4


def _cparams(sem):
    return pltpu.CompilerParams(dimension_semantics=sem, vmem_limit_bytes=VMEM_LIMIT)


def _dot(a, b):
    return jnp.dot(a.astype(BF16), b.astype(BF16), preferred_element_type=F32)


def _dot_nt(a, b):
    return lax.dot_general(a.astype(BF16), b.astype(BF16), (((1,), (1,)), ((), ())), preferred_element_type=F32)


def _dot_tn(a, b):
    return lax.dot_general(a.astype(BF16), b.astype(BF16), (((0,), (0,)), ((), ())), preferred_element_type=F32)


def _dot_f32(a, b):
    return jnp.dot(a, b, preferred_element_type=F32, precision=lax.Precision.HIGHEST)


def _dot_tn_f32(a, b):
    return lax.dot_general(a, b, (((0,), (0,)), ((), ())), preferred_element_type=F32,
                           precision=lax.Precision.HIGHEST)


def _sigmoid(x):
    return 1.0 / (1.0 + jnp.exp(-x))


def _col_tile(k, n, itemsize=2):
    best = LANE
    for t in range(LANE, n + 1, LANE):
        if n % t == 0 and k * t * itemsize <= WEIGHT_TILE_BYTES:
            best = t
    return best if n % LANE == 0 else n


def _row_tile(m):
    return min(ROW_TILE, m)


def mm_nn(a, b, res=None, out_dtype=F32, name="mm_nn"):
    m, k = a.shape
    n = b.shape[1]
    tm, tn = _row_tile(m), _col_tile(k, n)

    def body(*refs):
        if res is None:
            a_ref, b_ref, o_ref = refs
        else:
            a_ref, b_ref, r_ref, o_ref = refs
        acc = _dot(a_ref[...], b_ref[...])
        if res is not None:
            acc = acc + r_ref[...].astype(F32)
        o_ref[...] = acc.astype(out_dtype)

    in_specs = [pl.BlockSpec((tm, k), lambda i, j: (i, 0)), pl.BlockSpec((k, tn), lambda i, j: (0, j))]
    args = [a, b]
    if res is not None:
        in_specs.append(pl.BlockSpec((tm, tn), lambda i, j: (i, j)))
        args.append(res)
    return pl.pallas_call(
        body, name=name, grid=(m // tm, n // tn), in_specs=in_specs,
        out_specs=pl.BlockSpec((tm, tn), lambda i, j: (i, j)),
        out_shape=jax.ShapeDtypeStruct((m, n), out_dtype),
        compiler_params=_cparams(("parallel", "parallel")),
    )(*args)


def mm_tn(a, b, name="mm_tn"):
    t, k = a.shape
    n = b.shape[1]
    tt = _row_tile(t)
    tk = k if k <= 1024 else _col_tile(1024, k, 4)
    tn = n if n <= 1024 else _col_tile(1024, n, 4)

    def body(a_ref, b_ref, o_ref):
        @pl.when(pl.program_id(2) == 0)
        def _():
            o_ref[...] = jnp.zeros_like(o_ref)
        o_ref[...] += _dot_tn(a_ref[...], b_ref[...])

    return pl.pallas_call(
        body, name=name, grid=(k // tk, n // tn, t // tt),
        in_specs=[pl.BlockSpec((tt, tk), lambda i, j, s: (s, i)), pl.BlockSpec((tt, tn), lambda i, j, s: (s, j))],
        out_specs=pl.BlockSpec((tk, tn), lambda i, j, s: (i, j)),
        out_shape=jax.ShapeDtypeStruct((k, n), F32),
        compiler_params=_cparams(("parallel", "parallel", "arbitrary")),
    )(a, b)


def rmsnorm_fwd(x, g, name="rmsnorm_fwd"):
    t, d = x.shape
    tm = _row_tile(t)

    def body(x_ref, g_ref, h_ref):
        xv = x_ref[...]
        r = lax.rsqrt(jnp.mean(xv * xv, axis=-1, keepdims=True) + EPS)
        h_ref[...] = (xv * r * g_ref[...]).astype(BF16)

    return pl.pallas_call(
        body, name=name, grid=(t // tm,),
        in_specs=[pl.BlockSpec((tm, d), lambda i: (i, 0)), pl.BlockSpec((1, d), lambda i: (0, 0))],
        out_specs=pl.BlockSpec((tm, d), lambda i: (i, 0)),
        out_shape=jax.ShapeDtypeStruct((t, d), BF16),
        compiler_params=_cparams(("parallel",)),
    )(x, g.reshape(1, d))


def rmsnorm_bwd(x, g, dh, dres, name="rmsnorm_bwd"):
    t, d = x.shape
    tm = _row_tile(t)

    def body(x_ref, g_ref, dh_ref, dres_ref, dx_ref, dg_ref):
        @pl.when(pl.program_id(0) == 0)
        def _():
            dg_ref[...] = jnp.zeros_like(dg_ref)
        xv = x_ref[...]
        r = lax.rsqrt(jnp.mean(xv * xv, axis=-1, keepdims=True) + EPS)
        xh = xv * r
        dhv = dh_ref[...]
        dg_ref[...] += jnp.sum(dhv * xh, axis=0, keepdims=True)
        dxh = dhv * g_ref[...]
        dx_ref[...] = dres_ref[...] + r * (dxh - xh * jnp.mean(dxh * xh, axis=-1, keepdims=True))

    row = pl.BlockSpec((tm, d), lambda i: (i, 0))
    vec = pl.BlockSpec((1, d), lambda i: (0, 0))
    return pl.pallas_call(
        body, name=name, grid=(t // tm,),
        in_specs=[row, vec, row, row], out_specs=[row, vec],
        out_shape=[jax.ShapeDtypeStruct((t, d), F32), jax.ShapeDtypeStruct((1, d), F32)],
        compiler_params=_cparams(("arbitrary",)),
    )(x, g.reshape(1, d), dh, dres)


def _rope_apply(x, cos, s1, s2, half):
    return x * cos + pltpu.roll(x, LANE - half, 1) * s1 + pltpu.roll(x, half, 1) * s2


def _rope_transpose(dy, cos, s1, s2, half):
    return dy * cos + pltpu.roll(dy * s1, half, 1) + pltpu.roll(dy * s2, LANE - half, 1)


def rope_tables(positions, lane_start, half, period):
    pos = positions.reshape(-1).astype(F32)
    inv = ROPE_THETA ** (-jnp.arange(half, dtype=F32) / half)
    ang = pos[:, None] * inv[None, :]
    cos, sin = jnp.cos(ang), jnp.sin(ang)
    t = pos.shape[0]
    pre = lane_start
    post = period - lane_start - 2 * half
    ones = lambda n: jnp.ones((t, n), F32)
    zeros = lambda n: jnp.zeros((t, n), F32)
    c = jnp.concatenate([ones(pre), cos, cos, ones(post)], axis=1)
    a = jnp.concatenate([zeros(pre), -sin, zeros(half), zeros(post)], axis=1)
    b = jnp.concatenate([zeros(pre), zeros(half), sin, zeros(post)], axis=1)
    rep = LANE // period
    return tuple(jnp.tile(v, (1, rep)) for v in (c, a, b))


def _mla_forward_tile(p, cos, s1, s2, qn_g, kvn_g, wuq, wk, wv, qhn, khn):
    cq = p[:, EV_CQ:EV_CKV]
    ckv = p[:, EV_CKV:EV_KR]
    kr = p[:, EV_KR:EV_RQ]
    rq = lax.rsqrt(jnp.mean(cq * cq, axis=-1, keepdims=True) + EPS)
    rkv = lax.rsqrt(jnp.mean(ckv * ckv, axis=-1, keepdims=True) + EPS)
    qn = cq * rq * qn_g
    kvn = ckv * rkv * kvn_g
    q_raw = _dot(qn, wuq)
    k_raw = _dot(kvn, wk)
    v = _dot(kvn, wv)
    krp = pltpu.roll(kr, MLA_NOPE, 1)
    return cq, ckv, rq, rkv, qn, kvn, q_raw, k_raw, v, krp


def _head_norm(xh, g):
    r = lax.rsqrt(jnp.sum(xh * xh, axis=-1, keepdims=True) * (1.0 / MLA_QK) + EPS)
    return xh * r * g, r


def mla_prep_fwd(p, tabs, qn_g, kvn_g, wuq, wk, wv, qhn, khn, name="mla_prep_fwd"):
    t = p.shape[0]
    tm = _row_tile(t)
    hw = MLA_HEADS * LANE

    def body(p_ref, c_ref, s1_ref, s2_ref, qn_ref, kvn_ref, wuq_ref, wk_ref, wv_ref, qhn_ref, khn_ref,
             q_out, k_out, v_out):
        cos, s1, s2 = c_ref[...], s1_ref[...], s2_ref[...]
        (_, _, _, _, _, _, q_raw, k_raw, v, krp) = _mla_forward_tile(
            p_ref[...], cos, s1, s2, qn_ref[...], kvn_ref[...], wuq_ref[...], wk_ref[...], wv_ref[...],
            qhn_ref[...], khn_ref[...])
        v_out[...] = v.astype(BF16)
        for h in range(MLA_HEADS):
            sl = slice(h * LANE, (h + 1) * LANE)
            qh, _ = _head_norm(q_raw[:, sl], qhn_ref[...])
            kh, _ = _head_norm(k_raw[:, sl] + krp, khn_ref[...])
            q_out[:, sl] = _rope_apply(qh, cos, s1, s2, MLA_ROPE // 2).astype(BF16)
            k_out[:, sl] = _rope_apply(kh, cos, s1, s2, MLA_ROPE // 2).astype(BF16)

    row = lambda w: pl.BlockSpec((tm, w), lambda i: (i, 0))
    full = lambda a: pl.BlockSpec(a.shape, lambda i: (0,) * a.ndim)
    ws = [qn_g, kvn_g, wuq, wk, wv, qhn, khn]
    return pl.pallas_call(
        body, name=name, grid=(t // tm,),
        in_specs=[row(EV_RQ), row(LANE), row(LANE), row(LANE)] + [full(w) for w in ws],
        out_specs=[row(hw)] * 3,
        out_shape=[jax.ShapeDtypeStruct((t, hw), BF16)] * 3,
        compiler_params=_cparams(("parallel",)),
    )(p, *tabs, *ws)


def mla_prep_bwd(p, tabs, qn_g, kvn_g, wuq, wk, wv, wuq_t, wk_t, wv_t, qhn, khn, dq, dk, dv,
                 name="mla_prep_bwd"):
    t = p.shape[0]
    tm = _row_tile(t)
    hw = MLA_HEADS * LANE

    def body(p_ref, c_ref, s1_ref, s2_ref, qn_ref, kvn_ref, wuq_ref, wk_ref, wv_ref, wuqt_ref, wkt_ref, wvt_ref,
             qhn_ref, khn_ref, dq_ref, dk_ref, dv_ref,
             dp_ref, dwuq_ref, dwk_ref, dwv_ref, dqn_ref, dkvn_ref, dqhn_ref, dkhn_ref, dqraw_s, dkraw_s):
        @pl.when(pl.program_id(0) == 0)
        def _():
            for r in (dwuq_ref, dwk_ref, dwv_ref, dqn_ref, dkvn_ref, dqhn_ref, dkhn_ref):
                r[...] = jnp.zeros_like(r)
        cos, s1, s2 = c_ref[...], s1_ref[...], s2_ref[...]
        qhn_v, khn_v = qhn_ref[...], khn_ref[...]
        (cq, ckv, rq, rkv, qn, kvn, q_raw, k_raw, _, krp) = _mla_forward_tile(
            p_ref[...], cos, s1, s2, qn_ref[...], kvn_ref[...], wuq_ref[...], wk_ref[...], wv_ref[...],
            qhn_v, khn_v)
        half = MLA_ROPE // 2
        dkr_sum = jnp.zeros((tm, LANE), F32)
        dqhn_acc = jnp.zeros((1, LANE), F32)
        dkhn_acc = jnp.zeros((1, LANE), F32)
        for h in range(MLA_HEADS):
            sl = slice(h * LANE, (h + 1) * LANE)
            xq = q_raw[:, sl]
            _, r = _head_norm(xq, qhn_v)
            xh = xq * r
            dy = _rope_transpose(dq_ref[:, sl], cos, s1, s2, half)
            dqhn_acc = dqhn_acc + jnp.sum(dy * xh, axis=0, keepdims=True)
            dxh = dy * qhn_v
            dqraw_s[:, sl] = r * (dxh - xh * (jnp.sum(dxh * xh, axis=-1, keepdims=True) * (1.0 / MLA_QK)))
            xk = k_raw[:, sl] + krp
            _, r = _head_norm(xk, khn_v)
            xh = xk * r
            dy = _rope_transpose(dk_ref[:, sl], cos, s1, s2, half)
            dkhn_acc = dkhn_acc + jnp.sum(dy * xh, axis=0, keepdims=True)
            dxh = dy * khn_v
            dxk = r * (dxh - xh * (jnp.sum(dxh * xh, axis=-1, keepdims=True) * (1.0 / MLA_QK)))
            dkraw_s[:, sl] = dxk
            dkr_sum = dkr_sum + dxk
        dqhn_ref[...] += dqhn_acc
        dkhn_ref[...] += dkhn_acc
        dq_raw = dqraw_s[...]
        dk_raw = dkraw_s[...]
        dvv = dv_ref[...]
        dwuq_ref[...] += _dot_tn(qn, dq_raw)
        dwk_ref[...] += _dot_tn(kvn, dk_raw)
        dwv_ref[...] += _dot_tn(kvn, dvv)
        dqn = _dot(dq_raw, wuqt_ref[...])
        dkvn = _dot(dk_raw, wkt_ref[...]) + _dot(dvv, wvt_ref[...])
        xh = cq * rq
        dqn_ref[...] += jnp.sum(dqn * xh, axis=0, keepdims=True)
        dxh = dqn * qn_ref[...]
        dp_ref[:, EV_CQ:EV_CKV] = rq * (dxh - xh * jnp.mean(dxh * xh, axis=-1, keepdims=True))
        xh = ckv * rkv
        dkvn_ref[...] += jnp.sum(dkvn * xh, axis=0, keepdims=True)
        dxh = dkvn * kvn_ref[...]
        dp_ref[:, EV_CKV:EV_KR] = rkv * (dxh - xh * jnp.mean(dxh * xh, axis=-1, keepdims=True))
        lane = lax.broadcasted_iota(jnp.int32, (tm, LANE), 1)
        dp_ref[:, EV_KR:EV_RQ] = jnp.where(lane < MLA_ROPE, pltpu.roll(dkr_sum, LANE - MLA_NOPE, 1), 0.0)

    row = lambda w: pl.BlockSpec((tm, w), lambda i: (i, 0))
    full = lambda a: pl.BlockSpec(a.shape, lambda i: (0,) * a.ndim)
    ws = [qn_g, kvn_g, wuq, wk, wv, wuq_t, wk_t, wv_t, qhn, khn]
    outs = [jax.ShapeDtypeStruct((t, EV_RQ), F32), jax.ShapeDtypeStruct(wuq.shape, F32),
            jax.ShapeDtypeStruct(wk.shape, F32), jax.ShapeDtypeStruct(wv.shape, F32),
            jax.ShapeDtypeStruct(qn_g.shape, F32), jax.ShapeDtypeStruct(kvn_g.shape, F32),
            jax.ShapeDtypeStruct(qhn.shape, F32), jax.ShapeDtypeStruct(khn.shape, F32)]
    return pl.pallas_call(
        body, name=name, grid=(t // tm,),
        in_specs=[row(EV_RQ), row(LANE), row(LANE), row(LANE)] + [full(w) for w in ws] + [row(hw)] * 3,
        out_specs=[row(EV_RQ)] + [full(o) for o in outs[1:]],
        out_shape=outs,
        scratch_shapes=[pltpu.VMEM((tm, hw), F32), pltpu.VMEM((tm, hw), F32)],
        compiler_params=_cparams(("arbitrary",)),
    )(p, *tabs, *ws, dq, dk, dv)


ATTN_SCALE = MLA_QK ** -0.5


def attn_fwd(q, k, v, name="attn_fwd"):
    t = q.shape[0]
    tq = tk = _row_tile(t)
    nh = MLA_HEADS

    def body(q_ref, k_ref, v_ref, o_ref, lse_ref, m_s, l_s, acc_s):
        j = pl.program_id(2)

        @pl.when(j == 0)
        def _():
            m_s[...] = jnp.full_like(m_s, -jnp.inf)
            l_s[...] = jnp.zeros_like(l_s)
            acc_s[...] = jnp.zeros_like(acc_s)

        s = _dot_nt(q_ref[...], k_ref[...]) * ATTN_SCALE
        m_old = m_s[...]
        m_new = jnp.maximum(m_old, jnp.max(s, axis=-1, keepdims=True))
        pr = jnp.exp(s - m_new)
        alpha = jnp.exp(m_old - m_new)
        l_s[...] = alpha * l_s[...] + jnp.sum(pr, axis=-1, keepdims=True)
        acc_s[...] = alpha * acc_s[...] + _dot(pr, v_ref[...])
        m_s[...] = m_new

        @pl.when(j == pl.num_programs(2) - 1)
        def _():
            o_ref[...] = acc_s[...] / l_s[...]
            lse_ref[...] = m_s[...] + jnp.log(l_s[...])

    return pl.pallas_call(
        body, name=name, grid=(nh, t // tq, t // tk),
        in_specs=[pl.BlockSpec((tq, LANE), lambda h, i, j: (i, h)),
                  pl.BlockSpec((tk, LANE), lambda h, i, j: (j, h)),
                  pl.BlockSpec((tk, LANE), lambda h, i, j: (j, h))],
        out_specs=[pl.BlockSpec((tq, LANE), lambda h, i, j: (i, h)),
                   pl.BlockSpec((None, tq, 1), lambda h, i, j: (h, i, 0))],
        out_shape=[jax.ShapeDtypeStruct((t, nh * LANE), F32), jax.ShapeDtypeStruct((nh, t, 1), F32)],
        scratch_shapes=[pltpu.VMEM((tq, 1), F32), pltpu.VMEM((tq, 1), F32), pltpu.VMEM((tq, LANE), F32)],
        compiler_params=_cparams(("parallel", "parallel", "arbitrary")),
    )(q, k, v)


def attn_bwd(q, k, v, o, lse, do, name="attn_bwd"):
    t = q.shape[0]
    tq = tk = _row_tile(t)
    nh = MLA_HEADS
    nq = t // tq

    def body(q_ref, k_ref, v_ref, o_ref, lse_ref, do_ref, dq_ref, dk_ref, dv_ref):
        kj, qi = pl.program_id(1), pl.program_id(2)

        @pl.when(qi == 0)
        def _():
            dk_ref[...] = jnp.zeros_like(dk_ref)
            dv_ref[...] = jnp.zeros_like(dv_ref)

        qv, kv, vv, dov = q_ref[...], k_ref[...], v_ref[...], do_ref[...]
        s = _dot_nt(qv, kv) * ATTN_SCALE
        pr = jnp.exp(s - lse_ref[...])
        dp = _dot_nt(dov, vv)
        delta = jnp.sum(dov * o_ref[...], axis=-1, keepdims=True)
        ds = pr * (dp - delta) * ATTN_SCALE
        dv_ref[...] += _dot_tn(pr, dov)
        dk_ref[...] += _dot_tn(ds, qv)
        dq_tile = _dot(ds, kv)
        rows = pl.ds(pl.multiple_of(qi * tq, tq), tq)

        @pl.when(kj == 0)
        def _():
            dq_ref[rows, :] = dq_tile

        @pl.when(kj != 0)
        def _():
            dq_ref[rows, :] += dq_tile

    qspec = pl.BlockSpec((tq, LANE), lambda h, j, i: (i, h))
    kspec = pl.BlockSpec((tk, LANE), lambda h, j, i: (j, h))
    return pl.pallas_call(
        body, name=name, grid=(nh, t // tk, nq),
        in_specs=[qspec, kspec, kspec, qspec, pl.BlockSpec((None, tq, 1), lambda h, j, i: (h, i, 0)), qspec],
        out_specs=[pl.BlockSpec((t, LANE), lambda h, j, i: (0, h)), kspec, kspec],
        out_shape=[jax.ShapeDtypeStruct((t, nh * LANE), F32)] * 3,
        compiler_params=_cparams(("parallel", "arbitrary", "arbitrary")),
    )(q, k, v, o, lse, do)


def _scan_consts(c, reverse, inclusive):
    ii = lax.broadcasted_iota(jnp.int32, (c, c), 0)
    jj = lax.broadcasted_iota(jnp.int32, (c, c), 1)
    if reverse:
        incl = jj >= ii
        mask = incl if inclusive else jj > ii
    else:
        incl = jj <= ii
        mask = incl if inclusive else jj < ii
    mid = (c - 1 - c // 2) if reverse else c // 2
    return incl.astype(F32), mask.astype(F32), mid


def _sub_masks(sub, dvg, u):
    if sub == 1:
        return None, None
    kl = lax.broadcasted_iota(jnp.int32, (1, LANE), 1)
    vl = lax.broadcasted_iota(jnp.int32, (1, dvg), 1)
    kw, vw = LANE // sub, dvg // sub
    km = (kl >= u * kw) & (kl < (u + 1) * kw)
    vm = (vl >= u * vw) & (vl < (u + 1) * vw)
    return km.astype(F32), vm.astype(F32)


def _scan_chunk_fwd(q, k, la, incl, mid):
    b = _dot_f32(incl, la)
    row = lax.broadcasted_iota(jnp.int32, b.shape, 0)
    bm = jnp.sum(jnp.where(row == mid, b, 0.0), axis=0, keepdims=True)
    tot = jnp.sum(la, axis=0, keepdims=True)
    e_qc = jnp.exp(b - bm)
    e_kc = jnp.exp(bm - b)
    e_qe = jnp.exp(b)
    e_kd = jnp.exp(tot - b)
    return e_qc, e_kc, e_qe, e_kd


def scan_fwd(q_arr, k_arr, v_arr, la_arr, *, qcb, kcb, vcb, lacb, la_row, chunk, dvg, sub, reverse, inclusive,
             qscale, kscale, rope=None, name="scan_fwd"):
    t = q_arr.shape[0]
    r = _row_tile(t)
    nb, nc = t // r, r // chunk
    c = chunk
    rb = (lambda j: nb - 1 - j) if reverse else (lambda j: j)
    order = list(range(nc))[::-1] if reverse else list(range(nc))
    half = RET_DK // 2

    def body(*refs):
        if rope is None:
            q_ref, k_ref, v_ref, la_ref, o_ref, st_ref, s_s = refs
        else:
            q_ref, k_ref, v_ref, la_ref, c_ref, s1_ref, s2_ref, o_ref, st_ref, s_s = refs

        @pl.when(pl.program_id(1) == 0)
        def _():
            s_s[...] = jnp.zeros_like(s_s)

        incl, mask, mid = _scan_consts(c, reverse, inclusive)
        ones_cv = jnp.ones((c, dvg), F32)
        for ci in order:
            rows = slice(ci * c, (ci + 1) * c)
            qv = q_ref[rows, :] * qscale
            kv = k_ref[rows, :] * kscale
            if rope is not None:
                cs, a1, a2 = c_ref[rows, :], s1_ref[rows, :], s2_ref[rows, :]
                qv = _rope_apply(qv, cs, a1, a2, half)
                kv = _rope_apply(kv, cs, a1, a2, half)
            la = jnp.broadcast_to(la_ref[...], (c, LANE)) if la_row else la_ref[rows, :]
            vv = v_ref[rows, :]
            e_qc, e_kc, e_qe, e_kd = _scan_chunk_fwd(qv, kv, la, incl, mid)
            qc, kc, qe, kd = qv * e_qc, kv * e_kc, qv * e_qe, kv * e_kd
            sg = s_s[...]
            st_ref[ci] = sg
            acc = None
            for u in range(sub):
                mu, vmu = _sub_masks(sub, dvg, u)
                qcu = qc if mu is None else qc * mu
                qeu = qe if mu is None else qe * mu
                a = _dot_nt(qcu, kc) * mask
                ou = _dot(a, vv) + _dot(qeu, sg)
                ou = ou if vmu is None else ou * vmu
                acc = ou if acc is None else acc + ou
            o_ref[rows, :] = acc
            decay = jnp.exp(_dot_tn_f32(la, ones_cv))
            s_s[...] = decay * sg + _dot_tn(kd, vv)

    specs = [pl.BlockSpec((r, LANE), lambda g, j: (rb(j), qcb + g)),
             pl.BlockSpec((r, LANE), lambda g, j: (rb(j), kcb + g)),
             pl.BlockSpec((r, dvg), lambda g, j: (rb(j), vcb + g)),
             pl.BlockSpec((1, LANE), lambda g, j: (0, lacb + g)) if la_row
             else pl.BlockSpec((r, LANE), lambda g, j: (rb(j), lacb + g))]
    args = [q_arr, k_arr, v_arr, la_arr]
    if rope is not None:
        specs += [pl.BlockSpec((r, LANE), lambda g, j: (rb(j), 0))] * 3
        args += list(rope)
    return pl.pallas_call(
        body, name=name, grid=(N_GROUPS, nb), in_specs=specs,
        out_specs=[pl.BlockSpec((r, dvg), lambda g, j: (rb(j), g)),
                   pl.BlockSpec((nc, LANE, dvg), lambda g, j: (rb(j), g, 0))],
        out_shape=[jax.ShapeDtypeStruct((t, N_GROUPS * dvg), F32),
                   jax.ShapeDtypeStruct((t // c, N_GROUPS * LANE, dvg), F32)],
        scratch_shapes=[pltpu.VMEM((LANE, dvg), F32)],
        compiler_params=_cparams(("parallel", "arbitrary")),
    )(*args)


def scan_bwd(q_arr, k_arr, v_arr, la_arr, st_arr, do_arr, prev, *, qcb, kcb, vcb, lacb, la_row, chunk, dvg, sub,
             reverse, inclusive, qscale, kscale, rope=None, name="scan_bwd"):
    t = q_arr.shape[0]
    r = _row_tile(t)
    nb, nc = t // r, r // chunk
    c = chunk
    rb = (lambda j: j) if reverse else (lambda j: nb - 1 - j)
    order = list(range(nc)) if reverse else list(range(nc))[::-1]
    half = RET_DK // 2
    n_in = 6 + (3 if rope is not None else 0) + (3 if prev is not None else 0)

    def body(*refs):
        ins, outs = refs[:n_in], refs[n_in:]
        q_ref, k_ref, v_ref, la_ref, st_ref, do_ref = ins[:6]
        pos = 6
        if rope is not None:
            c_ref, s1_ref, s2_ref = ins[pos:pos + 3]
            pos += 3
        if prev is not None:
            pq_ref, pk_ref, pv_ref = ins[pos:pos + 3]
        dq_ref, dk_ref, dv_ref, dla_ref, g_s = outs

        @pl.when(pl.program_id(1) == 0)
        def _():
            g_s[...] = jnp.zeros_like(g_s)
            if la_row:
                dla_ref[...] = jnp.zeros_like(dla_ref)

        incl, mask, mid = _scan_consts(c, reverse, inclusive)
        ones_cv = jnp.ones((c, dvg), F32)
        for ci in order:
            rows = slice(ci * c, (ci + 1) * c)
            qv = q_ref[rows, :] * qscale
            kv = k_ref[rows, :] * kscale
            if rope is not None:
                cs, a1, a2 = c_ref[rows, :], s1_ref[rows, :], s2_ref[rows, :]
                qv = _rope_apply(qv, cs, a1, a2, half)
                kv = _rope_apply(kv, cs, a1, a2, half)
            la = jnp.broadcast_to(la_ref[...], (c, LANE)) if la_row else la_ref[rows, :]
            vv = v_ref[rows, :]
            dov = do_ref[rows, :]
            e_qc, e_kc, e_qe, e_kd = _scan_chunk_fwd(qv, kv, la, incl, mid)
            qc, kc, qe, kd = qv * e_qc, kv * e_kc, qv * e_qe, kv * e_kd
            sg = st_ref[ci]
            gn = g_s[...]
            dqc = jnp.zeros((c, LANE), F32)
            dkc = jnp.zeros((c, LANE), F32)
            dqe = jnp.zeros((c, LANE), F32)
            dvv = _dot(kd, gn)
            ds_direct = jnp.zeros((LANE, dvg), F32)
            for u in range(sub):
                mu, vmu = _sub_masks(sub, dvg, u)
                qcu = qc if mu is None else qc * mu
                qeu = qe if mu is None else qe * mu
                dou = dov if vmu is None else dov * vmu
                a = _dot_nt(qcu, kc) * mask
                da = _dot_nt(dou, vv) * mask
                dvv = dvv + _dot_tn(a, dou)
                t1 = _dot(da, kc)
                dqc = dqc + (t1 if mu is None else t1 * mu)
                dkc = dkc + _dot_tn(da, qcu)
                t2 = _dot_nt(dou, sg)
                dqe = dqe + (t2 if mu is None else t2 * mu)
                ds_direct = ds_direct + _dot_tn(qeu, dou)
            dkd = _dot_nt(vv, gn)
            decay = jnp.exp(_dot_tn_f32(la, ones_cv))
            gs = gn * sg * decay
            dtot = lax.dot_general(jnp.ones((SUBLANE, dvg), F32), gs, (((1,), (1,)), ((), ())),
                                   preferred_element_type=F32, precision=lax.Precision.HIGHEST)[0:1, :]
            dtot = dtot + jnp.sum(dkd * kd, axis=0, keepdims=True)
            db = dqc * qc - dkc * kc + dqe * qe - dkd * kd
            dla = _dot_tn_f32(incl, db) + dtot
            dqv = dqc * e_qc + dqe * e_qe
            dkv = dkc * e_kc + dkd * e_kd
            if rope is not None:
                dqv = _rope_transpose(dqv, cs, a1, a2, half)
                dkv = _rope_transpose(dkv, cs, a1, a2, half)
            dqv = dqv * qscale
            dkv = dkv * kscale
            if prev is not None:
                dqv = dqv + pq_ref[rows, :]
                dkv = dkv + pk_ref[rows, :]
                dvv = dvv + pv_ref[rows, :]
            dq_ref[rows, :] = dqv
            dk_ref[rows, :] = dkv
            dv_ref[rows, :] = dvv
            if la_row:
                dla_ref[...] += jnp.sum(dla, axis=0, keepdims=True)
            else:
                dla_ref[rows, :] = dla
            g_s[...] = ds_direct + decay * gn

    kblk = lambda cb: pl.BlockSpec((r, LANE), lambda g, j: (rb(j), cb + g))
    vblk = lambda cb: pl.BlockSpec((r, dvg), lambda g, j: (rb(j), cb + g))
    specs = [kblk(qcb), kblk(kcb), vblk(vcb),
             pl.BlockSpec((1, LANE), lambda g, j: (0, lacb + g)) if la_row else kblk(lacb),
             pl.BlockSpec((nc, LANE, dvg), lambda g, j: (rb(j), g, 0)), vblk(0)]
    args = [q_arr, k_arr, v_arr, la_arr, st_arr, do_arr]
    if rope is not None:
        specs += [pl.BlockSpec((r, LANE), lambda g, j: (rb(j), 0))] * 3
        args += list(rope)
    if prev is not None:
        specs += [kblk(0), kblk(0), vblk(0)]
        args += list(prev)
    wk = N_GROUPS * LANE
    outs = [jax.ShapeDtypeStruct((t, wk), F32), jax.ShapeDtypeStruct((t, wk), F32),
            jax.ShapeDtypeStruct((t, N_GROUPS * dvg), F32),
            jax.ShapeDtypeStruct((1, wk) if la_row else (t, wk), F32)]
    return pl.pallas_call(
        body, name=name, grid=(N_GROUPS, nb), in_specs=specs,
        out_specs=[kblk(0), kblk(0), vblk(0),
                   pl.BlockSpec((1, LANE), lambda g, j: (0, g)) if la_row else kblk(0)],
        out_shape=outs,
        scratch_shapes=[pltpu.VMEM((LANE, dvg), F32)],
        compiler_params=_cparams(("parallel", "arbitrary")),
    )(*args)


def _seg_mean(x, seg):
    w = x.shape[1]
    if seg % LANE == 0:
        parts = []
        for s in range(0, w, seg):
            m = jnp.mean(x[:, s:s + seg], axis=-1, keepdims=True)
            parts.append(jnp.broadcast_to(m, (x.shape[0], seg)))
        return jnp.concatenate(parts, axis=1)
    shift = seg.bit_length() - 1
    ii = lax.shift_right_logical(lax.broadcasted_iota(jnp.int32, (w, w), 0), shift)
    jj = lax.shift_right_logical(lax.broadcasted_iota(jnp.int32, (w, w), 1), shift)
    e = (ii == jj).astype(BF16)
    hi = x.astype(BF16)
    lo = (x - hi.astype(F32)).astype(BF16)
    return (jnp.dot(hi, e, preferred_element_type=F32) + jnp.dot(lo, e, preferred_element_type=F32)) * (1.0 / seg)


def gated_norm_fwd(o_f, o_b, gate_arr, gcb, gn, seg, name="gated_norm_fwd"):
    t, w = o_f.shape
    tm = _row_tile(t)

    def body(of_ref, ob_ref, g_ref, gn_ref, y_ref):
        o = of_ref[...] + ob_ref[...]
        r = lax.rsqrt(_seg_mean(o * o, seg) + EPS)
        gt = g_ref[...]
        y_ref[...] = (gt * _sigmoid(gt) * (o * r * gn_ref[...])).astype(BF16)

    bw = max(seg, LANE)
    row = pl.BlockSpec((tm, bw), lambda j, i: (i, j))
    return pl.pallas_call(
        body, name=name, grid=(w // bw, t // tm),
        in_specs=[row, row, pl.BlockSpec((tm, bw), lambda j, i: (i, gcb + j)),
                  pl.BlockSpec((1, bw), lambda j, i: (0, j))],
        out_specs=row, out_shape=jax.ShapeDtypeStruct((t, w), BF16),
        compiler_params=_cparams(("parallel", "parallel")),
    )(o_f, o_b, gate_arr, gn.reshape(1, w))


def gated_norm_bwd(o_f, o_b, gate_arr, gcb, gn, seg, dy, name="gated_norm_bwd"):
    t, w = o_f.shape
    tm = _row_tile(t)

    def body(of_ref, ob_ref, g_ref, gn_ref, dy_ref, do_ref, dg_ref, dgn_ref):
        @pl.when(pl.program_id(1) == 0)
        def _():
            dgn_ref[...] = jnp.zeros_like(dgn_ref)
        o = of_ref[...] + ob_ref[...]
        r = lax.rsqrt(_seg_mean(o * o, seg) + EPS)
        xh = o * r
        gt = g_ref[...]
        sg = _sigmoid(gt)
        dyv = dy_ref[...]
        n = xh * gn_ref[...]
        dg_ref[...] = dyv * n * (sg * (1.0 + gt * (1.0 - sg)))
        dn = dyv * (gt * sg)
        dgn_ref[...] += jnp.sum(dn * xh, axis=0, keepdims=True)
        dxh = dn * gn_ref[...]
        do_ref[...] = r * (dxh - xh * _seg_mean(dxh * xh, seg))

    bw = max(seg, LANE)
    row = pl.BlockSpec((tm, bw), lambda j, i: (i, j))
    vec = pl.BlockSpec((1, bw), lambda j, i: (0, j))
    return pl.pallas_call(
        body, name=name, grid=(w // bw, t // tm),
        in_specs=[row, row, pl.BlockSpec((tm, bw), lambda j, i: (i, gcb + j)), vec, row],
        out_specs=[row, row, vec],
        out_shape=[jax.ShapeDtypeStruct((t, w), F32), jax.ShapeDtypeStruct((t, w), F32),
                   jax.ShapeDtypeStruct((1, w), F32)],
        compiler_params=_cparams(("parallel", "arbitrary")),
    )(o_f, o_b, gate_arr, gn.reshape(1, w), dy)


def gla_gate_fwd(p, wg, bg, name="gla_gate_fwd"):
    t = p.shape[0]
    tm = _row_tile(t)
    w = wg.shape[1]
    gcb = OD_GA // LANE

    def body(ga_ref, wg_ref, bg_ref, la_ref):
        z = _dot(ga_ref[...], wg_ref[...]) + bg_ref[...]
        la_ref[...] = (jnp.minimum(z, 0.0) - jnp.log(1.0 + jnp.exp(-jnp.abs(z)))) * (1.0 / GLA_TAU)

    return pl.pallas_call(
        body, name=name, grid=(t // tm,),
        in_specs=[pl.BlockSpec((tm, LANE), lambda i: (i, gcb)), pl.BlockSpec((LANE, w), lambda i: (0, 0)),
                  pl.BlockSpec((1, w), lambda i: (0, 0))],
        out_specs=pl.BlockSpec((tm, w), lambda i: (i, 0)),
        out_shape=jax.ShapeDtypeStruct((t, w), F32),
        compiler_params=_cparams(("parallel",)),
    )(p, wg, bg)


def gla_gate_bwd(p, wg, wg_t, bg, dla, name="gla_gate_bwd"):
    t = p.shape[0]
    tm = _row_tile(t)
    w = wg.shape[1]
    gcb = OD_GA // LANE

    def body(ga_ref, wg_ref, wgt_ref, bg_ref, dla_ref, dga_ref, dwg_ref, dbg_ref):
        @pl.when(pl.program_id(0) == 0)
        def _():
            dwg_ref[...] = jnp.zeros_like(dwg_ref)
            dbg_ref[...] = jnp.zeros_like(dbg_ref)
        ga = ga_ref[...]
        z = _dot(ga, wg_ref[...]) + bg_ref[...]
        dz = dla_ref[...] * (1.0 / GLA_TAU) * _sigmoid(-z)
        dga_ref[...] = _dot(dz, wgt_ref[...])
        dwg_ref[...] += _dot_tn(ga, dz)
        dbg_ref[...] += jnp.sum(dz, axis=0, keepdims=True)

    return pl.pallas_call(
        body, name=name, grid=(t // tm,),
        in_specs=[pl.BlockSpec((tm, LANE), lambda i: (i, gcb)), pl.BlockSpec((LANE, w), lambda i: (0, 0)),
                  pl.BlockSpec((w, LANE), lambda i: (0, 0)), pl.BlockSpec((1, w), lambda i: (0, 0)),
                  pl.BlockSpec((tm, w), lambda i: (i, 0))],
        out_specs=[pl.BlockSpec((tm, LANE), lambda i: (i, 0)), pl.BlockSpec((LANE, w), lambda i: (0, 0)),
                   pl.BlockSpec((1, w), lambda i: (0, 0))],
        out_shape=[jax.ShapeDtypeStruct((t, LANE), F32), jax.ShapeDtypeStruct((LANE, w), F32),
                   jax.ShapeDtypeStruct((1, w), F32)],
        compiler_params=_cparams(("arbitrary",)),
    )(p, wg, wg_t, bg, dla)


FFN_COL = 1408


def _shifted(x, prev_row, next_row, first, last):
    tm = x.shape[0]
    row = lax.broadcasted_iota(jnp.int32, x.shape, 0)
    pr = jnp.where(first, 0.0, prev_row)
    nx = jnp.where(last, 0.0, next_row)
    xm1 = jnp.where(row == 0, pr, pltpu.roll(x, 1, 0))
    xp1 = jnp.where(row == tm - 1, nx, pltpu.roll(x, tm - 1, 0))
    return xm1, xp1


def _halo_specs(tm, tc, t, colmap, rowaxis):
    nb8 = tm // SUBLANE
    last8 = t // SUBLANE - 1

    def prev(*ids):
        i = ids[rowaxis]
        return (jnp.maximum(i * nb8 - 1, 0), colmap(*ids))

    def nxt(*ids):
        i = ids[rowaxis]
        return (jnp.minimum((i + 1) * nb8, last8), colmap(*ids))

    return pl.BlockSpec((SUBLANE, tc), prev), pl.BlockSpec((SUBLANE, tc), nxt)


def ffn_act_fwd(up, conv_w, conv_b, name="ffn_act_fwd"):
    t = up.shape[0]
    tm, tc = _row_tile(t), FFN_COL
    ncol = D_FF // tc

    def body(g_ref, gp_ref, gn_ref, v_ref, w_ref, b_ref, a_ref):
        i = pl.program_id(0)
        g = g_ref[...]
        gm1, gp1 = _shifted(g, gp_ref[SUBLANE - 1:SUBLANE, :], gn_ref[0:1, :], i == 0, i == pl.num_programs(0) - 1)
        cc = w_ref[0:1, :] * gm1 + w_ref[1:2, :] * g + w_ref[2:3, :] * gp1 + b_ref[...]
        a_ref[...] = (cc * _sigmoid(cc) * v_ref[...]).astype(BF16)

    prev, nxt = _halo_specs(tm, tc, t, lambda i, j: j, 0)
    return pl.pallas_call(
        body, name=name, grid=(t // tm, ncol),
        in_specs=[pl.BlockSpec((tm, tc), lambda i, j: (i, j)), prev, nxt,
                  pl.BlockSpec((tm, tc), lambda i, j: (i, j + ncol)),
                  pl.BlockSpec((SUBLANE, tc), lambda i, j: (0, j)), pl.BlockSpec((1, tc), lambda i, j: (0, j))],
        out_specs=pl.BlockSpec((tm, tc), lambda i, j: (i, j)),
        out_shape=jax.ShapeDtypeStruct((t, D_FF), BF16),
        compiler_params=_cparams(("parallel", "parallel")),
    )(up, up, up, up, conv_w, conv_b)


def ffn_act_bwd(up, conv_w, conv_b, dact, name="ffn_act_bwd"):
    t = up.shape[0]
    tm, tc = _row_tile(t), FFN_COL
    ncol = D_FF // tc

    def body(g_ref, gp_ref, gn_ref, v_ref, w_ref, b_ref, da_ref, dc_ref, dv_ref, dw_ref):
        i = pl.program_id(1)

        @pl.when(i == 0)
        def _():
            dw_ref[...] = jnp.zeros_like(dw_ref)
        g = g_ref[...]
        gm1, gp1 = _shifted(g, gp_ref[SUBLANE - 1:SUBLANE, :], gn_ref[0:1, :], i == 0, i == pl.num_programs(1) - 1)
        cc = w_ref[0:1, :] * gm1 + w_ref[1:2, :] * g + w_ref[2:3, :] * gp1 + b_ref[...]
        sg = _sigmoid(cc)
        da = da_ref[...]
        dv_ref[...] = da * (cc * sg)
        dc = da * v_ref[...] * (sg * (1.0 + cc * (1.0 - sg)))
        dc_ref[...] = dc
        dw_ref[0:1, :] += jnp.sum(dc * gm1, axis=0, keepdims=True)
        dw_ref[1:2, :] += jnp.sum(dc * g, axis=0, keepdims=True)
        dw_ref[2:3, :] += jnp.sum(dc * gp1, axis=0, keepdims=True)
        dw_ref[3:4, :] += jnp.sum(dc, axis=0, keepdims=True)

    prev, nxt = _halo_specs(tm, tc, t, lambda j, i: j, 1)
    tile = pl.BlockSpec((tm, tc), lambda j, i: (i, j))
    return pl.pallas_call(
        body, name=name, grid=(ncol, t // tm),
        in_specs=[tile, prev, nxt, pl.BlockSpec((tm, tc), lambda j, i: (i, j + ncol)),
                  pl.BlockSpec((SUBLANE, tc), lambda j, i: (0, j)), pl.BlockSpec((1, tc), lambda j, i: (0, j)), tile],
        out_specs=[tile, tile, pl.BlockSpec((SUBLANE, tc), lambda j, i: (0, j))],
        out_shape=[jax.ShapeDtypeStruct((t, D_FF), F32), jax.ShapeDtypeStruct((t, D_FF), F32),
                   jax.ShapeDtypeStruct((SUBLANE, D_FF), F32)],
        compiler_params=_cparams(("parallel", "arbitrary")),
    )(up, up, up, up, conv_w, conv_b, dact)


def conv_transpose(dc, conv_w, name="conv_transpose"):
    t = dc.shape[0]
    tm, tc = _row_tile(t), FFN_COL

    def body(d_ref, dp_ref, dn_ref, w_ref, o_ref):
        i = pl.program_id(0)
        d = d_ref[...]
        dm1, dp1 = _shifted(d, dp_ref[SUBLANE - 1:SUBLANE, :], dn_ref[0:1, :], i == 0, i == pl.num_programs(0) - 1)
        o_ref[...] = w_ref[0:1, :] * dp1 + w_ref[1:2, :] * d + w_ref[2:3, :] * dm1

    prev, nxt = _halo_specs(tm, tc, t, lambda i, j: j, 0)
    tile = pl.BlockSpec((tm, tc), lambda i, j: (i, j))
    return pl.pallas_call(
        body, name=name, grid=(t // tm, D_FF // tc),
        in_specs=[tile, prev, nxt, pl.BlockSpec((SUBLANE, tc), lambda i, j: (0, j))],
        out_specs=tile, out_shape=jax.ShapeDtypeStruct((t, D_FF), F32),
        compiler_params=_cparams(("parallel", "parallel")),
    )(dc, dc, dc, conv_w)


def loss_head(y, target, name="loss_head"):
    t, d = y.shape
    tm = _row_tile(t)

    def body(y_ref, t_ref, dy_ref, l_ref):
        @pl.when(pl.program_id(0) == 0)
        def _():
            l_ref[...] = jnp.zeros_like(l_ref)
        e = y_ref[...] - t_ref[...]
        dy_ref[...] = e * (1.0 / d)
        rowloss = jnp.sum(e * e, axis=-1, keepdims=True) * (0.5 / d)
        l_ref[...] += jnp.sum(rowloss, axis=0, keepdims=True)

    row = pl.BlockSpec((tm, d), lambda i: (i, 0))
    return pl.pallas_call(
        body, name=name, grid=(t // tm,), in_specs=[row, row],
        out_specs=[row, pl.BlockSpec((1, 1), lambda i: (0, 0))],
        out_shape=[jax.ShapeDtypeStruct((t, d), F32), jax.ShapeDtypeStruct((1, 1), F32)],
        compiler_params=_cparams(("arbitrary",)),
    )(y, target)


def _pad_heads(w, heads, width):
    lead = w.shape[:-1]
    w = w.reshape(*lead, heads, width)
    w = jnp.pad(w, [(0, 0)] * len(lead) + [(0, 0), (0, LANE - width)])
    return w.reshape(*lead, heads * LANE)


def _unpad_heads(w, heads, width):
    lead = w.shape[:-1]
    return w.reshape(*lead, heads, LANE)[..., :width].reshape(*lead, heads * width)


def _pad_rows_heads(w, heads, width):
    return _pad_heads(w.T, heads, width).T


def _unpad_rows_heads(w, heads, width):
    return _unpad_heads(w.T, heads, width).T


_EV_REAL = MLA_Q_RANK + MLA_KV_RANK + MLA_ROPE


def prep_even(wts, dt=BF16):
    w_in = wts["w_in"]
    w_in_p = jnp.concatenate([w_in[:, :_EV_REAL], jnp.zeros((D_MODEL, EV_RQ - _EV_REAL), w_in.dtype),
                              w_in[:, _EV_REAL:]], axis=1).astype(dt)
    wuq = _pad_heads(wts["w_uq"], MLA_HEADS, MLA_QK).astype(dt)
    ukv = wts["w_ukv"].reshape(MLA_KV_RANK, MLA_HEADS, MLA_NOPE + MLA_V)
    wk = _pad_heads(ukv[..., :MLA_NOPE].reshape(MLA_KV_RANK, -1), MLA_HEADS, MLA_NOPE).astype(dt)
    wv = _pad_heads(ukv[..., MLA_NOPE:].reshape(MLA_KV_RANK, -1), MLA_HEADS, MLA_V).astype(dt)
    w_out = wts["w_out"]
    wa = _pad_rows_heads(w_out[:MLA_HEADS * MLA_V], MLA_HEADS, MLA_V).astype(dt)
    wr = w_out[MLA_HEADS * MLA_V:].astype(dt)
    pad1 = lambda v, n: jnp.pad(v.astype(F32), (0, n - v.shape[0])).reshape(1, n)
    lg = lambda th: jnp.log1p(-jnp.exp2(-th.astype(F32)))
    return dict(
        w_in=w_in_p, w_in_t=w_in_p.T, wuq=wuq, wuq_t=wuq.T, wk=wk, wk_t=wk.T, wv=wv, wv_t=wv.T,
        wa=wa, wa_t=wa.T, wr=wr, wr_t=wr.T,
        mix_norm=wts["mix_norm"].astype(F32), q_norm=wts["q_norm"].astype(F32).reshape(1, -1),
        kv_norm=wts["kv_norm"].astype(F32).reshape(1, -1),
        qhn=pad1(wts["q_head_norm"], LANE), khn=pad1(wts["k_head_norm"], LANE),
        la_f=jnp.repeat(lg(wts["theta_fwd"]), RET_DK).reshape(1, -1),
        la_b=jnp.repeat(lg(wts["theta_bwd"]), RET_DK).reshape(1, -1),
        out_norm=wts["ret_out_norm"].astype(F32).reshape(-1),
    )


def prep_odd(wts, dt=BF16):
    w_in = wts["w_in"]
    w_in_p = jnp.concatenate([w_in, jnp.zeros((D_MODEL, OD_IN - w_in.shape[1]), w_in.dtype)], axis=1).astype(dt)
    hk = GLA_HEADS * GLA_DK
    wg = jnp.zeros((LANE, 2 * hk), F32)
    wg = wg.at[:GLA_GATE_RANK, :hk].set(wts["w_gate_fwd"].astype(F32))
    wg = wg.at[GLA_GATE_RANK:2 * GLA_GATE_RANK, hk:].set(wts["w_gate_bwd"].astype(F32))
    wg = wg.astype(dt)
    bg = jnp.concatenate([wts["b_gate_fwd"], wts["b_gate_bwd"]]).astype(F32).reshape(1, -1)
    w_out = wts["w_out"].astype(dt)
    return dict(w_in=w_in_p, w_in_t=w_in_p.T, wg=wg, wg_t=wg.T, bg=bg, w_out=w_out, w_out_t=w_out.T,
                mix_norm=wts["mix_norm"].astype(F32), out_norm=wts["gla_out_norm"].astype(F32).reshape(-1))


def prep_ffn(wts, dt=BF16):
    w_up = wts["w_up"].astype(dt)
    w_down = wts["w_down"].astype(dt)
    cw = jnp.pad(wts["conv_w"].astype(F32), ((0, SUBLANE - 3), (0, 0)))
    return dict(w_up=w_up, w_up_t=w_up.T, w_down=w_down, w_down_t=w_down.T, conv_w=cw,
                conv_b=wts["conv_b"].astype(F32).reshape(1, -1), norm=wts["norm"].astype(F32))


_RET = dict(qcb=EV_RQ // LANE, kcb=EV_RK // LANE, vcb=EV_RV // LANE, la_row=True, chunk=RET_CHUNK, dvg=LANE,
            sub=2, qscale=1.0, kscale=RET_DK ** -0.5)
_GLA = dict(qcb=OD_Q // LANE, kcb=OD_K // LANE, vcb=OD_V // GLA_DV, la_row=False, chunk=GLA_CHUNK, dvg=GLA_DV,
            sub=1, qscale=GLA_DK ** -0.5, kscale=1.0)
_FWD_DIR = dict(reverse=False, inclusive=True)
_BWD_DIR = dict(reverse=True, inclusive=False)


def even_fwd(x, w, tabs_mla, tabs_ret, tag):
    h = rmsnorm_fwd(x, w["mix_norm"], name=f"{tag}_norm")
    p = mm_nn(h, w["w_in"], name=f"{tag}_in")
    q, k, v = mla_prep_fwd(p, tabs_mla, w["q_norm"], w["kv_norm"], w["wuq"], w["wk"], w["wv"], w["qhn"], w["khn"],
                           name=f"{tag}_mla_prep")
    o, lse = attn_fwd(q, k, v, name=f"{tag}_attn")
    of, stf = scan_fwd(p, p, p, w["la_f"], lacb=0, rope=tabs_ret, name=f"{tag}_ret_f", **_RET, **_FWD_DIR)
    ob, stb = scan_fwd(p, p, p, w["la_b"], lacb=0, rope=tabs_ret, name=f"{tag}_ret_b", **_RET, **_BWD_DIR)
    r = gated_norm_fwd(of, ob, p, EV_RG // LANE, w["out_norm"], RET_DV, name=f"{tag}_ret_out")
    x1 = mm_nn(o, w["wa"], res=x, name=f"{tag}_out_a")
    x2 = mm_nn(r, w["wr"], res=x1, name=f"{tag}_out_r")
    return x2, dict(x=x, h=h, p=p, q=q, k=k, v=v, o=o, lse=lse, of=of, ob=ob, stf=stf, stb=stb, r=r)


def even_bwd(dx, s, w, tabs_mla, tabs_ret, tag):
    tag = tag + "_b"
    do = mm_nn(dx, w["wa_t"], name=f"{tag}_dout_a")
    dr = mm_nn(dx, w["wr_t"], name=f"{tag}_dout_r")
    d_wa = mm_tn(s["o"], dx, name=f"{tag}_dwa")
    d_wr = mm_tn(s["r"], dx, name=f"{tag}_dwr")
    dq, dk, dv = attn_bwd(s["q"], s["k"], s["v"], s["o"], s["lse"], do, name=f"{tag}_attn")
    (dp_mla, d_wuq, d_wk, d_wv, d_qn, d_kvn, d_qhn, d_khn) = mla_prep_bwd(
        s["p"], tabs_mla, w["q_norm"], w["kv_norm"], w["wuq"], w["wk"], w["wv"], w["wuq_t"], w["wk_t"], w["wv_t"],
        w["qhn"], w["khn"], dq, dk, dv, name=f"{tag}_mla_prep")
    d_o, d_gate, d_gn = gated_norm_bwd(s["of"], s["ob"], s["p"], EV_RG // LANE, w["out_norm"], RET_DV, dr,
                                       name=f"{tag}_ret_out")
    p = s["p"]
    g1 = scan_bwd(p, p, p, w["la_f"], s["stf"], d_o, None, lacb=0, rope=tabs_ret, name=f"{tag}_ret_f",
                  **_RET, **_FWD_DIR)
    g2 = scan_bwd(p, p, p, w["la_b"], s["stb"], d_o, g1[:3], lacb=0, rope=tabs_ret, name=f"{tag}_ret_b",
                  **_RET, **_BWD_DIR)
    dp = jnp.concatenate([dp_mla, g2[0], g2[1], g2[2], d_gate], axis=1)
    dh = mm_nn(dp, w["w_in_t"], name=f"{tag}_dh")
    d_win = mm_tn(s["h"], dp, name=f"{tag}_dwin")
    dx_in, d_mix = rmsnorm_bwd(s["x"], w["mix_norm"], dh, dx, name=f"{tag}_norm")
    grads = dict(w_in=d_win, wuq=d_wuq, wk=d_wk, wv=d_wv, wa=d_wa, wr=d_wr, mix_norm=d_mix, q_norm=d_qn,
                 kv_norm=d_kvn, qhn=d_qhn, khn=d_khn, la_f=g1[3], la_b=g2[3], out_norm=d_gn)
    return dx_in, grads


def odd_fwd(x, w, tag):
    h = rmsnorm_fwd(x, w["mix_norm"], name=f"{tag}_norm")
    p = mm_nn(h, w["w_in"], name=f"{tag}_in")
    la = gla_gate_fwd(p, w["wg"], w["bg"], name=f"{tag}_gate")
    of, stf = scan_fwd(p, p, p, la, lacb=0, name=f"{tag}_gla_f", **_GLA, **_FWD_DIR)
    ob, stb = scan_fwd(p, p, p, la, lacb=N_GROUPS, name=f"{tag}_gla_b", **_GLA, **_BWD_DIR)
    y = gated_norm_fwd(of, ob, p, OD_R // GLA_DV, w["out_norm"], GLA_DV, name=f"{tag}_gla_out")
    x1 = mm_nn(y, w["w_out"], res=x, name=f"{tag}_out")
    return x1, dict(x=x, h=h, p=p, la=la, of=of, ob=ob, stf=stf, stb=stb, y=y)


def odd_bwd(dx, s, w, tag):
    tag = tag + "_b"
    dy = mm_nn(dx, w["w_out_t"], name=f"{tag}_dout")
    d_wout = mm_tn(s["y"], dx, name=f"{tag}_dwout")
    d_o, d_gate, d_gn = gated_norm_bwd(s["of"], s["ob"], s["p"], OD_R // GLA_DV, w["out_norm"], GLA_DV, dy,
                                       name=f"{tag}_gla_out")
    p, la = s["p"], s["la"]
    g1 = scan_bwd(p, p, p, la, s["stf"], d_o, None, lacb=0, name=f"{tag}_gla_f", **_GLA, **_FWD_DIR)
    g2 = scan_bwd(p, p, p, la, s["stb"], d_o, g1[:3], lacb=N_GROUPS, name=f"{tag}_gla_b", **_GLA, **_BWD_DIR)
    dla = jnp.concatenate([g1[3], g2[3]], axis=1)
    d_ga, d_wg, d_bg = gla_gate_bwd(p, w["wg"], w["wg_t"], w["bg"], dla, name=f"{tag}_gate")
    dp = jnp.concatenate([g2[0], g2[1], g2[2], d_gate, d_ga], axis=1)
    dh = mm_nn(dp, w["w_in_t"], name=f"{tag}_dh")
    d_win = mm_tn(s["h"], dp, name=f"{tag}_dwin")
    dx_in, d_mix = rmsnorm_bwd(s["x"], w["mix_norm"], dh, dx, name=f"{tag}_norm")
    grads = dict(w_in=d_win, wg=d_wg, bg=d_bg, w_out=d_wout, mix_norm=d_mix, out_norm=d_gn)
    return dx_in, grads


def ffn_fwd(x, w, tag):
    h = rmsnorm_fwd(x, w["norm"], name=f"{tag}_norm")
    up = mm_nn(h, w["w_up"], name=f"{tag}_up")
    act = ffn_act_fwd(up, w["conv_w"], w["conv_b"], name=f"{tag}_act")
    x1 = mm_nn(act, w["w_down"], res=x, name=f"{tag}_down")
    return x1, dict(x=x, h=h, up=up, act=act)


def ffn_bwd(dx, s, w, tag):
    tag = tag + "_b"
    dact = mm_nn(dx, w["w_down_t"], name=f"{tag}_dact")
    d_wdown = mm_tn(s["act"], dx, name=f"{tag}_dwdown")
    dc, dval, d_conv = ffn_act_bwd(s["up"], w["conv_w"], w["conv_b"], dact, name=f"{tag}_act")
    dgate = conv_transpose(dc, w["conv_w"], name=f"{tag}_convt")
    dh1 = mm_nn(dgate, w["w_up_t"][:D_FF], name=f"{tag}_dh_g")
    dh = mm_nn(dval, w["w_up_t"][D_FF:], res=dh1, name=f"{tag}_dh_v")
    d_wup = jnp.concatenate([mm_tn(s["h"], dgate, name=f"{tag}_dwup_g"), mm_tn(s["h"], dval, name=f"{tag}_dwup_v")],
                            axis=1)
    dx_in, d_norm = rmsnorm_bwd(s["x"], w["norm"], dh, dx, name=f"{tag}_norm")
    grads = dict(w_up=d_wup, w_down=d_wdown, conv_w=d_conv[:3], conv_b=d_conv[3], norm=d_norm)
    return dx_in, grads


def unprep_even_grads(g, theta_fwd, theta_bwd):
    d_win = jnp.concatenate([g["w_in"][:, :_EV_REAL], g["w_in"][:, EV_RQ:]], axis=1)
    d_uq = _unpad_heads(g["wuq"], MLA_HEADS, MLA_QK)
    dk_ = _unpad_heads(g["wk"], MLA_HEADS, MLA_NOPE).reshape(MLA_KV_RANK, MLA_HEADS, MLA_NOPE)
    dv_ = _unpad_heads(g["wv"], MLA_HEADS, MLA_V).reshape(MLA_KV_RANK, MLA_HEADS, MLA_V)
    d_ukv = jnp.concatenate([dk_, dv_], axis=-1).reshape(MLA_KV_RANK, -1)
    d_wout = jnp.concatenate([_unpad_rows_heads(g["wa"], MLA_HEADS, MLA_V), g["wr"]], axis=0)

    def dtheta(dla, th):
        dlg = dla.reshape(RET_HEADS, RET_DK).sum(axis=-1)
        e = jnp.exp2(-th.astype(F32))
        return dlg * (e * math.log(2.0)) / (1.0 - e)

    return dict(mix_norm=g["mix_norm"].reshape(-1), w_in=d_win, q_norm=g["q_norm"].reshape(-1),
                kv_norm=g["kv_norm"].reshape(-1), w_uq=d_uq, w_ukv=d_ukv, q_head_norm=g["qhn"].reshape(-1)[:MLA_QK],
                k_head_norm=g["khn"].reshape(-1)[:MLA_QK], theta_fwd=dtheta(g["la_f"], theta_fwd),
                theta_bwd=dtheta(g["la_b"], theta_bwd), ret_out_norm=g["out_norm"].reshape(RET_HEADS, RET_DV),
                w_out=d_wout)


def unprep_odd_grads(g):
    hk = GLA_HEADS * GLA_DK
    return dict(mix_norm=g["mix_norm"].reshape(-1), w_in=g["w_in"][:, :OD_GA + 2 * GLA_GATE_RANK],
                w_gate_fwd=g["wg"][:GLA_GATE_RANK, :hk], b_gate_fwd=g["bg"].reshape(-1)[:hk],
                w_gate_bwd=g["wg"][GLA_GATE_RANK:2 * GLA_GATE_RANK, hk:], b_gate_bwd=g["bg"].reshape(-1)[hk:],
                gla_out_norm=g["out_norm"].reshape(GLA_HEADS, GLA_DV), w_out=g["w_out"])


def unprep_ffn_grads(g):
    return dict(norm=g["norm"].reshape(-1), w_up=g["w_up"], conv_w=g["conv_w"], conv_b=g["conv_b"],
                w_down=g["w_down"])


def _mesh_pos():
    return tuple(lax.axis_index(n) for n in MESH_AXES)


def _slot(px, py, pc):
    return 4 * px + 2 * py + pc


def all_gather_blocks(blk, name="weight_all_gather"):
    r, w = blk.shape

    def body(x_ref, out_ref, send_sems, recv_sems, local_sem):
        x, y, c = _mesh_pos()
        me, sibling = (x, y, c), (x, y, 1 - c)
        chips = [(1 - x, y), (x, 1 - y), (1 - x, 1 - y)]

        def copy(k, block, to, src=None):
            dst = out_ref.at[_slot(*block)]
            return pltpu.make_async_remote_copy(
                src_ref=dst if src is None else src, dst_ref=dst, send_sem=send_sems.at[k],
                recv_sem=recv_sems.at[k], device_id=to, device_id_type=pl.DeviceIdType.MESH)

        mine = pltpu.make_async_copy(x_ref, out_ref.at[_slot(*me)], local_sem)
        mine.start()
        first = [copy(0, me, sibling, src=x_ref)]
        first += [copy(1 + j, me, (*chip, c), src=x_ref) for j, chip in enumerate(chips)]
        for cp in first:
            cp.start()
        passed = [copy(4 + j, (*chip, c), sibling) for j, chip in enumerate(chips)]
        for j, chip in enumerate(chips):
            copy(1 + j, (*chip, c), me).wait_recv()
            passed[j].start()
        copy(0, sibling, me).wait_recv()
        for j, chip in enumerate(chips):
            copy(4 + j, (*chip, 1 - c), me).wait_recv()
        for cp in first + passed:
            cp.wait_send()
        mine.wait()

    return pl.pallas_call(
        body, name=name,
        out_shape=jax.ShapeDtypeStruct((N_DEV, r, w), blk.dtype),
        in_specs=[pl.BlockSpec(memory_space=pl.ANY)],
        out_specs=pl.BlockSpec(memory_space=pl.ANY),
        scratch_shapes=[pltpu.SemaphoreType.DMA((7,)), pltpu.SemaphoreType.DMA((7,)), pltpu.SemaphoreType.DMA],
    )(blk)


def all_to_all_blocks(send, name="grad_all_to_all"):
    _, r, w = send.shape

    def body(s_ref, r_ref, send_sems, recv_sems, local_sem):
        x, y, c = _mesh_pos()
        me = _slot(x, y, c)
        mine = pltpu.make_async_copy(s_ref.at[me], r_ref.at[me], local_sem)
        mine.start()
        copies = []
        for k in range(1, N_DEV):
            px = 1 - x if (k >> 2) & 1 else x
            py = 1 - y if (k >> 1) & 1 else y
            pc = 1 - c if k & 1 else c
            cp = pltpu.make_async_remote_copy(
                src_ref=s_ref.at[_slot(px, py, pc)], dst_ref=r_ref.at[me], send_sem=send_sems.at[k - 1],
                recv_sem=recv_sems.at[k - 1], device_id=(px, py, pc), device_id_type=pl.DeviceIdType.MESH)
            cp.start()
            copies.append(cp)
        for cp in copies:
            cp.wait()
        mine.wait()

    return pl.pallas_call(
        body, name=name,
        out_shape=jax.ShapeDtypeStruct((N_DEV, r, w), send.dtype),
        in_specs=[pl.BlockSpec(memory_space=pl.ANY)],
        out_specs=pl.BlockSpec(memory_space=pl.ANY),
        scratch_shapes=[pltpu.SemaphoreType.DMA((7,)), pltpu.SemaphoreType.DMA((7,)), pltpu.SemaphoreType.DMA],
    )(send)


FLAT_W = 1024
FLAT_TILE = 256


def sum_slots(recv, name="grad_sum"):
    _, r, w = recv.shape

    def body(r_ref, o_ref):
        acc = r_ref[0]
        for k in range(1, N_DEV):
            acc = acc + r_ref[k]
        o_ref[...] = acc

    return pl.pallas_call(
        body, name=name, grid=(r // FLAT_TILE,),
        in_specs=[pl.BlockSpec((N_DEV, FLAT_TILE, w), lambda i: (0, i, 0))],
        out_specs=pl.BlockSpec((FLAT_TILE, w), lambda i: (i, 0)),
        out_shape=jax.ShapeDtypeStruct((r, w), F32),
        compiler_params=_cparams(("parallel",)),
    )(recv)


def adamw(wf, gf, mf, vf, name="adamw"):
    r, w = wf.shape

    def body(w_ref, g_ref, m_ref, v_ref, d_ref, m_out, v_out):
        g = g_ref[...]
        m = ADAM_B1 * m_ref[...] + (1.0 - ADAM_B1) * g
        v = ADAM_B2 * v_ref[...] + (1.0 - ADAM_B2) * (g * g)
        m_hat = m / (1.0 - ADAM_B1 ** ADAM_STEP)
        v_hat = v / (1.0 - ADAM_B2 ** ADAM_STEP)
        d_ref[...] = -ADAM_LR * (m_hat / (jnp.sqrt(v_hat) + ADAM_EPS) + ADAM_WD * w_ref[...])
        m_out[...] = m
        v_out[...] = v

    tile = pl.BlockSpec((FLAT_TILE, w), lambda i: (i, 0))
    return pl.pallas_call(
        body, name=name, grid=(r // FLAT_TILE,), in_specs=[tile] * 4, out_specs=[tile] * 3,
        out_shape=[jax.ShapeDtypeStruct((r, w), F32)] * 3,
        compiler_params=_cparams(("parallel",)),
    )(wf, gf, mf, vf)


SHARDED_BIG = [("w_in_even", 2), ("mla_w_uq", 2), ("mla_w_ukv", 2), ("w_out_even", 1), ("w_in_odd", 2),
               ("w_out_odd", 1), ("ffn_w_up", 2), ("ffn_w_down", 1)]
SHARDED_SMALL = [("mix_norm_odd", 1), ("gla_w_gate_fwd", 2), ("gla_b_gate_fwd", 1), ("gla_w_gate_bwd", 2),
                 ("gla_b_gate_bwd", 1), ("gla_out_norm", 2), ("ffn_conv_w", 2)]
REPLICATED = ["mix_norm_even", "mla_q_norm", "mla_kv_norm", "mla_q_head_norm", "mla_k_head_norm", "ret_theta_fwd",
              "ret_theta_bwd", "ret_out_norm", "ffn_norm", "ffn_conv_b"]
WEIGHT_NAMES = ["mix_norm_even", "w_in_even", "mla_q_norm", "mla_kv_norm", "mla_w_uq", "mla_w_ukv",
                "mla_q_head_norm", "mla_k_head_norm", "ret_theta_fwd", "ret_theta_bwd", "ret_out_norm", "w_out_even",
                "mix_norm_odd", "w_in_odd", "gla_w_gate_fwd", "gla_b_gate_fwd", "gla_w_gate_bwd", "gla_b_gate_bwd",
                "gla_out_norm", "w_out_odd", "ffn_norm", "ffn_w_up", "ffn_conv_w", "ffn_conv_b", "ffn_w_down"]


def _round_up(n, m):
    return -(-n // m) * m


def _pack_rows(parts, rows):
    flat = jnp.concatenate(parts, axis=-1)
    pad = rows * FLAT_W - flat.shape[-1]
    flat = jnp.pad(flat, [(0, 0)] * (flat.ndim - 1) + [(0, pad)])
    return flat.reshape(*flat.shape[:-1], rows, FLAT_W)


def _slab_layout(local_shapes):
    n_sh = sum(math.prod(local_shapes[n]) for n, _ in SHARDED_BIG + SHARDED_SMALL)
    n_rep = sum(math.prod(local_shapes[n]) for n in REPLICATED)
    sh_rows = _round_up(-(-n_sh // FLAT_W), SUBLANE)
    rep_rows = _round_up(-(-n_rep // FLAT_W), SUBLANE)
    return sh_rows, rep_rows, _round_up(sh_rows + rep_rows, FLAT_TILE)


def pack_slab(vals, local_shapes):
    sh_rows, rep_rows, rows = _slab_layout(local_shapes)
    sh = _pack_rows([vals[n].astype(F32).reshape(-1) for n, _ in SHARDED_BIG + SHARDED_SMALL], sh_rows)
    rep = _pack_rows([vals[n].astype(F32).reshape(-1) for n in REPLICATED], rep_rows)
    return jnp.concatenate([sh, rep, jnp.zeros((rows - sh_rows - rep_rows, FLAT_W), F32)], axis=0)


def unpack_slab(slab, local_shapes):
    sh_rows, rep_rows, _ = _slab_layout(local_shapes)
    out = {}
    flat = slab[:sh_rows].reshape(-1)
    off = 0
    for n, _ in SHARDED_BIG + SHARDED_SMALL:
        k = math.prod(local_shapes[n])
        out[n] = flat[off:off + k].reshape(local_shapes[n])
        off += k
    flat = slab[sh_rows:sh_rows + rep_rows].reshape(-1)
    off = 0
    for n in REPLICATED:
        k = math.prod(local_shapes[n])
        out[n] = flat[off:off + k].reshape(local_shapes[n])
        off += k
    return out


def pack_grad_blocks(full_grads, local_shapes):
    sh_rows, rep_rows, rows = _slab_layout(local_shapes)
    parts = []
    for n, ax in SHARDED_BIG + SHARDED_SMALL:
        g = full_grads[n].astype(F32)
        loc = local_shapes[n]
        g = g.reshape(*g.shape[:ax], N_DEV, loc[ax], *g.shape[ax + 1:])
        parts.append(jnp.moveaxis(g, ax, 0).reshape(N_DEV, -1))
    sh = _pack_rows(parts, sh_rows)
    rep = _pack_rows([full_grads[n].astype(F32).reshape(-1) for n in REPLICATED], rep_rows)
    rep = jnp.broadcast_to(rep[None], (N_DEV, rep_rows, FLAT_W))
    return jnp.concatenate([sh, rep, jnp.zeros((N_DEV, rows - sh_rows - rep_rows, FLAT_W), F32)], axis=1)


def pack_gather_block(vals, local_shapes):
    big = [vals[n].astype(BF16).reshape(-1) for n, _ in SHARDED_BIG]
    small = jnp.concatenate([vals[n].astype(F32).reshape(-1) for n, _ in SHARDED_SMALL])
    small = lax.bitcast_convert_type(small, BF16).reshape(-1)
    n = sum(v.shape[0] for v in big) + small.shape[0]
    rows = _round_up(-(-n // FLAT_W), 2 * SUBLANE)
    return _pack_rows(big + [small], rows)


def unpack_gathered(gathered, local_shapes):
    flat = gathered.reshape(N_DEV, -1)
    out = {}
    off = 0

    def full(piece, n, ax):
        loc = local_shapes[n]
        piece = jnp.moveaxis(piece.reshape(N_DEV, *loc), 0, ax)
        return piece.reshape(*loc[:ax], N_DEV * loc[ax], *loc[ax + 1:])

    for n, ax in SHARDED_BIG:
        k = math.prod(local_shapes[n])
        out[n] = full(flat[:, off:off + k], n, ax)
        off += k
    for n, ax in SHARDED_SMALL:
        k = math.prod(local_shapes[n])
        pairs = flat[:, off:off + 2 * k].reshape(N_DEV, k, 2)
        out[n] = full(lax.bitcast_convert_type(pairs, F32), n, ax)
        off += 2 * k
    return out


def local_step(x, positions, target, fw):
    tabs_mla = rope_tables(positions, MLA_NOPE, MLA_ROPE // 2, LANE)
    tabs_ret = rope_tables(positions, 0, RET_DK // 2, RET_DK)
    layers = []
    for layer in range(DEPTH):
        i = layer // 2
        if layer % 2 == 0:
            mw = prep_even(dict(
                w_in=fw["w_in_even"][i], w_uq=fw["mla_w_uq"][i], w_ukv=fw["mla_w_ukv"][i], w_out=fw["w_out_even"][i],
                mix_norm=fw["mix_norm_even"][i], q_norm=fw["mla_q_norm"][i], kv_norm=fw["mla_kv_norm"][i],
                q_head_norm=fw["mla_q_head_norm"][i], k_head_norm=fw["mla_k_head_norm"][i],
                theta_fwd=fw["ret_theta_fwd"][i], theta_bwd=fw["ret_theta_bwd"][i],
                ret_out_norm=fw["ret_out_norm"][i]))
        else:
            mw = prep_odd(dict(
                w_in=fw["w_in_odd"][i], w_gate_fwd=fw["gla_w_gate_fwd"][i], b_gate_fwd=fw["gla_b_gate_fwd"][i],
                w_gate_bwd=fw["gla_w_gate_bwd"][i], b_gate_bwd=fw["gla_b_gate_bwd"][i],
                gla_out_norm=fw["gla_out_norm"][i], w_out=fw["w_out_odd"][i], mix_norm=fw["mix_norm_odd"][i]))
        fwt = prep_ffn(dict(norm=fw["ffn_norm"][layer], w_up=fw["ffn_w_up"][layer], conv_w=fw["ffn_conv_w"][layer],
                            conv_b=fw["ffn_conv_b"][layer], w_down=fw["ffn_w_down"][layer]))
        layers.append((mw, fwt))

    saved = []
    for layer, (mw, fwt) in enumerate(layers):
        if layer % 2 == 0:
            x, sm = even_fwd(x, mw, tabs_mla, tabs_ret, f"l{layer}_mix")
        else:
            x, sm = odd_fwd(x, mw, f"l{layer}_mix")
        x, sf = ffn_fwd(x, fwt, f"l{layer}_ffn")
        saved.append((sm, sf))

    dx, loss = loss_head(x, target, name="loss_head")

    per_layer = [None] * DEPTH
    for layer in reversed(range(DEPTH)):
        mw, fwt = layers[layer]
        sm, sf = saved[layer]
        dx, gf = ffn_bwd(dx, sf, fwt, f"l{layer}_ffn")
        if layer % 2 == 0:
            dx, gm = even_bwd(dx, sm, mw, tabs_mla, tabs_ret, f"l{layer}_mix")
            gm = unprep_even_grads(gm, fw["ret_theta_fwd"][layer // 2], fw["ret_theta_bwd"][layer // 2])
        else:
            dx, gm = odd_bwd(dx, sm, mw, f"l{layer}_mix")
            gm = unprep_odd_grads(gm)
        per_layer[layer] = (gm, unprep_ffn_grads(gf))

    ev = [per_layer[l][0] for l in range(0, DEPTH, 2)]
    od = [per_layer[l][0] for l in range(1, DEPTH, 2)]
    ff = [per_layer[l][1] for l in range(DEPTH)]
    st = lambda lst, key: jnp.stack([g[key] for g in lst])
    grads = {
        "mix_norm_even": st(ev, "mix_norm"), "w_in_even": st(ev, "w_in"), "mla_q_norm": st(ev, "q_norm"),
        "mla_kv_norm": st(ev, "kv_norm"), "mla_w_uq": st(ev, "w_uq"), "mla_w_ukv": st(ev, "w_ukv"),
        "mla_q_head_norm": st(ev, "q_head_norm"), "mla_k_head_norm": st(ev, "k_head_norm"),
        "ret_theta_fwd": st(ev, "theta_fwd"), "ret_theta_bwd": st(ev, "theta_bwd"),
        "ret_out_norm": st(ev, "ret_out_norm"), "w_out_even": st(ev, "w_out"),
        "mix_norm_odd": st(od, "mix_norm"), "w_in_odd": st(od, "w_in"), "gla_w_gate_fwd": st(od, "w_gate_fwd"),
        "gla_b_gate_fwd": st(od, "b_gate_fwd"), "gla_w_gate_bwd": st(od, "w_gate_bwd"),
        "gla_b_gate_bwd": st(od, "b_gate_bwd"), "gla_out_norm": st(od, "gla_out_norm"), "w_out_odd": st(od, "w_out"),
        "ffn_norm": st(ff, "norm"), "ffn_w_up": st(ff, "w_up"), "ffn_conv_w": st(ff, "conv_w"),
        "ffn_conv_b": st(ff, "conv_b"), "ffn_w_down": st(ff, "w_down"),
    }
    return loss, dx, grads


def kernel(x, positions, mix_norm_even, w_in_even, mla_q_norm, mla_kv_norm, mla_w_uq, mla_w_ukv, mla_q_head_norm, mla_k_head_norm, ret_theta_fwd, ret_theta_bwd, ret_out_norm, w_out_even, mix_norm_odd, w_in_odd, gla_w_gate_fwd, gla_b_gate_fwd, gla_w_gate_bwd, gla_b_gate_bwd, gla_out_norm, w_out_odd, ffn_norm, ffn_w_up, ffn_conv_w, ffn_conv_b, ffn_w_down, loss_target, m_mix_norm_even, m_w_in_even, m_mla_q_norm, m_mla_kv_norm, m_mla_w_uq, m_mla_w_ukv, m_mla_q_head_norm, m_mla_k_head_norm, m_ret_theta_fwd, m_ret_theta_bwd, m_ret_out_norm, m_w_out_even, m_mix_norm_odd, m_w_in_odd, m_gla_w_gate_fwd, m_gla_b_gate_fwd, m_gla_w_gate_bwd, m_gla_b_gate_bwd, m_gla_out_norm, m_w_out_odd, m_ffn_norm, m_ffn_w_up, m_ffn_conv_w, m_ffn_conv_b, m_ffn_w_down, v_mix_norm_even, v_w_in_even, v_mla_q_norm, v_mla_kv_norm, v_mla_w_uq, v_mla_w_ukv, v_mla_q_head_norm, v_mla_k_head_norm, v_ret_theta_fwd, v_ret_theta_bwd, v_ret_out_norm, v_w_out_even, v_mix_norm_odd, v_w_in_odd, v_gla_w_gate_fwd, v_gla_b_gate_fwd, v_gla_w_gate_bwd, v_gla_b_gate_bwd, v_gla_out_norm, v_w_out_odd, v_ffn_norm, v_ffn_w_up, v_ffn_conv_w, v_ffn_conv_b, v_ffn_w_down):
    a = dict(locals())
    wts = {n: a[n] for n in WEIGHT_NAMES}
    local_shapes = {n: tuple(wts[n].shape) for n in WEIGHT_NAMES}

    gathered = all_gather_blocks(pack_gather_block(wts, local_shapes))
    fw = unpack_gathered(gathered, local_shapes)
    for n in REPLICATED:
        fw[n] = wts[n]

    loss, grad_x, grads = local_step(x[0], positions, loss_target[0], fw)

    recv = all_to_all_blocks(pack_grad_blocks(grads, local_shapes))
    g_slab = sum_slots(recv)
    w_slab = pack_slab(wts, local_shapes)
    m_slab = pack_slab({n: a["m_" + n] for n in WEIGHT_NAMES}, local_shapes)
    v_slab = pack_slab({n: a["v_" + n] for n in WEIGHT_NAMES}, local_shapes)
    d_slab, m_new, v_new = adamw(w_slab, g_slab, m_slab, v_slab)

    g_out = unpack_slab(g_slab, local_shapes)
    d_out = unpack_slab(d_slab, local_shapes)
    m_out = unpack_slab(m_new, local_shapes)
    v_out = unpack_slab(v_new, local_shapes)
    total = lax.psum(loss[0, 0], MESH_AXES)
    return (total, grad_x[None], *[g_out[n] for n in WEIGHT_NAMES], *[d_out[n] for n in WEIGHT_NAMES],
            *[m_out[n] for n in WEIGHT_NAMES], *[v_out[n] for n in WEIGHT_NAMES])
```

```python
import math

import jax
import jax.numpy as jnp
from jax import lax
from jax.experimental import pallas as pl
from jax.experimental.pallas import tpu as pltpu

F32 = jnp.float32
BF16 = jnp.bfloat16

D_MODEL = 1024
DEPTH = 4
N_DEV = 8
MESH_AXES = ("x", "y", "c")

MLA_HEADS = 8
MLA_Q_RANK = 384
MLA_KV_RANK = 256
MLA_NOPE = 64
MLA_ROPE = 32
MLA_V = 64
MLA_QK = MLA_NOPE + MLA_ROPE
RET_HEADS = 8
RET_DK = 64
RET_DV = 64
RET_CHUNK = 128
GLA_HEADS = 4
GLA_DK = 128
GLA_DV = 256
GLA_GATE_RANK = 16
GLA_TAU = 16.0
GLA_CHUNK = 64
D_FF = 2816
ROPE_THETA = 10000.0
EPS = 1e-6

ADAM_LR = 0.001
ADAM_B1 = 0.9
ADAM_B2 = 0.999
ADAM_EPS = 1e-08
ADAM_WD = 0.01
ADAM_STEP = 10

LANE = 128
SUBLANE = 8
ROW_TILE = 512
VMEM_LIMIT = 56 * 1024 * 1024
WEIGHT_TILE_BYTES = 6 * 1024 * 1024

EV_CQ, EV_CKV, EV_KR, EV_RQ, EV_RK, EV_RV, EV_RG, EV_IN = 0, 384, 640, 768, 1280, 1792, 2304, 2816
OD_Q, OD_K, OD_V, OD_R, OD_GA, OD_IN = 0, 512, 1024, 2048, 3072, 3200
N_GROUPS = 4


def _cparams(sem):
    return pltpu.CompilerParams(dimension_semantics=sem, vmem_limit_bytes=VMEM_LIMIT)


def _dot(a, b):
    return jnp.dot(a.astype(BF16), b.astype(BF16), preferred_element_type=F32)


def _dot_nt(a, b):
    return lax.dot_general(a.astype(BF16), b.astype(BF16), (((1,), (1,)), ((), ())), preferred_element_type=F32)


def _dot_tn(a, b):
    return lax.dot_general(a.astype(BF16), b.astype(BF16), (((0,), (0,)), ((), ())), preferred_element_type=F32)


def _dot_f32(a, b):
    return jnp.dot(a, b, preferred_element_type=F32, precision=lax.Precision.HIGHEST)


def _dot_tn_f32(a, b):
    return lax.dot_general(a, b, (((0,), (0,)), ((), ())), preferred_element_type=F32,
                           precision=lax.Precision.HIGHEST)


def _sigmoid(x):
    return 1.0 / (1.0 + jnp.exp(-x))


def _col_tile(k, n, itemsize=2):
    best = LANE
    for t in range(LANE, n + 1, LANE):
        if n % t == 0 and k * t * itemsize <= WEIGHT_TILE_BYTES:
            best = t
    return best if n % LANE == 0 else n


def _row_tile(m):
    return min(ROW_TILE, m)


def mm_nn(a, b, res=None, out_dtype=F32, name="mm_nn"):
    m, k = a.shape
    n = b.shape[1]
    tm, tn = _row_tile(m), _col_tile(k, n)

    def body(*refs):
        if res is None:
            a_ref, b_ref, o_ref = refs
        else:
            a_ref, b_ref, r_ref, o_ref = refs
        acc = _dot(a_ref[...], b_ref[...])
        if res is not None:
            acc = acc + r_ref[...].astype(F32)
        o_ref[...] = acc.astype(out_dtype)

    in_specs = [pl.BlockSpec((tm, k), lambda i, j: (i, 0)), pl.BlockSpec((k, tn), lambda i, j: (0, j))]
    args = [a, b]
    if res is not None:
        in_specs.append(pl.BlockSpec((tm, tn), lambda i, j: (i, j)))
        args.append(res)
    return pl.pallas_call(
        body, name=name, grid=(m // tm, n // tn), in_specs=in_specs,
        out_specs=pl.BlockSpec((tm, tn), lambda i, j: (i, j)),
        out_shape=jax.ShapeDtypeStruct((m, n), out_dtype),
        compiler_params=_cparams(("parallel", "parallel")),
    )(*args)


def mm_tn(a, b, name="mm_tn"):
    t, k = a.shape
    n = b.shape[1]
    tt = _row_tile(t)
    tk = k if k <= 1024 else _col_tile(1024, k, 4)
    tn = n if n <= 1024 else _col_tile(1024, n, 4)

    def body(a_ref, b_ref, o_ref):
        @pl.when(pl.program_id(2) == 0)
        def _():
            o_ref[...] = jnp.zeros_like(o_ref)
        o_ref[...] += _dot_tn(a_ref[...], b_ref[...])

    return pl.pallas_call(
        body, name=name, grid=(k // tk, n // tn, t // tt),
        in_specs=[pl.BlockSpec((tt, tk), lambda i, j, s: (s, i)), pl.BlockSpec((tt, tn), lambda i, j, s: (s, j))],
        out_specs=pl.BlockSpec((tk, tn), lambda i, j, s: (i, j)),
        out_shape=jax.ShapeDtypeStruct((k, n), F32),
        compiler_params=_cparams(("parallel", "parallel", "arbitrary")),
    )(a, b)


def rmsnorm_fwd(x, g, name="rmsnorm_fwd"):
    t, d = x.shape
    tm = _row_tile(t)

    def body(x_ref, g_ref, h_ref):
        xv = x_ref[...]
        r = lax.rsqrt(jnp.mean(xv * xv, axis=-1, keepdims=True) + EPS)
        h_ref[...] = (xv * r * g_ref[...]).astype(BF16)

    return pl.pallas_call(
        body, name=name, grid=(t // tm,),
        in_specs=[pl.BlockSpec((tm, d), lambda i: (i, 0)), pl.BlockSpec((1, d), lambda i: (0, 0))],
        out_specs=pl.BlockSpec((tm, d), lambda i: (i, 0)),
        out_shape=jax.ShapeDtypeStruct((t, d), BF16),
        compiler_params=_cparams(("parallel",)),
    )(x, g.reshape(1, d))


def rmsnorm_bwd(x, g, dh, dres, name="rmsnorm_bwd"):
    t, d = x.shape
    tm = _row_tile(t)

    def body(x_ref, g_ref, dh_ref, dres_ref, dx_ref, dg_ref):
        @pl.when(pl.program_id(0) == 0)
        def _():
            dg_ref[...] = jnp.zeros_like(dg_ref)
        xv = x_ref[...]
        r = lax.rsqrt(jnp.mean(xv * xv, axis=-1, keepdims=True) + EPS)
        xh = xv * r
        dhv = dh_ref[...]
        dg_ref[...] += jnp.sum(dhv * xh, axis=0, keepdims=True)
        dxh = dhv * g_ref[...]
        dx_ref[...] = dres_ref[...] + r * (dxh - xh * jnp.mean(dxh * xh, axis=-1, keepdims=True))

    row = pl.BlockSpec((tm, d), lambda i: (i, 0))
    vec = pl.BlockSpec((1, d), lambda i: (0, 0))
    return pl.pallas_call(
        body, name=name, grid=(t // tm,),
        in_specs=[row, vec, row, row], out_specs=[row, vec],
        out_shape=[jax.ShapeDtypeStruct((t, d), F32), jax.ShapeDtypeStruct((1, d), F32)],
        compiler_params=_cparams(("arbitrary",)),
    )(x, g.reshape(1, d), dh, dres)


def _rope_apply(x, cos, s1, s2, half):
    return x * cos + pltpu.roll(x, LANE - half, 1) * s1 + pltpu.roll(x, half, 1) * s2


def _rope_transpose(dy, cos, s1, s2, half):
    return dy * cos + pltpu.roll(dy * s1, half, 1) + pltpu.roll(dy * s2, LANE - half, 1)


def rope_tables(positions, lane_start, half, period):
    pos = positions.reshape(-1).astype(F32)
    inv = ROPE_THETA ** (-jnp.arange(half, dtype=F32) / half)
    ang = pos[:, None] * inv[None, :]
    cos, sin = jnp.cos(ang), jnp.sin(ang)
    t = pos.shape[0]
    pre = lane_start
    post = period - lane_start - 2 * half
    ones = lambda n: jnp.ones((t, n), F32)
    zeros = lambda n: jnp.zeros((t, n), F32)
    c = jnp.concatenate([ones(pre), cos, cos, ones(post)], axis=1)
    a = jnp.concatenate([zeros(pre), -sin, zeros(half), zeros(post)], axis=1)
    b = jnp.concatenate([zeros(pre), zeros(half), sin, zeros(post)], axis=1)
    rep = LANE // period
    return tuple(jnp.tile(v, (1, rep)) for v in (c, a, b))


def _mla_forward_tile(p, cos, s1, s2, qn_g, kvn_g, wuq, wk, wv, qhn, khn):
    cq = p[:, EV_CQ:EV_CKV]
    ckv = p[:, EV_CKV:EV_KR]
    kr = p[:, EV_KR:EV_RQ]
    rq = lax.rsqrt(jnp.mean(cq * cq, axis=-1, keepdims=True) + EPS)
    rkv = lax.rsqrt(jnp.mean(ckv * ckv, axis=-1, keepdims=True) + EPS)
    qn = cq * rq * qn_g
    kvn = ckv * rkv * kvn_g
    q_raw = _dot(qn, wuq)
    k_raw = _dot(kvn, wk)
    v = _dot(kvn, wv)
    krp = pltpu.roll(kr, MLA_NOPE, 1)
    return cq, ckv, rq, rkv, qn, kvn, q_raw, k_raw, v, krp


def _head_norm(xh, g):
    r = lax.rsqrt(jnp.sum(xh * xh, axis=-1, keepdims=True) * (1.0 / MLA_QK) + EPS)
    return xh * r * g, r


def mla_prep_fwd(p, tabs, qn_g, kvn_g, wuq, wk, wv, qhn, khn, name="mla_prep_fwd"):
    t = p.shape[0]
    tm = _row_tile(t)
    hw = MLA_HEADS * LANE

    def body(p_ref, c_ref, s1_ref, s2_ref, qn_ref, kvn_ref, wuq_ref, wk_ref, wv_ref, qhn_ref, khn_ref,
             q_out, k_out, v_out):
        cos, s1, s2 = c_ref[...], s1_ref[...], s2_ref[...]
        (_, _, _, _, _, _, q_raw, k_raw, v, krp) = _mla_forward_tile(
            p_ref[...], cos, s1, s2, qn_ref[...], kvn_ref[...], wuq_ref[...], wk_ref[...], wv_ref[...],
            qhn_ref[...], khn_ref[...])
        v_out[...] = v.astype(BF16)
        for h in range(MLA_HEADS):
            sl = slice(h * LANE, (h + 1) * LANE)
            qh, _ = _head_norm(q_raw[:, sl], qhn_ref[...])
            kh, _ = _head_norm(k_raw[:, sl] + krp, khn_ref[...])
            q_out[:, sl] = (_rope_apply(qh, cos, s1, s2, MLA_ROPE // 2) * ATTN_QSCALE).astype(BF16)
            k_out[:, sl] = _rope_apply(kh, cos, s1, s2, MLA_ROPE // 2).astype(BF16)

    row = lambda w: pl.BlockSpec((tm, w), lambda i: (i, 0))
    full = lambda a: pl.BlockSpec(a.shape, lambda i: (0,) * a.ndim)
    ws = [qn_g, kvn_g, wuq, wk, wv, qhn, khn]
    return pl.pallas_call(
        body, name=name, grid=(t // tm,),
        in_specs=[row(EV_RQ), row(LANE), row(LANE), row(LANE)] + [full(w) for w in ws],
        out_specs=[row(hw)] * 3,
        out_shape=[jax.ShapeDtypeStruct((t, hw), BF16)] * 3,
        compiler_params=_cparams(("parallel",)),
    )(p, *tabs, *ws)


def mla_prep_bwd(p, tabs, qn_g, kvn_g, wuq, wk, wv, wuq_t, wk_t, wv_t, qhn, khn, dq, dk, dv,
                 name="mla_prep_bwd"):
    t = p.shape[0]
    tm = _row_tile(t)
    hw = MLA_HEADS * LANE

    def body(p_ref, c_ref, s1_ref, s2_ref, qn_ref, kvn_ref, wuq_ref, wk_ref, wv_ref, wuqt_ref, wkt_ref, wvt_ref,
             qhn_ref, khn_ref, dq_ref, dk_ref, dv_ref,
             dp_ref, dwuq_ref, dwk_ref, dwv_ref, dqn_ref, dkvn_ref, dqhn_ref, dkhn_ref, dqraw_s, dkraw_s):
        @pl.when(pl.program_id(0) == 0)
        def _():
            for r in (dwuq_ref, dwk_ref, dwv_ref, dqn_ref, dkvn_ref, dqhn_ref, dkhn_ref):
                r[...] = jnp.zeros_like(r)
        cos, s1, s2 = c_ref[...], s1_ref[...], s2_ref[...]
        qhn_v, khn_v = qhn_ref[...], khn_ref[...]
        (cq, ckv, rq, rkv, qn, kvn, q_raw, k_raw, _, krp) = _mla_forward_tile(
            p_ref[...], cos, s1, s2, qn_ref[...], kvn_ref[...], wuq_ref[...], wk_ref[...], wv_ref[...],
            qhn_v, khn_v)
        half = MLA_ROPE // 2
        dkr_sum = jnp.zeros((tm, LANE), F32)
        dqhn_acc = jnp.zeros((1, LANE), F32)
        dkhn_acc = jnp.zeros((1, LANE), F32)
        for h in range(MLA_HEADS):
            sl = slice(h * LANE, (h + 1) * LANE)
            xq = q_raw[:, sl]
            _, r = _head_norm(xq, qhn_v)
            xh = xq * r
            dy = _rope_transpose(dq_ref[:, sl] * ATTN_SCALE, cos, s1, s2, half)
            dqhn_acc = dqhn_acc + jnp.sum(dy * xh, axis=0, keepdims=True)
            dxh = dy * qhn_v
            dqraw_s[:, sl] = r * (dxh - xh * (jnp.sum(dxh * xh, axis=-1, keepdims=True) * (1.0 / MLA_QK)))
            xk = k_raw[:, sl] + krp
            _, r = _head_norm(xk, khn_v)
            xh = xk * r
            dy = _rope_transpose(dk_ref[:, sl] * math.log(2.0), cos, s1, s2, half)
            dkhn_acc = dkhn_acc + jnp.sum(dy * xh, axis=0, keepdims=True)
            dxh = dy * khn_v
            dxk = r * (dxh - xh * (jnp.sum(dxh * xh, axis=-1, keepdims=True) * (1.0 / MLA_QK)))
            dkraw_s[:, sl] = dxk
            dkr_sum = dkr_sum + dxk
        dqhn_ref[...] += dqhn_acc
        dkhn_ref[...] += dkhn_acc
        dq_raw = dqraw_s[...]
        dk_raw = dkraw_s[...]
        dvv = dv_ref[...]
        dwuq_ref[...] += _dot_tn(qn, dq_raw)
        dwk_ref[...] += _dot_tn(kvn, dk_raw)
        dwv_ref[...] += _dot_tn(kvn, dvv)
        dqn = _dot(dq_raw, wuqt_ref[...])
        dkvn = _dot(dk_raw, wkt_ref[...]) + _dot(dvv, wvt_ref[...])
        xh = cq * rq
        dqn_ref[...] += jnp.sum(dqn * xh, axis=0, keepdims=True)
        dxh = dqn * qn_ref[...]
        dp_ref[:, EV_CQ:EV_CKV] = rq * (dxh - xh * jnp.mean(dxh * xh, axis=-1, keepdims=True))
        xh = ckv * rkv
        dkvn_ref[...] += jnp.sum(dkvn * xh, axis=0, keepdims=True)
        dxh = dkvn * kvn_ref[...]
        dp_ref[:, EV_CKV:EV_KR] = rkv * (dxh - xh * jnp.mean(dxh * xh, axis=-1, keepdims=True))
        lane = lax.broadcasted_iota(jnp.int32, (tm, LANE), 1)
        dp_ref[:, EV_KR:EV_RQ] = jnp.where(lane < MLA_ROPE, pltpu.roll(dkr_sum, LANE - MLA_NOPE, 1), 0.0)

    row = lambda w: pl.BlockSpec((tm, w), lambda i: (i, 0))
    full = lambda a: pl.BlockSpec(a.shape, lambda i: (0,) * a.ndim)
    ws = [qn_g, kvn_g, wuq, wk, wv, wuq_t, wk_t, wv_t, qhn, khn]
    outs = [jax.ShapeDtypeStruct((t, EV_RQ), F32), jax.ShapeDtypeStruct(wuq.shape, F32),
            jax.ShapeDtypeStruct(wk.shape, F32), jax.ShapeDtypeStruct(wv.shape, F32),
            jax.ShapeDtypeStruct(qn_g.shape, F32), jax.ShapeDtypeStruct(kvn_g.shape, F32),
            jax.ShapeDtypeStruct(qhn.shape, F32), jax.ShapeDtypeStruct(khn.shape, F32)]
    return pl.pallas_call(
        body, name=name, grid=(t // tm,),
        in_specs=[row(EV_RQ), row(LANE), row(LANE), row(LANE)] + [full(w) for w in ws] + [row(hw)] * 3,
        out_specs=[row(EV_RQ)] + [full(o) for o in outs[1:]],
        out_shape=outs,
        scratch_shapes=[pltpu.VMEM((tm, hw), F32), pltpu.VMEM((tm, hw), F32)],
        compiler_params=_cparams(("arbitrary",)),
    )(p, *tabs, *ws, dq, dk, dv)


ATTN_SCALE = MLA_QK ** -0.5
ATTN_QSCALE = ATTN_SCALE * math.log2(math.e)
ATTN_FWD_TQ, ATTN_FWD_TK = 512, 8192
ATTN_BWD_TQ, ATTN_BWD_TK = 256, 4096


def attn_fwd(q, k, v, name="attn_fwd"):
    t = q.shape[0]
    tq, tk = min(ATTN_FWD_TQ, _row_tile(t)), min(ATTN_FWD_TK, t)
    nh = MLA_HEADS

    def body(q_ref, k_ref, v_ref, o_ref, lse_ref, m_s, l_s, acc_s):
        j = pl.program_id(2)

        @pl.when(j == 0)
        def _():
            m_s[...] = jnp.full_like(m_s, -jnp.inf)
            l_s[...] = jnp.zeros_like(l_s)
            acc_s[...] = jnp.zeros_like(acc_s)

        s = _dot_nt(q_ref[...], k_ref[...])
        m_old = m_s[...]
        m_new = jnp.maximum(m_old, jnp.max(s, axis=-1, keepdims=True))
        pr = jnp.exp2(s - m_new)
        alpha = jnp.exp2(m_old - m_new)
        l_s[...] = alpha * l_s[...] + jnp.sum(pr, axis=-1, keepdims=True)
        acc_s[...] = alpha * acc_s[...] + _dot(pr, v_ref[...])
        m_s[...] = m_new

        @pl.when(j == pl.num_programs(2) - 1)
        def _():
            o_ref[...] = acc_s[...] / l_s[...]
            lse_ref[...] = m_s[...] + jnp.log2(l_s[...])

    return pl.pallas_call(
        body, name=name, grid=(nh, t // tq, t // tk),
        in_specs=[pl.BlockSpec((tq, LANE), lambda h, i, j: (i, h)),
                  pl.BlockSpec((tk, LANE), lambda h, i, j: (j, h)),
                  pl.BlockSpec((tk, LANE), lambda h, i, j: (j, h))],
        out_specs=[pl.BlockSpec((tq, LANE), lambda h, i, j: (i, h)),
                   pl.BlockSpec((None, tq, 1), lambda h, i, j: (h, i, 0))],
        out_shape=[jax.ShapeDtypeStruct((t, nh * LANE), F32), jax.ShapeDtypeStruct((nh, t, 1), F32)],
        scratch_shapes=[pltpu.VMEM((tq, 1), F32), pltpu.VMEM((tq, 1), F32), pltpu.VMEM((tq, LANE), F32)],
        compiler_params=_cparams(("parallel", "parallel", "arbitrary")),
    )(q, k, v)


def attn_bwd(q, k, v, o, lse, do, name="attn_bwd"):
    t = q.shape[0]
    tq, tk = min(ATTN_BWD_TQ, _row_tile(t)), min(ATTN_BWD_TK, t)
    nh = MLA_HEADS
    nq = t // tq

    def body(q_ref, k_ref, v_ref, o_ref, lse_ref, do_ref, dq_ref, dk_ref, dv_ref):
        kj, qi = pl.program_id(1), pl.program_id(2)

        @pl.when(qi == 0)
        def _():
            dk_ref[...] = jnp.zeros_like(dk_ref)
            dv_ref[...] = jnp.zeros_like(dv_ref)

        qv, kv, vv, dov = q_ref[...], k_ref[...], v_ref[...], do_ref[...]
        s = _dot_nt(qv, kv)
        pr = jnp.exp2(s - lse_ref[...])
        dp = _dot_nt(dov, vv)
        delta = jnp.sum(dov * o_ref[...], axis=-1, keepdims=True)
        ds = pr * (dp - delta)
        dv_ref[...] += _dot_tn(pr, dov)
        dk_ref[...] += _dot_tn(ds, qv)
        dq_tile = _dot(ds, kv)
        rows = pl.ds(pl.multiple_of(qi * tq, tq), tq)

        @pl.when(kj == 0)
        def _():
            dq_ref[rows, :] = dq_tile

        @pl.when(kj != 0)
        def _():
            dq_ref[rows, :] += dq_tile

    qspec = pl.BlockSpec((tq, LANE), lambda h, j, i: (i, h))
    kspec = pl.BlockSpec((tk, LANE), lambda h, j, i: (j, h))
    return pl.pallas_call(
        body, name=name, grid=(nh, t // tk, nq),
        in_specs=[qspec, kspec, kspec, qspec, pl.BlockSpec((None, tq, 1), lambda h, j, i: (h, i, 0)), qspec],
        out_specs=[pl.BlockSpec((t, LANE), lambda h, j, i: (0, h)), kspec, kspec],
        out_shape=[jax.ShapeDtypeStruct((t, nh * LANE), F32)] * 3,
        compiler_params=_cparams(("parallel", "arbitrary", "arbitrary")),
    )(q, k, v, o, lse, do)


def _scan_consts(c, reverse, inclusive):
    ii = lax.broadcasted_iota(jnp.int32, (c, c), 0)
    jj = lax.broadcasted_iota(jnp.int32, (c, c), 1)
    if reverse:
        incl = jj >= ii
        mask = incl if inclusive else jj > ii
    else:
        incl = jj <= ii
        mask = incl if inclusive else jj < ii
    mid = (c - 1 - c // 2) if reverse else c // 2
    return incl.astype(F32), mask.astype(F32), mid


def _sub_masks(sub, dvg, u):
    if sub == 1:
        return None, None
    kl = lax.broadcasted_iota(jnp.int32, (1, LANE), 1)
    vl = lax.broadcasted_iota(jnp.int32, (1, dvg), 1)
    kw, vw = LANE // sub, dvg // sub
    km = (kl >= u * kw) & (kl < (u + 1) * kw)
    vm = (vl >= u * vw) & (vl < (u + 1) * vw)
    return km.astype(F32), vm.astype(F32)


def _scan_chunk_fwd(q, k, la, incl, mid):
    b = _dot_f32(incl, la)
    row = lax.broadcasted_iota(jnp.int32, b.shape, 0)
    bm = jnp.sum(jnp.where(row == mid, b, 0.0), axis=0, keepdims=True)
    tot = jnp.sum(la, axis=0, keepdims=True)
    e_qc = jnp.exp(b - bm)
    e_kc = jnp.exp(bm - b)
    e_qe = jnp.exp(b)
    e_kd = jnp.exp(tot - b)
    return e_qc, e_kc, e_qe, e_kd


def scan_fwd(q_arr, k_arr, v_arr, la_arr, *, qcb, kcb, vcb, lacb, la_row, chunk, dvg, sub, reverse, inclusive,
             qscale, kscale, rope=None, name="scan_fwd"):
    t = q_arr.shape[0]
    r = _row_tile(t)
    nb, nc = t // r, r // chunk
    c = chunk
    rb = (lambda j: nb - 1 - j) if reverse else (lambda j: j)
    order = list(range(nc))[::-1] if reverse else list(range(nc))
    half = RET_DK // 2

    def body(*refs):
        if rope is None:
            q_ref, k_ref, v_ref, la_ref, o_ref, st_ref, s_s = refs
        else:
            q_ref, k_ref, v_ref, la_ref, c_ref, s1_ref, s2_ref, o_ref, st_ref, s_s = refs

        @pl.when(pl.program_id(1) == 0)
        def _():
            s_s[...] = jnp.zeros_like(s_s)

        incl, mask, mid = _scan_consts(c, reverse, inclusive)
        ones_cv = jnp.ones((c, dvg), F32)
        for ci in order:
            rows = slice(ci * c, (ci + 1) * c)
            qv = q_ref[rows, :] * qscale
            kv = k_ref[rows, :] * kscale
            if rope is not None:
                cs, a1, a2 = c_ref[rows, :], s1_ref[rows, :], s2_ref[rows, :]
                qv = _rope_apply(qv, cs, a1, a2, half)
                kv = _rope_apply(kv, cs, a1, a2, half)
            la = jnp.broadcast_to(la_ref[...], (c, LANE)) if la_row else la_ref[rows, :]
            vv = v_ref[rows, :]
            e_qc, e_kc, e_qe, e_kd = _scan_chunk_fwd(qv, kv, la, incl, mid)
            qc, kc, qe, kd = qv * e_qc, kv * e_kc, qv * e_qe, kv * e_kd
            sg = s_s[...]
            st_ref[ci] = sg
            acc = None
            for u in range(sub):
                mu, vmu = _sub_masks(sub, dvg, u)
                qcu = qc if mu is None else qc * mu
                qeu = qe if mu is None else qe * mu
                a = _dot_nt(qcu, kc) * mask
                ou = _dot(a, vv) + _dot(qeu, sg)
                ou = ou if vmu is None else ou * vmu
                acc = ou if acc is None else acc + ou
            o_ref[rows, :] = acc
            decay = jnp.exp(_dot_tn_f32(la, ones_cv))
            s_s[...] = decay * sg + _dot_tn(kd, vv)

    specs = [pl.BlockSpec((r, LANE), lambda g, j: (rb(j), qcb + g)),
             pl.BlockSpec((r, LANE), lambda g, j: (rb(j), kcb + g)),
             pl.BlockSpec((r, dvg), lambda g, j: (rb(j), vcb + g)),
             pl.BlockSpec((1, LANE), lambda g, j: (0, lacb + g)) if la_row
             else pl.BlockSpec((r, LANE), lambda g, j: (rb(j), lacb + g))]
    args = [q_arr, k_arr, v_arr, la_arr]
    if rope is not None:
        specs += [pl.BlockSpec((r, LANE), lambda g, j: (rb(j), 0))] * 3
        args += list(rope)
    return pl.pallas_call(
        body, name=name, grid=(N_GROUPS, nb), in_specs=specs,
        out_specs=[pl.BlockSpec((r, dvg), lambda g, j: (rb(j), g)),
                   pl.BlockSpec((nc, LANE, dvg), lambda g, j: (rb(j), g, 0))],
        out_shape=[jax.ShapeDtypeStruct((t, N_GROUPS * dvg), F32),
                   jax.ShapeDtypeStruct((t // c, N_GROUPS * LANE, dvg), F32)],
        scratch_shapes=[pltpu.VMEM((LANE, dvg), F32)],
        compiler_params=_cparams(("parallel", "arbitrary")),
    )(*args)


def scan_bwd(q_arr, k_arr, v_arr, la_arr, st_arr, do_arr, prev, *, qcb, kcb, vcb, lacb, la_row, chunk, dvg, sub,
             reverse, inclusive, qscale, kscale, rope=None, name="scan_bwd"):
    t = q_arr.shape[0]
    r = _row_tile(t)
    nb, nc = t // r, r // chunk
    c = chunk
    rb = (lambda j: j) if reverse else (lambda j: nb - 1 - j)
    order = list(range(nc)) if reverse else list(range(nc))[::-1]
    half = RET_DK // 2
    n_in = 6 + (3 if rope is not None else 0) + (3 if prev is not None else 0)

    def body(*refs):
        ins, outs = refs[:n_in], refs[n_in:]
        q_ref, k_ref, v_ref, la_ref, st_ref, do_ref = ins[:6]
        pos = 6
        if rope is not None:
            c_ref, s1_ref, s2_ref = ins[pos:pos + 3]
            pos += 3
        if prev is not None:
            pq_ref, pk_ref, pv_ref = ins[pos:pos + 3]
        dq_ref, dk_ref, dv_ref, dla_ref, g_s = outs

        @pl.when(pl.program_id(1) == 0)
        def _():
            g_s[...] = jnp.zeros_like(g_s)
            if la_row:
                dla_ref[...] = jnp.zeros_like(dla_ref)

        incl, mask, mid = _scan_consts(c, reverse, inclusive)
        ones_cv = jnp.ones((c, dvg), F32)
        for ci in order:
            rows = slice(ci * c, (ci + 1) * c)
            qv = q_ref[rows, :] * qscale
            kv = k_ref[rows, :] * kscale
            if rope is not None:
                cs, a1, a2 = c_ref[rows, :], s1_ref[rows, :], s2_ref[rows, :]
                qv = _rope_apply(qv, cs, a1, a2, half)
                kv = _rope_apply(kv, cs, a1, a2, half)
            la = jnp.broadcast_to(la_ref[...], (c, LANE)) if la_row else la_ref[rows, :]
            vv = v_ref[rows, :]
            dov = do_ref[rows, :]
            e_qc, e_kc, e_qe, e_kd = _scan_chunk_fwd(qv, kv, la, incl, mid)
            qc, kc, qe, kd = qv * e_qc, kv * e_kc, qv * e_qe, kv * e_kd
            sg = st_ref[ci]
            gn = g_s[...]
            dqc = jnp.zeros((c, LANE), F32)
            dkc = jnp.zeros((c, LANE), F32)
            dqe = jnp.zeros((c, LANE), F32)
            dvv = _dot(kd, gn)
            ds_direct = jnp.zeros((LANE, dvg), F32)
            for u in range(sub):
                mu, vmu = _sub_masks(sub, dvg, u)
                qcu = qc if mu is None else qc * mu
                qeu = qe if mu is None else qe * mu
                dou = dov if vmu is None else dov * vmu
                a = _dot_nt(qcu, kc) * mask
                da = _dot_nt(dou, vv) * mask
                dvv = dvv + _dot_tn(a, dou)
                t1 = _dot(da, kc)
                dqc = dqc + (t1 if mu is None else t1 * mu)
                dkc = dkc + _dot_tn(da, qcu)
                t2 = _dot_nt(dou, sg)
                dqe = dqe + (t2 if mu is None else t2 * mu)
                ds_direct = ds_direct + _dot_tn(qeu, dou)
            dkd = _dot_nt(vv, gn)
            decay = jnp.exp(_dot_tn_f32(la, ones_cv))
            gs = gn * sg * decay
            dtot = lax.dot_general(jnp.ones((SUBLANE, dvg), F32), gs, (((1,), (1,)), ((), ())),
                                   preferred_element_type=F32, precision=lax.Precision.HIGHEST)[0:1, :]
            dtot = dtot + jnp.sum(dkd * kd, axis=0, keepdims=True)
            db = dqc * qc - dkc * kc + dqe * qe - dkd * kd
            dla = _dot_tn_f32(incl, db) + dtot
            dqv = dqc * e_qc + dqe * e_qe
            dkv = dkc * e_kc + dkd * e_kd
            if rope is not None:
                dqv = _rope_transpose(dqv, cs, a1, a2, half)
                dkv = _rope_transpose(dkv, cs, a1, a2, half)
            dqv = dqv * qscale
            dkv = dkv * kscale
            if prev is not None:
                dqv = dqv + pq_ref[rows, :]
                dkv = dkv + pk_ref[rows, :]
                dvv = dvv + pv_ref[rows, :]
            dq_ref[rows, :] = dqv
            dk_ref[rows, :] = dkv
            dv_ref[rows, :] = dvv
            if la_row:
                dla_ref[...] += jnp.sum(dla, axis=0, keepdims=True)
            else:
                dla_ref[rows, :] = dla
            g_s[...] = ds_direct + decay * gn

    kblk = lambda cb: pl.BlockSpec((r, LANE), lambda g, j: (rb(j), cb + g))
    vblk = lambda cb: pl.BlockSpec((r, dvg), lambda g, j: (rb(j), cb + g))
    specs = [kblk(qcb), kblk(kcb), vblk(vcb),
             pl.BlockSpec((1, LANE), lambda g, j: (0, lacb + g)) if la_row else kblk(lacb),
             pl.BlockSpec((nc, LANE, dvg), lambda g, j: (rb(j), g, 0)), vblk(0)]
    args = [q_arr, k_arr, v_arr, la_arr, st_arr, do_arr]
    if rope is not None:
        specs += [pl.BlockSpec((r, LANE), lambda g, j: (rb(j), 0))] * 3
        args += list(rope)
    if prev is not None:
        specs += [kblk(0), kblk(0), vblk(0)]
        args += list(prev)
    wk = N_GROUPS * LANE
    outs = [jax.ShapeDtypeStruct((t, wk), F32), jax.ShapeDtypeStruct((t, wk), F32),
            jax.ShapeDtypeStruct((t, N_GROUPS * dvg), F32),
            jax.ShapeDtypeStruct((1, wk) if la_row else (t, wk), F32)]
    return pl.pallas_call(
        body, name=name, grid=(N_GROUPS, nb), in_specs=specs,
        out_specs=[kblk(0), kblk(0), vblk(0),
                   pl.BlockSpec((1, LANE), lambda g, j: (0, g)) if la_row else kblk(0)],
        out_shape=outs,
        scratch_shapes=[pltpu.VMEM((LANE, dvg), F32)],
        compiler_params=_cparams(("parallel", "arbitrary")),
    )(*args)


def _seg_mean(x, seg):
    w = x.shape[1]
    if seg % LANE == 0:
        parts = []
        for s in range(0, w, seg):
            m = jnp.mean(x[:, s:s + seg], axis=-1, keepdims=True)
            parts.append(jnp.broadcast_to(m, (x.shape[0], seg)))
        return jnp.concatenate(parts, axis=1)
    shift = seg.bit_length() - 1
    ii = lax.shift_right_logical(lax.broadcasted_iota(jnp.int32, (w, w), 0), shift)
    jj = lax.shift_right_logical(lax.broadcasted_iota(jnp.int32, (w, w), 1), shift)
    e = (ii == jj).astype(BF16)
    hi = x.astype(BF16)
    lo = (x - hi.astype(F32)).astype(BF16)
    return (jnp.dot(hi, e, preferred_element_type=F32) + jnp.dot(lo, e, preferred_element_type=F32)) * (1.0 / seg)


def gated_norm_fwd(o_f, o_b, gate_arr, gcb, gn, seg, name="gated_norm_fwd"):
    t, w = o_f.shape
    tm = _row_tile(t)

    def body(of_ref, ob_ref, g_ref, gn_ref, y_ref):
        o = of_ref[...] + ob_ref[...]
        r = lax.rsqrt(_seg_mean(o * o, seg) + EPS)
        gt = g_ref[...]
        y_ref[...] = (gt * _sigmoid(gt) * (o * r * gn_ref[...])).astype(BF16)

    bw = max(seg, LANE)
    row = pl.BlockSpec((tm, bw), lambda j, i: (i, j))
    return pl.pallas_call(
        body, name=name, grid=(w // bw, t // tm),
        in_specs=[row, row, pl.BlockSpec((tm, bw), lambda j, i: (i, gcb + j)),
                  pl.BlockSpec((1, bw), lambda j, i: (0, j))],
        out_specs=row, out_shape=jax.ShapeDtypeStruct((t, w), BF16),
        compiler_params=_cparams(("parallel", "parallel")),
    )(o_f, o_b, gate_arr, gn.reshape(1, w))


def gated_norm_bwd(o_f, o_b, gate_arr, gcb, gn, seg, dy, name="gated_norm_bwd"):
    t, w = o_f.shape
    tm = _row_tile(t)

    def body(of_ref, ob_ref, g_ref, gn_ref, dy_ref, do_ref, dg_ref, dgn_ref):
        @pl.when(pl.program_id(1) == 0)
        def _():
            dgn_ref[...] = jnp.zeros_like(dgn_ref)
        o = of_ref[...] + ob_ref[...]
        r = lax.rsqrt(_seg_mean(o * o, seg) + EPS)
        xh = o * r
        gt = g_ref[...]
        sg = _sigmoid(gt)
        dyv = dy_ref[...]
        n = xh * gn_ref[...]
        dg_ref[...] = dyv * n * (sg * (1.0 + gt * (1.0 - sg)))
        dn = dyv * (gt * sg)
        dgn_ref[...] += jnp.sum(dn * xh, axis=0, keepdims=True)
        dxh = dn * gn_ref[...]
        do_ref[...] = r * (dxh - xh * _seg_mean(dxh * xh, seg))

    bw = max(seg, LANE)
    row = pl.BlockSpec((tm, bw), lambda j, i: (i, j))
    vec = pl.BlockSpec((1, bw), lambda j, i: (0, j))
    return pl.pallas_call(
        body, name=name, grid=(w // bw, t // tm),
        in_specs=[row, row, pl.BlockSpec((tm, bw), lambda j, i: (i, gcb + j)), vec, row],
        out_specs=[row, row, vec],
        out_shape=[jax.ShapeDtypeStruct((t, w), F32), jax.ShapeDtypeStruct((t, w), F32),
                   jax.ShapeDtypeStruct((1, w), F32)],
        compiler_params=_cparams(("parallel", "arbitrary")),
    )(o_f, o_b, gate_arr, gn.reshape(1, w), dy)


def gla_gate_fwd(p, wg, bg, name="gla_gate_fwd"):
    t = p.shape[0]
    tm = _row_tile(t)
    w = wg.shape[1]
    gcb = OD_GA // LANE

    def body(ga_ref, wg_ref, bg_ref, la_ref):
        z = _dot(ga_ref[...], wg_ref[...]) + bg_ref[...]
        la_ref[...] = (jnp.minimum(z, 0.0) - jnp.log(1.0 + jnp.exp(-jnp.abs(z)))) * (1.0 / GLA_TAU)

    return pl.pallas_call(
        body, name=name, grid=(t // tm,),
        in_specs=[pl.BlockSpec((tm, LANE), lambda i: (i, gcb)), pl.BlockSpec((LANE, w), lambda i: (0, 0)),
                  pl.BlockSpec((1, w), lambda i: (0, 0))],
        out_specs=pl.BlockSpec((tm, w), lambda i: (i, 0)),
        out_shape=jax.ShapeDtypeStruct((t, w), F32),
        compiler_params=_cparams(("parallel",)),
    )(p, wg, bg)


def gla_gate_bwd(p, wg, wg_t, bg, dla, name="gla_gate_bwd"):
    t = p.shape[0]
    tm = _row_tile(t)
    w = wg.shape[1]
    gcb = OD_GA // LANE

    def body(ga_ref, wg_ref, wgt_ref, bg_ref, dla_ref, dga_ref, dwg_ref, dbg_ref):
        @pl.when(pl.program_id(0) == 0)
        def _():
            dwg_ref[...] = jnp.zeros_like(dwg_ref)
            dbg_ref[...] = jnp.zeros_like(dbg_ref)
        ga = ga_ref[...]
        z = _dot(ga, wg_ref[...]) + bg_ref[...]
        dz = dla_ref[...] * (1.0 / GLA_TAU) * _sigmoid(-z)
        dga_ref[...] = _dot(dz, wgt_ref[...])
        dwg_ref[...] += _dot_tn(ga, dz)
        dbg_ref[...] += jnp.sum(dz, axis=0, keepdims=True)

    return pl.pallas_call(
        body, name=name, grid=(t // tm,),
        in_specs=[pl.BlockSpec((tm, LANE), lambda i: (i, gcb)), pl.BlockSpec((LANE, w), lambda i: (0, 0)),
                  pl.BlockSpec((w, LANE), lambda i: (0, 0)), pl.BlockSpec((1, w), lambda i: (0, 0)),
                  pl.BlockSpec((tm, w), lambda i: (i, 0))],
        out_specs=[pl.BlockSpec((tm, LANE), lambda i: (i, 0)), pl.BlockSpec((LANE, w), lambda i: (0, 0)),
                   pl.BlockSpec((1, w), lambda i: (0, 0))],
        out_shape=[jax.ShapeDtypeStruct((t, LANE), F32), jax.ShapeDtypeStruct((LANE, w), F32),
                   jax.ShapeDtypeStruct((1, w), F32)],
        compiler_params=_cparams(("arbitrary",)),
    )(p, wg, wg_t, bg, dla)


FFN_COL = 1408


def _shifted(x, prev_row, next_row, first, last):
    tm = x.shape[0]
    row = lax.broadcasted_iota(jnp.int32, x.shape, 0)
    pr = jnp.where(first, 0.0, prev_row)
    nx = jnp.where(last, 0.0, next_row)
    xm1 = jnp.where(row == 0, pr, pltpu.roll(x, 1, 0))
    xp1 = jnp.where(row == tm - 1, nx, pltpu.roll(x, tm - 1, 0))
    return xm1, xp1


def _halo_specs(tm, tc, t, colmap, rowaxis):
    nb8 = tm // SUBLANE
    last8 = t // SUBLANE - 1

    def prev(*ids):
        i = ids[rowaxis]
        return (jnp.maximum(i * nb8 - 1, 0), colmap(*ids))

    def nxt(*ids):
        i = ids[rowaxis]
        return (jnp.minimum((i + 1) * nb8, last8), colmap(*ids))

    return pl.BlockSpec((SUBLANE, tc), prev), pl.BlockSpec((SUBLANE, tc), nxt)


def ffn_act_fwd(up, conv_w, conv_b, name="ffn_act_fwd"):
    t = up.shape[0]
    tm, tc = _row_tile(t), FFN_COL
    ncol = D_FF // tc

    def body(g_ref, gp_ref, gn_ref, v_ref, w_ref, b_ref, a_ref):
        i = pl.program_id(0)
        g = g_ref[...]
        gm1, gp1 = _shifted(g, gp_ref[SUBLANE - 1:SUBLANE, :], gn_ref[0:1, :], i == 0, i == pl.num_programs(0) - 1)
        cc = w_ref[0:1, :] * gm1 + w_ref[1:2, :] * g + w_ref[2:3, :] * gp1 + b_ref[...]
        a_ref[...] = (cc * _sigmoid(cc) * v_ref[...]).astype(BF16)

    prev, nxt = _halo_specs(tm, tc, t, lambda i, j: j, 0)
    return pl.pallas_call(
        body, name=name, grid=(t // tm, ncol),
        in_specs=[pl.BlockSpec((tm, tc), lambda i, j: (i, j)), prev, nxt,
                  pl.BlockSpec((tm, tc), lambda i, j: (i, j + ncol)),
                  pl.BlockSpec((SUBLANE, tc), lambda i, j: (0, j)), pl.BlockSpec((1, tc), lambda i, j: (0, j))],
        out_specs=pl.BlockSpec((tm, tc), lambda i, j: (i, j)),
        out_shape=jax.ShapeDtypeStruct((t, D_FF), BF16),
        compiler_params=_cparams(("parallel", "parallel")),
    )(up, up, up, up, conv_w, conv_b)


def ffn_act_bwd(up, conv_w, conv_b, dact, name="ffn_act_bwd"):
    t = up.shape[0]
    tm, tc = _row_tile(t), FFN_COL
    ncol = D_FF // tc

    def body(g_ref, gp_ref, gn_ref, v_ref, w_ref, b_ref, da_ref, dc_ref, dv_ref, dw_ref):
        i = pl.program_id(1)

        @pl.when(i == 0)
        def _():
            dw_ref[...] = jnp.zeros_like(dw_ref)
        g = g_ref[...]
        gm1, gp1 = _shifted(g, gp_ref[SUBLANE - 1:SUBLANE, :], gn_ref[0:1, :], i == 0, i == pl.num_programs(1) - 1)
        cc = w_ref[0:1, :] * gm1 + w_ref[1:2, :] * g + w_ref[2:3, :] * gp1 + b_ref[...]
        sg = _sigmoid(cc)
        da = da_ref[...]
        dv_ref[...] = da * (cc * sg)
        dc = da * v_ref[...] * (sg * (1.0 + cc * (1.0 - sg)))
        dc_ref[...] = dc
        dw_ref[0:1, :] += jnp.sum(dc * gm1, axis=0, keepdims=True)
        dw_ref[1:2, :] += jnp.sum(dc * g, axis=0, keepdims=True)
        dw_ref[2:3, :] += jnp.sum(dc * gp1, axis=0, keepdims=True)
        dw_ref[3:4, :] += jnp.sum(dc, axis=0, keepdims=True)

    prev, nxt = _halo_specs(tm, tc, t, lambda j, i: j, 1)
    tile = pl.BlockSpec((tm, tc), lambda j, i: (i, j))
    return pl.pallas_call(
        body, name=name, grid=(ncol, t // tm),
        in_specs=[tile, prev, nxt, pl.BlockSpec((tm, tc), lambda j, i: (i, j + ncol)),
                  pl.BlockSpec((SUBLANE, tc), lambda j, i: (0, j)), pl.BlockSpec((1, tc), lambda j, i: (0, j)), tile],
        out_specs=[tile, tile, pl.BlockSpec((SUBLANE, tc), lambda j, i: (0, j))],
        out_shape=[jax.ShapeDtypeStruct((t, D_FF), F32), jax.ShapeDtypeStruct((t, D_FF), F32),
                   jax.ShapeDtypeStruct((SUBLANE, D_FF), F32)],
        compiler_params=_cparams(("parallel", "arbitrary")),
    )(up, up, up, up, conv_w, conv_b, dact)


def conv_transpose(dc, conv_w, name="conv_transpose"):
    t = dc.shape[0]
    tm, tc = _row_tile(t), FFN_COL

    def body(d_ref, dp_ref, dn_ref, w_ref, o_ref):
        i = pl.program_id(0)
        d = d_ref[...]
        dm1, dp1 = _shifted(d, dp_ref[SUBLANE - 1:SUBLANE, :], dn_ref[0:1, :], i == 0, i == pl.num_programs(0) - 1)
        o_ref[...] = w_ref[0:1, :] * dp1 + w_ref[1:2, :] * d + w_ref[2:3, :] * dm1

    prev, nxt = _halo_specs(tm, tc, t, lambda i, j: j, 0)
    tile = pl.BlockSpec((tm, tc), lambda i, j: (i, j))
    return pl.pallas_call(
        body, name=name, grid=(t // tm, D_FF // tc),
        in_specs=[tile, prev, nxt, pl.BlockSpec((SUBLANE, tc), lambda i, j: (0, j))],
        out_specs=tile, out_shape=jax.ShapeDtypeStruct((t, D_FF), F32),
        compiler_params=_cparams(("parallel", "parallel")),
    )(dc, dc, dc, conv_w)


def loss_head(y, target, name="loss_head"):
    t, d = y.shape
    tm = _row_tile(t)

    def body(y_ref, t_ref, dy_ref, l_ref):
        @pl.when(pl.program_id(0) == 0)
        def _():
            l_ref[...] = jnp.zeros_like(l_ref)
        e = y_ref[...] - t_ref[...]
        dy_ref[...] = e * (1.0 / d)
        rowloss = jnp.sum(e * e, axis=-1, keepdims=True) * (0.5 / d)
        l_ref[...] += jnp.sum(rowloss, axis=0, keepdims=True)

    row = pl.BlockSpec((tm, d), lambda i: (i, 0))
    return pl.pallas_call(
        body, name=name, grid=(t // tm,), in_specs=[row, row],
        out_specs=[row, pl.BlockSpec((1, 1), lambda i: (0, 0))],
        out_shape=[jax.ShapeDtypeStruct((t, d), F32), jax.ShapeDtypeStruct((1, 1), F32)],
        compiler_params=_cparams(("arbitrary",)),
    )(y, target)


def _pad_heads(w, heads, width):
    lead = w.shape[:-1]
    w = w.reshape(*lead, heads, width)
    w = jnp.pad(w, [(0, 0)] * len(lead) + [(0, 0), (0, LANE - width)])
    return w.reshape(*lead, heads * LANE)


def _unpad_heads(w, heads, width):
    lead = w.shape[:-1]
    return w.reshape(*lead, heads, LANE)[..., :width].reshape(*lead, heads * width)


def _pad_rows_heads(w, heads, width):
    return _pad_heads(w.T, heads, width).T


def _unpad_rows_heads(w, heads, width):
    return _unpad_heads(w.T, heads, width).T


_EV_REAL = MLA_Q_RANK + MLA_KV_RANK + MLA_ROPE


def prep_even(wts, dt=BF16):
    w_in_t = wts["w_in_t"]
    w_in_tp = jnp.concatenate([w_in_t[:_EV_REAL], jnp.zeros((EV_RQ - _EV_REAL, D_MODEL), w_in_t.dtype),
                               w_in_t[_EV_REAL:]], axis=0).astype(dt)
    wuq = _pad_heads(wts["w_uq"], MLA_HEADS, MLA_QK).astype(dt)
    ukv = wts["w_ukv"].reshape(MLA_KV_RANK, MLA_HEADS, MLA_NOPE + MLA_V)
    wk = _pad_heads(ukv[..., :MLA_NOPE].reshape(MLA_KV_RANK, -1), MLA_HEADS, MLA_NOPE).astype(dt)
    wv = _pad_heads(ukv[..., MLA_NOPE:].reshape(MLA_KV_RANK, -1), MLA_HEADS, MLA_V).astype(dt)
    w_out = wts["w_out"]
    wa = _pad_rows_heads(w_out[:MLA_HEADS * MLA_V], MLA_HEADS, MLA_V).astype(dt)
    wr = w_out[MLA_HEADS * MLA_V:].astype(dt)
    pad1 = lambda v, n: jnp.pad(v.astype(F32), (0, n - v.shape[0])).reshape(1, n)
    lg = lambda th: jnp.log1p(-jnp.exp2(-th.astype(F32)))
    return dict(
        w_in=w_in_tp.T, w_in_t=w_in_tp, wuq=wuq, wuq_t=wuq.T, wk=wk, wk_t=wk.T, wv=wv, wv_t=wv.T,
        wa=wa, wa_t=wa.T, wr=wr, wr_t=wr.T,
        mix_norm=wts["mix_norm"].astype(F32), q_norm=wts["q_norm"].astype(F32).reshape(1, -1),
        kv_norm=wts["kv_norm"].astype(F32).reshape(1, -1),
        qhn=pad1(wts["q_head_norm"], LANE), khn=pad1(wts["k_head_norm"], LANE),
        la_f=jnp.repeat(lg(wts["theta_fwd"]), RET_DK).reshape(1, -1),
        la_b=jnp.repeat(lg(wts["theta_bwd"]), RET_DK).reshape(1, -1),
        out_norm=wts["ret_out_norm"].astype(F32).reshape(-1),
    )


def prep_odd(wts, dt=BF16):
    w_in_t = wts["w_in_t"]
    w_in_tp = jnp.concatenate([w_in_t, jnp.zeros((OD_IN - w_in_t.shape[0], D_MODEL), w_in_t.dtype)],
                              axis=0).astype(dt)
    hk = GLA_HEADS * GLA_DK
    wg = jnp.zeros((LANE, 2 * hk), F32)
    wg = wg.at[:GLA_GATE_RANK, :hk].set(wts["w_gate_fwd"].astype(F32))
    wg = wg.at[GLA_GATE_RANK:2 * GLA_GATE_RANK, hk:].set(wts["w_gate_bwd"].astype(F32))
    wg = wg.astype(dt)
    bg = jnp.concatenate([wts["b_gate_fwd"], wts["b_gate_bwd"]]).astype(F32).reshape(1, -1)
    w_out = wts["w_out"].astype(dt)
    return dict(w_in=w_in_tp.T, w_in_t=w_in_tp, wg=wg, wg_t=wg.T, bg=bg, w_out=w_out, w_out_t=w_out.T,
                mix_norm=wts["mix_norm"].astype(F32), out_norm=wts["gla_out_norm"].astype(F32).reshape(-1))


def prep_ffn(wts, dt=BF16):
    w_up_t = wts["w_up_t"].astype(dt)
    w_down = wts["w_down"].astype(dt)
    cw = jnp.pad(wts["conv_w"].astype(F32), ((0, SUBLANE - 3), (0, 0)))
    return dict(w_up=w_up_t.T, w_up_t=w_up_t, w_down=w_down, w_down_t=w_down.T, conv_w=cw,
                conv_b=wts["conv_b"].astype(F32).reshape(1, -1), norm=wts["norm"].astype(F32))


_RET = dict(qcb=EV_RQ // LANE, kcb=EV_RK // LANE, vcb=EV_RV // LANE, la_row=True, chunk=RET_CHUNK, dvg=LANE,
            sub=2, qscale=1.0, kscale=RET_DK ** -0.5)
_GLA = dict(qcb=OD_Q // LANE, kcb=OD_K // LANE, vcb=OD_V // GLA_DV, la_row=False, chunk=GLA_CHUNK, dvg=GLA_DV,
            sub=1, qscale=GLA_DK ** -0.5, kscale=1.0)
_FWD_DIR = dict(reverse=False, inclusive=True)
_BWD_DIR = dict(reverse=True, inclusive=False)


def even_fwd(x, w, tabs_mla, tabs_ret, tag):
    h = rmsnorm_fwd(x, w["mix_norm"], name=f"{tag}_norm")
    p = mm_nn(h, w["w_in"], name=f"{tag}_in")
    q, k, v = mla_prep_fwd(p, tabs_mla, w["q_norm"], w["kv_norm"], w["wuq"], w["wk"], w["wv"], w["qhn"], w["khn"],
                           name=f"{tag}_mla_prep")
    o, lse = attn_fwd(q, k, v, name=f"{tag}_attn")
    of, stf = scan_fwd(p, p, p, w["la_f"], lacb=0, rope=tabs_ret, name=f"{tag}_ret_f", **_RET, **_FWD_DIR)
    ob, stb = scan_fwd(p, p, p, w["la_b"], lacb=0, rope=tabs_ret, name=f"{tag}_ret_b", **_RET, **_BWD_DIR)
    r = gated_norm_fwd(of, ob, p, EV_RG // LANE, w["out_norm"], RET_DV, name=f"{tag}_ret_out")
    x1 = mm_nn(o, w["wa"], res=x, name=f"{tag}_out_a")
    x2 = mm_nn(r, w["wr"], res=x1, name=f"{tag}_out_r")
    return x2, dict(x=x, h=h, p=p, q=q, k=k, v=v, o=o, lse=lse, of=of, ob=ob, stf=stf, stb=stb, r=r)


def even_bwd(dx, s, w, tabs_mla, tabs_ret, tag):
    tag = tag + "_b"
    do = mm_nn(dx, w["wa_t"], name=f"{tag}_dout_a")
    dr = mm_nn(dx, w["wr_t"], name=f"{tag}_dout_r")
    d_wa = mm_tn(s["o"], dx, name=f"{tag}_dwa")
    d_wr = mm_tn(s["r"], dx, name=f"{tag}_dwr")
    dq, dk, dv = attn_bwd(s["q"], s["k"], s["v"], s["o"], s["lse"], do, name=f"{tag}_attn")
    (dp_mla, d_wuq, d_wk, d_wv, d_qn, d_kvn, d_qhn, d_khn) = mla_prep_bwd(
        s["p"], tabs_mla, w["q_norm"], w["kv_norm"], w["wuq"], w["wk"], w["wv"], w["wuq_t"], w["wk_t"], w["wv_t"],
        w["qhn"], w["khn"], dq, dk, dv, name=f"{tag}_mla_prep")
    d_o, d_gate, d_gn = gated_norm_bwd(s["of"], s["ob"], s["p"], EV_RG // LANE, w["out_norm"], RET_DV, dr,
                                       name=f"{tag}_ret_out")
    p = s["p"]
    g1 = scan_bwd(p, p, p, w["la_f"], s["stf"], d_o, None, lacb=0, rope=tabs_ret, name=f"{tag}_ret_f",
                  **_RET, **_FWD_DIR)
    g2 = scan_bwd(p, p, p, w["la_b"], s["stb"], d_o, g1[:3], lacb=0, rope=tabs_ret, name=f"{tag}_ret_b",
                  **_RET, **_BWD_DIR)
    dp = jnp.concatenate([dp_mla, g2[0], g2[1], g2[2], d_gate], axis=1)
    dh = mm_nn(dp, w["w_in_t"], name=f"{tag}_dh")
    d_win_t = mm_tn(dp, s["h"], name=f"{tag}_dwin")
    dx_in, d_mix = rmsnorm_bwd(s["x"], w["mix_norm"], dh, dx, name=f"{tag}_norm")
    grads = dict(w_in_t=d_win_t, wuq=d_wuq, wk=d_wk, wv=d_wv, wa=d_wa, wr=d_wr, mix_norm=d_mix, q_norm=d_qn,
                 kv_norm=d_kvn, qhn=d_qhn, khn=d_khn, la_f=g1[3], la_b=g2[3], out_norm=d_gn)
    return dx_in, grads


def odd_fwd(x, w, tag):
    h = rmsnorm_fwd(x, w["mix_norm"], name=f"{tag}_norm")
    p = mm_nn(h, w["w_in"], name=f"{tag}_in")
    la = gla_gate_fwd(p, w["wg"], w["bg"], name=f"{tag}_gate")
    of, stf = scan_fwd(p, p, p, la, lacb=0, name=f"{tag}_gla_f", **_GLA, **_FWD_DIR)
    ob, stb = scan_fwd(p, p, p, la, lacb=N_GROUPS, name=f"{tag}_gla_b", **_GLA, **_BWD_DIR)
    y = gated_norm_fwd(of, ob, p, OD_R // GLA_DV, w["out_norm"], GLA_DV, name=f"{tag}_gla_out")
    x1 = mm_nn(y, w["w_out"], res=x, name=f"{tag}_out")
    return x1, dict(x=x, h=h, p=p, la=la, of=of, ob=ob, stf=stf, stb=stb, y=y)


def odd_bwd(dx, s, w, tag):
    tag = tag + "_b"
    dy = mm_nn(dx, w["w_out_t"], name=f"{tag}_dout")
    d_wout = mm_tn(s["y"], dx, name=f"{tag}_dwout")
    d_o, d_gate, d_gn = gated_norm_bwd(s["of"], s["ob"], s["p"], OD_R // GLA_DV, w["out_norm"], GLA_DV, dy,
                                       name=f"{tag}_gla_out")
    p, la = s["p"], s["la"]
    g1 = scan_bwd(p, p, p, la, s["stf"], d_o, None, lacb=0, name=f"{tag}_gla_f", **_GLA, **_FWD_DIR)
    g2 = scan_bwd(p, p, p, la, s["stb"], d_o, g1[:3], lacb=N_GROUPS, name=f"{tag}_gla_b", **_GLA, **_BWD_DIR)
    dla = jnp.concatenate([g1[3], g2[3]], axis=1)
    d_ga, d_wg, d_bg = gla_gate_bwd(p, w["wg"], w["wg_t"], w["bg"], dla, name=f"{tag}_gate")
    dp = jnp.concatenate([g2[0], g2[1], g2[2], d_gate, d_ga], axis=1)
    dh = mm_nn(dp, w["w_in_t"], name=f"{tag}_dh")
    d_win_t = mm_tn(dp, s["h"], name=f"{tag}_dwin")
    dx_in, d_mix = rmsnorm_bwd(s["x"], w["mix_norm"], dh, dx, name=f"{tag}_norm")
    grads = dict(w_in_t=d_win_t, wg=d_wg, bg=d_bg, w_out=d_wout, mix_norm=d_mix, out_norm=d_gn)
    return dx_in, grads


def ffn_fwd(x, w, tag):
    h = rmsnorm_fwd(x, w["norm"], name=f"{tag}_norm")
    up = mm_nn(h, w["w_up"], name=f"{tag}_up")
    act = ffn_act_fwd(up, w["conv_w"], w["conv_b"], name=f"{tag}_act")
    x1 = mm_nn(act, w["w_down"], res=x, name=f"{tag}_down")
    return x1, dict(x=x, h=h, up=up, act=act)


def ffn_bwd(dx, s, w, tag):
    tag = tag + "_b"
    dact = mm_nn(dx, w["w_down_t"], name=f"{tag}_dact")
    d_wdown = mm_tn(s["act"], dx, name=f"{tag}_dwdown")
    dc, dval, d_conv = ffn_act_bwd(s["up"], w["conv_w"], w["conv_b"], dact, name=f"{tag}_act")
    dgate = conv_transpose(dc, w["conv_w"], name=f"{tag}_convt")
    dh1 = mm_nn(dgate, w["w_up_t"][:D_FF], name=f"{tag}_dh_g")
    dh = mm_nn(dval, w["w_up_t"][D_FF:], res=dh1, name=f"{tag}_dh_v")
    d_wup_t = jnp.concatenate([mm_tn(dgate, s["h"], name=f"{tag}_dwup_g"),
                               mm_tn(dval, s["h"], name=f"{tag}_dwup_v")], axis=0)
    dx_in, d_norm = rmsnorm_bwd(s["x"], w["norm"], dh, dx, name=f"{tag}_norm")
    grads = dict(w_up_t=d_wup_t, w_down=d_wdown, conv_w=d_conv[:3], conv_b=d_conv[3], norm=d_norm)
    return dx_in, grads


def unprep_even_grads(g, theta_fwd, theta_bwd):
    d_win_t = jnp.concatenate([g["w_in_t"][:_EV_REAL], g["w_in_t"][EV_RQ:]], axis=0)
    d_uq = _unpad_heads(g["wuq"], MLA_HEADS, MLA_QK)
    dk_ = _unpad_heads(g["wk"], MLA_HEADS, MLA_NOPE).reshape(MLA_KV_RANK, MLA_HEADS, MLA_NOPE)
    dv_ = _unpad_heads(g["wv"], MLA_HEADS, MLA_V).reshape(MLA_KV_RANK, MLA_HEADS, MLA_V)
    d_ukv = jnp.concatenate([dk_, dv_], axis=-1).reshape(MLA_KV_RANK, -1)
    d_wout = jnp.concatenate([_unpad_rows_heads(g["wa"], MLA_HEADS, MLA_V), g["wr"]], axis=0)

    def dtheta(dla, th):
        dlg = dla.reshape(RET_HEADS, RET_DK).sum(axis=-1)
        e = jnp.exp2(-th.astype(F32))
        return dlg * (e * math.log(2.0)) / (1.0 - e)

    return dict(mix_norm=g["mix_norm"].reshape(-1), w_in_t=d_win_t, q_norm=g["q_norm"].reshape(-1),
                kv_norm=g["kv_norm"].reshape(-1), w_uq=d_uq, w_ukv=d_ukv, q_head_norm=g["qhn"].reshape(-1)[:MLA_QK],
                k_head_norm=g["khn"].reshape(-1)[:MLA_QK], theta_fwd=dtheta(g["la_f"], theta_fwd),
                theta_bwd=dtheta(g["la_b"], theta_bwd), ret_out_norm=g["out_norm"].reshape(RET_HEADS, RET_DV),
                w_out=d_wout)


def unprep_odd_grads(g):
    hk = GLA_HEADS * GLA_DK
    return dict(mix_norm=g["mix_norm"].reshape(-1), w_in_t=g["w_in_t"][:OD_GA + 2 * GLA_GATE_RANK],
                w_gate_fwd=g["wg"][:GLA_GATE_RANK, :hk], b_gate_fwd=g["bg"].reshape(-1)[:hk],
                w_gate_bwd=g["wg"][GLA_GATE_RANK:2 * GLA_GATE_RANK, hk:], b_gate_bwd=g["bg"].reshape(-1)[hk:],
                gla_out_norm=g["out_norm"].reshape(GLA_HEADS, GLA_DV), w_out=g["w_out"])


def unprep_ffn_grads(g):
    return dict(norm=g["norm"].reshape(-1), w_up_t=g["w_up_t"], conv_w=g["conv_w"], conv_b=g["conv_b"],
                w_down=g["w_down"])


def _mesh_pos():
    return tuple(lax.axis_index(n) for n in MESH_AXES)


def _slot(px, py, pc):
    return 4 * px + 2 * py + pc


def all_gather_blocks(blk, name="weight_all_gather"):
    r, w = blk.shape

    def body(x_ref, out_ref, send_sems, recv_sems, local_sem):
        x, y, c = _mesh_pos()
        me, sibling = (x, y, c), (x, y, 1 - c)
        chips = [(1 - x, y), (x, 1 - y), (1 - x, 1 - y)]

        def copy(k, block, to, src=None):
            dst = out_ref.at[_slot(*block)]
            return pltpu.make_async_remote_copy(
                src_ref=dst if src is None else src, dst_ref=dst, send_sem=send_sems.at[k],
                recv_sem=recv_sems.at[k], device_id=to, device_id_type=pl.DeviceIdType.MESH)

        mine = pltpu.make_async_copy(x_ref, out_ref.at[_slot(*me)], local_sem)
        mine.start()
        first = [copy(0, me, sibling, src=x_ref)]
        first += [copy(1 + j, me, (*chip, c), src=x_ref) for j, chip in enumerate(chips)]
        for cp in first:
            cp.start()
        passed = [copy(4 + j, (*chip, c), sibling) for j, chip in enumerate(chips)]
        for j, chip in enumerate(chips):
            copy(1 + j, (*chip, c), me).wait_recv()
            passed[j].start()
        copy(0, sibling, me).wait_recv()
        for j, chip in enumerate(chips):
            copy(4 + j, (*chip, 1 - c), me).wait_recv()
        for cp in first + passed:
            cp.wait_send()
        mine.wait()

    return pl.pallas_call(
        body, name=name,
        out_shape=jax.ShapeDtypeStruct((N_DEV, r, w), blk.dtype),
        in_specs=[pl.BlockSpec(memory_space=pl.ANY)],
        out_specs=pl.BlockSpec(memory_space=pl.ANY),
        scratch_shapes=[pltpu.SemaphoreType.DMA((7,)), pltpu.SemaphoreType.DMA((7,)), pltpu.SemaphoreType.DMA],
    )(blk)


def all_to_all_blocks(send, name="grad_all_to_all"):
    _, r, w = send.shape

    def body(s_ref, r_ref, send_sems, recv_sems, local_sem):
        x, y, c = _mesh_pos()
        me = _slot(x, y, c)
        mine = pltpu.make_async_copy(s_ref.at[me], r_ref.at[me], local_sem)
        mine.start()
        copies = []
        for k in range(1, N_DEV):
            px = 1 - x if (k >> 2) & 1 else x
            py = 1 - y if (k >> 1) & 1 else y
            pc = 1 - c if k & 1 else c
            cp = pltpu.make_async_remote_copy(
                src_ref=s_ref.at[_slot(px, py, pc)], dst_ref=r_ref.at[me], send_sem=send_sems.at[k - 1],
                recv_sem=recv_sems.at[k - 1], device_id=(px, py, pc), device_id_type=pl.DeviceIdType.MESH)
            cp.start()
            copies.append(cp)
        for cp in copies:
            cp.wait()
        mine.wait()

    return pl.pallas_call(
        body, name=name,
        out_shape=jax.ShapeDtypeStruct((N_DEV, r, w), send.dtype),
        in_specs=[pl.BlockSpec(memory_space=pl.ANY)],
        out_specs=pl.BlockSpec(memory_space=pl.ANY),
        scratch_shapes=[pltpu.SemaphoreType.DMA((7,)), pltpu.SemaphoreType.DMA((7,)), pltpu.SemaphoreType.DMA],
    )(send)


FLAT_W = 1024
FLAT_TILE = 256


def sum_slots(recv, name="grad_sum"):
    _, r, w = recv.shape

    def body(r_ref, o_ref):
        acc = r_ref[0]
        for k in range(1, N_DEV):
            acc = acc + r_ref[k]
        o_ref[...] = acc

    tr = _slab_tile(r)
    return pl.pallas_call(
        body, name=name, grid=(r // tr,),
        in_specs=[pl.BlockSpec((N_DEV, tr, w), lambda i: (0, i, 0))],
        out_specs=pl.BlockSpec((tr, w), lambda i: (i, 0)),
        out_shape=jax.ShapeDtypeStruct((r, w), F32),
        compiler_params=_cparams(("parallel",)),
    )(recv)


def _slab_tile(r):
    return max(t for t in range(SUBLANE, FLAT_TILE + 1, SUBLANE) if r % t == 0)


def adamw(wf, gf, mf, vf, name="adamw"):
    r, w = wf.shape
    tr = _slab_tile(r)

    def body(w_ref, g_ref, m_ref, v_ref, d_ref, m_out, v_out):
        g = g_ref[...]
        m = ADAM_B1 * m_ref[...] + (1.0 - ADAM_B1) * g
        v = ADAM_B2 * v_ref[...] + (1.0 - ADAM_B2) * (g * g)
        m_hat = m / (1.0 - ADAM_B1 ** ADAM_STEP)
        v_hat = v / (1.0 - ADAM_B2 ** ADAM_STEP)
        d_ref[...] = -ADAM_LR * (m_hat / (jnp.sqrt(v_hat) + ADAM_EPS) + ADAM_WD * w_ref[...])
        m_out[...] = m
        v_out[...] = v

    tile = pl.BlockSpec((tr, w), lambda i: (i, 0))
    return pl.pallas_call(
        body, name=name, grid=(r // tr,), in_specs=[tile] * 4, out_specs=[tile] * 3,
        out_shape=[jax.ShapeDtypeStruct((r, w), F32)] * 3,
        compiler_params=_cparams(("parallel",)),
    )(wf, gf, mf, vf)


ROW_FORM = [("w_in_even", "T"), ("w_out_even", "R"), ("w_in_odd", "T"), ("w_out_odd", "R"), ("ffn_w_up", "T"),
            ("ffn_w_down", "R")]
SHARDED_MID = [("mla_w_uq", 2), ("mla_w_ukv", 2)]
SHARDED_SMALL = [("mix_norm_odd", 1), ("gla_w_gate_fwd", 2), ("gla_b_gate_fwd", 1), ("gla_w_gate_bwd", 2),
                 ("gla_b_gate_bwd", 1), ("gla_out_norm", 2), ("ffn_conv_w", 2)]
REPLICATED = ["mix_norm_even", "mla_q_norm", "mla_kv_norm", "mla_q_head_norm", "mla_k_head_norm", "ret_theta_fwd",
              "ret_theta_bwd", "ret_out_norm", "ffn_norm", "ffn_conv_b"]
WEIGHT_NAMES = ["mix_norm_even", "w_in_even", "mla_q_norm", "mla_kv_norm", "mla_w_uq", "mla_w_ukv",
                "mla_q_head_norm", "mla_k_head_norm", "ret_theta_fwd", "ret_theta_bwd", "ret_out_norm", "w_out_even",
                "mix_norm_odd", "w_in_odd", "gla_w_gate_fwd", "gla_b_gate_fwd", "gla_w_gate_bwd", "gla_b_gate_bwd",
                "gla_out_norm", "w_out_odd", "ffn_norm", "ffn_w_up", "ffn_conv_w", "ffn_conv_b", "ffn_w_down"]


def _round_up(n, m):
    return -(-n // m) * m


def _pack_rows(parts, rows):
    flat = jnp.concatenate(parts, axis=-1)
    pad = rows * FLAT_W - flat.shape[-1]
    flat = jnp.pad(flat, [(0, 0)] * (flat.ndim - 1) + [(0, pad)])
    return flat.reshape(*flat.shape[:-1], rows, FLAT_W)


def _row_form(v, form):
    if form == "T":
        v = jnp.swapaxes(v, 1, 2)
    return v.reshape(-1, v.shape[-1])


def _row_counts(local_shapes):
    return [local_shapes[n][0] * local_shapes[n][2 if f == "T" else 1] for n, f in ROW_FORM]


def _tail_layout(local_shapes):
    n_sh = sum(math.prod(local_shapes[n]) for n, _ in SHARDED_MID + SHARDED_SMALL)
    n_rep = sum(math.prod(local_shapes[n]) for n in REPLICATED)
    sh_rows = _round_up(-(-n_sh // FLAT_W), SUBLANE)
    rep_rows = _round_up(-(-n_rep // FLAT_W), SUBLANE)
    return sh_rows, rep_rows, _round_up(sh_rows + rep_rows, FLAT_TILE)


def pack_tail(vals, local_shapes):
    sh_rows, rep_rows, rows = _tail_layout(local_shapes)
    sh = _pack_rows([vals[n].astype(F32).reshape(-1) for n, _ in SHARDED_MID + SHARDED_SMALL], sh_rows)
    rep = _pack_rows([vals[n].astype(F32).reshape(-1) for n in REPLICATED], rep_rows)
    return jnp.concatenate([sh, rep, jnp.zeros((rows - sh_rows - rep_rows, FLAT_W), F32)], axis=0)


def unpack_tail(tail, local_shapes):
    sh_rows, rep_rows, _ = _tail_layout(local_shapes)
    out = {}
    for names, flat in (([n for n, _ in SHARDED_MID + SHARDED_SMALL], tail[:sh_rows].reshape(-1)),
                        (REPLICATED, tail[sh_rows:sh_rows + rep_rows].reshape(-1))):
        off = 0
        for n in names:
            k = math.prod(local_shapes[n])
            out[n] = flat[off:off + k].reshape(local_shapes[n])
            off += k
    return out


def unpack_rows(slab, local_shapes):
    out = {}
    off = 0
    for (n, form), rows in zip(ROW_FORM, _row_counts(local_shapes)):
        loc = local_shapes[n]
        piece = slab[off:off + rows]
        if form == "T":
            piece = jnp.swapaxes(piece.reshape(loc[0], loc[2], loc[1]), 1, 2)
        out[n] = piece.reshape(loc)
        off += rows
    return out


def pack_grad_blocks(full_grads, local_shapes):
    sh_rows, rep_rows, rows = _tail_layout(local_shapes)
    blocks = []
    for n, form in ROW_FORM:
        g = full_grads[n].astype(F32)
        layers, total = g.shape[0], g.shape[1]
        g = g.reshape(layers, N_DEV, total // N_DEV, FLAT_W)
        blocks.append(jnp.swapaxes(g, 0, 1).reshape(N_DEV, -1, FLAT_W))
    parts = []
    for n, ax in SHARDED_MID + SHARDED_SMALL:
        g = full_grads[n].astype(F32)
        loc = local_shapes[n]
        g = g.reshape(*g.shape[:ax], N_DEV, loc[ax], *g.shape[ax + 1:])
        parts.append(jnp.moveaxis(g, ax, 0).reshape(N_DEV, -1))
    sh = _pack_rows(parts, sh_rows)
    rep = _pack_rows([full_grads[n].astype(F32).reshape(-1) for n in REPLICATED], rep_rows)
    rep = jnp.broadcast_to(rep[None], (N_DEV, rep_rows, FLAT_W))
    pad = jnp.zeros((N_DEV, rows - sh_rows - rep_rows, FLAT_W), F32)
    return jnp.concatenate(blocks + [sh, rep, pad], axis=1)


def pack_gather_block(vals, local_shapes):
    big = jnp.concatenate([_row_form(vals[n].astype(BF16), f) for n, f in ROW_FORM], axis=0)
    mid = [vals[n].astype(BF16).reshape(-1) for n, _ in SHARDED_MID]
    small = jnp.concatenate([vals[n].astype(F32).reshape(-1) for n, _ in SHARDED_SMALL])
    small = lax.bitcast_convert_type(small, BF16).reshape(-1)
    n = sum(v.shape[0] for v in mid) + small.shape[0]
    tail = _pack_rows(mid + [small], _round_up(-(-n // FLAT_W), 2 * SUBLANE))
    return jnp.concatenate([big, tail], axis=0)


def unpack_gathered(gathered, local_shapes):
    out = {}
    off = 0
    for (n, form), rows in zip(ROW_FORM, _row_counts(local_shapes)):
        layers = local_shapes[n][0]
        piece = gathered[:, off:off + rows].reshape(N_DEV, layers, rows // layers, FLAT_W)
        out[n] = jnp.swapaxes(piece, 0, 1).reshape(layers, N_DEV * (rows // layers), FLAT_W)
        off += rows
    flat = gathered[:, off:].reshape(N_DEV, -1)
    off = 0

    def full(piece, n, ax):
        loc = local_shapes[n]
        piece = jnp.moveaxis(piece.reshape(N_DEV, *loc), 0, ax)
        return piece.reshape(*loc[:ax], N_DEV * loc[ax], *loc[ax + 1:])

    for n, ax in SHARDED_MID:
        k = math.prod(local_shapes[n])
        out[n] = full(flat[:, off:off + k], n, ax)
        off += k
    for n, ax in SHARDED_SMALL:
        k = math.prod(local_shapes[n])
        pairs = flat[:, off:off + 2 * k].reshape(N_DEV, k, 2)
        out[n] = full(lax.bitcast_convert_type(pairs, F32), n, ax)
        off += 2 * k
    return out


def local_step(x, positions, target, fw):
    tabs_mla = rope_tables(positions, MLA_NOPE, MLA_ROPE // 2, LANE)
    tabs_ret = rope_tables(positions, 0, RET_DK // 2, RET_DK)
    layers = []
    for layer in range(DEPTH):
        i = layer // 2
        if layer % 2 == 0:
            mw = prep_even(dict(
                w_in_t=fw["w_in_even"][i], w_uq=fw["mla_w_uq"][i], w_ukv=fw["mla_w_ukv"][i], w_out=fw["w_out_even"][i],
                mix_norm=fw["mix_norm_even"][i], q_norm=fw["mla_q_norm"][i], kv_norm=fw["mla_kv_norm"][i],
                q_head_norm=fw["mla_q_head_norm"][i], k_head_norm=fw["mla_k_head_norm"][i],
                theta_fwd=fw["ret_theta_fwd"][i], theta_bwd=fw["ret_theta_bwd"][i],
                ret_out_norm=fw["ret_out_norm"][i]))
        else:
            mw = prep_odd(dict(
                w_in_t=fw["w_in_odd"][i], w_gate_fwd=fw["gla_w_gate_fwd"][i], b_gate_fwd=fw["gla_b_gate_fwd"][i],
                w_gate_bwd=fw["gla_w_gate_bwd"][i], b_gate_bwd=fw["gla_b_gate_bwd"][i],
                gla_out_norm=fw["gla_out_norm"][i], w_out=fw["w_out_odd"][i], mix_norm=fw["mix_norm_odd"][i]))
        fwt = prep_ffn(dict(norm=fw["ffn_norm"][layer], w_up_t=fw["ffn_w_up"][layer], conv_w=fw["ffn_conv_w"][layer],
                            conv_b=fw["ffn_conv_b"][layer], w_down=fw["ffn_w_down"][layer]))
        layers.append((mw, fwt))

    saved = []
    for layer, (mw, fwt) in enumerate(layers):
        if layer % 2 == 0:
            x, sm = even_fwd(x, mw, tabs_mla, tabs_ret, f"l{layer}_mix")
        else:
            x, sm = odd_fwd(x, mw, f"l{layer}_mix")
        x, sf = ffn_fwd(x, fwt, f"l{layer}_ffn")
        saved.append((sm, sf))

    dx, loss = loss_head(x, target, name="loss_head")

    per_layer = [None] * DEPTH
    for layer in reversed(range(DEPTH)):
        mw, fwt = layers[layer]
        sm, sf = saved[layer]
        dx, gf = ffn_bwd(dx, sf, fwt, f"l{layer}_ffn")
        if layer % 2 == 0:
            dx, gm = even_bwd(dx, sm, mw, tabs_mla, tabs_ret, f"l{layer}_mix")
            gm = unprep_even_grads(gm, fw["ret_theta_fwd"][layer // 2], fw["ret_theta_bwd"][layer // 2])
        else:
            dx, gm = odd_bwd(dx, sm, mw, f"l{layer}_mix")
            gm = unprep_odd_grads(gm)
        per_layer[layer] = (gm, unprep_ffn_grads(gf))

    ev = [per_layer[l][0] for l in range(0, DEPTH, 2)]
    od = [per_layer[l][0] for l in range(1, DEPTH, 2)]
    ff = [per_layer[l][1] for l in range(DEPTH)]
    st = lambda lst, key: jnp.stack([g[key] for g in lst])
    grads = {
        "mix_norm_even": st(ev, "mix_norm"), "w_in_even": st(ev, "w_in_t"), "mla_q_norm": st(ev, "q_norm"),
        "mla_kv_norm": st(ev, "kv_norm"), "mla_w_uq": st(ev, "w_uq"), "mla_w_ukv": st(ev, "w_ukv"),
        "mla_q_head_norm": st(ev, "q_head_norm"), "mla_k_head_norm": st(ev, "k_head_norm"),
        "ret_theta_fwd": st(ev, "theta_fwd"), "ret_theta_bwd": st(ev, "theta_bwd"),
        "ret_out_norm": st(ev, "ret_out_norm"), "w_out_even": st(ev, "w_out"),
        "mix_norm_odd": st(od, "mix_norm"), "w_in_odd": st(od, "w_in_t"), "gla_w_gate_fwd": st(od, "w_gate_fwd"),
        "gla_b_gate_fwd": st(od, "b_gate_fwd"), "gla_w_gate_bwd": st(od, "w_gate_bwd"),
        "gla_b_gate_bwd": st(od, "b_gate_bwd"), "gla_out_norm": st(od, "gla_out_norm"), "w_out_odd": st(od, "w_out"),
        "ffn_norm": st(ff, "norm"), "ffn_w_up": st(ff, "w_up_t"), "ffn_conv_w": st(ff, "conv_w"),
        "ffn_conv_b": st(ff, "conv_b"), "ffn_w_down": st(ff, "w_down"),
    }
    return loss, dx, grads


def kernel(x, positions, mix_norm_even, w_in_even, mla_q_norm, mla_kv_norm, mla_w_uq, mla_w_ukv, mla_q_head_norm, mla_k_head_norm, ret_theta_fwd, ret_theta_bwd, ret_out_norm, w_out_even, mix_norm_odd, w_in_odd, gla_w_gate_fwd, gla_b_gate_fwd, gla_w_gate_bwd, gla_b_gate_bwd, gla_out_norm, w_out_odd, ffn_norm, ffn_w_up, ffn_conv_w, ffn_conv_b, ffn_w_down, loss_target, m_mix_norm_even, m_w_in_even, m_mla_q_norm, m_mla_kv_norm, m_mla_w_uq, m_mla_w_ukv, m_mla_q_head_norm, m_mla_k_head_norm, m_ret_theta_fwd, m_ret_theta_bwd, m_ret_out_norm, m_w_out_even, m_mix_norm_odd, m_w_in_odd, m_gla_w_gate_fwd, m_gla_b_gate_fwd, m_gla_w_gate_bwd, m_gla_b_gate_bwd, m_gla_out_norm, m_w_out_odd, m_ffn_norm, m_ffn_w_up, m_ffn_conv_w, m_ffn_conv_b, m_ffn_w_down, v_mix_norm_even, v_w_in_even, v_mla_q_norm, v_mla_kv_norm, v_mla_w_uq, v_mla_w_ukv, v_mla_q_head_norm, v_mla_k_head_norm, v_ret_theta_fwd, v_ret_theta_bwd, v_ret_out_norm, v_w_out_even, v_mix_norm_odd, v_w_in_odd, v_gla_w_gate_fwd, v_gla_b_gate_fwd, v_gla_w_gate_bwd, v_gla_b_gate_bwd, v_gla_out_norm, v_w_out_odd, v_ffn_norm, v_ffn_w_up, v_ffn_conv_w, v_ffn_conv_b, v_ffn_w_down):
    a = dict(locals())
    wts = {n: a[n] for n in WEIGHT_NAMES}
    local_shapes = {n: tuple(wts[n].shape) for n in WEIGHT_NAMES}

    gathered = all_gather_blocks(pack_gather_block(wts, local_shapes))
    fw = unpack_gathered(gathered, local_shapes)
    for n in REPLICATED:
        fw[n] = wts[n]

    loss, grad_x, grads = local_step(x[0], positions, loss_target[0], fw)

    recv = all_to_all_blocks(pack_grad_blocks(grads, local_shapes))
    g_slab = sum_slots(recv)
    n_big = sum(_row_counts(local_shapes))
    ms = {n: a["m_" + n] for n in WEIGHT_NAMES}
    vs = {n: a["v_" + n] for n in WEIGHT_NAMES}

    g_out = unpack_rows(g_slab[:n_big], local_shapes)
    d_out, m_out, v_out = {}, {}, {}
    for n, _ in ROW_FORM:
        loc = local_shapes[n]
        two_d = lambda t: t.reshape(-1, loc[-1])
        d, m, v = adamw(two_d(wts[n]), two_d(g_out[n]), two_d(ms[n]), two_d(vs[n]), name=f"adamw_{n}")
        d_out[n], m_out[n], v_out[n] = d.reshape(loc), m.reshape(loc), v.reshape(loc)
    g_tail = g_slab[n_big:]
    d_tail, m_tail, v_tail = adamw(pack_tail(wts, local_shapes), g_tail, pack_tail(ms, local_shapes),
                                   pack_tail(vs, local_shapes), name="adamw_small")
    g_out.update(unpack_tail(g_tail, local_shapes))
    d_out.update(unpack_tail(d_tail, local_shapes))
    m_out.update(unpack_tail(m_tail, local_shapes))
    v_out.update(unpack_tail(v_tail, local_shapes))
    total = lax.psum(loss[0, 0], MESH_AXES)
    return (total, grad_x[None], *[g_out[n] for n in WEIGHT_NAMES], *[d_out[n] for n in WEIGHT_NAMES],
            *[m_out[n] for n in WEIGHT_NAMES], *[v_out[n] for n in WEIGHT_NAMES])
```

```python
import math

import jax
import jax.numpy as jnp
from jax import lax
from jax.experimental import pallas as pl
from jax.experimental.pallas import tpu as pltpu

F32 = jnp.float32
BF16 = jnp.bfloat16

D_MODEL = 1024
DEPTH = 4
N_DEV = 8
MESH_AXES = ("x", "y", "c")

MLA_HEADS = 8
MLA_Q_RANK = 384
MLA_KV_RANK = 256
MLA_NOPE = 64
MLA_ROPE = 32
MLA_V = 64
MLA_QK = MLA_NOPE + MLA_ROPE
RET_HEADS = 8
RET_DK = 64
RET_DV = 64
RET_CHUNK = 128
GLA_HEADS = 4
GLA_DK = 128
GLA_DV = 256
GLA_GATE_RANK = 16
GLA_TAU = 16.0
GLA_CHUNK = 64
D_FF = 2816
ROPE_THETA = 10000.0
EPS = 1e-6

ADAM_LR = 0.001
ADAM_B1 = 0.9
ADAM_B2 = 0.999
ADAM_EPS = 1e-08
ADAM_WD = 0.01
ADAM_STEP = 10

LANE = 128
SUBLANE = 8
ROW_TILE = 512
VMEM_LIMIT = 56 * 1024 * 1024
WEIGHT_TILE_BYTES = 6 * 1024 * 1024

EV_CQ, EV_CKV, EV_KR, EV_RQ, EV_RK, EV_RV, EV_RG, EV_IN = 0, 384, 640, 768, 1280, 1792, 2304, 2816
OD_Q, OD_K, OD_V, OD_R, OD_GA, OD_IN = 0, 512, 1024, 2048, 3072, 3200
N_GROUPS = 4


def _cparams(sem):
    return pltpu.CompilerParams(dimension_semantics=sem, vmem_limit_bytes=VMEM_LIMIT)


def _dot(a, b):
    return jnp.dot(a.astype(BF16), b.astype(BF16), preferred_element_type=F32)


def _dot_nt(a, b):
    return lax.dot_general(a.astype(BF16), b.astype(BF16), (((1,), (1,)), ((), ())), preferred_element_type=F32)


def _dot_tn(a, b):
    return lax.dot_general(a.astype(BF16), b.astype(BF16), (((0,), (0,)), ((), ())), preferred_element_type=F32)


def _sigmoid(x):
    return 1.0 / (1.0 + jnp.exp(-x))


def _col_tile(k, n, itemsize=2):
    best = LANE
    for t in range(LANE, n + 1, LANE):
        if n % t == 0 and k * t * itemsize <= WEIGHT_TILE_BYTES:
            best = t
    return best if n % LANE == 0 else n


def _row_tile(m):
    return min(ROW_TILE, m)


def mm_nn(a, b, res=None, out_dtype=F32, name="mm_nn"):
    m, k = a.shape
    n = b.shape[1]
    tm, tn = _row_tile(m), _col_tile(k, n)

    def body(*refs):
        if res is None:
            a_ref, b_ref, o_ref = refs
        else:
            a_ref, b_ref, r_ref, o_ref = refs
        acc = _dot(a_ref[...], b_ref[...])
        if res is not None:
            acc = acc + r_ref[...].astype(F32)
        o_ref[...] = acc.astype(out_dtype)

    in_specs = [pl.BlockSpec((tm, k), lambda i, j: (i, 0)), pl.BlockSpec((k, tn), lambda i, j: (0, j))]
    args = [a, b]
    if res is not None:
        in_specs.append(pl.BlockSpec((tm, tn), lambda i, j: (i, j)))
        args.append(res)
    return pl.pallas_call(
        body, name=name, grid=(m // tm, n // tn), in_specs=in_specs,
        out_specs=pl.BlockSpec((tm, tn), lambda i, j: (i, j)),
        out_shape=jax.ShapeDtypeStruct((m, n), out_dtype),
        compiler_params=_cparams(("parallel", "parallel")),
    )(*args)


def mm_tn(a, b, name="mm_tn"):
    t, k = a.shape
    n = b.shape[1]
    tt = _row_tile(t)
    tk = k if k <= 1024 else _col_tile(1024, k, 4)
    tn = n if n <= 1024 else _col_tile(1024, n, 4)

    def body(a_ref, b_ref, o_ref):
        @pl.when(pl.program_id(2) == 0)
        def _():
            o_ref[...] = jnp.zeros_like(o_ref)
        o_ref[...] += _dot_tn(a_ref[...], b_ref[...])

    return pl.pallas_call(
        body, name=name, grid=(k // tk, n // tn, t // tt),
        in_specs=[pl.BlockSpec((tt, tk), lambda i, j, s: (s, i)), pl.BlockSpec((tt, tn), lambda i, j, s: (s, j))],
        out_specs=pl.BlockSpec((tk, tn), lambda i, j, s: (i, j)),
        out_shape=jax.ShapeDtypeStruct((k, n), F32),
        compiler_params=_cparams(("parallel", "parallel", "arbitrary")),
    )(a, b)


def rmsnorm_fwd(x, g, name="rmsnorm_fwd"):
    t, d = x.shape
    tm = _row_tile(t)

    def body(x_ref, g_ref, h_ref):
        xv = x_ref[...]
        r = lax.rsqrt(jnp.mean(xv * xv, axis=-1, keepdims=True) + EPS)
        h_ref[...] = (xv * r * g_ref[...]).astype(BF16)

    return pl.pallas_call(
        body, name=name, grid=(t // tm,),
        in_specs=[pl.BlockSpec((tm, d), lambda i: (i, 0)), pl.BlockSpec((1, d), lambda i: (0, 0))],
        out_specs=pl.BlockSpec((tm, d), lambda i: (i, 0)),
        out_shape=jax.ShapeDtypeStruct((t, d), BF16),
        compiler_params=_cparams(("parallel",)),
    )(x, g.reshape(1, d))


def rmsnorm_bwd(x, g, dh, dres, name="rmsnorm_bwd"):
    t, d = x.shape
    tm = _row_tile(t)

    def body(x_ref, g_ref, dh_ref, dres_ref, dx_ref, dg_ref):
        @pl.when(pl.program_id(0) == 0)
        def _():
            dg_ref[...] = jnp.zeros_like(dg_ref)
        xv = x_ref[...]
        r = lax.rsqrt(jnp.mean(xv * xv, axis=-1, keepdims=True) + EPS)
        xh = xv * r
        dhv = dh_ref[...]
        dg_ref[...] += jnp.sum(dhv * xh, axis=0, keepdims=True)
        dxh = dhv * g_ref[...]
        dx_ref[...] = dres_ref[...] + r * (dxh - xh * jnp.mean(dxh * xh, axis=-1, keepdims=True))

    row = pl.BlockSpec((tm, d), lambda i: (i, 0))
    vec = pl.BlockSpec((1, d), lambda i: (0, 0))
    return pl.pallas_call(
        body, name=name, grid=(t // tm,),
        in_specs=[row, vec, row, row], out_specs=[row, vec],
        out_shape=[jax.ShapeDtypeStruct((t, d), F32), jax.ShapeDtypeStruct((1, d), F32)],
        compiler_params=_cparams(("arbitrary",)),
    )(x, g.reshape(1, d), dh, dres)


def _rope_apply(x, cos, s1, s2, half):
    return x * cos + pltpu.roll(x, LANE - half, 1) * s1 + pltpu.roll(x, half, 1) * s2


def _rope_transpose(dy, cos, s1, s2, half):
    return dy * cos + pltpu.roll(dy * s1, half, 1) + pltpu.roll(dy * s2, LANE - half, 1)


def rope_tables(positions, lane_start, half, period):
    pos = positions.reshape(-1).astype(F32)
    inv = ROPE_THETA ** (-jnp.arange(half, dtype=F32) / half)
    ang = pos[:, None] * inv[None, :]
    cos, sin = jnp.cos(ang), jnp.sin(ang)
    t = pos.shape[0]
    pre = lane_start
    post = period - lane_start - 2 * half
    ones = lambda n: jnp.ones((t, n), F32)
    zeros = lambda n: jnp.zeros((t, n), F32)
    c = jnp.concatenate([ones(pre), cos, cos, ones(post)], axis=1)
    a = jnp.concatenate([zeros(pre), -sin, zeros(half), zeros(post)], axis=1)
    b = jnp.concatenate([zeros(pre), zeros(half), sin, zeros(post)], axis=1)
    rep = LANE // period
    return tuple(jnp.tile(v, (1, rep)) for v in (c, a, b))


def _mla_forward_tile(p, cos, s1, s2, qn_g, kvn_g, wuq, wk, wv, qhn, khn):
    cq = p[:, EV_CQ:EV_CKV]
    ckv = p[:, EV_CKV:EV_KR]
    kr = p[:, EV_KR:EV_RQ]
    rq = lax.rsqrt(jnp.mean(cq * cq, axis=-1, keepdims=True) + EPS)
    rkv = lax.rsqrt(jnp.mean(ckv * ckv, axis=-1, keepdims=True) + EPS)
    qn = cq * rq * qn_g
    kvn = ckv * rkv * kvn_g
    q_raw = _dot(qn, wuq)
    k_raw = _dot(kvn, wk)
    v = _dot(kvn, wv)
    krp = pltpu.roll(kr, MLA_NOPE, 1)
    return cq, ckv, rq, rkv, qn, kvn, q_raw, k_raw, v, krp


def _head_norm(xh, g):
    r = lax.rsqrt(jnp.sum(xh * xh, axis=-1, keepdims=True) * (1.0 / MLA_QK) + EPS)
    return xh * r * g, r


def mla_prep_fwd(p, tabs, qn_g, kvn_g, wuq, wk, wv, qhn, khn, name="mla_prep_fwd"):
    t = p.shape[0]
    tm = _row_tile(t)
    hw = MLA_HEADS * LANE

    def body(p_ref, c_ref, s1_ref, s2_ref, qn_ref, kvn_ref, wuq_ref, wk_ref, wv_ref, qhn_ref, khn_ref,
             q_out, k_out, v_out):
        cos, s1, s2 = c_ref[...], s1_ref[...], s2_ref[...]
        (_, _, _, _, _, _, q_raw, k_raw, v, krp) = _mla_forward_tile(
            p_ref[...], cos, s1, s2, qn_ref[...], kvn_ref[...], wuq_ref[...], wk_ref[...], wv_ref[...],
            qhn_ref[...], khn_ref[...])
        v_out[...] = v.astype(BF16)
        for h in range(MLA_HEADS):
            sl = slice(h * LANE, (h + 1) * LANE)
            qh, _ = _head_norm(q_raw[:, sl], qhn_ref[...])
            kh, _ = _head_norm(k_raw[:, sl] + krp, khn_ref[...])
            q_out[:, sl] = (_rope_apply(qh, cos, s1, s2, MLA_ROPE // 2) * ATTN_QSCALE).astype(BF16)
            k_out[:, sl] = _rope_apply(kh, cos, s1, s2, MLA_ROPE // 2).astype(BF16)

    row = lambda w: pl.BlockSpec((tm, w), lambda i: (i, 0))
    full = lambda a: pl.BlockSpec(a.shape, lambda i: (0,) * a.ndim)
    ws = [qn_g, kvn_g, wuq, wk, wv, qhn, khn]
    return pl.pallas_call(
        body, name=name, grid=(t // tm,),
        in_specs=[row(EV_RQ), row(LANE), row(LANE), row(LANE)] + [full(w) for w in ws],
        out_specs=[row(hw)] * 3,
        out_shape=[jax.ShapeDtypeStruct((t, hw), BF16)] * 3,
        compiler_params=_cparams(("parallel",)),
    )(p, *tabs, *ws)


def mla_prep_bwd(p, tabs, qn_g, kvn_g, wuq, wk, wv, wuq_t, wk_t, wv_t, qhn, khn, dq, dk, dv,
                 name="mla_prep_bwd"):
    t = p.shape[0]
    tm = _row_tile(t)
    hw = MLA_HEADS * LANE

    def body(p_ref, c_ref, s1_ref, s2_ref, qn_ref, kvn_ref, wuq_ref, wk_ref, wv_ref, wuqt_ref, wkt_ref, wvt_ref,
             qhn_ref, khn_ref, dq_ref, dk_ref, dv_ref,
             dp_ref, dwuq_ref, dwk_ref, dwv_ref, dqn_ref, dkvn_ref, dqhn_ref, dkhn_ref, dqraw_s, dkraw_s):
        @pl.when(pl.program_id(0) == 0)
        def _():
            for r in (dwuq_ref, dwk_ref, dwv_ref, dqn_ref, dkvn_ref, dqhn_ref, dkhn_ref):
                r[...] = jnp.zeros_like(r)
        cos, s1, s2 = c_ref[...], s1_ref[...], s2_ref[...]
        qhn_v, khn_v = qhn_ref[...], khn_ref[...]
        (cq, ckv, rq, rkv, qn, kvn, q_raw, k_raw, _, krp) = _mla_forward_tile(
            p_ref[...], cos, s1, s2, qn_ref[...], kvn_ref[...], wuq_ref[...], wk_ref[...], wv_ref[...],
            qhn_v, khn_v)
        half = MLA_ROPE // 2
        dkr_sum = jnp.zeros((tm, LANE), F32)
        dqhn_acc = jnp.zeros((1, LANE), F32)
        dkhn_acc = jnp.zeros((1, LANE), F32)
        for h in range(MLA_HEADS):
            sl = slice(h * LANE, (h + 1) * LANE)
            xq = q_raw[:, sl]
            _, r = _head_norm(xq, qhn_v)
            xh = xq * r
            dy = _rope_transpose(dq_ref[:, sl] * ATTN_SCALE, cos, s1, s2, half)
            dqhn_acc = dqhn_acc + jnp.sum(dy * xh, axis=0, keepdims=True)
            dxh = dy * qhn_v
            dqraw_s[:, sl] = r * (dxh - xh * (jnp.sum(dxh * xh, axis=-1, keepdims=True) * (1.0 / MLA_QK)))
            xk = k_raw[:, sl] + krp
            _, r = _head_norm(xk, khn_v)
            xh = xk * r
            dy = _rope_transpose(dk_ref[:, sl] * math.log(2.0), cos, s1, s2, half)
            dkhn_acc = dkhn_acc + jnp.sum(dy * xh, axis=0, keepdims=True)
            dxh = dy * khn_v
            dxk = r * (dxh - xh * (jnp.sum(dxh * xh, axis=-1, keepdims=True) * (1.0 / MLA_QK)))
            dkraw_s[:, sl] = dxk
            dkr_sum = dkr_sum + dxk
        dqhn_ref[...] += dqhn_acc
        dkhn_ref[...] += dkhn_acc
        dq_raw = dqraw_s[...]
        dk_raw = dkraw_s[...]
        dvv = dv_ref[...]
        dwuq_ref[...] += _dot_tn(qn, dq_raw)
        dwk_ref[...] += _dot_tn(kvn, dk_raw)
        dwv_ref[...] += _dot_tn(kvn, dvv)
        dqn = _dot(dq_raw, wuqt_ref[...])
        dkvn = _dot(dk_raw, wkt_ref[...]) + _dot(dvv, wvt_ref[...])
        xh = cq * rq
        dqn_ref[...] += jnp.sum(dqn * xh, axis=0, keepdims=True)
        dxh = dqn * qn_ref[...]
        dp_ref[:, EV_CQ:EV_CKV] = rq * (dxh - xh * jnp.mean(dxh * xh, axis=-1, keepdims=True))
        xh = ckv * rkv
        dkvn_ref[...] += jnp.sum(dkvn * xh, axis=0, keepdims=True)
        dxh = dkvn * kvn_ref[...]
        dp_ref[:, EV_CKV:EV_KR] = rkv * (dxh - xh * jnp.mean(dxh * xh, axis=-1, keepdims=True))
        lane = lax.broadcasted_iota(jnp.int32, (tm, LANE), 1)
        dp_ref[:, EV_KR:EV_RQ] = jnp.where(lane < MLA_ROPE, pltpu.roll(dkr_sum, LANE - MLA_NOPE, 1), 0.0)

    row = lambda w: pl.BlockSpec((tm, w), lambda i: (i, 0))
    full = lambda a: pl.BlockSpec(a.shape, lambda i: (0,) * a.ndim)
    ws = [qn_g, kvn_g, wuq, wk, wv, wuq_t, wk_t, wv_t, qhn, khn]
    outs = [jax.ShapeDtypeStruct((t, EV_RQ), F32), jax.ShapeDtypeStruct(wuq.shape, F32),
            jax.ShapeDtypeStruct(wk.shape, F32), jax.ShapeDtypeStruct(wv.shape, F32),
            jax.ShapeDtypeStruct(qn_g.shape, F32), jax.ShapeDtypeStruct(kvn_g.shape, F32),
            jax.ShapeDtypeStruct(qhn.shape, F32), jax.ShapeDtypeStruct(khn.shape, F32)]
    return pl.pallas_call(
        body, name=name, grid=(t // tm,),
        in_specs=[row(EV_RQ), row(LANE), row(LANE), row(LANE)] + [full(w) for w in ws] + [row(hw)] * 3,
        out_specs=[row(EV_RQ)] + [full(o) for o in outs[1:]],
        out_shape=outs,
        scratch_shapes=[pltpu.VMEM((tm, hw), F32), pltpu.VMEM((tm, hw), F32)],
        compiler_params=_cparams(("arbitrary",)),
    )(p, *tabs, *ws, dq, dk, dv)


ATTN_SCALE = MLA_QK ** -0.5
ATTN_QSCALE = ATTN_SCALE * math.log2(math.e)
ATTN_FWD_TQ, ATTN_FWD_TK = 512, 8192
ATTN_BWD_TQ, ATTN_BWD_TK = 256, 4096


def attn_fwd(q, k, v, name="attn_fwd"):
    t = q.shape[0]
    tq, tk = min(ATTN_FWD_TQ, _row_tile(t)), min(ATTN_FWD_TK, t)
    nh = MLA_HEADS

    def body(q_ref, k_ref, v_ref, o_ref, lse_ref, m_s, l_s, acc_s):
        j = pl.program_id(2)

        @pl.when(j == 0)
        def _():
            m_s[...] = jnp.full_like(m_s, -jnp.inf)
            l_s[...] = jnp.zeros_like(l_s)
            acc_s[...] = jnp.zeros_like(acc_s)

        s = _dot_nt(q_ref[...], k_ref[...])
        m_old = m_s[...]
        m_new = jnp.maximum(m_old, jnp.max(s, axis=-1, keepdims=True))
        pr = jnp.exp2(s - m_new)
        alpha = jnp.exp2(m_old - m_new)
        l_s[...] = alpha * l_s[...] + jnp.sum(pr, axis=-1, keepdims=True)
        acc_s[...] = alpha * acc_s[...] + _dot(pr, v_ref[...])
        m_s[...] = m_new

        @pl.when(j == pl.num_programs(2) - 1)
        def _():
            o_ref[...] = acc_s[...] / l_s[...]
            lse_ref[...] = m_s[...] + jnp.log2(l_s[...])

    return pl.pallas_call(
        body, name=name, grid=(nh, t // tq, t // tk),
        in_specs=[pl.BlockSpec((tq, LANE), lambda h, i, j: (i, h)),
                  pl.BlockSpec((tk, LANE), lambda h, i, j: (j, h)),
                  pl.BlockSpec((tk, LANE), lambda h, i, j: (j, h))],
        out_specs=[pl.BlockSpec((tq, LANE), lambda h, i, j: (i, h)),
                   pl.BlockSpec((None, tq, 1), lambda h, i, j: (h, i, 0))],
        out_shape=[jax.ShapeDtypeStruct((t, nh * LANE), F32), jax.ShapeDtypeStruct((nh, t, 1), F32)],
        scratch_shapes=[pltpu.VMEM((tq, 1), F32), pltpu.VMEM((tq, 1), F32), pltpu.VMEM((tq, LANE), F32)],
        compiler_params=_cparams(("parallel", "parallel", "arbitrary")),
    )(q, k, v)


def attn_bwd(q, k, v, o, lse, do, name="attn_bwd"):
    t = q.shape[0]
    tq, tk = min(ATTN_BWD_TQ, _row_tile(t)), min(ATTN_BWD_TK, t)
    nh = MLA_HEADS
    nq = t // tq

    def body(q_ref, k_ref, v_ref, o_ref, lse_ref, do_ref, dq_ref, dk_ref, dv_ref):
        kj, qi = pl.program_id(1), pl.program_id(2)

        @pl.when(qi == 0)
        def _():
            dk_ref[...] = jnp.zeros_like(dk_ref)
            dv_ref[...] = jnp.zeros_like(dv_ref)

        qv, kv, vv, dov = q_ref[...], k_ref[...], v_ref[...], do_ref[...]
        s = _dot_nt(qv, kv)
        pr = jnp.exp2(s - lse_ref[...])
        dp = _dot_nt(dov, vv)
        delta = jnp.sum(dov * o_ref[...], axis=-1, keepdims=True)
        ds = pr * (dp - delta)
        dv_ref[...] += _dot_tn(pr, dov)
        dk_ref[...] += _dot_tn(ds, qv)
        dq_tile = _dot(ds, kv)
        rows = pl.ds(pl.multiple_of(qi * tq, tq), tq)

        @pl.when(kj == 0)
        def _():
            dq_ref[rows, :] = dq_tile

        @pl.when(kj != 0)
        def _():
            dq_ref[rows, :] += dq_tile

    qspec = pl.BlockSpec((tq, LANE), lambda h, j, i: (i, h))
    kspec = pl.BlockSpec((tk, LANE), lambda h, j, i: (j, h))
    return pl.pallas_call(
        body, name=name, grid=(nh, t // tk, nq),
        in_specs=[qspec, kspec, kspec, qspec, pl.BlockSpec((None, tq, 1), lambda h, j, i: (h, i, 0)), qspec],
        out_specs=[pl.BlockSpec((t, LANE), lambda h, j, i: (0, h)), kspec, kspec],
        out_shape=[jax.ShapeDtypeStruct((t, nh * LANE), F32)] * 3,
        compiler_params=_cparams(("parallel", "arbitrary", "arbitrary")),
    )(q, k, v, o, lse, do)


def _scan_consts(c, reverse, inclusive):
    ii = lax.broadcasted_iota(jnp.int32, (c, c), 0)
    jj = lax.broadcasted_iota(jnp.int32, (c, c), 1)
    if reverse:
        incl = jj >= ii
        mask = incl if inclusive else jj > ii
    else:
        incl = jj <= ii
        mask = incl if inclusive else jj < ii
    mid = (c - 1 - c // 2) if reverse else c // 2
    incl_t = (jj <= ii) if reverse else (jj >= ii)
    return incl.astype(F32), incl_t.astype(F32), mask.astype(F32), mid


def _dot_split(a01, x):
    hi = x.astype(BF16)
    lo = (x - hi.astype(F32)).astype(BF16)
    a = a01.astype(BF16)
    return jnp.dot(a, hi, preferred_element_type=F32) + jnp.dot(a, lo, preferred_element_type=F32)


def _sub_masks(sub, dvg, u):
    if sub == 1:
        return None, None
    kl = lax.broadcasted_iota(jnp.int32, (1, LANE), 1)
    vl = lax.broadcasted_iota(jnp.int32, (1, dvg), 1)
    kw, vw = LANE // sub, dvg // sub
    km = (kl >= u * kw) & (kl < (u + 1) * kw)
    vm = (vl >= u * vw) & (vl < (u + 1) * vw)
    return km.astype(F32), vm.astype(F32)


def _scan_chunk_fwd(q, k, la, incl, mid):
    b = _dot_split(incl, la)
    row = lax.broadcasted_iota(jnp.int32, b.shape, 0)
    bm = jnp.sum(jnp.where(row == mid, b, 0.0), axis=0, keepdims=True)
    tot = jnp.sum(la, axis=0, keepdims=True)
    e_qc = jnp.exp(b - bm)
    e_kc = jnp.exp(bm - b)
    e_qe = jnp.exp(b)
    e_kd = jnp.exp(tot - b)
    return e_qc, e_kc, e_qe, e_kd


def scan_fwd(q_arr, k_arr, v_arr, la_arr, *, qcb, kcb, vcb, lacb, la_row, chunk, dvg, sub, reverse, inclusive,
             qscale, kscale, rope=None, name="scan_fwd"):
    t = q_arr.shape[0]
    r = _row_tile(t)
    nb, nc = t // r, r // chunk
    c = chunk
    rb = (lambda j: nb - 1 - j) if reverse else (lambda j: j)
    order = list(range(nc))[::-1] if reverse else list(range(nc))
    half = RET_DK // 2

    def body(*refs):
        if rope is None:
            q_ref, k_ref, v_ref, la_ref, o_ref, st_ref, s_s = refs
        else:
            q_ref, k_ref, v_ref, la_ref, c_ref, s1_ref, s2_ref, o_ref, st_ref, s_s = refs

        @pl.when(pl.program_id(1) == 0)
        def _():
            s_s[...] = jnp.zeros_like(s_s)

        incl, _, mask, mid = _scan_consts(c, reverse, inclusive)
        for ci in order:
            rows = slice(ci * c, (ci + 1) * c)
            qv = q_ref[rows, :] * qscale
            kv = k_ref[rows, :] * kscale
            if rope is not None:
                cs, a1, a2 = c_ref[rows, :], s1_ref[rows, :], s2_ref[rows, :]
                qv = _rope_apply(qv, cs, a1, a2, half)
                kv = _rope_apply(kv, cs, a1, a2, half)
            la = jnp.broadcast_to(la_ref[...], (c, LANE)) if la_row else la_ref[rows, :]
            vv = v_ref[rows, :]
            e_qc, e_kc, e_qe, e_kd = _scan_chunk_fwd(qv, kv, la, incl, mid)
            qc, kc, qe, kd = qv * e_qc, kv * e_kc, qv * e_qe, kv * e_kd
            sg = s_s[...]
            st_ref[ci] = sg
            acc = None
            for u in range(sub):
                mu, vmu = _sub_masks(sub, dvg, u)
                qcu = qc if mu is None else qc * mu
                qeu = qe if mu is None else qe * mu
                a = _dot_nt(qcu, kc) * mask
                ou = _dot(a, vv) + _dot_nt(qeu, sg)
                ou = ou if vmu is None else ou * vmu
                acc = ou if acc is None else acc + ou
            o_ref[rows, :] = acc
            decay = jnp.exp(jnp.sum(la, axis=0, keepdims=True))
            s_s[...] = decay * sg + _dot_tn(vv, kd)

    specs = [pl.BlockSpec((r, LANE), lambda g, j: (rb(j), qcb + g)),
             pl.BlockSpec((r, LANE), lambda g, j: (rb(j), kcb + g)),
             pl.BlockSpec((r, dvg), lambda g, j: (rb(j), vcb + g)),
             pl.BlockSpec((1, LANE), lambda g, j: (0, lacb + g)) if la_row
             else pl.BlockSpec((r, LANE), lambda g, j: (rb(j), lacb + g))]
    args = [q_arr, k_arr, v_arr, la_arr]
    if rope is not None:
        specs += [pl.BlockSpec((r, LANE), lambda g, j: (rb(j), 0))] * 3
        args += list(rope)
    return pl.pallas_call(
        body, name=name, grid=(N_GROUPS, nb), in_specs=specs,
        out_specs=[pl.BlockSpec((r, dvg), lambda g, j: (rb(j), g)),
                   pl.BlockSpec((nc, dvg, LANE), lambda g, j: (rb(j), g, 0))],
        out_shape=[jax.ShapeDtypeStruct((t, N_GROUPS * dvg), F32),
                   jax.ShapeDtypeStruct((t // c, N_GROUPS * dvg, LANE), F32)],
        scratch_shapes=[pltpu.VMEM((dvg, LANE), F32)],
        compiler_params=_cparams(("parallel", "arbitrary")),
    )(*args)


def scan_bwd(q_arr, k_arr, v_arr, la_arr, st_arr, do_arr, prev, *, qcb, kcb, vcb, lacb, la_row, chunk, dvg, sub,
             reverse, inclusive, qscale, kscale, rope=None, name="scan_bwd"):
    t = q_arr.shape[0]
    r = _row_tile(t)
    nb, nc = t // r, r // chunk
    c = chunk
    rb = (lambda j: j) if reverse else (lambda j: nb - 1 - j)
    order = list(range(nc)) if reverse else list(range(nc))[::-1]
    half = RET_DK // 2
    n_in = 6 + (3 if rope is not None else 0) + (3 if prev is not None else 0)

    def body(*refs):
        ins, outs = refs[:n_in], refs[n_in:]
        q_ref, k_ref, v_ref, la_ref, st_ref, do_ref = ins[:6]
        pos = 6
        if rope is not None:
            c_ref, s1_ref, s2_ref = ins[pos:pos + 3]
            pos += 3
        if prev is not None:
            pq_ref, pk_ref, pv_ref = ins[pos:pos + 3]
        dq_ref, dk_ref, dv_ref, dla_ref, g_s = outs

        @pl.when(pl.program_id(1) == 0)
        def _():
            g_s[...] = jnp.zeros_like(g_s)
            if la_row:
                dla_ref[...] = jnp.zeros_like(dla_ref)

        incl, incl_t, mask, mid = _scan_consts(c, reverse, inclusive)
        for ci in order:
            rows = slice(ci * c, (ci + 1) * c)
            qv = q_ref[rows, :] * qscale
            kv = k_ref[rows, :] * kscale
            if rope is not None:
                cs, a1, a2 = c_ref[rows, :], s1_ref[rows, :], s2_ref[rows, :]
                qv = _rope_apply(qv, cs, a1, a2, half)
                kv = _rope_apply(kv, cs, a1, a2, half)
            la = jnp.broadcast_to(la_ref[...], (c, LANE)) if la_row else la_ref[rows, :]
            vv = v_ref[rows, :]
            dov = do_ref[rows, :]
            e_qc, e_kc, e_qe, e_kd = _scan_chunk_fwd(qv, kv, la, incl, mid)
            qc, kc, qe, kd = qv * e_qc, kv * e_kc, qv * e_qe, kv * e_kd
            sg = st_ref[ci]
            gn = g_s[...]
            dqc = jnp.zeros((c, LANE), F32)
            dkc = jnp.zeros((c, LANE), F32)
            dqe = jnp.zeros((c, LANE), F32)
            dvv = _dot_nt(kd, gn)
            ds_direct = jnp.zeros((dvg, LANE), F32)
            for u in range(sub):
                mu, vmu = _sub_masks(sub, dvg, u)
                qcu = qc if mu is None else qc * mu
                qeu = qe if mu is None else qe * mu
                dou = dov if vmu is None else dov * vmu
                a = _dot_nt(qcu, kc) * mask
                da = _dot_nt(dou, vv) * mask
                dvv = dvv + _dot_tn(a, dou)
                t1 = _dot(da, kc)
                dqc = dqc + (t1 if mu is None else t1 * mu)
                dkc = dkc + _dot_tn(da, qcu)
                t2 = _dot(dou, sg)
                dqe = dqe + (t2 if mu is None else t2 * mu)
                ds_direct = ds_direct + _dot_tn(dou, qeu)
            dkd = _dot(vv, gn)
            decay = jnp.exp(jnp.sum(la, axis=0, keepdims=True))
            dtot = jnp.sum(gn * sg, axis=0, keepdims=True) * decay + jnp.sum(dkd * kd, axis=0, keepdims=True)
            db = dqc * qc - dkc * kc + dqe * qe - dkd * kd
            dla = _dot_split(incl_t, db) + dtot
            dqv = dqc * e_qc + dqe * e_qe
            dkv = dkc * e_kc + dkd * e_kd
            if rope is not None:
                dqv = _rope_transpose(dqv, cs, a1, a2, half)
                dkv = _rope_transpose(dkv, cs, a1, a2, half)
            dqv = dqv * qscale
            dkv = dkv * kscale
            if prev is not None:
                dqv = dqv + pq_ref[rows, :]
                dkv = dkv + pk_ref[rows, :]
                dvv = dvv + pv_ref[rows, :]
            dq_ref[rows, :] = dqv
            dk_ref[rows, :] = dkv
            dv_ref[rows, :] = dvv
            if la_row:
                dla_ref[...] += jnp.sum(dla, axis=0, keepdims=True)
            else:
                dla_ref[rows, :] = dla
            g_s[...] = ds_direct + decay * gn

    kblk = lambda cb: pl.BlockSpec((r, LANE), lambda g, j: (rb(j), cb + g))
    vblk = lambda cb: pl.BlockSpec((r, dvg), lambda g, j: (rb(j), cb + g))
    specs = [kblk(qcb), kblk(kcb), vblk(vcb),
             pl.BlockSpec((1, LANE), lambda g, j: (0, lacb + g)) if la_row else kblk(lacb),
             pl.BlockSpec((nc, dvg, LANE), lambda g, j: (rb(j), g, 0)), vblk(0)]
    args = [q_arr, k_arr, v_arr, la_arr, st_arr, do_arr]
    if rope is not None:
        specs += [pl.BlockSpec((r, LANE), lambda g, j: (rb(j), 0))] * 3
        args += list(rope)
    if prev is not None:
        specs += [kblk(0), kblk(0), vblk(0)]
        args += list(prev)
    wk = N_GROUPS * LANE
    outs = [jax.ShapeDtypeStruct((t, wk), F32), jax.ShapeDtypeStruct((t, wk), F32),
            jax.ShapeDtypeStruct((t, N_GROUPS * dvg), F32),
            jax.ShapeDtypeStruct((1, wk) if la_row else (t, wk), F32)]
    return pl.pallas_call(
        body, name=name, grid=(N_GROUPS, nb), in_specs=specs,
        out_specs=[kblk(0), kblk(0), vblk(0),
                   pl.BlockSpec((1, LANE), lambda g, j: (0, g)) if la_row else kblk(0)],
        out_shape=outs,
        scratch_shapes=[pltpu.VMEM((dvg, LANE), F32)],
        compiler_params=_cparams(("parallel", "arbitrary")),
    )(*args)


def _seg_mean(x, seg):
    w = x.shape[1]
    if seg % LANE == 0:
        parts = []
        for s in range(0, w, seg):
            m = jnp.mean(x[:, s:s + seg], axis=-1, keepdims=True)
            parts.append(jnp.broadcast_to(m, (x.shape[0], seg)))
        return jnp.concatenate(parts, axis=1)
    shift = seg.bit_length() - 1
    ii = lax.shift_right_logical(lax.broadcasted_iota(jnp.int32, (w, w), 0), shift)
    jj = lax.shift_right_logical(lax.broadcasted_iota(jnp.int32, (w, w), 1), shift)
    e = (ii == jj).astype(BF16)
    hi = x.astype(BF16)
    lo = (x - hi.astype(F32)).astype(BF16)
    return (jnp.dot(hi, e, preferred_element_type=F32) + jnp.dot(lo, e, preferred_element_type=F32)) * (1.0 / seg)


def gated_norm_fwd(o_f, o_b, gate_arr, gcb, gn, seg, name="gated_norm_fwd"):
    t, w = o_f.shape
    tm = _row_tile(t)

    def body(of_ref, ob_ref, g_ref, gn_ref, y_ref):
        o = of_ref[...] + ob_ref[...]
        r = lax.rsqrt(_seg_mean(o * o, seg) + EPS)
        gt = g_ref[...]
        y_ref[...] = (gt * _sigmoid(gt) * (o * r * gn_ref[...])).astype(BF16)

    bw = max(seg, LANE)
    row = pl.BlockSpec((tm, bw), lambda j, i: (i, j))
    return pl.pallas_call(
        body, name=name, grid=(w // bw, t // tm),
        in_specs=[row, row, pl.BlockSpec((tm, bw), lambda j, i: (i, gcb + j)),
                  pl.BlockSpec((1, bw), lambda j, i: (0, j))],
        out_specs=row, out_shape=jax.ShapeDtypeStruct((t, w), BF16),
        compiler_params=_cparams(("parallel", "parallel")),
    )(o_f, o_b, gate_arr, gn.reshape(1, w))


def gated_norm_bwd(o_f, o_b, gate_arr, gcb, gn, seg, dy, name="gated_norm_bwd"):
    t, w = o_f.shape
    tm = _row_tile(t)

    def body(of_ref, ob_ref, g_ref, gn_ref, dy_ref, do_ref, dg_ref, dgn_ref):
        @pl.when(pl.program_id(1) == 0)
        def _():
            dgn_ref[...] = jnp.zeros_like(dgn_ref)
        o = of_ref[...] + ob_ref[...]
        r = lax.rsqrt(_seg_mean(o * o, seg) + EPS)
        xh = o * r
        gt = g_ref[...]
        sg = _sigmoid(gt)
        dyv = dy_ref[...]
        n = xh * gn_ref[...]
        dg_ref[...] = dyv * n * (sg * (1.0 + gt * (1.0 - sg)))
        dn = dyv * (gt * sg)
        dgn_ref[...] += jnp.sum(dn * xh, axis=0, keepdims=True)
        dxh = dn * gn_ref[...]
        do_ref[...] = r * (dxh - xh * _seg_mean(dxh * xh, seg))

    bw = max(seg, LANE)
    row = pl.BlockSpec((tm, bw), lambda j, i: (i, j))
    vec = pl.BlockSpec((1, bw), lambda j, i: (0, j))
    return pl.pallas_call(
        body, name=name, grid=(w // bw, t // tm),
        in_specs=[row, row, pl.BlockSpec((tm, bw), lambda j, i: (i, gcb + j)), vec, row],
        out_specs=[row, row, vec],
        out_shape=[jax.ShapeDtypeStruct((t, w), F32), jax.ShapeDtypeStruct((t, w), F32),
                   jax.ShapeDtypeStruct((1, w), F32)],
        compiler_params=_cparams(("parallel", "arbitrary")),
    )(o_f, o_b, gate_arr, gn.reshape(1, w), dy)


def gla_gate_fwd(p, wg, bg, name="gla_gate_fwd"):
    t = p.shape[0]
    tm = _row_tile(t)
    w = wg.shape[1]
    gcb = OD_GA // LANE

    def body(ga_ref, wg_ref, bg_ref, la_ref):
        z = _dot(ga_ref[...], wg_ref[...]) + bg_ref[...]
        la_ref[...] = (jnp.minimum(z, 0.0) - jnp.log(1.0 + jnp.exp(-jnp.abs(z)))) * (1.0 / GLA_TAU)

    return pl.pallas_call(
        body, name=name, grid=(t // tm,),
        in_specs=[pl.BlockSpec((tm, LANE), lambda i: (i, gcb)), pl.BlockSpec((LANE, w), lambda i: (0, 0)),
                  pl.BlockSpec((1, w), lambda i: (0, 0))],
        out_specs=pl.BlockSpec((tm, w), lambda i: (i, 0)),
        out_shape=jax.ShapeDtypeStruct((t, w), F32),
        compiler_params=_cparams(("parallel",)),
    )(p, wg, bg)


def gla_gate_bwd(p, wg, wg_t, bg, dla, name="gla_gate_bwd"):
    t = p.shape[0]
    tm = _row_tile(t)
    w = wg.shape[1]
    gcb = OD_GA // LANE

    def body(ga_ref, wg_ref, wgt_ref, bg_ref, dla_ref, dga_ref, dwg_ref, dbg_ref):
        @pl.when(pl.program_id(0) == 0)
        def _():
            dwg_ref[...] = jnp.zeros_like(dwg_ref)
            dbg_ref[...] = jnp.zeros_like(dbg_ref)
        ga = ga_ref[...]
        z = _dot(ga, wg_ref[...]) + bg_ref[...]
        dz = dla_ref[...] * (1.0 / GLA_TAU) * _sigmoid(-z)
        dga_ref[...] = _dot(dz, wgt_ref[...])
        dwg_ref[...] += _dot_tn(ga, dz)
        dbg_ref[...] += jnp.sum(dz, axis=0, keepdims=True)

    return pl.pallas_call(
        body, name=name, grid=(t // tm,),
        in_specs=[pl.BlockSpec((tm, LANE), lambda i: (i, gcb)), pl.BlockSpec((LANE, w), lambda i: (0, 0)),
                  pl.BlockSpec((w, LANE), lambda i: (0, 0)), pl.BlockSpec((1, w), lambda i: (0, 0)),
                  pl.BlockSpec((tm, w), lambda i: (i, 0))],
        out_specs=[pl.BlockSpec((tm, LANE), lambda i: (i, 0)), pl.BlockSpec((LANE, w), lambda i: (0, 0)),
                   pl.BlockSpec((1, w), lambda i: (0, 0))],
        out_shape=[jax.ShapeDtypeStruct((t, LANE), F32), jax.ShapeDtypeStruct((LANE, w), F32),
                   jax.ShapeDtypeStruct((1, w), F32)],
        compiler_params=_cparams(("arbitrary",)),
    )(p, wg, wg_t, bg, dla)


FFN_COL = 1408


def _shifted(x, prev_row, next_row, first, last):
    tm = x.shape[0]
    row = lax.broadcasted_iota(jnp.int32, x.shape, 0)
    pr = jnp.where(first, 0.0, prev_row)
    nx = jnp.where(last, 0.0, next_row)
    xm1 = jnp.where(row == 0, pr, pltpu.roll(x, 1, 0))
    xp1 = jnp.where(row == tm - 1, nx, pltpu.roll(x, tm - 1, 0))
    return xm1, xp1


def _halo_specs(tm, tc, t, colmap, rowaxis):
    nb8 = tm // SUBLANE
    last8 = t // SUBLANE - 1

    def prev(*ids):
        i = ids[rowaxis]
        return (jnp.maximum(i * nb8 - 1, 0), colmap(*ids))

    def nxt(*ids):
        i = ids[rowaxis]
        return (jnp.minimum((i + 1) * nb8, last8), colmap(*ids))

    return pl.BlockSpec((SUBLANE, tc), prev), pl.BlockSpec((SUBLANE, tc), nxt)


def ffn_act_fwd(up, conv_w, conv_b, name="ffn_act_fwd"):
    t = up.shape[0]
    tm, tc = _row_tile(t), FFN_COL
    ncol = D_FF // tc

    def body(g_ref, gp_ref, gn_ref, v_ref, w_ref, b_ref, a_ref):
        i = pl.program_id(0)
        g = g_ref[...]
        gm1, gp1 = _shifted(g, gp_ref[SUBLANE - 1:SUBLANE, :], gn_ref[0:1, :], i == 0, i == pl.num_programs(0) - 1)
        cc = w_ref[0:1, :] * gm1 + w_ref[1:2, :] * g + w_ref[2:3, :] * gp1 + b_ref[...]
        a_ref[...] = (cc * _sigmoid(cc) * v_ref[...]).astype(BF16)

    prev, nxt = _halo_specs(tm, tc, t, lambda i, j: j, 0)
    return pl.pallas_call(
        body, name=name, grid=(t // tm, ncol),
        in_specs=[pl.BlockSpec((tm, tc), lambda i, j: (i, j)), prev, nxt,
                  pl.BlockSpec((tm, tc), lambda i, j: (i, j + ncol)),
                  pl.BlockSpec((SUBLANE, tc), lambda i, j: (0, j)), pl.BlockSpec((1, tc), lambda i, j: (0, j))],
        out_specs=pl.BlockSpec((tm, tc), lambda i, j: (i, j)),
        out_shape=jax.ShapeDtypeStruct((t, D_FF), BF16),
        compiler_params=_cparams(("parallel", "parallel")),
    )(up, up, up, up, conv_w, conv_b)


def ffn_act_bwd(up, conv_w, conv_b, dact, name="ffn_act_bwd"):
    t = up.shape[0]
    tm, tc = _row_tile(t), FFN_COL
    ncol = D_FF // tc

    def body(g_ref, gp_ref, gn_ref, v_ref, w_ref, b_ref, da_ref, dc_ref, dv_ref, dw_ref):
        i = pl.program_id(1)

        @pl.when(i == 0)
        def _():
            dw_ref[...] = jnp.zeros_like(dw_ref)
        g = g_ref[...]
        gm1, gp1 = _shifted(g, gp_ref[SUBLANE - 1:SUBLANE, :], gn_ref[0:1, :], i == 0, i == pl.num_programs(1) - 1)
        cc = w_ref[0:1, :] * gm1 + w_ref[1:2, :] * g + w_ref[2:3, :] * gp1 + b_ref[...]
        sg = _sigmoid(cc)
        da = da_ref[...]
        dv_ref[...] = da * (cc * sg)
        dc = da * v_ref[...] * (sg * (1.0 + cc * (1.0 - sg)))
        dc_ref[...] = dc
        dw_ref[0:1, :] += jnp.sum(dc * gm1, axis=0, keepdims=True)
        dw_ref[1:2, :] += jnp.sum(dc * g, axis=0, keepdims=True)
        dw_ref[2:3, :] += jnp.sum(dc * gp1, axis=0, keepdims=True)
        dw_ref[3:4, :] += jnp.sum(dc, axis=0, keepdims=True)

    prev, nxt = _halo_specs(tm, tc, t, lambda j, i: j, 1)
    tile = pl.BlockSpec((tm, tc), lambda j, i: (i, j))
    return pl.pallas_call(
        body, name=name, grid=(ncol, t // tm),
        in_specs=[tile, prev, nxt, pl.BlockSpec((tm, tc), lambda j, i: (i, j + ncol)),
                  pl.BlockSpec((SUBLANE, tc), lambda j, i: (0, j)), pl.BlockSpec((1, tc), lambda j, i: (0, j)), tile],
        out_specs=[tile, tile, pl.BlockSpec((SUBLANE, tc), lambda j, i: (0, j))],
        out_shape=[jax.ShapeDtypeStruct((t, D_FF), F32), jax.ShapeDtypeStruct((t, D_FF), F32),
                   jax.ShapeDtypeStruct((SUBLANE, D_FF), F32)],
        compiler_params=_cparams(("parallel", "arbitrary")),
    )(up, up, up, up, conv_w, conv_b, dact)


def conv_transpose(dc, conv_w, name="conv_transpose"):
    t = dc.shape[0]
    tm, tc = _row_tile(t), FFN_COL

    def body(d_ref, dp_ref, dn_ref, w_ref, o_ref):
        i = pl.program_id(0)
        d = d_ref[...]
        dm1, dp1 = _shifted(d, dp_ref[SUBLANE - 1:SUBLANE, :], dn_ref[0:1, :], i == 0, i == pl.num_programs(0) - 1)
        o_ref[...] = w_ref[0:1, :] * dp1 + w_ref[1:2, :] * d + w_ref[2:3, :] * dm1

    prev, nxt = _halo_specs(tm, tc, t, lambda i, j: j, 0)
    tile = pl.BlockSpec((tm, tc), lambda i, j: (i, j))
    return pl.pallas_call(
        body, name=name, grid=(t // tm, D_FF // tc),
        in_specs=[tile, prev, nxt, pl.BlockSpec((SUBLANE, tc), lambda i, j: (0, j))],
        out_specs=tile, out_shape=jax.ShapeDtypeStruct((t, D_FF), F32),
        compiler_params=_cparams(("parallel", "parallel")),
    )(dc, dc, dc, conv_w)


def loss_head(y, target, name="loss_head"):
    t, d = y.shape
    tm = _row_tile(t)

    def body(y_ref, t_ref, dy_ref, l_ref):
        @pl.when(pl.program_id(0) == 0)
        def _():
            l_ref[...] = jnp.zeros_like(l_ref)
        e = y_ref[...] - t_ref[...]
        dy_ref[...] = e * (1.0 / d)
        rowloss = jnp.sum(e * e, axis=-1, keepdims=True) * (0.5 / d)
        l_ref[...] += jnp.sum(rowloss, axis=0, keepdims=True)

    row = pl.BlockSpec((tm, d), lambda i: (i, 0))
    return pl.pallas_call(
        body, name=name, grid=(t // tm,), in_specs=[row, row],
        out_specs=[row, pl.BlockSpec((1, 1), lambda i: (0, 0))],
        out_shape=[jax.ShapeDtypeStruct((t, d), F32), jax.ShapeDtypeStruct((1, 1), F32)],
        compiler_params=_cparams(("arbitrary",)),
    )(y, target)


def _pad_heads(w, heads, width):
    lead = w.shape[:-1]
    w = w.reshape(*lead, heads, width)
    w = jnp.pad(w, [(0, 0)] * len(lead) + [(0, 0), (0, LANE - width)])
    return w.reshape(*lead, heads * LANE)


def _unpad_heads(w, heads, width):
    lead = w.shape[:-1]
    return w.reshape(*lead, heads, LANE)[..., :width].reshape(*lead, heads * width)


def _pad_rows_heads(w, heads, width):
    return _pad_heads(w.T, heads, width).T


def _unpad_rows_heads(w, heads, width):
    return _unpad_heads(w.T, heads, width).T


_EV_REAL = MLA_Q_RANK + MLA_KV_RANK + MLA_ROPE


def prep_even(wts, dt=BF16):
    w_in_t = wts["w_in_t"]
    w_in_tp = jnp.concatenate([w_in_t[:_EV_REAL], jnp.zeros((EV_RQ - _EV_REAL, D_MODEL), w_in_t.dtype),
                               w_in_t[_EV_REAL:]], axis=0).astype(dt)
    wuq = _pad_heads(wts["w_uq"], MLA_HEADS, MLA_QK).astype(dt)
    ukv = wts["w_ukv"].reshape(MLA_KV_RANK, MLA_HEADS, MLA_NOPE + MLA_V)
    wk = _pad_heads(ukv[..., :MLA_NOPE].reshape(MLA_KV_RANK, -1), MLA_HEADS, MLA_NOPE).astype(dt)
    wv = _pad_heads(ukv[..., MLA_NOPE:].reshape(MLA_KV_RANK, -1), MLA_HEADS, MLA_V).astype(dt)
    w_out = wts["w_out"]
    wa = _pad_rows_heads(w_out[:MLA_HEADS * MLA_V], MLA_HEADS, MLA_V).astype(dt)
    wr = w_out[MLA_HEADS * MLA_V:].astype(dt)
    pad1 = lambda v, n: jnp.pad(v.astype(F32), (0, n - v.shape[0])).reshape(1, n)
    lg = lambda th: jnp.log1p(-jnp.exp2(-th.astype(F32)))
    return dict(
        w_in=w_in_tp.T, w_in_t=w_in_tp, wuq=wuq, wuq_t=wuq.T, wk=wk, wk_t=wk.T, wv=wv, wv_t=wv.T,
        wa=wa, wa_t=wa.T, wr=wr, wr_t=wr.T,
        mix_norm=wts["mix_norm"].astype(F32), q_norm=wts["q_norm"].astype(F32).reshape(1, -1),
        kv_norm=wts["kv_norm"].astype(F32).reshape(1, -1),
        qhn=pad1(wts["q_head_norm"], LANE), khn=pad1(wts["k_head_norm"], LANE),
        la_f=jnp.repeat(lg(wts["theta_fwd"]), RET_DK).reshape(1, -1),
        la_b=jnp.repeat(lg(wts["theta_bwd"]), RET_DK).reshape(1, -1),
        out_norm=wts["ret_out_norm"].astype(F32).reshape(-1),
    )


def prep_odd(wts, dt=BF16):
    w_in_t = wts["w_in_t"]
    w_in_tp = jnp.concatenate([w_in_t, jnp.zeros((OD_IN - w_in_t.shape[0], D_MODEL), w_in_t.dtype)],
                              axis=0).astype(dt)
    hk = GLA_HEADS * GLA_DK
    wg = jnp.zeros((LANE, 2 * hk), F32)
    wg = wg.at[:GLA_GATE_RANK, :hk].set(wts["w_gate_fwd"].astype(F32))
    wg = wg.at[GLA_GATE_RANK:2 * GLA_GATE_RANK, hk:].set(wts["w_gate_bwd"].astype(F32))
    wg = wg.astype(dt)
    bg = jnp.concatenate([wts["b_gate_fwd"], wts["b_gate_bwd"]]).astype(F32).reshape(1, -1)
    w_out = wts["w_out"].astype(dt)
    return dict(w_in=w_in_tp.T, w_in_t=w_in_tp, wg=wg, wg_t=wg.T, bg=bg, w_out=w_out, w_out_t=w_out.T,
                mix_norm=wts["mix_norm"].astype(F32), out_norm=wts["gla_out_norm"].astype(F32).reshape(-1))


def prep_ffn(wts, dt=BF16):
    w_up_t = wts["w_up_t"].astype(dt)
    w_down = wts["w_down"].astype(dt)
    cw = jnp.pad(wts["conv_w"].astype(F32), ((0, SUBLANE - 3), (0, 0)))
    return dict(w_up=w_up_t.T, w_up_t=w_up_t, w_down=w_down, w_down_t=w_down.T, conv_w=cw,
                conv_b=wts["conv_b"].astype(F32).reshape(1, -1), norm=wts["norm"].astype(F32))


_RET = dict(qcb=EV_RQ // LANE, kcb=EV_RK // LANE, vcb=EV_RV // LANE, la_row=True, chunk=RET_CHUNK, dvg=LANE,
            sub=2, qscale=1.0, kscale=RET_DK ** -0.5)
_GLA = dict(qcb=OD_Q // LANE, kcb=OD_K // LANE, vcb=OD_V // GLA_DV, la_row=False, chunk=GLA_CHUNK, dvg=GLA_DV,
            sub=1, qscale=GLA_DK ** -0.5, kscale=1.0)
_FWD_DIR = dict(reverse=False, inclusive=True)
_BWD_DIR = dict(reverse=True, inclusive=False)


def even_fwd(x, w, tabs_mla, tabs_ret, tag):
    h = rmsnorm_fwd(x, w["mix_norm"], name=f"{tag}_norm")
    p = mm_nn(h, w["w_in"], name=f"{tag}_in")
    q, k, v = mla_prep_fwd(p, tabs_mla, w["q_norm"], w["kv_norm"], w["wuq"], w["wk"], w["wv"], w["qhn"], w["khn"],
                           name=f"{tag}_mla_prep")
    o, lse = attn_fwd(q, k, v, name=f"{tag}_attn")
    of, stf = scan_fwd(p, p, p, w["la_f"], lacb=0, rope=tabs_ret, name=f"{tag}_ret_f", **_RET, **_FWD_DIR)
    ob, stb = scan_fwd(p, p, p, w["la_b"], lacb=0, rope=tabs_ret, name=f"{tag}_ret_b", **_RET, **_BWD_DIR)
    r = gated_norm_fwd(of, ob, p, EV_RG // LANE, w["out_norm"], RET_DV, name=f"{tag}_ret_out")
    x1 = mm_nn(o, w["wa"], res=x, name=f"{tag}_out_a")
    x2 = mm_nn(r, w["wr"], res=x1, name=f"{tag}_out_r")
    return x2, dict(x=x, h=h, p=p, q=q, k=k, v=v, o=o, lse=lse, of=of, ob=ob, stf=stf, stb=stb, r=r)


def even_bwd(dx, s, w, tabs_mla, tabs_ret, tag):
    tag = tag + "_b"
    do = mm_nn(dx, w["wa_t"], name=f"{tag}_dout_a")
    dr = mm_nn(dx, w["wr_t"], name=f"{tag}_dout_r")
    d_wa = mm_tn(s["o"], dx, name=f"{tag}_dwa")
    d_wr = mm_tn(s["r"], dx, name=f"{tag}_dwr")
    dq, dk, dv = attn_bwd(s["q"], s["k"], s["v"], s["o"], s["lse"], do, name=f"{tag}_attn")
    (dp_mla, d_wuq, d_wk, d_wv, d_qn, d_kvn, d_qhn, d_khn) = mla_prep_bwd(
        s["p"], tabs_mla, w["q_norm"], w["kv_norm"], w["wuq"], w["wk"], w["wv"], w["wuq_t"], w["wk_t"], w["wv_t"],
        w["qhn"], w["khn"], dq, dk, dv, name=f"{tag}_mla_prep")
    d_o, d_gate, d_gn = gated_norm_bwd(s["of"], s["ob"], s["p"], EV_RG // LANE, w["out_norm"], RET_DV, dr,
                                       name=f"{tag}_ret_out")
    p = s["p"]
    g1 = scan_bwd(p, p, p, w["la_f"], s["stf"], d_o, None, lacb=0, rope=tabs_ret, name=f"{tag}_ret_f",
                  **_RET, **_FWD_DIR)
    g2 = scan_bwd(p, p, p, w["la_b"], s["stb"], d_o, g1[:3], lacb=0, rope=tabs_ret, name=f"{tag}_ret_b",
                  **_RET, **_BWD_DIR)
    dp = jnp.concatenate([dp_mla, g2[0], g2[1], g2[2], d_gate], axis=1)
    dh = mm_nn(dp, w["w_in_t"], name=f"{tag}_dh")
    d_win_t = mm_tn(dp, s["h"], name=f"{tag}_dwin")
    dx_in, d_mix = rmsnorm_bwd(s["x"], w["mix_norm"], dh, dx, name=f"{tag}_norm")
    grads = dict(w_in_t=d_win_t, wuq=d_wuq, wk=d_wk, wv=d_wv, wa=d_wa, wr=d_wr, mix_norm=d_mix, q_norm=d_qn,
                 kv_norm=d_kvn, qhn=d_qhn, khn=d_khn, la_f=g1[3], la_b=g2[3], out_norm=d_gn)
    return dx_in, grads


def odd_fwd(x, w, tag):
    h = rmsnorm_fwd(x, w["mix_norm"], name=f"{tag}_norm")
    p = mm_nn(h, w["w_in"], name=f"{tag}_in")
    la = gla_gate_fwd(p, w["wg"], w["bg"], name=f"{tag}_gate")
    of, stf = scan_fwd(p, p, p, la, lacb=0, name=f"{tag}_gla_f", **_GLA, **_FWD_DIR)
    ob, stb = scan_fwd(p, p, p, la, lacb=N_GROUPS, name=f"{tag}_gla_b", **_GLA, **_BWD_DIR)
    y = gated_norm_fwd(of, ob, p, OD_R // GLA_DV, w["out_norm"], GLA_DV, name=f"{tag}_gla_out")
    x1 = mm_nn(y, w["w_out"], res=x, name=f"{tag}_out")
    return x1, dict(x=x, h=h, p=p, la=la, of=of, ob=ob, stf=stf, stb=stb, y=y)


def odd_bwd(dx, s, w, tag):
    tag = tag + "_b"
    dy = mm_nn(dx, w["w_out_t"], name=f"{tag}_dout")
    d_wout = mm_tn(s["y"], dx, name=f"{tag}_dwout")
    d_o, d_gate, d_gn = gated_norm_bwd(s["of"], s["ob"], s["p"], OD_R // GLA_DV, w["out_norm"], GLA_DV, dy,
                                       name=f"{tag}_gla_out")
    p, la = s["p"], s["la"]
    g1 = scan_bwd(p, p, p, la, s["stf"], d_o, None, lacb=0, name=f"{tag}_gla_f", **_GLA, **_FWD_DIR)
    g2 = scan_bwd(p, p, p, la, s["stb"], d_o, g1[:3], lacb=N_GROUPS, name=f"{tag}_gla_b", **_GLA, **_BWD_DIR)
    dla = jnp.concatenate([g1[3], g2[3]], axis=1)
    d_ga, d_wg, d_bg = gla_gate_bwd(p, w["wg"], w["wg_t"], w["bg"], dla, name=f"{tag}_gate")
    dp = jnp.concatenate([g2[0], g2[1], g2[2], d_gate, d_ga], axis=1)
    dh = mm_nn(dp, w["w_in_t"], name=f"{tag}_dh")
    d_win_t = mm_tn(dp, s["h"], name=f"{tag}_dwin")
    dx_in, d_mix = rmsnorm_bwd(s["x"], w["mix_norm"], dh, dx, name=f"{tag}_norm")
    grads = dict(w_in_t=d_win_t, wg=d_wg, bg=d_bg, w_out=d_wout, mix_norm=d_mix, out_norm=d_gn)
    return dx_in, grads


def ffn_fwd(x, w, tag):
    h = rmsnorm_fwd(x, w["norm"], name=f"{tag}_norm")
    up = mm_nn(h, w["w_up"], name=f"{tag}_up")
    act = ffn_act_fwd(up, w["conv_w"], w["conv_b"], name=f"{tag}_act")
    x1 = mm_nn(act, w["w_down"], res=x, name=f"{tag}_down")
    return x1, dict(x=x, h=h, up=up, act=act)


def ffn_bwd(dx, s, w, tag):
    tag = tag + "_b"
    dact = mm_nn(dx, w["w_down_t"], name=f"{tag}_dact")
    d_wdown = mm_tn(s["act"], dx, name=f"{tag}_dwdown")
    dc, dval, d_conv = ffn_act_bwd(s["up"], w["conv_w"], w["conv_b"], dact, name=f"{tag}_act")
    dgate = conv_transpose(dc, w["conv_w"], name=f"{tag}_convt")
    dh1 = mm_nn(dgate, w["w_up_t"][:D_FF], name=f"{tag}_dh_g")
    dh = mm_nn(dval, w["w_up_t"][D_FF:], res=dh1, name=f"{tag}_dh_v")
    d_wup_t = jnp.concatenate([mm_tn(dgate, s["h"], name=f"{tag}_dwup_g"),
                               mm_tn(dval, s["h"], name=f"{tag}_dwup_v")], axis=0)
    dx_in, d_norm = rmsnorm_bwd(s["x"], w["norm"], dh, dx, name=f"{tag}_norm")
    grads = dict(w_up_t=d_wup_t, w_down=d_wdown, conv_w=d_conv[:3], conv_b=d_conv[3], norm=d_norm)
    return dx_in, grads


def unprep_even_grads(g, theta_fwd, theta_bwd):
    d_win_t = jnp.concatenate([g["w_in_t"][:_EV_REAL], g["w_in_t"][EV_RQ:]], axis=0)
    d_uq = _unpad_heads(g["wuq"], MLA_HEADS, MLA_QK)
    dk_ = _unpad_heads(g["wk"], MLA_HEADS, MLA_NOPE).reshape(MLA_KV_RANK, MLA_HEADS, MLA_NOPE)
    dv_ = _unpad_heads(g["wv"], MLA_HEADS, MLA_V).reshape(MLA_KV_RANK, MLA_HEADS, MLA_V)
    d_ukv = jnp.concatenate([dk_, dv_], axis=-1).reshape(MLA_KV_RANK, -1)
    d_wout = jnp.concatenate([_unpad_rows_heads(g["wa"], MLA_HEADS, MLA_V), g["wr"]], axis=0)

    def dtheta(dla, th):
        dlg = dla.reshape(RET_HEADS, RET_DK).sum(axis=-1)
        e = jnp.exp2(-th.astype(F32))
        return dlg * (e * math.log(2.0)) / (1.0 - e)

    return dict(mix_norm=g["mix_norm"].reshape(-1), w_in_t=d_win_t, q_norm=g["q_norm"].reshape(-1),
                kv_norm=g["kv_norm"].reshape(-1), w_uq=d_uq, w_ukv=d_ukv, q_head_norm=g["qhn"].reshape(-1)[:MLA_QK],
                k_head_norm=g["khn"].reshape(-1)[:MLA_QK], theta_fwd=dtheta(g["la_f"], theta_fwd),
                theta_bwd=dtheta(g["la_b"], theta_bwd), ret_out_norm=g["out_norm"].reshape(RET_HEADS, RET_DV),
                w_out=d_wout)


def unprep_odd_grads(g):
    hk = GLA_HEADS * GLA_DK
    return dict(mix_norm=g["mix_norm"].reshape(-1), w_in_t=g["w_in_t"][:OD_GA + 2 * GLA_GATE_RANK],
                w_gate_fwd=g["wg"][:GLA_GATE_RANK, :hk], b_gate_fwd=g["bg"].reshape(-1)[:hk],
                w_gate_bwd=g["wg"][GLA_GATE_RANK:2 * GLA_GATE_RANK, hk:], b_gate_bwd=g["bg"].reshape(-1)[hk:],
                gla_out_norm=g["out_norm"].reshape(GLA_HEADS, GLA_DV), w_out=g["w_out"])


def unprep_ffn_grads(g):
    return dict(norm=g["norm"].reshape(-1), w_up_t=g["w_up_t"], conv_w=g["conv_w"], conv_b=g["conv_b"],
                w_down=g["w_down"])


def _mesh_pos():
    return tuple(lax.axis_index(n) for n in MESH_AXES)


def _slot(px, py, pc):
    return 4 * px + 2 * py + pc


def all_gather_blocks(blk, name="weight_all_gather"):
    r, w = blk.shape

    def body(x_ref, out_ref, send_sems, recv_sems, local_sem):
        x, y, c = _mesh_pos()
        me, sibling = (x, y, c), (x, y, 1 - c)
        chips = [(1 - x, y), (x, 1 - y), (1 - x, 1 - y)]

        def copy(k, block, to, src=None):
            dst = out_ref.at[_slot(*block)]
            return pltpu.make_async_remote_copy(
                src_ref=dst if src is None else src, dst_ref=dst, send_sem=send_sems.at[k],
                recv_sem=recv_sems.at[k], device_id=to, device_id_type=pl.DeviceIdType.MESH)

        mine = pltpu.make_async_copy(x_ref, out_ref.at[_slot(*me)], local_sem)
        mine.start()
        first = [copy(0, me, sibling, src=x_ref)]
        first += [copy(1 + j, me, (*chip, c), src=x_ref) for j, chip in enumerate(chips)]
        for cp in first:
            cp.start()
        passed = [copy(4 + j, (*chip, c), sibling) for j, chip in enumerate(chips)]
        for j, chip in enumerate(chips):
            copy(1 + j, (*chip, c), me).wait_recv()
            passed[j].start()
        copy(0, sibling, me).wait_recv()
        for j, chip in enumerate(chips):
            copy(4 + j, (*chip, 1 - c), me).wait_recv()
        for cp in first + passed:
            cp.wait_send()
        mine.wait()

    return pl.pallas_call(
        body, name=name,
        out_shape=jax.ShapeDtypeStruct((N_DEV, r, w), blk.dtype),
        in_specs=[pl.BlockSpec(memory_space=pl.ANY)],
        out_specs=pl.BlockSpec(memory_space=pl.ANY),
        scratch_shapes=[pltpu.SemaphoreType.DMA((7,)), pltpu.SemaphoreType.DMA((7,)), pltpu.SemaphoreType.DMA],
    )(blk)


def all_to_all_blocks(send, name="grad_all_to_all"):
    _, r, w = send.shape

    def body(s_ref, r_ref, send_sems, recv_sems, local_sem):
        x, y, c = _mesh_pos()
        me = _slot(x, y, c)
        mine = pltpu.make_async_copy(s_ref.at[me], r_ref.at[me], local_sem)
        mine.start()
        copies = []
        for k in range(1, N_DEV):
            px = 1 - x if (k >> 2) & 1 else x
            py = 1 - y if (k >> 1) & 1 else y
            pc = 1 - c if k & 1 else c
            cp = pltpu.make_async_remote_copy(
                src_ref=s_ref.at[_slot(px, py, pc)], dst_ref=r_ref.at[me], send_sem=send_sems.at[k - 1],
                recv_sem=recv_sems.at[k - 1], device_id=(px, py, pc), device_id_type=pl.DeviceIdType.MESH)
            cp.start()
            copies.append(cp)
        for cp in copies:
            cp.wait()
        mine.wait()

    return pl.pallas_call(
        body, name=name,
        out_shape=jax.ShapeDtypeStruct((N_DEV, r, w), send.dtype),
        in_specs=[pl.BlockSpec(memory_space=pl.ANY)],
        out_specs=pl.BlockSpec(memory_space=pl.ANY),
        scratch_shapes=[pltpu.SemaphoreType.DMA((7,)), pltpu.SemaphoreType.DMA((7,)), pltpu.SemaphoreType.DMA],
    )(send)


FLAT_W = 1024
FLAT_TILE = 256


def sum_slots(recv, name="grad_sum"):
    _, r, w = recv.shape

    def body(r_ref, o_ref):
        acc = r_ref[0].astype(F32)
        for k in range(1, N_DEV):
            acc = acc + r_ref[k].astype(F32)
        o_ref[...] = acc

    tr = _slab_tile(r)
    return pl.pallas_call(
        body, name=name, grid=(r // tr,),
        in_specs=[pl.BlockSpec((N_DEV, tr, w), lambda i: (0, i, 0))],
        out_specs=pl.BlockSpec((tr, w), lambda i: (i, 0)),
        out_shape=jax.ShapeDtypeStruct((r, w), F32),
        compiler_params=_cparams(("parallel",)),
    )(recv)


def _slab_tile(r):
    return max(t for t in range(SUBLANE, FLAT_TILE + 1, SUBLANE) if r % t == 0)


def adamw(wf, gf, mf, vf, name="adamw"):
    r, w = wf.shape
    tr = _slab_tile(r)

    def body(w_ref, g_ref, m_ref, v_ref, d_ref, m_out, v_out):
        g = g_ref[...]
        m = ADAM_B1 * m_ref[...] + (1.0 - ADAM_B1) * g
        v = ADAM_B2 * v_ref[...] + (1.0 - ADAM_B2) * (g * g)
        m_hat = m / (1.0 - ADAM_B1 ** ADAM_STEP)
        v_hat = v / (1.0 - ADAM_B2 ** ADAM_STEP)
        d_ref[...] = -ADAM_LR * (m_hat / (jnp.sqrt(v_hat) + ADAM_EPS) + ADAM_WD * w_ref[...])
        m_out[...] = m
        v_out[...] = v

    tile = pl.BlockSpec((tr, w), lambda i: (i, 0))
    return pl.pallas_call(
        body, name=name, grid=(r // tr,), in_specs=[tile] * 4, out_specs=[tile] * 3,
        out_shape=[jax.ShapeDtypeStruct((r, w), F32)] * 3,
        compiler_params=_cparams(("parallel",)),
    )(wf, gf, mf, vf)


ROW_FORM = [("w_in_even", "T"), ("w_out_even", "R"), ("w_in_odd", "T"), ("w_out_odd", "R"), ("ffn_w_up", "T"),
            ("ffn_w_down", "R")]
SHARDED_MID = [("mla_w_uq", 2), ("mla_w_ukv", 2)]
SHARDED_SMALL = [("mix_norm_odd", 1), ("gla_w_gate_fwd", 2), ("gla_b_gate_fwd", 1), ("gla_w_gate_bwd", 2),
                 ("gla_b_gate_bwd", 1), ("gla_out_norm", 2), ("ffn_conv_w", 2)]
REPLICATED = ["mix_norm_even", "mla_q_norm", "mla_kv_norm", "mla_q_head_norm", "mla_k_head_norm", "ret_theta_fwd",
              "ret_theta_bwd", "ret_out_norm", "ffn_norm", "ffn_conv_b"]
WEIGHT_NAMES = ["mix_norm_even", "w_in_even", "mla_q_norm", "mla_kv_norm", "mla_w_uq", "mla_w_ukv",
                "mla_q_head_norm", "mla_k_head_norm", "ret_theta_fwd", "ret_theta_bwd", "ret_out_norm", "w_out_even",
                "mix_norm_odd", "w_in_odd", "gla_w_gate_fwd", "gla_b_gate_fwd", "gla_w_gate_bwd", "gla_b_gate_bwd",
                "gla_out_norm", "w_out_odd", "ffn_norm", "ffn_w_up", "ffn_conv_w", "ffn_conv_b", "ffn_w_down"]


def _round_up(n, m):
    return -(-n // m) * m


def _pack_rows(parts, rows):
    flat = jnp.concatenate(parts, axis=-1)
    pad = rows * FLAT_W - flat.shape[-1]
    flat = jnp.pad(flat, [(0, 0)] * (flat.ndim - 1) + [(0, pad)])
    return flat.reshape(*flat.shape[:-1], rows, FLAT_W)


def _row_form(v, form):
    if form == "T":
        v = jnp.swapaxes(v, 1, 2)
    return v.reshape(-1, v.shape[-1])


def _row_counts(local_shapes):
    return [local_shapes[n][0] * local_shapes[n][2 if f == "T" else 1] for n, f in ROW_FORM]


def _tail_layout(local_shapes):
    n_sh = sum(math.prod(local_shapes[n]) for n, _ in SHARDED_MID + SHARDED_SMALL)
    n_rep = sum(math.prod(local_shapes[n]) for n in REPLICATED)
    sh_rows = _round_up(-(-n_sh // FLAT_W), SUBLANE)
    rep_rows = _round_up(-(-n_rep // FLAT_W), SUBLANE)
    return sh_rows, rep_rows, _round_up(sh_rows + rep_rows, FLAT_TILE)


def pack_tail(vals, local_shapes):
    sh_rows, rep_rows, rows = _tail_layout(local_shapes)
    sh = _pack_rows([vals[n].astype(F32).reshape(-1) for n, _ in SHARDED_MID + SHARDED_SMALL], sh_rows)
    rep = _pack_rows([vals[n].astype(F32).reshape(-1) for n in REPLICATED], rep_rows)
    return jnp.concatenate([sh, rep, jnp.zeros((rows - sh_rows - rep_rows, FLAT_W), F32)], axis=0)


def unpack_tail(tail, local_shapes):
    sh_rows, rep_rows, _ = _tail_layout(local_shapes)
    out = {}
    for names, flat in (([n for n, _ in SHARDED_MID + SHARDED_SMALL], tail[:sh_rows].reshape(-1)),
                        (REPLICATED, tail[sh_rows:sh_rows + rep_rows].reshape(-1))):
        off = 0
        for n in names:
            k = math.prod(local_shapes[n])
            out[n] = flat[off:off + k].reshape(local_shapes[n])
            off += k
    return out


def unpack_rows(slab, local_shapes):
    out = {}
    off = 0
    for (n, form), rows in zip(ROW_FORM, _row_counts(local_shapes)):
        loc = local_shapes[n]
        piece = slab[off:off + rows]
        if form == "T":
            piece = jnp.swapaxes(piece.reshape(loc[0], loc[2], loc[1]), 1, 2)
        out[n] = piece.reshape(loc)
        off += rows
    return out


def pack_grad_blocks(full_grads, local_shapes):
    sh_rows, rep_rows, rows = _tail_layout(local_shapes)
    blocks = []
    for n, form in ROW_FORM:
        g = full_grads[n].astype(BF16)
        layers, total = g.shape[0], g.shape[1]
        g = g.reshape(layers, N_DEV, total // N_DEV, FLAT_W)
        blocks.append(jnp.swapaxes(g, 0, 1).reshape(N_DEV, -1, FLAT_W))
    parts = []
    for n, ax in SHARDED_MID + SHARDED_SMALL:
        g = full_grads[n].astype(F32)
        loc = local_shapes[n]
        g = g.reshape(*g.shape[:ax], N_DEV, loc[ax], *g.shape[ax + 1:])
        parts.append(jnp.moveaxis(g, ax, 0).reshape(N_DEV, -1))
    sh = _pack_rows(parts, sh_rows)
    rep = _pack_rows([full_grads[n].astype(F32).reshape(-1) for n in REPLICATED], rep_rows)
    rep = jnp.broadcast_to(rep[None], (N_DEV, rep_rows, FLAT_W))
    pad = jnp.zeros((N_DEV, rows - sh_rows - rep_rows, FLAT_W), F32)
    tail = jnp.concatenate([sh, rep, pad], axis=1).astype(BF16)
    return jnp.concatenate(blocks + [tail], axis=1)


def pack_gather_block(vals, local_shapes):
    big = jnp.concatenate([_row_form(vals[n].astype(BF16), f) for n, f in ROW_FORM], axis=0)
    mid = [vals[n].astype(BF16).reshape(-1) for n, _ in SHARDED_MID]
    small = jnp.concatenate([vals[n].astype(F32).reshape(-1) for n, _ in SHARDED_SMALL])
    small = lax.bitcast_convert_type(small, BF16).reshape(-1)
    n = sum(v.shape[0] for v in mid) + small.shape[0]
    tail = _pack_rows(mid + [small], _round_up(-(-n // FLAT_W), 2 * SUBLANE))
    return jnp.concatenate([big, tail], axis=0)


def unpack_gathered(gathered, local_shapes):
    out = {}
    off = 0
    for (n, form), rows in zip(ROW_FORM, _row_counts(local_shapes)):
        layers = local_shapes[n][0]
        piece = gathered[:, off:off + rows].reshape(N_DEV, layers, rows // layers, FLAT_W)
        out[n] = jnp.swapaxes(piece, 0, 1).reshape(layers, N_DEV * (rows // layers), FLAT_W)
        off += rows
    flat = gathered[:, off:].reshape(N_DEV, -1)
    off = 0

    def full(piece, n, ax):
        loc = local_shapes[n]
        piece = jnp.moveaxis(piece.reshape(N_DEV, *loc), 0, ax)
        return piece.reshape(*loc[:ax], N_DEV * loc[ax], *loc[ax + 1:])

    for n, ax in SHARDED_MID:
        k = math.prod(local_shapes[n])
        out[n] = full(flat[:, off:off + k], n, ax)
        off += k
    for n, ax in SHARDED_SMALL:
        k = math.prod(local_shapes[n])
        pairs = flat[:, off:off + 2 * k].reshape(N_DEV, k, 2)
        out[n] = full(lax.bitcast_convert_type(pairs, F32), n, ax)
        off += 2 * k
    return out


def local_step(x, positions, target, fw):
    tabs_mla = rope_tables(positions, MLA_NOPE, MLA_ROPE // 2, LANE)
    tabs_ret = rope_tables(positions, 0, RET_DK // 2, RET_DK)
    layers = []
    for layer in range(DEPTH):
        i = layer // 2
        if layer % 2 == 0:
            mw = prep_even(dict(
                w_in_t=fw["w_in_even"][i], w_uq=fw["mla_w_uq"][i], w_ukv=fw["mla_w_ukv"][i], w_out=fw["w_out_even"][i],
                mix_norm=fw["mix_norm_even"][i], q_norm=fw["mla_q_norm"][i], kv_norm=fw["mla_kv_norm"][i],
                q_head_norm=fw["mla_q_head_norm"][i], k_head_norm=fw["mla_k_head_norm"][i],
                theta_fwd=fw["ret_theta_fwd"][i], theta_bwd=fw["ret_theta_bwd"][i],
                ret_out_norm=fw["ret_out_norm"][i]))
        else:
            mw = prep_odd(dict(
                w_in_t=fw["w_in_odd"][i], w_gate_fwd=fw["gla_w_gate_fwd"][i], b_gate_fwd=fw["gla_b_gate_fwd"][i],
                w_gate_bwd=fw["gla_w_gate_bwd"][i], b_gate_bwd=fw["gla_b_gate_bwd"][i],
                gla_out_norm=fw["gla_out_norm"][i], w_out=fw["w_out_odd"][i], mix_norm=fw["mix_norm_odd"][i]))
        fwt = prep_ffn(dict(norm=fw["ffn_norm"][layer], w_up_t=fw["ffn_w_up"][layer], conv_w=fw["ffn_conv_w"][layer],
                            conv_b=fw["ffn_conv_b"][layer], w_down=fw["ffn_w_down"][layer]))
        layers.append((mw, fwt))

    saved = []
    for layer, (mw, fwt) in enumerate(layers):
        if layer % 2 == 0:
            x, sm = even_fwd(x, mw, tabs_mla, tabs_ret, f"l{layer}_mix")
        else:
            x, sm = odd_fwd(x, mw, f"l{layer}_mix")
        x, sf = ffn_fwd(x, fwt, f"l{layer}_ffn")
        saved.append((sm, sf))

    dx, loss = loss_head(x, target, name="loss_head")

    per_layer = [None] * DEPTH
    for layer in reversed(range(DEPTH)):
        mw, fwt = layers[layer]
        sm, sf = saved[layer]
        dx, gf = ffn_bwd(dx, sf, fwt, f"l{layer}_ffn")
        if layer % 2 == 0:
            dx, gm = even_bwd(dx, sm, mw, tabs_mla, tabs_ret, f"l{layer}_mix")
            gm = unprep_even_grads(gm, fw["ret_theta_fwd"][layer // 2], fw["ret_theta_bwd"][layer // 2])
        else:
            dx, gm = odd_bwd(dx, sm, mw, f"l{layer}_mix")
            gm = unprep_odd_grads(gm)
        per_layer[layer] = (gm, unprep_ffn_grads(gf))

    ev = [per_layer[l][0] for l in range(0, DEPTH, 2)]
    od = [per_layer[l][0] for l in range(1, DEPTH, 2)]
    ff = [per_layer[l][1] for l in range(DEPTH)]
    st = lambda lst, key: jnp.stack([g[key] for g in lst])
    grads = {
        "mix_norm_even": st(ev, "mix_norm"), "w_in_even": st(ev, "w_in_t"), "mla_q_norm": st(ev, "q_norm"),
        "mla_kv_norm": st(ev, "kv_norm"), "mla_w_uq": st(ev, "w_uq"), "mla_w_ukv": st(ev, "w_ukv"),
        "mla_q_head_norm": st(ev, "q_head_norm"), "mla_k_head_norm": st(ev, "k_head_norm"),
        "ret_theta_fwd": st(ev, "theta_fwd"), "ret_theta_bwd": st(ev, "theta_bwd"),
        "ret_out_norm": st(ev, "ret_out_norm"), "w_out_even": st(ev, "w_out"),
        "mix_norm_odd": st(od, "mix_norm"), "w_in_odd": st(od, "w_in_t"), "gla_w_gate_fwd": st(od, "w_gate_fwd"),
        "gla_b_gate_fwd": st(od, "b_gate_fwd"), "gla_w_gate_bwd": st(od, "w_gate_bwd"),
        "gla_b_gate_bwd": st(od, "b_gate_bwd"), "gla_out_norm": st(od, "gla_out_norm"), "w_out_odd": st(od, "w_out"),
        "ffn_norm": st(ff, "norm"), "ffn_w_up": st(ff, "w_up_t"), "ffn_conv_w": st(ff, "conv_w"),
        "ffn_conv_b": st(ff, "conv_b"), "ffn_w_down": st(ff, "w_down"),
    }
    return loss, dx, grads


def kernel(x, positions, mix_norm_even, w_in_even, mla_q_norm, mla_kv_norm, mla_w_uq, mla_w_ukv, mla_q_head_norm, mla_k_head_norm, ret_theta_fwd, ret_theta_bwd, ret_out_norm, w_out_even, mix_norm_odd, w_in_odd, gla_w_gate_fwd, gla_b_gate_fwd, gla_w_gate_bwd, gla_b_gate_bwd, gla_out_norm, w_out_odd, ffn_norm, ffn_w_up, ffn_conv_w, ffn_conv_b, ffn_w_down, loss_target, m_mix_norm_even, m_w_in_even, m_mla_q_norm, m_mla_kv_norm, m_mla_w_uq, m_mla_w_ukv, m_mla_q_head_norm, m_mla_k_head_norm, m_ret_theta_fwd, m_ret_theta_bwd, m_ret_out_norm, m_w_out_even, m_mix_norm_odd, m_w_in_odd, m_gla_w_gate_fwd, m_gla_b_gate_fwd, m_gla_w_gate_bwd, m_gla_b_gate_bwd, m_gla_out_norm, m_w_out_odd, m_ffn_norm, m_ffn_w_up, m_ffn_conv_w, m_ffn_conv_b, m_ffn_w_down, v_mix_norm_even, v_w_in_even, v_mla_q_norm, v_mla_kv_norm, v_mla_w_uq, v_mla_w_ukv, v_mla_q_head_norm, v_mla_k_head_norm, v_ret_theta_fwd, v_ret_theta_bwd, v_ret_out_norm, v_w_out_even, v_mix_norm_odd, v_w_in_odd, v_gla_w_gate_fwd, v_gla_b_gate_fwd, v_gla_w_gate_bwd, v_gla_b_gate_bwd, v_gla_out_norm, v_w_out_odd, v_ffn_norm, v_ffn_w_up, v_ffn_conv_w, v_ffn_conv_b, v_ffn_w_down):
    a = dict(locals())
    wts = {n: a[n] for n in WEIGHT_NAMES}
    local_shapes = {n: tuple(wts[n].shape) for n in WEIGHT_NAMES}

    gathered = all_gather_blocks(pack_gather_block(wts, local_shapes))
    fw = unpack_gathered(gathered, local_shapes)
    for n in REPLICATED:
        fw[n] = wts[n]

    loss, grad_x, grads = local_step(x[0], positions, loss_target[0], fw)

    recv = all_to_all_blocks(pack_grad_blocks(grads, local_shapes))
    g_slab = sum_slots(recv)
    n_big = sum(_row_counts(local_shapes))
    ms = {n: a["m_" + n] for n in WEIGHT_NAMES}
    vs = {n: a["v_" + n] for n in WEIGHT_NAMES}

    g_out = unpack_rows(g_slab[:n_big], local_shapes)
    d_out, m_out, v_out = {}, {}, {}
    for n, _ in ROW_FORM:
        loc = local_shapes[n]
        two_d = lambda t: t.reshape(-1, loc[-1])
        d, m, v = adamw(two_d(wts[n]), two_d(g_out[n]), two_d(ms[n]), two_d(vs[n]), name=f"adamw_{n}")
        d_out[n], m_out[n], v_out[n] = d.reshape(loc), m.reshape(loc), v.reshape(loc)
    g_tail = g_slab[n_big:]
    d_tail, m_tail, v_tail = adamw(pack_tail(wts, local_shapes), g_tail, pack_tail(ms, local_shapes),
                                   pack_tail(vs, local_shapes), name="adamw_small")
    g_out.update(unpack_tail(g_tail, local_shapes))
    d_out.update(unpack_tail(d_tail, local_shapes))
    m_out.update(unpack_tail(m_tail, local_shapes))
    v_out.update(unpack_tail(v_tail, local_shapes))
    total = lax.psum(loss[0, 0], MESH_AXES)
    return (total, grad_x[None], *[g_out[n] for n in WEIGHT_NAMES], *[d_out[n] for n in WEIGHT_NAMES],
            *[m_out[n] for n in WEIGHT_NAMES], *[v_out[n] for n in WEIGHT_NAMES])
```

```python
import math

import jax
import jax.numpy as jnp
from jax import lax
from jax.experimental import pallas as pl
from jax.experimental.pallas import tpu as pltpu

F32 = jnp.float32
BF16 = jnp.bfloat16

D_MODEL = 1024
DEPTH = 4
N_DEV = 8
MESH_AXES = ("x", "y", "c")

MLA_HEADS = 8
MLA_Q_RANK = 384
MLA_KV_RANK = 256
MLA_NOPE = 64
MLA_ROPE = 32
MLA_V = 64
MLA_QK = MLA_NOPE + MLA_ROPE
RET_HEADS = 8
RET_DK = 64
RET_DV = 64
RET_CHUNK = 128
GLA_HEADS = 4
GLA_DK = 128
GLA_DV = 256
GLA_GATE_RANK = 16
GLA_TAU = 16.0
GLA_CHUNK = 64
D_FF = 2816
ROPE_THETA = 10000.0
EPS = 1e-6

ADAM_LR = 0.001
ADAM_B1 = 0.9
ADAM_B2 = 0.999
ADAM_EPS = 1e-08
ADAM_WD = 0.01
ADAM_STEP = 10

LANE = 128
SUBLANE = 8
ROW_TILE = 512
VMEM_LIMIT = 56 * 1024 * 1024
WEIGHT_TILE_BYTES = 8 * 1024 * 1024

EV_CQ, EV_CKV, EV_KR, EV_RQ, EV_RK, EV_RV, EV_RG, EV_IN = 0, 384, 640, 768, 1280, 1792, 2304, 2816
OD_Q, OD_K, OD_V, OD_R, OD_GA, OD_IN = 0, 512, 1024, 2048, 3072, 3200
N_GROUPS = 4


def _cparams(sem):
    return pltpu.CompilerParams(dimension_semantics=sem, vmem_limit_bytes=VMEM_LIMIT)


def _dot(a, b):
    return jnp.dot(a.astype(BF16), b.astype(BF16), preferred_element_type=F32)


def _dot_nt(a, b):
    return lax.dot_general(a.astype(BF16), b.astype(BF16), (((1,), (1,)), ((), ())), preferred_element_type=F32)


def _dot_tn(a, b):
    return lax.dot_general(a.astype(BF16), b.astype(BF16), (((0,), (0,)), ((), ())), preferred_element_type=F32)


def _sigmoid(x):
    return 1.0 / (1.0 + jnp.exp(-x))


def _col_tile(k, n, itemsize=2):
    best = LANE
    for t in range(LANE, n + 1, LANE):
        if n % t == 0 and k * t * itemsize <= WEIGHT_TILE_BYTES:
            best = t
    return best if n % LANE == 0 else n


def _row_tile(m):
    return min(ROW_TILE, m)


def mm_nn(a, b, res=None, out_dtype=F32, name="mm_nn"):
    m, k = a.shape
    n = b.shape[1]
    tm, tn = _row_tile(m), _col_tile(k, n)

    def body(*refs):
        if res is None:
            a_ref, b_ref, o_ref = refs
        else:
            a_ref, b_ref, r_ref, o_ref = refs
        acc = _dot(a_ref[...], b_ref[...])
        if res is not None:
            acc = acc + r_ref[...].astype(F32)
        o_ref[...] = acc.astype(out_dtype)

    in_specs = [pl.BlockSpec((tm, k), lambda j, i: (i, 0)), pl.BlockSpec((k, tn), lambda j, i: (0, j))]
    args = [a, b]
    if res is not None:
        in_specs.append(pl.BlockSpec((tm, tn), lambda j, i: (i, j)))
        args.append(res)
    return pl.pallas_call(
        body, name=name, grid=(n // tn, m // tm), in_specs=in_specs,
        out_specs=pl.BlockSpec((tm, tn), lambda j, i: (i, j)),
        out_shape=jax.ShapeDtypeStruct((m, n), out_dtype),
        compiler_params=_cparams(("parallel", "parallel")),
    )(*args)


def mm_tn(a, b, name="mm_tn"):
    t, k = a.shape
    n = b.shape[1]
    tt = min(2 * ROW_TILE, t)
    tk = k if k <= 1024 else _col_tile(1024, k, 4)
    tn = n if n <= 1024 else _col_tile(1024, n, 4)

    def body(a_ref, b_ref, o_ref):
        @pl.when(pl.program_id(2) == 0)
        def _():
            o_ref[...] = jnp.zeros_like(o_ref)
        o_ref[...] += _dot_tn(a_ref[...], b_ref[...])

    return pl.pallas_call(
        body, name=name, grid=(k // tk, n // tn, t // tt),
        in_specs=[pl.BlockSpec((tt, tk), lambda i, j, s: (s, i)), pl.BlockSpec((tt, tn), lambda i, j, s: (s, j))],
        out_specs=pl.BlockSpec((tk, tn), lambda i, j, s: (i, j)),
        out_shape=jax.ShapeDtypeStruct((k, n), F32),
        compiler_params=_cparams(("parallel", "parallel", "arbitrary")),
    )(a, b)


def rmsnorm_fwd(x, g, name="rmsnorm_fwd"):
    t, d = x.shape
    tm = _row_tile(t)

    def body(x_ref, g_ref, h_ref):
        xv = x_ref[...]
        r = lax.rsqrt(jnp.mean(xv * xv, axis=-1, keepdims=True) + EPS)
        h_ref[...] = (xv * r * g_ref[...]).astype(BF16)

    return pl.pallas_call(
        body, name=name, grid=(t // tm,),
        in_specs=[pl.BlockSpec((tm, d), lambda i: (i, 0)), pl.BlockSpec((1, d), lambda i: (0, 0))],
        out_specs=pl.BlockSpec((tm, d), lambda i: (i, 0)),
        out_shape=jax.ShapeDtypeStruct((t, d), BF16),
        compiler_params=_cparams(("parallel",)),
    )(x, g.reshape(1, d))


def rmsnorm_bwd(x, g, dh, dres, name="rmsnorm_bwd"):
    t, d = x.shape
    tm = _row_tile(t)

    def body(x_ref, g_ref, dh_ref, dres_ref, dx_ref, dg_ref):
        @pl.when(pl.program_id(0) == 0)
        def _():
            dg_ref[...] = jnp.zeros_like(dg_ref)
        xv = x_ref[...]
        r = lax.rsqrt(jnp.mean(xv * xv, axis=-1, keepdims=True) + EPS)
        xh = xv * r
        dhv = dh_ref[...]
        dg_ref[...] += jnp.sum(dhv * xh, axis=0, keepdims=True)
        dxh = dhv * g_ref[...]
        dx_ref[...] = dres_ref[...] + r * (dxh - xh * jnp.mean(dxh * xh, axis=-1, keepdims=True))

    row = pl.BlockSpec((tm, d), lambda i: (i, 0))
    vec = pl.BlockSpec((1, d), lambda i: (0, 0))
    return pl.pallas_call(
        body, name=name, grid=(t // tm,),
        in_specs=[row, vec, row, row], out_specs=[row, vec],
        out_shape=[jax.ShapeDtypeStruct((t, d), F32), jax.ShapeDtypeStruct((1, d), F32)],
        compiler_params=_cparams(("arbitrary",)),
    )(x, g.reshape(1, d), dh, dres)


def _rope_apply(x, cos, s1, s2, half):
    return x * cos + pltpu.roll(x, LANE - half, 1) * s1 + pltpu.roll(x, half, 1) * s2


def _rope_transpose(dy, cos, s1, s2, half):
    return dy * cos + pltpu.roll(dy * s1, half, 1) + pltpu.roll(dy * s2, LANE - half, 1)


def rope_tables(positions, lane_start, half, period):
    pos = positions.reshape(-1).astype(F32)
    inv = ROPE_THETA ** (-jnp.arange(half, dtype=F32) / half)
    ang = pos[:, None] * inv[None, :]
    cos, sin = jnp.cos(ang), jnp.sin(ang)
    t = pos.shape[0]
    pre = lane_start
    post = period - lane_start - 2 * half
    ones = lambda n: jnp.ones((t, n), F32)
    zeros = lambda n: jnp.zeros((t, n), F32)
    c = jnp.concatenate([ones(pre), cos, cos, ones(post)], axis=1)
    a = jnp.concatenate([zeros(pre), -sin, zeros(half), zeros(post)], axis=1)
    b = jnp.concatenate([zeros(pre), zeros(half), sin, zeros(post)], axis=1)
    rep = LANE // period
    return tuple(jnp.tile(v, (1, rep)) for v in (c, a, b))


def _mla_forward_tile(p, cos, s1, s2, qn_g, kvn_g, wuq, wk, wv, qhn, khn):
    cq = p[:, EV_CQ:EV_CKV]
    ckv = p[:, EV_CKV:EV_KR]
    kr = p[:, EV_KR:EV_RQ]
    rq = lax.rsqrt(jnp.mean(cq * cq, axis=-1, keepdims=True) + EPS)
    rkv = lax.rsqrt(jnp.mean(ckv * ckv, axis=-1, keepdims=True) + EPS)
    qn = cq * rq * qn_g
    kvn = ckv * rkv * kvn_g
    q_raw = _dot(qn, wuq)
    k_raw = _dot(kvn, wk)
    v = _dot(kvn, wv)
    krp = pltpu.roll(kr, MLA_NOPE, 1)
    return cq, ckv, rq, rkv, qn, kvn, q_raw, k_raw, v, krp


def _head_norm(xh, g):
    r = lax.rsqrt(jnp.sum(xh * xh, axis=-1, keepdims=True) * (1.0 / MLA_QK) + EPS)
    return xh * r * g, r


def mla_prep_fwd(p, tabs, qn_g, kvn_g, wuq, wk, wv, qhn, khn, name="mla_prep_fwd"):
    t = p.shape[0]
    tm = _row_tile(t)
    hw = MLA_HEADS * LANE

    def body(p_ref, c_ref, s1_ref, s2_ref, qn_ref, kvn_ref, wuq_ref, wk_ref, wv_ref, qhn_ref, khn_ref,
             q_out, k_out, v_out):
        cos, s1, s2 = c_ref[...], s1_ref[...], s2_ref[...]
        (_, _, _, _, _, _, q_raw, k_raw, v, krp) = _mla_forward_tile(
            p_ref[...], cos, s1, s2, qn_ref[...], kvn_ref[...], wuq_ref[...], wk_ref[...], wv_ref[...],
            qhn_ref[...], khn_ref[...])
        v_out[...] = v.astype(BF16)
        for h in range(MLA_HEADS):
            sl = slice(h * LANE, (h + 1) * LANE)
            qh, _ = _head_norm(q_raw[:, sl], qhn_ref[...])
            kh, _ = _head_norm(k_raw[:, sl] + krp, khn_ref[...])
            q_out[:, sl] = (_rope_apply(qh, cos, s1, s2, MLA_ROPE // 2) * ATTN_QSCALE).astype(BF16)
            k_out[:, sl] = _rope_apply(kh, cos, s1, s2, MLA_ROPE // 2).astype(BF16)

    row = lambda w: pl.BlockSpec((tm, w), lambda i: (i, 0))
    full = lambda a: pl.BlockSpec(a.shape, lambda i: (0,) * a.ndim)
    ws = [qn_g, kvn_g, wuq, wk, wv, qhn, khn]
    return pl.pallas_call(
        body, name=name, grid=(t // tm,),
        in_specs=[row(EV_RQ), row(LANE), row(LANE), row(LANE)] + [full(w) for w in ws],
        out_specs=[row(hw)] * 3,
        out_shape=[jax.ShapeDtypeStruct((t, hw), BF16)] * 3,
        compiler_params=_cparams(("parallel",)),
    )(p, *tabs, *ws)


def mla_prep_bwd(p, tabs, qn_g, kvn_g, wuq, wk, wv, wuq_t, wk_t, wv_t, qhn, khn, dq, dk, dv,
                 name="mla_prep_bwd"):
    t = p.shape[0]
    tm = _row_tile(t)
    hw = MLA_HEADS * LANE

    def body(p_ref, c_ref, s1_ref, s2_ref, qn_ref, kvn_ref, wuq_ref, wk_ref, wv_ref, wuqt_ref, wkt_ref, wvt_ref,
             qhn_ref, khn_ref, dq_ref, dk_ref, dv_ref,
             dp_ref, dwuq_ref, dwk_ref, dwv_ref, dqn_ref, dkvn_ref, dqhn_ref, dkhn_ref, dqraw_s, dkraw_s):
        @pl.when(pl.program_id(0) == 0)
        def _():
            for r in (dwuq_ref, dwk_ref, dwv_ref, dqn_ref, dkvn_ref, dqhn_ref, dkhn_ref):
                r[...] = jnp.zeros_like(r)
        cos, s1, s2 = c_ref[...], s1_ref[...], s2_ref[...]
        qhn_v, khn_v = qhn_ref[...], khn_ref[...]
        (cq, ckv, rq, rkv, qn, kvn, q_raw, k_raw, _, krp) = _mla_forward_tile(
            p_ref[...], cos, s1, s2, qn_ref[...], kvn_ref[...], wuq_ref[...], wk_ref[...], wv_ref[...],
            qhn_v, khn_v)
        half = MLA_ROPE // 2
        dkr_sum = jnp.zeros((tm, LANE), F32)
        dqhn_acc = jnp.zeros((1, LANE), F32)
        dkhn_acc = jnp.zeros((1, LANE), F32)
        for h in range(MLA_HEADS):
            sl = slice(h * LANE, (h + 1) * LANE)
            xq = q_raw[:, sl]
            _, r = _head_norm(xq, qhn_v)
            xh = xq * r
            dy = _rope_transpose(dq_ref[:, sl] * ATTN_SCALE, cos, s1, s2, half)
            dqhn_acc = dqhn_acc + jnp.sum(dy * xh, axis=0, keepdims=True)
            dxh = dy * qhn_v
            dqraw_s[:, sl] = r * (dxh - xh * (jnp.sum(dxh * xh, axis=-1, keepdims=True) * (1.0 / MLA_QK)))
            xk = k_raw[:, sl] + krp
            _, r = _head_norm(xk, khn_v)
            xh = xk * r
            dy = _rope_transpose(dk_ref[:, sl] * math.log(2.0), cos, s1, s2, half)
            dkhn_acc = dkhn_acc + jnp.sum(dy * xh, axis=0, keepdims=True)
            dxh = dy * khn_v
            dxk = r * (dxh - xh * (jnp.sum(dxh * xh, axis=-1, keepdims=True) * (1.0 / MLA_QK)))
            dkraw_s[:, sl] = dxk
            dkr_sum = dkr_sum + dxk
        dqhn_ref[...] += dqhn_acc
        dkhn_ref[...] += dkhn_acc
        dq_raw = dqraw_s[...]
        dk_raw = dkraw_s[...]
        dvv = dv_ref[...]
        dwuq_ref[...] += _dot_tn(qn, dq_raw)
        dwk_ref[...] += _dot_tn(kvn, dk_raw)
        dwv_ref[...] += _dot_tn(kvn, dvv)
        dqn = _dot(dq_raw, wuqt_ref[...])
        dkvn = _dot(dk_raw, wkt_ref[...]) + _dot(dvv, wvt_ref[...])
        xh = cq * rq
        dqn_ref[...] += jnp.sum(dqn * xh, axis=0, keepdims=True)
        dxh = dqn * qn_ref[...]
        dp_ref[:, EV_CQ:EV_CKV] = rq * (dxh - xh * jnp.mean(dxh * xh, axis=-1, keepdims=True))
        xh = ckv * rkv
        dkvn_ref[...] += jnp.sum(dkvn * xh, axis=0, keepdims=True)
        dxh = dkvn * kvn_ref[...]
        dp_ref[:, EV_CKV:EV_KR] = rkv * (dxh - xh * jnp.mean(dxh * xh, axis=-1, keepdims=True))
        lane = lax.broadcasted_iota(jnp.int32, (tm, LANE), 1)
        dp_ref[:, EV_KR:EV_RQ] = jnp.where(lane < MLA_ROPE, pltpu.roll(dkr_sum, LANE - MLA_NOPE, 1), 0.0)

    row = lambda w: pl.BlockSpec((tm, w), lambda i: (i, 0))
    full = lambda a: pl.BlockSpec(a.shape, lambda i: (0,) * a.ndim)
    ws = [qn_g, kvn_g, wuq, wk, wv, wuq_t, wk_t, wv_t, qhn, khn]
    outs = [jax.ShapeDtypeStruct((t, EV_RQ), F32), jax.ShapeDtypeStruct(wuq.shape, F32),
            jax.ShapeDtypeStruct(wk.shape, F32), jax.ShapeDtypeStruct(wv.shape, F32),
            jax.ShapeDtypeStruct(qn_g.shape, F32), jax.ShapeDtypeStruct(kvn_g.shape, F32),
            jax.ShapeDtypeStruct(qhn.shape, F32), jax.ShapeDtypeStruct(khn.shape, F32)]
    return pl.pallas_call(
        body, name=name, grid=(t // tm,),
        in_specs=[row(EV_RQ), row(LANE), row(LANE), row(LANE)] + [full(w) for w in ws] + [row(hw)] * 3,
        out_specs=[row(EV_RQ)] + [full(o) for o in outs[1:]],
        out_shape=outs,
        scratch_shapes=[pltpu.VMEM((tm, hw), F32), pltpu.VMEM((tm, hw), F32)],
        compiler_params=_cparams(("arbitrary",)),
    )(p, *tabs, *ws, dq, dk, dv)


ATTN_SCALE = MLA_QK ** -0.5
ATTN_QSCALE = ATTN_SCALE * math.log2(math.e)
ATTN_FWD_TQ, ATTN_FWD_TK = 512, 8192
ATTN_BWD_TQ, ATTN_BWD_TK = 256, 4096


def attn_fwd(q, k, v, name="attn_fwd"):
    t = q.shape[0]
    tq, tk = min(ATTN_FWD_TQ, _row_tile(t)), min(ATTN_FWD_TK, t)
    nh = MLA_HEADS

    def body(q_ref, k_ref, v_ref, o_ref, lse_ref, m_s, l_s, acc_s):
        j = pl.program_id(2)

        @pl.when(j == 0)
        def _():
            m_s[...] = jnp.full_like(m_s, -jnp.inf)
            l_s[...] = jnp.zeros_like(l_s)
            acc_s[...] = jnp.zeros_like(acc_s)

        s = _dot_nt(q_ref[...], k_ref[...])
        m_old = m_s[...]
        m_new = jnp.maximum(m_old, jnp.max(s, axis=-1, keepdims=True))
        pr = jnp.exp2(s - m_new)
        alpha = jnp.exp2(m_old - m_new)
        l_s[...] = alpha * l_s[...] + jnp.sum(pr, axis=-1, keepdims=True)
        acc_s[...] = alpha * acc_s[...] + _dot(pr, v_ref[...])
        m_s[...] = m_new

        @pl.when(j == pl.num_programs(2) - 1)
        def _():
            o_ref[...] = acc_s[...] / l_s[...]
            lse_ref[...] = m_s[...] + jnp.log2(l_s[...])

    return pl.pallas_call(
        body, name=name, grid=(nh, t // tq, t // tk),
        in_specs=[pl.BlockSpec((tq, LANE), lambda h, i, j: (i, h)),
                  pl.BlockSpec((tk, LANE), lambda h, i, j: (j, h)),
                  pl.BlockSpec((tk, LANE), lambda h, i, j: (j, h))],
        out_specs=[pl.BlockSpec((tq, LANE), lambda h, i, j: (i, h)),
                   pl.BlockSpec((None, tq, 1), lambda h, i, j: (h, i, 0))],
        out_shape=[jax.ShapeDtypeStruct((t, nh * LANE), F32), jax.ShapeDtypeStruct((nh, t, 1), F32)],
        scratch_shapes=[pltpu.VMEM((tq, 1), F32), pltpu.VMEM((tq, 1), F32), pltpu.VMEM((tq, LANE), F32)],
        compiler_params=_cparams(("parallel", "parallel", "arbitrary")),
    )(q, k, v)


def attn_bwd(q, k, v, o, lse, do, name="attn_bwd"):
    t = q.shape[0]
    tq, tk = min(ATTN_BWD_TQ, _row_tile(t)), min(ATTN_BWD_TK, t)
    nh = MLA_HEADS
    nq = t // tq

    def body(q_ref, k_ref, v_ref, o_ref, lse_ref, do_ref, dq_ref, dk_ref, dv_ref):
        kj, qi = pl.program_id(1), pl.program_id(2)

        @pl.when(qi == 0)
        def _():
            dk_ref[...] = jnp.zeros_like(dk_ref)
            dv_ref[...] = jnp.zeros_like(dv_ref)

        qv, kv, vv, dov = q_ref[...], k_ref[...], v_ref[...], do_ref[...]
        s = _dot_nt(qv, kv)
        pr = jnp.exp2(s - lse_ref[...])
        dp = _dot_nt(dov, vv)
        delta = jnp.sum(dov * o_ref[...], axis=-1, keepdims=True)
        ds = pr * (dp - delta)
        dv_ref[...] += _dot_tn(pr, dov)
        dk_ref[...] += _dot_tn(ds, qv)
        dq_tile = _dot(ds, kv)
        rows = pl.ds(pl.multiple_of(qi * tq, tq), tq)

        @pl.when(kj == 0)
        def _():
            dq_ref[rows, :] = dq_tile

        @pl.when(kj != 0)
        def _():
            dq_ref[rows, :] += dq_tile

    qspec = pl.BlockSpec((tq, LANE), lambda h, j, i: (i, h))
    kspec = pl.BlockSpec((tk, LANE), lambda h, j, i: (j, h))
    return pl.pallas_call(
        body, name=name, grid=(nh, t // tk, nq),
        in_specs=[qspec, kspec, kspec, qspec, pl.BlockSpec((None, tq, 1), lambda h, j, i: (h, i, 0)), qspec],
        out_specs=[pl.BlockSpec((t, LANE), lambda h, j, i: (0, h)), kspec, kspec],
        out_shape=[jax.ShapeDtypeStruct((t, nh * LANE), F32)] * 3,
        compiler_params=_cparams(("parallel", "arbitrary", "arbitrary")),
    )(q, k, v, o, lse, do)


def _scan_consts(c, reverse, inclusive):
    ii = lax.broadcasted_iota(jnp.int32, (c, c), 0)
    jj = lax.broadcasted_iota(jnp.int32, (c, c), 1)
    if reverse:
        incl = jj >= ii
        mask = incl if inclusive else jj > ii
    else:
        incl = jj <= ii
        mask = incl if inclusive else jj < ii
    mid = (c - 1 - c // 2) if reverse else c // 2
    incl_t = (jj <= ii) if reverse else (jj >= ii)
    return incl.astype(F32), incl_t.astype(F32), mask.astype(F32), mid


def _dot_split(a01, x):
    hi = x.astype(BF16)
    lo = (x - hi.astype(F32)).astype(BF16)
    a = a01.astype(BF16)
    return jnp.dot(a, hi, preferred_element_type=F32) + jnp.dot(a, lo, preferred_element_type=F32)


def _sub_masks(sub, dvg, u):
    if sub == 1:
        return None, None
    kl = lax.broadcasted_iota(jnp.int32, (1, LANE), 1)
    vl = lax.broadcasted_iota(jnp.int32, (1, dvg), 1)
    kw, vw = LANE // sub, dvg // sub
    km = (kl >= u * kw) & (kl < (u + 1) * kw)
    vm = (vl >= u * vw) & (vl < (u + 1) * vw)
    return km.astype(F32), vm.astype(F32)


def _block_incl(r, c, reverse, transposed):
    shift = c.bit_length() - 1
    ii = lax.broadcasted_iota(jnp.int32, (r, r), 0)
    jj = lax.broadcasted_iota(jnp.int32, (r, r), 1)
    same = lax.shift_right_logical(ii, shift) == lax.shift_right_logical(jj, shift)
    lower = (jj <= ii) if (reverse == transposed) else (jj >= ii)
    return (same & lower).astype(F32)


def _scan_chunk_fwd(b, la, mid):
    row = lax.broadcasted_iota(jnp.int32, b.shape, 0)
    bm = jnp.sum(jnp.where(row == mid, b, 0.0), axis=0, keepdims=True)
    tot = jnp.sum(la, axis=0, keepdims=True)
    e_qc = jnp.exp(b - bm)
    e_kc = jnp.exp(bm - b)
    e_qe = jnp.exp(b)
    e_kd = jnp.exp(tot - b)
    return e_qc, e_kc, e_qe, e_kd


def scan_fwd(q_arr, k_arr, v_arr, la_arr, *, qcb, kcb, vcb, lacb, la_row, chunk, dvg, sub, reverse, inclusive,
             qscale, kscale, rope=None, name="scan_fwd"):
    t = q_arr.shape[0]
    r = _row_tile(t)
    nb, nc = t // r, r // chunk
    c = chunk
    rb = (lambda j: nb - 1 - j) if reverse else (lambda j: j)
    order = list(range(nc))[::-1] if reverse else list(range(nc))
    half = RET_DK // 2

    def body(*refs):
        if rope is None:
            q_ref, k_ref, v_ref, la_ref, o_ref, st_ref, s_s = refs
        else:
            q_ref, k_ref, v_ref, la_ref, c_ref, s1_ref, s2_ref, o_ref, st_ref, s_s = refs

        @pl.when(pl.program_id(1) == 0)
        def _():
            s_s[...] = jnp.zeros_like(s_s)

        incl, _, mask, mid = _scan_consts(c, reverse, inclusive)
        for ci in order:
            rows = slice(ci * c, (ci + 1) * c)
            qv = q_ref[rows, :] * qscale
            kv = k_ref[rows, :] * kscale
            if rope is not None:
                cs, a1, a2 = c_ref[rows, :], s1_ref[rows, :], s2_ref[rows, :]
                qv = _rope_apply(qv, cs, a1, a2, half)
                kv = _rope_apply(kv, cs, a1, a2, half)
            la = jnp.broadcast_to(la_ref[...], (c, LANE)) if la_row else la_ref[rows, :]
            vv = v_ref[rows, :]
            e_qc, e_kc, e_qe, e_kd = _scan_chunk_fwd(_dot_split(incl, la), la, mid)
            qc, kc, qe, kd = qv * e_qc, kv * e_kc, qv * e_qe, kv * e_kd
            sg = s_s[...]
            st_ref[ci] = sg
            acc = None
            for u in range(sub):
                mu, vmu = _sub_masks(sub, dvg, u)
                qcu = qc if mu is None else qc * mu
                qeu = qe if mu is None else qe * mu
                a = _dot_nt(qcu, kc) * mask
                ou = _dot(a, vv) + _dot_nt(qeu, sg)
                ou = ou if vmu is None else ou * vmu
                acc = ou if acc is None else acc + ou
            o_ref[rows, :] = acc
            decay = jnp.exp(jnp.sum(la, axis=0, keepdims=True))
            s_s[...] = decay * sg + _dot_tn(vv, kd)

    specs = [pl.BlockSpec((r, LANE), lambda g, j: (rb(j), qcb + g)),
             pl.BlockSpec((r, LANE), lambda g, j: (rb(j), kcb + g)),
             pl.BlockSpec((r, dvg), lambda g, j: (rb(j), vcb + g)),
             pl.BlockSpec((1, LANE), lambda g, j: (0, lacb + g)) if la_row
             else pl.BlockSpec((r, LANE), lambda g, j: (rb(j), lacb + g))]
    args = [q_arr, k_arr, v_arr, la_arr]
    if rope is not None:
        specs += [pl.BlockSpec((r, LANE), lambda g, j: (rb(j), 0))] * 3
        args += list(rope)
    return pl.pallas_call(
        body, name=name, grid=(N_GROUPS, nb), in_specs=specs,
        out_specs=[pl.BlockSpec((r, dvg), lambda g, j: (rb(j), g)),
                   pl.BlockSpec((nc, dvg, LANE), lambda g, j: (rb(j), g, 0))],
        out_shape=[jax.ShapeDtypeStruct((t, N_GROUPS * dvg), F32),
                   jax.ShapeDtypeStruct((t // c, N_GROUPS * dvg, LANE), F32)],
        scratch_shapes=[pltpu.VMEM((dvg, LANE), F32)],
        compiler_params=_cparams(("parallel", "arbitrary")),
    )(*args)


def scan_bwd(q_arr, k_arr, v_arr, la_arr, st_arr, do_arr, prev, *, qcb, kcb, vcb, lacb, la_row, chunk, dvg, sub,
             reverse, inclusive, qscale, kscale, rope=None, name="scan_bwd"):
    t = q_arr.shape[0]
    r = _row_tile(t)
    nb, nc = t // r, r // chunk
    c = chunk
    rb = (lambda j: j) if reverse else (lambda j: nb - 1 - j)
    order = list(range(nc)) if reverse else list(range(nc))[::-1]
    half = RET_DK // 2
    n_in = 6 + (3 if rope is not None else 0) + (3 if prev is not None else 0)

    def body(*refs):
        ins, outs = refs[:n_in], refs[n_in:]
        q_ref, k_ref, v_ref, la_ref, st_ref, do_ref = ins[:6]
        pos = 6
        if rope is not None:
            c_ref, s1_ref, s2_ref = ins[pos:pos + 3]
            pos += 3
        if prev is not None:
            pq_ref, pk_ref, pv_ref = ins[pos:pos + 3]
        dq_ref, dk_ref, dv_ref, dla_ref, g_s = outs

        @pl.when(pl.program_id(1) == 0)
        def _():
            g_s[...] = jnp.zeros_like(g_s)
            if la_row:
                dla_ref[...] = jnp.zeros_like(dla_ref)

        incl, _, mask, mid = _scan_consts(c, reverse, inclusive)
        b_all = None if la_row else _dot_split(_block_incl(r, c, reverse, False), la_ref[...])
        pos = lax.broadcasted_iota(jnp.int32, (c, LANE), 0)
        cnt = ((c - pos) if reverse else (pos + 1)).astype(F32)
        dla_sum = jnp.zeros((1, LANE), F32)
        db_parts, dtot_parts = [None] * nc, [None] * nc
        for ci in order:
            rows = slice(ci * c, (ci + 1) * c)
            qv = q_ref[rows, :] * qscale
            kv = k_ref[rows, :] * kscale
            if rope is not None:
                cs, a1, a2 = c_ref[rows, :], s1_ref[rows, :], s2_ref[rows, :]
                qv = _rope_apply(qv, cs, a1, a2, half)
                kv = _rope_apply(kv, cs, a1, a2, half)
            la = jnp.broadcast_to(la_ref[...], (c, LANE)) if la_row else la_ref[rows, :]
            vv = v_ref[rows, :]
            dov = do_ref[rows, :]
            b = _dot_split(incl, la) if la_row else b_all[rows, :]
            e_qc, e_kc, e_qe, e_kd = _scan_chunk_fwd(b, la, mid)
            qc, kc, qe, kd = qv * e_qc, kv * e_kc, qv * e_qe, kv * e_kd
            sg = st_ref[ci]
            gn = g_s[...]
            dqc = jnp.zeros((c, LANE), F32)
            dkc = jnp.zeros((c, LANE), F32)
            dqe = jnp.zeros((c, LANE), F32)
            dvv = _dot_nt(kd, gn)
            ds_direct = jnp.zeros((dvg, LANE), F32)
            for u in range(sub):
                mu, vmu = _sub_masks(sub, dvg, u)
                qcu = qc if mu is None else qc * mu
                qeu = qe if mu is None else qe * mu
                dou = dov if vmu is None else dov * vmu
                a = _dot_nt(qcu, kc) * mask
                da = _dot_nt(dou, vv) * mask
                dvv = dvv + _dot_tn(a, dou)
                t1 = _dot(da, kc)
                dqc = dqc + (t1 if mu is None else t1 * mu)
                dkc = dkc + _dot_tn(da, qcu)
                t2 = _dot(dou, sg)
                dqe = dqe + (t2 if mu is None else t2 * mu)
                ds_direct = ds_direct + _dot_tn(dou, qeu)
            dkd = _dot(vv, gn)
            decay = jnp.exp(jnp.sum(la, axis=0, keepdims=True))
            dtot = jnp.sum(gn * sg, axis=0, keepdims=True) * decay + jnp.sum(dkd * kd, axis=0, keepdims=True)
            db = dqc * qc - dkc * kc + dqe * qe - dkd * kd
            if la_row:
                dla_sum = dla_sum + jnp.sum(db * cnt, axis=0, keepdims=True) + float(c) * dtot
            else:
                db_parts[ci] = db
                dtot_parts[ci] = jnp.broadcast_to(dtot, (c, LANE))
            dqv = dqc * e_qc + dqe * e_qe
            dkv = dkc * e_kc + dkd * e_kd
            if rope is not None:
                dqv = _rope_transpose(dqv, cs, a1, a2, half)
                dkv = _rope_transpose(dkv, cs, a1, a2, half)
            dqv = dqv * qscale
            dkv = dkv * kscale
            if prev is not None:
                dqv = dqv + pq_ref[rows, :]
                dkv = dkv + pk_ref[rows, :]
                dvv = dvv + pv_ref[rows, :]
            dq_ref[rows, :] = dqv
            dk_ref[rows, :] = dkv
            dv_ref[rows, :] = dvv
            g_s[...] = ds_direct + decay * gn
        if la_row:
            dla_ref[...] += dla_sum
        else:
            db_all = jnp.concatenate(db_parts, axis=0)
            dla_ref[...] = _dot_split(_block_incl(r, c, reverse, True), db_all) + jnp.concatenate(dtot_parts, axis=0)

    kblk = lambda cb: pl.BlockSpec((r, LANE), lambda g, j: (rb(j), cb + g))
    vblk = lambda cb: pl.BlockSpec((r, dvg), lambda g, j: (rb(j), cb + g))
    specs = [kblk(qcb), kblk(kcb), vblk(vcb),
             pl.BlockSpec((1, LANE), lambda g, j: (0, lacb + g)) if la_row else kblk(lacb),
             pl.BlockSpec((nc, dvg, LANE), lambda g, j: (rb(j), g, 0)), vblk(0)]
    args = [q_arr, k_arr, v_arr, la_arr, st_arr, do_arr]
    if rope is not None:
        specs += [pl.BlockSpec((r, LANE), lambda g, j: (rb(j), 0))] * 3
        args += list(rope)
    if prev is not None:
        specs += [kblk(0), kblk(0), vblk(0)]
        args += list(prev)
    wk = N_GROUPS * LANE
    outs = [jax.ShapeDtypeStruct((t, wk), F32), jax.ShapeDtypeStruct((t, wk), F32),
            jax.ShapeDtypeStruct((t, N_GROUPS * dvg), F32),
            jax.ShapeDtypeStruct((1, wk) if la_row else (t, wk), F32)]
    return pl.pallas_call(
        body, name=name, grid=(N_GROUPS, nb), in_specs=specs,
        out_specs=[kblk(0), kblk(0), vblk(0),
                   pl.BlockSpec((1, LANE), lambda g, j: (0, g)) if la_row else kblk(0)],
        out_shape=outs,
        scratch_shapes=[pltpu.VMEM((dvg, LANE), F32)],
        compiler_params=_cparams(("parallel", "arbitrary")),
    )(*args)


def _seg_mean(x, seg):
    w = x.shape[1]
    if seg % LANE == 0:
        parts = []
        for s in range(0, w, seg):
            m = jnp.mean(x[:, s:s + seg], axis=-1, keepdims=True)
            parts.append(jnp.broadcast_to(m, (x.shape[0], seg)))
        return jnp.concatenate(parts, axis=1)
    shift = seg.bit_length() - 1
    ii = lax.shift_right_logical(lax.broadcasted_iota(jnp.int32, (w, w), 0), shift)
    jj = lax.shift_right_logical(lax.broadcasted_iota(jnp.int32, (w, w), 1), shift)
    e = (ii == jj).astype(BF16)
    hi = x.astype(BF16)
    lo = (x - hi.astype(F32)).astype(BF16)
    return (jnp.dot(hi, e, preferred_element_type=F32) + jnp.dot(lo, e, preferred_element_type=F32)) * (1.0 / seg)


def gated_norm_fwd(o_f, o_b, gate_arr, gcb, gn, seg, name="gated_norm_fwd"):
    t, w = o_f.shape
    tm = _row_tile(t)

    def body(of_ref, ob_ref, g_ref, gn_ref, y_ref):
        o = of_ref[...] + ob_ref[...]
        r = lax.rsqrt(_seg_mean(o * o, seg) + EPS)
        gt = g_ref[...]
        y_ref[...] = (gt * _sigmoid(gt) * (o * r * gn_ref[...])).astype(BF16)

    bw = max(seg, LANE)
    row = pl.BlockSpec((tm, bw), lambda j, i: (i, j))
    return pl.pallas_call(
        body, name=name, grid=(w // bw, t // tm),
        in_specs=[row, row, pl.BlockSpec((tm, bw), lambda j, i: (i, gcb + j)),
                  pl.BlockSpec((1, bw), lambda j, i: (0, j))],
        out_specs=row, out_shape=jax.ShapeDtypeStruct((t, w), BF16),
        compiler_params=_cparams(("parallel", "parallel")),
    )(o_f, o_b, gate_arr, gn.reshape(1, w))


def gated_norm_bwd(o_f, o_b, gate_arr, gcb, gn, seg, dy, name="gated_norm_bwd"):
    t, w = o_f.shape
    tm = _row_tile(t)

    def body(of_ref, ob_ref, g_ref, gn_ref, dy_ref, do_ref, dg_ref, dgn_ref):
        @pl.when(pl.program_id(1) == 0)
        def _():
            dgn_ref[...] = jnp.zeros_like(dgn_ref)
        o = of_ref[...] + ob_ref[...]
        r = lax.rsqrt(_seg_mean(o * o, seg) + EPS)
        xh = o * r
        gt = g_ref[...]
        sg = _sigmoid(gt)
        dyv = dy_ref[...]
        n = xh * gn_ref[...]
        dg_ref[...] = dyv * n * (sg * (1.0 + gt * (1.0 - sg)))
        dn = dyv * (gt * sg)
        dgn_ref[...] += jnp.sum(dn * xh, axis=0, keepdims=True)
        dxh = dn * gn_ref[...]
        do_ref[...] = r * (dxh - xh * _seg_mean(dxh * xh, seg))

    bw = max(seg, LANE)
    row = pl.BlockSpec((tm, bw), lambda j, i: (i, j))
    vec = pl.BlockSpec((1, bw), lambda j, i: (0, j))
    return pl.pallas_call(
        body, name=name, grid=(w // bw, t // tm),
        in_specs=[row, row, pl.BlockSpec((tm, bw), lambda j, i: (i, gcb + j)), vec, row],
        out_specs=[row, row, vec],
        out_shape=[jax.ShapeDtypeStruct((t, w), F32), jax.ShapeDtypeStruct((t, w), F32),
                   jax.ShapeDtypeStruct((1, w), F32)],
        compiler_params=_cparams(("parallel", "arbitrary")),
    )(o_f, o_b, gate_arr, gn.reshape(1, w), dy)


def gla_gate_fwd(p, wg, bg, name="gla_gate_fwd"):
    t = p.shape[0]
    tm = _row_tile(t)
    w = wg.shape[1]
    gcb = OD_GA // LANE

    def body(ga_ref, wg_ref, bg_ref, la_ref):
        z = _dot(ga_ref[...], wg_ref[...]) + bg_ref[...]
        la_ref[...] = (jnp.minimum(z, 0.0) - jnp.log(1.0 + jnp.exp(-jnp.abs(z)))) * (1.0 / GLA_TAU)

    return pl.pallas_call(
        body, name=name, grid=(t // tm,),
        in_specs=[pl.BlockSpec((tm, LANE), lambda i: (i, gcb)), pl.BlockSpec((LANE, w), lambda i: (0, 0)),
                  pl.BlockSpec((1, w), lambda i: (0, 0))],
        out_specs=pl.BlockSpec((tm, w), lambda i: (i, 0)),
        out_shape=jax.ShapeDtypeStruct((t, w), F32),
        compiler_params=_cparams(("parallel",)),
    )(p, wg, bg)


def gla_gate_bwd(p, wg, wg_t, bg, dla, name="gla_gate_bwd"):
    t = p.shape[0]
    tm = _row_tile(t)
    w = wg.shape[1]
    gcb = OD_GA // LANE

    def body(ga_ref, wg_ref, wgt_ref, bg_ref, dla_ref, dga_ref, dwg_ref, dbg_ref):
        @pl.when(pl.program_id(0) == 0)
        def _():
            dwg_ref[...] = jnp.zeros_like(dwg_ref)
            dbg_ref[...] = jnp.zeros_like(dbg_ref)
        ga = ga_ref[...]
        z = _dot(ga, wg_ref[...]) + bg_ref[...]
        dz = dla_ref[...] * (1.0 / GLA_TAU) * _sigmoid(-z)
        dga_ref[...] = _dot(dz, wgt_ref[...])
        dwg_ref[...] += _dot_tn(ga, dz)
        dbg_ref[...] += jnp.sum(dz, axis=0, keepdims=True)

    return pl.pallas_call(
        body, name=name, grid=(t // tm,),
        in_specs=[pl.BlockSpec((tm, LANE), lambda i: (i, gcb)), pl.BlockSpec((LANE, w), lambda i: (0, 0)),
                  pl.BlockSpec((w, LANE), lambda i: (0, 0)), pl.BlockSpec((1, w), lambda i: (0, 0)),
                  pl.BlockSpec((tm, w), lambda i: (i, 0))],
        out_specs=[pl.BlockSpec((tm, LANE), lambda i: (i, 0)), pl.BlockSpec((LANE, w), lambda i: (0, 0)),
                   pl.BlockSpec((1, w), lambda i: (0, 0))],
        out_shape=[jax.ShapeDtypeStruct((t, LANE), F32), jax.ShapeDtypeStruct((LANE, w), F32),
                   jax.ShapeDtypeStruct((1, w), F32)],
        compiler_params=_cparams(("arbitrary",)),
    )(p, wg, wg_t, bg, dla)


FFN_COL = 1408


def _shifted(x, prev_row, next_row, first, last):
    tm = x.shape[0]
    row = lax.broadcasted_iota(jnp.int32, x.shape, 0)
    pr = jnp.where(first, 0.0, prev_row)
    nx = jnp.where(last, 0.0, next_row)
    xm1 = jnp.where(row == 0, pr, pltpu.roll(x, 1, 0))
    xp1 = jnp.where(row == tm - 1, nx, pltpu.roll(x, tm - 1, 0))
    return xm1, xp1


def _halo_specs(tm, tc, t, colmap, rowaxis):
    nb8 = tm // SUBLANE
    last8 = t // SUBLANE - 1

    def prev(*ids):
        i = ids[rowaxis]
        return (jnp.maximum(i * nb8 - 1, 0), colmap(*ids))

    def nxt(*ids):
        i = ids[rowaxis]
        return (jnp.minimum((i + 1) * nb8, last8), colmap(*ids))

    return pl.BlockSpec((SUBLANE, tc), prev), pl.BlockSpec((SUBLANE, tc), nxt)


def ffn_act_fwd(up, conv_w, conv_b, name="ffn_act_fwd"):
    t = up.shape[0]
    tm, tc = _row_tile(t), FFN_COL
    ncol = D_FF // tc

    def body(g_ref, gp_ref, gn_ref, v_ref, w_ref, b_ref, a_ref):
        i = pl.program_id(0)
        g = g_ref[...]
        gm1, gp1 = _shifted(g, gp_ref[SUBLANE - 1:SUBLANE, :], gn_ref[0:1, :], i == 0, i == pl.num_programs(0) - 1)
        cc = w_ref[0:1, :] * gm1 + w_ref[1:2, :] * g + w_ref[2:3, :] * gp1 + b_ref[...]
        a_ref[...] = (cc * _sigmoid(cc) * v_ref[...]).astype(BF16)

    prev, nxt = _halo_specs(tm, tc, t, lambda i, j: j, 0)
    return pl.pallas_call(
        body, name=name, grid=(t // tm, ncol),
        in_specs=[pl.BlockSpec((tm, tc), lambda i, j: (i, j)), prev, nxt,
                  pl.BlockSpec((tm, tc), lambda i, j: (i, j + ncol)),
                  pl.BlockSpec((SUBLANE, tc), lambda i, j: (0, j)), pl.BlockSpec((1, tc), lambda i, j: (0, j))],
        out_specs=pl.BlockSpec((tm, tc), lambda i, j: (i, j)),
        out_shape=jax.ShapeDtypeStruct((t, D_FF), BF16),
        compiler_params=_cparams(("parallel", "parallel")),
    )(up, up, up, up, conv_w, conv_b)


def ffn_act_bwd(up, conv_w, conv_b, dact, name="ffn_act_bwd"):
    t = up.shape[0]
    tm, tc = _row_tile(t), FFN_COL
    ncol = D_FF // tc

    def body(g_ref, gp_ref, gn_ref, v_ref, w_ref, b_ref, da_ref, dc_ref, dv_ref, dw_ref):
        i = pl.program_id(1)

        @pl.when(i == 0)
        def _():
            dw_ref[...] = jnp.zeros_like(dw_ref)
        g = g_ref[...]
        gm1, gp1 = _shifted(g, gp_ref[SUBLANE - 1:SUBLANE, :], gn_ref[0:1, :], i == 0, i == pl.num_programs(1) - 1)
        cc = w_ref[0:1, :] * gm1 + w_ref[1:2, :] * g + w_ref[2:3, :] * gp1 + b_ref[...]
        sg = _sigmoid(cc)
        da = da_ref[...]
        dv_ref[...] = da * (cc * sg)
        dc = da * v_ref[...] * (sg * (1.0 + cc * (1.0 - sg)))
        dc_ref[...] = dc
        dw_ref[0:1, :] += jnp.sum(dc * gm1, axis=0, keepdims=True)
        dw_ref[1:2, :] += jnp.sum(dc * g, axis=0, keepdims=True)
        dw_ref[2:3, :] += jnp.sum(dc * gp1, axis=0, keepdims=True)
        dw_ref[3:4, :] += jnp.sum(dc, axis=0, keepdims=True)

    prev, nxt = _halo_specs(tm, tc, t, lambda j, i: j, 1)
    tile = pl.BlockSpec((tm, tc), lambda j, i: (i, j))
    return pl.pallas_call(
        body, name=name, grid=(ncol, t // tm),
        in_specs=[tile, prev, nxt, pl.BlockSpec((tm, tc), lambda j, i: (i, j + ncol)),
                  pl.BlockSpec((SUBLANE, tc), lambda j, i: (0, j)), pl.BlockSpec((1, tc), lambda j, i: (0, j)), tile],
        out_specs=[tile, tile, pl.BlockSpec((SUBLANE, tc), lambda j, i: (0, j))],
        out_shape=[jax.ShapeDtypeStruct((t, D_FF), F32), jax.ShapeDtypeStruct((t, D_FF), F32),
                   jax.ShapeDtypeStruct((SUBLANE, D_FF), F32)],
        compiler_params=_cparams(("parallel", "arbitrary")),
    )(up, up, up, up, conv_w, conv_b, dact)


def conv_transpose(dc, conv_w, name="conv_transpose"):
    t = dc.shape[0]
    tm, tc = _row_tile(t), FFN_COL

    def body(d_ref, dp_ref, dn_ref, w_ref, o_ref):
        i = pl.program_id(0)
        d = d_ref[...]
        dm1, dp1 = _shifted(d, dp_ref[SUBLANE - 1:SUBLANE, :], dn_ref[0:1, :], i == 0, i == pl.num_programs(0) - 1)
        o_ref[...] = w_ref[0:1, :] * dp1 + w_ref[1:2, :] * d + w_ref[2:3, :] * dm1

    prev, nxt = _halo_specs(tm, tc, t, lambda i, j: j, 0)
    tile = pl.BlockSpec((tm, tc), lambda i, j: (i, j))
    return pl.pallas_call(
        body, name=name, grid=(t // tm, D_FF // tc),
        in_specs=[tile, prev, nxt, pl.BlockSpec((SUBLANE, tc), lambda i, j: (0, j))],
        out_specs=tile, out_shape=jax.ShapeDtypeStruct((t, D_FF), F32),
        compiler_params=_cparams(("parallel", "parallel")),
    )(dc, dc, dc, conv_w)


def loss_head(y, target, name="loss_head"):
    t, d = y.shape
    tm = _row_tile(t)

    def body(y_ref, t_ref, dy_ref, l_ref):
        @pl.when(pl.program_id(0) == 0)
        def _():
            l_ref[...] = jnp.zeros_like(l_ref)
        e = y_ref[...] - t_ref[...]
        dy_ref[...] = e * (1.0 / d)
        rowloss = jnp.sum(e * e, axis=-1, keepdims=True) * (0.5 / d)
        l_ref[...] += jnp.sum(rowloss, axis=0, keepdims=True)

    row = pl.BlockSpec((tm, d), lambda i: (i, 0))
    return pl.pallas_call(
        body, name=name, grid=(t // tm,), in_specs=[row, row],
        out_specs=[row, pl.BlockSpec((1, 1), lambda i: (0, 0))],
        out_shape=[jax.ShapeDtypeStruct((t, d), F32), jax.ShapeDtypeStruct((1, 1), F32)],
        compiler_params=_cparams(("arbitrary",)),
    )(y, target)


def _pad_heads(w, heads, width):
    lead = w.shape[:-1]
    w = w.reshape(*lead, heads, width)
    w = jnp.pad(w, [(0, 0)] * len(lead) + [(0, 0), (0, LANE - width)])
    return w.reshape(*lead, heads * LANE)


def _unpad_heads(w, heads, width):
    lead = w.shape[:-1]
    return w.reshape(*lead, heads, LANE)[..., :width].reshape(*lead, heads * width)


def _pad_rows_heads(w, heads, width):
    return _pad_heads(w.T, heads, width).T


def _unpad_rows_heads(w, heads, width):
    return _unpad_heads(w.T, heads, width).T


_EV_REAL = MLA_Q_RANK + MLA_KV_RANK + MLA_ROPE


def prep_even(wts, dt=BF16):
    w_in_t = wts["w_in_t"]
    w_in_tp = jnp.concatenate([w_in_t[:_EV_REAL], jnp.zeros((EV_RQ - _EV_REAL, D_MODEL), w_in_t.dtype),
                               w_in_t[_EV_REAL:]], axis=0).astype(dt)
    wuq = _pad_heads(wts["w_uq"], MLA_HEADS, MLA_QK).astype(dt)
    ukv = wts["w_ukv"].reshape(MLA_KV_RANK, MLA_HEADS, MLA_NOPE + MLA_V)
    wk = _pad_heads(ukv[..., :MLA_NOPE].reshape(MLA_KV_RANK, -1), MLA_HEADS, MLA_NOPE).astype(dt)
    wv = _pad_heads(ukv[..., MLA_NOPE:].reshape(MLA_KV_RANK, -1), MLA_HEADS, MLA_V).astype(dt)
    w_out = wts["w_out"]
    wa = _pad_rows_heads(w_out[:MLA_HEADS * MLA_V], MLA_HEADS, MLA_V).astype(dt)
    wr = w_out[MLA_HEADS * MLA_V:].astype(dt)
    pad1 = lambda v, n: jnp.pad(v.astype(F32), (0, n - v.shape[0])).reshape(1, n)
    lg = lambda th: jnp.log1p(-jnp.exp2(-th.astype(F32)))
    return dict(
        w_in=w_in_tp.T, w_in_t=w_in_tp, wuq=wuq, wuq_t=wuq.T, wk=wk, wk_t=wk.T, wv=wv, wv_t=wv.T,
        wa=wa, wa_t=wa.T, wr=wr, wr_t=wr.T,
        mix_norm=wts["mix_norm"].astype(F32), q_norm=wts["q_norm"].astype(F32).reshape(1, -1),
        kv_norm=wts["kv_norm"].astype(F32).reshape(1, -1),
        qhn=pad1(wts["q_head_norm"], LANE), khn=pad1(wts["k_head_norm"], LANE),
        la_f=jnp.repeat(lg(wts["theta_fwd"]), RET_DK).reshape(1, -1),
        la_b=jnp.repeat(lg(wts["theta_bwd"]), RET_DK).reshape(1, -1),
        out_norm=wts["ret_out_norm"].astype(F32).reshape(-1),
    )


def prep_odd(wts, dt=BF16):
    w_in_t = wts["w_in_t"]
    w_in_tp = jnp.concatenate([w_in_t, jnp.zeros((OD_IN - w_in_t.shape[0], D_MODEL), w_in_t.dtype)],
                              axis=0).astype(dt)
    hk = GLA_HEADS * GLA_DK
    wg = jnp.zeros((LANE, 2 * hk), F32)
    wg = wg.at[:GLA_GATE_RANK, :hk].set(wts["w_gate_fwd"].astype(F32))
    wg = wg.at[GLA_GATE_RANK:2 * GLA_GATE_RANK, hk:].set(wts["w_gate_bwd"].astype(F32))
    wg = wg.astype(dt)
    bg = jnp.concatenate([wts["b_gate_fwd"], wts["b_gate_bwd"]]).astype(F32).reshape(1, -1)
    w_out = wts["w_out"].astype(dt)
    return dict(w_in=w_in_tp.T, w_in_t=w_in_tp, wg=wg, wg_t=wg.T, bg=bg, w_out=w_out, w_out_t=w_out.T,
                mix_norm=wts["mix_norm"].astype(F32), out_norm=wts["gla_out_norm"].astype(F32).reshape(-1))


def prep_ffn(wts, dt=BF16):
    w_up_t = wts["w_up_t"].astype(dt)
    w_down = wts["w_down"].astype(dt)
    cw = jnp.pad(wts["conv_w"].astype(F32), ((0, SUBLANE - 3), (0, 0)))
    return dict(w_up=w_up_t.T, w_up_t=w_up_t, w_down=w_down, w_down_t=w_down.T, conv_w=cw,
                conv_b=wts["conv_b"].astype(F32).reshape(1, -1), norm=wts["norm"].astype(F32))


_RET = dict(qcb=EV_RQ // LANE, kcb=EV_RK // LANE, vcb=EV_RV // LANE, la_row=True, chunk=RET_CHUNK, dvg=LANE,
            sub=2, qscale=1.0, kscale=RET_DK ** -0.5)
_GLA = dict(qcb=OD_Q // LANE, kcb=OD_K // LANE, vcb=OD_V // GLA_DV, la_row=False, chunk=GLA_CHUNK, dvg=GLA_DV,
            sub=1, qscale=GLA_DK ** -0.5, kscale=1.0)
_FWD_DIR = dict(reverse=False, inclusive=True)
_BWD_DIR = dict(reverse=True, inclusive=False)


def even_fwd(x, w, tabs_mla, tabs_ret, tag):
    h = rmsnorm_fwd(x, w["mix_norm"], name=f"{tag}_norm")
    p = mm_nn(h, w["w_in"], name=f"{tag}_in")
    q, k, v = mla_prep_fwd(p, tabs_mla, w["q_norm"], w["kv_norm"], w["wuq"], w["wk"], w["wv"], w["qhn"], w["khn"],
                           name=f"{tag}_mla_prep")
    o, lse = attn_fwd(q, k, v, name=f"{tag}_attn")
    of, stf = scan_fwd(p, p, p, w["la_f"], lacb=0, rope=tabs_ret, name=f"{tag}_ret_f", **_RET, **_FWD_DIR)
    ob, stb = scan_fwd(p, p, p, w["la_b"], lacb=0, rope=tabs_ret, name=f"{tag}_ret_b", **_RET, **_BWD_DIR)
    r = gated_norm_fwd(of, ob, p, EV_RG // LANE, w["out_norm"], RET_DV, name=f"{tag}_ret_out")
    x1 = mm_nn(o, w["wa"], res=x, name=f"{tag}_out_a")
    x2 = mm_nn(r, w["wr"], res=x1, name=f"{tag}_out_r")
    return x2, dict(x=x, h=h, p=p, q=q, k=k, v=v, o=o, lse=lse, of=of, ob=ob, stf=stf, stb=stb, r=r)


def even_bwd(dx, s, w, tabs_mla, tabs_ret, tag):
    tag = tag + "_b"
    do = mm_nn(dx, w["wa_t"], name=f"{tag}_dout_a")
    dr = mm_nn(dx, w["wr_t"], name=f"{tag}_dout_r")
    d_wa = mm_tn(s["o"], dx, name=f"{tag}_dwa")
    d_wr = mm_tn(s["r"], dx, name=f"{tag}_dwr")
    dq, dk, dv = attn_bwd(s["q"], s["k"], s["v"], s["o"], s["lse"], do, name=f"{tag}_attn")
    (dp_mla, d_wuq, d_wk, d_wv, d_qn, d_kvn, d_qhn, d_khn) = mla_prep_bwd(
        s["p"], tabs_mla, w["q_norm"], w["kv_norm"], w["wuq"], w["wk"], w["wv"], w["wuq_t"], w["wk_t"], w["wv_t"],
        w["qhn"], w["khn"], dq, dk, dv, name=f"{tag}_mla_prep")
    d_o, d_gate, d_gn = gated_norm_bwd(s["of"], s["ob"], s["p"], EV_RG // LANE, w["out_norm"], RET_DV, dr,
                                       name=f"{tag}_ret_out")
    p = s["p"]
    g1 = scan_bwd(p, p, p, w["la_f"], s["stf"], d_o, None, lacb=0, rope=tabs_ret, name=f"{tag}_ret_f",
                  **_RET, **_FWD_DIR)
    g2 = scan_bwd(p, p, p, w["la_b"], s["stb"], d_o, g1[:3], lacb=0, rope=tabs_ret, name=f"{tag}_ret_b",
                  **_RET, **_BWD_DIR)
    dp = jnp.concatenate([dp_mla, g2[0], g2[1], g2[2], d_gate], axis=1)
    dh = mm_nn(dp, w["w_in_t"], name=f"{tag}_dh")
    d_win_t = mm_tn(dp, s["h"], name=f"{tag}_dwin")
    dx_in, d_mix = rmsnorm_bwd(s["x"], w["mix_norm"], dh, dx, name=f"{tag}_norm")
    grads = dict(w_in_t=d_win_t, wuq=d_wuq, wk=d_wk, wv=d_wv, wa=d_wa, wr=d_wr, mix_norm=d_mix, q_norm=d_qn,
                 kv_norm=d_kvn, qhn=d_qhn, khn=d_khn, la_f=g1[3], la_b=g2[3], out_norm=d_gn)
    return dx_in, grads


def odd_fwd(x, w, tag):
    h = rmsnorm_fwd(x, w["mix_norm"], name=f"{tag}_norm")
    p = mm_nn(h, w["w_in"], name=f"{tag}_in")
    la = gla_gate_fwd(p, w["wg"], w["bg"], name=f"{tag}_gate")
    of, stf = scan_fwd(p, p, p, la, lacb=0, name=f"{tag}_gla_f", **_GLA, **_FWD_DIR)
    ob, stb = scan_fwd(p, p, p, la, lacb=N_GROUPS, name=f"{tag}_gla_b", **_GLA, **_BWD_DIR)
    y = gated_norm_fwd(of, ob, p, OD_R // GLA_DV, w["out_norm"], GLA_DV, name=f"{tag}_gla_out")
    x1 = mm_nn(y, w["w_out"], res=x, name=f"{tag}_out")
    return x1, dict(x=x, h=h, p=p, la=la, of=of, ob=ob, stf=stf, stb=stb, y=y)


def odd_bwd(dx, s, w, tag):
    tag = tag + "_b"
    dy = mm_nn(dx, w["w_out_t"], name=f"{tag}_dout")
    d_wout = mm_tn(s["y"], dx, name=f"{tag}_dwout")
    d_o, d_gate, d_gn = gated_norm_bwd(s["of"], s["ob"], s["p"], OD_R // GLA_DV, w["out_norm"], GLA_DV, dy,
                                       name=f"{tag}_gla_out")
    p, la = s["p"], s["la"]
    g1 = scan_bwd(p, p, p, la, s["stf"], d_o, None, lacb=0, name=f"{tag}_gla_f", **_GLA, **_FWD_DIR)
    g2 = scan_bwd(p, p, p, la, s["stb"], d_o, g1[:3], lacb=N_GROUPS, name=f"{tag}_gla_b", **_GLA, **_BWD_DIR)
    dla = jnp.concatenate([g1[3], g2[3]], axis=1)
    d_ga, d_wg, d_bg = gla_gate_bwd(p, w["wg"], w["wg_t"], w["bg"], dla, name=f"{tag}_gate")
    dp = jnp.concatenate([g2[0], g2[1], g2[2], d_gate, d_ga], axis=1)
    dh = mm_nn(dp, w["w_in_t"], name=f"{tag}_dh")
    d_win_t = mm_tn(dp, s["h"], name=f"{tag}_dwin")
    dx_in, d_mix = rmsnorm_bwd(s["x"], w["mix_norm"], dh, dx, name=f"{tag}_norm")
    grads = dict(w_in_t=d_win_t, wg=d_wg, bg=d_bg, w_out=d_wout, mix_norm=d_mix, out_norm=d_gn)
    return dx_in, grads


def ffn_fwd(x, w, tag):
    h = rmsnorm_fwd(x, w["norm"], name=f"{tag}_norm")
    up = mm_nn(h, w["w_up"], name=f"{tag}_up")
    act = ffn_act_fwd(up, w["conv_w"], w["conv_b"], name=f"{tag}_act")
    x1 = mm_nn(act, w["w_down"], res=x, name=f"{tag}_down")
    return x1, dict(x=x, h=h, up=up, act=act)


def ffn_bwd(dx, s, w, tag):
    tag = tag + "_b"
    dact = mm_nn(dx, w["w_down_t"], name=f"{tag}_dact")
    d_wdown = mm_tn(s["act"], dx, name=f"{tag}_dwdown")
    dc, dval, d_conv = ffn_act_bwd(s["up"], w["conv_w"], w["conv_b"], dact, name=f"{tag}_act")
    dgate = conv_transpose(dc, w["conv_w"], name=f"{tag}_convt")
    dh1 = mm_nn(dgate, w["w_up_t"][:D_FF], name=f"{tag}_dh_g")
    dh = mm_nn(dval, w["w_up_t"][D_FF:], res=dh1, name=f"{tag}_dh_v")
    d_wup_t = jnp.concatenate([mm_tn(dgate, s["h"], name=f"{tag}_dwup_g"),
                               mm_tn(dval, s["h"], name=f"{tag}_dwup_v")], axis=0)
    dx_in, d_norm = rmsnorm_bwd(s["x"], w["norm"], dh, dx, name=f"{tag}_norm")
    grads = dict(w_up_t=d_wup_t, w_down=d_wdown, conv_w=d_conv[:3], conv_b=d_conv[3], norm=d_norm)
    return dx_in, grads


def unprep_even_grads(g, theta_fwd, theta_bwd):
    d_win_t = jnp.concatenate([g["w_in_t"][:_EV_REAL], g["w_in_t"][EV_RQ:]], axis=0)
    d_uq = _unpad_heads(g["wuq"], MLA_HEADS, MLA_QK)
    dk_ = _unpad_heads(g["wk"], MLA_HEADS, MLA_NOPE).reshape(MLA_KV_RANK, MLA_HEADS, MLA_NOPE)
    dv_ = _unpad_heads(g["wv"], MLA_HEADS, MLA_V).reshape(MLA_KV_RANK, MLA_HEADS, MLA_V)
    d_ukv = jnp.concatenate([dk_, dv_], axis=-1).reshape(MLA_KV_RANK, -1)
    d_wout = jnp.concatenate([_unpad_rows_heads(g["wa"], MLA_HEADS, MLA_V), g["wr"]], axis=0)

    def dtheta(dla, th):
        dlg = dla.reshape(RET_HEADS, RET_DK).sum(axis=-1)
        e = jnp.exp2(-th.astype(F32))
        return dlg * (e * math.log(2.0)) / (1.0 - e)

    return dict(mix_norm=g["mix_norm"].reshape(-1), w_in_t=d_win_t, q_norm=g["q_norm"].reshape(-1),
                kv_norm=g["kv_norm"].reshape(-1), w_uq=d_uq, w_ukv=d_ukv, q_head_norm=g["qhn"].reshape(-1)[:MLA_QK],
                k_head_norm=g["khn"].reshape(-1)[:MLA_QK], theta_fwd=dtheta(g["la_f"], theta_fwd),
                theta_bwd=dtheta(g["la_b"], theta_bwd), ret_out_norm=g["out_norm"].reshape(RET_HEADS, RET_DV),
                w_out=d_wout)


def unprep_odd_grads(g):
    hk = GLA_HEADS * GLA_DK
    return dict(mix_norm=g["mix_norm"].reshape(-1), w_in_t=g["w_in_t"][:OD_GA + 2 * GLA_GATE_RANK],
                w_gate_fwd=g["wg"][:GLA_GATE_RANK, :hk], b_gate_fwd=g["bg"].reshape(-1)[:hk],
                w_gate_bwd=g["wg"][GLA_GATE_RANK:2 * GLA_GATE_RANK, hk:], b_gate_bwd=g["bg"].reshape(-1)[hk:],
                gla_out_norm=g["out_norm"].reshape(GLA_HEADS, GLA_DV), w_out=g["w_out"])


def unprep_ffn_grads(g):
    return dict(norm=g["norm"].reshape(-1), w_up_t=g["w_up_t"], conv_w=g["conv_w"], conv_b=g["conv_b"],
                w_down=g["w_down"])


def _mesh_pos():
    return tuple(lax.axis_index(n) for n in MESH_AXES)


def _slot(px, py, pc):
    return 4 * px + 2 * py + pc


def all_gather_blocks(blk, name="weight_all_gather"):
    r, w = blk.shape

    def body(x_ref, out_ref, send_sems, recv_sems, local_sem):
        x, y, c = _mesh_pos()
        me, sibling = (x, y, c), (x, y, 1 - c)
        chips = [(1 - x, y), (x, 1 - y), (1 - x, 1 - y)]

        def copy(k, block, to, src=None):
            dst = out_ref.at[_slot(*block)]
            return pltpu.make_async_remote_copy(
                src_ref=dst if src is None else src, dst_ref=dst, send_sem=send_sems.at[k],
                recv_sem=recv_sems.at[k], device_id=to, device_id_type=pl.DeviceIdType.MESH)

        mine = pltpu.make_async_copy(x_ref, out_ref.at[_slot(*me)], local_sem)
        mine.start()
        first = [copy(0, me, sibling, src=x_ref)]
        first += [copy(1 + j, me, (*chip, c), src=x_ref) for j, chip in enumerate(chips)]
        for cp in first:
            cp.start()
        passed = [copy(4 + j, (*chip, c), sibling) for j, chip in enumerate(chips)]
        for j, chip in enumerate(chips):
            copy(1 + j, (*chip, c), me).wait_recv()
            passed[j].start()
        copy(0, sibling, me).wait_recv()
        for j, chip in enumerate(chips):
            copy(4 + j, (*chip, 1 - c), me).wait_recv()
        for cp in first + passed:
            cp.wait_send()
        mine.wait()

    return pl.pallas_call(
        body, name=name,
        out_shape=jax.ShapeDtypeStruct((N_DEV, r, w), blk.dtype),
        in_specs=[pl.BlockSpec(memory_space=pl.ANY)],
        out_specs=pl.BlockSpec(memory_space=pl.ANY),
        scratch_shapes=[pltpu.SemaphoreType.DMA((7,)), pltpu.SemaphoreType.DMA((7,)), pltpu.SemaphoreType.DMA],
    )(blk)


def all_to_all_blocks(send, name="grad_all_to_all"):
    _, r, w = send.shape

    def body(s_ref, r_ref, send_sems, recv_sems, local_sem):
        x, y, c = _mesh_pos()
        me = _slot(x, y, c)
        mine = pltpu.make_async_copy(s_ref.at[me], r_ref.at[me], local_sem)
        mine.start()
        copies = []
        for k in range(1, N_DEV):
            px = 1 - x if (k >> 2) & 1 else x
            py = 1 - y if (k >> 1) & 1 else y
            pc = 1 - c if k & 1 else c
            cp = pltpu.make_async_remote_copy(
                src_ref=s_ref.at[_slot(px, py, pc)], dst_ref=r_ref.at[me], send_sem=send_sems.at[k - 1],
                recv_sem=recv_sems.at[k - 1], device_id=(px, py, pc), device_id_type=pl.DeviceIdType.MESH)
            cp.start()
            copies.append(cp)
        for cp in copies:
            cp.wait()
        mine.wait()

    return pl.pallas_call(
        body, name=name,
        out_shape=jax.ShapeDtypeStruct((N_DEV, r, w), send.dtype),
        in_specs=[pl.BlockSpec(memory_space=pl.ANY)],
        out_specs=pl.BlockSpec(memory_space=pl.ANY),
        scratch_shapes=[pltpu.SemaphoreType.DMA((7,)), pltpu.SemaphoreType.DMA((7,)), pltpu.SemaphoreType.DMA],
    )(send)


FLAT_W = 1024
FLAT_TILE = 256


def sum_slots(recv, name="grad_sum"):
    _, r, w = recv.shape

    def body(r_ref, o_ref):
        acc = r_ref[0].astype(F32)
        for k in range(1, N_DEV):
            acc = acc + r_ref[k].astype(F32)
        o_ref[...] = acc

    tr = _slab_tile(r)
    return pl.pallas_call(
        body, name=name, grid=(r // tr,),
        in_specs=[pl.BlockSpec((N_DEV, tr, w), lambda i: (0, i, 0))],
        out_specs=pl.BlockSpec((tr, w), lambda i: (i, 0)),
        out_shape=jax.ShapeDtypeStruct((r, w), F32),
        compiler_params=_cparams(("parallel",)),
    )(recv)


def _slab_tile(r):
    return max(t for t in range(SUBLANE, FLAT_TILE + 1, SUBLANE) if r % t == 0)


def adamw(wf, gf, mf, vf, name="adamw"):
    r, w = wf.shape
    tr = _slab_tile(r)

    def body(w_ref, g_ref, m_ref, v_ref, d_ref, m_out, v_out):
        g = g_ref[...]
        m = ADAM_B1 * m_ref[...] + (1.0 - ADAM_B1) * g
        v = ADAM_B2 * v_ref[...] + (1.0 - ADAM_B2) * (g * g)
        m_hat = m / (1.0 - ADAM_B1 ** ADAM_STEP)
        v_hat = v / (1.0 - ADAM_B2 ** ADAM_STEP)
        d_ref[...] = -ADAM_LR * (m_hat / (jnp.sqrt(v_hat) + ADAM_EPS) + ADAM_WD * w_ref[...])
        m_out[...] = m
        v_out[...] = v

    tile = pl.BlockSpec((tr, w), lambda i: (i, 0))
    return pl.pallas_call(
        body, name=name, grid=(r // tr,), in_specs=[tile] * 4, out_specs=[tile] * 3,
        out_shape=[jax.ShapeDtypeStruct((r, w), F32)] * 3,
        compiler_params=_cparams(("parallel",)),
    )(wf, gf, mf, vf)


ROW_FORM = [("w_in_even", "T"), ("w_out_even", "R"), ("w_in_odd", "T"), ("w_out_odd", "R"), ("ffn_w_up", "T"),
            ("ffn_w_down", "R")]
SHARDED_MID = [("mla_w_uq", 2), ("mla_w_ukv", 2)]
SHARDED_SMALL = [("mix_norm_odd", 1), ("gla_w_gate_fwd", 2), ("gla_b_gate_fwd", 1), ("gla_w_gate_bwd", 2),
                 ("gla_b_gate_bwd", 1), ("gla_out_norm", 2), ("ffn_conv_w", 2)]
REPLICATED = ["mix_norm_even", "mla_q_norm", "mla_kv_norm", "mla_q_head_norm", "mla_k_head_norm", "ret_theta_fwd",
              "ret_theta_bwd", "ret_out_norm", "ffn_norm", "ffn_conv_b"]
WEIGHT_NAMES = ["mix_norm_even", "w_in_even", "mla_q_norm", "mla_kv_norm", "mla_w_uq", "mla_w_ukv",
                "mla_q_head_norm", "mla_k_head_norm", "ret_theta_fwd", "ret_theta_bwd", "ret_out_norm", "w_out_even",
                "mix_norm_odd", "w_in_odd", "gla_w_gate_fwd", "gla_b_gate_fwd", "gla_w_gate_bwd", "gla_b_gate_bwd",
                "gla_out_norm", "w_out_odd", "ffn_norm", "ffn_w_up", "ffn_conv_w", "ffn_conv_b", "ffn_w_down"]


def _round_up(n, m):
    return -(-n // m) * m


def _pack_rows(parts, rows):
    flat = jnp.concatenate(parts, axis=-1)
    pad = rows * FLAT_W - flat.shape[-1]
    flat = jnp.pad(flat, [(0, 0)] * (flat.ndim - 1) + [(0, pad)])
    return flat.reshape(*flat.shape[:-1], rows, FLAT_W)


def _row_form(v, form):
    if form == "T":
        v = jnp.swapaxes(v, 1, 2)
    return v.reshape(-1, v.shape[-1])


def _row_counts(local_shapes):
    return [local_shapes[n][0] * local_shapes[n][2 if f == "T" else 1] for n, f in ROW_FORM]


def _tail_layout(local_shapes):
    n_sh = sum(math.prod(local_shapes[n]) for n, _ in SHARDED_MID + SHARDED_SMALL)
    n_rep = sum(math.prod(local_shapes[n]) for n in REPLICATED)
    sh_rows = _round_up(-(-n_sh // FLAT_W), SUBLANE)
    rep_rows = _round_up(-(-n_rep // FLAT_W), SUBLANE)
    return sh_rows, rep_rows, _round_up(sh_rows + rep_rows, FLAT_TILE)


def pack_tail(vals, local_shapes):
    sh_rows, rep_rows, rows = _tail_layout(local_shapes)
    sh = _pack_rows([vals[n].astype(F32).reshape(-1) for n, _ in SHARDED_MID + SHARDED_SMALL], sh_rows)
    rep = _pack_rows([vals[n].astype(F32).reshape(-1) for n in REPLICATED], rep_rows)
    return jnp.concatenate([sh, rep, jnp.zeros((rows - sh_rows - rep_rows, FLAT_W), F32)], axis=0)


def unpack_tail(tail, local_shapes):
    sh_rows, rep_rows, _ = _tail_layout(local_shapes)
    out = {}
    for names, flat in (([n for n, _ in SHARDED_MID + SHARDED_SMALL], tail[:sh_rows].reshape(-1)),
                        (REPLICATED, tail[sh_rows:sh_rows + rep_rows].reshape(-1))):
        off = 0
        for n in names:
            k = math.prod(local_shapes[n])
            out[n] = flat[off:off + k].reshape(local_shapes[n])
            off += k
    return out


def unpack_rows(slab, local_shapes):
    out = {}
    off = 0
    for (n, form), rows in zip(ROW_FORM, _row_counts(local_shapes)):
        loc = local_shapes[n]
        piece = slab[off:off + rows]
        if form == "T":
            piece = jnp.swapaxes(piece.reshape(loc[0], loc[2], loc[1]), 1, 2)
        out[n] = piece.reshape(loc)
        off += rows
    return out


def pack_grad_blocks(full_grads, local_shapes):
    sh_rows, rep_rows, rows = _tail_layout(local_shapes)
    blocks = []
    for n, form in ROW_FORM:
        g = full_grads[n].astype(BF16)
        layers, total = g.shape[0], g.shape[1]
        g = g.reshape(layers, N_DEV, total // N_DEV, FLAT_W)
        blocks.append(jnp.swapaxes(g, 0, 1).reshape(N_DEV, -1, FLAT_W))
    parts = []
    for n, ax in SHARDED_MID + SHARDED_SMALL:
        g = full_grads[n].astype(F32)
        loc = local_shapes[n]
        g = g.reshape(*g.shape[:ax], N_DEV, loc[ax], *g.shape[ax + 1:])
        parts.append(jnp.moveaxis(g, ax, 0).reshape(N_DEV, -1))
    sh = _pack_rows(parts, sh_rows)
    rep = _pack_rows([full_grads[n].astype(F32).reshape(-1) for n in REPLICATED], rep_rows)
    rep = jnp.broadcast_to(rep[None], (N_DEV, rep_rows, FLAT_W))
    pad = jnp.zeros((N_DEV, rows - sh_rows - rep_rows, FLAT_W), F32)
    tail = jnp.concatenate([sh, rep, pad], axis=1).astype(BF16)
    return jnp.concatenate(blocks + [tail], axis=1)


def pack_gather_block(vals, local_shapes):
    big = jnp.concatenate([_row_form(vals[n].astype(BF16), f) for n, f in ROW_FORM], axis=0)
    mid = [vals[n].astype(BF16).reshape(-1) for n, _ in SHARDED_MID]
    small = jnp.concatenate([vals[n].astype(F32).reshape(-1) for n, _ in SHARDED_SMALL])
    small = lax.bitcast_convert_type(small, BF16).reshape(-1)
    n = sum(v.shape[0] for v in mid) + small.shape[0]
    tail = _pack_rows(mid + [small], _round_up(-(-n // FLAT_W), 2 * SUBLANE))
    return jnp.concatenate([big, tail], axis=0)


def unpack_gathered(gathered, local_shapes):
    out = {}
    off = 0
    for (n, form), rows in zip(ROW_FORM, _row_counts(local_shapes)):
        layers = local_shapes[n][0]
        piece = gathered[:, off:off + rows].reshape(N_DEV, layers, rows // layers, FLAT_W)
        out[n] = jnp.swapaxes(piece, 0, 1).reshape(layers, N_DEV * (rows // layers), FLAT_W)
        off += rows
    flat = gathered[:, off:].reshape(N_DEV, -1)
    off = 0

    def full(piece, n, ax):
        loc = local_shapes[n]
        piece = jnp.moveaxis(piece.reshape(N_DEV, *loc), 0, ax)
        return piece.reshape(*loc[:ax], N_DEV * loc[ax], *loc[ax + 1:])

    for n, ax in SHARDED_MID:
        k = math.prod(local_shapes[n])
        out[n] = full(flat[:, off:off + k], n, ax)
        off += k
    for n, ax in SHARDED_SMALL:
        k = math.prod(local_shapes[n])
        pairs = flat[:, off:off + 2 * k].reshape(N_DEV, k, 2)
        out[n] = full(lax.bitcast_convert_type(pairs, F32), n, ax)
        off += 2 * k
    return out


def local_step(x, positions, target, fw):
    tabs_mla = rope_tables(positions, MLA_NOPE, MLA_ROPE // 2, LANE)
    tabs_ret = rope_tables(positions, 0, RET_DK // 2, RET_DK)
    layers = []
    for layer in range(DEPTH):
        i = layer // 2
        if layer % 2 == 0:
            mw = prep_even(dict(
                w_in_t=fw["w_in_even"][i], w_uq=fw["mla_w_uq"][i], w_ukv=fw["mla_w_ukv"][i], w_out=fw["w_out_even"][i],
                mix_norm=fw["mix_norm_even"][i], q_norm=fw["mla_q_norm"][i], kv_norm=fw["mla_kv_norm"][i],
                q_head_norm=fw["mla_q_head_norm"][i], k_head_norm=fw["mla_k_head_norm"][i],
                theta_fwd=fw["ret_theta_fwd"][i], theta_bwd=fw["ret_theta_bwd"][i],
                ret_out_norm=fw["ret_out_norm"][i]))
        else:
            mw = prep_odd(dict(
                w_in_t=fw["w_in_odd"][i], w_gate_fwd=fw["gla_w_gate_fwd"][i], b_gate_fwd=fw["gla_b_gate_fwd"][i],
                w_gate_bwd=fw["gla_w_gate_bwd"][i], b_gate_bwd=fw["gla_b_gate_bwd"][i],
                gla_out_norm=fw["gla_out_norm"][i], w_out=fw["w_out_odd"][i], mix_norm=fw["mix_norm_odd"][i]))
        fwt = prep_ffn(dict(norm=fw["ffn_norm"][layer], w_up_t=fw["ffn_w_up"][layer], conv_w=fw["ffn_conv_w"][layer],
                            conv_b=fw["ffn_conv_b"][layer], w_down=fw["ffn_w_down"][layer]))
        layers.append((mw, fwt))

    saved = []
    for layer, (mw, fwt) in enumerate(layers):
        if layer % 2 == 0:
            x, sm = even_fwd(x, mw, tabs_mla, tabs_ret, f"l{layer}_mix")
        else:
            x, sm = odd_fwd(x, mw, f"l{layer}_mix")
        x, sf = ffn_fwd(x, fwt, f"l{layer}_ffn")
        saved.append((sm, sf))

    dx, loss = loss_head(x, target, name="loss_head")

    per_layer = [None] * DEPTH
    for layer in reversed(range(DEPTH)):
        mw, fwt = layers[layer]
        sm, sf = saved[layer]
        dx, gf = ffn_bwd(dx, sf, fwt, f"l{layer}_ffn")
        if layer % 2 == 0:
            dx, gm = even_bwd(dx, sm, mw, tabs_mla, tabs_ret, f"l{layer}_mix")
            gm = unprep_even_grads(gm, fw["ret_theta_fwd"][layer // 2], fw["ret_theta_bwd"][layer // 2])
        else:
            dx, gm = odd_bwd(dx, sm, mw, f"l{layer}_mix")
            gm = unprep_odd_grads(gm)
        per_layer[layer] = (gm, unprep_ffn_grads(gf))

    ev = [per_layer[l][0] for l in range(0, DEPTH, 2)]
    od = [per_layer[l][0] for l in range(1, DEPTH, 2)]
    ff = [per_layer[l][1] for l in range(DEPTH)]
    st = lambda lst, key: jnp.stack([g[key] for g in lst])
    grads = {
        "mix_norm_even": st(ev, "mix_norm"), "w_in_even": st(ev, "w_in_t"), "mla_q_norm": st(ev, "q_norm"),
        "mla_kv_norm": st(ev, "kv_norm"), "mla_w_uq": st(ev, "w_uq"), "mla_w_ukv": st(ev, "w_ukv"),
        "mla_q_head_norm": st(ev, "q_head_norm"), "mla_k_head_norm": st(ev, "k_head_norm"),
        "ret_theta_fwd": st(ev, "theta_fwd"), "ret_theta_bwd": st(ev, "theta_bwd"),
        "ret_out_norm": st(ev, "ret_out_norm"), "w_out_even": st(ev, "w_out"),
        "mix_norm_odd": st(od, "mix_norm"), "w_in_odd": st(od, "w_in_t"), "gla_w_gate_fwd": st(od, "w_gate_fwd"),
        "gla_b_gate_fwd": st(od, "b_gate_fwd"), "gla_w_gate_bwd": st(od, "w_gate_bwd"),
        "gla_b_gate_bwd": st(od, "b_gate_bwd"), "gla_out_norm": st(od, "gla_out_norm"), "w_out_odd": st(od, "w_out"),
        "ffn_norm": st(ff, "norm"), "ffn_w_up": st(ff, "w_up_t"), "ffn_conv_w": st(ff, "conv_w"),
        "ffn_conv_b": st(ff, "conv_b"), "ffn_w_down": st(ff, "w_down"),
    }
    return loss, dx, grads


def kernel(x, positions, mix_norm_even, w_in_even, mla_q_norm, mla_kv_norm, mla_w_uq, mla_w_ukv, mla_q_head_norm, mla_k_head_norm, ret_theta_fwd, ret_theta_bwd, ret_out_norm, w_out_even, mix_norm_odd, w_in_odd, gla_w_gate_fwd, gla_b_gate_fwd, gla_w_gate_bwd, gla_b_gate_bwd, gla_out_norm, w_out_odd, ffn_norm, ffn_w_up, ffn_conv_w, ffn_conv_b, ffn_w_down, loss_target, m_mix_norm_even, m_w_in_even, m_mla_q_norm, m_mla_kv_norm, m_mla_w_uq, m_mla_w_ukv, m_mla_q_head_norm, m_mla_k_head_norm, m_ret_theta_fwd, m_ret_theta_bwd, m_ret_out_norm, m_w_out_even, m_mix_norm_odd, m_w_in_odd, m_gla_w_gate_fwd, m_gla_b_gate_fwd, m_gla_w_gate_bwd, m_gla_b_gate_bwd, m_gla_out_norm, m_w_out_odd, m_ffn_norm, m_ffn_w_up, m_ffn_conv_w, m_ffn_conv_b, m_ffn_w_down, v_mix_norm_even, v_w_in_even, v_mla_q_norm, v_mla_kv_norm, v_mla_w_uq, v_mla_w_ukv, v_mla_q_head_norm, v_mla_k_head_norm, v_ret_theta_fwd, v_ret_theta_bwd, v_ret_out_norm, v_w_out_even, v_mix_norm_odd, v_w_in_odd, v_gla_w_gate_fwd, v_gla_b_gate_fwd, v_gla_w_gate_bwd, v_gla_b_gate_bwd, v_gla_out_norm, v_w_out_odd, v_ffn_norm, v_ffn_w_up, v_ffn_conv_w, v_ffn_conv_b, v_ffn_w_down):
    a = dict(locals())
    wts = {n: a[n] for n in WEIGHT_NAMES}
    local_shapes = {n: tuple(wts[n].shape) for n in WEIGHT_NAMES}

    gathered = all_gather_blocks(pack_gather_block(wts, local_shapes))
    fw = unpack_gathered(gathered, local_shapes)
    for n in REPLICATED:
        fw[n] = wts[n]

    loss, grad_x, grads = local_step(x[0], positions, loss_target[0], fw)

    recv = all_to_all_blocks(pack_grad_blocks(grads, local_shapes))
    g_slab = sum_slots(recv)
    n_big = sum(_row_counts(local_shapes))
    ms = {n: a["m_" + n] for n in WEIGHT_NAMES}
    vs = {n: a["v_" + n] for n in WEIGHT_NAMES}

    g_out = unpack_rows(g_slab[:n_big], local_shapes)
    d_out, m_out, v_out = {}, {}, {}
    for n, _ in ROW_FORM:
        loc = local_shapes[n]
        two_d = lambda t: t.reshape(-1, loc[-1])
        d, m, v = adamw(two_d(wts[n]), two_d(g_out[n]), two_d(ms[n]), two_d(vs[n]), name=f"adamw_{n}")
        d_out[n], m_out[n], v_out[n] = d.reshape(loc), m.reshape(loc), v.reshape(loc)
    g_tail = g_slab[n_big:]
    d_tail, m_tail, v_tail = adamw(pack_tail(wts, local_shapes), g_tail, pack_tail(ms, local_shapes),
                                   pack_tail(vs, local_shapes), name="adamw_small")
    g_out.update(unpack_tail(g_tail, local_shapes))
    d_out.update(unpack_tail(d_tail, local_shapes))
    m_out.update(unpack_tail(m_tail, local_shapes))
    v_out.update(unpack_tail(v_tail, local_shapes))
    total = lax.psum(loss[0, 0], MESH_AXES)
    return (total, grad_x[None], *[g_out[n] for n in WEIGHT_NAMES], *[d_out[n] for n in WEIGHT_NAMES],
            *[m_out[n] for n in WEIGHT_NAMES], *[v_out[n] for n in WEIGHT_NAMES])
```

```python
import math

import jax
import jax.numpy as jnp
from jax import lax
from jax.experimental import pallas as pl
from jax.experimental.pallas import tpu as pltpu

F32 = jnp.float32
BF16 = jnp.bfloat16

D_MODEL = 1024
DEPTH = 4
N_DEV = 8
MESH_AXES = ("x", "y", "c")

MLA_HEADS = 8
MLA_Q_RANK = 384
MLA_KV_RANK = 256
MLA_NOPE = 64
MLA_ROPE = 32
MLA_V = 64
MLA_QK = MLA_NOPE + MLA_ROPE
RET_HEADS = 8
RET_DK = 64
RET_DV = 64
RET_CHUNK = 128
GLA_HEADS = 4
GLA_DK = 128
GLA_DV = 256
GLA_GATE_RANK = 16
GLA_TAU = 16.0
GLA_CHUNK = 64
D_FF = 2816
ROPE_THETA = 10000.0
EPS = 1e-6

ADAM_LR = 0.001
ADAM_B1 = 0.9
ADAM_B2 = 0.999
ADAM_EPS = 1e-08
ADAM_WD = 0.01
ADAM_STEP = 10

LANE = 128
SUBLANE = 8
ROW_TILE = 512
VMEM_LIMIT = 56 * 1024 * 1024
WEIGHT_TILE_BYTES = 8 * 1024 * 1024

EV_CQ, EV_CKV, EV_KR, EV_RQ, EV_RK, EV_RV, EV_RG, EV_IN = 0, 384, 640, 768, 1280, 1792, 2304, 2816
OD_Q, OD_K, OD_V, OD_R, OD_GA, OD_IN = 0, 512, 1024, 2048, 3072, 3200
N_GROUPS = 4


def _cparams(sem):
    return pltpu.CompilerParams(dimension_semantics=sem, vmem_limit_bytes=VMEM_LIMIT)


def _dot(a, b):
    return jnp.dot(a.astype(BF16), b.astype(BF16), preferred_element_type=F32)


def _dot_nt(a, b):
    return lax.dot_general(a.astype(BF16), b.astype(BF16), (((1,), (1,)), ((), ())), preferred_element_type=F32)


def _dot_tn(a, b):
    return lax.dot_general(a.astype(BF16), b.astype(BF16), (((0,), (0,)), ((), ())), preferred_element_type=F32)


def _sigmoid(x):
    return 1.0 / (1.0 + jnp.exp(-x))


def _col_tile(k, n, itemsize=2):
    best = LANE
    for t in range(LANE, n + 1, LANE):
        if n % t == 0 and k * t * itemsize <= WEIGHT_TILE_BYTES:
            best = t
    return best if n % LANE == 0 else n


def _row_tile(m):
    return min(ROW_TILE, m)


def mm_nn(a, b, res=None, out_dtype=F32, name="mm_nn"):
    m, k = a.shape
    n = b.shape[1]
    tm, tn = _row_tile(m), _col_tile(k, n)

    def body(*refs):
        if res is None:
            a_ref, b_ref, o_ref = refs
        else:
            a_ref, b_ref, r_ref, o_ref = refs
        acc = _dot(a_ref[...], b_ref[...])
        if res is not None:
            acc = acc + r_ref[...].astype(F32)
        o_ref[...] = acc.astype(out_dtype)

    in_specs = [pl.BlockSpec((tm, k), lambda j, i: (i, 0)), pl.BlockSpec((k, tn), lambda j, i: (0, j))]
    args = [a, b]
    if res is not None:
        in_specs.append(pl.BlockSpec((tm, tn), lambda j, i: (i, j)))
        args.append(res)
    return pl.pallas_call(
        body, name=name, grid=(n // tn, m // tm), in_specs=in_specs,
        out_specs=pl.BlockSpec((tm, tn), lambda j, i: (i, j)),
        out_shape=jax.ShapeDtypeStruct((m, n), out_dtype),
        compiler_params=_cparams(("parallel", "parallel")),
    )(*args)


def mm_tn(a, b, out_dtype=F32, name="mm_tn"):
    t, k = a.shape
    n = b.shape[1]
    tt = min(2 * ROW_TILE, t)
    tk = k if k <= 1024 else _col_tile(1024, k, 4)
    tn = n if n <= 1024 else _col_tile(1024, n, 4)

    def body(a_ref, b_ref, o_ref, acc_s):
        s = pl.program_id(2)

        @pl.when(s == 0)
        def _():
            acc_s[...] = jnp.zeros_like(acc_s)
        acc_s[...] += _dot_tn(a_ref[...], b_ref[...])

        @pl.when(s == pl.num_programs(2) - 1)
        def _():
            o_ref[...] = acc_s[...].astype(out_dtype)

    return pl.pallas_call(
        body, name=name, grid=(k // tk, n // tn, t // tt),
        in_specs=[pl.BlockSpec((tt, tk), lambda i, j, s: (s, i)), pl.BlockSpec((tt, tn), lambda i, j, s: (s, j))],
        out_specs=pl.BlockSpec((tk, tn), lambda i, j, s: (i, j)),
        out_shape=jax.ShapeDtypeStruct((k, n), out_dtype),
        scratch_shapes=[pltpu.VMEM((tk, tn), F32)],
        compiler_params=_cparams(("parallel", "parallel", "arbitrary")),
    )(a, b)


def rmsnorm_fwd(x, g, name="rmsnorm_fwd"):
    t, d = x.shape
    tm = _row_tile(t)

    def body(x_ref, g_ref, h_ref):
        xv = x_ref[...]
        r = lax.rsqrt(jnp.mean(xv * xv, axis=-1, keepdims=True) + EPS)
        h_ref[...] = (xv * r * g_ref[...]).astype(BF16)

    return pl.pallas_call(
        body, name=name, grid=(t // tm,),
        in_specs=[pl.BlockSpec((tm, d), lambda i: (i, 0)), pl.BlockSpec((1, d), lambda i: (0, 0))],
        out_specs=pl.BlockSpec((tm, d), lambda i: (i, 0)),
        out_shape=jax.ShapeDtypeStruct((t, d), BF16),
        compiler_params=_cparams(("parallel",)),
    )(x, g.reshape(1, d))


def rmsnorm_bwd(x, g, dh, dres, name="rmsnorm_bwd"):
    t, d = x.shape
    tm = _row_tile(t)

    def body(x_ref, g_ref, dh_ref, dres_ref, dx_ref, dg_ref):
        @pl.when(pl.program_id(0) == 0)
        def _():
            dg_ref[...] = jnp.zeros_like(dg_ref)
        xv = x_ref[...]
        r = lax.rsqrt(jnp.mean(xv * xv, axis=-1, keepdims=True) + EPS)
        xh = xv * r
        dhv = dh_ref[...]
        dg_ref[...] += jnp.sum(dhv * xh, axis=0, keepdims=True)
        dxh = dhv * g_ref[...]
        dx_ref[...] = dres_ref[...] + r * (dxh - xh * jnp.mean(dxh * xh, axis=-1, keepdims=True))

    row = pl.BlockSpec((tm, d), lambda i: (i, 0))
    vec = pl.BlockSpec((1, d), lambda i: (0, 0))
    return pl.pallas_call(
        body, name=name, grid=(t // tm,),
        in_specs=[row, vec, row, row], out_specs=[row, vec],
        out_shape=[jax.ShapeDtypeStruct((t, d), F32), jax.ShapeDtypeStruct((1, d), F32)],
        compiler_params=_cparams(("arbitrary",)),
    )(x, g.reshape(1, d), dh, dres)


def _rope_apply(x, cos, s1, s2, half):
    return x * cos + pltpu.roll(x, LANE - half, 1) * s1 + pltpu.roll(x, half, 1) * s2


def _rope_transpose(dy, cos, s1, s2, half):
    return dy * cos + pltpu.roll(dy * s1, half, 1) + pltpu.roll(dy * s2, LANE - half, 1)


def rope_tables(positions, lane_start, half, period):
    pos = positions.reshape(-1).astype(F32)
    inv = ROPE_THETA ** (-jnp.arange(half, dtype=F32) / half)
    ang = pos[:, None] * inv[None, :]
    cos, sin = jnp.cos(ang), jnp.sin(ang)
    t = pos.shape[0]
    pre = lane_start
    post = period - lane_start - 2 * half
    ones = lambda n: jnp.ones((t, n), F32)
    zeros = lambda n: jnp.zeros((t, n), F32)
    c = jnp.concatenate([ones(pre), cos, cos, ones(post)], axis=1)
    a = jnp.concatenate([zeros(pre), -sin, zeros(half), zeros(post)], axis=1)
    b = jnp.concatenate([zeros(pre), zeros(half), sin, zeros(post)], axis=1)
    rep = LANE // period
    return tuple(jnp.tile(v, (1, rep)) for v in (c, a, b))


def _mla_forward_tile(p, cos, s1, s2, qn_g, kvn_g, wuq, wk, wv, qhn, khn):
    cq = p[:, EV_CQ:EV_CKV]
    ckv = p[:, EV_CKV:EV_KR]
    kr = p[:, EV_KR:EV_RQ]
    rq = lax.rsqrt(jnp.mean(cq * cq, axis=-1, keepdims=True) + EPS)
    rkv = lax.rsqrt(jnp.mean(ckv * ckv, axis=-1, keepdims=True) + EPS)
    qn = cq * rq * qn_g
    kvn = ckv * rkv * kvn_g
    q_raw = _dot(qn, wuq)
    k_raw = _dot(kvn, wk)
    v = _dot(kvn, wv)
    krp = pltpu.roll(kr, MLA_NOPE, 1)
    return cq, ckv, rq, rkv, qn, kvn, q_raw, k_raw, v, krp


def _head_norm(xh, g):
    r = lax.rsqrt(jnp.sum(xh * xh, axis=-1, keepdims=True) * (1.0 / MLA_QK) + EPS)
    return xh * r * g, r


def mla_prep_fwd(p, tabs, qn_g, kvn_g, wuq, wk, wv, qhn, khn, name="mla_prep_fwd"):
    t = p.shape[0]
    tm = _row_tile(t)
    hw = MLA_HEADS * LANE

    def body(p_ref, c_ref, s1_ref, s2_ref, qn_ref, kvn_ref, wuq_ref, wk_ref, wv_ref, qhn_ref, khn_ref,
             q_out, k_out, v_out):
        cos, s1, s2 = c_ref[...], s1_ref[...], s2_ref[...]
        (_, _, _, _, _, _, q_raw, k_raw, v, krp) = _mla_forward_tile(
            p_ref[...], cos, s1, s2, qn_ref[...], kvn_ref[...], wuq_ref[...], wk_ref[...], wv_ref[...],
            qhn_ref[...], khn_ref[...])
        v_out[...] = v.astype(BF16)
        for h in range(MLA_HEADS):
            sl = slice(h * LANE, (h + 1) * LANE)
            qh, _ = _head_norm(q_raw[:, sl], qhn_ref[...])
            kh, _ = _head_norm(k_raw[:, sl] + krp, khn_ref[...])
            q_out[:, sl] = (_rope_apply(qh, cos, s1, s2, MLA_ROPE // 2) * ATTN_QSCALE).astype(BF16)
            k_out[:, sl] = _rope_apply(kh, cos, s1, s2, MLA_ROPE // 2).astype(BF16)

    row = lambda w: pl.BlockSpec((tm, w), lambda i: (i, 0))
    full = lambda a: pl.BlockSpec(a.shape, lambda i: (0,) * a.ndim)
    ws = [qn_g, kvn_g, wuq, wk, wv, qhn, khn]
    return pl.pallas_call(
        body, name=name, grid=(t // tm,),
        in_specs=[row(EV_RQ), row(LANE), row(LANE), row(LANE)] + [full(w) for w in ws],
        out_specs=[row(hw)] * 3,
        out_shape=[jax.ShapeDtypeStruct((t, hw), BF16)] * 3,
        compiler_params=_cparams(("parallel",)),
    )(p, *tabs, *ws)


def mla_prep_bwd(p, tabs, qn_g, kvn_g, wuq, wk, wv, wuq_t, wk_t, wv_t, qhn, khn, dq, dk, dv,
                 name="mla_prep_bwd"):
    t = p.shape[0]
    tm = _row_tile(t)
    hw = MLA_HEADS * LANE

    def body(p_ref, c_ref, s1_ref, s2_ref, qn_ref, kvn_ref, wuq_ref, wk_ref, wv_ref, wuqt_ref, wkt_ref, wvt_ref,
             qhn_ref, khn_ref, dq_ref, dk_ref, dv_ref,
             dp_ref, dwuq_ref, dwk_ref, dwv_ref, dqn_ref, dkvn_ref, dqhn_ref, dkhn_ref, dqraw_s, dkraw_s):
        @pl.when(pl.program_id(0) == 0)
        def _():
            for r in (dwuq_ref, dwk_ref, dwv_ref, dqn_ref, dkvn_ref, dqhn_ref, dkhn_ref):
                r[...] = jnp.zeros_like(r)
        cos, s1, s2 = c_ref[...], s1_ref[...], s2_ref[...]
        qhn_v, khn_v = qhn_ref[...], khn_ref[...]
        (cq, ckv, rq, rkv, qn, kvn, q_raw, k_raw, _, krp) = _mla_forward_tile(
            p_ref[...], cos, s1, s2, qn_ref[...], kvn_ref[...], wuq_ref[...], wk_ref[...], wv_ref[...],
            qhn_v, khn_v)
        half = MLA_ROPE // 2
        dkr_sum = jnp.zeros((tm, LANE), F32)
        dqhn_acc = jnp.zeros((1, LANE), F32)
        dkhn_acc = jnp.zeros((1, LANE), F32)
        for h in range(MLA_HEADS):
            sl = slice(h * LANE, (h + 1) * LANE)
            xq = q_raw[:, sl]
            _, r = _head_norm(xq, qhn_v)
            xh = xq * r
            dy = _rope_transpose(dq_ref[:, sl] * ATTN_SCALE, cos, s1, s2, half)
            dqhn_acc = dqhn_acc + jnp.sum(dy * xh, axis=0, keepdims=True)
            dxh = dy * qhn_v
            dqraw_s[:, sl] = r * (dxh - xh * (jnp.sum(dxh * xh, axis=-1, keepdims=True) * (1.0 / MLA_QK)))
            xk = k_raw[:, sl] + krp
            _, r = _head_norm(xk, khn_v)
            xh = xk * r
            dy = _rope_transpose(dk_ref[:, sl] * math.log(2.0), cos, s1, s2, half)
            dkhn_acc = dkhn_acc + jnp.sum(dy * xh, axis=0, keepdims=True)
            dxh = dy * khn_v
            dxk = r * (dxh - xh * (jnp.sum(dxh * xh, axis=-1, keepdims=True) * (1.0 / MLA_QK)))
            dkraw_s[:, sl] = dxk
            dkr_sum = dkr_sum + dxk
        dqhn_ref[...] += dqhn_acc
        dkhn_ref[...] += dkhn_acc
        dq_raw = dqraw_s[...]
        dk_raw = dkraw_s[...]
        dvv = dv_ref[...]
        dwuq_ref[...] += _dot_tn(qn, dq_raw)
        dwk_ref[...] += _dot_tn(kvn, dk_raw)
        dwv_ref[...] += _dot_tn(kvn, dvv)
        dqn = _dot(dq_raw, wuqt_ref[...])
        dkvn = _dot(dk_raw, wkt_ref[...]) + _dot(dvv, wvt_ref[...])
        xh = cq * rq
        dqn_ref[...] += jnp.sum(dqn * xh, axis=0, keepdims=True)
        dxh = dqn * qn_ref[...]
        dcq = rq * (dxh - xh * jnp.mean(dxh * xh, axis=-1, keepdims=True))
        dp_ref[:, EV_CQ:EV_CKV] = dcq.astype(BF16)
        xh = ckv * rkv
        dkvn_ref[...] += jnp.sum(dkvn * xh, axis=0, keepdims=True)
        dxh = dkvn * kvn_ref[...]
        dckv = rkv * (dxh - xh * jnp.mean(dxh * xh, axis=-1, keepdims=True))
        dp_ref[:, EV_CKV:EV_KR] = dckv.astype(BF16)
        lane = lax.broadcasted_iota(jnp.int32, (tm, LANE), 1)
        dkr = jnp.where(lane < MLA_ROPE, pltpu.roll(dkr_sum, LANE - MLA_NOPE, 1), 0.0)
        dp_ref[:, EV_KR:EV_RQ] = dkr.astype(BF16)

    row = lambda w: pl.BlockSpec((tm, w), lambda i: (i, 0))
    full = lambda a: pl.BlockSpec(a.shape, lambda i: (0,) * a.ndim)
    ws = [qn_g, kvn_g, wuq, wk, wv, wuq_t, wk_t, wv_t, qhn, khn]
    outs = [jax.ShapeDtypeStruct((t, EV_RQ), BF16), jax.ShapeDtypeStruct(wuq.shape, F32),
            jax.ShapeDtypeStruct(wk.shape, F32), jax.ShapeDtypeStruct(wv.shape, F32),
            jax.ShapeDtypeStruct(qn_g.shape, F32), jax.ShapeDtypeStruct(kvn_g.shape, F32),
            jax.ShapeDtypeStruct(qhn.shape, F32), jax.ShapeDtypeStruct(khn.shape, F32)]
    return pl.pallas_call(
        body, name=name, grid=(t // tm,),
        in_specs=[row(EV_RQ), row(LANE), row(LANE), row(LANE)] + [full(w) for w in ws] + [row(hw)] * 3,
        out_specs=[row(EV_RQ)] + [full(o) for o in outs[1:]],
        out_shape=outs,
        scratch_shapes=[pltpu.VMEM((tm, hw), F32), pltpu.VMEM((tm, hw), F32)],
        compiler_params=_cparams(("arbitrary",)),
    )(p, *tabs, *ws, dq, dk, dv)


ATTN_SCALE = MLA_QK ** -0.5
ATTN_QSCALE = ATTN_SCALE * math.log2(math.e)
ATTN_FWD_TQ, ATTN_FWD_TK = 512, 8192
ATTN_BWD_TQ, ATTN_BWD_TK = 256, 4096


def attn_fwd(q, k, v, name="attn_fwd"):
    t = q.shape[0]
    tq, tk = min(ATTN_FWD_TQ, _row_tile(t)), min(ATTN_FWD_TK, t)
    nh = MLA_HEADS

    def body(q_ref, k_ref, v_ref, o_ref, lse_ref, m_s, l_s, acc_s):
        j = pl.program_id(2)

        @pl.when(j == 0)
        def _():
            m_s[...] = jnp.full_like(m_s, -jnp.inf)
            l_s[...] = jnp.zeros_like(l_s)
            acc_s[...] = jnp.zeros_like(acc_s)

        s = _dot_nt(q_ref[...], k_ref[...])
        m_old = m_s[...]
        m_new = jnp.maximum(m_old, jnp.max(s, axis=-1, keepdims=True))
        pr = jnp.exp2(s - m_new)
        alpha = jnp.exp2(m_old - m_new)
        l_s[...] = alpha * l_s[...] + jnp.sum(pr, axis=-1, keepdims=True)
        acc_s[...] = alpha * acc_s[...] + _dot(pr, v_ref[...])
        m_s[...] = m_new

        @pl.when(j == pl.num_programs(2) - 1)
        def _():
            o_ref[...] = acc_s[...] / l_s[...]
            lse_ref[...] = m_s[...] + jnp.log2(l_s[...])

    return pl.pallas_call(
        body, name=name, grid=(nh, t // tq, t // tk),
        in_specs=[pl.BlockSpec((tq, LANE), lambda h, i, j: (i, h)),
                  pl.BlockSpec((tk, LANE), lambda h, i, j: (j, h)),
                  pl.BlockSpec((tk, LANE), lambda h, i, j: (j, h))],
        out_specs=[pl.BlockSpec((tq, LANE), lambda h, i, j: (i, h)),
                   pl.BlockSpec((None, tq, 1), lambda h, i, j: (h, i, 0))],
        out_shape=[jax.ShapeDtypeStruct((t, nh * LANE), F32), jax.ShapeDtypeStruct((nh, t, 1), F32)],
        scratch_shapes=[pltpu.VMEM((tq, 1), F32), pltpu.VMEM((tq, 1), F32), pltpu.VMEM((tq, LANE), F32)],
        compiler_params=_cparams(("parallel", "parallel", "arbitrary")),
    )(q, k, v)


def attn_bwd(q, k, v, o, lse, do, name="attn_bwd"):
    t = q.shape[0]
    tq, tk = min(ATTN_BWD_TQ, _row_tile(t)), min(ATTN_BWD_TK, t)
    nh = MLA_HEADS
    nq = t // tq

    def body(q_ref, k_ref, v_ref, o_ref, lse_ref, do_ref, dq_ref, dk_ref, dv_ref):
        kj, qi = pl.program_id(1), pl.program_id(2)

        @pl.when(qi == 0)
        def _():
            dk_ref[...] = jnp.zeros_like(dk_ref)
            dv_ref[...] = jnp.zeros_like(dv_ref)

        qv, kv, vv, dov = q_ref[...], k_ref[...], v_ref[...], do_ref[...]
        s = _dot_nt(qv, kv)
        pr = jnp.exp2(s - lse_ref[...])
        dp = _dot_nt(dov, vv)
        delta = jnp.sum(dov * o_ref[...], axis=-1, keepdims=True)
        ds = pr * (dp - delta)
        dv_ref[...] += _dot_tn(pr, dov)
        dk_ref[...] += _dot_tn(ds, qv)
        dq_tile = _dot(ds, kv)
        rows = pl.ds(pl.multiple_of(qi * tq, tq), tq)

        @pl.when(kj == 0)
        def _():
            dq_ref[rows, :] = dq_tile

        @pl.when(kj != 0)
        def _():
            dq_ref[rows, :] += dq_tile

    qspec = pl.BlockSpec((tq, LANE), lambda h, j, i: (i, h))
    kspec = pl.BlockSpec((tk, LANE), lambda h, j, i: (j, h))
    return pl.pallas_call(
        body, name=name, grid=(nh, t // tk, nq),
        in_specs=[qspec, kspec, kspec, qspec, pl.BlockSpec((None, tq, 1), lambda h, j, i: (h, i, 0)), qspec],
        out_specs=[pl.BlockSpec((t, LANE), lambda h, j, i: (0, h)), kspec, kspec],
        out_shape=[jax.ShapeDtypeStruct((t, nh * LANE), F32)] * 3,
        compiler_params=_cparams(("parallel", "arbitrary", "arbitrary")),
    )(q, k, v, o, lse, do)


def _scan_consts(c, reverse, inclusive):
    ii = lax.broadcasted_iota(jnp.int32, (c, c), 0)
    jj = lax.broadcasted_iota(jnp.int32, (c, c), 1)
    if reverse:
        incl = jj >= ii
        mask = incl if inclusive else jj > ii
    else:
        incl = jj <= ii
        mask = incl if inclusive else jj < ii
    mid = (c - 1 - c // 2) if reverse else c // 2
    incl_t = (jj <= ii) if reverse else (jj >= ii)
    return incl.astype(F32), incl_t.astype(F32), mask.astype(F32), mid


def _dot_split(a01, x):
    hi = x.astype(BF16)
    lo = (x - hi.astype(F32)).astype(BF16)
    a = a01.astype(BF16)
    return jnp.dot(a, hi, preferred_element_type=F32) + jnp.dot(a, lo, preferred_element_type=F32)


def _sub_masks(sub, dvg, u):
    if sub == 1:
        return None, None
    kl = lax.broadcasted_iota(jnp.int32, (1, LANE), 1)
    vl = lax.broadcasted_iota(jnp.int32, (1, dvg), 1)
    kw, vw = LANE // sub, dvg // sub
    km = (kl >= u * kw) & (kl < (u + 1) * kw)
    vm = (vl >= u * vw) & (vl < (u + 1) * vw)
    return km.astype(F32), vm.astype(F32)


def _block_incl(r, c, reverse, transposed):
    shift = c.bit_length() - 1
    ii = lax.broadcasted_iota(jnp.int32, (r, r), 0)
    jj = lax.broadcasted_iota(jnp.int32, (r, r), 1)
    same = lax.shift_right_logical(ii, shift) == lax.shift_right_logical(jj, shift)
    lower = (jj <= ii) if (reverse == transposed) else (jj >= ii)
    return (same & lower).astype(F32)


def _scan_chunk_fwd(b, la, mid):
    row = lax.broadcasted_iota(jnp.int32, b.shape, 0)
    bm = jnp.sum(jnp.where(row == mid, b, 0.0), axis=0, keepdims=True)
    tot = jnp.sum(la, axis=0, keepdims=True)
    e_qc = jnp.exp(b - bm)
    e_kc = jnp.exp(bm - b)
    e_qe = jnp.exp(b)
    e_kd = jnp.exp(tot - b)
    return e_qc, e_kc, e_qe, e_kd


def scan_fwd(q_arr, k_arr, v_arr, la_arr, *, qcb, kcb, vcb, lacb, la_row, chunk, dvg, sub, reverse, inclusive,
             qscale, kscale, rope=None, name="scan_fwd"):
    t = q_arr.shape[0]
    r = _row_tile(t)
    nb, nc = t // r, r // chunk
    c = chunk
    rb = (lambda j: nb - 1 - j) if reverse else (lambda j: j)
    order = list(range(nc))[::-1] if reverse else list(range(nc))
    half = RET_DK // 2

    def body(*refs):
        if rope is None:
            q_ref, k_ref, v_ref, la_ref, o_ref, st_ref, s_s = refs
        else:
            q_ref, k_ref, v_ref, la_ref, c_ref, s1_ref, s2_ref, o_ref, st_ref, s_s = refs

        @pl.when(pl.program_id(1) == 0)
        def _():
            s_s[...] = jnp.zeros_like(s_s)

        incl, _, mask, mid = _scan_consts(c, reverse, inclusive)
        for ci in order:
            rows = slice(ci * c, (ci + 1) * c)
            qv = q_ref[rows, :] * qscale
            kv = k_ref[rows, :] * kscale
            if rope is not None:
                cs, a1, a2 = c_ref[rows, :], s1_ref[rows, :], s2_ref[rows, :]
                qv = _rope_apply(qv, cs, a1, a2, half)
                kv = _rope_apply(kv, cs, a1, a2, half)
            la = jnp.broadcast_to(la_ref[...], (c, LANE)) if la_row else la_ref[rows, :]
            vv = v_ref[rows, :]
            e_qc, e_kc, e_qe, e_kd = _scan_chunk_fwd(_dot_split(incl, la), la, mid)
            qc, kc, qe, kd = qv * e_qc, kv * e_kc, qv * e_qe, kv * e_kd
            sg = s_s[...]
            st_ref[ci] = sg
            acc = None
            for u in range(sub):
                mu, vmu = _sub_masks(sub, dvg, u)
                qcu = qc if mu is None else qc * mu
                qeu = qe if mu is None else qe * mu
                a = _dot_nt(qcu, kc) * mask
                ou = _dot(a, vv) + _dot_nt(qeu, sg)
                ou = ou if vmu is None else ou * vmu
                acc = ou if acc is None else acc + ou
            o_ref[rows, :] = acc
            decay = jnp.exp(jnp.sum(la, axis=0, keepdims=True))
            s_s[...] = decay * sg + _dot_tn(vv, kd)

    specs = [pl.BlockSpec((r, LANE), lambda g, j: (rb(j), qcb + g)),
             pl.BlockSpec((r, LANE), lambda g, j: (rb(j), kcb + g)),
             pl.BlockSpec((r, dvg), lambda g, j: (rb(j), vcb + g)),
             pl.BlockSpec((1, LANE), lambda g, j: (0, lacb + g)) if la_row
             else pl.BlockSpec((r, LANE), lambda g, j: (rb(j), lacb + g))]
    args = [q_arr, k_arr, v_arr, la_arr]
    if rope is not None:
        specs += [pl.BlockSpec((r, LANE), lambda g, j: (rb(j), 0))] * 3
        args += list(rope)
    return pl.pallas_call(
        body, name=name, grid=(N_GROUPS, nb), in_specs=specs,
        out_specs=[pl.BlockSpec((r, dvg), lambda g, j: (rb(j), g)),
                   pl.BlockSpec((nc, dvg, LANE), lambda g, j: (rb(j), g, 0))],
        out_shape=[jax.ShapeDtypeStruct((t, N_GROUPS * dvg), F32),
                   jax.ShapeDtypeStruct((t // c, N_GROUPS * dvg, LANE), F32)],
        scratch_shapes=[pltpu.VMEM((dvg, LANE), F32)],
        compiler_params=_cparams(("parallel", "arbitrary")),
    )(*args)


def scan_bwd(q_arr, k_arr, v_arr, la_arr, st_arr, do_arr, prev, *, qcb, kcb, vcb, lacb, la_row, chunk, dvg, sub,
             reverse, inclusive, qscale, kscale, rope=None, name="scan_bwd"):
    t = q_arr.shape[0]
    r = _row_tile(t)
    nb, nc = t // r, r // chunk
    c = chunk
    rb = (lambda j: j) if reverse else (lambda j: nb - 1 - j)
    order = list(range(nc)) if reverse else list(range(nc))[::-1]
    half = RET_DK // 2
    n_in = 6 + (3 if rope is not None else 0) + (3 if prev is not None else 0)
    gdt = F32 if prev is None else BF16

    def body(*refs):
        ins, outs = refs[:n_in], refs[n_in:]
        q_ref, k_ref, v_ref, la_ref, st_ref, do_ref = ins[:6]
        pos = 6
        if rope is not None:
            c_ref, s1_ref, s2_ref = ins[pos:pos + 3]
            pos += 3
        if prev is not None:
            pq_ref, pk_ref, pv_ref = ins[pos:pos + 3]
        dq_ref, dk_ref, dv_ref, dla_ref, g_s = outs

        @pl.when(pl.program_id(1) == 0)
        def _():
            g_s[...] = jnp.zeros_like(g_s)
            if la_row:
                dla_ref[...] = jnp.zeros_like(dla_ref)

        incl, _, mask, mid = _scan_consts(c, reverse, inclusive)
        b_all = None if la_row else _dot_split(_block_incl(r, c, reverse, False), la_ref[...])
        pos = lax.broadcasted_iota(jnp.int32, (c, LANE), 0)
        cnt = ((c - pos) if reverse else (pos + 1)).astype(F32)
        dla_sum = jnp.zeros((1, LANE), F32)
        db_parts, dtot_parts = [None] * nc, [None] * nc
        for ci in order:
            rows = slice(ci * c, (ci + 1) * c)
            qv = q_ref[rows, :] * qscale
            kv = k_ref[rows, :] * kscale
            if rope is not None:
                cs, a1, a2 = c_ref[rows, :], s1_ref[rows, :], s2_ref[rows, :]
                qv = _rope_apply(qv, cs, a1, a2, half)
                kv = _rope_apply(kv, cs, a1, a2, half)
            la = jnp.broadcast_to(la_ref[...], (c, LANE)) if la_row else la_ref[rows, :]
            vv = v_ref[rows, :]
            dov = do_ref[rows, :]
            b = _dot_split(incl, la) if la_row else b_all[rows, :]
            e_qc, e_kc, e_qe, e_kd = _scan_chunk_fwd(b, la, mid)
            qc, kc, qe, kd = qv * e_qc, kv * e_kc, qv * e_qe, kv * e_kd
            sg = st_ref[ci]
            gn = g_s[...]
            dqc = jnp.zeros((c, LANE), F32)
            dkc = jnp.zeros((c, LANE), F32)
            dqe = jnp.zeros((c, LANE), F32)
            dvv = _dot_nt(kd, gn)
            ds_direct = jnp.zeros((dvg, LANE), F32)
            for u in range(sub):
                mu, vmu = _sub_masks(sub, dvg, u)
                qcu = qc if mu is None else qc * mu
                qeu = qe if mu is None else qe * mu
                dou = dov if vmu is None else dov * vmu
                a = _dot_nt(qcu, kc) * mask
                da = _dot_nt(dou, vv) * mask
                dvv = dvv + _dot_tn(a, dou)
                t1 = _dot(da, kc)
                dqc = dqc + (t1 if mu is None else t1 * mu)
                dkc = dkc + _dot_tn(da, qcu)
                t2 = _dot(dou, sg)
                dqe = dqe + (t2 if mu is None else t2 * mu)
                ds_direct = ds_direct + _dot_tn(dou, qeu)
            dkd = _dot(vv, gn)
            decay = jnp.exp(jnp.sum(la, axis=0, keepdims=True))
            dtot = jnp.sum(gn * sg, axis=0, keepdims=True) * decay + jnp.sum(dkd * kd, axis=0, keepdims=True)
            db = dqc * qc - dkc * kc + dqe * qe - dkd * kd
            if la_row:
                dla_sum = dla_sum + jnp.sum(db * cnt, axis=0, keepdims=True) + float(c) * dtot
            else:
                db_parts[ci] = db
                dtot_parts[ci] = jnp.broadcast_to(dtot, (c, LANE))
            dqv = dqc * e_qc + dqe * e_qe
            dkv = dkc * e_kc + dkd * e_kd
            if rope is not None:
                dqv = _rope_transpose(dqv, cs, a1, a2, half)
                dkv = _rope_transpose(dkv, cs, a1, a2, half)
            dqv = dqv * qscale
            dkv = dkv * kscale
            if prev is not None:
                dqv = dqv + pq_ref[rows, :]
                dkv = dkv + pk_ref[rows, :]
                dvv = dvv + pv_ref[rows, :]
            dq_ref[rows, :] = dqv.astype(gdt)
            dk_ref[rows, :] = dkv.astype(gdt)
            dv_ref[rows, :] = dvv.astype(gdt)
            g_s[...] = ds_direct + decay * gn
        if la_row:
            dla_ref[...] += dla_sum
        else:
            db_all = jnp.concatenate(db_parts, axis=0)
            dla_ref[...] = _dot_split(_block_incl(r, c, reverse, True), db_all) + jnp.concatenate(dtot_parts, axis=0)

    kblk = lambda cb: pl.BlockSpec((r, LANE), lambda g, j: (rb(j), cb + g))
    vblk = lambda cb: pl.BlockSpec((r, dvg), lambda g, j: (rb(j), cb + g))
    specs = [kblk(qcb), kblk(kcb), vblk(vcb),
             pl.BlockSpec((1, LANE), lambda g, j: (0, lacb + g)) if la_row else kblk(lacb),
             pl.BlockSpec((nc, dvg, LANE), lambda g, j: (rb(j), g, 0)), vblk(0)]
    args = [q_arr, k_arr, v_arr, la_arr, st_arr, do_arr]
    if rope is not None:
        specs += [pl.BlockSpec((r, LANE), lambda g, j: (rb(j), 0))] * 3
        args += list(rope)
    if prev is not None:
        specs += [kblk(0), kblk(0), vblk(0)]
        args += list(prev)
    wk = N_GROUPS * LANE
    outs = [jax.ShapeDtypeStruct((t, wk), gdt), jax.ShapeDtypeStruct((t, wk), gdt),
            jax.ShapeDtypeStruct((t, N_GROUPS * dvg), gdt),
            jax.ShapeDtypeStruct((1, wk) if la_row else (t, wk), F32)]
    return pl.pallas_call(
        body, name=name, grid=(N_GROUPS, nb), in_specs=specs,
        out_specs=[kblk(0), kblk(0), vblk(0),
                   pl.BlockSpec((1, LANE), lambda g, j: (0, g)) if la_row else kblk(0)],
        out_shape=outs,
        scratch_shapes=[pltpu.VMEM((dvg, LANE), F32)],
        compiler_params=_cparams(("parallel", "arbitrary")),
    )(*args)


def _seg_mean(x, seg):
    w = x.shape[1]
    if seg % LANE == 0:
        parts = []
        for s in range(0, w, seg):
            m = jnp.mean(x[:, s:s + seg], axis=-1, keepdims=True)
            parts.append(jnp.broadcast_to(m, (x.shape[0], seg)))
        return jnp.concatenate(parts, axis=1)
    shift = seg.bit_length() - 1
    ii = lax.shift_right_logical(lax.broadcasted_iota(jnp.int32, (w, w), 0), shift)
    jj = lax.shift_right_logical(lax.broadcasted_iota(jnp.int32, (w, w), 1), shift)
    e = (ii == jj).astype(BF16)
    hi = x.astype(BF16)
    lo = (x - hi.astype(F32)).astype(BF16)
    return (jnp.dot(hi, e, preferred_element_type=F32) + jnp.dot(lo, e, preferred_element_type=F32)) * (1.0 / seg)


def gated_norm_fwd(o_f, o_b, gate_arr, gcb, gn, seg, name="gated_norm_fwd"):
    t, w = o_f.shape
    tm = _row_tile(t)

    def body(of_ref, ob_ref, g_ref, gn_ref, y_ref):
        o = of_ref[...] + ob_ref[...]
        r = lax.rsqrt(_seg_mean(o * o, seg) + EPS)
        gt = g_ref[...]
        y_ref[...] = (gt * _sigmoid(gt) * (o * r * gn_ref[...])).astype(BF16)

    bw = max(seg, LANE)
    row = pl.BlockSpec((tm, bw), lambda j, i: (i, j))
    return pl.pallas_call(
        body, name=name, grid=(w // bw, t // tm),
        in_specs=[row, row, pl.BlockSpec((tm, bw), lambda j, i: (i, gcb + j)),
                  pl.BlockSpec((1, bw), lambda j, i: (0, j))],
        out_specs=row, out_shape=jax.ShapeDtypeStruct((t, w), BF16),
        compiler_params=_cparams(("parallel", "parallel")),
    )(o_f, o_b, gate_arr, gn.reshape(1, w))


def gated_norm_bwd(o_f, o_b, gate_arr, gcb, gn, seg, dy, name="gated_norm_bwd"):
    t, w = o_f.shape
    tm = _row_tile(t)

    def body(of_ref, ob_ref, g_ref, gn_ref, dy_ref, do_ref, dg_ref, dgn_ref):
        @pl.when(pl.program_id(1) == 0)
        def _():
            dgn_ref[...] = jnp.zeros_like(dgn_ref)
        o = of_ref[...] + ob_ref[...]
        r = lax.rsqrt(_seg_mean(o * o, seg) + EPS)
        xh = o * r
        gt = g_ref[...]
        sg = _sigmoid(gt)
        dyv = dy_ref[...]
        n = xh * gn_ref[...]
        dg_ref[...] = (dyv * n * (sg * (1.0 + gt * (1.0 - sg)))).astype(BF16)
        dn = dyv * (gt * sg)
        dgn_ref[...] += jnp.sum(dn * xh, axis=0, keepdims=True)
        dxh = dn * gn_ref[...]
        do_ref[...] = r * (dxh - xh * _seg_mean(dxh * xh, seg))

    bw = max(seg, LANE)
    row = pl.BlockSpec((tm, bw), lambda j, i: (i, j))
    vec = pl.BlockSpec((1, bw), lambda j, i: (0, j))
    return pl.pallas_call(
        body, name=name, grid=(w // bw, t // tm),
        in_specs=[row, row, pl.BlockSpec((tm, bw), lambda j, i: (i, gcb + j)), vec, row],
        out_specs=[row, row, vec],
        out_shape=[jax.ShapeDtypeStruct((t, w), F32), jax.ShapeDtypeStruct((t, w), BF16),
                   jax.ShapeDtypeStruct((1, w), F32)],
        compiler_params=_cparams(("parallel", "arbitrary")),
    )(o_f, o_b, gate_arr, gn.reshape(1, w), dy)


def gla_gate_fwd(p, wg, bg, name="gla_gate_fwd"):
    t = p.shape[0]
    tm = _row_tile(t)
    w = wg.shape[1]
    gcb = OD_GA // LANE

    def body(ga_ref, wg_ref, bg_ref, la_ref):
        z = _dot(ga_ref[...], wg_ref[...]) + bg_ref[...]
        la_ref[...] = (jnp.minimum(z, 0.0) - jnp.log(1.0 + jnp.exp(-jnp.abs(z)))) * (1.0 / GLA_TAU)

    return pl.pallas_call(
        body, name=name, grid=(t // tm,),
        in_specs=[pl.BlockSpec((tm, LANE), lambda i: (i, gcb)), pl.BlockSpec((LANE, w), lambda i: (0, 0)),
                  pl.BlockSpec((1, w), lambda i: (0, 0))],
        out_specs=pl.BlockSpec((tm, w), lambda i: (i, 0)),
        out_shape=jax.ShapeDtypeStruct((t, w), F32),
        compiler_params=_cparams(("parallel",)),
    )(p, wg, bg)


def gla_gate_bwd(p, wg, wg_t, bg, dla, name="gla_gate_bwd"):
    t = p.shape[0]
    tm = _row_tile(t)
    w = wg.shape[1]
    gcb = OD_GA // LANE

    def body(ga_ref, wg_ref, wgt_ref, bg_ref, dla_ref, dga_ref, dwg_ref, dbg_ref):
        @pl.when(pl.program_id(0) == 0)
        def _():
            dwg_ref[...] = jnp.zeros_like(dwg_ref)
            dbg_ref[...] = jnp.zeros_like(dbg_ref)
        ga = ga_ref[...]
        z = _dot(ga, wg_ref[...]) + bg_ref[...]
        dz = dla_ref[...] * (1.0 / GLA_TAU) * _sigmoid(-z)
        dga_ref[...] = _dot(dz, wgt_ref[...]).astype(BF16)
        dwg_ref[...] += _dot_tn(ga, dz)
        dbg_ref[...] += jnp.sum(dz, axis=0, keepdims=True)

    return pl.pallas_call(
        body, name=name, grid=(t // tm,),
        in_specs=[pl.BlockSpec((tm, LANE), lambda i: (i, gcb)), pl.BlockSpec((LANE, w), lambda i: (0, 0)),
                  pl.BlockSpec((w, LANE), lambda i: (0, 0)), pl.BlockSpec((1, w), lambda i: (0, 0)),
                  pl.BlockSpec((tm, w), lambda i: (i, 0))],
        out_specs=[pl.BlockSpec((tm, LANE), lambda i: (i, 0)), pl.BlockSpec((LANE, w), lambda i: (0, 0)),
                   pl.BlockSpec((1, w), lambda i: (0, 0))],
        out_shape=[jax.ShapeDtypeStruct((t, LANE), BF16), jax.ShapeDtypeStruct((LANE, w), F32),
                   jax.ShapeDtypeStruct((1, w), F32)],
        compiler_params=_cparams(("arbitrary",)),
    )(p, wg, wg_t, bg, dla)


FFN_COL = 1408


def _shifted(x, prev_row, next_row, first, last):
    tm = x.shape[0]
    row = lax.broadcasted_iota(jnp.int32, x.shape, 0)
    pr = jnp.where(first, 0.0, prev_row)
    nx = jnp.where(last, 0.0, next_row)
    xm1 = jnp.where(row == 0, pr, pltpu.roll(x, 1, 0))
    xp1 = jnp.where(row == tm - 1, nx, pltpu.roll(x, tm - 1, 0))
    return xm1, xp1


def _halo_specs(tm, tc, t, colmap, rowaxis):
    nb8 = tm // SUBLANE
    last8 = t // SUBLANE - 1

    def prev(*ids):
        i = ids[rowaxis]
        return (jnp.maximum(i * nb8 - 1, 0), colmap(*ids))

    def nxt(*ids):
        i = ids[rowaxis]
        return (jnp.minimum((i + 1) * nb8, last8), colmap(*ids))

    return pl.BlockSpec((SUBLANE, tc), prev), pl.BlockSpec((SUBLANE, tc), nxt)


def ffn_act_fwd(up, conv_w, conv_b, name="ffn_act_fwd"):
    t = up.shape[0]
    tm, tc = _row_tile(t), FFN_COL
    ncol = D_FF // tc

    def body(g_ref, gp_ref, gn_ref, v_ref, w_ref, b_ref, a_ref):
        i = pl.program_id(0)
        g = g_ref[...]
        gm1, gp1 = _shifted(g, gp_ref[SUBLANE - 1:SUBLANE, :], gn_ref[0:1, :], i == 0, i == pl.num_programs(0) - 1)
        cc = w_ref[0:1, :] * gm1 + w_ref[1:2, :] * g + w_ref[2:3, :] * gp1 + b_ref[...]
        a_ref[...] = (cc * _sigmoid(cc) * v_ref[...]).astype(BF16)

    prev, nxt = _halo_specs(tm, tc, t, lambda i, j: j, 0)
    return pl.pallas_call(
        body, name=name, grid=(t // tm, ncol),
        in_specs=[pl.BlockSpec((tm, tc), lambda i, j: (i, j)), prev, nxt,
                  pl.BlockSpec((tm, tc), lambda i, j: (i, j + ncol)),
                  pl.BlockSpec((SUBLANE, tc), lambda i, j: (0, j)), pl.BlockSpec((1, tc), lambda i, j: (0, j))],
        out_specs=pl.BlockSpec((tm, tc), lambda i, j: (i, j)),
        out_shape=jax.ShapeDtypeStruct((t, D_FF), BF16),
        compiler_params=_cparams(("parallel", "parallel")),
    )(up, up, up, up, conv_w, conv_b)


def ffn_act_bwd(up, conv_w, conv_b, dact, name="ffn_act_bwd"):
    t = up.shape[0]
    tm, tc = _row_tile(t), FFN_COL
    ncol = D_FF // tc

    def body(g_ref, gp_ref, gn_ref, v_ref, w_ref, b_ref, da_ref, dc_ref, dv_ref, dw_ref):
        i = pl.program_id(1)

        @pl.when(i == 0)
        def _():
            dw_ref[...] = jnp.zeros_like(dw_ref)
        g = g_ref[...]
        gm1, gp1 = _shifted(g, gp_ref[SUBLANE - 1:SUBLANE, :], gn_ref[0:1, :], i == 0, i == pl.num_programs(1) - 1)
        cc = w_ref[0:1, :] * gm1 + w_ref[1:2, :] * g + w_ref[2:3, :] * gp1 + b_ref[...]
        sg = _sigmoid(cc)
        da = da_ref[...]
        dv_ref[...] = (da * (cc * sg)).astype(BF16)
        dc = da * v_ref[...] * (sg * (1.0 + cc * (1.0 - sg)))
        dc_ref[...] = dc
        dw_ref[0:1, :] += jnp.sum(dc * gm1, axis=0, keepdims=True)
        dw_ref[1:2, :] += jnp.sum(dc * g, axis=0, keepdims=True)
        dw_ref[2:3, :] += jnp.sum(dc * gp1, axis=0, keepdims=True)
        dw_ref[3:4, :] += jnp.sum(dc, axis=0, keepdims=True)

    prev, nxt = _halo_specs(tm, tc, t, lambda j, i: j, 1)
    tile = pl.BlockSpec((tm, tc), lambda j, i: (i, j))
    return pl.pallas_call(
        body, name=name, grid=(ncol, t // tm),
        in_specs=[tile, prev, nxt, pl.BlockSpec((tm, tc), lambda j, i: (i, j + ncol)),
                  pl.BlockSpec((SUBLANE, tc), lambda j, i: (0, j)), pl.BlockSpec((1, tc), lambda j, i: (0, j)), tile],
        out_specs=[tile, tile, pl.BlockSpec((SUBLANE, tc), lambda j, i: (0, j))],
        out_shape=[jax.ShapeDtypeStruct((t, D_FF), F32), jax.ShapeDtypeStruct((t, D_FF), BF16),
                   jax.ShapeDtypeStruct((SUBLANE, D_FF), F32)],
        compiler_params=_cparams(("parallel", "arbitrary")),
    )(up, up, up, up, conv_w, conv_b, dact)


def conv_transpose(dc, conv_w, name="conv_transpose"):
    t = dc.shape[0]
    tm, tc = _row_tile(t), FFN_COL

    def body(d_ref, dp_ref, dn_ref, w_ref, o_ref):
        i = pl.program_id(0)
        d = d_ref[...]
        dm1, dp1 = _shifted(d, dp_ref[SUBLANE - 1:SUBLANE, :], dn_ref[0:1, :], i == 0, i == pl.num_programs(0) - 1)
        o_ref[...] = (w_ref[0:1, :] * dp1 + w_ref[1:2, :] * d + w_ref[2:3, :] * dm1).astype(BF16)

    prev, nxt = _halo_specs(tm, tc, t, lambda i, j: j, 0)
    tile = pl.BlockSpec((tm, tc), lambda i, j: (i, j))
    return pl.pallas_call(
        body, name=name, grid=(t // tm, D_FF // tc),
        in_specs=[tile, prev, nxt, pl.BlockSpec((SUBLANE, tc), lambda i, j: (0, j))],
        out_specs=tile, out_shape=jax.ShapeDtypeStruct((t, D_FF), BF16),
        compiler_params=_cparams(("parallel", "parallel")),
    )(dc, dc, dc, conv_w)


def loss_head(y, target, name="loss_head"):
    t, d = y.shape
    tm = _row_tile(t)

    def body(y_ref, t_ref, dy_ref, l_ref):
        @pl.when(pl.program_id(0) == 0)
        def _():
            l_ref[...] = jnp.zeros_like(l_ref)
        e = y_ref[...] - t_ref[...]
        dy_ref[...] = e * (1.0 / d)
        rowloss = jnp.sum(e * e, axis=-1, keepdims=True) * (0.5 / d)
        l_ref[...] += jnp.sum(rowloss, axis=0, keepdims=True)

    row = pl.BlockSpec((tm, d), lambda i: (i, 0))
    return pl.pallas_call(
        body, name=name, grid=(t // tm,), in_specs=[row, row],
        out_specs=[row, pl.BlockSpec((1, 1), lambda i: (0, 0))],
        out_shape=[jax.ShapeDtypeStruct((t, d), F32), jax.ShapeDtypeStruct((1, 1), F32)],
        compiler_params=_cparams(("arbitrary",)),
    )(y, target)


def _pad_heads(w, heads, width):
    lead = w.shape[:-1]
    w = w.reshape(*lead, heads, width)
    w = jnp.pad(w, [(0, 0)] * len(lead) + [(0, 0), (0, LANE - width)])
    return w.reshape(*lead, heads * LANE)


def _unpad_heads(w, heads, width):
    lead = w.shape[:-1]
    return w.reshape(*lead, heads, LANE)[..., :width].reshape(*lead, heads * width)


def _pad_rows_heads(w, heads, width):
    return _pad_heads(w.T, heads, width).T


def _unpad_rows_heads(w, heads, width):
    return _unpad_heads(w.T, heads, width).T


_EV_REAL = MLA_Q_RANK + MLA_KV_RANK + MLA_ROPE


def prep_even(wts, dt=BF16):
    w_in_t = wts["w_in_t"]
    w_in_tp = jnp.concatenate([w_in_t[:_EV_REAL], jnp.zeros((EV_RQ - _EV_REAL, D_MODEL), w_in_t.dtype),
                               w_in_t[_EV_REAL:]], axis=0).astype(dt)
    wuq = _pad_heads(wts["w_uq"], MLA_HEADS, MLA_QK).astype(dt)
    ukv = wts["w_ukv"].reshape(MLA_KV_RANK, MLA_HEADS, MLA_NOPE + MLA_V)
    wk = _pad_heads(ukv[..., :MLA_NOPE].reshape(MLA_KV_RANK, -1), MLA_HEADS, MLA_NOPE).astype(dt)
    wv = _pad_heads(ukv[..., MLA_NOPE:].reshape(MLA_KV_RANK, -1), MLA_HEADS, MLA_V).astype(dt)
    w_out = wts["w_out"]
    wa = _pad_rows_heads(w_out[:MLA_HEADS * MLA_V], MLA_HEADS, MLA_V).astype(dt)
    wr = w_out[MLA_HEADS * MLA_V:].astype(dt)
    pad1 = lambda v, n: jnp.pad(v.astype(F32), (0, n - v.shape[0])).reshape(1, n)
    lg = lambda th: jnp.log1p(-jnp.exp2(-th.astype(F32)))
    return dict(
        w_in=w_in_tp.T, w_in_t=w_in_tp, wuq=wuq, wuq_t=wuq.T, wk=wk, wk_t=wk.T, wv=wv, wv_t=wv.T,
        wa=wa, wa_t=wa.T, wr=wr, wr_t=wr.T,
        mix_norm=wts["mix_norm"].astype(F32), q_norm=wts["q_norm"].astype(F32).reshape(1, -1),
        kv_norm=wts["kv_norm"].astype(F32).reshape(1, -1),
        qhn=pad1(wts["q_head_norm"], LANE), khn=pad1(wts["k_head_norm"], LANE),
        la_f=jnp.repeat(lg(wts["theta_fwd"]), RET_DK).reshape(1, -1),
        la_b=jnp.repeat(lg(wts["theta_bwd"]), RET_DK).reshape(1, -1),
        out_norm=wts["ret_out_norm"].astype(F32).reshape(-1),
    )


def prep_odd(wts, dt=BF16):
    w_in_t = wts["w_in_t"]
    w_in_tp = jnp.concatenate([w_in_t, jnp.zeros((OD_IN - w_in_t.shape[0], D_MODEL), w_in_t.dtype)],
                              axis=0).astype(dt)
    hk = GLA_HEADS * GLA_DK
    wg = jnp.zeros((LANE, 2 * hk), F32)
    wg = wg.at[:GLA_GATE_RANK, :hk].set(wts["w_gate_fwd"].astype(F32))
    wg = wg.at[GLA_GATE_RANK:2 * GLA_GATE_RANK, hk:].set(wts["w_gate_bwd"].astype(F32))
    wg = wg.astype(dt)
    bg = jnp.concatenate([wts["b_gate_fwd"], wts["b_gate_bwd"]]).astype(F32).reshape(1, -1)
    w_out = wts["w_out"].astype(dt)
    return dict(w_in=w_in_tp.T, w_in_t=w_in_tp, wg=wg, wg_t=wg.T, bg=bg, w_out=w_out, w_out_t=w_out.T,
                mix_norm=wts["mix_norm"].astype(F32), out_norm=wts["gla_out_norm"].astype(F32).reshape(-1))


def prep_ffn(wts, dt=BF16):
    w_up_t = wts["w_up_t"].astype(dt)
    w_down = wts["w_down"].astype(dt)
    cw = jnp.pad(wts["conv_w"].astype(F32), ((0, SUBLANE - 3), (0, 0)))
    return dict(w_up=w_up_t.T, w_up_t=w_up_t, w_down=w_down, w_down_t=w_down.T, conv_w=cw,
                conv_b=wts["conv_b"].astype(F32).reshape(1, -1), norm=wts["norm"].astype(F32))


_RET = dict(qcb=EV_RQ // LANE, kcb=EV_RK // LANE, vcb=EV_RV // LANE, la_row=True, chunk=RET_CHUNK, dvg=LANE,
            sub=2, qscale=1.0, kscale=RET_DK ** -0.5)
_GLA = dict(qcb=OD_Q // LANE, kcb=OD_K // LANE, vcb=OD_V // GLA_DV, la_row=False, chunk=GLA_CHUNK, dvg=GLA_DV,
            sub=1, qscale=GLA_DK ** -0.5, kscale=1.0)
_FWD_DIR = dict(reverse=False, inclusive=True)
_BWD_DIR = dict(reverse=True, inclusive=False)


def even_fwd(x, w, tabs_mla, tabs_ret, tag):
    h = rmsnorm_fwd(x, w["mix_norm"], name=f"{tag}_norm")
    p = mm_nn(h, w["w_in"], name=f"{tag}_in")
    q, k, v = mla_prep_fwd(p, tabs_mla, w["q_norm"], w["kv_norm"], w["wuq"], w["wk"], w["wv"], w["qhn"], w["khn"],
                           name=f"{tag}_mla_prep")
    o, lse = attn_fwd(q, k, v, name=f"{tag}_attn")
    of, stf = scan_fwd(p, p, p, w["la_f"], lacb=0, rope=tabs_ret, name=f"{tag}_ret_f", **_RET, **_FWD_DIR)
    ob, stb = scan_fwd(p, p, p, w["la_b"], lacb=0, rope=tabs_ret, name=f"{tag}_ret_b", **_RET, **_BWD_DIR)
    r = gated_norm_fwd(of, ob, p, EV_RG // LANE, w["out_norm"], RET_DV, name=f"{tag}_ret_out")
    x1 = mm_nn(o, w["wa"], res=x, name=f"{tag}_out_a")
    x2 = mm_nn(r, w["wr"], res=x1, name=f"{tag}_out_r")
    return x2, dict(x=x, h=h, p=p, q=q, k=k, v=v, o=o, lse=lse, of=of, ob=ob, stf=stf, stb=stb, r=r)


def even_bwd(dx, s, w, tabs_mla, tabs_ret, tag):
    tag = tag + "_b"
    do = mm_nn(dx, w["wa_t"], name=f"{tag}_dout_a")
    dr = mm_nn(dx, w["wr_t"], name=f"{tag}_dout_r")
    d_wa = mm_tn(s["o"], dx, out_dtype=BF16, name=f"{tag}_dwa")
    d_wr = mm_tn(s["r"], dx, out_dtype=BF16, name=f"{tag}_dwr")
    dq, dk, dv = attn_bwd(s["q"], s["k"], s["v"], s["o"], s["lse"], do, name=f"{tag}_attn")
    (dp_mla, d_wuq, d_wk, d_wv, d_qn, d_kvn, d_qhn, d_khn) = mla_prep_bwd(
        s["p"], tabs_mla, w["q_norm"], w["kv_norm"], w["wuq"], w["wk"], w["wv"], w["wuq_t"], w["wk_t"], w["wv_t"],
        w["qhn"], w["khn"], dq, dk, dv, name=f"{tag}_mla_prep")
    d_o, d_gate, d_gn = gated_norm_bwd(s["of"], s["ob"], s["p"], EV_RG // LANE, w["out_norm"], RET_DV, dr,
                                       name=f"{tag}_ret_out")
    p = s["p"]
    g1 = scan_bwd(p, p, p, w["la_f"], s["stf"], d_o, None, lacb=0, rope=tabs_ret, name=f"{tag}_ret_f",
                  **_RET, **_FWD_DIR)
    g2 = scan_bwd(p, p, p, w["la_b"], s["stb"], d_o, g1[:3], lacb=0, rope=tabs_ret, name=f"{tag}_ret_b",
                  **_RET, **_BWD_DIR)
    dp = jnp.concatenate([dp_mla, g2[0], g2[1], g2[2], d_gate], axis=1)
    dh = mm_nn(dp, w["w_in_t"], name=f"{tag}_dh")
    d_win_t = mm_tn(dp, s["h"], out_dtype=BF16, name=f"{tag}_dwin")
    dx_in, d_mix = rmsnorm_bwd(s["x"], w["mix_norm"], dh, dx, name=f"{tag}_norm")
    grads = dict(w_in_t=d_win_t, wuq=d_wuq, wk=d_wk, wv=d_wv, wa=d_wa, wr=d_wr, mix_norm=d_mix, q_norm=d_qn,
                 kv_norm=d_kvn, qhn=d_qhn, khn=d_khn, la_f=g1[3], la_b=g2[3], out_norm=d_gn)
    return dx_in, grads


def odd_fwd(x, w, tag):
    h = rmsnorm_fwd(x, w["mix_norm"], name=f"{tag}_norm")
    p = mm_nn(h, w["w_in"], name=f"{tag}_in")
    la = gla_gate_fwd(p, w["wg"], w["bg"], name=f"{tag}_gate")
    of, stf = scan_fwd(p, p, p, la, lacb=0, name=f"{tag}_gla_f", **_GLA, **_FWD_DIR)
    ob, stb = scan_fwd(p, p, p, la, lacb=N_GROUPS, name=f"{tag}_gla_b", **_GLA, **_BWD_DIR)
    y = gated_norm_fwd(of, ob, p, OD_R // GLA_DV, w["out_norm"], GLA_DV, name=f"{tag}_gla_out")
    x1 = mm_nn(y, w["w_out"], res=x, name=f"{tag}_out")
    return x1, dict(x=x, h=h, p=p, la=la, of=of, ob=ob, stf=stf, stb=stb, y=y)


def odd_bwd(dx, s, w, tag):
    tag = tag + "_b"
    dy = mm_nn(dx, w["w_out_t"], name=f"{tag}_dout")
    d_wout = mm_tn(s["y"], dx, out_dtype=BF16, name=f"{tag}_dwout")
    d_o, d_gate, d_gn = gated_norm_bwd(s["of"], s["ob"], s["p"], OD_R // GLA_DV, w["out_norm"], GLA_DV, dy,
                                       name=f"{tag}_gla_out")
    p, la = s["p"], s["la"]
    g1 = scan_bwd(p, p, p, la, s["stf"], d_o, None, lacb=0, name=f"{tag}_gla_f", **_GLA, **_FWD_DIR)
    g2 = scan_bwd(p, p, p, la, s["stb"], d_o, g1[:3], lacb=N_GROUPS, name=f"{tag}_gla_b", **_GLA, **_BWD_DIR)
    dla = jnp.concatenate([g1[3], g2[3]], axis=1)
    d_ga, d_wg, d_bg = gla_gate_bwd(p, w["wg"], w["wg_t"], w["bg"], dla, name=f"{tag}_gate")
    dp = jnp.concatenate([g2[0], g2[1], g2[2], d_gate, d_ga], axis=1)
    dh = mm_nn(dp, w["w_in_t"], name=f"{tag}_dh")
    d_win_t = mm_tn(dp, s["h"], out_dtype=BF16, name=f"{tag}_dwin")
    dx_in, d_mix = rmsnorm_bwd(s["x"], w["mix_norm"], dh, dx, name=f"{tag}_norm")
    grads = dict(w_in_t=d_win_t, wg=d_wg, bg=d_bg, w_out=d_wout, mix_norm=d_mix, out_norm=d_gn)
    return dx_in, grads


def ffn_fwd(x, w, tag):
    h = rmsnorm_fwd(x, w["norm"], name=f"{tag}_norm")
    up = mm_nn(h, w["w_up"], name=f"{tag}_up")
    act = ffn_act_fwd(up, w["conv_w"], w["conv_b"], name=f"{tag}_act")
    x1 = mm_nn(act, w["w_down"], res=x, name=f"{tag}_down")
    return x1, dict(x=x, h=h, up=up, act=act)


def ffn_bwd(dx, s, w, tag):
    tag = tag + "_b"
    dact = mm_nn(dx, w["w_down_t"], name=f"{tag}_dact")
    d_wdown = mm_tn(s["act"], dx, out_dtype=BF16, name=f"{tag}_dwdown")
    dc, dval, d_conv = ffn_act_bwd(s["up"], w["conv_w"], w["conv_b"], dact, name=f"{tag}_act")
    dgate = conv_transpose(dc, w["conv_w"], name=f"{tag}_convt")
    dh1 = mm_nn(dgate, w["w_up_t"][:D_FF], name=f"{tag}_dh_g")
    dh = mm_nn(dval, w["w_up_t"][D_FF:], res=dh1, name=f"{tag}_dh_v")
    d_wup_t = jnp.concatenate([mm_tn(dgate, s["h"], out_dtype=BF16, name=f"{tag}_dwup_g"),
                               mm_tn(dval, s["h"], out_dtype=BF16, name=f"{tag}_dwup_v")], axis=0)
    dx_in, d_norm = rmsnorm_bwd(s["x"], w["norm"], dh, dx, name=f"{tag}_norm")
    grads = dict(w_up_t=d_wup_t, w_down=d_wdown, conv_w=d_conv[:3], conv_b=d_conv[3], norm=d_norm)
    return dx_in, grads


def unprep_even_grads(g, theta_fwd, theta_bwd):
    d_win_t = jnp.concatenate([g["w_in_t"][:_EV_REAL], g["w_in_t"][EV_RQ:]], axis=0)
    d_uq = _unpad_heads(g["wuq"], MLA_HEADS, MLA_QK)
    dk_ = _unpad_heads(g["wk"], MLA_HEADS, MLA_NOPE).reshape(MLA_KV_RANK, MLA_HEADS, MLA_NOPE)
    dv_ = _unpad_heads(g["wv"], MLA_HEADS, MLA_V).reshape(MLA_KV_RANK, MLA_HEADS, MLA_V)
    d_ukv = jnp.concatenate([dk_, dv_], axis=-1).reshape(MLA_KV_RANK, -1)
    d_wout = jnp.concatenate([_unpad_rows_heads(g["wa"], MLA_HEADS, MLA_V), g["wr"]], axis=0)

    def dtheta(dla, th):
        dlg = dla.reshape(RET_HEADS, RET_DK).sum(axis=-1)
        e = jnp.exp2(-th.astype(F32))
        return dlg * (e * math.log(2.0)) / (1.0 - e)

    return dict(mix_norm=g["mix_norm"].reshape(-1), w_in_t=d_win_t, q_norm=g["q_norm"].reshape(-1),
                kv_norm=g["kv_norm"].reshape(-1), w_uq=d_uq, w_ukv=d_ukv, q_head_norm=g["qhn"].reshape(-1)[:MLA_QK],
                k_head_norm=g["khn"].reshape(-1)[:MLA_QK], theta_fwd=dtheta(g["la_f"], theta_fwd),
                theta_bwd=dtheta(g["la_b"], theta_bwd), ret_out_norm=g["out_norm"].reshape(RET_HEADS, RET_DV),
                w_out=d_wout)


def unprep_odd_grads(g):
    hk = GLA_HEADS * GLA_DK
    return dict(mix_norm=g["mix_norm"].reshape(-1), w_in_t=g["w_in_t"][:OD_GA + 2 * GLA_GATE_RANK],
                w_gate_fwd=g["wg"][:GLA_GATE_RANK, :hk], b_gate_fwd=g["bg"].reshape(-1)[:hk],
                w_gate_bwd=g["wg"][GLA_GATE_RANK:2 * GLA_GATE_RANK, hk:], b_gate_bwd=g["bg"].reshape(-1)[hk:],
                gla_out_norm=g["out_norm"].reshape(GLA_HEADS, GLA_DV), w_out=g["w_out"])


def unprep_ffn_grads(g):
    return dict(norm=g["norm"].reshape(-1), w_up_t=g["w_up_t"], conv_w=g["conv_w"], conv_b=g["conv_b"],
                w_down=g["w_down"])


def _mesh_pos():
    return tuple(lax.axis_index(n) for n in MESH_AXES)


def _slot(px, py, pc):
    return 4 * px + 2 * py + pc


def all_gather_blocks(blk, name="weight_all_gather"):
    r, w = blk.shape

    def body(x_ref, out_ref, send_sems, recv_sems, local_sem):
        x, y, c = _mesh_pos()
        me, sibling = (x, y, c), (x, y, 1 - c)
        chips = [(1 - x, y), (x, 1 - y), (1 - x, 1 - y)]

        def copy(k, block, to, src=None):
            dst = out_ref.at[_slot(*block)]
            return pltpu.make_async_remote_copy(
                src_ref=dst if src is None else src, dst_ref=dst, send_sem=send_sems.at[k],
                recv_sem=recv_sems.at[k], device_id=to, device_id_type=pl.DeviceIdType.MESH)

        mine = pltpu.make_async_copy(x_ref, out_ref.at[_slot(*me)], local_sem)
        mine.start()
        first = [copy(0, me, sibling, src=x_ref)]
        first += [copy(1 + j, me, (*chip, c), src=x_ref) for j, chip in enumerate(chips)]
        for cp in first:
            cp.start()
        passed = [copy(4 + j, (*chip, c), sibling) for j, chip in enumerate(chips)]
        for j, chip in enumerate(chips):
            copy(1 + j, (*chip, c), me).wait_recv()
            passed[j].start()
        copy(0, sibling, me).wait_recv()
        for j, chip in enumerate(chips):
            copy(4 + j, (*chip, 1 - c), me).wait_recv()
        for cp in first + passed:
            cp.wait_send()
        mine.wait()

    return pl.pallas_call(
        body, name=name,
        out_shape=jax.ShapeDtypeStruct((N_DEV, r, w), blk.dtype),
        in_specs=[pl.BlockSpec(memory_space=pl.ANY)],
        out_specs=pl.BlockSpec(memory_space=pl.ANY),
        scratch_shapes=[pltpu.SemaphoreType.DMA((7,)), pltpu.SemaphoreType.DMA((7,)), pltpu.SemaphoreType.DMA],
    )(blk)


def all_to_all_blocks(send, name="grad_all_to_all"):
    _, r, w = send.shape

    def body(s_ref, r_ref, send_sems, recv_sems, local_sem):
        x, y, c = _mesh_pos()
        me = _slot(x, y, c)
        mine = pltpu.make_async_copy(s_ref.at[me], r_ref.at[me], local_sem)
        mine.start()
        copies = []
        for k in range(1, N_DEV):
            px = 1 - x if (k >> 2) & 1 else x
            py = 1 - y if (k >> 1) & 1 else y
            pc = 1 - c if k & 1 else c
            cp = pltpu.make_async_remote_copy(
                src_ref=s_ref.at[_slot(px, py, pc)], dst_ref=r_ref.at[me], send_sem=send_sems.at[k - 1],
                recv_sem=recv_sems.at[k - 1], device_id=(px, py, pc), device_id_type=pl.DeviceIdType.MESH)
            cp.start()
            copies.append(cp)
        for cp in copies:
            cp.wait()
        mine.wait()

    return pl.pallas_call(
        body, name=name,
        out_shape=jax.ShapeDtypeStruct((N_DEV, r, w), send.dtype),
        in_specs=[pl.BlockSpec(memory_space=pl.ANY)],
        out_specs=pl.BlockSpec(memory_space=pl.ANY),
        scratch_shapes=[pltpu.SemaphoreType.DMA((7,)), pltpu.SemaphoreType.DMA((7,)), pltpu.SemaphoreType.DMA],
    )(send)


FLAT_W = 1024
FLAT_TILE = 256


def sum_slots(recv, name="grad_sum"):
    _, r, w = recv.shape

    def body(r_ref, o_ref):
        acc = r_ref[0].astype(F32)
        for k in range(1, N_DEV):
            acc = acc + r_ref[k].astype(F32)
        o_ref[...] = acc

    tr = _slab_tile(r)
    return pl.pallas_call(
        body, name=name, grid=(r // tr,),
        in_specs=[pl.BlockSpec((N_DEV, tr, w), lambda i: (0, i, 0))],
        out_specs=pl.BlockSpec((tr, w), lambda i: (i, 0)),
        out_shape=jax.ShapeDtypeStruct((r, w), F32),
        compiler_params=_cparams(("parallel",)),
    )(recv)


def _slab_tile(r):
    return max(t for t in range(SUBLANE, FLAT_TILE + 1, SUBLANE) if r % t == 0)


def adamw(wf, gf, mf, vf, name="adamw"):
    r, w = wf.shape
    tr = _slab_tile(r)

    def body(w_ref, g_ref, m_ref, v_ref, d_ref, m_out, v_out):
        g = g_ref[...]
        m = ADAM_B1 * m_ref[...] + (1.0 - ADAM_B1) * g
        v = ADAM_B2 * v_ref[...] + (1.0 - ADAM_B2) * (g * g)
        m_hat = m / (1.0 - ADAM_B1 ** ADAM_STEP)
        v_hat = v / (1.0 - ADAM_B2 ** ADAM_STEP)
        d_ref[...] = -ADAM_LR * (m_hat / (jnp.sqrt(v_hat) + ADAM_EPS) + ADAM_WD * w_ref[...])
        m_out[...] = m
        v_out[...] = v

    tile = pl.BlockSpec((tr, w), lambda i: (i, 0))
    return pl.pallas_call(
        body, name=name, grid=(r // tr,), in_specs=[tile] * 4, out_specs=[tile] * 3,
        out_shape=[jax.ShapeDtypeStruct((r, w), F32)] * 3,
        compiler_params=_cparams(("parallel",)),
    )(wf, gf, mf, vf)


ROW_FORM = [("w_in_even", "T"), ("w_out_even", "R"), ("w_in_odd", "T"), ("w_out_odd", "R"), ("ffn_w_up", "T"),
            ("ffn_w_down", "R")]
SHARDED_MID = [("mla_w_uq", 2), ("mla_w_ukv", 2)]
SHARDED_SMALL = [("mix_norm_odd", 1), ("gla_w_gate_fwd", 2), ("gla_b_gate_fwd", 1), ("gla_w_gate_bwd", 2),
                 ("gla_b_gate_bwd", 1), ("gla_out_norm", 2), ("ffn_conv_w", 2)]
REPLICATED = ["mix_norm_even", "mla_q_norm", "mla_kv_norm", "mla_q_head_norm", "mla_k_head_norm", "ret_theta_fwd",
              "ret_theta_bwd", "ret_out_norm", "ffn_norm", "ffn_conv_b"]
WEIGHT_NAMES = ["mix_norm_even", "w_in_even", "mla_q_norm", "mla_kv_norm", "mla_w_uq", "mla_w_ukv",
                "mla_q_head_norm", "mla_k_head_norm", "ret_theta_fwd", "ret_theta_bwd", "ret_out_norm", "w_out_even",
                "mix_norm_odd", "w_in_odd", "gla_w_gate_fwd", "gla_b_gate_fwd", "gla_w_gate_bwd", "gla_b_gate_bwd",
                "gla_out_norm", "w_out_odd", "ffn_norm", "ffn_w_up", "ffn_conv_w", "ffn_conv_b", "ffn_w_down"]


def _round_up(n, m):
    return -(-n // m) * m


def _pack_rows(parts, rows):
    flat = jnp.concatenate(parts, axis=-1)
    pad = rows * FLAT_W - flat.shape[-1]
    flat = jnp.pad(flat, [(0, 0)] * (flat.ndim - 1) + [(0, pad)])
    return flat.reshape(*flat.shape[:-1], rows, FLAT_W)


def _row_form(v, form):
    if form == "T":
        v = jnp.swapaxes(v, 1, 2)
    return v.reshape(-1, v.shape[-1])


def _row_counts(local_shapes):
    return [local_shapes[n][0] * local_shapes[n][2 if f == "T" else 1] for n, f in ROW_FORM]


def _tail_layout(local_shapes):
    n_sh = sum(math.prod(local_shapes[n]) for n, _ in SHARDED_MID + SHARDED_SMALL)
    n_rep = sum(math.prod(local_shapes[n]) for n in REPLICATED)
    sh_rows = _round_up(-(-n_sh // FLAT_W), SUBLANE)
    rep_rows = _round_up(-(-n_rep // FLAT_W), SUBLANE)
    return sh_rows, rep_rows, _round_up(sh_rows + rep_rows, FLAT_TILE)


def pack_tail(vals, local_shapes):
    sh_rows, rep_rows, rows = _tail_layout(local_shapes)
    sh = _pack_rows([vals[n].astype(F32).reshape(-1) for n, _ in SHARDED_MID + SHARDED_SMALL], sh_rows)
    rep = _pack_rows([vals[n].astype(F32).reshape(-1) for n in REPLICATED], rep_rows)
    return jnp.concatenate([sh, rep, jnp.zeros((rows - sh_rows - rep_rows, FLAT_W), F32)], axis=0)


def unpack_tail(tail, local_shapes):
    sh_rows, rep_rows, _ = _tail_layout(local_shapes)
    out = {}
    for names, flat in (([n for n, _ in SHARDED_MID + SHARDED_SMALL], tail[:sh_rows].reshape(-1)),
                        (REPLICATED, tail[sh_rows:sh_rows + rep_rows].reshape(-1))):
        off = 0
        for n in names:
            k = math.prod(local_shapes[n])
            out[n] = flat[off:off + k].reshape(local_shapes[n])
            off += k
    return out


def unpack_rows(slab, local_shapes):
    out = {}
    off = 0
    for (n, form), rows in zip(ROW_FORM, _row_counts(local_shapes)):
        loc = local_shapes[n]
        piece = slab[off:off + rows]
        if form == "T":
            piece = jnp.swapaxes(piece.reshape(loc[0], loc[2], loc[1]), 1, 2)
        out[n] = piece.reshape(loc)
        off += rows
    return out


def pack_grad_blocks(full_grads, local_shapes):
    sh_rows, rep_rows, rows = _tail_layout(local_shapes)
    blocks = []
    for n, form in ROW_FORM:
        g = full_grads[n].astype(BF16)
        layers, total = g.shape[0], g.shape[1]
        g = g.reshape(layers, N_DEV, total // N_DEV, FLAT_W)
        blocks.append(jnp.swapaxes(g, 0, 1).reshape(N_DEV, -1, FLAT_W))
    parts = []
    for n, ax in SHARDED_MID + SHARDED_SMALL:
        g = full_grads[n].astype(F32)
        loc = local_shapes[n]
        g = g.reshape(*g.shape[:ax], N_DEV, loc[ax], *g.shape[ax + 1:])
        parts.append(jnp.moveaxis(g, ax, 0).reshape(N_DEV, -1))
    sh = _pack_rows(parts, sh_rows)
    rep = _pack_rows([full_grads[n].astype(F32).reshape(-1) for n in REPLICATED], rep_rows)
    rep = jnp.broadcast_to(rep[None], (N_DEV, rep_rows, FLAT_W))
    pad = jnp.zeros((N_DEV, rows - sh_rows - rep_rows, FLAT_W), F32)
    tail = jnp.concatenate([sh, rep, pad], axis=1).astype(BF16)
    return jnp.concatenate(blocks + [tail], axis=1)


def pack_gather_block(vals, local_shapes):
    big = jnp.concatenate([_row_form(vals[n].astype(BF16), f) for n, f in ROW_FORM], axis=0)
    mid = [vals[n].astype(BF16).reshape(-1) for n, _ in SHARDED_MID]
    small = jnp.concatenate([vals[n].astype(F32).reshape(-1) for n, _ in SHARDED_SMALL])
    small = lax.bitcast_convert_type(small, BF16).reshape(-1)
    n = sum(v.shape[0] for v in mid) + small.shape[0]
    tail = _pack_rows(mid + [small], _round_up(-(-n // FLAT_W), 2 * SUBLANE))
    return jnp.concatenate([big, tail], axis=0)


def unpack_gathered(gathered, local_shapes):
    out = {}
    off = 0
    for (n, form), rows in zip(ROW_FORM, _row_counts(local_shapes)):
        layers = local_shapes[n][0]
        piece = gathered[:, off:off + rows].reshape(N_DEV, layers, rows // layers, FLAT_W)
        out[n] = jnp.swapaxes(piece, 0, 1).reshape(layers, N_DEV * (rows // layers), FLAT_W)
        off += rows
    flat = gathered[:, off:].reshape(N_DEV, -1)
    off = 0

    def full(piece, n, ax):
        loc = local_shapes[n]
        piece = jnp.moveaxis(piece.reshape(N_DEV, *loc), 0, ax)
        return piece.reshape(*loc[:ax], N_DEV * loc[ax], *loc[ax + 1:])

    for n, ax in SHARDED_MID:
        k = math.prod(local_shapes[n])
        out[n] = full(flat[:, off:off + k], n, ax)
        off += k
    for n, ax in SHARDED_SMALL:
        k = math.prod(local_shapes[n])
        pairs = flat[:, off:off + 2 * k].reshape(N_DEV, k, 2)
        out[n] = full(lax.bitcast_convert_type(pairs, F32), n, ax)
        off += 2 * k
    return out


def local_step(x, positions, target, fw):
    tabs_mla = rope_tables(positions, MLA_NOPE, MLA_ROPE // 2, LANE)
    tabs_ret = rope_tables(positions, 0, RET_DK // 2, RET_DK)
    layers = []
    for layer in range(DEPTH):
        i = layer // 2
        if layer % 2 == 0:
            mw = prep_even(dict(
                w_in_t=fw["w_in_even"][i], w_uq=fw["mla_w_uq"][i], w_ukv=fw["mla_w_ukv"][i], w_out=fw["w_out_even"][i],
                mix_norm=fw["mix_norm_even"][i], q_norm=fw["mla_q_norm"][i], kv_norm=fw["mla_kv_norm"][i],
                q_head_norm=fw["mla_q_head_norm"][i], k_head_norm=fw["mla_k_head_norm"][i],
                theta_fwd=fw["ret_theta_fwd"][i], theta_bwd=fw["ret_theta_bwd"][i],
                ret_out_norm=fw["ret_out_norm"][i]))
        else:
            mw = prep_odd(dict(
                w_in_t=fw["w_in_odd"][i], w_gate_fwd=fw["gla_w_gate_fwd"][i], b_gate_fwd=fw["gla_b_gate_fwd"][i],
                w_gate_bwd=fw["gla_w_gate_bwd"][i], b_gate_bwd=fw["gla_b_gate_bwd"][i],
                gla_out_norm=fw["gla_out_norm"][i], w_out=fw["w_out_odd"][i], mix_norm=fw["mix_norm_odd"][i]))
        fwt = prep_ffn(dict(norm=fw["ffn_norm"][layer], w_up_t=fw["ffn_w_up"][layer], conv_w=fw["ffn_conv_w"][layer],
                            conv_b=fw["ffn_conv_b"][layer], w_down=fw["ffn_w_down"][layer]))
        layers.append((mw, fwt))

    saved = []
    for layer, (mw, fwt) in enumerate(layers):
        if layer % 2 == 0:
            x, sm = even_fwd(x, mw, tabs_mla, tabs_ret, f"l{layer}_mix")
        else:
            x, sm = odd_fwd(x, mw, f"l{layer}_mix")
        x, sf = ffn_fwd(x, fwt, f"l{layer}_ffn")
        saved.append((sm, sf))

    dx, loss = loss_head(x, target, name="loss_head")

    per_layer = [None] * DEPTH
    for layer in reversed(range(DEPTH)):
        mw, fwt = layers[layer]
        sm, sf = saved[layer]
        dx, gf = ffn_bwd(dx, sf, fwt, f"l{layer}_ffn")
        if layer % 2 == 0:
            dx, gm = even_bwd(dx, sm, mw, tabs_mla, tabs_ret, f"l{layer}_mix")
            gm = unprep_even_grads(gm, fw["ret_theta_fwd"][layer // 2], fw["ret_theta_bwd"][layer // 2])
        else:
            dx, gm = odd_bwd(dx, sm, mw, f"l{layer}_mix")
            gm = unprep_odd_grads(gm)
        per_layer[layer] = (gm, unprep_ffn_grads(gf))

    ev = [per_layer[l][0] for l in range(0, DEPTH, 2)]
    od = [per_layer[l][0] for l in range(1, DEPTH, 2)]
    ff = [per_layer[l][1] for l in range(DEPTH)]
    st = lambda lst, key: jnp.stack([g[key] for g in lst])
    grads = {
        "mix_norm_even": st(ev, "mix_norm"), "w_in_even": st(ev, "w_in_t"), "mla_q_norm": st(ev, "q_norm"),
        "mla_kv_norm": st(ev, "kv_norm"), "mla_w_uq": st(ev, "w_uq"), "mla_w_ukv": st(ev, "w_ukv"),
        "mla_q_head_norm": st(ev, "q_head_norm"), "mla_k_head_norm": st(ev, "k_head_norm"),
        "ret_theta_fwd": st(ev, "theta_fwd"), "ret_theta_bwd": st(ev, "theta_bwd"),
        "ret_out_norm": st(ev, "ret_out_norm"), "w_out_even": st(ev, "w_out"),
        "mix_norm_odd": st(od, "mix_norm"), "w_in_odd": st(od, "w_in_t"), "gla_w_gate_fwd": st(od, "w_gate_fwd"),
        "gla_b_gate_fwd": st(od, "b_gate_fwd"), "gla_w_gate_bwd": st(od, "w_gate_bwd"),
        "gla_b_gate_bwd": st(od, "b_gate_bwd"), "gla_out_norm": st(od, "gla_out_norm"), "w_out_odd": st(od, "w_out"),
        "ffn_norm": st(ff, "norm"), "ffn_w_up": st(ff, "w_up_t"), "ffn_conv_w": st(ff, "conv_w"),
        "ffn_conv_b": st(ff, "conv_b"), "ffn_w_down": st(ff, "w_down"),
    }
    return loss, dx, grads


def kernel(x, positions, mix_norm_even, w_in_even, mla_q_norm, mla_kv_norm, mla_w_uq, mla_w_ukv, mla_q_head_norm, mla_k_head_norm, ret_theta_fwd, ret_theta_bwd, ret_out_norm, w_out_even, mix_norm_odd, w_in_odd, gla_w_gate_fwd, gla_b_gate_fwd, gla_w_gate_bwd, gla_b_gate_bwd, gla_out_norm, w_out_odd, ffn_norm, ffn_w_up, ffn_conv_w, ffn_conv_b, ffn_w_down, loss_target, m_mix_norm_even, m_w_in_even, m_mla_q_norm, m_mla_kv_norm, m_mla_w_uq, m_mla_w_ukv, m_mla_q_head_norm, m_mla_k_head_norm, m_ret_theta_fwd, m_ret_theta_bwd, m_ret_out_norm, m_w_out_even, m_mix_norm_odd, m_w_in_odd, m_gla_w_gate_fwd, m_gla_b_gate_fwd, m_gla_w_gate_bwd, m_gla_b_gate_bwd, m_gla_out_norm, m_w_out_odd, m_ffn_norm, m_ffn_w_up, m_ffn_conv_w, m_ffn_conv_b, m_ffn_w_down, v_mix_norm_even, v_w_in_even, v_mla_q_norm, v_mla_kv_norm, v_mla_w_uq, v_mla_w_ukv, v_mla_q_head_norm, v_mla_k_head_norm, v_ret_theta_fwd, v_ret_theta_bwd, v_ret_out_norm, v_w_out_even, v_mix_norm_odd, v_w_in_odd, v_gla_w_gate_fwd, v_gla_b_gate_fwd, v_gla_w_gate_bwd, v_gla_b_gate_bwd, v_gla_out_norm, v_w_out_odd, v_ffn_norm, v_ffn_w_up, v_ffn_conv_w, v_ffn_conv_b, v_ffn_w_down):
    a = dict(locals())
    wts = {n: a[n] for n in WEIGHT_NAMES}
    local_shapes = {n: tuple(wts[n].shape) for n in WEIGHT_NAMES}

    gathered = all_gather_blocks(pack_gather_block(wts, local_shapes))
    fw = unpack_gathered(gathered, local_shapes)
    for n in REPLICATED:
        fw[n] = wts[n]

    loss, grad_x, grads = local_step(x[0], positions, loss_target[0], fw)

    recv = all_to_all_blocks(pack_grad_blocks(grads, local_shapes))
    g_slab = sum_slots(recv)
    n_big = sum(_row_counts(local_shapes))
    ms = {n: a["m_" + n] for n in WEIGHT_NAMES}
    vs = {n: a["v_" + n] for n in WEIGHT_NAMES}

    g_out = unpack_rows(g_slab[:n_big], local_shapes)
    d_out, m_out, v_out = {}, {}, {}
    for n, _ in ROW_FORM:
        loc = local_shapes[n]
        two_d = lambda t: t.reshape(-1, loc[-1])
        d, m, v = adamw(two_d(wts[n]), two_d(g_out[n]), two_d(ms[n]), two_d(vs[n]), name=f"adamw_{n}")
        d_out[n], m_out[n], v_out[n] = d.reshape(loc), m.reshape(loc), v.reshape(loc)
    g_tail = g_slab[n_big:]
    d_tail, m_tail, v_tail = adamw(pack_tail(wts, local_shapes), g_tail, pack_tail(ms, local_shapes),
                                   pack_tail(vs, local_shapes), name="adamw_small")
    g_out.update(unpack_tail(g_tail, local_shapes))
    d_out.update(unpack_tail(d_tail, local_shapes))
    m_out.update(unpack_tail(m_tail, local_shapes))
    v_out.update(unpack_tail(v_tail, local_shapes))
    total = lax.psum(loss[0, 0], MESH_AXES)
    return (total, grad_x[None], *[g_out[n] for n in WEIGHT_NAMES], *[d_out[n] for n in WEIGHT_NAMES],
            *[m_out[n] for n in WEIGHT_NAMES], *[v_out[n] for n in WEIGHT_NAMES])
```

```python
import math

import jax
import jax.numpy as jnp
from jax import lax
from jax.experimental import pallas as pl
from jax.experimental.pallas import tpu as pltpu

F32 = jnp.float32
BF16 = jnp.bfloat16

D_MODEL = 1024
DEPTH = 4
N_DEV = 8
MESH_AXES = ("x", "y", "c")

MLA_HEADS = 8
MLA_Q_RANK = 384
MLA_KV_RANK = 256
MLA_NOPE = 64
MLA_ROPE = 32
MLA_V = 64
MLA_QK = MLA_NOPE + MLA_ROPE
RET_HEADS = 8
RET_DK = 64
RET_DV = 64
RET_CHUNK = 128
GLA_HEADS = 4
GLA_DK = 128
GLA_DV = 256
GLA_GATE_RANK = 16
GLA_TAU = 16.0
GLA_CHUNK = 64
D_FF = 2816
ROPE_THETA = 10000.0
EPS = 1e-6

ADAM_LR = 0.001
ADAM_B1 = 0.9
ADAM_B2 = 0.999
ADAM_EPS = 1e-08
ADAM_WD = 0.01
ADAM_STEP = 10

LANE = 128
SUBLANE = 8
ROW_TILE = 512
VMEM_LIMIT = 56 * 1024 * 1024
WEIGHT_TILE_BYTES = 8 * 1024 * 1024

EV_CQ, EV_CKV, EV_KR, EV_RQ, EV_RK, EV_RV, EV_RG, EV_IN = 0, 384, 640, 768, 1280, 1792, 2304, 2816
OD_Q, OD_K, OD_V, OD_R, OD_GA, OD_IN = 0, 512, 1024, 2048, 3072, 3200
N_GROUPS = 4


def _cparams(sem):
    return pltpu.CompilerParams(dimension_semantics=sem, vmem_limit_bytes=VMEM_LIMIT)


def _dot(a, b):
    return jnp.dot(a.astype(BF16), b.astype(BF16), preferred_element_type=F32)


def _dot_nt(a, b):
    return lax.dot_general(a.astype(BF16), b.astype(BF16), (((1,), (1,)), ((), ())), preferred_element_type=F32)


def _dot_tn(a, b):
    return lax.dot_general(a.astype(BF16), b.astype(BF16), (((0,), (0,)), ((), ())), preferred_element_type=F32)


def _sigmoid(x):
    return 1.0 / (1.0 + jnp.exp(-x))


def _col_tile(k, n, itemsize=2):
    best = LANE
    for t in range(LANE, n + 1, LANE):
        if n % t == 0 and k * t * itemsize <= WEIGHT_TILE_BYTES:
            best = t
    return best if n % LANE == 0 else n


def _row_tile(m):
    return min(ROW_TILE, m)


def mm_nn(a, b, res=None, out_dtype=F32, name="mm_nn"):
    m, k = a.shape
    n = b.shape[1]
    tm, tn = _row_tile(m), _col_tile(k, n)

    def body(*refs):
        if res is None:
            a_ref, b_ref, o_ref = refs
        else:
            a_ref, b_ref, r_ref, o_ref = refs
        acc = _dot(a_ref[...], b_ref[...])
        if res is not None:
            acc = acc + r_ref[...].astype(F32)
        o_ref[...] = acc.astype(out_dtype)

    in_specs = [pl.BlockSpec((tm, k), lambda j, i: (i, 0)), pl.BlockSpec((k, tn), lambda j, i: (0, j))]
    args = [a, b]
    if res is not None:
        in_specs.append(pl.BlockSpec((tm, tn), lambda j, i: (i, j)))
        args.append(res)
    return pl.pallas_call(
        body, name=name, grid=(n // tn, m // tm), in_specs=in_specs,
        out_specs=pl.BlockSpec((tm, tn), lambda j, i: (i, j)),
        out_shape=jax.ShapeDtypeStruct((m, n), out_dtype),
        compiler_params=_cparams(("parallel", "parallel")),
    )(*args)


def mm_tn(a, b, out_dtype=F32, name="mm_tn"):
    t, k = a.shape
    n = b.shape[1]
    tt = min(2 * ROW_TILE, t)
    tk = k if k <= 1024 else _col_tile(1024, k, 4)
    tn = n if n <= 1024 else _col_tile(1024, n, 4)

    def body(a_ref, b_ref, o_ref, acc_s):
        s = pl.program_id(2)

        @pl.when(s == 0)
        def _():
            acc_s[...] = jnp.zeros_like(acc_s)
        acc_s[...] += _dot_tn(a_ref[...], b_ref[...])

        @pl.when(s == pl.num_programs(2) - 1)
        def _():
            o_ref[...] = acc_s[...].astype(out_dtype)

    return pl.pallas_call(
        body, name=name, grid=(k // tk, n // tn, t // tt),
        in_specs=[pl.BlockSpec((tt, tk), lambda i, j, s: (s, i)), pl.BlockSpec((tt, tn), lambda i, j, s: (s, j))],
        out_specs=pl.BlockSpec((tk, tn), lambda i, j, s: (i, j)),
        out_shape=jax.ShapeDtypeStruct((k, n), out_dtype),
        scratch_shapes=[pltpu.VMEM((tk, tn), F32)],
        compiler_params=_cparams(("parallel", "parallel", "arbitrary")),
    )(a, b)


def rmsnorm_fwd(x, g, name="rmsnorm_fwd"):
    t, d = x.shape
    tm = _row_tile(t)

    def body(x_ref, g_ref, h_ref):
        xv = x_ref[...]
        r = lax.rsqrt(jnp.mean(xv * xv, axis=-1, keepdims=True) + EPS)
        h_ref[...] = (xv * r * g_ref[...]).astype(BF16)

    return pl.pallas_call(
        body, name=name, grid=(t // tm,),
        in_specs=[pl.BlockSpec((tm, d), lambda i: (i, 0)), pl.BlockSpec((1, d), lambda i: (0, 0))],
        out_specs=pl.BlockSpec((tm, d), lambda i: (i, 0)),
        out_shape=jax.ShapeDtypeStruct((t, d), BF16),
        compiler_params=_cparams(("parallel",)),
    )(x, g.reshape(1, d))


def rmsnorm_bwd(x, g, dh, dres, name="rmsnorm_bwd"):
    t, d = x.shape
    tm = _row_tile(t)

    def body(x_ref, g_ref, dh_ref, dres_ref, dx_ref, dg_ref):
        @pl.when(pl.program_id(0) == 0)
        def _():
            dg_ref[...] = jnp.zeros_like(dg_ref)
        xv = x_ref[...]
        r = lax.rsqrt(jnp.mean(xv * xv, axis=-1, keepdims=True) + EPS)
        xh = xv * r
        dhv = dh_ref[...]
        dg_ref[...] += jnp.sum(dhv * xh, axis=0, keepdims=True)
        dxh = dhv * g_ref[...]
        dx_ref[...] = dres_ref[...] + r * (dxh - xh * jnp.mean(dxh * xh, axis=-1, keepdims=True))

    row = pl.BlockSpec((tm, d), lambda i: (i, 0))
    vec = pl.BlockSpec((1, d), lambda i: (0, 0))
    return pl.pallas_call(
        body, name=name, grid=(t // tm,),
        in_specs=[row, vec, row, row], out_specs=[row, vec],
        out_shape=[jax.ShapeDtypeStruct((t, d), F32), jax.ShapeDtypeStruct((1, d), F32)],
        compiler_params=_cparams(("arbitrary",)),
    )(x, g.reshape(1, d), dh, dres)


def _rope_apply(x, cos, s1, s2, half):
    return x * cos + pltpu.roll(x, LANE - half, 1) * s1 + pltpu.roll(x, half, 1) * s2


def _rope_transpose(dy, cos, s1, s2, half):
    return dy * cos + pltpu.roll(dy * s1, half, 1) + pltpu.roll(dy * s2, LANE - half, 1)


def rope_tables(positions, lane_start, half, period):
    pos = positions.reshape(-1).astype(F32)
    inv = ROPE_THETA ** (-jnp.arange(half, dtype=F32) / half)
    ang = pos[:, None] * inv[None, :]
    cos, sin = jnp.cos(ang), jnp.sin(ang)
    t = pos.shape[0]
    pre = lane_start
    post = period - lane_start - 2 * half
    ones = lambda n: jnp.ones((t, n), F32)
    zeros = lambda n: jnp.zeros((t, n), F32)
    c = jnp.concatenate([ones(pre), cos, cos, ones(post)], axis=1)
    a = jnp.concatenate([zeros(pre), -sin, zeros(half), zeros(post)], axis=1)
    b = jnp.concatenate([zeros(pre), zeros(half), sin, zeros(post)], axis=1)
    rep = LANE // period
    return tuple(jnp.tile(v, (1, rep)) for v in (c, a, b))


def _mla_forward_tile(p, cos, s1, s2, qn_g, kvn_g, wuq, wk, wv, qhn, khn):
    cq = p[:, EV_CQ:EV_CKV]
    ckv = p[:, EV_CKV:EV_KR]
    kr = p[:, EV_KR:EV_RQ]
    rq = lax.rsqrt(jnp.mean(cq * cq, axis=-1, keepdims=True) + EPS)
    rkv = lax.rsqrt(jnp.mean(ckv * ckv, axis=-1, keepdims=True) + EPS)
    qn = cq * rq * qn_g
    kvn = ckv * rkv * kvn_g
    q_raw = _dot(qn, wuq)
    k_raw = _dot(kvn, wk)
    v = _dot(kvn, wv)
    krp = pltpu.roll(kr, MLA_NOPE, 1)
    return cq, ckv, rq, rkv, qn, kvn, q_raw, k_raw, v, krp


def _head_norm(xh, g):
    r = lax.rsqrt(jnp.sum(xh * xh, axis=-1, keepdims=True) * (1.0 / MLA_QK) + EPS)
    return xh * r * g, r


def mla_prep_fwd(p, tabs, qn_g, kvn_g, wuq, wk, wv, qhn, khn, name="mla_prep_fwd"):
    t = p.shape[0]
    tm = _row_tile(t)
    hw = MLA_HEADS * LANE

    def body(p_ref, c_ref, s1_ref, s2_ref, qn_ref, kvn_ref, wuq_ref, wk_ref, wv_ref, qhn_ref, khn_ref,
             q_out, k_out, v_out):
        cos, s1, s2 = c_ref[...], s1_ref[...], s2_ref[...]
        (_, _, _, _, _, _, q_raw, k_raw, v, krp) = _mla_forward_tile(
            p_ref[...], cos, s1, s2, qn_ref[...], kvn_ref[...], wuq_ref[...], wk_ref[...], wv_ref[...],
            qhn_ref[...], khn_ref[...])
        v_out[...] = v.astype(BF16)
        for h in range(MLA_HEADS):
            sl = slice(h * LANE, (h + 1) * LANE)
            qh, _ = _head_norm(q_raw[:, sl], qhn_ref[...])
            kh, _ = _head_norm(k_raw[:, sl] + krp, khn_ref[...])
            q_out[:, sl] = (_rope_apply(qh, cos, s1, s2, MLA_ROPE // 2) * ATTN_QSCALE).astype(BF16)
            k_out[:, sl] = _rope_apply(kh, cos, s1, s2, MLA_ROPE // 2).astype(BF16)

    row = lambda w: pl.BlockSpec((tm, w), lambda i: (i, 0))
    full = lambda a: pl.BlockSpec(a.shape, lambda i: (0,) * a.ndim)
    ws = [qn_g, kvn_g, wuq, wk, wv, qhn, khn]
    return pl.pallas_call(
        body, name=name, grid=(t // tm,),
        in_specs=[row(EV_RQ), row(LANE), row(LANE), row(LANE)] + [full(w) for w in ws],
        out_specs=[row(hw)] * 3,
        out_shape=[jax.ShapeDtypeStruct((t, hw), BF16)] * 3,
        compiler_params=_cparams(("parallel",)),
    )(p, *tabs, *ws)


def mla_prep_bwd(p, tabs, qn_g, kvn_g, wuq, wk, wv, wuq_t, wk_t, wv_t, qhn, khn, dq, dk, dv,
                 name="mla_prep_bwd"):
    t = p.shape[0]
    tm = _row_tile(t)
    hw = MLA_HEADS * LANE

    def body(p_ref, c_ref, s1_ref, s2_ref, qn_ref, kvn_ref, wuq_ref, wk_ref, wv_ref, wuqt_ref, wkt_ref, wvt_ref,
             qhn_ref, khn_ref, dq_ref, dk_ref, dv_ref,
             dp_ref, dwuq_ref, dwk_ref, dwv_ref, dqn_ref, dkvn_ref, dqhn_ref, dkhn_ref, dqraw_s, dkraw_s):
        @pl.when(pl.program_id(0) == 0)
        def _():
            for r in (dwuq_ref, dwk_ref, dwv_ref, dqn_ref, dkvn_ref, dqhn_ref, dkhn_ref):
                r[...] = jnp.zeros_like(r)
        cos, s1, s2 = c_ref[...], s1_ref[...], s2_ref[...]
        qhn_v, khn_v = qhn_ref[...], khn_ref[...]
        (cq, ckv, rq, rkv, qn, kvn, q_raw, k_raw, _, krp) = _mla_forward_tile(
            p_ref[...], cos, s1, s2, qn_ref[...], kvn_ref[...], wuq_ref[...], wk_ref[...], wv_ref[...],
            qhn_v, khn_v)
        half = MLA_ROPE // 2
        dkr_sum = jnp.zeros((tm, LANE), F32)
        dqhn_acc = jnp.zeros((1, LANE), F32)
        dkhn_acc = jnp.zeros((1, LANE), F32)
        for h in range(MLA_HEADS):
            sl = slice(h * LANE, (h + 1) * LANE)
            xq = q_raw[:, sl]
            _, r = _head_norm(xq, qhn_v)
            xh = xq * r
            dy = _rope_transpose(dq_ref[:, sl] * ATTN_SCALE, cos, s1, s2, half)
            dqhn_acc = dqhn_acc + jnp.sum(dy * xh, axis=0, keepdims=True)
            dxh = dy * qhn_v
            dqraw_s[:, sl] = r * (dxh - xh * (jnp.sum(dxh * xh, axis=-1, keepdims=True) * (1.0 / MLA_QK)))
            xk = k_raw[:, sl] + krp
            _, r = _head_norm(xk, khn_v)
            xh = xk * r
            dy = _rope_transpose(dk_ref[:, sl] * math.log(2.0), cos, s1, s2, half)
            dkhn_acc = dkhn_acc + jnp.sum(dy * xh, axis=0, keepdims=True)
            dxh = dy * khn_v
            dxk = r * (dxh - xh * (jnp.sum(dxh * xh, axis=-1, keepdims=True) * (1.0 / MLA_QK)))
            dkraw_s[:, sl] = dxk
            dkr_sum = dkr_sum + dxk
        dqhn_ref[...] += dqhn_acc
        dkhn_ref[...] += dkhn_acc
        dq_raw = dqraw_s[...]
        dk_raw = dkraw_s[...]
        dvv = dv_ref[...]
        dwuq_ref[...] += _dot_tn(qn, dq_raw)
        dwk_ref[...] += _dot_tn(kvn, dk_raw)
        dwv_ref[...] += _dot_tn(kvn, dvv)
        dqn = _dot(dq_raw, wuqt_ref[...])
        dkvn = _dot(dk_raw, wkt_ref[...]) + _dot(dvv, wvt_ref[...])
        xh = cq * rq
        dqn_ref[...] += jnp.sum(dqn * xh, axis=0, keepdims=True)
        dxh = dqn * qn_ref[...]
        dcq = rq * (dxh - xh * jnp.mean(dxh * xh, axis=-1, keepdims=True))
        dp_ref[:, EV_CQ:EV_CKV] = dcq.astype(BF16)
        xh = ckv * rkv
        dkvn_ref[...] += jnp.sum(dkvn * xh, axis=0, keepdims=True)
        dxh = dkvn * kvn_ref[...]
        dckv = rkv * (dxh - xh * jnp.mean(dxh * xh, axis=-1, keepdims=True))
        dp_ref[:, EV_CKV:EV_KR] = dckv.astype(BF16)
        lane = lax.broadcasted_iota(jnp.int32, (tm, LANE), 1)
        dkr = jnp.where(lane < MLA_ROPE, pltpu.roll(dkr_sum, LANE - MLA_NOPE, 1), 0.0)
        dp_ref[:, EV_KR:EV_RQ] = dkr.astype(BF16)

    row = lambda w: pl.BlockSpec((tm, w), lambda i: (i, 0))
    full = lambda a: pl.BlockSpec(a.shape, lambda i: (0,) * a.ndim)
    ws = [qn_g, kvn_g, wuq, wk, wv, wuq_t, wk_t, wv_t, qhn, khn]
    outs = [jax.ShapeDtypeStruct((t, EV_RQ), BF16), jax.ShapeDtypeStruct(wuq.shape, F32),
            jax.ShapeDtypeStruct(wk.shape, F32), jax.ShapeDtypeStruct(wv.shape, F32),
            jax.ShapeDtypeStruct(qn_g.shape, F32), jax.ShapeDtypeStruct(kvn_g.shape, F32),
            jax.ShapeDtypeStruct(qhn.shape, F32), jax.ShapeDtypeStruct(khn.shape, F32)]
    return pl.pallas_call(
        body, name=name, grid=(t // tm,),
        in_specs=[row(EV_RQ), row(LANE), row(LANE), row(LANE)] + [full(w) for w in ws] + [row(hw)] * 3,
        out_specs=[row(EV_RQ)] + [full(o) for o in outs[1:]],
        out_shape=outs,
        scratch_shapes=[pltpu.VMEM((tm, hw), F32), pltpu.VMEM((tm, hw), F32)],
        compiler_params=_cparams(("arbitrary",)),
    )(p, *tabs, *ws, dq, dk, dv)


ATTN_SCALE = MLA_QK ** -0.5
ATTN_QSCALE = ATTN_SCALE * math.log2(math.e)
ATTN_FWD_TQ, ATTN_FWD_TK = 512, 8192
ATTN_BWD_TQ, ATTN_BWD_TK = 256, 4096


def attn_fwd(q, k, v, name="attn_fwd"):
    t = q.shape[0]
    tq, tk = min(ATTN_FWD_TQ, _row_tile(t)), min(ATTN_FWD_TK, t)
    nh = MLA_HEADS

    def body(q_ref, k_ref, v_ref, o_ref, lse_ref, m_s, l_s, acc_s):
        j = pl.program_id(2)

        @pl.when(j == 0)
        def _():
            m_s[...] = jnp.full_like(m_s, -jnp.inf)
            l_s[...] = jnp.zeros_like(l_s)
            acc_s[...] = jnp.zeros_like(acc_s)

        s = _dot_nt(q_ref[...], k_ref[...])
        m_old = m_s[...]
        m_new = jnp.maximum(m_old, jnp.max(s, axis=-1, keepdims=True))
        pr = jnp.exp2(s - m_new)
        alpha = jnp.exp2(m_old - m_new)
        l_s[...] = alpha * l_s[...] + jnp.sum(pr, axis=-1, keepdims=True)
        acc_s[...] = alpha * acc_s[...] + _dot(pr, v_ref[...])
        m_s[...] = m_new

        @pl.when(j == pl.num_programs(2) - 1)
        def _():
            o_ref[...] = acc_s[...] / l_s[...]
            lse_ref[...] = m_s[...] + jnp.log2(l_s[...])

    return pl.pallas_call(
        body, name=name, grid=(nh, t // tq, t // tk),
        in_specs=[pl.BlockSpec((tq, LANE), lambda h, i, j: (i, h)),
                  pl.BlockSpec((tk, LANE), lambda h, i, j: (j, h)),
                  pl.BlockSpec((tk, LANE), lambda h, i, j: (j, h))],
        out_specs=[pl.BlockSpec((tq, LANE), lambda h, i, j: (i, h)),
                   pl.BlockSpec((None, tq, 1), lambda h, i, j: (h, i, 0))],
        out_shape=[jax.ShapeDtypeStruct((t, nh * LANE), F32), jax.ShapeDtypeStruct((nh, t, 1), F32)],
        scratch_shapes=[pltpu.VMEM((tq, 1), F32), pltpu.VMEM((tq, 1), F32), pltpu.VMEM((tq, LANE), F32)],
        compiler_params=_cparams(("parallel", "parallel", "arbitrary")),
    )(q, k, v)


def attn_bwd(q, k, v, o, lse, do, name="attn_bwd"):
    t = q.shape[0]
    tq, tk = min(ATTN_BWD_TQ, _row_tile(t)), min(ATTN_BWD_TK, t)
    nh = MLA_HEADS
    nq = t // tq

    def body(q_ref, k_ref, v_ref, o_ref, lse_ref, do_ref, dq_ref, dk_ref, dv_ref):
        kj, qi = pl.program_id(1), pl.program_id(2)

        @pl.when(qi == 0)
        def _():
            dk_ref[...] = jnp.zeros_like(dk_ref)
            dv_ref[...] = jnp.zeros_like(dv_ref)

        qv, kv, vv, dov = q_ref[...], k_ref[...], v_ref[...], do_ref[...]
        s = _dot_nt(qv, kv)
        pr = jnp.exp2(s - lse_ref[...])
        dp = _dot_nt(dov, vv)
        delta = jnp.sum(dov * o_ref[...], axis=-1, keepdims=True)
        ds = pr * (dp - delta)
        dv_ref[...] += _dot_tn(pr, dov)
        dk_ref[...] += _dot_tn(ds, qv)
        dq_tile = _dot(ds, kv)
        rows = pl.ds(pl.multiple_of(qi * tq, tq), tq)

        @pl.when(kj == 0)
        def _():
            dq_ref[rows, :] = dq_tile

        @pl.when(kj != 0)
        def _():
            dq_ref[rows, :] += dq_tile

    qspec = pl.BlockSpec((tq, LANE), lambda h, j, i: (i, h))
    kspec = pl.BlockSpec((tk, LANE), lambda h, j, i: (j, h))
    return pl.pallas_call(
        body, name=name, grid=(nh, t // tk, nq),
        in_specs=[qspec, kspec, kspec, qspec, pl.BlockSpec((None, tq, 1), lambda h, j, i: (h, i, 0)), qspec],
        out_specs=[pl.BlockSpec((t, LANE), lambda h, j, i: (0, h)), kspec, kspec],
        out_shape=[jax.ShapeDtypeStruct((t, nh * LANE), F32)] * 3,
        compiler_params=_cparams(("parallel", "arbitrary", "arbitrary")),
    )(q, k, v, o, lse, do)


def _scan_consts(c, reverse, inclusive):
    ii = lax.broadcasted_iota(jnp.int32, (c, c), 0)
    jj = lax.broadcasted_iota(jnp.int32, (c, c), 1)
    if reverse:
        incl = jj >= ii
        mask = incl if inclusive else jj > ii
    else:
        incl = jj <= ii
        mask = incl if inclusive else jj < ii
    mid = (c - 1 - c // 2) if reverse else c // 2
    incl_t = (jj <= ii) if reverse else (jj >= ii)
    return incl.astype(F32), incl_t.astype(F32), mask.astype(F32), mid


def _dot_split(a01, x):
    hi = x.astype(BF16)
    lo = (x - hi.astype(F32)).astype(BF16)
    a = a01.astype(BF16)
    return jnp.dot(a, hi, preferred_element_type=F32) + jnp.dot(a, lo, preferred_element_type=F32)


def _sub_masks(sub, dvg, u):
    if sub == 1:
        return None, None
    kl = lax.broadcasted_iota(jnp.int32, (1, LANE), 1)
    vl = lax.broadcasted_iota(jnp.int32, (1, dvg), 1)
    kw, vw = LANE // sub, dvg // sub
    km = (kl >= u * kw) & (kl < (u + 1) * kw)
    vm = (vl >= u * vw) & (vl < (u + 1) * vw)
    return km.astype(F32), vm.astype(F32)


def _block_incl(r, c, reverse, transposed):
    shift = c.bit_length() - 1
    ii = lax.broadcasted_iota(jnp.int32, (r, r), 0)
    jj = lax.broadcasted_iota(jnp.int32, (r, r), 1)
    same = lax.shift_right_logical(ii, shift) == lax.shift_right_logical(jj, shift)
    lower = (jj <= ii) if (reverse == transposed) else (jj >= ii)
    return (same & lower).astype(F32)


def _scan_chunk_fwd(b, la, mid):
    row = lax.broadcasted_iota(jnp.int32, b.shape, 0)
    bm = jnp.sum(jnp.where(row == mid, b, 0.0), axis=0, keepdims=True)
    tot = jnp.sum(la, axis=0, keepdims=True)
    e_qc = jnp.exp(b - bm)
    e_kc = jnp.exp(bm - b)
    e_qe = jnp.exp(b)
    e_kd = jnp.exp(tot - b)
    return e_qc, e_kc, e_qe, e_kd


def scan_fwd(q_arr, k_arr, v_arr, la_arr, *, qcb, kcb, vcb, lacb, la_row, chunk, dvg, sub, reverse, inclusive,
             qscale, kscale, rope=None, name="scan_fwd"):
    t = q_arr.shape[0]
    r = _row_tile(t)
    nb, nc = t // r, r // chunk
    c = chunk
    rb = (lambda j: nb - 1 - j) if reverse else (lambda j: j)
    order = list(range(nc))[::-1] if reverse else list(range(nc))
    half = RET_DK // 2

    def body(*refs):
        if rope is None:
            q_ref, k_ref, v_ref, la_ref, o_ref, st_ref, s_s = refs
        else:
            q_ref, k_ref, v_ref, la_ref, c_ref, s1_ref, s2_ref, o_ref, st_ref, s_s = refs

        @pl.when(pl.program_id(1) == 0)
        def _():
            s_s[...] = jnp.zeros_like(s_s)

        incl, _, mask, mid = _scan_consts(c, reverse, inclusive)
        for ci in order:
            rows = slice(ci * c, (ci + 1) * c)
            qv = q_ref[rows, :] * qscale
            kv = k_ref[rows, :] * kscale
            if rope is not None:
                cs, a1, a2 = c_ref[rows, :], s1_ref[rows, :], s2_ref[rows, :]
                qv = _rope_apply(qv, cs, a1, a2, half)
                kv = _rope_apply(kv, cs, a1, a2, half)
            la = jnp.broadcast_to(la_ref[...], (c, LANE)) if la_row else la_ref[rows, :]
            vv = v_ref[rows, :]
            e_qc, e_kc, e_qe, e_kd = _scan_chunk_fwd(_dot_split(incl, la), la, mid)
            qc, kc, qe, kd = qv * e_qc, kv * e_kc, qv * e_qe, kv * e_kd
            sg = s_s[...]
            st_ref[ci] = sg
            acc = None
            for u in range(sub):
                mu, vmu = _sub_masks(sub, dvg, u)
                qcu = qc if mu is None else qc * mu
                qeu = qe if mu is None else qe * mu
                a = _dot_nt(qcu, kc) * mask
                ou = _dot(a, vv) + _dot_nt(qeu, sg)
                ou = ou if vmu is None else ou * vmu
                acc = ou if acc is None else acc + ou
            o_ref[rows, :] = acc
            decay = jnp.exp(jnp.sum(la, axis=0, keepdims=True))
            s_s[...] = decay * sg + _dot_tn(vv, kd)

    specs = [pl.BlockSpec((r, LANE), lambda g, j: (rb(j), qcb + g)),
             pl.BlockSpec((r, LANE), lambda g, j: (rb(j), kcb + g)),
             pl.BlockSpec((r, dvg), lambda g, j: (rb(j), vcb + g)),
             pl.BlockSpec((1, LANE), lambda g, j: (0, lacb + g)) if la_row
             else pl.BlockSpec((r, LANE), lambda g, j: (rb(j), lacb + g))]
    args = [q_arr, k_arr, v_arr, la_arr]
    if rope is not None:
        specs += [pl.BlockSpec((r, LANE), lambda g, j: (rb(j), 0))] * 3
        args += list(rope)
    return pl.pallas_call(
        body, name=name, grid=(N_GROUPS, nb), in_specs=specs,
        out_specs=[pl.BlockSpec((r, dvg), lambda g, j: (rb(j), g)),
                   pl.BlockSpec((nc, dvg, LANE), lambda g, j: (rb(j), g, 0))],
        out_shape=[jax.ShapeDtypeStruct((t, N_GROUPS * dvg), F32),
                   jax.ShapeDtypeStruct((t // c, N_GROUPS * dvg, LANE), F32)],
        scratch_shapes=[pltpu.VMEM((dvg, LANE), F32)],
        compiler_params=_cparams(("parallel", "arbitrary")),
    )(*args)


def scan_bwd(q_arr, k_arr, v_arr, la_arr, st_arr, do_arr, prev, *, qcb, kcb, vcb, lacb, la_row, chunk, dvg, sub,
             reverse, inclusive, qscale, kscale, rope=None, name="scan_bwd"):
    t = q_arr.shape[0]
    r = _row_tile(t)
    nb, nc = t // r, r // chunk
    c = chunk
    rb = (lambda j: j) if reverse else (lambda j: nb - 1 - j)
    order = list(range(nc)) if reverse else list(range(nc))[::-1]
    half = RET_DK // 2
    n_in = 6 + (3 if rope is not None else 0) + (3 if prev is not None else 0)
    gdt = F32 if prev is None else BF16

    def body(*refs):
        ins, outs = refs[:n_in], refs[n_in:]
        q_ref, k_ref, v_ref, la_ref, st_ref, do_ref = ins[:6]
        pos = 6
        if rope is not None:
            c_ref, s1_ref, s2_ref = ins[pos:pos + 3]
            pos += 3
        if prev is not None:
            pq_ref, pk_ref, pv_ref = ins[pos:pos + 3]
        dq_ref, dk_ref, dv_ref, dla_ref, g_s = outs

        @pl.when(pl.program_id(1) == 0)
        def _():
            g_s[...] = jnp.zeros_like(g_s)
            if la_row:
                dla_ref[...] = jnp.zeros_like(dla_ref)

        incl, _, mask, mid = _scan_consts(c, reverse, inclusive)
        b_all = None if la_row else _dot_split(_block_incl(r, c, reverse, False), la_ref[...])
        pos = lax.broadcasted_iota(jnp.int32, (c, LANE), 0)
        cnt = ((c - pos) if reverse else (pos + 1)).astype(F32)
        dla_sum = jnp.zeros((1, LANE), F32)
        db_parts, dtot_parts = [None] * nc, [None] * nc
        for ci in order:
            rows = slice(ci * c, (ci + 1) * c)
            qv = q_ref[rows, :] * qscale
            kv = k_ref[rows, :] * kscale
            if rope is not None:
                cs, a1, a2 = c_ref[rows, :], s1_ref[rows, :], s2_ref[rows, :]
                qv = _rope_apply(qv, cs, a1, a2, half)
                kv = _rope_apply(kv, cs, a1, a2, half)
            la = jnp.broadcast_to(la_ref[...], (c, LANE)) if la_row else la_ref[rows, :]
            vv = v_ref[rows, :]
            dov = do_ref[rows, :]
            b = _dot_split(incl, la) if la_row else b_all[rows, :]
            e_qc, e_kc, e_qe, e_kd = _scan_chunk_fwd(b, la, mid)
            qc, kc, qe, kd = qv * e_qc, kv * e_kc, qv * e_qe, kv * e_kd
            sg = st_ref[ci]
            gn = g_s[...]
            dqc = jnp.zeros((c, LANE), F32)
            dkc = jnp.zeros((c, LANE), F32)
            dqe = jnp.zeros((c, LANE), F32)
            dvv = _dot_nt(kd, gn)
            ds_direct = jnp.zeros((dvg, LANE), F32)
            for u in range(sub):
                mu, vmu = _sub_masks(sub, dvg, u)
                qcu = qc if mu is None else qc * mu
                qeu = qe if mu is None else qe * mu
                dou = dov if vmu is None else dov * vmu
                a = _dot_nt(qcu, kc) * mask
                da = _dot_nt(dou, vv) * mask
                dvv = dvv + _dot_tn(a, dou)
                t1 = _dot(da, kc)
                dqc = dqc + (t1 if mu is None else t1 * mu)
                dkc = dkc + _dot_tn(da, qcu)
                t2 = _dot(dou, sg)
                dqe = dqe + (t2 if mu is None else t2 * mu)
                ds_direct = ds_direct + _dot_tn(dou, qeu)
            dkd = _dot(vv, gn)
            decay = jnp.exp(jnp.sum(la, axis=0, keepdims=True))
            dtot = jnp.sum(gn * sg, axis=0, keepdims=True) * decay + jnp.sum(dkd * kd, axis=0, keepdims=True)
            db = dqc * qc - dkc * kc + dqe * qe - dkd * kd
            if la_row:
                dla_sum = dla_sum + jnp.sum(db * cnt, axis=0, keepdims=True) + float(c) * dtot
            else:
                db_parts[ci] = db
                dtot_parts[ci] = jnp.broadcast_to(dtot, (c, LANE))
            dqv = dqc * e_qc + dqe * e_qe
            dkv = dkc * e_kc + dkd * e_kd
            if rope is not None:
                dqv = _rope_transpose(dqv, cs, a1, a2, half)
                dkv = _rope_transpose(dkv, cs, a1, a2, half)
            dqv = dqv * qscale
            dkv = dkv * kscale
            if prev is not None:
                dqv = dqv + pq_ref[rows, :]
                dkv = dkv + pk_ref[rows, :]
                dvv = dvv + pv_ref[rows, :]
            dq_ref[rows, :] = dqv.astype(gdt)
            dk_ref[rows, :] = dkv.astype(gdt)
            dv_ref[rows, :] = dvv.astype(gdt)
            g_s[...] = ds_direct + decay * gn
        if la_row:
            dla_ref[...] += dla_sum
        else:
            db_all = jnp.concatenate(db_parts, axis=0)
            dla_ref[...] = _dot_split(_block_incl(r, c, reverse, True), db_all) + jnp.concatenate(dtot_parts, axis=0)

    kblk = lambda cb: pl.BlockSpec((r, LANE), lambda g, j: (rb(j), cb + g))
    vblk = lambda cb: pl.BlockSpec((r, dvg), lambda g, j: (rb(j), cb + g))
    specs = [kblk(qcb), kblk(kcb), vblk(vcb),
             pl.BlockSpec((1, LANE), lambda g, j: (0, lacb + g)) if la_row else kblk(lacb),
             pl.BlockSpec((nc, dvg, LANE), lambda g, j: (rb(j), g, 0)), vblk(0)]
    args = [q_arr, k_arr, v_arr, la_arr, st_arr, do_arr]
    if rope is not None:
        specs += [pl.BlockSpec((r, LANE), lambda g, j: (rb(j), 0))] * 3
        args += list(rope)
    if prev is not None:
        specs += [kblk(0), kblk(0), vblk(0)]
        args += list(prev)
    wk = N_GROUPS * LANE
    outs = [jax.ShapeDtypeStruct((t, wk), gdt), jax.ShapeDtypeStruct((t, wk), gdt),
            jax.ShapeDtypeStruct((t, N_GROUPS * dvg), gdt),
            jax.ShapeDtypeStruct((1, wk) if la_row else (t, wk), F32)]
    return pl.pallas_call(
        body, name=name, grid=(N_GROUPS, nb), in_specs=specs,
        out_specs=[kblk(0), kblk(0), vblk(0),
                   pl.BlockSpec((1, LANE), lambda g, j: (0, g)) if la_row else kblk(0)],
        out_shape=outs,
        scratch_shapes=[pltpu.VMEM((dvg, LANE), F32)],
        compiler_params=_cparams(("parallel", "arbitrary")),
    )(*args)


def _seg_mean(x, seg):
    w = x.shape[1]
    if seg % LANE == 0:
        parts = []
        for s in range(0, w, seg):
            m = jnp.mean(x[:, s:s + seg], axis=-1, keepdims=True)
            parts.append(jnp.broadcast_to(m, (x.shape[0], seg)))
        return jnp.concatenate(parts, axis=1)
    shift = seg.bit_length() - 1
    ii = lax.shift_right_logical(lax.broadcasted_iota(jnp.int32, (w, w), 0), shift)
    jj = lax.shift_right_logical(lax.broadcasted_iota(jnp.int32, (w, w), 1), shift)
    e = (ii == jj).astype(BF16)
    hi = x.astype(BF16)
    lo = (x - hi.astype(F32)).astype(BF16)
    return (jnp.dot(hi, e, preferred_element_type=F32) + jnp.dot(lo, e, preferred_element_type=F32)) * (1.0 / seg)


def gated_norm_fwd(o_f, o_b, gate_arr, gcb, gn, seg, name="gated_norm_fwd"):
    t, w = o_f.shape
    tm = _row_tile(t)

    def body(of_ref, ob_ref, g_ref, gn_ref, y_ref):
        o = of_ref[...] + ob_ref[...]
        r = lax.rsqrt(_seg_mean(o * o, seg) + EPS)
        gt = g_ref[...]
        y_ref[...] = (gt * _sigmoid(gt) * (o * r * gn_ref[...])).astype(BF16)

    bw = max(seg, LANE)
    row = pl.BlockSpec((tm, bw), lambda j, i: (i, j))
    return pl.pallas_call(
        body, name=name, grid=(w // bw, t // tm),
        in_specs=[row, row, pl.BlockSpec((tm, bw), lambda j, i: (i, gcb + j)),
                  pl.BlockSpec((1, bw), lambda j, i: (0, j))],
        out_specs=row, out_shape=jax.ShapeDtypeStruct((t, w), BF16),
        compiler_params=_cparams(("parallel", "parallel")),
    )(o_f, o_b, gate_arr, gn.reshape(1, w))


def gated_norm_bwd(o_f, o_b, gate_arr, gcb, gn, seg, dy, name="gated_norm_bwd"):
    t, w = o_f.shape
    tm = _row_tile(t)

    def body(of_ref, ob_ref, g_ref, gn_ref, dy_ref, do_ref, dg_ref, dgn_ref):
        @pl.when(pl.program_id(1) == 0)
        def _():
            dgn_ref[...] = jnp.zeros_like(dgn_ref)
        o = of_ref[...] + ob_ref[...]
        r = lax.rsqrt(_seg_mean(o * o, seg) + EPS)
        xh = o * r
        gt = g_ref[...]
        sg = _sigmoid(gt)
        dyv = dy_ref[...]
        n = xh * gn_ref[...]
        dg_ref[...] = (dyv * n * (sg * (1.0 + gt * (1.0 - sg)))).astype(BF16)
        dn = dyv * (gt * sg)
        dgn_ref[...] += jnp.sum(dn * xh, axis=0, keepdims=True)
        dxh = dn * gn_ref[...]
        do_ref[...] = r * (dxh - xh * _seg_mean(dxh * xh, seg))

    bw = max(seg, LANE)
    row = pl.BlockSpec((tm, bw), lambda j, i: (i, j))
    vec = pl.BlockSpec((1, bw), lambda j, i: (0, j))
    return pl.pallas_call(
        body, name=name, grid=(w // bw, t // tm),
        in_specs=[row, row, pl.BlockSpec((tm, bw), lambda j, i: (i, gcb + j)), vec, row],
        out_specs=[row, row, vec],
        out_shape=[jax.ShapeDtypeStruct((t, w), F32), jax.ShapeDtypeStruct((t, w), BF16),
                   jax.ShapeDtypeStruct((1, w), F32)],
        compiler_params=_cparams(("parallel", "arbitrary")),
    )(o_f, o_b, gate_arr, gn.reshape(1, w), dy)


def gla_gate_fwd(p, wg, bg, name="gla_gate_fwd"):
    t = p.shape[0]
    tm = _row_tile(t)
    w = wg.shape[1]
    gcb = OD_GA // LANE

    def body(ga_ref, wg_ref, bg_ref, la_ref):
        z = _dot(ga_ref[...], wg_ref[...]) + bg_ref[...]
        la_ref[...] = (jnp.minimum(z, 0.0) - jnp.log(1.0 + jnp.exp(-jnp.abs(z)))) * (1.0 / GLA_TAU)

    return pl.pallas_call(
        body, name=name, grid=(t // tm,),
        in_specs=[pl.BlockSpec((tm, LANE), lambda i: (i, gcb)), pl.BlockSpec((LANE, w), lambda i: (0, 0)),
                  pl.BlockSpec((1, w), lambda i: (0, 0))],
        out_specs=pl.BlockSpec((tm, w), lambda i: (i, 0)),
        out_shape=jax.ShapeDtypeStruct((t, w), F32),
        compiler_params=_cparams(("parallel",)),
    )(p, wg, bg)


def gla_gate_bwd(p, wg, wg_t, bg, dla, name="gla_gate_bwd"):
    t = p.shape[0]
    tm = _row_tile(t)
    w = wg.shape[1]
    gcb = OD_GA // LANE

    def body(ga_ref, wg_ref, wgt_ref, bg_ref, dla_ref, dga_ref, dwg_ref, dbg_ref):
        @pl.when(pl.program_id(0) == 0)
        def _():
            dwg_ref[...] = jnp.zeros_like(dwg_ref)
            dbg_ref[...] = jnp.zeros_like(dbg_ref)
        ga = ga_ref[...]
        z = _dot(ga, wg_ref[...]) + bg_ref[...]
        dz = dla_ref[...] * (1.0 / GLA_TAU) * _sigmoid(-z)
        dga_ref[...] = _dot(dz, wgt_ref[...]).astype(BF16)
        dwg_ref[...] += _dot_tn(ga, dz)
        dbg_ref[...] += jnp.sum(dz, axis=0, keepdims=True)

    return pl.pallas_call(
        body, name=name, grid=(t // tm,),
        in_specs=[pl.BlockSpec((tm, LANE), lambda i: (i, gcb)), pl.BlockSpec((LANE, w), lambda i: (0, 0)),
                  pl.BlockSpec((w, LANE), lambda i: (0, 0)), pl.BlockSpec((1, w), lambda i: (0, 0)),
                  pl.BlockSpec((tm, w), lambda i: (i, 0))],
        out_specs=[pl.BlockSpec((tm, LANE), lambda i: (i, 0)), pl.BlockSpec((LANE, w), lambda i: (0, 0)),
                   pl.BlockSpec((1, w), lambda i: (0, 0))],
        out_shape=[jax.ShapeDtypeStruct((t, LANE), BF16), jax.ShapeDtypeStruct((LANE, w), F32),
                   jax.ShapeDtypeStruct((1, w), F32)],
        compiler_params=_cparams(("arbitrary",)),
    )(p, wg, wg_t, bg, dla)


FFN_COL = 1408


def _shifted(x, prev_row, next_row, first, last):
    tm = x.shape[0]
    row = lax.broadcasted_iota(jnp.int32, x.shape, 0)
    pr = jnp.where(first, 0.0, prev_row)
    nx = jnp.where(last, 0.0, next_row)
    xm1 = jnp.where(row == 0, pr, pltpu.roll(x, 1, 0))
    xp1 = jnp.where(row == tm - 1, nx, pltpu.roll(x, tm - 1, 0))
    return xm1, xp1


def _halo_rows(dtype):
    return SUBLANE * (4 // jnp.dtype(dtype).itemsize)


def _halo_specs(tm, tc, t, colmap, rowaxis, hr):
    nbh = tm // hr
    lasth = t // hr - 1

    def prev(*ids):
        i = ids[rowaxis]
        return (jnp.maximum(i * nbh - 1, 0), colmap(*ids))

    def nxt(*ids):
        i = ids[rowaxis]
        return (jnp.minimum((i + 1) * nbh, lasth), colmap(*ids))

    return pl.BlockSpec((hr, tc), prev), pl.BlockSpec((hr, tc), nxt)


def ffn_act_fwd(up, conv_w, conv_b, name="ffn_act_fwd"):
    t = up.shape[0]
    tm, tc = _row_tile(t), FFN_COL
    ncol = D_FF // tc

    hr = _halo_rows(up.dtype)

    def body(g_ref, gp_ref, gn_ref, v_ref, w_ref, b_ref, a_ref):
        i = pl.program_id(0)
        g = g_ref[...].astype(F32)
        gm1, gp1 = _shifted(g, gp_ref[hr - 1:hr, :].astype(F32), gn_ref[0:1, :].astype(F32), i == 0,
                            i == pl.num_programs(0) - 1)
        cc = w_ref[0:1, :] * gm1 + w_ref[1:2, :] * g + w_ref[2:3, :] * gp1 + b_ref[...]
        a_ref[...] = (cc * _sigmoid(cc) * v_ref[...].astype(F32)).astype(BF16)

    prev, nxt = _halo_specs(tm, tc, t, lambda i, j: j, 0, hr)
    return pl.pallas_call(
        body, name=name, grid=(t // tm, ncol),
        in_specs=[pl.BlockSpec((tm, tc), lambda i, j: (i, j)), prev, nxt,
                  pl.BlockSpec((tm, tc), lambda i, j: (i, j + ncol)),
                  pl.BlockSpec((SUBLANE, tc), lambda i, j: (0, j)), pl.BlockSpec((1, tc), lambda i, j: (0, j))],
        out_specs=pl.BlockSpec((tm, tc), lambda i, j: (i, j)),
        out_shape=jax.ShapeDtypeStruct((t, D_FF), BF16),
        compiler_params=_cparams(("parallel", "parallel")),
    )(up, up, up, up, conv_w, conv_b)


def ffn_act_bwd(up, conv_w, conv_b, dact, name="ffn_act_bwd"):
    t = up.shape[0]
    tm, tc = _row_tile(t), FFN_COL
    ncol = D_FF // tc
    hr = _halo_rows(up.dtype)

    def body(g_ref, gp_ref, gn_ref, v_ref, w_ref, b_ref, da_ref, dc_ref, dv_ref, dw_ref):
        i = pl.program_id(1)

        @pl.when(i == 0)
        def _():
            dw_ref[...] = jnp.zeros_like(dw_ref)
        g = g_ref[...].astype(F32)
        gm1, gp1 = _shifted(g, gp_ref[hr - 1:hr, :].astype(F32), gn_ref[0:1, :].astype(F32), i == 0,
                            i == pl.num_programs(1) - 1)
        cc = w_ref[0:1, :] * gm1 + w_ref[1:2, :] * g + w_ref[2:3, :] * gp1 + b_ref[...]
        sg = _sigmoid(cc)
        da = da_ref[...]
        dv_ref[...] = (da * (cc * sg)).astype(BF16)
        dc = da * v_ref[...].astype(F32) * (sg * (1.0 + cc * (1.0 - sg)))
        dc_ref[...] = dc
        dw_ref[0:1, :] += jnp.sum(dc * gm1, axis=0, keepdims=True)
        dw_ref[1:2, :] += jnp.sum(dc * g, axis=0, keepdims=True)
        dw_ref[2:3, :] += jnp.sum(dc * gp1, axis=0, keepdims=True)
        dw_ref[3:4, :] += jnp.sum(dc, axis=0, keepdims=True)

    prev, nxt = _halo_specs(tm, tc, t, lambda j, i: j, 1, hr)
    tile = pl.BlockSpec((tm, tc), lambda j, i: (i, j))
    return pl.pallas_call(
        body, name=name, grid=(ncol, t // tm),
        in_specs=[tile, prev, nxt, pl.BlockSpec((tm, tc), lambda j, i: (i, j + ncol)),
                  pl.BlockSpec((SUBLANE, tc), lambda j, i: (0, j)), pl.BlockSpec((1, tc), lambda j, i: (0, j)), tile],
        out_specs=[tile, tile, pl.BlockSpec((SUBLANE, tc), lambda j, i: (0, j))],
        out_shape=[jax.ShapeDtypeStruct((t, D_FF), F32), jax.ShapeDtypeStruct((t, D_FF), BF16),
                   jax.ShapeDtypeStruct((SUBLANE, D_FF), F32)],
        compiler_params=_cparams(("parallel", "arbitrary")),
    )(up, up, up, up, conv_w, conv_b, dact)


def conv_transpose(dc, conv_w, name="conv_transpose"):
    t = dc.shape[0]
    tm, tc = _row_tile(t), FFN_COL
    hr = _halo_rows(dc.dtype)

    def body(d_ref, dp_ref, dn_ref, w_ref, o_ref):
        i = pl.program_id(0)
        d = d_ref[...]
        dm1, dp1 = _shifted(d, dp_ref[hr - 1:hr, :], dn_ref[0:1, :], i == 0, i == pl.num_programs(0) - 1)
        o_ref[...] = (w_ref[0:1, :] * dp1 + w_ref[1:2, :] * d + w_ref[2:3, :] * dm1).astype(BF16)

    prev, nxt = _halo_specs(tm, tc, t, lambda i, j: j, 0, hr)
    tile = pl.BlockSpec((tm, tc), lambda i, j: (i, j))
    return pl.pallas_call(
        body, name=name, grid=(t // tm, D_FF // tc),
        in_specs=[tile, prev, nxt, pl.BlockSpec((SUBLANE, tc), lambda i, j: (0, j))],
        out_specs=tile, out_shape=jax.ShapeDtypeStruct((t, D_FF), BF16),
        compiler_params=_cparams(("parallel", "parallel")),
    )(dc, dc, dc, conv_w)


def loss_head(y, target, name="loss_head"):
    t, d = y.shape
    tm = _row_tile(t)

    def body(y_ref, t_ref, dy_ref, l_ref):
        @pl.when(pl.program_id(0) == 0)
        def _():
            l_ref[...] = jnp.zeros_like(l_ref)
        e = y_ref[...] - t_ref[...]
        dy_ref[...] = e * (1.0 / d)
        rowloss = jnp.sum(e * e, axis=-1, keepdims=True) * (0.5 / d)
        l_ref[...] += jnp.sum(rowloss, axis=0, keepdims=True)

    row = pl.BlockSpec((tm, d), lambda i: (i, 0))
    return pl.pallas_call(
        body, name=name, grid=(t // tm,), in_specs=[row, row],
        out_specs=[row, pl.BlockSpec((1, 1), lambda i: (0, 0))],
        out_shape=[jax.ShapeDtypeStruct((t, d), F32), jax.ShapeDtypeStruct((1, 1), F32)],
        compiler_params=_cparams(("arbitrary",)),
    )(y, target)


def _pad_heads(w, heads, width):
    lead = w.shape[:-1]
    w = w.reshape(*lead, heads, width)
    w = jnp.pad(w, [(0, 0)] * len(lead) + [(0, 0), (0, LANE - width)])
    return w.reshape(*lead, heads * LANE)


def _unpad_heads(w, heads, width):
    lead = w.shape[:-1]
    return w.reshape(*lead, heads, LANE)[..., :width].reshape(*lead, heads * width)


def _pad_rows_heads(w, heads, width):
    return _pad_heads(w.T, heads, width).T


def _unpad_rows_heads(w, heads, width):
    return _unpad_heads(w.T, heads, width).T


_EV_REAL = MLA_Q_RANK + MLA_KV_RANK + MLA_ROPE


def prep_even(wts, dt=BF16):
    w_in_t = wts["w_in_t"]
    w_in_tp = jnp.concatenate([w_in_t[:_EV_REAL], jnp.zeros((EV_RQ - _EV_REAL, D_MODEL), w_in_t.dtype),
                               w_in_t[_EV_REAL:]], axis=0).astype(dt)
    wuq = _pad_heads(wts["w_uq"], MLA_HEADS, MLA_QK).astype(dt)
    ukv = wts["w_ukv"].reshape(MLA_KV_RANK, MLA_HEADS, MLA_NOPE + MLA_V)
    wk = _pad_heads(ukv[..., :MLA_NOPE].reshape(MLA_KV_RANK, -1), MLA_HEADS, MLA_NOPE).astype(dt)
    wv = _pad_heads(ukv[..., MLA_NOPE:].reshape(MLA_KV_RANK, -1), MLA_HEADS, MLA_V).astype(dt)
    w_out = wts["w_out"]
    wa = _pad_rows_heads(w_out[:MLA_HEADS * MLA_V], MLA_HEADS, MLA_V).astype(dt)
    wr = w_out[MLA_HEADS * MLA_V:].astype(dt)
    pad1 = lambda v, n: jnp.pad(v.astype(F32), (0, n - v.shape[0])).reshape(1, n)
    lg = lambda th: jnp.log1p(-jnp.exp2(-th.astype(F32)))
    return dict(
        w_in=w_in_tp.T, w_in_t=w_in_tp, wuq=wuq, wuq_t=wuq.T, wk=wk, wk_t=wk.T, wv=wv, wv_t=wv.T,
        wa=wa, wa_t=wa.T, wr=wr, wr_t=wr.T,
        mix_norm=wts["mix_norm"].astype(F32), q_norm=wts["q_norm"].astype(F32).reshape(1, -1),
        kv_norm=wts["kv_norm"].astype(F32).reshape(1, -1),
        qhn=pad1(wts["q_head_norm"], LANE), khn=pad1(wts["k_head_norm"], LANE),
        la_f=jnp.repeat(lg(wts["theta_fwd"]), RET_DK).reshape(1, -1),
        la_b=jnp.repeat(lg(wts["theta_bwd"]), RET_DK).reshape(1, -1),
        out_norm=wts["ret_out_norm"].astype(F32).reshape(-1),
    )


def prep_odd(wts, dt=BF16):
    w_in_t = wts["w_in_t"]
    w_in_tp = jnp.concatenate([w_in_t, jnp.zeros((OD_IN - w_in_t.shape[0], D_MODEL), w_in_t.dtype)],
                              axis=0).astype(dt)
    hk = GLA_HEADS * GLA_DK
    wg = jnp.zeros((LANE, 2 * hk), F32)
    wg = wg.at[:GLA_GATE_RANK, :hk].set(wts["w_gate_fwd"].astype(F32))
    wg = wg.at[GLA_GATE_RANK:2 * GLA_GATE_RANK, hk:].set(wts["w_gate_bwd"].astype(F32))
    wg = wg.astype(dt)
    bg = jnp.concatenate([wts["b_gate_fwd"], wts["b_gate_bwd"]]).astype(F32).reshape(1, -1)
    w_out = wts["w_out"].astype(dt)
    return dict(w_in=w_in_tp.T, w_in_t=w_in_tp, wg=wg, wg_t=wg.T, bg=bg, w_out=w_out, w_out_t=w_out.T,
                mix_norm=wts["mix_norm"].astype(F32), out_norm=wts["gla_out_norm"].astype(F32).reshape(-1))


def prep_ffn(wts, dt=BF16):
    w_up_t = wts["w_up_t"].astype(dt)
    w_down = wts["w_down"].astype(dt)
    cw = jnp.pad(wts["conv_w"].astype(F32), ((0, SUBLANE - 3), (0, 0)))
    return dict(w_up=w_up_t.T, w_up_t=w_up_t, w_down=w_down, w_down_t=w_down.T, conv_w=cw,
                conv_b=wts["conv_b"].astype(F32).reshape(1, -1), norm=wts["norm"].astype(F32))


_RET = dict(qcb=EV_RQ // LANE, kcb=EV_RK // LANE, vcb=EV_RV // LANE, la_row=True, chunk=RET_CHUNK, dvg=LANE,
            sub=2, qscale=1.0, kscale=RET_DK ** -0.5)
_GLA = dict(qcb=OD_Q // LANE, kcb=OD_K // LANE, vcb=OD_V // GLA_DV, la_row=False, chunk=GLA_CHUNK, dvg=GLA_DV,
            sub=1, qscale=GLA_DK ** -0.5, kscale=1.0)
_FWD_DIR = dict(reverse=False, inclusive=True)
_BWD_DIR = dict(reverse=True, inclusive=False)


def even_fwd(x, w, tabs_mla, tabs_ret, tag):
    h = rmsnorm_fwd(x, w["mix_norm"], name=f"{tag}_norm")
    p = mm_nn(h, w["w_in"], name=f"{tag}_in")
    q, k, v = mla_prep_fwd(p, tabs_mla, w["q_norm"], w["kv_norm"], w["wuq"], w["wk"], w["wv"], w["qhn"], w["khn"],
                           name=f"{tag}_mla_prep")
    o, lse = attn_fwd(q, k, v, name=f"{tag}_attn")
    of, stf = scan_fwd(p, p, p, w["la_f"], lacb=0, rope=tabs_ret, name=f"{tag}_ret_f", **_RET, **_FWD_DIR)
    ob, stb = scan_fwd(p, p, p, w["la_b"], lacb=0, rope=tabs_ret, name=f"{tag}_ret_b", **_RET, **_BWD_DIR)
    r = gated_norm_fwd(of, ob, p, EV_RG // LANE, w["out_norm"], RET_DV, name=f"{tag}_ret_out")
    x1 = mm_nn(o, w["wa"], res=x, name=f"{tag}_out_a")
    x2 = mm_nn(r, w["wr"], res=x1, name=f"{tag}_out_r")
    return x2, dict(x=x, h=h, p=p, q=q, k=k, v=v, o=o, lse=lse, of=of, ob=ob, stf=stf, stb=stb, r=r)


def even_bwd(dx, s, w, tabs_mla, tabs_ret, tag):
    tag = tag + "_b"
    do = mm_nn(dx, w["wa_t"], name=f"{tag}_dout_a")
    dr = mm_nn(dx, w["wr_t"], name=f"{tag}_dout_r")
    d_wa = mm_tn(s["o"], dx, out_dtype=BF16, name=f"{tag}_dwa")
    d_wr = mm_tn(s["r"], dx, out_dtype=BF16, name=f"{tag}_dwr")
    dq, dk, dv = attn_bwd(s["q"], s["k"], s["v"], s["o"], s["lse"], do, name=f"{tag}_attn")
    (dp_mla, d_wuq, d_wk, d_wv, d_qn, d_kvn, d_qhn, d_khn) = mla_prep_bwd(
        s["p"], tabs_mla, w["q_norm"], w["kv_norm"], w["wuq"], w["wk"], w["wv"], w["wuq_t"], w["wk_t"], w["wv_t"],
        w["qhn"], w["khn"], dq, dk, dv, name=f"{tag}_mla_prep")
    d_o, d_gate, d_gn = gated_norm_bwd(s["of"], s["ob"], s["p"], EV_RG // LANE, w["out_norm"], RET_DV, dr,
                                       name=f"{tag}_ret_out")
    p = s["p"]
    g1 = scan_bwd(p, p, p, w["la_f"], s["stf"], d_o, None, lacb=0, rope=tabs_ret, name=f"{tag}_ret_f",
                  **_RET, **_FWD_DIR)
    g2 = scan_bwd(p, p, p, w["la_b"], s["stb"], d_o, g1[:3], lacb=0, rope=tabs_ret, name=f"{tag}_ret_b",
                  **_RET, **_BWD_DIR)
    dp = jnp.concatenate([dp_mla, g2[0], g2[1], g2[2], d_gate], axis=1)
    dh = mm_nn(dp, w["w_in_t"], name=f"{tag}_dh")
    d_win_t = mm_tn(dp, s["h"], out_dtype=BF16, name=f"{tag}_dwin")
    dx_in, d_mix = rmsnorm_bwd(s["x"], w["mix_norm"], dh, dx, name=f"{tag}_norm")
    grads = dict(w_in_t=d_win_t, wuq=d_wuq, wk=d_wk, wv=d_wv, wa=d_wa, wr=d_wr, mix_norm=d_mix, q_norm=d_qn,
                 kv_norm=d_kvn, qhn=d_qhn, khn=d_khn, la_f=g1[3], la_b=g2[3], out_norm=d_gn)
    return dx_in, grads


def odd_fwd(x, w, tag):
    h = rmsnorm_fwd(x, w["mix_norm"], name=f"{tag}_norm")
    p = mm_nn(h, w["w_in"], name=f"{tag}_in")
    la = gla_gate_fwd(p, w["wg"], w["bg"], name=f"{tag}_gate")
    of, stf = scan_fwd(p, p, p, la, lacb=0, name=f"{tag}_gla_f", **_GLA, **_FWD_DIR)
    ob, stb = scan_fwd(p, p, p, la, lacb=N_GROUPS, name=f"{tag}_gla_b", **_GLA, **_BWD_DIR)
    y = gated_norm_fwd(of, ob, p, OD_R // GLA_DV, w["out_norm"], GLA_DV, name=f"{tag}_gla_out")
    x1 = mm_nn(y, w["w_out"], res=x, name=f"{tag}_out")
    return x1, dict(x=x, h=h, p=p, la=la, of=of, ob=ob, stf=stf, stb=stb, y=y)


def odd_bwd(dx, s, w, tag):
    tag = tag + "_b"
    dy = mm_nn(dx, w["w_out_t"], name=f"{tag}_dout")
    d_wout = mm_tn(s["y"], dx, out_dtype=BF16, name=f"{tag}_dwout")
    d_o, d_gate, d_gn = gated_norm_bwd(s["of"], s["ob"], s["p"], OD_R // GLA_DV, w["out_norm"], GLA_DV, dy,
                                       name=f"{tag}_gla_out")
    p, la = s["p"], s["la"]
    g1 = scan_bwd(p, p, p, la, s["stf"], d_o, None, lacb=0, name=f"{tag}_gla_f", **_GLA, **_FWD_DIR)
    g2 = scan_bwd(p, p, p, la, s["stb"], d_o, g1[:3], lacb=N_GROUPS, name=f"{tag}_gla_b", **_GLA, **_BWD_DIR)
    dla = jnp.concatenate([g1[3], g2[3]], axis=1)
    d_ga, d_wg, d_bg = gla_gate_bwd(p, w["wg"], w["wg_t"], w["bg"], dla, name=f"{tag}_gate")
    dp = jnp.concatenate([g2[0], g2[1], g2[2], d_gate, d_ga], axis=1)
    dh = mm_nn(dp, w["w_in_t"], name=f"{tag}_dh")
    d_win_t = mm_tn(dp, s["h"], out_dtype=BF16, name=f"{tag}_dwin")
    dx_in, d_mix = rmsnorm_bwd(s["x"], w["mix_norm"], dh, dx, name=f"{tag}_norm")
    grads = dict(w_in_t=d_win_t, wg=d_wg, bg=d_bg, w_out=d_wout, mix_norm=d_mix, out_norm=d_gn)
    return dx_in, grads


def ffn_fwd(x, w, tag):
    h = rmsnorm_fwd(x, w["norm"], name=f"{tag}_norm")
    up = mm_nn(h, w["w_up"], out_dtype=BF16, name=f"{tag}_up")
    act = ffn_act_fwd(up, w["conv_w"], w["conv_b"], name=f"{tag}_act")
    x1 = mm_nn(act, w["w_down"], res=x, name=f"{tag}_down")
    return x1, dict(x=x, h=h, up=up, act=act)


def ffn_bwd(dx, s, w, tag):
    tag = tag + "_b"
    dact = mm_nn(dx, w["w_down_t"], name=f"{tag}_dact")
    d_wdown = mm_tn(s["act"], dx, out_dtype=BF16, name=f"{tag}_dwdown")
    dc, dval, d_conv = ffn_act_bwd(s["up"], w["conv_w"], w["conv_b"], dact, name=f"{tag}_act")
    dgate = conv_transpose(dc, w["conv_w"], name=f"{tag}_convt")
    dh1 = mm_nn(dgate, w["w_up_t"][:D_FF], name=f"{tag}_dh_g")
    dh = mm_nn(dval, w["w_up_t"][D_FF:], res=dh1, name=f"{tag}_dh_v")
    d_wup_t = jnp.concatenate([mm_tn(dgate, s["h"], out_dtype=BF16, name=f"{tag}_dwup_g"),
                               mm_tn(dval, s["h"], out_dtype=BF16, name=f"{tag}_dwup_v")], axis=0)
    dx_in, d_norm = rmsnorm_bwd(s["x"], w["norm"], dh, dx, name=f"{tag}_norm")
    grads = dict(w_up_t=d_wup_t, w_down=d_wdown, conv_w=d_conv[:3], conv_b=d_conv[3], norm=d_norm)
    return dx_in, grads


def unprep_even_grads(g, theta_fwd, theta_bwd):
    d_win_t = jnp.concatenate([g["w_in_t"][:_EV_REAL], g["w_in_t"][EV_RQ:]], axis=0)
    d_uq = _unpad_heads(g["wuq"], MLA_HEADS, MLA_QK)
    dk_ = _unpad_heads(g["wk"], MLA_HEADS, MLA_NOPE).reshape(MLA_KV_RANK, MLA_HEADS, MLA_NOPE)
    dv_ = _unpad_heads(g["wv"], MLA_HEADS, MLA_V).reshape(MLA_KV_RANK, MLA_HEADS, MLA_V)
    d_ukv = jnp.concatenate([dk_, dv_], axis=-1).reshape(MLA_KV_RANK, -1)
    d_wout = jnp.concatenate([_unpad_rows_heads(g["wa"], MLA_HEADS, MLA_V), g["wr"]], axis=0)

    def dtheta(dla, th):
        dlg = dla.reshape(RET_HEADS, RET_DK).sum(axis=-1)
        e = jnp.exp2(-th.astype(F32))
        return dlg * (e * math.log(2.0)) / (1.0 - e)

    return dict(mix_norm=g["mix_norm"].reshape(-1), w_in_t=d_win_t, q_norm=g["q_norm"].reshape(-1),
                kv_norm=g["kv_norm"].reshape(-1), w_uq=d_uq, w_ukv=d_ukv, q_head_norm=g["qhn"].reshape(-1)[:MLA_QK],
                k_head_norm=g["khn"].reshape(-1)[:MLA_QK], theta_fwd=dtheta(g["la_f"], theta_fwd),
                theta_bwd=dtheta(g["la_b"], theta_bwd), ret_out_norm=g["out_norm"].reshape(RET_HEADS, RET_DV),
                w_out=d_wout)


def unprep_odd_grads(g):
    hk = GLA_HEADS * GLA_DK
    return dict(mix_norm=g["mix_norm"].reshape(-1), w_in_t=g["w_in_t"][:OD_GA + 2 * GLA_GATE_RANK],
                w_gate_fwd=g["wg"][:GLA_GATE_RANK, :hk], b_gate_fwd=g["bg"].reshape(-1)[:hk],
                w_gate_bwd=g["wg"][GLA_GATE_RANK:2 * GLA_GATE_RANK, hk:], b_gate_bwd=g["bg"].reshape(-1)[hk:],
                gla_out_norm=g["out_norm"].reshape(GLA_HEADS, GLA_DV), w_out=g["w_out"])


def unprep_ffn_grads(g):
    return dict(norm=g["norm"].reshape(-1), w_up_t=g["w_up_t"], conv_w=g["conv_w"], conv_b=g["conv_b"],
                w_down=g["w_down"])


def _mesh_pos():
    return tuple(lax.axis_index(n) for n in MESH_AXES)


def _slot(px, py, pc):
    return 4 * px + 2 * py + pc


def all_gather_blocks(blk, name="weight_all_gather"):
    r, w = blk.shape

    def body(x_ref, out_ref, send_sems, recv_sems, local_sem):
        x, y, c = _mesh_pos()
        me, sibling = (x, y, c), (x, y, 1 - c)
        chips = [(1 - x, y), (x, 1 - y), (1 - x, 1 - y)]

        def copy(k, block, to, src=None):
            dst = out_ref.at[_slot(*block)]
            return pltpu.make_async_remote_copy(
                src_ref=dst if src is None else src, dst_ref=dst, send_sem=send_sems.at[k],
                recv_sem=recv_sems.at[k], device_id=to, device_id_type=pl.DeviceIdType.MESH)

        mine = pltpu.make_async_copy(x_ref, out_ref.at[_slot(*me)], local_sem)
        mine.start()
        first = [copy(0, me, sibling, src=x_ref)]
        first += [copy(1 + j, me, (*chip, c), src=x_ref) for j, chip in enumerate(chips)]
        for cp in first:
            cp.start()
        passed = [copy(4 + j, (*chip, c), sibling) for j, chip in enumerate(chips)]
        for j, chip in enumerate(chips):
            copy(1 + j, (*chip, c), me).wait_recv()
            passed[j].start()
        copy(0, sibling, me).wait_recv()
        for j, chip in enumerate(chips):
            copy(4 + j, (*chip, 1 - c), me).wait_recv()
        for cp in first + passed:
            cp.wait_send()
        mine.wait()

    return pl.pallas_call(
        body, name=name,
        out_shape=jax.ShapeDtypeStruct((N_DEV, r, w), blk.dtype),
        in_specs=[pl.BlockSpec(memory_space=pl.ANY)],
        out_specs=pl.BlockSpec(memory_space=pl.ANY),
        scratch_shapes=[pltpu.SemaphoreType.DMA((7,)), pltpu.SemaphoreType.DMA((7,)), pltpu.SemaphoreType.DMA],
    )(blk)


N_CHIP = 4


def pair_exchange(other, name="grad_pair_exchange"):
    _, r, w = other.shape

    def body(s_ref, r_ref, send_sems, recv_sems):
        x, y, c = _mesh_pos()
        copies = []
        for s in range(N_CHIP):
            cp = pltpu.make_async_remote_copy(
                src_ref=s_ref.at[s], dst_ref=r_ref.at[s], send_sem=send_sems.at[s], recv_sem=recv_sems.at[s],
                device_id=(x, y, 1 - c), device_id_type=pl.DeviceIdType.MESH)
            cp.start()
            copies.append(cp)
        for cp in copies:
            cp.wait()

    return pl.pallas_call(
        body, name=name,
        out_shape=jax.ShapeDtypeStruct((N_CHIP, r, w), other.dtype),
        in_specs=[pl.BlockSpec(memory_space=pl.ANY)],
        out_specs=pl.BlockSpec(memory_space=pl.ANY),
        scratch_shapes=[pltpu.SemaphoreType.DMA((N_CHIP,)), pltpu.SemaphoreType.DMA((N_CHIP,))],
    )(other)


def chip_exchange(part, name="grad_chip_exchange"):
    _, r, w = part.shape

    def body(s_ref, r_ref, send_sems, recv_sems, local_sem):
        x, y, c = _mesh_pos()
        me = 2 * x + y
        mine = pltpu.make_async_copy(s_ref.at[me], r_ref.at[me], local_sem)
        mine.start()
        copies = []
        for k in range(1, N_CHIP):
            px = 1 - x if (k >> 1) & 1 else x
            py = 1 - y if k & 1 else y
            cp = pltpu.make_async_remote_copy(
                src_ref=s_ref.at[2 * px + py], dst_ref=r_ref.at[me], send_sem=send_sems.at[k - 1],
                recv_sem=recv_sems.at[k - 1], device_id=(px, py, c), device_id_type=pl.DeviceIdType.MESH)
            cp.start()
            copies.append(cp)
        for cp in copies:
            cp.wait()
        mine.wait()

    return pl.pallas_call(
        body, name=name,
        out_shape=jax.ShapeDtypeStruct((N_CHIP, r, w), part.dtype),
        in_specs=[pl.BlockSpec(memory_space=pl.ANY)],
        out_specs=pl.BlockSpec(memory_space=pl.ANY),
        scratch_shapes=[pltpu.SemaphoreType.DMA((N_CHIP - 1,)), pltpu.SemaphoreType.DMA((N_CHIP - 1,)),
                        pltpu.SemaphoreType.DMA],
    )(part)


FLAT_W = 1024
FLAT_TILE = 256


def pair_add(mine, theirs, name="grad_pair_add"):
    n, r, w = mine.shape

    def body(a_ref, b_ref, o_ref):
        o_ref[...] = (a_ref[...].astype(F32) + b_ref[...].astype(F32)).astype(o_ref.dtype)

    tr = _slab_tile(r)
    blk = pl.BlockSpec((n, tr, w), lambda i: (0, i, 0))
    return pl.pallas_call(
        body, name=name, grid=(r // tr,), in_specs=[blk, blk], out_specs=blk,
        out_shape=jax.ShapeDtypeStruct((n, r, w), mine.dtype),
        compiler_params=_cparams(("parallel",)),
    )(mine, theirs)


def sum_slots(recv, name="grad_sum"):
    n, r, w = recv.shape

    def body(r_ref, o_ref):
        acc = r_ref[0].astype(F32)
        for k in range(1, n):
            acc = acc + r_ref[k].astype(F32)
        o_ref[...] = acc

    tr = _slab_tile(r)
    return pl.pallas_call(
        body, name=name, grid=(r // tr,),
        in_specs=[pl.BlockSpec((n, tr, w), lambda i: (0, i, 0))],
        out_specs=pl.BlockSpec((tr, w), lambda i: (i, 0)),
        out_shape=jax.ShapeDtypeStruct((r, w), F32),
        compiler_params=_cparams(("parallel",)),
    )(recv)


def _slab_tile(r):
    return max(t for t in range(SUBLANE, FLAT_TILE + 1, SUBLANE) if r % t == 0)


def adamw(wf, gf, mf, vf, name="adamw"):
    r, w = wf.shape
    tr = _slab_tile(r)

    def body(w_ref, g_ref, m_ref, v_ref, d_ref, m_out, v_out):
        g = g_ref[...]
        m = ADAM_B1 * m_ref[...] + (1.0 - ADAM_B1) * g
        v = ADAM_B2 * v_ref[...] + (1.0 - ADAM_B2) * (g * g)
        m_hat = m / (1.0 - ADAM_B1 ** ADAM_STEP)
        v_hat = v / (1.0 - ADAM_B2 ** ADAM_STEP)
        d_ref[...] = -ADAM_LR * (m_hat / (jnp.sqrt(v_hat) + ADAM_EPS) + ADAM_WD * w_ref[...])
        m_out[...] = m
        v_out[...] = v

    tile = pl.BlockSpec((tr, w), lambda i: (i, 0))
    return pl.pallas_call(
        body, name=name, grid=(r // tr,), in_specs=[tile] * 4, out_specs=[tile] * 3,
        out_shape=[jax.ShapeDtypeStruct((r, w), F32)] * 3,
        compiler_params=_cparams(("parallel",)),
    )(wf, gf, mf, vf)


ROW_FORM = [("w_in_even", "T"), ("w_out_even", "R"), ("w_in_odd", "T"), ("w_out_odd", "R"), ("ffn_w_up", "T"),
            ("ffn_w_down", "R")]
SHARDED_MID = [("mla_w_uq", 2), ("mla_w_ukv", 2)]
SHARDED_SMALL = [("mix_norm_odd", 1), ("gla_w_gate_fwd", 2), ("gla_b_gate_fwd", 1), ("gla_w_gate_bwd", 2),
                 ("gla_b_gate_bwd", 1), ("gla_out_norm", 2), ("ffn_conv_w", 2)]
REPLICATED = ["mix_norm_even", "mla_q_norm", "mla_kv_norm", "mla_q_head_norm", "mla_k_head_norm", "ret_theta_fwd",
              "ret_theta_bwd", "ret_out_norm", "ffn_norm", "ffn_conv_b"]
WEIGHT_NAMES = ["mix_norm_even", "w_in_even", "mla_q_norm", "mla_kv_norm", "mla_w_uq", "mla_w_ukv",
                "mla_q_head_norm", "mla_k_head_norm", "ret_theta_fwd", "ret_theta_bwd", "ret_out_norm", "w_out_even",
                "mix_norm_odd", "w_in_odd", "gla_w_gate_fwd", "gla_b_gate_fwd", "gla_w_gate_bwd", "gla_b_gate_bwd",
                "gla_out_norm", "w_out_odd", "ffn_norm", "ffn_w_up", "ffn_conv_w", "ffn_conv_b", "ffn_w_down"]


def _round_up(n, m):
    return -(-n // m) * m


def _pack_rows(parts, rows):
    flat = jnp.concatenate(parts, axis=-1)
    pad = rows * FLAT_W - flat.shape[-1]
    flat = jnp.pad(flat, [(0, 0)] * (flat.ndim - 1) + [(0, pad)])
    return flat.reshape(*flat.shape[:-1], rows, FLAT_W)


def _row_form(v, form):
    if form == "T":
        v = jnp.swapaxes(v, 1, 2)
    return v.reshape(-1, v.shape[-1])


def _row_counts(local_shapes):
    return [local_shapes[n][0] * local_shapes[n][2 if f == "T" else 1] for n, f in ROW_FORM]


def _tail_layout(local_shapes):
    n_sh = sum(math.prod(local_shapes[n]) for n, _ in SHARDED_MID + SHARDED_SMALL)
    n_rep = sum(math.prod(local_shapes[n]) for n in REPLICATED)
    sh_rows = _round_up(-(-n_sh // FLAT_W), SUBLANE)
    rep_rows = _round_up(-(-n_rep // FLAT_W), SUBLANE)
    return sh_rows, rep_rows, _round_up(sh_rows + rep_rows, FLAT_TILE)


def pack_tail(vals, local_shapes):
    sh_rows, rep_rows, rows = _tail_layout(local_shapes)
    sh = _pack_rows([vals[n].astype(F32).reshape(-1) for n, _ in SHARDED_MID + SHARDED_SMALL], sh_rows)
    rep = _pack_rows([vals[n].astype(F32).reshape(-1) for n in REPLICATED], rep_rows)
    return jnp.concatenate([sh, rep, jnp.zeros((rows - sh_rows - rep_rows, FLAT_W), F32)], axis=0)


def unpack_tail(tail, local_shapes):
    sh_rows, rep_rows, _ = _tail_layout(local_shapes)
    out = {}
    for names, flat in (([n for n, _ in SHARDED_MID + SHARDED_SMALL], tail[:sh_rows].reshape(-1)),
                        (REPLICATED, tail[sh_rows:sh_rows + rep_rows].reshape(-1))):
        off = 0
        for n in names:
            k = math.prod(local_shapes[n])
            out[n] = flat[off:off + k].reshape(local_shapes[n])
            off += k
    return out


def unpack_rows(slab, local_shapes):
    out = {}
    off = 0
    for (n, form), rows in zip(ROW_FORM, _row_counts(local_shapes)):
        loc = local_shapes[n]
        piece = slab[off:off + rows]
        if form == "T":
            piece = jnp.swapaxes(piece.reshape(loc[0], loc[2], loc[1]), 1, 2)
        out[n] = piece.reshape(loc)
        off += rows
    return out


def pack_grad_blocks(full_grads, local_shapes):
    sh_rows, rep_rows, rows = _tail_layout(local_shapes)
    my_c = lax.axis_index("c")

    def by_core(blocks8):
        b = blocks8.reshape(N_CHIP, 2, *blocks8.shape[1:])
        return (lax.dynamic_index_in_dim(b, my_c, 1, keepdims=False),
                lax.dynamic_index_in_dim(b, 1 - my_c, 1, keepdims=False))

    blocks = []
    for n, form in ROW_FORM:
        g = full_grads[n].astype(BF16)
        layers, total = g.shape[0], g.shape[1]
        g = g.reshape(layers, N_DEV, total // N_DEV, FLAT_W)
        blocks.append(by_core(jnp.swapaxes(g, 0, 1).reshape(N_DEV, -1, FLAT_W)))
    parts = []
    for n, ax in SHARDED_MID + SHARDED_SMALL:
        g = full_grads[n].astype(F32)
        loc = local_shapes[n]
        g = g.reshape(*g.shape[:ax], N_DEV, loc[ax], *g.shape[ax + 1:])
        parts.append(jnp.moveaxis(g, ax, 0).reshape(N_DEV, -1))
    sh = _pack_rows(parts, sh_rows)
    rep = _pack_rows([full_grads[n].astype(F32).reshape(-1) for n in REPLICATED], rep_rows)
    rep = jnp.broadcast_to(rep[None], (N_DEV, rep_rows, FLAT_W))
    pad = jnp.zeros((N_DEV, rows - sh_rows - rep_rows, FLAT_W), F32)
    blocks.append(by_core(jnp.concatenate([sh, rep, pad], axis=1).astype(BF16)))
    return (jnp.concatenate([b[0] for b in blocks], axis=1), jnp.concatenate([b[1] for b in blocks], axis=1))


def pack_gather_block(vals, local_shapes):
    big = jnp.concatenate([_row_form(vals[n].astype(BF16), f) for n, f in ROW_FORM], axis=0)
    mid = [vals[n].astype(BF16).reshape(-1) for n, _ in SHARDED_MID]
    small = jnp.concatenate([vals[n].astype(F32).reshape(-1) for n, _ in SHARDED_SMALL])
    small = lax.bitcast_convert_type(small, BF16).reshape(-1)
    n = sum(v.shape[0] for v in mid) + small.shape[0]
    tail = _pack_rows(mid + [small], _round_up(-(-n // FLAT_W), 2 * SUBLANE))
    return jnp.concatenate([big, tail], axis=0)


def unpack_gathered(gathered, local_shapes):
    out = {}
    off = 0
    for (n, form), rows in zip(ROW_FORM, _row_counts(local_shapes)):
        layers = local_shapes[n][0]
        piece = gathered[:, off:off + rows].reshape(N_DEV, layers, rows // layers, FLAT_W)
        out[n] = jnp.swapaxes(piece, 0, 1).reshape(layers, N_DEV * (rows // layers), FLAT_W)
        off += rows
    flat = gathered[:, off:].reshape(N_DEV, -1)
    off = 0

    def full(piece, n, ax):
        loc = local_shapes[n]
        piece = jnp.moveaxis(piece.reshape(N_DEV, *loc), 0, ax)
        return piece.reshape(*loc[:ax], N_DEV * loc[ax], *loc[ax + 1:])

    for n, ax in SHARDED_MID:
        k = math.prod(local_shapes[n])
        out[n] = full(flat[:, off:off + k], n, ax)
        off += k
    for n, ax in SHARDED_SMALL:
        k = math.prod(local_shapes[n])
        pairs = flat[:, off:off + 2 * k].reshape(N_DEV, k, 2)
        out[n] = full(lax.bitcast_convert_type(pairs, F32), n, ax)
        off += 2 * k
    return out


def local_step(x, positions, target, fw):
    tabs_mla = rope_tables(positions, MLA_NOPE, MLA_ROPE // 2, LANE)
    tabs_ret = rope_tables(positions, 0, RET_DK // 2, RET_DK)
    layers = []
    for layer in range(DEPTH):
        i = layer // 2
        if layer % 2 == 0:
            mw = prep_even(dict(
                w_in_t=fw["w_in_even"][i], w_uq=fw["mla_w_uq"][i], w_ukv=fw["mla_w_ukv"][i], w_out=fw["w_out_even"][i],
                mix_norm=fw["mix_norm_even"][i], q_norm=fw["mla_q_norm"][i], kv_norm=fw["mla_kv_norm"][i],
                q_head_norm=fw["mla_q_head_norm"][i], k_head_norm=fw["mla_k_head_norm"][i],
                theta_fwd=fw["ret_theta_fwd"][i], theta_bwd=fw["ret_theta_bwd"][i],
                ret_out_norm=fw["ret_out_norm"][i]))
        else:
            mw = prep_odd(dict(
                w_in_t=fw["w_in_odd"][i], w_gate_fwd=fw["gla_w_gate_fwd"][i], b_gate_fwd=fw["gla_b_gate_fwd"][i],
                w_gate_bwd=fw["gla_w_gate_bwd"][i], b_gate_bwd=fw["gla_b_gate_bwd"][i],
                gla_out_norm=fw["gla_out_norm"][i], w_out=fw["w_out_odd"][i], mix_norm=fw["mix_norm_odd"][i]))
        fwt = prep_ffn(dict(norm=fw["ffn_norm"][layer], w_up_t=fw["ffn_w_up"][layer], conv_w=fw["ffn_conv_w"][layer],
                            conv_b=fw["ffn_conv_b"][layer], w_down=fw["ffn_w_down"][layer]))
        layers.append((mw, fwt))

    saved = []
    for layer, (mw, fwt) in enumerate(layers):
        if layer % 2 == 0:
            x, sm = even_fwd(x, mw, tabs_mla, tabs_ret, f"l{layer}_mix")
        else:
            x, sm = odd_fwd(x, mw, f"l{layer}_mix")
        x, sf = ffn_fwd(x, fwt, f"l{layer}_ffn")
        saved.append((sm, sf))

    dx, loss = loss_head(x, target, name="loss_head")

    per_layer = [None] * DEPTH
    for layer in reversed(range(DEPTH)):
        mw, fwt = layers[layer]
        sm, sf = saved[layer]
        dx, gf = ffn_bwd(dx, sf, fwt, f"l{layer}_ffn")
        if layer % 2 == 0:
            dx, gm = even_bwd(dx, sm, mw, tabs_mla, tabs_ret, f"l{layer}_mix")
            gm = unprep_even_grads(gm, fw["ret_theta_fwd"][layer // 2], fw["ret_theta_bwd"][layer // 2])
        else:
            dx, gm = odd_bwd(dx, sm, mw, f"l{layer}_mix")
            gm = unprep_odd_grads(gm)
        per_layer[layer] = (gm, unprep_ffn_grads(gf))

    ev = [per_layer[l][0] for l in range(0, DEPTH, 2)]
    od = [per_layer[l][0] for l in range(1, DEPTH, 2)]
    ff = [per_layer[l][1] for l in range(DEPTH)]
    st = lambda lst, key: jnp.stack([g[key] for g in lst])
    grads = {
        "mix_norm_even": st(ev, "mix_norm"), "w_in_even": st(ev, "w_in_t"), "mla_q_norm": st(ev, "q_norm"),
        "mla_kv_norm": st(ev, "kv_norm"), "mla_w_uq": st(ev, "w_uq"), "mla_w_ukv": st(ev, "w_ukv"),
        "mla_q_head_norm": st(ev, "q_head_norm"), "mla_k_head_norm": st(ev, "k_head_norm"),
        "ret_theta_fwd": st(ev, "theta_fwd"), "ret_theta_bwd": st(ev, "theta_bwd"),
        "ret_out_norm": st(ev, "ret_out_norm"), "w_out_even": st(ev, "w_out"),
        "mix_norm_odd": st(od, "mix_norm"), "w_in_odd": st(od, "w_in_t"), "gla_w_gate_fwd": st(od, "w_gate_fwd"),
        "gla_b_gate_fwd": st(od, "b_gate_fwd"), "gla_w_gate_bwd": st(od, "w_gate_bwd"),
        "gla_b_gate_bwd": st(od, "b_gate_bwd"), "gla_out_norm": st(od, "gla_out_norm"), "w_out_odd": st(od, "w_out"),
        "ffn_norm": st(ff, "norm"), "ffn_w_up": st(ff, "w_up_t"), "ffn_conv_w": st(ff, "conv_w"),
        "ffn_conv_b": st(ff, "conv_b"), "ffn_w_down": st(ff, "w_down"),
    }
    return loss, dx, grads


def kernel(x, positions, mix_norm_even, w_in_even, mla_q_norm, mla_kv_norm, mla_w_uq, mla_w_ukv, mla_q_head_norm, mla_k_head_norm, ret_theta_fwd, ret_theta_bwd, ret_out_norm, w_out_even, mix_norm_odd, w_in_odd, gla_w_gate_fwd, gla_b_gate_fwd, gla_w_gate_bwd, gla_b_gate_bwd, gla_out_norm, w_out_odd, ffn_norm, ffn_w_up, ffn_conv_w, ffn_conv_b, ffn_w_down, loss_target, m_mix_norm_even, m_w_in_even, m_mla_q_norm, m_mla_kv_norm, m_mla_w_uq, m_mla_w_ukv, m_mla_q_head_norm, m_mla_k_head_norm, m_ret_theta_fwd, m_ret_theta_bwd, m_ret_out_norm, m_w_out_even, m_mix_norm_odd, m_w_in_odd, m_gla_w_gate_fwd, m_gla_b_gate_fwd, m_gla_w_gate_bwd, m_gla_b_gate_bwd, m_gla_out_norm, m_w_out_odd, m_ffn_norm, m_ffn_w_up, m_ffn_conv_w, m_ffn_conv_b, m_ffn_w_down, v_mix_norm_even, v_w_in_even, v_mla_q_norm, v_mla_kv_norm, v_mla_w_uq, v_mla_w_ukv, v_mla_q_head_norm, v_mla_k_head_norm, v_ret_theta_fwd, v_ret_theta_bwd, v_ret_out_norm, v_w_out_even, v_mix_norm_odd, v_w_in_odd, v_gla_w_gate_fwd, v_gla_b_gate_fwd, v_gla_w_gate_bwd, v_gla_b_gate_bwd, v_gla_out_norm, v_w_out_odd, v_ffn_norm, v_ffn_w_up, v_ffn_conv_w, v_ffn_conv_b, v_ffn_w_down):
    a = dict(locals())
    wts = {n: a[n] for n in WEIGHT_NAMES}
    local_shapes = {n: tuple(wts[n].shape) for n in WEIGHT_NAMES}

    gathered = all_gather_blocks(pack_gather_block(wts, local_shapes))
    fw = unpack_gathered(gathered, local_shapes)
    for n in REPLICATED:
        fw[n] = wts[n]

    loss, grad_x, grads = local_step(x[0], positions, loss_target[0], fw)

    mine, other = pack_grad_blocks(grads, local_shapes)
    chip_part = pair_add(mine, pair_exchange(other))
    g_slab = sum_slots(chip_exchange(chip_part))
    n_big = sum(_row_counts(local_shapes))
    ms = {n: a["m_" + n] for n in WEIGHT_NAMES}
    vs = {n: a["v_" + n] for n in WEIGHT_NAMES}

    g_out = unpack_rows(g_slab[:n_big], local_shapes)
    d_out, m_out, v_out = {}, {}, {}
    for n, _ in ROW_FORM:
        loc = local_shapes[n]
        two_d = lambda t: t.reshape(-1, loc[-1])
        d, m, v = adamw(two_d(wts[n]), two_d(g_out[n]), two_d(ms[n]), two_d(vs[n]), name=f"adamw_{n}")
        d_out[n], m_out[n], v_out[n] = d.reshape(loc), m.reshape(loc), v.reshape(loc)
    g_tail = g_slab[n_big:]
    d_tail, m_tail, v_tail = adamw(pack_tail(wts, local_shapes), g_tail, pack_tail(ms, local_shapes),
                                   pack_tail(vs, local_shapes), name="adamw_small")
    g_out.update(unpack_tail(g_tail, local_shapes))
    d_out.update(unpack_tail(d_tail, local_shapes))
    m_out.update(unpack_tail(m_tail, local_shapes))
    v_out.update(unpack_tail(v_tail, local_shapes))
    total = lax.psum(loss[0, 0], MESH_AXES)
    return (total, grad_x[None], *[g_out[n] for n in WEIGHT_NAMES], *[d_out[n] for n in WEIGHT_NAMES],
            *[m_out[n] for n in WEIGHT_NAMES], *[v_out[n] for n in WEIGHT_NAMES])
```

```python
import math

import jax
import jax.numpy as jnp
from jax import lax
from jax.experimental import pallas as pl
from jax.experimental.pallas import tpu as pltpu

F32 = jnp.float32
BF16 = jnp.bfloat16

D_MODEL = 1024
DEPTH = 4
N_DEV = 8
MESH_AXES = ("x", "y", "c")

MLA_HEADS = 8
MLA_Q_RANK = 384
MLA_KV_RANK = 256
MLA_NOPE = 64
MLA_ROPE = 32
MLA_V = 64
MLA_QK = MLA_NOPE + MLA_ROPE
RET_HEADS = 8
RET_DK = 64
RET_DV = 64
RET_CHUNK = 128
GLA_HEADS = 4
GLA_DK = 128
GLA_DV = 256
GLA_GATE_RANK = 16
GLA_TAU = 16.0
GLA_CHUNK = 64
D_FF = 2816
ROPE_THETA = 10000.0
EPS = 1e-6

ADAM_LR = 0.001
ADAM_B1 = 0.9
ADAM_B2 = 0.999
ADAM_EPS = 1e-08
ADAM_WD = 0.01
ADAM_STEP = 10

LANE = 128
SUBLANE = 8
ROW_TILE = 512
VMEM_LIMIT = 56 * 1024 * 1024
WEIGHT_TILE_BYTES = 8 * 1024 * 1024

EV_CQ, EV_CKV, EV_KR, EV_RQ, EV_RK, EV_RV, EV_RG, EV_IN = 0, 384, 640, 768, 1280, 1792, 2304, 2816
OD_Q, OD_K, OD_V, OD_R, OD_GA, OD_IN = 0, 512, 1024, 2048, 3072, 3200
N_GROUPS = 4


def _cparams(sem):
    return pltpu.CompilerParams(dimension_semantics=sem, vmem_limit_bytes=VMEM_LIMIT)


def _dot(a, b):
    return jnp.dot(a.astype(BF16), b.astype(BF16), preferred_element_type=F32)


def _dot_nt(a, b):
    return lax.dot_general(a.astype(BF16), b.astype(BF16), (((1,), (1,)), ((), ())), preferred_element_type=F32)


def _dot_tn(a, b):
    return lax.dot_general(a.astype(BF16), b.astype(BF16), (((0,), (0,)), ((), ())), preferred_element_type=F32)


def _sigmoid(x):
    return 1.0 / (1.0 + jnp.exp(-x))


def _col_tile(k, n, itemsize=2):
    best = LANE
    for t in range(LANE, n + 1, LANE):
        if n % t == 0 and k * t * itemsize <= WEIGHT_TILE_BYTES:
            best = t
    return best if n % LANE == 0 else n


def _row_tile(m):
    return min(ROW_TILE, m)


def mm_nn(a, b, res=None, out_dtype=F32, name="mm_nn"):
    m, k = a.shape
    n = b.shape[1]
    tm, tn = _row_tile(m), _col_tile(k, n)

    def body(*refs):
        if res is None:
            a_ref, b_ref, o_ref = refs
        else:
            a_ref, b_ref, r_ref, o_ref = refs
        acc = _dot(a_ref[...], b_ref[...])
        if res is not None:
            acc = acc + r_ref[...].astype(F32)
        o_ref[...] = acc.astype(out_dtype)

    in_specs = [pl.BlockSpec((tm, k), lambda j, i: (i, 0)), pl.BlockSpec((k, tn), lambda j, i: (0, j))]
    args = [a, b]
    if res is not None:
        in_specs.append(pl.BlockSpec((tm, tn), lambda j, i: (i, j)))
        args.append(res)
    return pl.pallas_call(
        body, name=name, grid=(n // tn, m // tm), in_specs=in_specs,
        out_specs=pl.BlockSpec((tm, tn), lambda j, i: (i, j)),
        out_shape=jax.ShapeDtypeStruct((m, n), out_dtype),
        compiler_params=_cparams(("parallel", "parallel")),
    )(*args)


def mm_tn(a, b, out_dtype=F32, name="mm_tn"):
    t, k = a.shape
    n = b.shape[1]
    tt = min(2 * ROW_TILE, t)
    tk = k if k <= 1024 else _col_tile(1024, k, 4)
    tn = n if n <= 1024 else _col_tile(1024, n, 4)

    def body(a_ref, b_ref, o_ref, acc_s):
        s = pl.program_id(2)

        @pl.when(s == 0)
        def _():
            acc_s[...] = jnp.zeros_like(acc_s)
        acc_s[...] += _dot_tn(a_ref[...], b_ref[...])

        @pl.when(s == pl.num_programs(2) - 1)
        def _():
            o_ref[...] = acc_s[...].astype(out_dtype)

    return pl.pallas_call(
        body, name=name, grid=(k // tk, n // tn, t // tt),
        in_specs=[pl.BlockSpec((tt, tk), lambda i, j, s: (s, i)), pl.BlockSpec((tt, tn), lambda i, j, s: (s, j))],
        out_specs=pl.BlockSpec((tk, tn), lambda i, j, s: (i, j)),
        out_shape=jax.ShapeDtypeStruct((k, n), out_dtype),
        scratch_shapes=[pltpu.VMEM((tk, tn), F32)],
        compiler_params=_cparams(("parallel", "parallel", "arbitrary")),
    )(a, b)


def rmsnorm_fwd(x, g, name="rmsnorm_fwd"):
    t, d = x.shape
    tm = _row_tile(t)

    def body(x_ref, g_ref, h_ref):
        xv = x_ref[...]
        r = lax.rsqrt(jnp.mean(xv * xv, axis=-1, keepdims=True) + EPS)
        h_ref[...] = (xv * r * g_ref[...]).astype(BF16)

    return pl.pallas_call(
        body, name=name, grid=(t // tm,),
        in_specs=[pl.BlockSpec((tm, d), lambda i: (i, 0)), pl.BlockSpec((1, d), lambda i: (0, 0))],
        out_specs=pl.BlockSpec((tm, d), lambda i: (i, 0)),
        out_shape=jax.ShapeDtypeStruct((t, d), BF16),
        compiler_params=_cparams(("parallel",)),
    )(x, g.reshape(1, d))


def rmsnorm_bwd(x, g, dh, dres, name="rmsnorm_bwd"):
    t, d = x.shape
    tm = _row_tile(t)

    def body(x_ref, g_ref, dh_ref, dres_ref, dx_ref, dg_ref):
        @pl.when(pl.program_id(0) == 0)
        def _():
            dg_ref[...] = jnp.zeros_like(dg_ref)
        xv = x_ref[...]
        r = lax.rsqrt(jnp.mean(xv * xv, axis=-1, keepdims=True) + EPS)
        xh = xv * r
        dhv = dh_ref[...]
        dg_ref[...] += jnp.sum(dhv * xh, axis=0, keepdims=True)
        dxh = dhv * g_ref[...]
        dx_ref[...] = dres_ref[...] + r * (dxh - xh * jnp.mean(dxh * xh, axis=-1, keepdims=True))

    row = pl.BlockSpec((tm, d), lambda i: (i, 0))
    vec = pl.BlockSpec((1, d), lambda i: (0, 0))
    return pl.pallas_call(
        body, name=name, grid=(t // tm,),
        in_specs=[row, vec, row, row], out_specs=[row, vec],
        out_shape=[jax.ShapeDtypeStruct((t, d), F32), jax.ShapeDtypeStruct((1, d), F32)],
        compiler_params=_cparams(("arbitrary",)),
    )(x, g.reshape(1, d), dh, dres)


def _rope_apply(x, cos, s1, s2, half):
    return x * cos + pltpu.roll(x, LANE - half, 1) * s1 + pltpu.roll(x, half, 1) * s2


def _rope_transpose(dy, cos, s1, s2, half):
    return dy * cos + pltpu.roll(dy * s1, half, 1) + pltpu.roll(dy * s2, LANE - half, 1)


def rope_tables(positions, lane_start, half, period):
    pos = positions.reshape(-1).astype(F32)
    inv = ROPE_THETA ** (-jnp.arange(half, dtype=F32) / half)
    ang = pos[:, None] * inv[None, :]
    cos, sin = jnp.cos(ang), jnp.sin(ang)
    t = pos.shape[0]
    pre = lane_start
    post = period - lane_start - 2 * half
    ones = lambda n: jnp.ones((t, n), F32)
    zeros = lambda n: jnp.zeros((t, n), F32)
    c = jnp.concatenate([ones(pre), cos, cos, ones(post)], axis=1)
    a = jnp.concatenate([zeros(pre), -sin, zeros(half), zeros(post)], axis=1)
    b = jnp.concatenate([zeros(pre), zeros(half), sin, zeros(post)], axis=1)
    rep = LANE // period
    return tuple(jnp.tile(v, (1, rep)) for v in (c, a, b))


def _mla_forward_tile(p, cos, s1, s2, qn_g, kvn_g, wuq, wk, wv, qhn, khn):
    cq = p[:, EV_CQ:EV_CKV]
    ckv = p[:, EV_CKV:EV_KR]
    kr = p[:, EV_KR:EV_RQ]
    rq = lax.rsqrt(jnp.mean(cq * cq, axis=-1, keepdims=True) + EPS)
    rkv = lax.rsqrt(jnp.mean(ckv * ckv, axis=-1, keepdims=True) + EPS)
    qn = cq * rq * qn_g
    kvn = ckv * rkv * kvn_g
    q_raw = _dot(qn, wuq)
    k_raw = _dot(kvn, wk)
    v = _dot(kvn, wv)
    krp = pltpu.roll(kr, MLA_NOPE, 1)
    return cq, ckv, rq, rkv, qn, kvn, q_raw, k_raw, v, krp


def _head_norm(xh, g):
    r = lax.rsqrt(jnp.sum(xh * xh, axis=-1, keepdims=True) * (1.0 / MLA_QK) + EPS)
    return xh * r * g, r


def mla_prep_fwd(p, tabs, qn_g, kvn_g, wuq, wk, wv, qhn, khn, name="mla_prep_fwd"):
    t = p.shape[0]
    tm = _row_tile(t)
    hw = MLA_HEADS * LANE

    def body(p_ref, c_ref, s1_ref, s2_ref, qn_ref, kvn_ref, wuq_ref, wk_ref, wv_ref, qhn_ref, khn_ref,
             q_out, k_out, v_out):
        cos, s1, s2 = c_ref[...], s1_ref[...], s2_ref[...]
        (_, _, _, _, _, _, q_raw, k_raw, v, krp) = _mla_forward_tile(
            p_ref[...].astype(F32), cos, s1, s2, qn_ref[...], kvn_ref[...], wuq_ref[...], wk_ref[...], wv_ref[...],
            qhn_ref[...], khn_ref[...])
        v_out[...] = v.astype(BF16)
        for h in range(MLA_HEADS):
            sl = slice(h * LANE, (h + 1) * LANE)
            qh, _ = _head_norm(q_raw[:, sl], qhn_ref[...])
            kh, _ = _head_norm(k_raw[:, sl] + krp, khn_ref[...])
            q_out[:, sl] = (_rope_apply(qh, cos, s1, s2, MLA_ROPE // 2) * ATTN_QSCALE).astype(BF16)
            k_out[:, sl] = _rope_apply(kh, cos, s1, s2, MLA_ROPE // 2).astype(BF16)

    row = lambda w: pl.BlockSpec((tm, w), lambda i: (i, 0))
    full = lambda a: pl.BlockSpec(a.shape, lambda i: (0,) * a.ndim)
    ws = [qn_g, kvn_g, wuq, wk, wv, qhn, khn]
    return pl.pallas_call(
        body, name=name, grid=(t // tm,),
        in_specs=[row(EV_RQ), row(LANE), row(LANE), row(LANE)] + [full(w) for w in ws],
        out_specs=[row(hw)] * 3,
        out_shape=[jax.ShapeDtypeStruct((t, hw), BF16)] * 3,
        compiler_params=_cparams(("parallel",)),
    )(p, *tabs, *ws)


def mla_prep_bwd(p, tabs, qn_g, kvn_g, wuq, wk, wv, wuq_t, wk_t, wv_t, qhn, khn, dq, dk, dv,
                 name="mla_prep_bwd"):
    t = p.shape[0]
    tm = _row_tile(t)
    hw = MLA_HEADS * LANE

    def body(p_ref, c_ref, s1_ref, s2_ref, qn_ref, kvn_ref, wuq_ref, wk_ref, wv_ref, wuqt_ref, wkt_ref, wvt_ref,
             qhn_ref, khn_ref, dq_ref, dk_ref, dv_ref,
             dp_ref, dwuq_ref, dwk_ref, dwv_ref, dqn_ref, dkvn_ref, dqhn_ref, dkhn_ref, dqraw_s, dkraw_s):
        @pl.when(pl.program_id(0) == 0)
        def _():
            for r in (dwuq_ref, dwk_ref, dwv_ref, dqn_ref, dkvn_ref, dqhn_ref, dkhn_ref):
                r[...] = jnp.zeros_like(r)
        cos, s1, s2 = c_ref[...], s1_ref[...], s2_ref[...]
        qhn_v, khn_v = qhn_ref[...], khn_ref[...]
        (cq, ckv, rq, rkv, qn, kvn, q_raw, k_raw, _, krp) = _mla_forward_tile(
            p_ref[...].astype(F32), cos, s1, s2, qn_ref[...], kvn_ref[...], wuq_ref[...], wk_ref[...], wv_ref[...],
            qhn_v, khn_v)
        half = MLA_ROPE // 2
        dkr_sum = jnp.zeros((tm, LANE), F32)
        dqhn_acc = jnp.zeros((1, LANE), F32)
        dkhn_acc = jnp.zeros((1, LANE), F32)
        for h in range(MLA_HEADS):
            sl = slice(h * LANE, (h + 1) * LANE)
            xq = q_raw[:, sl]
            _, r = _head_norm(xq, qhn_v)
            xh = xq * r
            dy = _rope_transpose(dq_ref[:, sl] * ATTN_SCALE, cos, s1, s2, half)
            dqhn_acc = dqhn_acc + jnp.sum(dy * xh, axis=0, keepdims=True)
            dxh = dy * qhn_v
            dqraw_s[:, sl] = r * (dxh - xh * (jnp.sum(dxh * xh, axis=-1, keepdims=True) * (1.0 / MLA_QK)))
            xk = k_raw[:, sl] + krp
            _, r = _head_norm(xk, khn_v)
            xh = xk * r
            dy = _rope_transpose(dk_ref[:, sl] * math.log(2.0), cos, s1, s2, half)
            dkhn_acc = dkhn_acc + jnp.sum(dy * xh, axis=0, keepdims=True)
            dxh = dy * khn_v
            dxk = r * (dxh - xh * (jnp.sum(dxh * xh, axis=-1, keepdims=True) * (1.0 / MLA_QK)))
            dkraw_s[:, sl] = dxk
            dkr_sum = dkr_sum + dxk
        dqhn_ref[...] += dqhn_acc
        dkhn_ref[...] += dkhn_acc
        dq_raw = dqraw_s[...]
        dk_raw = dkraw_s[...]
        dvv = dv_ref[...]
        dwuq_ref[...] += _dot_tn(qn, dq_raw)
        dwk_ref[...] += _dot_tn(kvn, dk_raw)
        dwv_ref[...] += _dot_tn(kvn, dvv)
        dqn = _dot(dq_raw, wuqt_ref[...])
        dkvn = _dot(dk_raw, wkt_ref[...]) + _dot(dvv, wvt_ref[...])
        xh = cq * rq
        dqn_ref[...] += jnp.sum(dqn * xh, axis=0, keepdims=True)
        dxh = dqn * qn_ref[...]
        dcq = rq * (dxh - xh * jnp.mean(dxh * xh, axis=-1, keepdims=True))
        dp_ref[:, EV_CQ:EV_CKV] = dcq.astype(BF16)
        xh = ckv * rkv
        dkvn_ref[...] += jnp.sum(dkvn * xh, axis=0, keepdims=True)
        dxh = dkvn * kvn_ref[...]
        dckv = rkv * (dxh - xh * jnp.mean(dxh * xh, axis=-1, keepdims=True))
        dp_ref[:, EV_CKV:EV_KR] = dckv.astype(BF16)
        lane = lax.broadcasted_iota(jnp.int32, (tm, LANE), 1)
        dkr = jnp.where(lane < MLA_ROPE, pltpu.roll(dkr_sum, LANE - MLA_NOPE, 1), 0.0)
        dp_ref[:, EV_KR:EV_RQ] = dkr.astype(BF16)

    row = lambda w: pl.BlockSpec((tm, w), lambda i: (i, 0))
    full = lambda a: pl.BlockSpec(a.shape, lambda i: (0,) * a.ndim)
    ws = [qn_g, kvn_g, wuq, wk, wv, wuq_t, wk_t, wv_t, qhn, khn]
    outs = [jax.ShapeDtypeStruct((t, EV_RQ), BF16), jax.ShapeDtypeStruct(wuq.shape, F32),
            jax.ShapeDtypeStruct(wk.shape, F32), jax.ShapeDtypeStruct(wv.shape, F32),
            jax.ShapeDtypeStruct(qn_g.shape, F32), jax.ShapeDtypeStruct(kvn_g.shape, F32),
            jax.ShapeDtypeStruct(qhn.shape, F32), jax.ShapeDtypeStruct(khn.shape, F32)]
    return pl.pallas_call(
        body, name=name, grid=(t // tm,),
        in_specs=[row(EV_RQ), row(LANE), row(LANE), row(LANE)] + [full(w) for w in ws] + [row(hw)] * 3,
        out_specs=[row(EV_RQ)] + [full(o) for o in outs[1:]],
        out_shape=outs,
        scratch_shapes=[pltpu.VMEM((tm, hw), F32), pltpu.VMEM((tm, hw), F32)],
        compiler_params=_cparams(("arbitrary",)),
    )(p, *tabs, *ws, dq, dk, dv)


ATTN_SCALE = MLA_QK ** -0.5
ATTN_QSCALE = ATTN_SCALE * math.log2(math.e)
ATTN_FWD_TQ, ATTN_FWD_TK = 512, 8192
ATTN_BWD_TQ, ATTN_BWD_TK = 256, 4096


def attn_fwd(q, k, v, name="attn_fwd"):
    t = q.shape[0]
    tq, tk = min(ATTN_FWD_TQ, _row_tile(t)), min(ATTN_FWD_TK, t)
    nh = MLA_HEADS

    def body(q_ref, k_ref, v_ref, o_ref, lse_ref, m_s, l_s, acc_s):
        j = pl.program_id(2)

        @pl.when(j == 0)
        def _():
            m_s[...] = jnp.full_like(m_s, -jnp.inf)
            l_s[...] = jnp.zeros_like(l_s)
            acc_s[...] = jnp.zeros_like(acc_s)

        s = _dot_nt(q_ref[...], k_ref[...])
        m_old = m_s[...]
        m_new = jnp.maximum(m_old, jnp.max(s, axis=-1, keepdims=True))
        pr = jnp.exp2(s - m_new)
        alpha = jnp.exp2(m_old - m_new)
        l_s[...] = alpha * l_s[...] + jnp.sum(pr, axis=-1, keepdims=True)
        acc_s[...] = alpha * acc_s[...] + _dot(pr, v_ref[...])
        m_s[...] = m_new

        @pl.when(j == pl.num_programs(2) - 1)
        def _():
            o_ref[...] = acc_s[...] / l_s[...]
            lse_ref[...] = m_s[...] + jnp.log2(l_s[...])

    return pl.pallas_call(
        body, name=name, grid=(nh, t // tq, t // tk),
        in_specs=[pl.BlockSpec((tq, LANE), lambda h, i, j: (i, h)),
                  pl.BlockSpec((tk, LANE), lambda h, i, j: (j, h)),
                  pl.BlockSpec((tk, LANE), lambda h, i, j: (j, h))],
        out_specs=[pl.BlockSpec((tq, LANE), lambda h, i, j: (i, h)),
                   pl.BlockSpec((None, tq, 1), lambda h, i, j: (h, i, 0))],
        out_shape=[jax.ShapeDtypeStruct((t, nh * LANE), F32), jax.ShapeDtypeStruct((nh, t, 1), F32)],
        scratch_shapes=[pltpu.VMEM((tq, 1), F32), pltpu.VMEM((tq, 1), F32), pltpu.VMEM((tq, LANE), F32)],
        compiler_params=_cparams(("parallel", "parallel", "arbitrary")),
    )(q, k, v)


def attn_bwd(q, k, v, o, lse, do, name="attn_bwd"):
    t = q.shape[0]
    tq, tk = min(ATTN_BWD_TQ, _row_tile(t)), min(ATTN_BWD_TK, t)
    nh = MLA_HEADS
    nq = t // tq

    def body(q_ref, k_ref, v_ref, o_ref, lse_ref, do_ref, dq_ref, dk_ref, dv_ref):
        kj, qi = pl.program_id(1), pl.program_id(2)

        @pl.when(qi == 0)
        def _():
            dk_ref[...] = jnp.zeros_like(dk_ref)
            dv_ref[...] = jnp.zeros_like(dv_ref)

        qv, kv, vv, dov = q_ref[...], k_ref[...], v_ref[...], do_ref[...]
        s = _dot_nt(qv, kv)
        pr = jnp.exp2(s - lse_ref[...])
        dp = _dot_nt(dov, vv)
        delta = jnp.sum(dov * o_ref[...], axis=-1, keepdims=True)
        ds = pr * (dp - delta)
        dv_ref[...] += _dot_tn(pr, dov)
        dk_ref[...] += _dot_tn(ds, qv)
        dq_tile = _dot(ds, kv)
        rows = pl.ds(pl.multiple_of(qi * tq, tq), tq)

        @pl.when(kj == 0)
        def _():
            dq_ref[rows, :] = dq_tile

        @pl.when(kj != 0)
        def _():
            dq_ref[rows, :] += dq_tile

    qspec = pl.BlockSpec((tq, LANE), lambda h, j, i: (i, h))
    kspec = pl.BlockSpec((tk, LANE), lambda h, j, i: (j, h))
    return pl.pallas_call(
        body, name=name, grid=(nh, t // tk, nq),
        in_specs=[qspec, kspec, kspec, qspec, pl.BlockSpec((None, tq, 1), lambda h, j, i: (h, i, 0)), qspec],
        out_specs=[pl.BlockSpec((t, LANE), lambda h, j, i: (0, h)), kspec, kspec],
        out_shape=[jax.ShapeDtypeStruct((t, nh * LANE), F32)] * 3,
        compiler_params=_cparams(("parallel", "arbitrary", "arbitrary")),
    )(q, k, v, o, lse, do)


def _scan_consts(c, reverse, inclusive):
    ii = lax.broadcasted_iota(jnp.int32, (c, c), 0)
    jj = lax.broadcasted_iota(jnp.int32, (c, c), 1)
    if reverse:
        incl = jj >= ii
        mask = incl if inclusive else jj > ii
    else:
        incl = jj <= ii
        mask = incl if inclusive else jj < ii
    mid = (c - 1 - c // 2) if reverse else c // 2
    incl_t = (jj <= ii) if reverse else (jj >= ii)
    return incl.astype(F32), incl_t.astype(F32), mask.astype(F32), mid


def _dot_split(a01, x):
    hi = x.astype(BF16)
    lo = (x - hi.astype(F32)).astype(BF16)
    a = a01.astype(BF16)
    return jnp.dot(a, hi, preferred_element_type=F32) + jnp.dot(a, lo, preferred_element_type=F32)


def _sub_masks(sub, dvg, u):
    if sub == 1:
        return None, None
    kl = lax.broadcasted_iota(jnp.int32, (1, LANE), 1)
    vl = lax.broadcasted_iota(jnp.int32, (1, dvg), 1)
    kw, vw = LANE // sub, dvg // sub
    km = (kl >= u * kw) & (kl < (u + 1) * kw)
    vm = (vl >= u * vw) & (vl < (u + 1) * vw)
    return km.astype(F32), vm.astype(F32)


def _block_incl(r, c, reverse, transposed):
    shift = c.bit_length() - 1
    ii = lax.broadcasted_iota(jnp.int32, (r, r), 0)
    jj = lax.broadcasted_iota(jnp.int32, (r, r), 1)
    same = lax.shift_right_logical(ii, shift) == lax.shift_right_logical(jj, shift)
    lower = (jj <= ii) if (reverse == transposed) else (jj >= ii)
    return (same & lower).astype(F32)


def _scan_chunk_fwd(b, la, mid):
    row = lax.broadcasted_iota(jnp.int32, b.shape, 0)
    bm = jnp.sum(jnp.where(row == mid, b, 0.0), axis=0, keepdims=True)
    tot = jnp.sum(la, axis=0, keepdims=True)
    e_qc = jnp.exp(b - bm)
    e_kc = jnp.exp(bm - b)
    e_qe = jnp.exp(b)
    e_kd = jnp.exp(tot - b)
    return e_qc, e_kc, e_qe, e_kd


def scan_fwd(q_arr, k_arr, v_arr, la_arr, *, qcb, kcb, vcb, lacb, la_row, chunk, dvg, sub, reverse, inclusive,
             qscale, kscale, rope=None, name="scan_fwd"):
    t = q_arr.shape[0]
    r = _row_tile(t)
    nb, nc = t // r, r // chunk
    c = chunk
    rb = (lambda j: nb - 1 - j) if reverse else (lambda j: j)
    order = list(range(nc))[::-1] if reverse else list(range(nc))
    half = RET_DK // 2

    def body(*refs):
        if rope is None:
            q_ref, k_ref, v_ref, la_ref, o_ref, st_ref, s_s = refs
        else:
            q_ref, k_ref, v_ref, la_ref, c_ref, s1_ref, s2_ref, o_ref, st_ref, s_s = refs

        @pl.when(pl.program_id(1) == 0)
        def _():
            s_s[...] = jnp.zeros_like(s_s)

        incl, _, mask, mid = _scan_consts(c, reverse, inclusive)
        for ci in order:
            rows = slice(ci * c, (ci + 1) * c)
            qv = q_ref[rows, :].astype(F32) * qscale
            kv = k_ref[rows, :].astype(F32) * kscale
            if rope is not None:
                cs, a1, a2 = c_ref[rows, :], s1_ref[rows, :], s2_ref[rows, :]
                qv = _rope_apply(qv, cs, a1, a2, half)
                kv = _rope_apply(kv, cs, a1, a2, half)
            la = jnp.broadcast_to(la_ref[...], (c, LANE)) if la_row else la_ref[rows, :]
            vv = v_ref[rows, :].astype(F32)
            e_qc, e_kc, e_qe, e_kd = _scan_chunk_fwd(_dot_split(incl, la), la, mid)
            qc, kc, qe, kd = qv * e_qc, kv * e_kc, qv * e_qe, kv * e_kd
            sg = s_s[...]
            st_ref[ci] = sg
            acc = None
            for u in range(sub):
                mu, vmu = _sub_masks(sub, dvg, u)
                qcu = qc if mu is None else qc * mu
                qeu = qe if mu is None else qe * mu
                a = _dot_nt(qcu, kc) * mask
                ou = _dot(a, vv) + _dot_nt(qeu, sg)
                ou = ou if vmu is None else ou * vmu
                acc = ou if acc is None else acc + ou
            o_ref[rows, :] = acc
            decay = jnp.exp(jnp.sum(la, axis=0, keepdims=True))
            s_s[...] = decay * sg + _dot_tn(vv, kd)

    specs = [pl.BlockSpec((r, LANE), lambda g, j: (rb(j), qcb + g)),
             pl.BlockSpec((r, LANE), lambda g, j: (rb(j), kcb + g)),
             pl.BlockSpec((r, dvg), lambda g, j: (rb(j), vcb + g)),
             pl.BlockSpec((1, LANE), lambda g, j: (0, lacb + g)) if la_row
             else pl.BlockSpec((r, LANE), lambda g, j: (rb(j), lacb + g))]
    args = [q_arr, k_arr, v_arr, la_arr]
    if rope is not None:
        specs += [pl.BlockSpec((r, LANE), lambda g, j: (rb(j), 0))] * 3
        args += list(rope)
    return pl.pallas_call(
        body, name=name, grid=(N_GROUPS, nb), in_specs=specs,
        out_specs=[pl.BlockSpec((r, dvg), lambda g, j: (rb(j), g)),
                   pl.BlockSpec((nc, dvg, LANE), lambda g, j: (rb(j), g, 0))],
        out_shape=[jax.ShapeDtypeStruct((t, N_GROUPS * dvg), F32),
                   jax.ShapeDtypeStruct((t // c, N_GROUPS * dvg, LANE), F32)],
        scratch_shapes=[pltpu.VMEM((dvg, LANE), F32)],
        compiler_params=_cparams(("parallel", "arbitrary")),
    )(*args)


def scan_bwd(q_arr, k_arr, v_arr, la_arr, st_arr, do_arr, prev, *, qcb, kcb, vcb, lacb, la_row, chunk, dvg, sub,
             reverse, inclusive, qscale, kscale, rope=None, name="scan_bwd"):
    t = q_arr.shape[0]
    r = _row_tile(t)
    nb, nc = t // r, r // chunk
    c = chunk
    rb = (lambda j: j) if reverse else (lambda j: nb - 1 - j)
    order = list(range(nc)) if reverse else list(range(nc))[::-1]
    half = RET_DK // 2
    n_in = 6 + (3 if rope is not None else 0) + (3 if prev is not None else 0)
    gdt = F32 if prev is None else BF16

    def body(*refs):
        ins, outs = refs[:n_in], refs[n_in:]
        q_ref, k_ref, v_ref, la_ref, st_ref, do_ref = ins[:6]
        pos = 6
        if rope is not None:
            c_ref, s1_ref, s2_ref = ins[pos:pos + 3]
            pos += 3
        if prev is not None:
            pq_ref, pk_ref, pv_ref = ins[pos:pos + 3]
        dq_ref, dk_ref, dv_ref, dla_ref, g_s = outs

        @pl.when(pl.program_id(1) == 0)
        def _():
            g_s[...] = jnp.zeros_like(g_s)
            if la_row:
                dla_ref[...] = jnp.zeros_like(dla_ref)

        incl, _, mask, mid = _scan_consts(c, reverse, inclusive)
        b_all = None if la_row else _dot_split(_block_incl(r, c, reverse, False), la_ref[...])
        pos = lax.broadcasted_iota(jnp.int32, (c, LANE), 0)
        cnt = ((c - pos) if reverse else (pos + 1)).astype(F32)
        dla_sum = jnp.zeros((1, LANE), F32)
        db_parts, dtot_parts = [None] * nc, [None] * nc
        for ci in order:
            rows = slice(ci * c, (ci + 1) * c)
            qv = q_ref[rows, :].astype(F32) * qscale
            kv = k_ref[rows, :].astype(F32) * kscale
            if rope is not None:
                cs, a1, a2 = c_ref[rows, :], s1_ref[rows, :], s2_ref[rows, :]
                qv = _rope_apply(qv, cs, a1, a2, half)
                kv = _rope_apply(kv, cs, a1, a2, half)
            la = jnp.broadcast_to(la_ref[...], (c, LANE)) if la_row else la_ref[rows, :]
            vv = v_ref[rows, :].astype(F32)
            dov = do_ref[rows, :]
            b = _dot_split(incl, la) if la_row else b_all[rows, :]
            e_qc, e_kc, e_qe, e_kd = _scan_chunk_fwd(b, la, mid)
            qc, kc, qe, kd = qv * e_qc, kv * e_kc, qv * e_qe, kv * e_kd
            sg = st_ref[ci]
            gn = g_s[...]
            dqc = jnp.zeros((c, LANE), F32)
            dkc = jnp.zeros((c, LANE), F32)
            dqe = jnp.zeros((c, LANE), F32)
            dvv = _dot_nt(kd, gn)
            ds_direct = jnp.zeros((dvg, LANE), F32)
            for u in range(sub):
                mu, vmu = _sub_masks(sub, dvg, u)
                qcu = qc if mu is None else qc * mu
                qeu = qe if mu is None else qe * mu
                dou = dov if vmu is None else dov * vmu
                a = _dot_nt(qcu, kc) * mask
                da = _dot_nt(dou, vv) * mask
                dvv = dvv + _dot_tn(a, dou)
                t1 = _dot(da, kc)
                dqc = dqc + (t1 if mu is None else t1 * mu)
                dkc = dkc + _dot_tn(da, qcu)
                t2 = _dot(dou, sg)
                dqe = dqe + (t2 if mu is None else t2 * mu)
                ds_direct = ds_direct + _dot_tn(dou, qeu)
            dkd = _dot(vv, gn)
            decay = jnp.exp(jnp.sum(la, axis=0, keepdims=True))
            dtot = jnp.sum(gn * sg, axis=0, keepdims=True) * decay + jnp.sum(dkd * kd, axis=0, keepdims=True)
            db = dqc * qc - dkc * kc + dqe * qe - dkd * kd
            if la_row:
                dla_sum = dla_sum + jnp.sum(db * cnt, axis=0, keepdims=True) + float(c) * dtot
            else:
                db_parts[ci] = db
                dtot_parts[ci] = jnp.broadcast_to(dtot, (c, LANE))
            dqv = dqc * e_qc + dqe * e_qe
            dkv = dkc * e_kc + dkd * e_kd
            if rope is not None:
                dqv = _rope_transpose(dqv, cs, a1, a2, half)
                dkv = _rope_transpose(dkv, cs, a1, a2, half)
            dqv = dqv * qscale
            dkv = dkv * kscale
            if prev is not None:
                dqv = dqv + pq_ref[rows, :]
                dkv = dkv + pk_ref[rows, :]
                dvv = dvv + pv_ref[rows, :]
            dq_ref[rows, :] = dqv.astype(gdt)
            dk_ref[rows, :] = dkv.astype(gdt)
            dv_ref[rows, :] = dvv.astype(gdt)
            g_s[...] = ds_direct + decay * gn
        if la_row:
            dla_ref[...] += dla_sum
        else:
            db_all = jnp.concatenate(db_parts, axis=0)
            dla_ref[...] = _dot_split(_block_incl(r, c, reverse, True), db_all) + jnp.concatenate(dtot_parts, axis=0)

    kblk = lambda cb: pl.BlockSpec((r, LANE), lambda g, j: (rb(j), cb + g))
    vblk = lambda cb: pl.BlockSpec((r, dvg), lambda g, j: (rb(j), cb + g))
    specs = [kblk(qcb), kblk(kcb), vblk(vcb),
             pl.BlockSpec((1, LANE), lambda g, j: (0, lacb + g)) if la_row else kblk(lacb),
             pl.BlockSpec((nc, dvg, LANE), lambda g, j: (rb(j), g, 0)), vblk(0)]
    args = [q_arr, k_arr, v_arr, la_arr, st_arr, do_arr]
    if rope is not None:
        specs += [pl.BlockSpec((r, LANE), lambda g, j: (rb(j), 0))] * 3
        args += list(rope)
    if prev is not None:
        specs += [kblk(0), kblk(0), vblk(0)]
        args += list(prev)
    wk = N_GROUPS * LANE
    outs = [jax.ShapeDtypeStruct((t, wk), gdt), jax.ShapeDtypeStruct((t, wk), gdt),
            jax.ShapeDtypeStruct((t, N_GROUPS * dvg), gdt),
            jax.ShapeDtypeStruct((1, wk) if la_row else (t, wk), F32)]
    return pl.pallas_call(
        body, name=name, grid=(N_GROUPS, nb), in_specs=specs,
        out_specs=[kblk(0), kblk(0), vblk(0),
                   pl.BlockSpec((1, LANE), lambda g, j: (0, g)) if la_row else kblk(0)],
        out_shape=outs,
        scratch_shapes=[pltpu.VMEM((dvg, LANE), F32)],
        compiler_params=_cparams(("parallel", "arbitrary")),
    )(*args)


def _seg_mean(x, seg):
    w = x.shape[1]
    if seg % LANE == 0:
        parts = []
        for s in range(0, w, seg):
            m = jnp.mean(x[:, s:s + seg], axis=-1, keepdims=True)
            parts.append(jnp.broadcast_to(m, (x.shape[0], seg)))
        return jnp.concatenate(parts, axis=1)
    shift = seg.bit_length() - 1
    ii = lax.shift_right_logical(lax.broadcasted_iota(jnp.int32, (w, w), 0), shift)
    jj = lax.shift_right_logical(lax.broadcasted_iota(jnp.int32, (w, w), 1), shift)
    e = (ii == jj).astype(BF16)
    hi = x.astype(BF16)
    lo = (x - hi.astype(F32)).astype(BF16)
    return (jnp.dot(hi, e, preferred_element_type=F32) + jnp.dot(lo, e, preferred_element_type=F32)) * (1.0 / seg)


def gated_norm_fwd(o_f, o_b, gate_arr, gcb, gn, seg, name="gated_norm_fwd"):
    t, w = o_f.shape
    tm = _row_tile(t)

    def body(of_ref, ob_ref, g_ref, gn_ref, y_ref):
        o = of_ref[...] + ob_ref[...]
        r = lax.rsqrt(_seg_mean(o * o, seg) + EPS)
        gt = g_ref[...].astype(F32)
        y_ref[...] = (gt * _sigmoid(gt) * (o * r * gn_ref[...])).astype(BF16)

    bw = max(seg, LANE)
    row = pl.BlockSpec((tm, bw), lambda j, i: (i, j))
    return pl.pallas_call(
        body, name=name, grid=(w // bw, t // tm),
        in_specs=[row, row, pl.BlockSpec((tm, bw), lambda j, i: (i, gcb + j)),
                  pl.BlockSpec((1, bw), lambda j, i: (0, j))],
        out_specs=row, out_shape=jax.ShapeDtypeStruct((t, w), BF16),
        compiler_params=_cparams(("parallel", "parallel")),
    )(o_f, o_b, gate_arr, gn.reshape(1, w))


def gated_norm_bwd(o_f, o_b, gate_arr, gcb, gn, seg, dy, name="gated_norm_bwd"):
    t, w = o_f.shape
    tm = _row_tile(t)

    def body(of_ref, ob_ref, g_ref, gn_ref, dy_ref, do_ref, dg_ref, dgn_ref):
        @pl.when(pl.program_id(1) == 0)
        def _():
            dgn_ref[...] = jnp.zeros_like(dgn_ref)
        o = of_ref[...] + ob_ref[...]
        r = lax.rsqrt(_seg_mean(o * o, seg) + EPS)
        xh = o * r
        gt = g_ref[...].astype(F32)
        sg = _sigmoid(gt)
        dyv = dy_ref[...]
        n = xh * gn_ref[...]
        dg_ref[...] = (dyv * n * (sg * (1.0 + gt * (1.0 - sg)))).astype(BF16)
        dn = dyv * (gt * sg)
        dgn_ref[...] += jnp.sum(dn * xh, axis=0, keepdims=True)
        dxh = dn * gn_ref[...]
        do_ref[...] = r * (dxh - xh * _seg_mean(dxh * xh, seg))

    bw = max(seg, LANE)
    row = pl.BlockSpec((tm, bw), lambda j, i: (i, j))
    vec = pl.BlockSpec((1, bw), lambda j, i: (0, j))
    return pl.pallas_call(
        body, name=name, grid=(w // bw, t // tm),
        in_specs=[row, row, pl.BlockSpec((tm, bw), lambda j, i: (i, gcb + j)), vec, row],
        out_specs=[row, row, vec],
        out_shape=[jax.ShapeDtypeStruct((t, w), F32), jax.ShapeDtypeStruct((t, w), BF16),
                   jax.ShapeDtypeStruct((1, w), F32)],
        compiler_params=_cparams(("parallel", "arbitrary")),
    )(o_f, o_b, gate_arr, gn.reshape(1, w), dy)


def gla_gate_fwd(p, wg, bg, name="gla_gate_fwd"):
    t = p.shape[0]
    tm = _row_tile(t)
    w = wg.shape[1]
    gcb = OD_GA // LANE

    def body(ga_ref, wg_ref, bg_ref, la_ref):
        z = _dot(ga_ref[...], wg_ref[...]) + bg_ref[...]
        la_ref[...] = (jnp.minimum(z, 0.0) - jnp.log(1.0 + jnp.exp(-jnp.abs(z)))) * (1.0 / GLA_TAU)

    return pl.pallas_call(
        body, name=name, grid=(t // tm,),
        in_specs=[pl.BlockSpec((tm, LANE), lambda i: (i, gcb)), pl.BlockSpec((LANE, w), lambda i: (0, 0)),
                  pl.BlockSpec((1, w), lambda i: (0, 0))],
        out_specs=pl.BlockSpec((tm, w), lambda i: (i, 0)),
        out_shape=jax.ShapeDtypeStruct((t, w), F32),
        compiler_params=_cparams(("parallel",)),
    )(p, wg, bg)


def gla_gate_bwd(p, wg, wg_t, bg, dla, name="gla_gate_bwd"):
    t = p.shape[0]
    tm = _row_tile(t)
    w = wg.shape[1]
    gcb = OD_GA // LANE

    def body(ga_ref, wg_ref, wgt_ref, bg_ref, dla_ref, dga_ref, dwg_ref, dbg_ref):
        @pl.when(pl.program_id(0) == 0)
        def _():
            dwg_ref[...] = jnp.zeros_like(dwg_ref)
            dbg_ref[...] = jnp.zeros_like(dbg_ref)
        ga = ga_ref[...]
        z = _dot(ga, wg_ref[...]) + bg_ref[...]
        dz = dla_ref[...] * (1.0 / GLA_TAU) * _sigmoid(-z)
        dga_ref[...] = _dot(dz, wgt_ref[...]).astype(BF16)
        dwg_ref[...] += _dot_tn(ga, dz)
        dbg_ref[...] += jnp.sum(dz, axis=0, keepdims=True)

    return pl.pallas_call(
        body, name=name, grid=(t // tm,),
        in_specs=[pl.BlockSpec((tm, LANE), lambda i: (i, gcb)), pl.BlockSpec((LANE, w), lambda i: (0, 0)),
                  pl.BlockSpec((w, LANE), lambda i: (0, 0)), pl.BlockSpec((1, w), lambda i: (0, 0)),
                  pl.BlockSpec((tm, w), lambda i: (i, 0))],
        out_specs=[pl.BlockSpec((tm, LANE), lambda i: (i, 0)), pl.BlockSpec((LANE, w), lambda i: (0, 0)),
                   pl.BlockSpec((1, w), lambda i: (0, 0))],
        out_shape=[jax.ShapeDtypeStruct((t, LANE), BF16), jax.ShapeDtypeStruct((LANE, w), F32),
                   jax.ShapeDtypeStruct((1, w), F32)],
        compiler_params=_cparams(("arbitrary",)),
    )(p, wg, wg_t, bg, dla)


FFN_COL = 1408


def _shifted(x, prev_row, next_row, first, last):
    tm = x.shape[0]
    row = lax.broadcasted_iota(jnp.int32, x.shape, 0)
    pr = jnp.where(first, 0.0, prev_row)
    nx = jnp.where(last, 0.0, next_row)
    xm1 = jnp.where(row == 0, pr, pltpu.roll(x, 1, 0))
    xp1 = jnp.where(row == tm - 1, nx, pltpu.roll(x, tm - 1, 0))
    return xm1, xp1


def _halo_rows(dtype):
    return SUBLANE * (4 // jnp.dtype(dtype).itemsize)


def _halo_specs(tm, tc, t, colmap, rowaxis, hr):
    nbh = tm // hr
    lasth = t // hr - 1

    def prev(*ids):
        i = ids[rowaxis]
        return (jnp.maximum(i * nbh - 1, 0), colmap(*ids))

    def nxt(*ids):
        i = ids[rowaxis]
        return (jnp.minimum((i + 1) * nbh, lasth), colmap(*ids))

    return pl.BlockSpec((hr, tc), prev), pl.BlockSpec((hr, tc), nxt)


def ffn_act_fwd(up, conv_w, conv_b, name="ffn_act_fwd"):
    t = up.shape[0]
    tm, tc = _row_tile(t), FFN_COL
    ncol = D_FF // tc

    hr = _halo_rows(up.dtype)

    def body(g_ref, gp_ref, gn_ref, v_ref, w_ref, b_ref, a_ref):
        i = pl.program_id(0)
        g = g_ref[...].astype(F32)
        gm1, gp1 = _shifted(g, gp_ref[hr - 1:hr, :].astype(F32), gn_ref[0:1, :].astype(F32), i == 0,
                            i == pl.num_programs(0) - 1)
        cc = w_ref[0:1, :] * gm1 + w_ref[1:2, :] * g + w_ref[2:3, :] * gp1 + b_ref[...]
        a_ref[...] = (cc * _sigmoid(cc) * v_ref[...].astype(F32)).astype(BF16)

    prev, nxt = _halo_specs(tm, tc, t, lambda i, j: j, 0, hr)
    return pl.pallas_call(
        body, name=name, grid=(t // tm, ncol),
        in_specs=[pl.BlockSpec((tm, tc), lambda i, j: (i, j)), prev, nxt,
                  pl.BlockSpec((tm, tc), lambda i, j: (i, j + ncol)),
                  pl.BlockSpec((SUBLANE, tc), lambda i, j: (0, j)), pl.BlockSpec((1, tc), lambda i, j: (0, j))],
        out_specs=pl.BlockSpec((tm, tc), lambda i, j: (i, j)),
        out_shape=jax.ShapeDtypeStruct((t, D_FF), BF16),
        compiler_params=_cparams(("parallel", "parallel")),
    )(up, up, up, up, conv_w, conv_b)


def ffn_act_bwd(up, conv_w, conv_b, dact, name="ffn_act_bwd"):
    t = up.shape[0]
    tm, tc = _row_tile(t), FFN_COL
    ncol = D_FF // tc
    hr = _halo_rows(up.dtype)

    def body(g_ref, gp_ref, gn_ref, v_ref, w_ref, b_ref, da_ref, dc_ref, dv_ref, dw_ref):
        i = pl.program_id(1)

        @pl.when(i == 0)
        def _():
            dw_ref[...] = jnp.zeros_like(dw_ref)
        g = g_ref[...].astype(F32)
        gm1, gp1 = _shifted(g, gp_ref[hr - 1:hr, :].astype(F32), gn_ref[0:1, :].astype(F32), i == 0,
                            i == pl.num_programs(1) - 1)
        cc = w_ref[0:1, :] * gm1 + w_ref[1:2, :] * g + w_ref[2:3, :] * gp1 + b_ref[...]
        sg = _sigmoid(cc)
        da = da_ref[...]
        dv_ref[...] = (da * (cc * sg)).astype(BF16)
        dc = da * v_ref[...].astype(F32) * (sg * (1.0 + cc * (1.0 - sg)))
        dc_ref[...] = dc
        dw_ref[0:1, :] += jnp.sum(dc * gm1, axis=0, keepdims=True)
        dw_ref[1:2, :] += jnp.sum(dc * g, axis=0, keepdims=True)
        dw_ref[2:3, :] += jnp.sum(dc * gp1, axis=0, keepdims=True)
        dw_ref[3:4, :] += jnp.sum(dc, axis=0, keepdims=True)

    prev, nxt = _halo_specs(tm, tc, t, lambda j, i: j, 1, hr)
    tile = pl.BlockSpec((tm, tc), lambda j, i: (i, j))
    return pl.pallas_call(
        body, name=name, grid=(ncol, t // tm),
        in_specs=[tile, prev, nxt, pl.BlockSpec((tm, tc), lambda j, i: (i, j + ncol)),
                  pl.BlockSpec((SUBLANE, tc), lambda j, i: (0, j)), pl.BlockSpec((1, tc), lambda j, i: (0, j)), tile],
        out_specs=[tile, tile, pl.BlockSpec((SUBLANE, tc), lambda j, i: (0, j))],
        out_shape=[jax.ShapeDtypeStruct((t, D_FF), F32), jax.ShapeDtypeStruct((t, D_FF), BF16),
                   jax.ShapeDtypeStruct((SUBLANE, D_FF), F32)],
        compiler_params=_cparams(("parallel", "arbitrary")),
    )(up, up, up, up, conv_w, conv_b, dact)


def conv_transpose(dc, conv_w, name="conv_transpose"):
    t = dc.shape[0]
    tm, tc = _row_tile(t), FFN_COL
    hr = _halo_rows(dc.dtype)

    def body(d_ref, dp_ref, dn_ref, w_ref, o_ref):
        i = pl.program_id(0)
        d = d_ref[...]
        dm1, dp1 = _shifted(d, dp_ref[hr - 1:hr, :], dn_ref[0:1, :], i == 0, i == pl.num_programs(0) - 1)
        o_ref[...] = (w_ref[0:1, :] * dp1 + w_ref[1:2, :] * d + w_ref[2:3, :] * dm1).astype(BF16)

    prev, nxt = _halo_specs(tm, tc, t, lambda i, j: j, 0, hr)
    tile = pl.BlockSpec((tm, tc), lambda i, j: (i, j))
    return pl.pallas_call(
        body, name=name, grid=(t // tm, D_FF // tc),
        in_specs=[tile, prev, nxt, pl.BlockSpec((SUBLANE, tc), lambda i, j: (0, j))],
        out_specs=tile, out_shape=jax.ShapeDtypeStruct((t, D_FF), BF16),
        compiler_params=_cparams(("parallel", "parallel")),
    )(dc, dc, dc, conv_w)


def loss_head(y, target, name="loss_head"):
    t, d = y.shape
    tm = _row_tile(t)

    def body(y_ref, t_ref, dy_ref, l_ref):
        @pl.when(pl.program_id(0) == 0)
        def _():
            l_ref[...] = jnp.zeros_like(l_ref)
        e = y_ref[...] - t_ref[...]
        dy_ref[...] = e * (1.0 / d)
        rowloss = jnp.sum(e * e, axis=-1, keepdims=True) * (0.5 / d)
        l_ref[...] += jnp.sum(rowloss, axis=0, keepdims=True)

    row = pl.BlockSpec((tm, d), lambda i: (i, 0))
    return pl.pallas_call(
        body, name=name, grid=(t // tm,), in_specs=[row, row],
        out_specs=[row, pl.BlockSpec((1, 1), lambda i: (0, 0))],
        out_shape=[jax.ShapeDtypeStruct((t, d), F32), jax.ShapeDtypeStruct((1, 1), F32)],
        compiler_params=_cparams(("arbitrary",)),
    )(y, target)


def _pad_heads(w, heads, width):
    lead = w.shape[:-1]
    w = w.reshape(*lead, heads, width)
    w = jnp.pad(w, [(0, 0)] * len(lead) + [(0, 0), (0, LANE - width)])
    return w.reshape(*lead, heads * LANE)


def _unpad_heads(w, heads, width):
    lead = w.shape[:-1]
    return w.reshape(*lead, heads, LANE)[..., :width].reshape(*lead, heads * width)


def _pad_rows_heads(w, heads, width):
    return _pad_heads(w.T, heads, width).T


def _unpad_rows_heads(w, heads, width):
    return _unpad_heads(w.T, heads, width).T


_EV_REAL = MLA_Q_RANK + MLA_KV_RANK + MLA_ROPE


def prep_even(wts, dt=BF16):
    w_in_t = wts["w_in_t"]
    w_in_tp = jnp.concatenate([w_in_t[:_EV_REAL], jnp.zeros((EV_RQ - _EV_REAL, D_MODEL), w_in_t.dtype),
                               w_in_t[_EV_REAL:]], axis=0).astype(dt)
    wuq = _pad_heads(wts["w_uq"], MLA_HEADS, MLA_QK).astype(dt)
    ukv = wts["w_ukv"].reshape(MLA_KV_RANK, MLA_HEADS, MLA_NOPE + MLA_V)
    wk = _pad_heads(ukv[..., :MLA_NOPE].reshape(MLA_KV_RANK, -1), MLA_HEADS, MLA_NOPE).astype(dt)
    wv = _pad_heads(ukv[..., MLA_NOPE:].reshape(MLA_KV_RANK, -1), MLA_HEADS, MLA_V).astype(dt)
    w_out = wts["w_out"]
    wa = _pad_rows_heads(w_out[:MLA_HEADS * MLA_V], MLA_HEADS, MLA_V).astype(dt)
    wr = w_out[MLA_HEADS * MLA_V:].astype(dt)
    pad1 = lambda v, n: jnp.pad(v.astype(F32), (0, n - v.shape[0])).reshape(1, n)
    lg = lambda th: jnp.log1p(-jnp.exp2(-th.astype(F32)))
    return dict(
        w_in=w_in_tp.T, w_in_t=w_in_tp, wuq=wuq, wuq_t=wuq.T, wk=wk, wk_t=wk.T, wv=wv, wv_t=wv.T,
        wa=wa, wa_t=wa.T, wr=wr, wr_t=wr.T,
        mix_norm=wts["mix_norm"].astype(F32), q_norm=wts["q_norm"].astype(F32).reshape(1, -1),
        kv_norm=wts["kv_norm"].astype(F32).reshape(1, -1),
        qhn=pad1(wts["q_head_norm"], LANE), khn=pad1(wts["k_head_norm"], LANE),
        la_f=jnp.repeat(lg(wts["theta_fwd"]), RET_DK).reshape(1, -1),
        la_b=jnp.repeat(lg(wts["theta_bwd"]), RET_DK).reshape(1, -1),
        out_norm=wts["ret_out_norm"].astype(F32).reshape(-1),
    )


def prep_odd(wts, dt=BF16):
    w_in_t = wts["w_in_t"]
    w_in_tp = jnp.concatenate([w_in_t, jnp.zeros((OD_IN - w_in_t.shape[0], D_MODEL), w_in_t.dtype)],
                              axis=0).astype(dt)
    hk = GLA_HEADS * GLA_DK
    wg = jnp.zeros((LANE, 2 * hk), F32)
    wg = wg.at[:GLA_GATE_RANK, :hk].set(wts["w_gate_fwd"].astype(F32))
    wg = wg.at[GLA_GATE_RANK:2 * GLA_GATE_RANK, hk:].set(wts["w_gate_bwd"].astype(F32))
    wg = wg.astype(dt)
    bg = jnp.concatenate([wts["b_gate_fwd"], wts["b_gate_bwd"]]).astype(F32).reshape(1, -1)
    w_out = wts["w_out"].astype(dt)
    return dict(w_in=w_in_tp.T, w_in_t=w_in_tp, wg=wg, wg_t=wg.T, bg=bg, w_out=w_out, w_out_t=w_out.T,
                mix_norm=wts["mix_norm"].astype(F32), out_norm=wts["gla_out_norm"].astype(F32).reshape(-1))


def prep_ffn(wts, dt=BF16):
    w_up_t = wts["w_up_t"].astype(dt)
    w_down = wts["w_down"].astype(dt)
    cw = jnp.pad(wts["conv_w"].astype(F32), ((0, SUBLANE - 3), (0, 0)))
    return dict(w_up=w_up_t.T, w_up_t=w_up_t, w_down=w_down, w_down_t=w_down.T, conv_w=cw,
                conv_b=wts["conv_b"].astype(F32).reshape(1, -1), norm=wts["norm"].astype(F32))


_RET = dict(qcb=EV_RQ // LANE, kcb=EV_RK // LANE, vcb=EV_RV // LANE, la_row=True, chunk=RET_CHUNK, dvg=LANE,
            sub=2, qscale=1.0, kscale=RET_DK ** -0.5)
_GLA = dict(qcb=OD_Q // LANE, kcb=OD_K // LANE, vcb=OD_V // GLA_DV, la_row=False, chunk=GLA_CHUNK, dvg=GLA_DV,
            sub=1, qscale=GLA_DK ** -0.5, kscale=1.0)
_FWD_DIR = dict(reverse=False, inclusive=True)
_BWD_DIR = dict(reverse=True, inclusive=False)


def even_fwd(x, w, tabs_mla, tabs_ret, tag):
    h = rmsnorm_fwd(x, w["mix_norm"], name=f"{tag}_norm")
    p = mm_nn(h, w["w_in"], out_dtype=BF16, name=f"{tag}_in")
    q, k, v = mla_prep_fwd(p, tabs_mla, w["q_norm"], w["kv_norm"], w["wuq"], w["wk"], w["wv"], w["qhn"], w["khn"],
                           name=f"{tag}_mla_prep")
    o, lse = attn_fwd(q, k, v, name=f"{tag}_attn")
    of, stf = scan_fwd(p, p, p, w["la_f"], lacb=0, rope=tabs_ret, name=f"{tag}_ret_f", **_RET, **_FWD_DIR)
    ob, stb = scan_fwd(p, p, p, w["la_b"], lacb=0, rope=tabs_ret, name=f"{tag}_ret_b", **_RET, **_BWD_DIR)
    r = gated_norm_fwd(of, ob, p, EV_RG // LANE, w["out_norm"], RET_DV, name=f"{tag}_ret_out")
    x1 = mm_nn(o, w["wa"], res=x, name=f"{tag}_out_a")
    x2 = mm_nn(r, w["wr"], res=x1, name=f"{tag}_out_r")
    return x2, dict(x=x, h=h, p=p, q=q, k=k, v=v, o=o, lse=lse, of=of, ob=ob, stf=stf, stb=stb, r=r)


def even_bwd(dx, s, w, tabs_mla, tabs_ret, tag):
    tag = tag + "_b"
    do = mm_nn(dx, w["wa_t"], name=f"{tag}_dout_a")
    dr = mm_nn(dx, w["wr_t"], name=f"{tag}_dout_r")
    d_wa = mm_tn(s["o"], dx, out_dtype=BF16, name=f"{tag}_dwa")
    d_wr = mm_tn(s["r"], dx, out_dtype=BF16, name=f"{tag}_dwr")
    dq, dk, dv = attn_bwd(s["q"], s["k"], s["v"], s["o"], s["lse"], do, name=f"{tag}_attn")
    (dp_mla, d_wuq, d_wk, d_wv, d_qn, d_kvn, d_qhn, d_khn) = mla_prep_bwd(
        s["p"], tabs_mla, w["q_norm"], w["kv_norm"], w["wuq"], w["wk"], w["wv"], w["wuq_t"], w["wk_t"], w["wv_t"],
        w["qhn"], w["khn"], dq, dk, dv, name=f"{tag}_mla_prep")
    d_o, d_gate, d_gn = gated_norm_bwd(s["of"], s["ob"], s["p"], EV_RG // LANE, w["out_norm"], RET_DV, dr,
                                       name=f"{tag}_ret_out")
    p = s["p"]
    g1 = scan_bwd(p, p, p, w["la_f"], s["stf"], d_o, None, lacb=0, rope=tabs_ret, name=f"{tag}_ret_f",
                  **_RET, **_FWD_DIR)
    g2 = scan_bwd(p, p, p, w["la_b"], s["stb"], d_o, g1[:3], lacb=0, rope=tabs_ret, name=f"{tag}_ret_b",
                  **_RET, **_BWD_DIR)
    dp = jnp.concatenate([dp_mla, g2[0], g2[1], g2[2], d_gate], axis=1)
    dh = mm_nn(dp, w["w_in_t"], name=f"{tag}_dh")
    d_win_t = mm_tn(dp, s["h"], out_dtype=BF16, name=f"{tag}_dwin")
    dx_in, d_mix = rmsnorm_bwd(s["x"], w["mix_norm"], dh, dx, name=f"{tag}_norm")
    grads = dict(w_in_t=d_win_t, wuq=d_wuq, wk=d_wk, wv=d_wv, wa=d_wa, wr=d_wr, mix_norm=d_mix, q_norm=d_qn,
                 kv_norm=d_kvn, qhn=d_qhn, khn=d_khn, la_f=g1[3], la_b=g2[3], out_norm=d_gn)
    return dx_in, grads


def odd_fwd(x, w, tag):
    h = rmsnorm_fwd(x, w["mix_norm"], name=f"{tag}_norm")
    p = mm_nn(h, w["w_in"], out_dtype=BF16, name=f"{tag}_in")
    la = gla_gate_fwd(p, w["wg"], w["bg"], name=f"{tag}_gate")
    of, stf = scan_fwd(p, p, p, la, lacb=0, name=f"{tag}_gla_f", **_GLA, **_FWD_DIR)
    ob, stb = scan_fwd(p, p, p, la, lacb=N_GROUPS, name=f"{tag}_gla_b", **_GLA, **_BWD_DIR)
    y = gated_norm_fwd(of, ob, p, OD_R // GLA_DV, w["out_norm"], GLA_DV, name=f"{tag}_gla_out")
    x1 = mm_nn(y, w["w_out"], res=x, name=f"{tag}_out")
    return x1, dict(x=x, h=h, p=p, la=la, of=of, ob=ob, stf=stf, stb=stb, y=y)


def odd_bwd(dx, s, w, tag):
    tag = tag + "_b"
    dy = mm_nn(dx, w["w_out_t"], name=f"{tag}_dout")
    d_wout = mm_tn(s["y"], dx, out_dtype=BF16, name=f"{tag}_dwout")
    d_o, d_gate, d_gn = gated_norm_bwd(s["of"], s["ob"], s["p"], OD_R // GLA_DV, w["out_norm"], GLA_DV, dy,
                                       name=f"{tag}_gla_out")
    p, la = s["p"], s["la"]
    g1 = scan_bwd(p, p, p, la, s["stf"], d_o, None, lacb=0, name=f"{tag}_gla_f", **_GLA, **_FWD_DIR)
    g2 = scan_bwd(p, p, p, la, s["stb"], d_o, g1[:3], lacb=N_GROUPS, name=f"{tag}_gla_b", **_GLA, **_BWD_DIR)
    dla = jnp.concatenate([g1[3], g2[3]], axis=1)
    d_ga, d_wg, d_bg = gla_gate_bwd(p, w["wg"], w["wg_t"], w["bg"], dla, name=f"{tag}_gate")
    dp = jnp.concatenate([g2[0], g2[1], g2[2], d_gate, d_ga], axis=1)
    dh = mm_nn(dp, w["w_in_t"], name=f"{tag}_dh")
    d_win_t = mm_tn(dp, s["h"], out_dtype=BF16, name=f"{tag}_dwin")
    dx_in, d_mix = rmsnorm_bwd(s["x"], w["mix_norm"], dh, dx, name=f"{tag}_norm")
    grads = dict(w_in_t=d_win_t, wg=d_wg, bg=d_bg, w_out=d_wout, mix_norm=d_mix, out_norm=d_gn)
    return dx_in, grads


def ffn_fwd(x, w, tag):
    h = rmsnorm_fwd(x, w["norm"], name=f"{tag}_norm")
    up = mm_nn(h, w["w_up"], out_dtype=BF16, name=f"{tag}_up")
    act = ffn_act_fwd(up, w["conv_w"], w["conv_b"], name=f"{tag}_act")
    x1 = mm_nn(act, w["w_down"], res=x, name=f"{tag}_down")
    return x1, dict(x=x, h=h, up=up, act=act)


def ffn_bwd(dx, s, w, tag):
    tag = tag + "_b"
    dact = mm_nn(dx, w["w_down_t"], name=f"{tag}_dact")
    d_wdown = mm_tn(s["act"], dx, out_dtype=BF16, name=f"{tag}_dwdown")
    dc, dval, d_conv = ffn_act_bwd(s["up"], w["conv_w"], w["conv_b"], dact, name=f"{tag}_act")
    dgate = conv_transpose(dc, w["conv_w"], name=f"{tag}_convt")
    dh1 = mm_nn(dgate, w["w_up_t"][:D_FF], name=f"{tag}_dh_g")
    dh = mm_nn(dval, w["w_up_t"][D_FF:], res=dh1, name=f"{tag}_dh_v")
    d_wup_t = jnp.concatenate([mm_tn(dgate, s["h"], out_dtype=BF16, name=f"{tag}_dwup_g"),
                               mm_tn(dval, s["h"], out_dtype=BF16, name=f"{tag}_dwup_v")], axis=0)
    dx_in, d_norm = rmsnorm_bwd(s["x"], w["norm"], dh, dx, name=f"{tag}_norm")
    grads = dict(w_up_t=d_wup_t, w_down=d_wdown, conv_w=d_conv[:3], conv_b=d_conv[3], norm=d_norm)
    return dx_in, grads


def unprep_even_grads(g, theta_fwd, theta_bwd):
    d_win_t = jnp.concatenate([g["w_in_t"][:_EV_REAL], g["w_in_t"][EV_RQ:]], axis=0)
    d_uq = _unpad_heads(g["wuq"], MLA_HEADS, MLA_QK)
    dk_ = _unpad_heads(g["wk"], MLA_HEADS, MLA_NOPE).reshape(MLA_KV_RANK, MLA_HEADS, MLA_NOPE)
    dv_ = _unpad_heads(g["wv"], MLA_HEADS, MLA_V).reshape(MLA_KV_RANK, MLA_HEADS, MLA_V)
    d_ukv = jnp.concatenate([dk_, dv_], axis=-1).reshape(MLA_KV_RANK, -1)
    d_wout = jnp.concatenate([_unpad_rows_heads(g["wa"], MLA_HEADS, MLA_V), g["wr"]], axis=0)

    def dtheta(dla, th):
        dlg = dla.reshape(RET_HEADS, RET_DK).sum(axis=-1)
        e = jnp.exp2(-th.astype(F32))
        return dlg * (e * math.log(2.0)) / (1.0 - e)

    return dict(mix_norm=g["mix_norm"].reshape(-1), w_in_t=d_win_t, q_norm=g["q_norm"].reshape(-1),
                kv_norm=g["kv_norm"].reshape(-1), w_uq=d_uq, w_ukv=d_ukv, q_head_norm=g["qhn"].reshape(-1)[:MLA_QK],
                k_head_norm=g["khn"].reshape(-1)[:MLA_QK], theta_fwd=dtheta(g["la_f"], theta_fwd),
                theta_bwd=dtheta(g["la_b"], theta_bwd), ret_out_norm=g["out_norm"].reshape(RET_HEADS, RET_DV),
                w_out=d_wout)


def unprep_odd_grads(g):
    hk = GLA_HEADS * GLA_DK
    return dict(mix_norm=g["mix_norm"].reshape(-1), w_in_t=g["w_in_t"][:OD_GA + 2 * GLA_GATE_RANK],
                w_gate_fwd=g["wg"][:GLA_GATE_RANK, :hk], b_gate_fwd=g["bg"].reshape(-1)[:hk],
                w_gate_bwd=g["wg"][GLA_GATE_RANK:2 * GLA_GATE_RANK, hk:], b_gate_bwd=g["bg"].reshape(-1)[hk:],
                gla_out_norm=g["out_norm"].reshape(GLA_HEADS, GLA_DV), w_out=g["w_out"])


def unprep_ffn_grads(g):
    return dict(norm=g["norm"].reshape(-1), w_up_t=g["w_up_t"], conv_w=g["conv_w"], conv_b=g["conv_b"],
                w_down=g["w_down"])


def _mesh_pos():
    return tuple(lax.axis_index(n) for n in MESH_AXES)


def _slot(px, py, pc):
    return 4 * px + 2 * py + pc


def all_gather_blocks(blk, name="weight_all_gather"):
    r, w = blk.shape

    def body(x_ref, out_ref, send_sems, recv_sems, local_sem):
        x, y, c = _mesh_pos()
        me, sibling = (x, y, c), (x, y, 1 - c)
        chips = [(1 - x, y), (x, 1 - y), (1 - x, 1 - y)]

        def copy(k, block, to, src=None):
            dst = out_ref.at[_slot(*block)]
            return pltpu.make_async_remote_copy(
                src_ref=dst if src is None else src, dst_ref=dst, send_sem=send_sems.at[k],
                recv_sem=recv_sems.at[k], device_id=to, device_id_type=pl.DeviceIdType.MESH)

        mine = pltpu.make_async_copy(x_ref, out_ref.at[_slot(*me)], local_sem)
        mine.start()
        first = [copy(0, me, sibling, src=x_ref)]
        first += [copy(1 + j, me, (*chip, c), src=x_ref) for j, chip in enumerate(chips)]
        for cp in first:
            cp.start()
        passed = [copy(4 + j, (*chip, c), sibling) for j, chip in enumerate(chips)]
        for j, chip in enumerate(chips):
            copy(1 + j, (*chip, c), me).wait_recv()
            passed[j].start()
        copy(0, sibling, me).wait_recv()
        for j, chip in enumerate(chips):
            copy(4 + j, (*chip, 1 - c), me).wait_recv()
        for cp in first + passed:
            cp.wait_send()
        mine.wait()

    return pl.pallas_call(
        body, name=name,
        out_shape=jax.ShapeDtypeStruct((N_DEV, r, w), blk.dtype),
        in_specs=[pl.BlockSpec(memory_space=pl.ANY)],
        out_specs=pl.BlockSpec(memory_space=pl.ANY),
        scratch_shapes=[pltpu.SemaphoreType.DMA((7,)), pltpu.SemaphoreType.DMA((7,)), pltpu.SemaphoreType.DMA],
    )(blk)


N_CHIP = 4


def pair_exchange(other, name="grad_pair_exchange"):
    _, r, w = other.shape

    def body(s_ref, r_ref, send_sems, recv_sems):
        x, y, c = _mesh_pos()
        copies = []
        for s in range(N_CHIP):
            cp = pltpu.make_async_remote_copy(
                src_ref=s_ref.at[s], dst_ref=r_ref.at[s], send_sem=send_sems.at[s], recv_sem=recv_sems.at[s],
                device_id=(x, y, 1 - c), device_id_type=pl.DeviceIdType.MESH)
            cp.start()
            copies.append(cp)
        for cp in copies:
            cp.wait()

    return pl.pallas_call(
        body, name=name,
        out_shape=jax.ShapeDtypeStruct((N_CHIP, r, w), other.dtype),
        in_specs=[pl.BlockSpec(memory_space=pl.ANY)],
        out_specs=pl.BlockSpec(memory_space=pl.ANY),
        scratch_shapes=[pltpu.SemaphoreType.DMA((N_CHIP,)), pltpu.SemaphoreType.DMA((N_CHIP,))],
    )(other)


def chip_exchange(part, name="grad_chip_exchange"):
    _, r, w = part.shape

    def body(s_ref, r_ref, send_sems, recv_sems, local_sem):
        x, y, c = _mesh_pos()
        me = 2 * x + y
        mine = pltpu.make_async_copy(s_ref.at[me], r_ref.at[me], local_sem)
        mine.start()
        copies = []
        for k in range(1, N_CHIP):
            px = 1 - x if (k >> 1) & 1 else x
            py = 1 - y if k & 1 else y
            cp = pltpu.make_async_remote_copy(
                src_ref=s_ref.at[2 * px + py], dst_ref=r_ref.at[me], send_sem=send_sems.at[k - 1],
                recv_sem=recv_sems.at[k - 1], device_id=(px, py, c), device_id_type=pl.DeviceIdType.MESH)
            cp.start()
            copies.append(cp)
        for cp in copies:
            cp.wait()
        mine.wait()

    return pl.pallas_call(
        body, name=name,
        out_shape=jax.ShapeDtypeStruct((N_CHIP, r, w), part.dtype),
        in_specs=[pl.BlockSpec(memory_space=pl.ANY)],
        out_specs=pl.BlockSpec(memory_space=pl.ANY),
        scratch_shapes=[pltpu.SemaphoreType.DMA((N_CHIP - 1,)), pltpu.SemaphoreType.DMA((N_CHIP - 1,)),
                        pltpu.SemaphoreType.DMA],
    )(part)


FLAT_W = 1024
FLAT_TILE = 256


def pair_add(mine, theirs, name="grad_pair_add"):
    n, r, w = mine.shape

    def body(a_ref, b_ref, o_ref):
        o_ref[...] = (a_ref[...].astype(F32) + b_ref[...].astype(F32)).astype(o_ref.dtype)

    tr = _slab_tile(r)
    blk = pl.BlockSpec((n, tr, w), lambda i: (0, i, 0))
    return pl.pallas_call(
        body, name=name, grid=(r // tr,), in_specs=[blk, blk], out_specs=blk,
        out_shape=jax.ShapeDtypeStruct((n, r, w), mine.dtype),
        compiler_params=_cparams(("parallel",)),
    )(mine, theirs)


def sum_slots(recv, name="grad_sum"):
    n, r, w = recv.shape

    def body(r_ref, o_ref):
        acc = r_ref[0].astype(F32)
        for k in range(1, n):
            acc = acc + r_ref[k].astype(F32)
        o_ref[...] = acc

    tr = _slab_tile(r)
    return pl.pallas_call(
        body, name=name, grid=(r // tr,),
        in_specs=[pl.BlockSpec((n, tr, w), lambda i: (0, i, 0))],
        out_specs=pl.BlockSpec((tr, w), lambda i: (i, 0)),
        out_shape=jax.ShapeDtypeStruct((r, w), F32),
        compiler_params=_cparams(("parallel",)),
    )(recv)


def _slab_tile(r):
    return max(t for t in range(SUBLANE, FLAT_TILE + 1, SUBLANE) if r % t == 0)


def adamw(wf, gf, mf, vf, name="adamw"):
    r, w = wf.shape
    tr = _slab_tile(r)

    def body(w_ref, g_ref, m_ref, v_ref, d_ref, m_out, v_out):
        g = g_ref[...]
        m = ADAM_B1 * m_ref[...] + (1.0 - ADAM_B1) * g
        v = ADAM_B2 * v_ref[...] + (1.0 - ADAM_B2) * (g * g)
        m_hat = m / (1.0 - ADAM_B1 ** ADAM_STEP)
        v_hat = v / (1.0 - ADAM_B2 ** ADAM_STEP)
        d_ref[...] = -ADAM_LR * (m_hat / (jnp.sqrt(v_hat) + ADAM_EPS) + ADAM_WD * w_ref[...])
        m_out[...] = m
        v_out[...] = v

    tile = pl.BlockSpec((tr, w), lambda i: (i, 0))
    return pl.pallas_call(
        body, name=name, grid=(r // tr,), in_specs=[tile] * 4, out_specs=[tile] * 3,
        out_shape=[jax.ShapeDtypeStruct((r, w), F32)] * 3,
        compiler_params=_cparams(("parallel",)),
    )(wf, gf, mf, vf)


ROW_FORM = [("w_in_even", "T"), ("w_out_even", "R"), ("w_in_odd", "T"), ("w_out_odd", "R"), ("ffn_w_up", "T"),
            ("ffn_w_down", "R")]
SHARDED_MID = [("mla_w_uq", 2), ("mla_w_ukv", 2)]
SHARDED_SMALL = [("mix_norm_odd", 1), ("gla_w_gate_fwd", 2), ("gla_b_gate_fwd", 1), ("gla_w_gate_bwd", 2),
                 ("gla_b_gate_bwd", 1), ("gla_out_norm", 2), ("ffn_conv_w", 2)]
REPLICATED = ["mix_norm_even", "mla_q_norm", "mla_kv_norm", "mla_q_head_norm", "mla_k_head_norm", "ret_theta_fwd",
              "ret_theta_bwd", "ret_out_norm", "ffn_norm", "ffn_conv_b"]
WEIGHT_NAMES = ["mix_norm_even", "w_in_even", "mla_q_norm", "mla_kv_norm", "mla_w_uq", "mla_w_ukv",
                "mla_q_head_norm", "mla_k_head_norm", "ret_theta_fwd", "ret_theta_bwd", "ret_out_norm", "w_out_even",
                "mix_norm_odd", "w_in_odd", "gla_w_gate_fwd", "gla_b_gate_fwd", "gla_w_gate_bwd", "gla_b_gate_bwd",
                "gla_out_norm", "w_out_odd", "ffn_norm", "ffn_w_up", "ffn_conv_w", "ffn_conv_b", "ffn_w_down"]


def _round_up(n, m):
    return -(-n // m) * m


def _pack_rows(parts, rows):
    flat = jnp.concatenate(parts, axis=-1)
    pad = rows * FLAT_W - flat.shape[-1]
    flat = jnp.pad(flat, [(0, 0)] * (flat.ndim - 1) + [(0, pad)])
    return flat.reshape(*flat.shape[:-1], rows, FLAT_W)


def _row_form(v, form):
    if form == "T":
        v = jnp.swapaxes(v, 1, 2)
    return v.reshape(-1, v.shape[-1])


def _row_counts(local_shapes):
    return [local_shapes[n][0] * local_shapes[n][2 if f == "T" else 1] for n, f in ROW_FORM]


def _tail_layout(local_shapes):
    n_sh = sum(math.prod(local_shapes[n]) for n, _ in SHARDED_MID + SHARDED_SMALL)
    n_rep = sum(math.prod(local_shapes[n]) for n in REPLICATED)
    sh_rows = _round_up(-(-n_sh // FLAT_W), SUBLANE)
    rep_rows = _round_up(-(-n_rep // FLAT_W), SUBLANE)
    return sh_rows, rep_rows, _round_up(sh_rows + rep_rows, FLAT_TILE)


def pack_tail(vals, local_shapes):
    sh_rows, rep_rows, rows = _tail_layout(local_shapes)
    sh = _pack_rows([vals[n].astype(F32).reshape(-1) for n, _ in SHARDED_MID + SHARDED_SMALL], sh_rows)
    rep = _pack_rows([vals[n].astype(F32).reshape(-1) for n in REPLICATED], rep_rows)
    return jnp.concatenate([sh, rep, jnp.zeros((rows - sh_rows - rep_rows, FLAT_W), F32)], axis=0)


def unpack_tail(tail, local_shapes):
    sh_rows, rep_rows, _ = _tail_layout(local_shapes)
    out = {}
    for names, flat in (([n for n, _ in SHARDED_MID + SHARDED_SMALL], tail[:sh_rows].reshape(-1)),
                        (REPLICATED, tail[sh_rows:sh_rows + rep_rows].reshape(-1))):
        off = 0
        for n in names:
            k = math.prod(local_shapes[n])
            out[n] = flat[off:off + k].reshape(local_shapes[n])
            off += k
    return out


def unpack_rows(slab, local_shapes):
    out = {}
    off = 0
    for (n, form), rows in zip(ROW_FORM, _row_counts(local_shapes)):
        loc = local_shapes[n]
        piece = slab[off:off + rows]
        if form == "T":
            piece = jnp.swapaxes(piece.reshape(loc[0], loc[2], loc[1]), 1, 2)
        out[n] = piece.reshape(loc)
        off += rows
    return out


def pack_grad_blocks(full_grads, local_shapes):
    sh_rows, rep_rows, rows = _tail_layout(local_shapes)
    my_c = lax.axis_index("c")

    def by_core(blocks8):
        b = blocks8.reshape(N_CHIP, 2, *blocks8.shape[1:])
        return (lax.dynamic_index_in_dim(b, my_c, 1, keepdims=False),
                lax.dynamic_index_in_dim(b, 1 - my_c, 1, keepdims=False))

    blocks = []
    for n, form in ROW_FORM:
        g = full_grads[n].astype(BF16)
        layers, total = g.shape[0], g.shape[1]
        g = g.reshape(layers, N_DEV, total // N_DEV, FLAT_W)
        blocks.append(by_core(jnp.swapaxes(g, 0, 1).reshape(N_DEV, -1, FLAT_W)))
    parts = []
    for n, ax in SHARDED_MID + SHARDED_SMALL:
        g = full_grads[n].astype(F32)
        loc = local_shapes[n]
        g = g.reshape(*g.shape[:ax], N_DEV, loc[ax], *g.shape[ax + 1:])
        parts.append(jnp.moveaxis(g, ax, 0).reshape(N_DEV, -1))
    sh = _pack_rows(parts, sh_rows)
    rep = _pack_rows([full_grads[n].astype(F32).reshape(-1) for n in REPLICATED], rep_rows)
    rep = jnp.broadcast_to(rep[None], (N_DEV, rep_rows, FLAT_W))
    pad = jnp.zeros((N_DEV, rows - sh_rows - rep_rows, FLAT_W), F32)
    blocks.append(by_core(jnp.concatenate([sh, rep, pad], axis=1).astype(BF16)))
    return (jnp.concatenate([b[0] for b in blocks], axis=1), jnp.concatenate([b[1] for b in blocks], axis=1))


def pack_gather_block(vals, local_shapes):
    big = jnp.concatenate([_row_form(vals[n].astype(BF16), f) for n, f in ROW_FORM], axis=0)
    mid = [vals[n].astype(BF16).reshape(-1) for n, _ in SHARDED_MID]
    small = jnp.concatenate([vals[n].astype(F32).reshape(-1) for n, _ in SHARDED_SMALL])
    small = lax.bitcast_convert_type(small, BF16).reshape(-1)
    n = sum(v.shape[0] for v in mid) + small.shape[0]
    tail = _pack_rows(mid + [small], _round_up(-(-n // FLAT_W), 2 * SUBLANE))
    return jnp.concatenate([big, tail], axis=0)


def unpack_gathered(gathered, local_shapes):
    out = {}
    off = 0
    for (n, form), rows in zip(ROW_FORM, _row_counts(local_shapes)):
        layers = local_shapes[n][0]
        piece = gathered[:, off:off + rows].reshape(N_DEV, layers, rows // layers, FLAT_W)
        out[n] = jnp.swapaxes(piece, 0, 1).reshape(layers, N_DEV * (rows // layers), FLAT_W)
        off += rows
    flat = gathered[:, off:].reshape(N_DEV, -1)
    off = 0

    def full(piece, n, ax):
        loc = local_shapes[n]
        piece = jnp.moveaxis(piece.reshape(N_DEV, *loc), 0, ax)
        return piece.reshape(*loc[:ax], N_DEV * loc[ax], *loc[ax + 1:])

    for n, ax in SHARDED_MID:
        k = math.prod(local_shapes[n])
        out[n] = full(flat[:, off:off + k], n, ax)
        off += k
    for n, ax in SHARDED_SMALL:
        k = math.prod(local_shapes[n])
        pairs = flat[:, off:off + 2 * k].reshape(N_DEV, k, 2)
        out[n] = full(lax.bitcast_convert_type(pairs, F32), n, ax)
        off += 2 * k
    return out


def local_step(x, positions, target, fw):
    tabs_mla = rope_tables(positions, MLA_NOPE, MLA_ROPE // 2, LANE)
    tabs_ret = rope_tables(positions, 0, RET_DK // 2, RET_DK)
    layers = []
    for layer in range(DEPTH):
        i = layer // 2
        if layer % 2 == 0:
            mw = prep_even(dict(
                w_in_t=fw["w_in_even"][i], w_uq=fw["mla_w_uq"][i], w_ukv=fw["mla_w_ukv"][i], w_out=fw["w_out_even"][i],
                mix_norm=fw["mix_norm_even"][i], q_norm=fw["mla_q_norm"][i], kv_norm=fw["mla_kv_norm"][i],
                q_head_norm=fw["mla_q_head_norm"][i], k_head_norm=fw["mla_k_head_norm"][i],
                theta_fwd=fw["ret_theta_fwd"][i], theta_bwd=fw["ret_theta_bwd"][i],
                ret_out_norm=fw["ret_out_norm"][i]))
        else:
            mw = prep_odd(dict(
                w_in_t=fw["w_in_odd"][i], w_gate_fwd=fw["gla_w_gate_fwd"][i], b_gate_fwd=fw["gla_b_gate_fwd"][i],
                w_gate_bwd=fw["gla_w_gate_bwd"][i], b_gate_bwd=fw["gla_b_gate_bwd"][i],
                gla_out_norm=fw["gla_out_norm"][i], w_out=fw["w_out_odd"][i], mix_norm=fw["mix_norm_odd"][i]))
        fwt = prep_ffn(dict(norm=fw["ffn_norm"][layer], w_up_t=fw["ffn_w_up"][layer], conv_w=fw["ffn_conv_w"][layer],
                            conv_b=fw["ffn_conv_b"][layer], w_down=fw["ffn_w_down"][layer]))
        layers.append((mw, fwt))

    saved = []
    for layer, (mw, fwt) in enumerate(layers):
        if layer % 2 == 0:
            x, sm = even_fwd(x, mw, tabs_mla, tabs_ret, f"l{layer}_mix")
        else:
            x, sm = odd_fwd(x, mw, f"l{layer}_mix")
        x, sf = ffn_fwd(x, fwt, f"l{layer}_ffn")
        saved.append((sm, sf))

    dx, loss = loss_head(x, target, name="loss_head")

    per_layer = [None] * DEPTH
    for layer in reversed(range(DEPTH)):
        mw, fwt = layers[layer]
        sm, sf = saved[layer]
        dx, gf = ffn_bwd(dx, sf, fwt, f"l{layer}_ffn")
        if layer % 2 == 0:
            dx, gm = even_bwd(dx, sm, mw, tabs_mla, tabs_ret, f"l{layer}_mix")
            gm = unprep_even_grads(gm, fw["ret_theta_fwd"][layer // 2], fw["ret_theta_bwd"][layer // 2])
        else:
            dx, gm = odd_bwd(dx, sm, mw, f"l{layer}_mix")
            gm = unprep_odd_grads(gm)
        per_layer[layer] = (gm, unprep_ffn_grads(gf))

    ev = [per_layer[l][0] for l in range(0, DEPTH, 2)]
    od = [per_layer[l][0] for l in range(1, DEPTH, 2)]
    ff = [per_layer[l][1] for l in range(DEPTH)]
    st = lambda lst, key: jnp.stack([g[key] for g in lst])
    grads = {
        "mix_norm_even": st(ev, "mix_norm"), "w_in_even": st(ev, "w_in_t"), "mla_q_norm": st(ev, "q_norm"),
        "mla_kv_norm": st(ev, "kv_norm"), "mla_w_uq": st(ev, "w_uq"), "mla_w_ukv": st(ev, "w_ukv"),
        "mla_q_head_norm": st(ev, "q_head_norm"), "mla_k_head_norm": st(ev, "k_head_norm"),
        "ret_theta_fwd": st(ev, "theta_fwd"), "ret_theta_bwd": st(ev, "theta_bwd"),
        "ret_out_norm": st(ev, "ret_out_norm"), "w_out_even": st(ev, "w_out"),
        "mix_norm_odd": st(od, "mix_norm"), "w_in_odd": st(od, "w_in_t"), "gla_w_gate_fwd": st(od, "w_gate_fwd"),
        "gla_b_gate_fwd": st(od, "b_gate_fwd"), "gla_w_gate_bwd": st(od, "w_gate_bwd"),
        "gla_b_gate_bwd": st(od, "b_gate_bwd"), "gla_out_norm": st(od, "gla_out_norm"), "w_out_odd": st(od, "w_out"),
        "ffn_norm": st(ff, "norm"), "ffn_w_up": st(ff, "w_up_t"), "ffn_conv_w": st(ff, "conv_w"),
        "ffn_conv_b": st(ff, "conv_b"), "ffn_w_down": st(ff, "w_down"),
    }
    return loss, dx, grads


def kernel(x, positions, mix_norm_even, w_in_even, mla_q_norm, mla_kv_norm, mla_w_uq, mla_w_ukv, mla_q_head_norm, mla_k_head_norm, ret_theta_fwd, ret_theta_bwd, ret_out_norm, w_out_even, mix_norm_odd, w_in_odd, gla_w_gate_fwd, gla_b_gate_fwd, gla_w_gate_bwd, gla_b_gate_bwd, gla_out_norm, w_out_odd, ffn_norm, ffn_w_up, ffn_conv_w, ffn_conv_b, ffn_w_down, loss_target, m_mix_norm_even, m_w_in_even, m_mla_q_norm, m_mla_kv_norm, m_mla_w_uq, m_mla_w_ukv, m_mla_q_head_norm, m_mla_k_head_norm, m_ret_theta_fwd, m_ret_theta_bwd, m_ret_out_norm, m_w_out_even, m_mix_norm_odd, m_w_in_odd, m_gla_w_gate_fwd, m_gla_b_gate_fwd, m_gla_w_gate_bwd, m_gla_b_gate_bwd, m_gla_out_norm, m_w_out_odd, m_ffn_norm, m_ffn_w_up, m_ffn_conv_w, m_ffn_conv_b, m_ffn_w_down, v_mix_norm_even, v_w_in_even, v_mla_q_norm, v_mla_kv_norm, v_mla_w_uq, v_mla_w_ukv, v_mla_q_head_norm, v_mla_k_head_norm, v_ret_theta_fwd, v_ret_theta_bwd, v_ret_out_norm, v_w_out_even, v_mix_norm_odd, v_w_in_odd, v_gla_w_gate_fwd, v_gla_b_gate_fwd, v_gla_w_gate_bwd, v_gla_b_gate_bwd, v_gla_out_norm, v_w_out_odd, v_ffn_norm, v_ffn_w_up, v_ffn_conv_w, v_ffn_conv_b, v_ffn_w_down):
    a = dict(locals())
    wts = {n: a[n] for n in WEIGHT_NAMES}
    local_shapes = {n: tuple(wts[n].shape) for n in WEIGHT_NAMES}

    gathered = all_gather_blocks(pack_gather_block(wts, local_shapes))
    fw = unpack_gathered(gathered, local_shapes)
    for n in REPLICATED:
        fw[n] = wts[n]

    loss, grad_x, grads = local_step(x[0], positions, loss_target[0], fw)

    mine, other = pack_grad_blocks(grads, local_shapes)
    chip_part = pair_add(mine, pair_exchange(other))
    g_slab = sum_slots(chip_exchange(chip_part))
    n_big = sum(_row_counts(local_shapes))
    ms = {n: a["m_" + n] for n in WEIGHT_NAMES}
    vs = {n: a["v_" + n] for n in WEIGHT_NAMES}

    g_out = unpack_rows(g_slab[:n_big], local_shapes)
    d_out, m_out, v_out = {}, {}, {}
    for n, _ in ROW_FORM:
        loc = local_shapes[n]
        two_d = lambda t: t.reshape(-1, loc[-1])
        d, m, v = adamw(two_d(wts[n]), two_d(g_out[n]), two_d(ms[n]), two_d(vs[n]), name=f"adamw_{n}")
        d_out[n], m_out[n], v_out[n] = d.reshape(loc), m.reshape(loc), v.reshape(loc)
    g_tail = g_slab[n_big:]
    d_tail, m_tail, v_tail = adamw(pack_tail(wts, local_shapes), g_tail, pack_tail(ms, local_shapes),
                                   pack_tail(vs, local_shapes), name="adamw_small")
    g_out.update(unpack_tail(g_tail, local_shapes))
    d_out.update(unpack_tail(d_tail, local_shapes))
    m_out.update(unpack_tail(m_tail, local_shapes))
    v_out.update(unpack_tail(v_tail, local_shapes))
    total = lax.psum(loss[0, 0], MESH_AXES)
    return (total, grad_x[None], *[g_out[n] for n in WEIGHT_NAMES], *[d_out[n] for n in WEIGHT_NAMES],
            *[m_out[n] for n in WEIGHT_NAMES], *[v_out[n] for n in WEIGHT_NAMES])
```

```python
import math

import jax
import jax.numpy as jnp
from jax import lax
from jax.experimental import pallas as pl
from jax.experimental.pallas import tpu as pltpu

F32 = jnp.float32
BF16 = jnp.bfloat16

D_MODEL = 1024
DEPTH = 4
N_DEV = 8
MESH_AXES = ("x", "y", "c")

MLA_HEADS = 8
MLA_Q_RANK = 384
MLA_KV_RANK = 256
MLA_NOPE = 64
MLA_ROPE = 32
MLA_V = 64
MLA_QK = MLA_NOPE + MLA_ROPE
RET_HEADS = 8
RET_DK = 64
RET_DV = 64
RET_CHUNK = 128
GLA_HEADS = 4
GLA_DK = 128
GLA_DV = 256
GLA_GATE_RANK = 16
GLA_TAU = 16.0
GLA_CHUNK = 64
D_FF = 2816
ROPE_THETA = 10000.0
EPS = 1e-6

ADAM_LR = 0.001
ADAM_B1 = 0.9
ADAM_B2 = 0.999
ADAM_EPS = 1e-08
ADAM_WD = 0.01
ADAM_STEP = 10

LANE = 128
SUBLANE = 8
ROW_TILE = 512
VMEM_LIMIT = 56 * 1024 * 1024
WEIGHT_TILE_BYTES = 8 * 1024 * 1024

EV_CQ, EV_CKV, EV_KR, EV_RQ, EV_RK, EV_RV, EV_RG, EV_IN = 0, 384, 640, 768, 1280, 1792, 2304, 2816
OD_Q, OD_K, OD_V, OD_R, OD_GA, OD_IN = 0, 512, 1024, 2048, 3072, 3200
N_GROUPS = 4


def _cparams(sem):
    return pltpu.CompilerParams(dimension_semantics=sem, vmem_limit_bytes=VMEM_LIMIT)


def _dot(a, b):
    return jnp.dot(a.astype(BF16), b.astype(BF16), preferred_element_type=F32)


def _dot_nt(a, b):
    return lax.dot_general(a.astype(BF16), b.astype(BF16), (((1,), (1,)), ((), ())), preferred_element_type=F32)


def _dot_tn(a, b):
    return lax.dot_general(a.astype(BF16), b.astype(BF16), (((0,), (0,)), ((), ())), preferred_element_type=F32)


def _sigmoid(x):
    return 1.0 / (1.0 + jnp.exp(-x))


def _col_tile(k, n, itemsize=2):
    best = LANE
    for t in range(LANE, n + 1, LANE):
        if n % t == 0 and k * t * itemsize <= WEIGHT_TILE_BYTES:
            best = t
    return best if n % LANE == 0 else n


def _row_tile(m):
    return min(ROW_TILE, m)


def mm_nn(a, b, res=None, out_dtype=F32, name="mm_nn"):
    m, k = a.shape
    n = b.shape[1]
    tm, tn = _row_tile(m), _col_tile(k, n)

    def body(*refs):
        if res is None:
            a_ref, b_ref, o_ref = refs
        else:
            a_ref, b_ref, r_ref, o_ref = refs
        acc = _dot(a_ref[...], b_ref[...])
        if res is not None:
            acc = acc + r_ref[...].astype(F32)
        o_ref[...] = acc.astype(out_dtype)

    in_specs = [pl.BlockSpec((tm, k), lambda j, i: (i, 0)), pl.BlockSpec((k, tn), lambda j, i: (0, j))]
    args = [a, b]
    if res is not None:
        in_specs.append(pl.BlockSpec((tm, tn), lambda j, i: (i, j)))
        args.append(res)
    return pl.pallas_call(
        body, name=name, grid=(n // tn, m // tm), in_specs=in_specs,
        out_specs=pl.BlockSpec((tm, tn), lambda j, i: (i, j)),
        out_shape=jax.ShapeDtypeStruct((m, n), out_dtype),
        compiler_params=_cparams(("parallel", "parallel")),
    )(*args)


def mm_tn(a, b, out_dtype=F32, name="mm_tn"):
    t, k = a.shape
    n = b.shape[1]
    tt = min(2 * ROW_TILE, t)
    tk = k if k <= 1024 else _col_tile(1024, k, 4)
    tn = n if n <= 1024 else _col_tile(1024, n, 4)

    def body(a_ref, b_ref, o_ref, acc_s):
        s = pl.program_id(2)

        @pl.when(s == 0)
        def _():
            acc_s[...] = jnp.zeros_like(acc_s)
        acc_s[...] += _dot_tn(a_ref[...], b_ref[...])

        @pl.when(s == pl.num_programs(2) - 1)
        def _():
            o_ref[...] = acc_s[...].astype(out_dtype)

    return pl.pallas_call(
        body, name=name, grid=(k // tk, n // tn, t // tt),
        in_specs=[pl.BlockSpec((tt, tk), lambda i, j, s: (s, i)), pl.BlockSpec((tt, tn), lambda i, j, s: (s, j))],
        out_specs=pl.BlockSpec((tk, tn), lambda i, j, s: (i, j)),
        out_shape=jax.ShapeDtypeStruct((k, n), out_dtype),
        scratch_shapes=[pltpu.VMEM((tk, tn), F32)],
        compiler_params=_cparams(("parallel", "parallel", "arbitrary")),
    )(a, b)


def rmsnorm_fwd(x, g, name="rmsnorm_fwd"):
    t, d = x.shape
    tm = _row_tile(t)

    def body(x_ref, g_ref, h_ref):
        xv = x_ref[...]
        r = lax.rsqrt(jnp.mean(xv * xv, axis=-1, keepdims=True) + EPS)
        h_ref[...] = (xv * r * g_ref[...]).astype(BF16)

    return pl.pallas_call(
        body, name=name, grid=(t // tm,),
        in_specs=[pl.BlockSpec((tm, d), lambda i: (i, 0)), pl.BlockSpec((1, d), lambda i: (0, 0))],
        out_specs=pl.BlockSpec((tm, d), lambda i: (i, 0)),
        out_shape=jax.ShapeDtypeStruct((t, d), BF16),
        compiler_params=_cparams(("parallel",)),
    )(x, g.reshape(1, d))


def rmsnorm_bwd(x, g, dh, dres, name="rmsnorm_bwd"):
    t, d = x.shape
    tm = _row_tile(t)

    def body(x_ref, g_ref, dh_ref, dres_ref, dx_ref, dg_ref):
        @pl.when(pl.program_id(0) == 0)
        def _():
            dg_ref[...] = jnp.zeros_like(dg_ref)
        xv = x_ref[...]
        r = lax.rsqrt(jnp.mean(xv * xv, axis=-1, keepdims=True) + EPS)
        xh = xv * r
        dhv = dh_ref[...]
        dg_ref[...] += jnp.sum(dhv * xh, axis=0, keepdims=True)
        dxh = dhv * g_ref[...]
        dx_ref[...] = dres_ref[...] + r * (dxh - xh * jnp.mean(dxh * xh, axis=-1, keepdims=True))

    row = pl.BlockSpec((tm, d), lambda i: (i, 0))
    vec = pl.BlockSpec((1, d), lambda i: (0, 0))
    return pl.pallas_call(
        body, name=name, grid=(t // tm,),
        in_specs=[row, vec, row, row], out_specs=[row, vec],
        out_shape=[jax.ShapeDtypeStruct((t, d), F32), jax.ShapeDtypeStruct((1, d), F32)],
        compiler_params=_cparams(("arbitrary",)),
    )(x, g.reshape(1, d), dh, dres)


def _rope_apply(x, cos, s1, s2, half):
    return x * cos + pltpu.roll(x, LANE - half, 1) * s1 + pltpu.roll(x, half, 1) * s2


def _rope_transpose(dy, cos, s1, s2, half):
    return dy * cos + pltpu.roll(dy * s1, half, 1) + pltpu.roll(dy * s2, LANE - half, 1)


def rope_tables(positions, lane_start, half, period):
    pos = positions.reshape(-1).astype(F32)
    inv = ROPE_THETA ** (-jnp.arange(half, dtype=F32) / half)
    ang = pos[:, None] * inv[None, :]
    cos, sin = jnp.cos(ang), jnp.sin(ang)
    t = pos.shape[0]
    pre = lane_start
    post = period - lane_start - 2 * half
    ones = lambda n: jnp.ones((t, n), F32)
    zeros = lambda n: jnp.zeros((t, n), F32)
    c = jnp.concatenate([ones(pre), cos, cos, ones(post)], axis=1)
    a = jnp.concatenate([zeros(pre), -sin, zeros(half), zeros(post)], axis=1)
    b = jnp.concatenate([zeros(pre), zeros(half), sin, zeros(post)], axis=1)
    rep = LANE // period
    return tuple(jnp.tile(v, (1, rep)) for v in (c, a, b))


def _mla_forward_tile(p, cos, s1, s2, qn_g, kvn_g, wuq, wk, wv, qhn, khn):
    cq = p[:, EV_CQ:EV_CKV]
    ckv = p[:, EV_CKV:EV_KR]
    kr = p[:, EV_KR:EV_RQ]
    rq = lax.rsqrt(jnp.mean(cq * cq, axis=-1, keepdims=True) + EPS)
    rkv = lax.rsqrt(jnp.mean(ckv * ckv, axis=-1, keepdims=True) + EPS)
    qn = cq * rq * qn_g
    kvn = ckv * rkv * kvn_g
    q_raw = _dot(qn, wuq)
    k_raw = _dot(kvn, wk)
    v = _dot(kvn, wv)
    krp = pltpu.roll(kr, MLA_NOPE, 1)
    return cq, ckv, rq, rkv, qn, kvn, q_raw, k_raw, v, krp


def _head_norm(xh, g):
    r = lax.rsqrt(jnp.sum(xh * xh, axis=-1, keepdims=True) * (1.0 / MLA_QK) + EPS)
    return xh * r * g, r


def mla_prep_fwd(p, tabs, qn_g, kvn_g, wuq, wk, wv, qhn, khn, name="mla_prep_fwd"):
    t = p.shape[0]
    tm = _row_tile(t)
    hw = MLA_HEADS * LANE

    def body(p_ref, c_ref, s1_ref, s2_ref, qn_ref, kvn_ref, wuq_ref, wk_ref, wv_ref, qhn_ref, khn_ref,
             q_out, k_out, v_out):
        cos, s1, s2 = c_ref[...], s1_ref[...], s2_ref[...]
        (_, _, _, _, _, _, q_raw, k_raw, v, krp) = _mla_forward_tile(
            p_ref[...], cos, s1, s2, qn_ref[...], kvn_ref[...], wuq_ref[...], wk_ref[...], wv_ref[...],
            qhn_ref[...], khn_ref[...])
        v_out[...] = v.astype(BF16)
        for h in range(MLA_HEADS):
            sl = slice(h * LANE, (h + 1) * LANE)
            qh, _ = _head_norm(q_raw[:, sl], qhn_ref[...])
            kh, _ = _head_norm(k_raw[:, sl] + krp, khn_ref[...])
            q_out[:, sl] = (_rope_apply(qh, cos, s1, s2, MLA_ROPE // 2) * ATTN_QSCALE).astype(BF16)
            k_out[:, sl] = _rope_apply(kh, cos, s1, s2, MLA_ROPE // 2).astype(BF16)

    row = lambda w: pl.BlockSpec((tm, w), lambda i: (i, 0))
    full = lambda a: pl.BlockSpec(a.shape, lambda i: (0,) * a.ndim)
    ws = [qn_g, kvn_g, wuq, wk, wv, qhn, khn]
    return pl.pallas_call(
        body, name=name, grid=(t // tm,),
        in_specs=[row(EV_RQ), row(LANE), row(LANE), row(LANE)] + [full(w) for w in ws],
        out_specs=[row(hw)] * 3,
        out_shape=[jax.ShapeDtypeStruct((t, hw), BF16)] * 3,
        compiler_params=_cparams(("parallel",)),
    )(p, *tabs, *ws)


def mla_prep_bwd(p, tabs, qn_g, kvn_g, wuq, wk, wv, wuq_t, wk_t, wv_t, qhn, khn, dq, dk, dv,
                 name="mla_prep_bwd"):
    t = p.shape[0]
    tm = _row_tile(t)
    hw = MLA_HEADS * LANE

    def body(p_ref, c_ref, s1_ref, s2_ref, qn_ref, kvn_ref, wuq_ref, wk_ref, wv_ref, wuqt_ref, wkt_ref, wvt_ref,
             qhn_ref, khn_ref, dq_ref, dk_ref, dv_ref,
             dp_ref, dwuq_ref, dwk_ref, dwv_ref, dqn_ref, dkvn_ref, dqhn_ref, dkhn_ref, dqraw_s, dkraw_s):
        @pl.when(pl.program_id(0) == 0)
        def _():
            for r in (dwuq_ref, dwk_ref, dwv_ref, dqn_ref, dkvn_ref, dqhn_ref, dkhn_ref):
                r[...] = jnp.zeros_like(r)
        cos, s1, s2 = c_ref[...], s1_ref[...], s2_ref[...]
        qhn_v, khn_v = qhn_ref[...], khn_ref[...]
        (cq, ckv, rq, rkv, qn, kvn, q_raw, k_raw, _, krp) = _mla_forward_tile(
            p_ref[...], cos, s1, s2, qn_ref[...], kvn_ref[...], wuq_ref[...], wk_ref[...], wv_ref[...],
            qhn_v, khn_v)
        half = MLA_ROPE // 2
        dkr_sum = jnp.zeros((tm, LANE), F32)
        dqhn_acc = jnp.zeros((1, LANE), F32)
        dkhn_acc = jnp.zeros((1, LANE), F32)
        for h in range(MLA_HEADS):
            sl = slice(h * LANE, (h + 1) * LANE)
            xq = q_raw[:, sl]
            _, r = _head_norm(xq, qhn_v)
            xh = xq * r
            dy = _rope_transpose(dq_ref[:, sl] * ATTN_SCALE, cos, s1, s2, half)
            dqhn_acc = dqhn_acc + jnp.sum(dy * xh, axis=0, keepdims=True)
            dxh = dy * qhn_v
            dqraw_s[:, sl] = r * (dxh - xh * (jnp.sum(dxh * xh, axis=-1, keepdims=True) * (1.0 / MLA_QK)))
            xk = k_raw[:, sl] + krp
            _, r = _head_norm(xk, khn_v)
            xh = xk * r
            dy = _rope_transpose(dk_ref[:, sl] * math.log(2.0), cos, s1, s2, half)
            dkhn_acc = dkhn_acc + jnp.sum(dy * xh, axis=0, keepdims=True)
            dxh = dy * khn_v
            dxk = r * (dxh - xh * (jnp.sum(dxh * xh, axis=-1, keepdims=True) * (1.0 / MLA_QK)))
            dkraw_s[:, sl] = dxk
            dkr_sum = dkr_sum + dxk
        dqhn_ref[...] += dqhn_acc
        dkhn_ref[...] += dkhn_acc
        dq_raw = dqraw_s[...]
        dk_raw = dkraw_s[...]
        dvv = dv_ref[...]
        dwuq_ref[...] += _dot_tn(qn, dq_raw)
        dwk_ref[...] += _dot_tn(kvn, dk_raw)
        dwv_ref[...] += _dot_tn(kvn, dvv)
        dqn = _dot(dq_raw, wuqt_ref[...])
        dkvn = _dot(dk_raw, wkt_ref[...]) + _dot(dvv, wvt_ref[...])
        xh = cq * rq
        dqn_ref[...] += jnp.sum(dqn * xh, axis=0, keepdims=True)
        dxh = dqn * qn_ref[...]
        dcq = rq * (dxh - xh * jnp.mean(dxh * xh, axis=-1, keepdims=True))
        dp_ref[:, EV_CQ:EV_CKV] = dcq.astype(BF16)
        xh = ckv * rkv
        dkvn_ref[...] += jnp.sum(dkvn * xh, axis=0, keepdims=True)
        dxh = dkvn * kvn_ref[...]
        dckv = rkv * (dxh - xh * jnp.mean(dxh * xh, axis=-1, keepdims=True))
        dp_ref[:, EV_CKV:EV_KR] = dckv.astype(BF16)
        lane = lax.broadcasted_iota(jnp.int32, (tm, LANE), 1)
        dkr = jnp.where(lane < MLA_ROPE, pltpu.roll(dkr_sum, LANE - MLA_NOPE, 1), 0.0)
        dp_ref[:, EV_KR:EV_RQ] = dkr.astype(BF16)

    row = lambda w: pl.BlockSpec((tm, w), lambda i: (i, 0))
    full = lambda a: pl.BlockSpec(a.shape, lambda i: (0,) * a.ndim)
    ws = [qn_g, kvn_g, wuq, wk, wv, wuq_t, wk_t, wv_t, qhn, khn]
    outs = [jax.ShapeDtypeStruct((t, EV_RQ), BF16), jax.ShapeDtypeStruct(wuq.shape, F32),
            jax.ShapeDtypeStruct(wk.shape, F32), jax.ShapeDtypeStruct(wv.shape, F32),
            jax.ShapeDtypeStruct(qn_g.shape, F32), jax.ShapeDtypeStruct(kvn_g.shape, F32),
            jax.ShapeDtypeStruct(qhn.shape, F32), jax.ShapeDtypeStruct(khn.shape, F32)]
    return pl.pallas_call(
        body, name=name, grid=(t // tm,),
        in_specs=[row(EV_RQ), row(LANE), row(LANE), row(LANE)] + [full(w) for w in ws] + [row(hw)] * 3,
        out_specs=[row(EV_RQ)] + [full(o) for o in outs[1:]],
        out_shape=outs,
        scratch_shapes=[pltpu.VMEM((tm, hw), F32), pltpu.VMEM((tm, hw), F32)],
        compiler_params=_cparams(("arbitrary",)),
    )(p, *tabs, *ws, dq, dk, dv)


ATTN_SCALE = MLA_QK ** -0.5
ATTN_QSCALE = ATTN_SCALE * math.log2(math.e)
ATTN_FWD_TQ, ATTN_FWD_TK = 512, 8192
ATTN_BWD_TQ, ATTN_BWD_TK = 512, 4096


def attn_fwd(q, k, v, name="attn_fwd"):
    t = q.shape[0]
    tq, tk = min(ATTN_FWD_TQ, _row_tile(t)), min(ATTN_FWD_TK, t)
    nh = MLA_HEADS

    def body(q_ref, k_ref, v_ref, o_ref, lse_ref, m_s, l_s, acc_s):
        j = pl.program_id(2)

        @pl.when(j == 0)
        def _():
            m_s[...] = jnp.full_like(m_s, -jnp.inf)
            l_s[...] = jnp.zeros_like(l_s)
            acc_s[...] = jnp.zeros_like(acc_s)

        s = _dot_nt(q_ref[...], k_ref[...])
        m_old = m_s[...]
        m_new = jnp.maximum(m_old, jnp.max(s, axis=-1, keepdims=True))
        pr = jnp.exp2(s - m_new)
        alpha = jnp.exp2(m_old - m_new)
        l_s[...] = alpha * l_s[...] + jnp.sum(pr, axis=-1, keepdims=True)
        acc_s[...] = alpha * acc_s[...] + _dot(pr, v_ref[...])
        m_s[...] = m_new

        @pl.when(j == pl.num_programs(2) - 1)
        def _():
            o_ref[...] = acc_s[...] / l_s[...]
            lse_ref[...] = m_s[...] + jnp.log2(l_s[...])

    return pl.pallas_call(
        body, name=name, grid=(nh, t // tq, t // tk),
        in_specs=[pl.BlockSpec((tq, LANE), lambda h, i, j: (i, h)),
                  pl.BlockSpec((tk, LANE), lambda h, i, j: (j, h)),
                  pl.BlockSpec((tk, LANE), lambda h, i, j: (j, h))],
        out_specs=[pl.BlockSpec((tq, LANE), lambda h, i, j: (i, h)),
                   pl.BlockSpec((None, tq, 1), lambda h, i, j: (h, i, 0))],
        out_shape=[jax.ShapeDtypeStruct((t, nh * LANE), F32), jax.ShapeDtypeStruct((nh, t, 1), F32)],
        scratch_shapes=[pltpu.VMEM((tq, 1), F32), pltpu.VMEM((tq, 1), F32), pltpu.VMEM((tq, LANE), F32)],
        compiler_params=_cparams(("parallel", "parallel", "arbitrary")),
    )(q, k, v)


def attn_bwd(q, k, v, o, lse, do, name="attn_bwd"):
    t = q.shape[0]
    tq, tk = min(ATTN_BWD_TQ, _row_tile(t)), min(ATTN_BWD_TK, t)
    nh = MLA_HEADS
    nq = t // tq

    def body(q_ref, k_ref, v_ref, o_ref, lse_ref, do_ref, dq_ref, dk_ref, dv_ref):
        kj, qi = pl.program_id(1), pl.program_id(2)

        @pl.when(qi == 0)
        def _():
            dk_ref[...] = jnp.zeros_like(dk_ref)
            dv_ref[...] = jnp.zeros_like(dv_ref)

        qv, kv, vv, dov = q_ref[...], k_ref[...], v_ref[...], do_ref[...]
        s = _dot_nt(qv, kv)
        pr = jnp.exp2(s - lse_ref[...])
        dp = _dot_nt(dov, vv)
        delta = jnp.sum(dov * o_ref[...], axis=-1, keepdims=True)
        ds = pr * (dp - delta)
        dv_ref[...] += _dot_tn(pr, dov)
        dk_ref[...] += _dot_tn(ds, qv)
        dq_tile = _dot(ds, kv)
        rows = pl.ds(pl.multiple_of(qi * tq, tq), tq)

        @pl.when(kj == 0)
        def _():
            dq_ref[rows, :] = dq_tile

        @pl.when(kj != 0)
        def _():
            dq_ref[rows, :] += dq_tile

    qspec = pl.BlockSpec((tq, LANE), lambda h, j, i: (i, h))
    kspec = pl.BlockSpec((tk, LANE), lambda h, j, i: (j, h))
    return pl.pallas_call(
        body, name=name, grid=(nh, t // tk, nq),
        in_specs=[qspec, kspec, kspec, qspec, pl.BlockSpec((None, tq, 1), lambda h, j, i: (h, i, 0)), qspec],
        out_specs=[pl.BlockSpec((t, LANE), lambda h, j, i: (0, h)), kspec, kspec],
        out_shape=[jax.ShapeDtypeStruct((t, nh * LANE), F32)] * 3,
        compiler_params=_cparams(("parallel", "arbitrary", "arbitrary")),
    )(q, k, v, o, lse, do)


def _scan_consts(c, reverse, inclusive):
    ii = lax.broadcasted_iota(jnp.int32, (c, c), 0)
    jj = lax.broadcasted_iota(jnp.int32, (c, c), 1)
    if reverse:
        incl = jj >= ii
        mask = incl if inclusive else jj > ii
    else:
        incl = jj <= ii
        mask = incl if inclusive else jj < ii
    mid = (c - 1 - c // 2) if reverse else c // 2
    incl_t = (jj <= ii) if reverse else (jj >= ii)
    return incl.astype(F32), incl_t.astype(F32), mask.astype(F32), mid


def _dot_split(a01, x):
    hi = x.astype(BF16)
    lo = (x - hi.astype(F32)).astype(BF16)
    a = a01.astype(BF16)
    return jnp.dot(a, hi, preferred_element_type=F32) + jnp.dot(a, lo, preferred_element_type=F32)


def _sub_masks(sub, dvg, u):
    if sub == 1:
        return None, None
    kl = lax.broadcasted_iota(jnp.int32, (1, LANE), 1)
    vl = lax.broadcasted_iota(jnp.int32, (1, dvg), 1)
    kw, vw = LANE // sub, dvg // sub
    km = (kl >= u * kw) & (kl < (u + 1) * kw)
    vm = (vl >= u * vw) & (vl < (u + 1) * vw)
    return km.astype(F32), vm.astype(F32)


def _block_incl(r, c, reverse, transposed):
    shift = c.bit_length() - 1
    ii = lax.broadcasted_iota(jnp.int32, (r, r), 0)
    jj = lax.broadcasted_iota(jnp.int32, (r, r), 1)
    same = lax.shift_right_logical(ii, shift) == lax.shift_right_logical(jj, shift)
    lower = (jj <= ii) if (reverse == transposed) else (jj >= ii)
    return (same & lower).astype(F32)


def _scan_chunk_fwd(b, la, mid):
    row = lax.broadcasted_iota(jnp.int32, b.shape, 0)
    bm = jnp.sum(jnp.where(row == mid, b, 0.0), axis=0, keepdims=True)
    tot = jnp.sum(la, axis=0, keepdims=True)
    e_qc = jnp.exp(b - bm)
    e_kc = jnp.exp(bm - b)
    e_qe = jnp.exp(b)
    e_kd = jnp.exp(tot - b)
    return e_qc, e_kc, e_qe, e_kd


def scan_fwd(q_arr, k_arr, v_arr, la_arr, *, qcb, kcb, vcb, lacb, la_row, chunk, dvg, sub, reverse, inclusive,
             qscale, kscale, rope=None, name="scan_fwd"):
    t = q_arr.shape[0]
    r = _row_tile(t)
    nb, nc = t // r, r // chunk
    c = chunk
    rb = (lambda j: nb - 1 - j) if reverse else (lambda j: j)
    order = list(range(nc))[::-1] if reverse else list(range(nc))
    half = RET_DK // 2

    def body(*refs):
        if rope is None:
            q_ref, k_ref, v_ref, la_ref, o_ref, st_ref, s_s = refs
        else:
            q_ref, k_ref, v_ref, la_ref, c_ref, s1_ref, s2_ref, o_ref, st_ref, s_s = refs

        @pl.when(pl.program_id(1) == 0)
        def _():
            s_s[...] = jnp.zeros_like(s_s)

        incl, _, mask, mid = _scan_consts(c, reverse, inclusive)
        for ci in order:
            rows = slice(ci * c, (ci + 1) * c)
            qv = q_ref[rows, :] * qscale
            kv = k_ref[rows, :] * kscale
            if rope is not None:
                cs, a1, a2 = c_ref[rows, :], s1_ref[rows, :], s2_ref[rows, :]
                qv = _rope_apply(qv, cs, a1, a2, half)
                kv = _rope_apply(kv, cs, a1, a2, half)
            la = jnp.broadcast_to(la_ref[...], (c, LANE)) if la_row else la_ref[rows, :]
            vv = v_ref[rows, :]
            e_qc, e_kc, e_qe, e_kd = _scan_chunk_fwd(_dot_split(incl, la), la, mid)
            qc, kc, qe, kd = qv * e_qc, kv * e_kc, qv * e_qe, kv * e_kd
            sg = s_s[...]
            st_ref[ci] = sg
            acc = None
            for u in range(sub):
                mu, vmu = _sub_masks(sub, dvg, u)
                qcu = qc if mu is None else qc * mu
                qeu = qe if mu is None else qe * mu
                a = _dot_nt(qcu, kc) * mask
                ou = _dot(a, vv) + _dot_nt(qeu, sg)
                ou = ou if vmu is None else ou * vmu
                acc = ou if acc is None else acc + ou
            o_ref[rows, :] = acc
            decay = jnp.exp(jnp.sum(la, axis=0, keepdims=True))
            s_s[...] = decay * sg + _dot_tn(vv, kd)

    specs = [pl.BlockSpec((r, LANE), lambda g, j: (rb(j), qcb + g)),
             pl.BlockSpec((r, LANE), lambda g, j: (rb(j), kcb + g)),
             pl.BlockSpec((r, dvg), lambda g, j: (rb(j), vcb + g)),
             pl.BlockSpec((1, LANE), lambda g, j: (0, lacb + g)) if la_row
             else pl.BlockSpec((r, LANE), lambda g, j: (rb(j), lacb + g))]
    args = [q_arr, k_arr, v_arr, la_arr]
    if rope is not None:
        specs += [pl.BlockSpec((r, LANE), lambda g, j: (rb(j), 0))] * 3
        args += list(rope)
    return pl.pallas_call(
        body, name=name, grid=(N_GROUPS, nb), in_specs=specs,
        out_specs=[pl.BlockSpec((r, dvg), lambda g, j: (rb(j), g)),
                   pl.BlockSpec((nc, dvg, LANE), lambda g, j: (rb(j), g, 0))],
        out_shape=[jax.ShapeDtypeStruct((t, N_GROUPS * dvg), F32),
                   jax.ShapeDtypeStruct((t // c, N_GROUPS * dvg, LANE), F32)],
        scratch_shapes=[pltpu.VMEM((dvg, LANE), F32)],
        compiler_params=_cparams(("parallel", "arbitrary")),
    )(*args)


def scan_bwd(q_arr, k_arr, v_arr, la_arr, st_arr, do_arr, prev, *, qcb, kcb, vcb, lacb, la_row, chunk, dvg, sub,
             reverse, inclusive, qscale, kscale, rope=None, block_rows=None, name="scan_bwd"):
    t = q_arr.shape[0]
    r = _row_tile(t) if block_rows is None else min(block_rows, t)
    nb, nc = t // r, r // chunk
    c = chunk
    rb = (lambda j: j) if reverse else (lambda j: nb - 1 - j)
    order = list(range(nc)) if reverse else list(range(nc))[::-1]
    half = RET_DK // 2
    n_in = 6 + (3 if rope is not None else 0) + (3 if prev is not None else 0)
    gdt = F32 if prev is None else BF16

    def body(*refs):
        ins, outs = refs[:n_in], refs[n_in:]
        q_ref, k_ref, v_ref, la_ref, st_ref, do_ref = ins[:6]
        pos = 6
        if rope is not None:
            c_ref, s1_ref, s2_ref = ins[pos:pos + 3]
            pos += 3
        if prev is not None:
            pq_ref, pk_ref, pv_ref = ins[pos:pos + 3]
        dq_ref, dk_ref, dv_ref, dla_ref, g_s = outs

        @pl.when(pl.program_id(1) == 0)
        def _():
            g_s[...] = jnp.zeros_like(g_s)
            if la_row:
                dla_ref[...] = jnp.zeros_like(dla_ref)

        incl, _, mask, mid = _scan_consts(c, reverse, inclusive)
        b_all = None if la_row else _dot_split(_block_incl(r, c, reverse, False), la_ref[...])
        pos = lax.broadcasted_iota(jnp.int32, (c, LANE), 0)
        cnt = ((c - pos) if reverse else (pos + 1)).astype(F32)
        dla_sum = jnp.zeros((1, LANE), F32)
        db_parts, dtot_parts = [None] * nc, [None] * nc
        for ci in order:
            rows = slice(ci * c, (ci + 1) * c)
            qv = q_ref[rows, :] * qscale
            kv = k_ref[rows, :] * kscale
            if rope is not None:
                cs, a1, a2 = c_ref[rows, :], s1_ref[rows, :], s2_ref[rows, :]
                qv = _rope_apply(qv, cs, a1, a2, half)
                kv = _rope_apply(kv, cs, a1, a2, half)
            la = jnp.broadcast_to(la_ref[...], (c, LANE)) if la_row else la_ref[rows, :]
            vv = v_ref[rows, :]
            dov = do_ref[rows, :]
            b = _dot_split(incl, la) if la_row else b_all[rows, :]
            e_qc, e_kc, e_qe, e_kd = _scan_chunk_fwd(b, la, mid)
            qc, kc, qe, kd = qv * e_qc, kv * e_kc, qv * e_qe, kv * e_kd
            sg = st_ref[ci]
            gn = g_s[...]
            dqc = jnp.zeros((c, LANE), F32)
            dkc = jnp.zeros((c, LANE), F32)
            dqe = jnp.zeros((c, LANE), F32)
            dvv = _dot_nt(kd, gn)
            ds_direct = jnp.zeros((dvg, LANE), F32)
            for u in range(sub):
                mu, vmu = _sub_masks(sub, dvg, u)
                qcu = qc if mu is None else qc * mu
                qeu = qe if mu is None else qe * mu
                dou = dov if vmu is None else dov * vmu
                a = _dot_nt(qcu, kc) * mask
                da = _dot_nt(dou, vv) * mask
                dvv = dvv + _dot_tn(a, dou)
                t1 = _dot(da, kc)
                dqc = dqc + (t1 if mu is None else t1 * mu)
                dkc = dkc + _dot_tn(da, qcu)
                t2 = _dot(dou, sg)
                dqe = dqe + (t2 if mu is None else t2 * mu)
                ds_direct = ds_direct + _dot_tn(dou, qeu)
            dkd = _dot(vv, gn)
            decay = jnp.exp(jnp.sum(la, axis=0, keepdims=True))
            dtot = jnp.sum(gn * sg, axis=0, keepdims=True) * decay + jnp.sum(dkd * kd, axis=0, keepdims=True)
            db = dqc * qc - dkc * kc + dqe * qe - dkd * kd
            if la_row:
                dla_sum = dla_sum + jnp.sum(db * cnt, axis=0, keepdims=True) + float(c) * dtot
            else:
                db_parts[ci] = db
                dtot_parts[ci] = jnp.broadcast_to(dtot, (c, LANE))
            dqv = dqc * e_qc + dqe * e_qe
            dkv = dkc * e_kc + dkd * e_kd
            if rope is not None:
                dqv = _rope_transpose(dqv, cs, a1, a2, half)
                dkv = _rope_transpose(dkv, cs, a1, a2, half)
            dqv = dqv * qscale
            dkv = dkv * kscale
            if prev is not None:
                dqv = dqv + pq_ref[rows, :]
                dkv = dkv + pk_ref[rows, :]
                dvv = dvv + pv_ref[rows, :]
            dq_ref[rows, :] = dqv.astype(gdt)
            dk_ref[rows, :] = dkv.astype(gdt)
            dv_ref[rows, :] = dvv.astype(gdt)
            g_s[...] = ds_direct + decay * gn
        if la_row:
            dla_ref[...] += dla_sum
        else:
            db_all = jnp.concatenate(db_parts, axis=0)
            dla_ref[...] = _dot_split(_block_incl(r, c, reverse, True), db_all) + jnp.concatenate(dtot_parts, axis=0)

    kblk = lambda cb: pl.BlockSpec((r, LANE), lambda g, j: (rb(j), cb + g))
    vblk = lambda cb: pl.BlockSpec((r, dvg), lambda g, j: (rb(j), cb + g))
    specs = [kblk(qcb), kblk(kcb), vblk(vcb),
             pl.BlockSpec((1, LANE), lambda g, j: (0, lacb + g)) if la_row else kblk(lacb),
             pl.BlockSpec((nc, dvg, LANE), lambda g, j: (rb(j), g, 0)), vblk(0)]
    args = [q_arr, k_arr, v_arr, la_arr, st_arr, do_arr]
    if rope is not None:
        specs += [pl.BlockSpec((r, LANE), lambda g, j: (rb(j), 0))] * 3
        args += list(rope)
    if prev is not None:
        specs += [kblk(0), kblk(0), vblk(0)]
        args += list(prev)
    wk = N_GROUPS * LANE
    outs = [jax.ShapeDtypeStruct((t, wk), gdt), jax.ShapeDtypeStruct((t, wk), gdt),
            jax.ShapeDtypeStruct((t, N_GROUPS * dvg), gdt),
            jax.ShapeDtypeStruct((1, wk) if la_row else (t, wk), F32)]
    return pl.pallas_call(
        body, name=name, grid=(N_GROUPS, nb), in_specs=specs,
        out_specs=[kblk(0), kblk(0), vblk(0),
                   pl.BlockSpec((1, LANE), lambda g, j: (0, g)) if la_row else kblk(0)],
        out_shape=outs,
        scratch_shapes=[pltpu.VMEM((dvg, LANE), F32)],
        compiler_params=_cparams(("parallel", "arbitrary")),
    )(*args)


def _seg_mean(x, seg):
    w = x.shape[1]
    if seg % LANE == 0:
        parts = []
        for s in range(0, w, seg):
            m = jnp.mean(x[:, s:s + seg], axis=-1, keepdims=True)
            parts.append(jnp.broadcast_to(m, (x.shape[0], seg)))
        return jnp.concatenate(parts, axis=1)
    shift = seg.bit_length() - 1
    ii = lax.shift_right_logical(lax.broadcasted_iota(jnp.int32, (w, w), 0), shift)
    jj = lax.shift_right_logical(lax.broadcasted_iota(jnp.int32, (w, w), 1), shift)
    e = (ii == jj).astype(BF16)
    hi = x.astype(BF16)
    lo = (x - hi.astype(F32)).astype(BF16)
    return (jnp.dot(hi, e, preferred_element_type=F32) + jnp.dot(lo, e, preferred_element_type=F32)) * (1.0 / seg)


def gated_norm_fwd(o_f, o_b, gate_arr, gcb, gn, seg, name="gated_norm_fwd"):
    t, w = o_f.shape
    tm = _row_tile(t)

    def body(of_ref, ob_ref, g_ref, gn_ref, y_ref):
        o = of_ref[...] + ob_ref[...]
        r = lax.rsqrt(_seg_mean(o * o, seg) + EPS)
        gt = g_ref[...]
        y_ref[...] = (gt * _sigmoid(gt) * (o * r * gn_ref[...])).astype(BF16)

    bw = max(seg, LANE)
    row = pl.BlockSpec((tm, bw), lambda j, i: (i, j))
    return pl.pallas_call(
        body, name=name, grid=(w // bw, t // tm),
        in_specs=[row, row, pl.BlockSpec((tm, bw), lambda j, i: (i, gcb + j)),
                  pl.BlockSpec((1, bw), lambda j, i: (0, j))],
        out_specs=row, out_shape=jax.ShapeDtypeStruct((t, w), BF16),
        compiler_params=_cparams(("parallel", "parallel")),
    )(o_f, o_b, gate_arr, gn.reshape(1, w))


def gated_norm_bwd(o_f, o_b, gate_arr, gcb, gn, seg, dy, name="gated_norm_bwd"):
    t, w = o_f.shape
    tm = _row_tile(t)

    def body(of_ref, ob_ref, g_ref, gn_ref, dy_ref, do_ref, dg_ref, dgn_ref):
        @pl.when(pl.program_id(1) == 0)
        def _():
            dgn_ref[...] = jnp.zeros_like(dgn_ref)
        o = of_ref[...] + ob_ref[...]
        r = lax.rsqrt(_seg_mean(o * o, seg) + EPS)
        xh = o * r
        gt = g_ref[...]
        sg = _sigmoid(gt)
        dyv = dy_ref[...]
        n = xh * gn_ref[...]
        dg_ref[...] = (dyv * n * (sg * (1.0 + gt * (1.0 - sg)))).astype(BF16)
        dn = dyv * (gt * sg)
        dgn_ref[...] += jnp.sum(dn * xh, axis=0, keepdims=True)
        dxh = dn * gn_ref[...]
        do_ref[...] = r * (dxh - xh * _seg_mean(dxh * xh, seg))

    bw = max(seg, LANE)
    row = pl.BlockSpec((tm, bw), lambda j, i: (i, j))
    vec = pl.BlockSpec((1, bw), lambda j, i: (0, j))
    return pl.pallas_call(
        body, name=name, grid=(w // bw, t // tm),
        in_specs=[row, row, pl.BlockSpec((tm, bw), lambda j, i: (i, gcb + j)), vec, row],
        out_specs=[row, row, vec],
        out_shape=[jax.ShapeDtypeStruct((t, w), F32), jax.ShapeDtypeStruct((t, w), BF16),
                   jax.ShapeDtypeStruct((1, w), F32)],
        compiler_params=_cparams(("parallel", "arbitrary")),
    )(o_f, o_b, gate_arr, gn.reshape(1, w), dy)


def gla_gate_fwd(p, wg, bg, name="gla_gate_fwd"):
    t = p.shape[0]
    tm = _row_tile(t)
    w = wg.shape[1]
    gcb = OD_GA // LANE

    def body(ga_ref, wg_ref, bg_ref, la_ref):
        z = _dot(ga_ref[...], wg_ref[...]) + bg_ref[...]
        la_ref[...] = (jnp.minimum(z, 0.0) - jnp.log(1.0 + jnp.exp(-jnp.abs(z)))) * (1.0 / GLA_TAU)

    return pl.pallas_call(
        body, name=name, grid=(t // tm,),
        in_specs=[pl.BlockSpec((tm, LANE), lambda i: (i, gcb)), pl.BlockSpec((LANE, w), lambda i: (0, 0)),
                  pl.BlockSpec((1, w), lambda i: (0, 0))],
        out_specs=pl.BlockSpec((tm, w), lambda i: (i, 0)),
        out_shape=jax.ShapeDtypeStruct((t, w), F32),
        compiler_params=_cparams(("parallel",)),
    )(p, wg, bg)


def gla_gate_bwd(p, wg, wg_t, bg, dla, name="gla_gate_bwd"):
    t = p.shape[0]
    tm = _row_tile(t)
    w = wg.shape[1]
    gcb = OD_GA // LANE

    def body(ga_ref, wg_ref, wgt_ref, bg_ref, dla_ref, dga_ref, dwg_ref, dbg_ref):
        @pl.when(pl.program_id(0) == 0)
        def _():
            dwg_ref[...] = jnp.zeros_like(dwg_ref)
            dbg_ref[...] = jnp.zeros_like(dbg_ref)
        ga = ga_ref[...]
        z = _dot(ga, wg_ref[...]) + bg_ref[...]
        dz = dla_ref[...] * (1.0 / GLA_TAU) * _sigmoid(-z)
        dga_ref[...] = _dot(dz, wgt_ref[...]).astype(BF16)
        dwg_ref[...] += _dot_tn(ga, dz)
        dbg_ref[...] += jnp.sum(dz, axis=0, keepdims=True)

    return pl.pallas_call(
        body, name=name, grid=(t // tm,),
        in_specs=[pl.BlockSpec((tm, LANE), lambda i: (i, gcb)), pl.BlockSpec((LANE, w), lambda i: (0, 0)),
                  pl.BlockSpec((w, LANE), lambda i: (0, 0)), pl.BlockSpec((1, w), lambda i: (0, 0)),
                  pl.BlockSpec((tm, w), lambda i: (i, 0))],
        out_specs=[pl.BlockSpec((tm, LANE), lambda i: (i, 0)), pl.BlockSpec((LANE, w), lambda i: (0, 0)),
                   pl.BlockSpec((1, w), lambda i: (0, 0))],
        out_shape=[jax.ShapeDtypeStruct((t, LANE), BF16), jax.ShapeDtypeStruct((LANE, w), F32),
                   jax.ShapeDtypeStruct((1, w), F32)],
        compiler_params=_cparams(("arbitrary",)),
    )(p, wg, wg_t, bg, dla)


FFN_COL = 1408


def _shifted(x, prev_row, next_row, first, last):
    tm = x.shape[0]
    row = lax.broadcasted_iota(jnp.int32, x.shape, 0)
    pr = jnp.where(first, 0.0, prev_row)
    nx = jnp.where(last, 0.0, next_row)
    xm1 = jnp.where(row == 0, pr, pltpu.roll(x, 1, 0))
    xp1 = jnp.where(row == tm - 1, nx, pltpu.roll(x, tm - 1, 0))
    return xm1, xp1


def _halo_rows(dtype):
    return SUBLANE * (4 // jnp.dtype(dtype).itemsize)


def _halo_specs(tm, tc, t, colmap, rowaxis, hr):
    nbh = tm // hr
    lasth = t // hr - 1

    def prev(*ids):
        i = ids[rowaxis]
        return (jnp.maximum(i * nbh - 1, 0), colmap(*ids))

    def nxt(*ids):
        i = ids[rowaxis]
        return (jnp.minimum((i + 1) * nbh, lasth), colmap(*ids))

    return pl.BlockSpec((hr, tc), prev), pl.BlockSpec((hr, tc), nxt)


def ffn_act_fwd(up, conv_w, conv_b, name="ffn_act_fwd"):
    t = up.shape[0]
    tm, tc = _row_tile(t), FFN_COL
    ncol = D_FF // tc

    hr = _halo_rows(up.dtype)

    def body(g_ref, gp_ref, gn_ref, v_ref, w_ref, b_ref, a_ref):
        i = pl.program_id(0)
        g = g_ref[...].astype(F32)
        gm1, gp1 = _shifted(g, gp_ref[hr - 1:hr, :].astype(F32), gn_ref[0:1, :].astype(F32), i == 0,
                            i == pl.num_programs(0) - 1)
        cc = w_ref[0:1, :] * gm1 + w_ref[1:2, :] * g + w_ref[2:3, :] * gp1 + b_ref[...]
        a_ref[...] = (cc * _sigmoid(cc) * v_ref[...].astype(F32)).astype(BF16)

    prev, nxt = _halo_specs(tm, tc, t, lambda i, j: j, 0, hr)
    return pl.pallas_call(
        body, name=name, grid=(t // tm, ncol),
        in_specs=[pl.BlockSpec((tm, tc), lambda i, j: (i, j)), prev, nxt,
                  pl.BlockSpec((tm, tc), lambda i, j: (i, j + ncol)),
                  pl.BlockSpec((SUBLANE, tc), lambda i, j: (0, j)), pl.BlockSpec((1, tc), lambda i, j: (0, j))],
        out_specs=pl.BlockSpec((tm, tc), lambda i, j: (i, j)),
        out_shape=jax.ShapeDtypeStruct((t, D_FF), BF16),
        compiler_params=_cparams(("parallel", "parallel")),
    )(up, up, up, up, conv_w, conv_b)


def ffn_act_bwd(up, conv_w, conv_b, dact, name="ffn_act_bwd"):
    t = up.shape[0]
    tm, tc = _row_tile(t), FFN_COL
    ncol = D_FF // tc
    hr = _halo_rows(up.dtype)

    def body(g_ref, gp_ref, gn_ref, v_ref, w_ref, b_ref, da_ref, dc_ref, dv_ref, dw_ref):
        i = pl.program_id(1)

        @pl.when(i == 0)
        def _():
            dw_ref[...] = jnp.zeros_like(dw_ref)
        g = g_ref[...].astype(F32)
        gm1, gp1 = _shifted(g, gp_ref[hr - 1:hr, :].astype(F32), gn_ref[0:1, :].astype(F32), i == 0,
                            i == pl.num_programs(1) - 1)
        cc = w_ref[0:1, :] * gm1 + w_ref[1:2, :] * g + w_ref[2:3, :] * gp1 + b_ref[...]
        sg = _sigmoid(cc)
        da = da_ref[...]
        dv_ref[...] = (da * (cc * sg)).astype(BF16)
        dc = da * v_ref[...].astype(F32) * (sg * (1.0 + cc * (1.0 - sg)))
        dc_ref[...] = dc
        dw_ref[0:1, :] += jnp.sum(dc * gm1, axis=0, keepdims=True)
        dw_ref[1:2, :] += jnp.sum(dc * g, axis=0, keepdims=True)
        dw_ref[2:3, :] += jnp.sum(dc * gp1, axis=0, keepdims=True)
        dw_ref[3:4, :] += jnp.sum(dc, axis=0, keepdims=True)

    prev, nxt = _halo_specs(tm, tc, t, lambda j, i: j, 1, hr)
    tile = pl.BlockSpec((tm, tc), lambda j, i: (i, j))
    return pl.pallas_call(
        body, name=name, grid=(ncol, t // tm),
        in_specs=[tile, prev, nxt, pl.BlockSpec((tm, tc), lambda j, i: (i, j + ncol)),
                  pl.BlockSpec((SUBLANE, tc), lambda j, i: (0, j)), pl.BlockSpec((1, tc), lambda j, i: (0, j)), tile],
        out_specs=[tile, tile, pl.BlockSpec((SUBLANE, tc), lambda j, i: (0, j))],
        out_shape=[jax.ShapeDtypeStruct((t, D_FF), F32), jax.ShapeDtypeStruct((t, D_FF), BF16),
                   jax.ShapeDtypeStruct((SUBLANE, D_FF), F32)],
        compiler_params=_cparams(("parallel", "arbitrary")),
    )(up, up, up, up, conv_w, conv_b, dact)


def conv_transpose(dc, conv_w, name="conv_transpose"):
    t = dc.shape[0]
    tm, tc = _row_tile(t), FFN_COL
    hr = _halo_rows(dc.dtype)

    def body(d_ref, dp_ref, dn_ref, w_ref, o_ref):
        i = pl.program_id(0)
        d = d_ref[...]
        dm1, dp1 = _shifted(d, dp_ref[hr - 1:hr, :], dn_ref[0:1, :], i == 0, i == pl.num_programs(0) - 1)
        o_ref[...] = (w_ref[0:1, :] * dp1 + w_ref[1:2, :] * d + w_ref[2:3, :] * dm1).astype(BF16)

    prev, nxt = _halo_specs(tm, tc, t, lambda i, j: j, 0, hr)
    tile = pl.BlockSpec((tm, tc), lambda i, j: (i, j))
    return pl.pallas_call(
        body, name=name, grid=(t // tm, D_FF // tc),
        in_specs=[tile, prev, nxt, pl.BlockSpec((SUBLANE, tc), lambda i, j: (0, j))],
        out_specs=tile, out_shape=jax.ShapeDtypeStruct((t, D_FF), BF16),
        compiler_params=_cparams(("parallel", "parallel")),
    )(dc, dc, dc, conv_w)


def loss_head(y, target, name="loss_head"):
    t, d = y.shape
    tm = _row_tile(t)

    def body(y_ref, t_ref, dy_ref, l_ref):
        @pl.when(pl.program_id(0) == 0)
        def _():
            l_ref[...] = jnp.zeros_like(l_ref)
        e = y_ref[...] - t_ref[...]
        dy_ref[...] = e * (1.0 / d)
        rowloss = jnp.sum(e * e, axis=-1, keepdims=True) * (0.5 / d)
        l_ref[...] += jnp.sum(rowloss, axis=0, keepdims=True)

    row = pl.BlockSpec((tm, d), lambda i: (i, 0))
    return pl.pallas_call(
        body, name=name, grid=(t // tm,), in_specs=[row, row],
        out_specs=[row, pl.BlockSpec((1, 1), lambda i: (0, 0))],
        out_shape=[jax.ShapeDtypeStruct((t, d), F32), jax.ShapeDtypeStruct((1, 1), F32)],
        compiler_params=_cparams(("arbitrary",)),
    )(y, target)


def _pad_heads(w, heads, width):
    lead = w.shape[:-1]
    w = w.reshape(*lead, heads, width)
    w = jnp.pad(w, [(0, 0)] * len(lead) + [(0, 0), (0, LANE - width)])
    return w.reshape(*lead, heads * LANE)


def _unpad_heads(w, heads, width):
    lead = w.shape[:-1]
    return w.reshape(*lead, heads, LANE)[..., :width].reshape(*lead, heads * width)


def _pad_rows_heads(w, heads, width):
    return _pad_heads(w.T, heads, width).T


def _unpad_rows_heads(w, heads, width):
    return _unpad_heads(w.T, heads, width).T


_EV_REAL = MLA_Q_RANK + MLA_KV_RANK + MLA_ROPE


def prep_even(wts, dt=BF16):
    w_in_t = wts["w_in_t"]
    w_in_tp = jnp.concatenate([w_in_t[:_EV_REAL], jnp.zeros((EV_RQ - _EV_REAL, D_MODEL), w_in_t.dtype),
                               w_in_t[_EV_REAL:]], axis=0).astype(dt)
    wuq = _pad_heads(wts["w_uq"], MLA_HEADS, MLA_QK).astype(dt)
    ukv = wts["w_ukv"].reshape(MLA_KV_RANK, MLA_HEADS, MLA_NOPE + MLA_V)
    wk = _pad_heads(ukv[..., :MLA_NOPE].reshape(MLA_KV_RANK, -1), MLA_HEADS, MLA_NOPE).astype(dt)
    wv = _pad_heads(ukv[..., MLA_NOPE:].reshape(MLA_KV_RANK, -1), MLA_HEADS, MLA_V).astype(dt)
    w_out = wts["w_out"]
    wa = _pad_rows_heads(w_out[:MLA_HEADS * MLA_V], MLA_HEADS, MLA_V).astype(dt)
    wr = w_out[MLA_HEADS * MLA_V:].astype(dt)
    pad1 = lambda v, n: jnp.pad(v.astype(F32), (0, n - v.shape[0])).reshape(1, n)
    lg = lambda th: jnp.log1p(-jnp.exp2(-th.astype(F32)))
    return dict(
        w_in=w_in_tp.T, w_in_t=w_in_tp, wuq=wuq, wuq_t=wuq.T, wk=wk, wk_t=wk.T, wv=wv, wv_t=wv.T,
        wa=wa, wa_t=wa.T, wr=wr, wr_t=wr.T,
        mix_norm=wts["mix_norm"].astype(F32), q_norm=wts["q_norm"].astype(F32).reshape(1, -1),
        kv_norm=wts["kv_norm"].astype(F32).reshape(1, -1),
        qhn=pad1(wts["q_head_norm"], LANE), khn=pad1(wts["k_head_norm"], LANE),
        la_f=jnp.repeat(lg(wts["theta_fwd"]), RET_DK).reshape(1, -1),
        la_b=jnp.repeat(lg(wts["theta_bwd"]), RET_DK).reshape(1, -1),
        out_norm=wts["ret_out_norm"].astype(F32).reshape(-1),
    )


def prep_odd(wts, dt=BF16):
    w_in_t = wts["w_in_t"]
    w_in_tp = jnp.concatenate([w_in_t, jnp.zeros((OD_IN - w_in_t.shape[0], D_MODEL), w_in_t.dtype)],
                              axis=0).astype(dt)
    hk = GLA_HEADS * GLA_DK
    wg = jnp.zeros((LANE, 2 * hk), F32)
    wg = wg.at[:GLA_GATE_RANK, :hk].set(wts["w_gate_fwd"].astype(F32))
    wg = wg.at[GLA_GATE_RANK:2 * GLA_GATE_RANK, hk:].set(wts["w_gate_bwd"].astype(F32))
    wg = wg.astype(dt)
    bg = jnp.concatenate([wts["b_gate_fwd"], wts["b_gate_bwd"]]).astype(F32).reshape(1, -1)
    w_out = wts["w_out"].astype(dt)
    return dict(w_in=w_in_tp.T, w_in_t=w_in_tp, wg=wg, wg_t=wg.T, bg=bg, w_out=w_out, w_out_t=w_out.T,
                mix_norm=wts["mix_norm"].astype(F32), out_norm=wts["gla_out_norm"].astype(F32).reshape(-1))


def prep_ffn(wts, dt=BF16):
    w_up_t = wts["w_up_t"].astype(dt)
    w_down = wts["w_down"].astype(dt)
    cw = jnp.pad(wts["conv_w"].astype(F32), ((0, SUBLANE - 3), (0, 0)))
    return dict(w_up=w_up_t.T, w_up_t=w_up_t, w_down=w_down, w_down_t=w_down.T, conv_w=cw,
                conv_b=wts["conv_b"].astype(F32).reshape(1, -1), norm=wts["norm"].astype(F32))


_RET = dict(qcb=EV_RQ // LANE, kcb=EV_RK // LANE, vcb=EV_RV // LANE, la_row=True, chunk=RET_CHUNK, dvg=LANE,
            sub=2, qscale=1.0, kscale=RET_DK ** -0.5)
_GLA = dict(qcb=OD_Q // LANE, kcb=OD_K // LANE, vcb=OD_V // GLA_DV, la_row=False, chunk=GLA_CHUNK, dvg=GLA_DV,
            sub=1, qscale=GLA_DK ** -0.5, kscale=1.0)
GLA_BWD_ROWS = 256
_FWD_DIR = dict(reverse=False, inclusive=True)
_BWD_DIR = dict(reverse=True, inclusive=False)


def even_fwd(x, w, tabs_mla, tabs_ret, tag):
    h = rmsnorm_fwd(x, w["mix_norm"], name=f"{tag}_norm")
    p = mm_nn(h, w["w_in"], name=f"{tag}_in")
    q, k, v = mla_prep_fwd(p, tabs_mla, w["q_norm"], w["kv_norm"], w["wuq"], w["wk"], w["wv"], w["qhn"], w["khn"],
                           name=f"{tag}_mla_prep")
    o, lse = attn_fwd(q, k, v, name=f"{tag}_attn")
    of, stf = scan_fwd(p, p, p, w["la_f"], lacb=0, rope=tabs_ret, name=f"{tag}_ret_f", **_RET, **_FWD_DIR)
    ob, stb = scan_fwd(p, p, p, w["la_b"], lacb=0, rope=tabs_ret, name=f"{tag}_ret_b", **_RET, **_BWD_DIR)
    r = gated_norm_fwd(of, ob, p, EV_RG // LANE, w["out_norm"], RET_DV, name=f"{tag}_ret_out")
    x1 = mm_nn(o, w["wa"], res=x, name=f"{tag}_out_a")
    x2 = mm_nn(r, w["wr"], res=x1, name=f"{tag}_out_r")
    return x2, dict(x=x, h=h, p=p, q=q, k=k, v=v, o=o, lse=lse, of=of, ob=ob, stf=stf, stb=stb, r=r)


def even_bwd(dx, s, w, tabs_mla, tabs_ret, tag):
    tag = tag + "_b"
    do = mm_nn(dx, w["wa_t"], name=f"{tag}_dout_a")
    dr = mm_nn(dx, w["wr_t"], name=f"{tag}_dout_r")
    d_wa = mm_tn(s["o"], dx, out_dtype=BF16, name=f"{tag}_dwa")
    d_wr = mm_tn(s["r"], dx, out_dtype=BF16, name=f"{tag}_dwr")
    dq, dk, dv = attn_bwd(s["q"], s["k"], s["v"], s["o"], s["lse"], do, name=f"{tag}_attn")
    (dp_mla, d_wuq, d_wk, d_wv, d_qn, d_kvn, d_qhn, d_khn) = mla_prep_bwd(
        s["p"], tabs_mla, w["q_norm"], w["kv_norm"], w["wuq"], w["wk"], w["wv"], w["wuq_t"], w["wk_t"], w["wv_t"],
        w["qhn"], w["khn"], dq, dk, dv, name=f"{tag}_mla_prep")
    d_o, d_gate, d_gn = gated_norm_bwd(s["of"], s["ob"], s["p"], EV_RG // LANE, w["out_norm"], RET_DV, dr,
                                       name=f"{tag}_ret_out")
    p = s["p"]
    g1 = scan_bwd(p, p, p, w["la_f"], s["stf"], d_o, None, lacb=0, rope=tabs_ret, name=f"{tag}_ret_f",
                  **_RET, **_FWD_DIR)
    g2 = scan_bwd(p, p, p, w["la_b"], s["stb"], d_o, g1[:3], lacb=0, rope=tabs_ret, name=f"{tag}_ret_b",
                  **_RET, **_BWD_DIR)
    dp = jnp.concatenate([dp_mla, g2[0], g2[1], g2[2], d_gate], axis=1)
    dh = mm_nn(dp, w["w_in_t"], name=f"{tag}_dh")
    d_win_t = mm_tn(dp, s["h"], out_dtype=BF16, name=f"{tag}_dwin")
    dx_in, d_mix = rmsnorm_bwd(s["x"], w["mix_norm"], dh, dx, name=f"{tag}_norm")
    grads = dict(w_in_t=d_win_t, wuq=d_wuq, wk=d_wk, wv=d_wv, wa=d_wa, wr=d_wr, mix_norm=d_mix, q_norm=d_qn,
                 kv_norm=d_kvn, qhn=d_qhn, khn=d_khn, la_f=g1[3], la_b=g2[3], out_norm=d_gn)
    return dx_in, grads


def odd_fwd(x, w, tag):
    h = rmsnorm_fwd(x, w["mix_norm"], name=f"{tag}_norm")
    p = mm_nn(h, w["w_in"], name=f"{tag}_in")
    la = gla_gate_fwd(p, w["wg"], w["bg"], name=f"{tag}_gate")
    of, stf = scan_fwd(p, p, p, la, lacb=0, name=f"{tag}_gla_f", **_GLA, **_FWD_DIR)
    ob, stb = scan_fwd(p, p, p, la, lacb=N_GROUPS, name=f"{tag}_gla_b", **_GLA, **_BWD_DIR)
    y = gated_norm_fwd(of, ob, p, OD_R // GLA_DV, w["out_norm"], GLA_DV, name=f"{tag}_gla_out")
    x1 = mm_nn(y, w["w_out"], res=x, name=f"{tag}_out")
    return x1, dict(x=x, h=h, p=p, la=la, of=of, ob=ob, stf=stf, stb=stb, y=y)


def odd_bwd(dx, s, w, tag):
    tag = tag + "_b"
    dy = mm_nn(dx, w["w_out_t"], name=f"{tag}_dout")
    d_wout = mm_tn(s["y"], dx, out_dtype=BF16, name=f"{tag}_dwout")
    d_o, d_gate, d_gn = gated_norm_bwd(s["of"], s["ob"], s["p"], OD_R // GLA_DV, w["out_norm"], GLA_DV, dy,
                                       name=f"{tag}_gla_out")
    p, la = s["p"], s["la"]
    g1 = scan_bwd(p, p, p, la, s["stf"], d_o, None, lacb=0, block_rows=GLA_BWD_ROWS, name=f"{tag}_gla_f",
                  **_GLA, **_FWD_DIR)
    g2 = scan_bwd(p, p, p, la, s["stb"], d_o, g1[:3], lacb=N_GROUPS, block_rows=GLA_BWD_ROWS, name=f"{tag}_gla_b",
                  **_GLA, **_BWD_DIR)
    dla = jnp.concatenate([g1[3], g2[3]], axis=1)
    d_ga, d_wg, d_bg = gla_gate_bwd(p, w["wg"], w["wg_t"], w["bg"], dla, name=f"{tag}_gate")
    dp = jnp.concatenate([g2[0], g2[1], g2[2], d_gate, d_ga], axis=1)
    dh = mm_nn(dp, w["w_in_t"], name=f"{tag}_dh")
    d_win_t = mm_tn(dp, s["h"], out_dtype=BF16, name=f"{tag}_dwin")
    dx_in, d_mix = rmsnorm_bwd(s["x"], w["mix_norm"], dh, dx, name=f"{tag}_norm")
    grads = dict(w_in_t=d_win_t, wg=d_wg, bg=d_bg, w_out=d_wout, mix_norm=d_mix, out_norm=d_gn)
    return dx_in, grads


def ffn_fwd(x, w, tag):
    h = rmsnorm_fwd(x, w["norm"], name=f"{tag}_norm")
    up = mm_nn(h, w["w_up"], out_dtype=BF16, name=f"{tag}_up")
    act = ffn_act_fwd(up, w["conv_w"], w["conv_b"], name=f"{tag}_act")
    x1 = mm_nn(act, w["w_down"], res=x, name=f"{tag}_down")
    return x1, dict(x=x, h=h, up=up, act=act)


def ffn_bwd(dx, s, w, tag):
    tag = tag + "_b"
    dact = mm_nn(dx, w["w_down_t"], name=f"{tag}_dact")
    d_wdown = mm_tn(s["act"], dx, out_dtype=BF16, name=f"{tag}_dwdown")
    dc, dval, d_conv = ffn_act_bwd(s["up"], w["conv_w"], w["conv_b"], dact, name=f"{tag}_act")
    dgate = conv_transpose(dc, w["conv_w"], name=f"{tag}_convt")
    dh1 = mm_nn(dgate, w["w_up_t"][:D_FF], name=f"{tag}_dh_g")
    dh = mm_nn(dval, w["w_up_t"][D_FF:], res=dh1, name=f"{tag}_dh_v")
    d_wup_t = jnp.concatenate([mm_tn(dgate, s["h"], out_dtype=BF16, name=f"{tag}_dwup_g"),
                               mm_tn(dval, s["h"], out_dtype=BF16, name=f"{tag}_dwup_v")], axis=0)
    dx_in, d_norm = rmsnorm_bwd(s["x"], w["norm"], dh, dx, name=f"{tag}_norm")
    grads = dict(w_up_t=d_wup_t, w_down=d_wdown, conv_w=d_conv[:3], conv_b=d_conv[3], norm=d_norm)
    return dx_in, grads


def unprep_even_grads(g, theta_fwd, theta_bwd):
    d_win_t = jnp.concatenate([g["w_in_t"][:_EV_REAL], g["w_in_t"][EV_RQ:]], axis=0)
    d_uq = _unpad_heads(g["wuq"], MLA_HEADS, MLA_QK)
    dk_ = _unpad_heads(g["wk"], MLA_HEADS, MLA_NOPE).reshape(MLA_KV_RANK, MLA_HEADS, MLA_NOPE)
    dv_ = _unpad_heads(g["wv"], MLA_HEADS, MLA_V).reshape(MLA_KV_RANK, MLA_HEADS, MLA_V)
    d_ukv = jnp.concatenate([dk_, dv_], axis=-1).reshape(MLA_KV_RANK, -1)
    d_wout = jnp.concatenate([_unpad_rows_heads(g["wa"], MLA_HEADS, MLA_V), g["wr"]], axis=0)

    def dtheta(dla, th):
        dlg = dla.reshape(RET_HEADS, RET_DK).sum(axis=-1)
        e = jnp.exp2(-th.astype(F32))
        return dlg * (e * math.log(2.0)) / (1.0 - e)

    return dict(mix_norm=g["mix_norm"].reshape(-1), w_in_t=d_win_t, q_norm=g["q_norm"].reshape(-1),
                kv_norm=g["kv_norm"].reshape(-1), w_uq=d_uq, w_ukv=d_ukv, q_head_norm=g["qhn"].reshape(-1)[:MLA_QK],
                k_head_norm=g["khn"].reshape(-1)[:MLA_QK], theta_fwd=dtheta(g["la_f"], theta_fwd),
                theta_bwd=dtheta(g["la_b"], theta_bwd), ret_out_norm=g["out_norm"].reshape(RET_HEADS, RET_DV),
                w_out=d_wout)


def unprep_odd_grads(g):
    hk = GLA_HEADS * GLA_DK
    return dict(mix_norm=g["mix_norm"].reshape(-1), w_in_t=g["w_in_t"][:OD_GA + 2 * GLA_GATE_RANK],
                w_gate_fwd=g["wg"][:GLA_GATE_RANK, :hk], b_gate_fwd=g["bg"].reshape(-1)[:hk],
                w_gate_bwd=g["wg"][GLA_GATE_RANK:2 * GLA_GATE_RANK, hk:], b_gate_bwd=g["bg"].reshape(-1)[hk:],
                gla_out_norm=g["out_norm"].reshape(GLA_HEADS, GLA_DV), w_out=g["w_out"])


def unprep_ffn_grads(g):
    return dict(norm=g["norm"].reshape(-1), w_up_t=g["w_up_t"], conv_w=g["conv_w"], conv_b=g["conv_b"],
                w_down=g["w_down"])


def _mesh_pos():
    return tuple(lax.axis_index(n) for n in MESH_AXES)


def _slot(px, py, pc):
    return 4 * px + 2 * py + pc


def all_gather_blocks(blk, name="weight_all_gather"):
    r, w = blk.shape

    def body(x_ref, out_ref, send_sems, recv_sems, local_sem):
        x, y, c = _mesh_pos()
        me, sibling = (x, y, c), (x, y, 1 - c)
        chips = [(1 - x, y), (x, 1 - y), (1 - x, 1 - y)]

        def copy(k, block, to, src=None):
            dst = out_ref.at[_slot(*block)]
            return pltpu.make_async_remote_copy(
                src_ref=dst if src is None else src, dst_ref=dst, send_sem=send_sems.at[k],
                recv_sem=recv_sems.at[k], device_id=to, device_id_type=pl.DeviceIdType.MESH)

        mine = pltpu.make_async_copy(x_ref, out_ref.at[_slot(*me)], local_sem)
        mine.start()
        first = [copy(0, me, sibling, src=x_ref)]
        first += [copy(1 + j, me, (*chip, c), src=x_ref) for j, chip in enumerate(chips)]
        for cp in first:
            cp.start()
        passed = [copy(4 + j, (*chip, c), sibling) for j, chip in enumerate(chips)]
        for j, chip in enumerate(chips):
            copy(1 + j, (*chip, c), me).wait_recv()
            passed[j].start()
        copy(0, sibling, me).wait_recv()
        for j, chip in enumerate(chips):
            copy(4 + j, (*chip, 1 - c), me).wait_recv()
        for cp in first + passed:
            cp.wait_send()
        mine.wait()

    return pl.pallas_call(
        body, name=name,
        out_shape=jax.ShapeDtypeStruct((N_DEV, r, w), blk.dtype),
        in_specs=[pl.BlockSpec(memory_space=pl.ANY)],
        out_specs=pl.BlockSpec(memory_space=pl.ANY),
        scratch_shapes=[pltpu.SemaphoreType.DMA((7,)), pltpu.SemaphoreType.DMA((7,)), pltpu.SemaphoreType.DMA],
    )(blk)


N_CHIP = 4


def pair_exchange(other, name="grad_pair_exchange"):
    _, r, w = other.shape

    def body(s_ref, r_ref, send_sems, recv_sems):
        x, y, c = _mesh_pos()
        copies = []
        for s in range(N_CHIP):
            cp = pltpu.make_async_remote_copy(
                src_ref=s_ref.at[s], dst_ref=r_ref.at[s], send_sem=send_sems.at[s], recv_sem=recv_sems.at[s],
                device_id=(x, y, 1 - c), device_id_type=pl.DeviceIdType.MESH)
            cp.start()
            copies.append(cp)
        for cp in copies:
            cp.wait()

    return pl.pallas_call(
        body, name=name,
        out_shape=jax.ShapeDtypeStruct((N_CHIP, r, w), other.dtype),
        in_specs=[pl.BlockSpec(memory_space=pl.ANY)],
        out_specs=pl.BlockSpec(memory_space=pl.ANY),
        scratch_shapes=[pltpu.SemaphoreType.DMA((N_CHIP,)), pltpu.SemaphoreType.DMA((N_CHIP,))],
    )(other)


def chip_exchange(part, name="grad_chip_exchange"):
    _, r, w = part.shape

    def body(s_ref, r_ref, send_sems, recv_sems, local_sem):
        x, y, c = _mesh_pos()
        me = 2 * x + y
        mine = pltpu.make_async_copy(s_ref.at[me], r_ref.at[me], local_sem)
        mine.start()
        copies = []
        for k in range(1, N_CHIP):
            px = 1 - x if (k >> 1) & 1 else x
            py = 1 - y if k & 1 else y
            cp = pltpu.make_async_remote_copy(
                src_ref=s_ref.at[2 * px + py], dst_ref=r_ref.at[me], send_sem=send_sems.at[k - 1],
                recv_sem=recv_sems.at[k - 1], device_id=(px, py, c), device_id_type=pl.DeviceIdType.MESH)
            cp.start()
            copies.append(cp)
        for cp in copies:
            cp.wait()
        mine.wait()

    return pl.pallas_call(
        body, name=name,
        out_shape=jax.ShapeDtypeStruct((N_CHIP, r, w), part.dtype),
        in_specs=[pl.BlockSpec(memory_space=pl.ANY)],
        out_specs=pl.BlockSpec(memory_space=pl.ANY),
        scratch_shapes=[pltpu.SemaphoreType.DMA((N_CHIP - 1,)), pltpu.SemaphoreType.DMA((N_CHIP - 1,)),
                        pltpu.SemaphoreType.DMA],
    )(part)


FLAT_W = 1024
FLAT_TILE = 256


def pair_add(mine, theirs, name="grad_pair_add"):
    n, r, w = mine.shape

    def body(a_ref, b_ref, o_ref):
        o_ref[...] = (a_ref[...].astype(F32) + b_ref[...].astype(F32)).astype(o_ref.dtype)

    tr = _slab_tile(r)
    blk = pl.BlockSpec((n, tr, w), lambda i: (0, i, 0))
    return pl.pallas_call(
        body, name=name, grid=(r // tr,), in_specs=[blk, blk], out_specs=blk,
        out_shape=jax.ShapeDtypeStruct((n, r, w), mine.dtype),
        compiler_params=_cparams(("parallel",)),
    )(mine, theirs)


def sum_slots(recv, name="grad_sum"):
    n, r, w = recv.shape

    def body(r_ref, o_ref):
        acc = r_ref[0].astype(F32)
        for k in range(1, n):
            acc = acc + r_ref[k].astype(F32)
        o_ref[...] = acc

    tr = _slab_tile(r)
    return pl.pallas_call(
        body, name=name, grid=(r // tr,),
        in_specs=[pl.BlockSpec((n, tr, w), lambda i: (0, i, 0))],
        out_specs=pl.BlockSpec((tr, w), lambda i: (i, 0)),
        out_shape=jax.ShapeDtypeStruct((r, w), F32),
        compiler_params=_cparams(("parallel",)),
    )(recv)


def _slab_tile(r):
    return max(t for t in range(SUBLANE, FLAT_TILE + 1, SUBLANE) if r % t == 0)


def adamw(wf, gf, mf, vf, name="adamw"):
    r, w = wf.shape
    tr = _slab_tile(r)

    def body(w_ref, g_ref, m_ref, v_ref, d_ref, m_out, v_out):
        g = g_ref[...]
        m = ADAM_B1 * m_ref[...] + (1.0 - ADAM_B1) * g
        v = ADAM_B2 * v_ref[...] + (1.0 - ADAM_B2) * (g * g)
        m_hat = m / (1.0 - ADAM_B1 ** ADAM_STEP)
        v_hat = v / (1.0 - ADAM_B2 ** ADAM_STEP)
        d_ref[...] = -ADAM_LR * (m_hat / (jnp.sqrt(v_hat) + ADAM_EPS) + ADAM_WD * w_ref[...])
        m_out[...] = m
        v_out[...] = v

    tile = pl.BlockSpec((tr, w), lambda i: (i, 0))
    return pl.pallas_call(
        body, name=name, grid=(r // tr,), in_specs=[tile] * 4, out_specs=[tile] * 3,
        out_shape=[jax.ShapeDtypeStruct((r, w), F32)] * 3,
        compiler_params=_cparams(("parallel",)),
    )(wf, gf, mf, vf)


ROW_FORM = [("w_in_even", "T"), ("w_out_even", "R"), ("w_in_odd", "T"), ("w_out_odd", "R"), ("ffn_w_up", "T"),
            ("ffn_w_down", "R")]
SHARDED_MID = [("mla_w_uq", 2), ("mla_w_ukv", 2)]
SHARDED_SMALL = [("mix_norm_odd", 1), ("gla_w_gate_fwd", 2), ("gla_b_gate_fwd", 1), ("gla_w_gate_bwd", 2),
                 ("gla_b_gate_bwd", 1), ("gla_out_norm", 2), ("ffn_conv_w", 2)]
REPLICATED = ["mix_norm_even", "mla_q_norm", "mla_kv_norm", "mla_q_head_norm", "mla_k_head_norm", "ret_theta_fwd",
              "ret_theta_bwd", "ret_out_norm", "ffn_norm", "ffn_conv_b"]
WEIGHT_NAMES = ["mix_norm_even", "w_in_even", "mla_q_norm", "mla_kv_norm", "mla_w_uq", "mla_w_ukv",
                "mla_q_head_norm", "mla_k_head_norm", "ret_theta_fwd", "ret_theta_bwd", "ret_out_norm", "w_out_even",
                "mix_norm_odd", "w_in_odd", "gla_w_gate_fwd", "gla_b_gate_fwd", "gla_w_gate_bwd", "gla_b_gate_bwd",
                "gla_out_norm", "w_out_odd", "ffn_norm", "ffn_w_up", "ffn_conv_w", "ffn_conv_b", "ffn_w_down"]


def _round_up(n, m):
    return -(-n // m) * m


def _pack_rows(parts, rows):
    flat = jnp.concatenate(parts, axis=-1)
    pad = rows * FLAT_W - flat.shape[-1]
    flat = jnp.pad(flat, [(0, 0)] * (flat.ndim - 1) + [(0, pad)])
    return flat.reshape(*flat.shape[:-1], rows, FLAT_W)


def _row_form(v, form):
    if form == "T":
        v = jnp.swapaxes(v, 1, 2)
    return v.reshape(-1, v.shape[-1])


def _row_counts(local_shapes):
    return [local_shapes[n][0] * local_shapes[n][2 if f == "T" else 1] for n, f in ROW_FORM]


def _tail_layout(local_shapes):
    n_sh = sum(math.prod(local_shapes[n]) for n, _ in SHARDED_MID + SHARDED_SMALL)
    n_rep = sum(math.prod(local_shapes[n]) for n in REPLICATED)
    sh_rows = _round_up(-(-n_sh // FLAT_W), SUBLANE)
    rep_rows = _round_up(-(-n_rep // FLAT_W), SUBLANE)
    return sh_rows, rep_rows, _round_up(sh_rows + rep_rows, FLAT_TILE)


def pack_tail(vals, local_shapes):
    sh_rows, rep_rows, rows = _tail_layout(local_shapes)
    sh = _pack_rows([vals[n].astype(F32).reshape(-1) for n, _ in SHARDED_MID + SHARDED_SMALL], sh_rows)
    rep = _pack_rows([vals[n].astype(F32).reshape(-1) for n in REPLICATED], rep_rows)
    return jnp.concatenate([sh, rep, jnp.zeros((rows - sh_rows - rep_rows, FLAT_W), F32)], axis=0)


def unpack_tail(tail, local_shapes):
    sh_rows, rep_rows, _ = _tail_layout(local_shapes)
    out = {}
    for names, flat in (([n for n, _ in SHARDED_MID + SHARDED_SMALL], tail[:sh_rows].reshape(-1)),
                        (REPLICATED, tail[sh_rows:sh_rows + rep_rows].reshape(-1))):
        off = 0
        for n in names:
            k = math.prod(local_shapes[n])
            out[n] = flat[off:off + k].reshape(local_shapes[n])
            off += k
    return out


def unpack_rows(slab, local_shapes):
    out = {}
    off = 0
    for (n, form), rows in zip(ROW_FORM, _row_counts(local_shapes)):
        loc = local_shapes[n]
        piece = slab[off:off + rows]
        if form == "T":
            piece = jnp.swapaxes(piece.reshape(loc[0], loc[2], loc[1]), 1, 2)
        out[n] = piece.reshape(loc)
        off += rows
    return out


def pack_grad_blocks(full_grads, local_shapes):
    sh_rows, rep_rows, rows = _tail_layout(local_shapes)
    my_c = lax.axis_index("c")

    def by_core(blocks8):
        b = blocks8.reshape(N_CHIP, 2, *blocks8.shape[1:])
        return (lax.dynamic_index_in_dim(b, my_c, 1, keepdims=False),
                lax.dynamic_index_in_dim(b, 1 - my_c, 1, keepdims=False))

    blocks = []
    for n, form in ROW_FORM:
        g = full_grads[n].astype(BF16)
        layers, total = g.shape[0], g.shape[1]
        g = g.reshape(layers, N_DEV, total // N_DEV, FLAT_W)
        blocks.append(by_core(jnp.swapaxes(g, 0, 1).reshape(N_DEV, -1, FLAT_W)))
    parts = []
    for n, ax in SHARDED_MID + SHARDED_SMALL:
        g = full_grads[n].astype(F32)
        loc = local_shapes[n]
        g = g.reshape(*g.shape[:ax], N_DEV, loc[ax], *g.shape[ax + 1:])
        parts.append(jnp.moveaxis(g, ax, 0).reshape(N_DEV, -1))
    sh = _pack_rows(parts, sh_rows)
    rep = _pack_rows([full_grads[n].astype(F32).reshape(-1) for n in REPLICATED], rep_rows)
    rep = jnp.broadcast_to(rep[None], (N_DEV, rep_rows, FLAT_W))
    pad = jnp.zeros((N_DEV, rows - sh_rows - rep_rows, FLAT_W), F32)
    blocks.append(by_core(jnp.concatenate([sh, rep, pad], axis=1).astype(BF16)))
    return (jnp.concatenate([b[0] for b in blocks], axis=1), jnp.concatenate([b[1] for b in blocks], axis=1))


def pack_gather_block(vals, local_shapes):
    big = jnp.concatenate([_row_form(vals[n].astype(BF16), f) for n, f in ROW_FORM], axis=0)
    mid = [vals[n].astype(BF16).reshape(-1) for n, _ in SHARDED_MID]
    small = jnp.concatenate([vals[n].astype(F32).reshape(-1) for n, _ in SHARDED_SMALL])
    small = lax.bitcast_convert_type(small, BF16).reshape(-1)
    n = sum(v.shape[0] for v in mid) + small.shape[0]
    tail = _pack_rows(mid + [small], _round_up(-(-n // FLAT_W), 2 * SUBLANE))
    return jnp.concatenate([big, tail], axis=0)


def unpack_gathered(gathered, local_shapes):
    out = {}
    off = 0
    for (n, form), rows in zip(ROW_FORM, _row_counts(local_shapes)):
        layers = local_shapes[n][0]
        piece = gathered[:, off:off + rows].reshape(N_DEV, layers, rows // layers, FLAT_W)
        out[n] = jnp.swapaxes(piece, 0, 1).reshape(layers, N_DEV * (rows // layers), FLAT_W)
        off += rows
    flat = gathered[:, off:].reshape(N_DEV, -1)
    off = 0

    def full(piece, n, ax):
        loc = local_shapes[n]
        piece = jnp.moveaxis(piece.reshape(N_DEV, *loc), 0, ax)
        return piece.reshape(*loc[:ax], N_DEV * loc[ax], *loc[ax + 1:])

    for n, ax in SHARDED_MID:
        k = math.prod(local_shapes[n])
        out[n] = full(flat[:, off:off + k], n, ax)
        off += k
    for n, ax in SHARDED_SMALL:
        k = math.prod(local_shapes[n])
        pairs = flat[:, off:off + 2 * k].reshape(N_DEV, k, 2)
        out[n] = full(lax.bitcast_convert_type(pairs, F32), n, ax)
        off += 2 * k
    return out


def local_step(x, positions, target, fw):
    tabs_mla = rope_tables(positions, MLA_NOPE, MLA_ROPE // 2, LANE)
    tabs_ret = rope_tables(positions, 0, RET_DK // 2, RET_DK)
    layers = []
    for layer in range(DEPTH):
        i = layer // 2
        if layer % 2 == 0:
            mw = prep_even(dict(
                w_in_t=fw["w_in_even"][i], w_uq=fw["mla_w_uq"][i], w_ukv=fw["mla_w_ukv"][i], w_out=fw["w_out_even"][i],
                mix_norm=fw["mix_norm_even"][i], q_norm=fw["mla_q_norm"][i], kv_norm=fw["mla_kv_norm"][i],
                q_head_norm=fw["mla_q_head_norm"][i], k_head_norm=fw["mla_k_head_norm"][i],
                theta_fwd=fw["ret_theta_fwd"][i], theta_bwd=fw["ret_theta_bwd"][i],
                ret_out_norm=fw["ret_out_norm"][i]))
        else:
            mw = prep_odd(dict(
                w_in_t=fw["w_in_odd"][i], w_gate_fwd=fw["gla_w_gate_fwd"][i], b_gate_fwd=fw["gla_b_gate_fwd"][i],
                w_gate_bwd=fw["gla_w_gate_bwd"][i], b_gate_bwd=fw["gla_b_gate_bwd"][i],
                gla_out_norm=fw["gla_out_norm"][i], w_out=fw["w_out_odd"][i], mix_norm=fw["mix_norm_odd"][i]))
        fwt = prep_ffn(dict(norm=fw["ffn_norm"][layer], w_up_t=fw["ffn_w_up"][layer], conv_w=fw["ffn_conv_w"][layer],
                            conv_b=fw["ffn_conv_b"][layer], w_down=fw["ffn_w_down"][layer]))
        layers.append((mw, fwt))

    saved = []
    for layer, (mw, fwt) in enumerate(layers):
        if layer % 2 == 0:
            x, sm = even_fwd(x, mw, tabs_mla, tabs_ret, f"l{layer}_mix")
        else:
            x, sm = odd_fwd(x, mw, f"l{layer}_mix")
        x, sf = ffn_fwd(x, fwt, f"l{layer}_ffn")
        saved.append((sm, sf))

    dx, loss = loss_head(x, target, name="loss_head")

    per_layer = [None] * DEPTH
    for layer in reversed(range(DEPTH)):
        mw, fwt = layers[layer]
        sm, sf = saved[layer]
        dx, gf = ffn_bwd(dx, sf, fwt, f"l{layer}_ffn")
        if layer % 2 == 0:
            dx, gm = even_bwd(dx, sm, mw, tabs_mla, tabs_ret, f"l{layer}_mix")
            gm = unprep_even_grads(gm, fw["ret_theta_fwd"][layer // 2], fw["ret_theta_bwd"][layer // 2])
        else:
            dx, gm = odd_bwd(dx, sm, mw, f"l{layer}_mix")
            gm = unprep_odd_grads(gm)
        per_layer[layer] = (gm, unprep_ffn_grads(gf))

    ev = [per_layer[l][0] for l in range(0, DEPTH, 2)]
    od = [per_layer[l][0] for l in range(1, DEPTH, 2)]
    ff = [per_layer[l][1] for l in range(DEPTH)]
    st = lambda lst, key: jnp.stack([g[key] for g in lst])
    grads = {
        "mix_norm_even": st(ev, "mix_norm"), "w_in_even": st(ev, "w_in_t"), "mla_q_norm": st(ev, "q_norm"),
        "mla_kv_norm": st(ev, "kv_norm"), "mla_w_uq": st(ev, "w_uq"), "mla_w_ukv": st(ev, "w_ukv"),
        "mla_q_head_norm": st(ev, "q_head_norm"), "mla_k_head_norm": st(ev, "k_head_norm"),
        "ret_theta_fwd": st(ev, "theta_fwd"), "ret_theta_bwd": st(ev, "theta_bwd"),
        "ret_out_norm": st(ev, "ret_out_norm"), "w_out_even": st(ev, "w_out"),
        "mix_norm_odd": st(od, "mix_norm"), "w_in_odd": st(od, "w_in_t"), "gla_w_gate_fwd": st(od, "w_gate_fwd"),
        "gla_b_gate_fwd": st(od, "b_gate_fwd"), "gla_w_gate_bwd": st(od, "w_gate_bwd"),
        "gla_b_gate_bwd": st(od, "b_gate_bwd"), "gla_out_norm": st(od, "gla_out_norm"), "w_out_odd": st(od, "w_out"),
        "ffn_norm": st(ff, "norm"), "ffn_w_up": st(ff, "w_up_t"), "ffn_conv_w": st(ff, "conv_w"),
        "ffn_conv_b": st(ff, "conv_b"), "ffn_w_down": st(ff, "w_down"),
    }
    return loss, dx, grads


def kernel(x, positions, mix_norm_even, w_in_even, mla_q_norm, mla_kv_norm, mla_w_uq, mla_w_ukv, mla_q_head_norm, mla_k_head_norm, ret_theta_fwd, ret_theta_bwd, ret_out_norm, w_out_even, mix_norm_odd, w_in_odd, gla_w_gate_fwd, gla_b_gate_fwd, gla_w_gate_bwd, gla_b_gate_bwd, gla_out_norm, w_out_odd, ffn_norm, ffn_w_up, ffn_conv_w, ffn_conv_b, ffn_w_down, loss_target, m_mix_norm_even, m_w_in_even, m_mla_q_norm, m_mla_kv_norm, m_mla_w_uq, m_mla_w_ukv, m_mla_q_head_norm, m_mla_k_head_norm, m_ret_theta_fwd, m_ret_theta_bwd, m_ret_out_norm, m_w_out_even, m_mix_norm_odd, m_w_in_odd, m_gla_w_gate_fwd, m_gla_b_gate_fwd, m_gla_w_gate_bwd, m_gla_b_gate_bwd, m_gla_out_norm, m_w_out_odd, m_ffn_norm, m_ffn_w_up, m_ffn_conv_w, m_ffn_conv_b, m_ffn_w_down, v_mix_norm_even, v_w_in_even, v_mla_q_norm, v_mla_kv_norm, v_mla_w_uq, v_mla_w_ukv, v_mla_q_head_norm, v_mla_k_head_norm, v_ret_theta_fwd, v_ret_theta_bwd, v_ret_out_norm, v_w_out_even, v_mix_norm_odd, v_w_in_odd, v_gla_w_gate_fwd, v_gla_b_gate_fwd, v_gla_w_gate_bwd, v_gla_b_gate_bwd, v_gla_out_norm, v_w_out_odd, v_ffn_norm, v_ffn_w_up, v_ffn_conv_w, v_ffn_conv_b, v_ffn_w_down):
    a = dict(locals())
    wts = {n: a[n] for n in WEIGHT_NAMES}
    local_shapes = {n: tuple(wts[n].shape) for n in WEIGHT_NAMES}

    gathered = all_gather_blocks(pack_gather_block(wts, local_shapes))
    fw = unpack_gathered(gathered, local_shapes)
    for n in REPLICATED:
        fw[n] = wts[n]

    loss, grad_x, grads = local_step(x[0], positions, loss_target[0], fw)

    mine, other = pack_grad_blocks(grads, local_shapes)
    chip_part = pair_add(mine, pair_exchange(other))
    g_slab = sum_slots(chip_exchange(chip_part))
    n_big = sum(_row_counts(local_shapes))
    ms = {n: a["m_" + n] for n in WEIGHT_NAMES}
    vs = {n: a["v_" + n] for n in WEIGHT_NAMES}

    g_out = unpack_rows(g_slab[:n_big], local_shapes)
    d_out, m_out, v_out = {}, {}, {}
    for n, _ in ROW_FORM:
        loc = local_shapes[n]
        two_d = lambda t: t.reshape(-1, loc[-1])
        d, m, v = adamw(two_d(wts[n]), two_d(g_out[n]), two_d(ms[n]), two_d(vs[n]), name=f"adamw_{n}")
        d_out[n], m_out[n], v_out[n] = d.reshape(loc), m.reshape(loc), v.reshape(loc)
    g_tail = g_slab[n_big:]
    d_tail, m_tail, v_tail = adamw(pack_tail(wts, local_shapes), g_tail, pack_tail(ms, local_shapes),
                                   pack_tail(vs, local_shapes), name="adamw_small")
    g_out.update(unpack_tail(g_tail, local_shapes))
    d_out.update(unpack_tail(d_tail, local_shapes))
    m_out.update(unpack_tail(m_tail, local_shapes))
    v_out.update(unpack_tail(v_tail, local_shapes))
    total = lax.psum(loss[0, 0], MESH_AXES)
    return (total, grad_x[None], *[g_out[n] for n in WEIGHT_NAMES], *[d_out[n] for n in WEIGHT_NAMES],
            *[m_out[n] for n in WEIGHT_NAMES], *[v_out[n] for n in WEIGHT_NAMES])
```

```python
import math

import jax
import jax.numpy as jnp
from jax import lax
from jax.experimental import pallas as pl
from jax.experimental.pallas import tpu as pltpu

F32 = jnp.float32
BF16 = jnp.bfloat16

D_MODEL = 1024
DEPTH = 4
N_DEV = 8
MESH_AXES = ("x", "y", "c")

MLA_HEADS = 8
MLA_Q_RANK = 384
MLA_KV_RANK = 256
MLA_NOPE = 64
MLA_ROPE = 32
MLA_V = 64
MLA_QK = MLA_NOPE + MLA_ROPE
RET_HEADS = 8
RET_DK = 64
RET_DV = 64
RET_CHUNK = 128
GLA_HEADS = 4
GLA_DK = 128
GLA_DV = 256
GLA_GATE_RANK = 16
GLA_TAU = 16.0
GLA_CHUNK = 64
D_FF = 2816
ROPE_THETA = 10000.0
EPS = 1e-6

ADAM_LR = 0.001
ADAM_B1 = 0.9
ADAM_B2 = 0.999
ADAM_EPS = 1e-08
ADAM_WD = 0.01
ADAM_STEP = 10

LANE = 128
SUBLANE = 8
ROW_TILE = 512
VMEM_LIMIT = 56 * 1024 * 1024
WEIGHT_TILE_BYTES = 8 * 1024 * 1024

EV_CQ, EV_CKV, EV_KR, EV_RQ, EV_RK, EV_RV, EV_RG, EV_IN = 0, 384, 640, 768, 1280, 1792, 2304, 2816
OD_Q, OD_K, OD_V, OD_R, OD_GA, OD_IN = 0, 512, 1024, 2048, 3072, 3200
N_GROUPS = 4


def _cparams(sem):
    return pltpu.CompilerParams(dimension_semantics=sem, vmem_limit_bytes=VMEM_LIMIT)


def _dot(a, b):
    return jnp.dot(a.astype(BF16), b.astype(BF16), preferred_element_type=F32)


def _dot_nt(a, b):
    return lax.dot_general(a.astype(BF16), b.astype(BF16), (((1,), (1,)), ((), ())), preferred_element_type=F32)


def _dot_tn(a, b):
    return lax.dot_general(a.astype(BF16), b.astype(BF16), (((0,), (0,)), ((), ())), preferred_element_type=F32)


def _sigmoid(x):
    return 1.0 / (1.0 + jnp.exp(-x))


def _col_tile(k, n, itemsize=2):
    best = LANE
    for t in range(LANE, n + 1, LANE):
        if n % t == 0 and k * t * itemsize <= WEIGHT_TILE_BYTES:
            best = t
    return best if n % LANE == 0 else n


def _row_tile(m):
    return min(ROW_TILE, m)


def mm_nn(a, b, res=None, out_dtype=F32, name="mm_nn"):
    m, k = a.shape
    n = b.shape[1]
    tm, tn = min(2 * ROW_TILE, m), _col_tile(k, n)

    def body(*refs):
        if res is None:
            a_ref, b_ref, o_ref = refs
        else:
            a_ref, b_ref, r_ref, o_ref = refs
        acc = _dot(a_ref[...], b_ref[...])
        if res is not None:
            acc = acc + r_ref[...].astype(F32)
        o_ref[...] = acc.astype(out_dtype)

    in_specs = [pl.BlockSpec((tm, k), lambda j, i: (i, 0)), pl.BlockSpec((k, tn), lambda j, i: (0, j))]
    args = [a, b]
    if res is not None:
        in_specs.append(pl.BlockSpec((tm, tn), lambda j, i: (i, j)))
        args.append(res)
    return pl.pallas_call(
        body, name=name, grid=(n // tn, m // tm), in_specs=in_specs,
        out_specs=pl.BlockSpec((tm, tn), lambda j, i: (i, j)),
        out_shape=jax.ShapeDtypeStruct((m, n), out_dtype),
        compiler_params=_cparams(("parallel", "parallel")),
    )(*args)


def mm_tn(a, b, out_dtype=F32, name="mm_tn"):
    t, k = a.shape
    n = b.shape[1]
    tt = min(2 * ROW_TILE, t)
    tk = k if k <= 1024 else _col_tile(1024, k, 4)
    tn = n if n <= 1024 else _col_tile(1024, n, 4)

    def body(a_ref, b_ref, o_ref, acc_s):
        s = pl.program_id(2)

        @pl.when(s == 0)
        def _():
            acc_s[...] = jnp.zeros_like(acc_s)
        acc_s[...] += _dot_tn(a_ref[...], b_ref[...])

        @pl.when(s == pl.num_programs(2) - 1)
        def _():
            o_ref[...] = acc_s[...].astype(out_dtype)

    return pl.pallas_call(
        body, name=name, grid=(k // tk, n // tn, t // tt),
        in_specs=[pl.BlockSpec((tt, tk), lambda i, j, s: (s, i)), pl.BlockSpec((tt, tn), lambda i, j, s: (s, j))],
        out_specs=pl.BlockSpec((tk, tn), lambda i, j, s: (i, j)),
        out_shape=jax.ShapeDtypeStruct((k, n), out_dtype),
        scratch_shapes=[pltpu.VMEM((tk, tn), F32)],
        compiler_params=_cparams(("parallel", "parallel", "arbitrary")),
    )(a, b)


def rmsnorm_fwd(x, g, name="rmsnorm_fwd"):
    t, d = x.shape
    tm = _row_tile(t)

    def body(x_ref, g_ref, h_ref):
        xv = x_ref[...]
        r = lax.rsqrt(jnp.mean(xv * xv, axis=-1, keepdims=True) + EPS)
        h_ref[...] = (xv * r * g_ref[...]).astype(BF16)

    return pl.pallas_call(
        body, name=name, grid=(t // tm,),
        in_specs=[pl.BlockSpec((tm, d), lambda i: (i, 0)), pl.BlockSpec((1, d), lambda i: (0, 0))],
        out_specs=pl.BlockSpec((tm, d), lambda i: (i, 0)),
        out_shape=jax.ShapeDtypeStruct((t, d), BF16),
        compiler_params=_cparams(("parallel",)),
    )(x, g.reshape(1, d))


def rmsnorm_bwd(x, g, dh, dres, name="rmsnorm_bwd"):
    t, d = x.shape
    tm = _row_tile(t)

    def body(x_ref, g_ref, dh_ref, dres_ref, dx_ref, dg_ref):
        @pl.when(pl.program_id(0) == 0)
        def _():
            dg_ref[...] = jnp.zeros_like(dg_ref)
        xv = x_ref[...]
        r = lax.rsqrt(jnp.mean(xv * xv, axis=-1, keepdims=True) + EPS)
        xh = xv * r
        dhv = dh_ref[...]
        dg_ref[...] += jnp.sum(dhv * xh, axis=0, keepdims=True)
        dxh = dhv * g_ref[...]
        dx_ref[...] = dres_ref[...] + r * (dxh - xh * jnp.mean(dxh * xh, axis=-1, keepdims=True))

    row = pl.BlockSpec((tm, d), lambda i: (i, 0))
    vec = pl.BlockSpec((1, d), lambda i: (0, 0))
    return pl.pallas_call(
        body, name=name, grid=(t // tm,),
        in_specs=[row, vec, row, row], out_specs=[row, vec],
        out_shape=[jax.ShapeDtypeStruct((t, d), F32), jax.ShapeDtypeStruct((1, d), F32)],
        compiler_params=_cparams(("arbitrary",)),
    )(x, g.reshape(1, d), dh, dres)


def _rope_apply(x, cos, s1, s2, half):
    return x * cos + pltpu.roll(x, LANE - half, 1) * s1 + pltpu.roll(x, half, 1) * s2


def _rope_transpose(dy, cos, s1, s2, half):
    return dy * cos + pltpu.roll(dy * s1, half, 1) + pltpu.roll(dy * s2, LANE - half, 1)


def rope_tables(positions, lane_start, half, period):
    pos = positions.reshape(-1).astype(F32)
    inv = ROPE_THETA ** (-jnp.arange(half, dtype=F32) / half)
    ang = pos[:, None] * inv[None, :]
    cos, sin = jnp.cos(ang), jnp.sin(ang)
    t = pos.shape[0]
    pre = lane_start
    post = period - lane_start - 2 * half
    ones = lambda n: jnp.ones((t, n), F32)
    zeros = lambda n: jnp.zeros((t, n), F32)
    c = jnp.concatenate([ones(pre), cos, cos, ones(post)], axis=1)
    a = jnp.concatenate([zeros(pre), -sin, zeros(half), zeros(post)], axis=1)
    b = jnp.concatenate([zeros(pre), zeros(half), sin, zeros(post)], axis=1)
    rep = LANE // period
    return tuple(jnp.tile(v, (1, rep)) for v in (c, a, b))


def _mla_forward_tile(p, cos, s1, s2, qn_g, kvn_g, wuq, wk, wv, qhn, khn):
    cq = p[:, EV_CQ:EV_CKV]
    ckv = p[:, EV_CKV:EV_KR]
    kr = p[:, EV_KR:EV_RQ]
    rq = lax.rsqrt(jnp.mean(cq * cq, axis=-1, keepdims=True) + EPS)
    rkv = lax.rsqrt(jnp.mean(ckv * ckv, axis=-1, keepdims=True) + EPS)
    qn = cq * rq * qn_g
    kvn = ckv * rkv * kvn_g
    q_raw = _dot(qn, wuq)
    k_raw = _dot(kvn, wk)
    v = _dot(kvn, wv)
    krp = pltpu.roll(kr, MLA_NOPE, 1)
    return cq, ckv, rq, rkv, qn, kvn, q_raw, k_raw, v, krp


def _head_norm(xh, g):
    r = lax.rsqrt(jnp.sum(xh * xh, axis=-1, keepdims=True) * (1.0 / MLA_QK) + EPS)
    return xh * r * g, r


def mla_prep_fwd(p, tabs, qn_g, kvn_g, wuq, wk, wv, qhn, khn, name="mla_prep_fwd"):
    t = p.shape[0]
    tm = _row_tile(t)
    hw = MLA_HEADS * LANE

    def body(p_ref, c_ref, s1_ref, s2_ref, qn_ref, kvn_ref, wuq_ref, wk_ref, wv_ref, qhn_ref, khn_ref,
             q_out, k_out, v_out):
        cos, s1, s2 = c_ref[...], s1_ref[...], s2_ref[...]
        (_, _, _, _, _, _, q_raw, k_raw, v, krp) = _mla_forward_tile(
            p_ref[...], cos, s1, s2, qn_ref[...], kvn_ref[...], wuq_ref[...], wk_ref[...], wv_ref[...],
            qhn_ref[...], khn_ref[...])
        v_out[...] = v.astype(BF16)
        for h in range(MLA_HEADS):
            sl = slice(h * LANE, (h + 1) * LANE)
            qh, _ = _head_norm(q_raw[:, sl], qhn_ref[...])
            kh, _ = _head_norm(k_raw[:, sl] + krp, khn_ref[...])
            q_out[:, sl] = (_rope_apply(qh, cos, s1, s2, MLA_ROPE // 2) * ATTN_QSCALE).astype(BF16)
            k_out[:, sl] = _rope_apply(kh, cos, s1, s2, MLA_ROPE // 2).astype(BF16)

    row = lambda w: pl.BlockSpec((tm, w), lambda i: (i, 0))
    full = lambda a: pl.BlockSpec(a.shape, lambda i: (0,) * a.ndim)
    ws = [qn_g, kvn_g, wuq, wk, wv, qhn, khn]
    return pl.pallas_call(
        body, name=name, grid=(t // tm,),
        in_specs=[row(EV_RQ), row(LANE), row(LANE), row(LANE)] + [full(w) for w in ws],
        out_specs=[row(hw)] * 3,
        out_shape=[jax.ShapeDtypeStruct((t, hw), BF16)] * 3,
        compiler_params=_cparams(("parallel",)),
    )(p, *tabs, *ws)


def mla_prep_bwd(p, tabs, qn_g, kvn_g, wuq, wk, wv, wuq_t, wk_t, wv_t, qhn, khn, dq, dk, dv,
                 name="mla_prep_bwd"):
    t = p.shape[0]
    tm = _row_tile(t)
    hw = MLA_HEADS * LANE

    def body(p_ref, c_ref, s1_ref, s2_ref, qn_ref, kvn_ref, wuq_ref, wk_ref, wv_ref, wuqt_ref, wkt_ref, wvt_ref,
             qhn_ref, khn_ref, dq_ref, dk_ref, dv_ref,
             dp_ref, dwuq_ref, dwk_ref, dwv_ref, dqn_ref, dkvn_ref, dqhn_ref, dkhn_ref, dqraw_s, dkraw_s):
        @pl.when(pl.program_id(0) == 0)
        def _():
            for r in (dwuq_ref, dwk_ref, dwv_ref, dqn_ref, dkvn_ref, dqhn_ref, dkhn_ref):
                r[...] = jnp.zeros_like(r)
        cos, s1, s2 = c_ref[...], s1_ref[...], s2_ref[...]
        qhn_v, khn_v = qhn_ref[...], khn_ref[...]
        (cq, ckv, rq, rkv, qn, kvn, q_raw, k_raw, _, krp) = _mla_forward_tile(
            p_ref[...], cos, s1, s2, qn_ref[...], kvn_ref[...], wuq_ref[...], wk_ref[...], wv_ref[...],
            qhn_v, khn_v)
        half = MLA_ROPE // 2
        dkr_sum = jnp.zeros((tm, LANE), F32)
        dqhn_acc = jnp.zeros((1, LANE), F32)
        dkhn_acc = jnp.zeros((1, LANE), F32)
        for h in range(MLA_HEADS):
            sl = slice(h * LANE, (h + 1) * LANE)
            xq = q_raw[:, sl]
            _, r = _head_norm(xq, qhn_v)
            xh = xq * r
            dy = _rope_transpose(dq_ref[:, sl] * ATTN_SCALE, cos, s1, s2, half)
            dqhn_acc = dqhn_acc + jnp.sum(dy * xh, axis=0, keepdims=True)
            dxh = dy * qhn_v
            dqraw_s[:, sl] = r * (dxh - xh * (jnp.sum(dxh * xh, axis=-1, keepdims=True) * (1.0 / MLA_QK)))
            xk = k_raw[:, sl] + krp
            _, r = _head_norm(xk, khn_v)
            xh = xk * r
            dy = _rope_transpose(dk_ref[:, sl] * math.log(2.0), cos, s1, s2, half)
            dkhn_acc = dkhn_acc + jnp.sum(dy * xh, axis=0, keepdims=True)
            dxh = dy * khn_v
            dxk = r * (dxh - xh * (jnp.sum(dxh * xh, axis=-1, keepdims=True) * (1.0 / MLA_QK)))
            dkraw_s[:, sl] = dxk
            dkr_sum = dkr_sum + dxk
        dqhn_ref[...] += dqhn_acc
        dkhn_ref[...] += dkhn_acc
        dq_raw = dqraw_s[...]
        dk_raw = dkraw_s[...]
        dvv = dv_ref[...]
        dwuq_ref[...] += _dot_tn(qn, dq_raw)
        dwk_ref[...] += _dot_tn(kvn, dk_raw)
        dwv_ref[...] += _dot_tn(kvn, dvv)
        dqn = _dot(dq_raw, wuqt_ref[...])
        dkvn = _dot(dk_raw, wkt_ref[...]) + _dot(dvv, wvt_ref[...])
        xh = cq * rq
        dqn_ref[...] += jnp.sum(dqn * xh, axis=0, keepdims=True)
        dxh = dqn * qn_ref[...]
        dcq = rq * (dxh - xh * jnp.mean(dxh * xh, axis=-1, keepdims=True))
        dp_ref[:, EV_CQ:EV_CKV] = dcq.astype(BF16)
        xh = ckv * rkv
        dkvn_ref[...] += jnp.sum(dkvn * xh, axis=0, keepdims=True)
        dxh = dkvn * kvn_ref[...]
        dckv = rkv * (dxh - xh * jnp.mean(dxh * xh, axis=-1, keepdims=True))
        dp_ref[:, EV_CKV:EV_KR] = dckv.astype(BF16)
        lane = lax.broadcasted_iota(jnp.int32, (tm, LANE), 1)
        dkr = jnp.where(lane < MLA_ROPE, pltpu.roll(dkr_sum, LANE - MLA_NOPE, 1), 0.0)
        dp_ref[:, EV_KR:EV_RQ] = dkr.astype(BF16)

    row = lambda w: pl.BlockSpec((tm, w), lambda i: (i, 0))
    full = lambda a: pl.BlockSpec(a.shape, lambda i: (0,) * a.ndim)
    ws = [qn_g, kvn_g, wuq, wk, wv, wuq_t, wk_t, wv_t, qhn, khn]
    outs = [jax.ShapeDtypeStruct((t, EV_RQ), BF16), jax.ShapeDtypeStruct(wuq.shape, F32),
            jax.ShapeDtypeStruct(wk.shape, F32), jax.ShapeDtypeStruct(wv.shape, F32),
            jax.ShapeDtypeStruct(qn_g.shape, F32), jax.ShapeDtypeStruct(kvn_g.shape, F32),
            jax.ShapeDtypeStruct(qhn.shape, F32), jax.ShapeDtypeStruct(khn.shape, F32)]
    return pl.pallas_call(
        body, name=name, grid=(t // tm,),
        in_specs=[row(EV_RQ), row(LANE), row(LANE), row(LANE)] + [full(w) for w in ws] + [row(hw)] * 3,
        out_specs=[row(EV_RQ)] + [full(o) for o in outs[1:]],
        out_shape=outs,
        scratch_shapes=[pltpu.VMEM((tm, hw), F32), pltpu.VMEM((tm, hw), F32)],
        compiler_params=_cparams(("arbitrary",)),
    )(p, *tabs, *ws, dq, dk, dv)


ATTN_SCALE = MLA_QK ** -0.5
ATTN_QSCALE = ATTN_SCALE * math.log2(math.e)
ATTN_FWD_TQ, ATTN_FWD_TK = 512, 8192
ATTN_BWD_TQ, ATTN_BWD_TK = 512, 4096


def attn_fwd(q, k, v, name="attn_fwd"):
    t = q.shape[0]
    tq, tk = min(ATTN_FWD_TQ, _row_tile(t)), min(ATTN_FWD_TK, t)
    nh = MLA_HEADS

    def body(q_ref, k_ref, v_ref, o_ref, lse_ref, m_s, l_s, acc_s):
        j = pl.program_id(2)

        @pl.when(j == 0)
        def _():
            m_s[...] = jnp.full_like(m_s, -jnp.inf)
            l_s[...] = jnp.zeros_like(l_s)
            acc_s[...] = jnp.zeros_like(acc_s)

        s = _dot_nt(q_ref[...], k_ref[...])
        m_old = m_s[...]
        m_new = jnp.maximum(m_old, jnp.max(s, axis=-1, keepdims=True))
        pr = jnp.exp2(s - m_new)
        alpha = jnp.exp2(m_old - m_new)
        l_s[...] = alpha * l_s[...] + jnp.sum(pr, axis=-1, keepdims=True)
        acc_s[...] = alpha * acc_s[...] + _dot(pr, v_ref[...])
        m_s[...] = m_new

        @pl.when(j == pl.num_programs(2) - 1)
        def _():
            o_ref[...] = acc_s[...] / l_s[...]
            lse_ref[...] = m_s[...] + jnp.log2(l_s[...])

    return pl.pallas_call(
        body, name=name, grid=(nh, t // tq, t // tk),
        in_specs=[pl.BlockSpec((tq, LANE), lambda h, i, j: (i, h)),
                  pl.BlockSpec((tk, LANE), lambda h, i, j: (j, h)),
                  pl.BlockSpec((tk, LANE), lambda h, i, j: (j, h))],
        out_specs=[pl.BlockSpec((tq, LANE), lambda h, i, j: (i, h)),
                   pl.BlockSpec((None, tq, 1), lambda h, i, j: (h, i, 0))],
        out_shape=[jax.ShapeDtypeStruct((t, nh * LANE), F32), jax.ShapeDtypeStruct((nh, t, 1), F32)],
        scratch_shapes=[pltpu.VMEM((tq, 1), F32), pltpu.VMEM((tq, 1), F32), pltpu.VMEM((tq, LANE), F32)],
        compiler_params=_cparams(("parallel", "parallel", "arbitrary")),
    )(q, k, v)


def attn_bwd(q, k, v, o, lse, do, name="attn_bwd"):
    t = q.shape[0]
    tq, tk = min(ATTN_BWD_TQ, _row_tile(t)), min(ATTN_BWD_TK, t)
    nh = MLA_HEADS
    nq = t // tq

    def body(q_ref, k_ref, v_ref, o_ref, lse_ref, do_ref, dq_ref, dk_ref, dv_ref):
        kj, qi = pl.program_id(1), pl.program_id(2)

        @pl.when(qi == 0)
        def _():
            dk_ref[...] = jnp.zeros_like(dk_ref)
            dv_ref[...] = jnp.zeros_like(dv_ref)

        qv, kv, vv, dov = q_ref[...], k_ref[...], v_ref[...], do_ref[...]
        s = _dot_nt(qv, kv)
        pr = jnp.exp2(s - lse_ref[...])
        dp = _dot_nt(dov, vv)
        delta = jnp.sum(dov * o_ref[...], axis=-1, keepdims=True)
        ds = pr * (dp - delta)
        dv_ref[...] += _dot_tn(pr, dov)
        dk_ref[...] += _dot_tn(ds, qv)
        dq_tile = _dot(ds, kv)
        rows = pl.ds(pl.multiple_of(qi * tq, tq), tq)

        @pl.when(kj == 0)
        def _():
            dq_ref[rows, :] = dq_tile

        @pl.when(kj != 0)
        def _():
            dq_ref[rows, :] += dq_tile

    qspec = pl.BlockSpec((tq, LANE), lambda h, j, i: (i, h))
    kspec = pl.BlockSpec((tk, LANE), lambda h, j, i: (j, h))
    return pl.pallas_call(
        body, name=name, grid=(nh, t // tk, nq),
        in_specs=[qspec, kspec, kspec, qspec, pl.BlockSpec((None, tq, 1), lambda h, j, i: (h, i, 0)), qspec],
        out_specs=[pl.BlockSpec((t, LANE), lambda h, j, i: (0, h)), kspec, kspec],
        out_shape=[jax.ShapeDtypeStruct((t, nh * LANE), F32)] * 3,
        compiler_params=_cparams(("parallel", "arbitrary", "arbitrary")),
    )(q, k, v, o, lse, do)


def _scan_consts(c, reverse, inclusive):
    ii = lax.broadcasted_iota(jnp.int32, (c, c), 0)
    jj = lax.broadcasted_iota(jnp.int32, (c, c), 1)
    if reverse:
        incl = jj >= ii
        mask = incl if inclusive else jj > ii
    else:
        incl = jj <= ii
        mask = incl if inclusive else jj < ii
    mid = (c - 1 - c // 2) if reverse else c // 2
    incl_t = (jj <= ii) if reverse else (jj >= ii)
    return incl.astype(F32), incl_t.astype(F32), mask.astype(F32), mid


def _dot_split(a01, x):
    hi = x.astype(BF16)
    lo = (x - hi.astype(F32)).astype(BF16)
    a = a01.astype(BF16)
    return jnp.dot(a, hi, preferred_element_type=F32) + jnp.dot(a, lo, preferred_element_type=F32)


def _sub_masks(sub, dvg, u):
    if sub == 1:
        return None, None
    kl = lax.broadcasted_iota(jnp.int32, (1, LANE), 1)
    vl = lax.broadcasted_iota(jnp.int32, (1, dvg), 1)
    kw, vw = LANE // sub, dvg // sub
    km = (kl >= u * kw) & (kl < (u + 1) * kw)
    vm = (vl >= u * vw) & (vl < (u + 1) * vw)
    return km.astype(F32), vm.astype(F32)


def _block_incl(r, c, reverse, transposed):
    shift = c.bit_length() - 1
    ii = lax.broadcasted_iota(jnp.int32, (r, r), 0)
    jj = lax.broadcasted_iota(jnp.int32, (r, r), 1)
    same = lax.shift_right_logical(ii, shift) == lax.shift_right_logical(jj, shift)
    lower = (jj <= ii) if (reverse == transposed) else (jj >= ii)
    return (same & lower).astype(F32)


def _scan_chunk_fwd(b, la, mid):
    row = lax.broadcasted_iota(jnp.int32, b.shape, 0)
    bm = jnp.sum(jnp.where(row == mid, b, 0.0), axis=0, keepdims=True)
    tot = jnp.sum(la, axis=0, keepdims=True)
    e_qc = jnp.exp(b - bm)
    e_kc = jnp.exp(bm - b)
    e_qe = jnp.exp(b)
    e_kd = jnp.exp(tot - b)
    return e_qc, e_kc, e_qe, e_kd


def scan_fwd(q_arr, k_arr, v_arr, la_arr, *, qcb, kcb, vcb, lacb, la_row, chunk, dvg, sub, reverse, inclusive,
             qscale, kscale, rope=None, name="scan_fwd"):
    t = q_arr.shape[0]
    r = _row_tile(t)
    nb, nc = t // r, r // chunk
    c = chunk
    rb = (lambda j: nb - 1 - j) if reverse else (lambda j: j)
    order = list(range(nc))[::-1] if reverse else list(range(nc))
    half = RET_DK // 2

    def body(*refs):
        if rope is None:
            q_ref, k_ref, v_ref, la_ref, o_ref, st_ref, s_s = refs
        else:
            q_ref, k_ref, v_ref, la_ref, c_ref, s1_ref, s2_ref, o_ref, st_ref, s_s = refs

        @pl.when(pl.program_id(1) == 0)
        def _():
            s_s[...] = jnp.zeros_like(s_s)

        incl, _, mask, mid = _scan_consts(c, reverse, inclusive)
        for ci in order:
            rows = slice(ci * c, (ci + 1) * c)
            qv = q_ref[rows, :] * qscale
            kv = k_ref[rows, :] * kscale
            if rope is not None:
                cs, a1, a2 = c_ref[rows, :], s1_ref[rows, :], s2_ref[rows, :]
                qv = _rope_apply(qv, cs, a1, a2, half)
                kv = _rope_apply(kv, cs, a1, a2, half)
            la = jnp.broadcast_to(la_ref[...], (c, LANE)) if la_row else la_ref[rows, :]
            vv = v_ref[rows, :]
            e_qc, e_kc, e_qe, e_kd = _scan_chunk_fwd(_dot_split(incl, la), la, mid)
            qc, kc, qe, kd = qv * e_qc, kv * e_kc, qv * e_qe, kv * e_kd
            sg = s_s[...]
            st_ref[ci] = sg
            acc = None
            for u in range(sub):
                mu, vmu = _sub_masks(sub, dvg, u)
                qcu = qc if mu is None else qc * mu
                qeu = qe if mu is None else qe * mu
                a = _dot_nt(qcu, kc) * mask
                ou = _dot(a, vv) + _dot_nt(qeu, sg)
                ou = ou if vmu is None else ou * vmu
                acc = ou if acc is None else acc + ou
            o_ref[rows, :] = acc
            decay = jnp.exp(jnp.sum(la, axis=0, keepdims=True))
            s_s[...] = decay * sg + _dot_tn(vv, kd)

    specs = [pl.BlockSpec((r, LANE), lambda g, j: (rb(j), qcb + g)),
             pl.BlockSpec((r, LANE), lambda g, j: (rb(j), kcb + g)),
             pl.BlockSpec((r, dvg), lambda g, j: (rb(j), vcb + g)),
             pl.BlockSpec((1, LANE), lambda g, j: (0, lacb + g)) if la_row
             else pl.BlockSpec((r, LANE), lambda g, j: (rb(j), lacb + g))]
    args = [q_arr, k_arr, v_arr, la_arr]
    if rope is not None:
        specs += [pl.BlockSpec((r, LANE), lambda g, j: (rb(j), 0))] * 3
        args += list(rope)
    return pl.pallas_call(
        body, name=name, grid=(N_GROUPS, nb), in_specs=specs,
        out_specs=[pl.BlockSpec((r, dvg), lambda g, j: (rb(j), g)),
                   pl.BlockSpec((nc, dvg, LANE), lambda g, j: (rb(j), g, 0))],
        out_shape=[jax.ShapeDtypeStruct((t, N_GROUPS * dvg), F32),
                   jax.ShapeDtypeStruct((t // c, N_GROUPS * dvg, LANE), F32)],
        scratch_shapes=[pltpu.VMEM((dvg, LANE), F32)],
        compiler_params=_cparams(("parallel", "arbitrary")),
    )(*args)


def scan_bwd(q_arr, k_arr, v_arr, la_arr, st_arr, do_arr, prev, *, qcb, kcb, vcb, lacb, la_row, chunk, dvg, sub,
             reverse, inclusive, qscale, kscale, rope=None, block_rows=None, name="scan_bwd"):
    t = q_arr.shape[0]
    r = _row_tile(t) if block_rows is None else min(block_rows, t)
    nb, nc = t // r, r // chunk
    c = chunk
    rb = (lambda j: j) if reverse else (lambda j: nb - 1 - j)
    order = list(range(nc)) if reverse else list(range(nc))[::-1]
    half = RET_DK // 2
    n_in = 6 + (3 if rope is not None else 0) + (3 if prev is not None else 0)
    gdt = F32 if prev is None else BF16

    def body(*refs):
        ins, outs = refs[:n_in], refs[n_in:]
        q_ref, k_ref, v_ref, la_ref, st_ref, do_ref = ins[:6]
        pos = 6
        if rope is not None:
            c_ref, s1_ref, s2_ref = ins[pos:pos + 3]
            pos += 3
        if prev is not None:
            pq_ref, pk_ref, pv_ref = ins[pos:pos + 3]
        dq_ref, dk_ref, dv_ref, dla_ref, g_s = outs

        @pl.when(pl.program_id(1) == 0)
        def _():
            g_s[...] = jnp.zeros_like(g_s)
            if la_row:
                dla_ref[...] = jnp.zeros_like(dla_ref)

        incl, _, mask, mid = _scan_consts(c, reverse, inclusive)
        b_all = None if la_row else _dot_split(_block_incl(r, c, reverse, False), la_ref[...])
        pos = lax.broadcasted_iota(jnp.int32, (c, LANE), 0)
        cnt = ((c - pos) if reverse else (pos + 1)).astype(F32)
        dla_sum = jnp.zeros((1, LANE), F32)
        db_parts, dtot_parts = [None] * nc, [None] * nc
        for ci in order:
            rows = slice(ci * c, (ci + 1) * c)
            qv = q_ref[rows, :] * qscale
            kv = k_ref[rows, :] * kscale
            if rope is not None:
                cs, a1, a2 = c_ref[rows, :], s1_ref[rows, :], s2_ref[rows, :]
                qv = _rope_apply(qv, cs, a1, a2, half)
                kv = _rope_apply(kv, cs, a1, a2, half)
            la = jnp.broadcast_to(la_ref[...], (c, LANE)) if la_row else la_ref[rows, :]
            vv = v_ref[rows, :]
            dov = do_ref[rows, :]
            b = _dot_split(incl, la) if la_row else b_all[rows, :]
            e_qc, e_kc, e_qe, e_kd = _scan_chunk_fwd(b, la, mid)
            qc, kc, qe, kd = qv * e_qc, kv * e_kc, qv * e_qe, kv * e_kd
            sg = st_ref[ci]
            gn = g_s[...]
            dqc = jnp.zeros((c, LANE), F32)
            dkc = jnp.zeros((c, LANE), F32)
            dqe = jnp.zeros((c, LANE), F32)
            dvv = _dot_nt(kd, gn)
            ds_direct = jnp.zeros((dvg, LANE), F32)
            for u in range(sub):
                mu, vmu = _sub_masks(sub, dvg, u)
                qcu = qc if mu is None else qc * mu
                qeu = qe if mu is None else qe * mu
                dou = dov if vmu is None else dov * vmu
                a = _dot_nt(qcu, kc) * mask
                da = _dot_nt(dou, vv) * mask
                dvv = dvv + _dot_tn(a, dou)
                t1 = _dot(da, kc)
                dqc = dqc + (t1 if mu is None else t1 * mu)
                dkc = dkc + _dot_tn(da, qcu)
                t2 = _dot(dou, sg)
                dqe = dqe + (t2 if mu is None else t2 * mu)
                ds_direct = ds_direct + _dot_tn(dou, qeu)
            dkd = _dot(vv, gn)
            decay = jnp.exp(jnp.sum(la, axis=0, keepdims=True))
            dtot = jnp.sum(gn * sg, axis=0, keepdims=True) * decay + jnp.sum(dkd * kd, axis=0, keepdims=True)
            db = dqc * qc - dkc * kc + dqe * qe - dkd * kd
            if la_row:
                dla_sum = dla_sum + jnp.sum(db * cnt, axis=0, keepdims=True) + float(c) * dtot
            else:
                db_parts[ci] = db
                dtot_parts[ci] = jnp.broadcast_to(dtot, (c, LANE))
            dqv = dqc * e_qc + dqe * e_qe
            dkv = dkc * e_kc + dkd * e_kd
            if rope is not None:
                dqv = _rope_transpose(dqv, cs, a1, a2, half)
                dkv = _rope_transpose(dkv, cs, a1, a2, half)
            dqv = dqv * qscale
            dkv = dkv * kscale
            if prev is not None:
                dqv = dqv + pq_ref[rows, :]
                dkv = dkv + pk_ref[rows, :]
                dvv = dvv + pv_ref[rows, :]
            dq_ref[rows, :] = dqv.astype(gdt)
            dk_ref[rows, :] = dkv.astype(gdt)
            dv_ref[rows, :] = dvv.astype(gdt)
            g_s[...] = ds_direct + decay * gn
        if la_row:
            dla_ref[...] += dla_sum
        else:
            db_all = jnp.concatenate(db_parts, axis=0)
            dla_ref[...] = _dot_split(_block_incl(r, c, reverse, True), db_all) + jnp.concatenate(dtot_parts, axis=0)

    kblk = lambda cb: pl.BlockSpec((r, LANE), lambda g, j: (rb(j), cb + g))
    vblk = lambda cb: pl.BlockSpec((r, dvg), lambda g, j: (rb(j), cb + g))
    specs = [kblk(qcb), kblk(kcb), vblk(vcb),
             pl.BlockSpec((1, LANE), lambda g, j: (0, lacb + g)) if la_row else kblk(lacb),
             pl.BlockSpec((nc, dvg, LANE), lambda g, j: (rb(j), g, 0)), vblk(0)]
    args = [q_arr, k_arr, v_arr, la_arr, st_arr, do_arr]
    if rope is not None:
        specs += [pl.BlockSpec((r, LANE), lambda g, j: (rb(j), 0))] * 3
        args += list(rope)
    if prev is not None:
        specs += [kblk(0), kblk(0), vblk(0)]
        args += list(prev)
    wk = N_GROUPS * LANE
    outs = [jax.ShapeDtypeStruct((t, wk), gdt), jax.ShapeDtypeStruct((t, wk), gdt),
            jax.ShapeDtypeStruct((t, N_GROUPS * dvg), gdt),
            jax.ShapeDtypeStruct((1, wk) if la_row else (t, wk), F32)]
    return pl.pallas_call(
        body, name=name, grid=(N_GROUPS, nb), in_specs=specs,
        out_specs=[kblk(0), kblk(0), vblk(0),
                   pl.BlockSpec((1, LANE), lambda g, j: (0, g)) if la_row else kblk(0)],
        out_shape=outs,
        scratch_shapes=[pltpu.VMEM((dvg, LANE), F32)],
        compiler_params=_cparams(("parallel", "arbitrary")),
    )(*args)


def _seg_mean(x, seg):
    w = x.shape[1]
    if seg % LANE == 0:
        parts = []
        for s in range(0, w, seg):
            m = jnp.mean(x[:, s:s + seg], axis=-1, keepdims=True)
            parts.append(jnp.broadcast_to(m, (x.shape[0], seg)))
        return jnp.concatenate(parts, axis=1)
    shift = seg.bit_length() - 1
    ii = lax.shift_right_logical(lax.broadcasted_iota(jnp.int32, (w, w), 0), shift)
    jj = lax.shift_right_logical(lax.broadcasted_iota(jnp.int32, (w, w), 1), shift)
    e = (ii == jj).astype(BF16)
    hi = x.astype(BF16)
    lo = (x - hi.astype(F32)).astype(BF16)
    return (jnp.dot(hi, e, preferred_element_type=F32) + jnp.dot(lo, e, preferred_element_type=F32)) * (1.0 / seg)


def gated_norm_fwd(o_f, o_b, gate_arr, gcb, gn, seg, name="gated_norm_fwd"):
    t, w = o_f.shape
    tm = _row_tile(t)

    def body(of_ref, ob_ref, g_ref, gn_ref, y_ref):
        o = of_ref[...] + ob_ref[...]
        r = lax.rsqrt(_seg_mean(o * o, seg) + EPS)
        gt = g_ref[...]
        y_ref[...] = (gt * _sigmoid(gt) * (o * r * gn_ref[...])).astype(BF16)

    bw = max(seg, LANE)
    row = pl.BlockSpec((tm, bw), lambda j, i: (i, j))
    return pl.pallas_call(
        body, name=name, grid=(w // bw, t // tm),
        in_specs=[row, row, pl.BlockSpec((tm, bw), lambda j, i: (i, gcb + j)),
                  pl.BlockSpec((1, bw), lambda j, i: (0, j))],
        out_specs=row, out_shape=jax.ShapeDtypeStruct((t, w), BF16),
        compiler_params=_cparams(("parallel", "parallel")),
    )(o_f, o_b, gate_arr, gn.reshape(1, w))


def gated_norm_bwd(o_f, o_b, gate_arr, gcb, gn, seg, dy, name="gated_norm_bwd"):
    t, w = o_f.shape
    tm = _row_tile(t)

    def body(of_ref, ob_ref, g_ref, gn_ref, dy_ref, do_ref, dg_ref, dgn_ref):
        @pl.when(pl.program_id(1) == 0)
        def _():
            dgn_ref[...] = jnp.zeros_like(dgn_ref)
        o = of_ref[...] + ob_ref[...]
        r = lax.rsqrt(_seg_mean(o * o, seg) + EPS)
        xh = o * r
        gt = g_ref[...]
        sg = _sigmoid(gt)
        dyv = dy_ref[...]
        n = xh * gn_ref[...]
        dg_ref[...] = (dyv * n * (sg * (1.0 + gt * (1.0 - sg)))).astype(BF16)
        dn = dyv * (gt * sg)
        dgn_ref[...] += jnp.sum(dn * xh, axis=0, keepdims=True)
        dxh = dn * gn_ref[...]
        do_ref[...] = r * (dxh - xh * _seg_mean(dxh * xh, seg))

    bw = max(seg, LANE)
    row = pl.BlockSpec((tm, bw), lambda j, i: (i, j))
    vec = pl.BlockSpec((1, bw), lambda j, i: (0, j))
    return pl.pallas_call(
        body, name=name, grid=(w // bw, t // tm),
        in_specs=[row, row, pl.BlockSpec((tm, bw), lambda j, i: (i, gcb + j)), vec, row],
        out_specs=[row, row, vec],
        out_shape=[jax.ShapeDtypeStruct((t, w), F32), jax.ShapeDtypeStruct((t, w), BF16),
                   jax.ShapeDtypeStruct((1, w), F32)],
        compiler_params=_cparams(("parallel", "arbitrary")),
    )(o_f, o_b, gate_arr, gn.reshape(1, w), dy)


def gla_gate_fwd(p, wg, bg, name="gla_gate_fwd"):
    t = p.shape[0]
    tm = _row_tile(t)
    w = wg.shape[1]
    gcb = OD_GA // LANE

    def body(ga_ref, wg_ref, bg_ref, la_ref):
        z = _dot(ga_ref[...], wg_ref[...]) + bg_ref[...]
        la_ref[...] = (jnp.minimum(z, 0.0) - jnp.log(1.0 + jnp.exp(-jnp.abs(z)))) * (1.0 / GLA_TAU)

    return pl.pallas_call(
        body, name=name, grid=(t // tm,),
        in_specs=[pl.BlockSpec((tm, LANE), lambda i: (i, gcb)), pl.BlockSpec((LANE, w), lambda i: (0, 0)),
                  pl.BlockSpec((1, w), lambda i: (0, 0))],
        out_specs=pl.BlockSpec((tm, w), lambda i: (i, 0)),
        out_shape=jax.ShapeDtypeStruct((t, w), F32),
        compiler_params=_cparams(("parallel",)),
    )(p, wg, bg)


def gla_gate_bwd(p, wg, wg_t, bg, dla, name="gla_gate_bwd"):
    t = p.shape[0]
    tm = _row_tile(t)
    w = wg.shape[1]
    gcb = OD_GA // LANE

    def body(ga_ref, wg_ref, wgt_ref, bg_ref, dla_ref, dga_ref, dwg_ref, dbg_ref):
        @pl.when(pl.program_id(0) == 0)
        def _():
            dwg_ref[...] = jnp.zeros_like(dwg_ref)
            dbg_ref[...] = jnp.zeros_like(dbg_ref)
        ga = ga_ref[...]
        z = _dot(ga, wg_ref[...]) + bg_ref[...]
        dz = dla_ref[...] * (1.0 / GLA_TAU) * _sigmoid(-z)
        dga_ref[...] = _dot(dz, wgt_ref[...]).astype(BF16)
        dwg_ref[...] += _dot_tn(ga, dz)
        dbg_ref[...] += jnp.sum(dz, axis=0, keepdims=True)

    return pl.pallas_call(
        body, name=name, grid=(t // tm,),
        in_specs=[pl.BlockSpec((tm, LANE), lambda i: (i, gcb)), pl.BlockSpec((LANE, w), lambda i: (0, 0)),
                  pl.BlockSpec((w, LANE), lambda i: (0, 0)), pl.BlockSpec((1, w), lambda i: (0, 0)),
                  pl.BlockSpec((tm, w), lambda i: (i, 0))],
        out_specs=[pl.BlockSpec((tm, LANE), lambda i: (i, 0)), pl.BlockSpec((LANE, w), lambda i: (0, 0)),
                   pl.BlockSpec((1, w), lambda i: (0, 0))],
        out_shape=[jax.ShapeDtypeStruct((t, LANE), BF16), jax.ShapeDtypeStruct((LANE, w), F32),
                   jax.ShapeDtypeStruct((1, w), F32)],
        compiler_params=_cparams(("arbitrary",)),
    )(p, wg, wg_t, bg, dla)


FFN_COL = 1408


def _shifted(x, prev_row, next_row, first, last):
    tm = x.shape[0]
    row = lax.broadcasted_iota(jnp.int32, x.shape, 0)
    pr = jnp.where(first, 0.0, prev_row)
    nx = jnp.where(last, 0.0, next_row)
    xm1 = jnp.where(row == 0, pr, pltpu.roll(x, 1, 0))
    xp1 = jnp.where(row == tm - 1, nx, pltpu.roll(x, tm - 1, 0))
    return xm1, xp1


def _halo_rows(dtype):
    return SUBLANE * (4 // jnp.dtype(dtype).itemsize)


def _halo_specs(tm, tc, t, colmap, rowaxis, hr):
    nbh = tm // hr
    lasth = t // hr - 1

    def prev(*ids):
        i = ids[rowaxis]
        return (jnp.maximum(i * nbh - 1, 0), colmap(*ids))

    def nxt(*ids):
        i = ids[rowaxis]
        return (jnp.minimum((i + 1) * nbh, lasth), colmap(*ids))

    return pl.BlockSpec((hr, tc), prev), pl.BlockSpec((hr, tc), nxt)


def ffn_act_fwd(up, conv_w, conv_b, name="ffn_act_fwd"):
    t = up.shape[0]
    tm, tc = _row_tile(t), FFN_COL
    ncol = D_FF // tc

    hr = _halo_rows(up.dtype)

    def body(g_ref, gp_ref, gn_ref, v_ref, w_ref, b_ref, a_ref):
        i = pl.program_id(0)
        g = g_ref[...].astype(F32)
        gm1, gp1 = _shifted(g, gp_ref[hr - 1:hr, :].astype(F32), gn_ref[0:1, :].astype(F32), i == 0,
                            i == pl.num_programs(0) - 1)
        cc = w_ref[0:1, :] * gm1 + w_ref[1:2, :] * g + w_ref[2:3, :] * gp1 + b_ref[...]
        a_ref[...] = (cc * _sigmoid(cc) * v_ref[...].astype(F32)).astype(BF16)

    prev, nxt = _halo_specs(tm, tc, t, lambda i, j: j, 0, hr)
    return pl.pallas_call(
        body, name=name, grid=(t // tm, ncol),
        in_specs=[pl.BlockSpec((tm, tc), lambda i, j: (i, j)), prev, nxt,
                  pl.BlockSpec((tm, tc), lambda i, j: (i, j + ncol)),
                  pl.BlockSpec((SUBLANE, tc), lambda i, j: (0, j)), pl.BlockSpec((1, tc), lambda i, j: (0, j))],
        out_specs=pl.BlockSpec((tm, tc), lambda i, j: (i, j)),
        out_shape=jax.ShapeDtypeStruct((t, D_FF), BF16),
        compiler_params=_cparams(("parallel", "parallel")),
    )(up, up, up, up, conv_w, conv_b)


def ffn_act_bwd(up, conv_w, conv_b, dact, name="ffn_act_bwd"):
    t = up.shape[0]
    tm, tc = _row_tile(t), FFN_COL
    ncol = D_FF // tc
    hr = _halo_rows(up.dtype)

    def body(g_ref, gp_ref, gn_ref, v_ref, w_ref, b_ref, da_ref, dc_ref, dv_ref, dw_ref):
        i = pl.program_id(1)

        @pl.when(i == 0)
        def _():
            dw_ref[...] = jnp.zeros_like(dw_ref)
        g = g_ref[...].astype(F32)
        gm1, gp1 = _shifted(g, gp_ref[hr - 1:hr, :].astype(F32), gn_ref[0:1, :].astype(F32), i == 0,
                            i == pl.num_programs(1) - 1)
        cc = w_ref[0:1, :] * gm1 + w_ref[1:2, :] * g + w_ref[2:3, :] * gp1 + b_ref[...]
        sg = _sigmoid(cc)
        da = da_ref[...]
        dv_ref[...] = (da * (cc * sg)).astype(BF16)
        dc = da * v_ref[...].astype(F32) * (sg * (1.0 + cc * (1.0 - sg)))
        dc_ref[...] = dc
        dw_ref[0:1, :] += jnp.sum(dc * gm1, axis=0, keepdims=True)
        dw_ref[1:2, :] += jnp.sum(dc * g, axis=0, keepdims=True)
        dw_ref[2:3, :] += jnp.sum(dc * gp1, axis=0, keepdims=True)
        dw_ref[3:4, :] += jnp.sum(dc, axis=0, keepdims=True)

    prev, nxt = _halo_specs(tm, tc, t, lambda j, i: j, 1, hr)
    tile = pl.BlockSpec((tm, tc), lambda j, i: (i, j))
    return pl.pallas_call(
        body, name=name, grid=(ncol, t // tm),
        in_specs=[tile, prev, nxt, pl.BlockSpec((tm, tc), lambda j, i: (i, j + ncol)),
                  pl.BlockSpec((SUBLANE, tc), lambda j, i: (0, j)), pl.BlockSpec((1, tc), lambda j, i: (0, j)), tile],
        out_specs=[tile, tile, pl.BlockSpec((SUBLANE, tc), lambda j, i: (0, j))],
        out_shape=[jax.ShapeDtypeStruct((t, D_FF), F32), jax.ShapeDtypeStruct((t, D_FF), BF16),
                   jax.ShapeDtypeStruct((SUBLANE, D_FF), F32)],
        compiler_params=_cparams(("parallel", "arbitrary")),
    )(up, up, up, up, conv_w, conv_b, dact)


def conv_transpose(dc, conv_w, name="conv_transpose"):
    t = dc.shape[0]
    tm, tc = _row_tile(t), FFN_COL
    hr = _halo_rows(dc.dtype)

    def body(d_ref, dp_ref, dn_ref, w_ref, o_ref):
        i = pl.program_id(0)
        d = d_ref[...]
        dm1, dp1 = _shifted(d, dp_ref[hr - 1:hr, :], dn_ref[0:1, :], i == 0, i == pl.num_programs(0) - 1)
        o_ref[...] = (w_ref[0:1, :] * dp1 + w_ref[1:2, :] * d + w_ref[2:3, :] * dm1).astype(BF16)

    prev, nxt = _halo_specs(tm, tc, t, lambda i, j: j, 0, hr)
    tile = pl.BlockSpec((tm, tc), lambda i, j: (i, j))
    return pl.pallas_call(
        body, name=name, grid=(t // tm, D_FF // tc),
        in_specs=[tile, prev, nxt, pl.BlockSpec((SUBLANE, tc), lambda i, j: (0, j))],
        out_specs=tile, out_shape=jax.ShapeDtypeStruct((t, D_FF), BF16),
        compiler_params=_cparams(("parallel", "parallel")),
    )(dc, dc, dc, conv_w)


def loss_head(y, target, name="loss_head"):
    t, d = y.shape
    tm = _row_tile(t)

    def body(y_ref, t_ref, dy_ref, l_ref):
        @pl.when(pl.program_id(0) == 0)
        def _():
            l_ref[...] = jnp.zeros_like(l_ref)
        e = y_ref[...] - t_ref[...]
        dy_ref[...] = e * (1.0 / d)
        rowloss = jnp.sum(e * e, axis=-1, keepdims=True) * (0.5 / d)
        l_ref[...] += jnp.sum(rowloss, axis=0, keepdims=True)

    row = pl.BlockSpec((tm, d), lambda i: (i, 0))
    return pl.pallas_call(
        body, name=name, grid=(t // tm,), in_specs=[row, row],
        out_specs=[row, pl.BlockSpec((1, 1), lambda i: (0, 0))],
        out_shape=[jax.ShapeDtypeStruct((t, d), F32), jax.ShapeDtypeStruct((1, 1), F32)],
        compiler_params=_cparams(("arbitrary",)),
    )(y, target)


def _pad_heads(w, heads, width):
    lead = w.shape[:-1]
    w = w.reshape(*lead, heads, width)
    w = jnp.pad(w, [(0, 0)] * len(lead) + [(0, 0), (0, LANE - width)])
    return w.reshape(*lead, heads * LANE)


def _unpad_heads(w, heads, width):
    lead = w.shape[:-1]
    return w.reshape(*lead, heads, LANE)[..., :width].reshape(*lead, heads * width)


def _pad_rows_heads(w, heads, width):
    return _pad_heads(w.T, heads, width).T


def _unpad_rows_heads(w, heads, width):
    return _unpad_heads(w.T, heads, width).T


_EV_REAL = MLA_Q_RANK + MLA_KV_RANK + MLA_ROPE


def prep_even(wts, dt=BF16):
    w_in_t = wts["w_in_t"]
    w_in_tp = jnp.concatenate([w_in_t[:_EV_REAL], jnp.zeros((EV_RQ - _EV_REAL, D_MODEL), w_in_t.dtype),
                               w_in_t[_EV_REAL:]], axis=0).astype(dt)
    wuq = _pad_heads(wts["w_uq"], MLA_HEADS, MLA_QK).astype(dt)
    ukv = wts["w_ukv"].reshape(MLA_KV_RANK, MLA_HEADS, MLA_NOPE + MLA_V)
    wk = _pad_heads(ukv[..., :MLA_NOPE].reshape(MLA_KV_RANK, -1), MLA_HEADS, MLA_NOPE).astype(dt)
    wv = _pad_heads(ukv[..., MLA_NOPE:].reshape(MLA_KV_RANK, -1), MLA_HEADS, MLA_V).astype(dt)
    w_out = wts["w_out"]
    wa = _pad_rows_heads(w_out[:MLA_HEADS * MLA_V], MLA_HEADS, MLA_V).astype(dt)
    wr = w_out[MLA_HEADS * MLA_V:].astype(dt)
    pad1 = lambda v, n: jnp.pad(v.astype(F32), (0, n - v.shape[0])).reshape(1, n)
    lg = lambda th: jnp.log1p(-jnp.exp2(-th.astype(F32)))
    return dict(
        w_in=w_in_tp.T, w_in_t=w_in_tp, wuq=wuq, wuq_t=wuq.T, wk=wk, wk_t=wk.T, wv=wv, wv_t=wv.T,
        wa=wa, wa_t=wa.T, wr=wr, wr_t=wr.T,
        mix_norm=wts["mix_norm"].astype(F32), q_norm=wts["q_norm"].astype(F32).reshape(1, -1),
        kv_norm=wts["kv_norm"].astype(F32).reshape(1, -1),
        qhn=pad1(wts["q_head_norm"], LANE), khn=pad1(wts["k_head_norm"], LANE),
        la_f=jnp.repeat(lg(wts["theta_fwd"]), RET_DK).reshape(1, -1),
        la_b=jnp.repeat(lg(wts["theta_bwd"]), RET_DK).reshape(1, -1),
        out_norm=wts["ret_out_norm"].astype(F32).reshape(-1),
    )


def prep_odd(wts, dt=BF16):
    w_in_t = wts["w_in_t"]
    w_in_tp = jnp.concatenate([w_in_t, jnp.zeros((OD_IN - w_in_t.shape[0], D_MODEL), w_in_t.dtype)],
                              axis=0).astype(dt)
    hk = GLA_HEADS * GLA_DK
    wg = jnp.zeros((LANE, 2 * hk), F32)
    wg = wg.at[:GLA_GATE_RANK, :hk].set(wts["w_gate_fwd"].astype(F32))
    wg = wg.at[GLA_GATE_RANK:2 * GLA_GATE_RANK, hk:].set(wts["w_gate_bwd"].astype(F32))
    wg = wg.astype(dt)
    bg = jnp.concatenate([wts["b_gate_fwd"], wts["b_gate_bwd"]]).astype(F32).reshape(1, -1)
    w_out = wts["w_out"].astype(dt)
    return dict(w_in=w_in_tp.T, w_in_t=w_in_tp, wg=wg, wg_t=wg.T, bg=bg, w_out=w_out, w_out_t=w_out.T,
                mix_norm=wts["mix_norm"].astype(F32), out_norm=wts["gla_out_norm"].astype(F32).reshape(-1))


def prep_ffn(wts, dt=BF16):
    w_up_t = wts["w_up_t"].astype(dt)
    w_down = wts["w_down"].astype(dt)
    cw = jnp.pad(wts["conv_w"].astype(F32), ((0, SUBLANE - 3), (0, 0)))
    return dict(w_up=w_up_t.T, w_up_t=w_up_t, w_down=w_down, w_down_t=w_down.T, conv_w=cw,
                conv_b=wts["conv_b"].astype(F32).reshape(1, -1), norm=wts["norm"].astype(F32))


_RET = dict(qcb=EV_RQ // LANE, kcb=EV_RK // LANE, vcb=EV_RV // LANE, la_row=True, chunk=RET_CHUNK, dvg=LANE,
            sub=2, qscale=1.0, kscale=RET_DK ** -0.5)
_GLA = dict(qcb=OD_Q // LANE, kcb=OD_K // LANE, vcb=OD_V // GLA_DV, la_row=False, chunk=GLA_CHUNK, dvg=GLA_DV,
            sub=1, qscale=GLA_DK ** -0.5, kscale=1.0)
GLA_BWD_ROWS = 256
_FWD_DIR = dict(reverse=False, inclusive=True)
_BWD_DIR = dict(reverse=True, inclusive=False)


def even_fwd(x, w, tabs_mla, tabs_ret, tag):
    h = rmsnorm_fwd(x, w["mix_norm"], name=f"{tag}_norm")
    p = mm_nn(h, w["w_in"], name=f"{tag}_in")
    q, k, v = mla_prep_fwd(p, tabs_mla, w["q_norm"], w["kv_norm"], w["wuq"], w["wk"], w["wv"], w["qhn"], w["khn"],
                           name=f"{tag}_mla_prep")
    o, lse = attn_fwd(q, k, v, name=f"{tag}_attn")
    of, stf = scan_fwd(p, p, p, w["la_f"], lacb=0, rope=tabs_ret, name=f"{tag}_ret_f", **_RET, **_FWD_DIR)
    ob, stb = scan_fwd(p, p, p, w["la_b"], lacb=0, rope=tabs_ret, name=f"{tag}_ret_b", **_RET, **_BWD_DIR)
    r = gated_norm_fwd(of, ob, p, EV_RG // LANE, w["out_norm"], RET_DV, name=f"{tag}_ret_out")
    x1 = mm_nn(o, w["wa"], res=x, name=f"{tag}_out_a")
    x2 = mm_nn(r, w["wr"], res=x1, name=f"{tag}_out_r")
    return x2, dict(x=x, h=h, p=p, q=q, k=k, v=v, o=o, lse=lse, of=of, ob=ob, stf=stf, stb=stb, r=r)


def even_bwd(dx, s, w, tabs_mla, tabs_ret, tag):
    tag = tag + "_b"
    do = mm_nn(dx, w["wa_t"], name=f"{tag}_dout_a")
    dr = mm_nn(dx, w["wr_t"], name=f"{tag}_dout_r")
    d_wa = mm_tn(s["o"], dx, out_dtype=BF16, name=f"{tag}_dwa")
    d_wr = mm_tn(s["r"], dx, out_dtype=BF16, name=f"{tag}_dwr")
    dq, dk, dv = attn_bwd(s["q"], s["k"], s["v"], s["o"], s["lse"], do, name=f"{tag}_attn")
    (dp_mla, d_wuq, d_wk, d_wv, d_qn, d_kvn, d_qhn, d_khn) = mla_prep_bwd(
        s["p"], tabs_mla, w["q_norm"], w["kv_norm"], w["wuq"], w["wk"], w["wv"], w["wuq_t"], w["wk_t"], w["wv_t"],
        w["qhn"], w["khn"], dq, dk, dv, name=f"{tag}_mla_prep")
    d_o, d_gate, d_gn = gated_norm_bwd(s["of"], s["ob"], s["p"], EV_RG // LANE, w["out_norm"], RET_DV, dr,
                                       name=f"{tag}_ret_out")
    p = s["p"]
    g1 = scan_bwd(p, p, p, w["la_f"], s["stf"], d_o, None, lacb=0, rope=tabs_ret, name=f"{tag}_ret_f",
                  **_RET, **_FWD_DIR)
    g2 = scan_bwd(p, p, p, w["la_b"], s["stb"], d_o, g1[:3], lacb=0, rope=tabs_ret, name=f"{tag}_ret_b",
                  **_RET, **_BWD_DIR)
    dp = jnp.concatenate([dp_mla, g2[0], g2[1], g2[2], d_gate], axis=1)
    dh = mm_nn(dp, w["w_in_t"], name=f"{tag}_dh")
    d_win_t = mm_tn(dp, s["h"], out_dtype=BF16, name=f"{tag}_dwin")
    dx_in, d_mix = rmsnorm_bwd(s["x"], w["mix_norm"], dh, dx, name=f"{tag}_norm")
    grads = dict(w_in_t=d_win_t, wuq=d_wuq, wk=d_wk, wv=d_wv, wa=d_wa, wr=d_wr, mix_norm=d_mix, q_norm=d_qn,
                 kv_norm=d_kvn, qhn=d_qhn, khn=d_khn, la_f=g1[3], la_b=g2[3], out_norm=d_gn)
    return dx_in, grads


def odd_fwd(x, w, tag):
    h = rmsnorm_fwd(x, w["mix_norm"], name=f"{tag}_norm")
    p = mm_nn(h, w["w_in"], name=f"{tag}_in")
    la = gla_gate_fwd(p, w["wg"], w["bg"], name=f"{tag}_gate")
    of, stf = scan_fwd(p, p, p, la, lacb=0, name=f"{tag}_gla_f", **_GLA, **_FWD_DIR)
    ob, stb = scan_fwd(p, p, p, la, lacb=N_GROUPS, name=f"{tag}_gla_b", **_GLA, **_BWD_DIR)
    y = gated_norm_fwd(of, ob, p, OD_R // GLA_DV, w["out_norm"], GLA_DV, name=f"{tag}_gla_out")
    x1 = mm_nn(y, w["w_out"], res=x, name=f"{tag}_out")
    return x1, dict(x=x, h=h, p=p, la=la, of=of, ob=ob, stf=stf, stb=stb, y=y)


def odd_bwd(dx, s, w, tag):
    tag = tag + "_b"
    dy = mm_nn(dx, w["w_out_t"], name=f"{tag}_dout")
    d_wout = mm_tn(s["y"], dx, out_dtype=BF16, name=f"{tag}_dwout")
    d_o, d_gate, d_gn = gated_norm_bwd(s["of"], s["ob"], s["p"], OD_R // GLA_DV, w["out_norm"], GLA_DV, dy,
                                       name=f"{tag}_gla_out")
    p, la = s["p"], s["la"]
    g1 = scan_bwd(p, p, p, la, s["stf"], d_o, None, lacb=0, block_rows=GLA_BWD_ROWS, name=f"{tag}_gla_f",
                  **_GLA, **_FWD_DIR)
    g2 = scan_bwd(p, p, p, la, s["stb"], d_o, g1[:3], lacb=N_GROUPS, block_rows=GLA_BWD_ROWS, name=f"{tag}_gla_b",
                  **_GLA, **_BWD_DIR)
    dla = jnp.concatenate([g1[3], g2[3]], axis=1)
    d_ga, d_wg, d_bg = gla_gate_bwd(p, w["wg"], w["wg_t"], w["bg"], dla, name=f"{tag}_gate")
    dp = jnp.concatenate([g2[0], g2[1], g2[2], d_gate, d_ga], axis=1)
    dh = mm_nn(dp, w["w_in_t"], name=f"{tag}_dh")
    d_win_t = mm_tn(dp, s["h"], out_dtype=BF16, name=f"{tag}_dwin")
    dx_in, d_mix = rmsnorm_bwd(s["x"], w["mix_norm"], dh, dx, name=f"{tag}_norm")
    grads = dict(w_in_t=d_win_t, wg=d_wg, bg=d_bg, w_out=d_wout, mix_norm=d_mix, out_norm=d_gn)
    return dx_in, grads


def ffn_fwd(x, w, tag):
    h = rmsnorm_fwd(x, w["norm"], name=f"{tag}_norm")
    up = mm_nn(h, w["w_up"], out_dtype=BF16, name=f"{tag}_up")
    act = ffn_act_fwd(up, w["conv_w"], w["conv_b"], name=f"{tag}_act")
    x1 = mm_nn(act, w["w_down"], res=x, name=f"{tag}_down")
    return x1, dict(x=x, h=h, up=up, act=act)


def ffn_bwd(dx, s, w, tag):
    tag = tag + "_b"
    dact = mm_nn(dx, w["w_down_t"], name=f"{tag}_dact")
    d_wdown = mm_tn(s["act"], dx, out_dtype=BF16, name=f"{tag}_dwdown")
    dc, dval, d_conv = ffn_act_bwd(s["up"], w["conv_w"], w["conv_b"], dact, name=f"{tag}_act")
    dgate = conv_transpose(dc, w["conv_w"], name=f"{tag}_convt")
    dh1 = mm_nn(dgate, w["w_up_t"][:D_FF], name=f"{tag}_dh_g")
    dh = mm_nn(dval, w["w_up_t"][D_FF:], res=dh1, name=f"{tag}_dh_v")
    d_wup_t = jnp.concatenate([mm_tn(dgate, s["h"], out_dtype=BF16, name=f"{tag}_dwup_g"),
                               mm_tn(dval, s["h"], out_dtype=BF16, name=f"{tag}_dwup_v")], axis=0)
    dx_in, d_norm = rmsnorm_bwd(s["x"], w["norm"], dh, dx, name=f"{tag}_norm")
    grads = dict(w_up_t=d_wup_t, w_down=d_wdown, conv_w=d_conv[:3], conv_b=d_conv[3], norm=d_norm)
    return dx_in, grads


def unprep_even_grads(g, theta_fwd, theta_bwd):
    d_win_t = jnp.concatenate([g["w_in_t"][:_EV_REAL], g["w_in_t"][EV_RQ:]], axis=0)
    d_uq = _unpad_heads(g["wuq"], MLA_HEADS, MLA_QK)
    dk_ = _unpad_heads(g["wk"], MLA_HEADS, MLA_NOPE).reshape(MLA_KV_RANK, MLA_HEADS, MLA_NOPE)
    dv_ = _unpad_heads(g["wv"], MLA_HEADS, MLA_V).reshape(MLA_KV_RANK, MLA_HEADS, MLA_V)
    d_ukv = jnp.concatenate([dk_, dv_], axis=-1).reshape(MLA_KV_RANK, -1)
    d_wout = jnp.concatenate([_unpad_rows_heads(g["wa"], MLA_HEADS, MLA_V), g["wr"]], axis=0)

    def dtheta(dla, th):
        dlg = dla.reshape(RET_HEADS, RET_DK).sum(axis=-1)
        e = jnp.exp2(-th.astype(F32))
        return dlg * (e * math.log(2.0)) / (1.0 - e)

    return dict(mix_norm=g["mix_norm"].reshape(-1), w_in_t=d_win_t, q_norm=g["q_norm"].reshape(-1),
                kv_norm=g["kv_norm"].reshape(-1), w_uq=d_uq, w_ukv=d_ukv, q_head_norm=g["qhn"].reshape(-1)[:MLA_QK],
                k_head_norm=g["khn"].reshape(-1)[:MLA_QK], theta_fwd=dtheta(g["la_f"], theta_fwd),
                theta_bwd=dtheta(g["la_b"], theta_bwd), ret_out_norm=g["out_norm"].reshape(RET_HEADS, RET_DV),
                w_out=d_wout)


def unprep_odd_grads(g):
    hk = GLA_HEADS * GLA_DK
    return dict(mix_norm=g["mix_norm"].reshape(-1), w_in_t=g["w_in_t"][:OD_GA + 2 * GLA_GATE_RANK],
                w_gate_fwd=g["wg"][:GLA_GATE_RANK, :hk], b_gate_fwd=g["bg"].reshape(-1)[:hk],
                w_gate_bwd=g["wg"][GLA_GATE_RANK:2 * GLA_GATE_RANK, hk:], b_gate_bwd=g["bg"].reshape(-1)[hk:],
                gla_out_norm=g["out_norm"].reshape(GLA_HEADS, GLA_DV), w_out=g["w_out"])


def unprep_ffn_grads(g):
    return dict(norm=g["norm"].reshape(-1), w_up_t=g["w_up_t"], conv_w=g["conv_w"], conv_b=g["conv_b"],
                w_down=g["w_down"])


def _mesh_pos():
    return tuple(lax.axis_index(n) for n in MESH_AXES)


def _slot(px, py, pc):
    return 4 * px + 2 * py + pc


def all_gather_blocks(blk, name="weight_all_gather"):
    r, w = blk.shape

    def body(x_ref, out_ref, send_sems, recv_sems, local_sem):
        x, y, c = _mesh_pos()
        me, sibling = (x, y, c), (x, y, 1 - c)
        chips = [(1 - x, y), (x, 1 - y), (1 - x, 1 - y)]

        def copy(k, block, to, src=None):
            dst = out_ref.at[_slot(*block)]
            return pltpu.make_async_remote_copy(
                src_ref=dst if src is None else src, dst_ref=dst, send_sem=send_sems.at[k],
                recv_sem=recv_sems.at[k], device_id=to, device_id_type=pl.DeviceIdType.MESH)

        mine = pltpu.make_async_copy(x_ref, out_ref.at[_slot(*me)], local_sem)
        mine.start()
        first = [copy(0, me, sibling, src=x_ref)]
        first += [copy(1 + j, me, (*chip, c), src=x_ref) for j, chip in enumerate(chips)]
        for cp in first:
            cp.start()
        passed = [copy(4 + j, (*chip, c), sibling) for j, chip in enumerate(chips)]
        for j, chip in enumerate(chips):
            copy(1 + j, (*chip, c), me).wait_recv()
            passed[j].start()
        copy(0, sibling, me).wait_recv()
        for j, chip in enumerate(chips):
            copy(4 + j, (*chip, 1 - c), me).wait_recv()
        for cp in first + passed:
            cp.wait_send()
        mine.wait()

    return pl.pallas_call(
        body, name=name,
        out_shape=jax.ShapeDtypeStruct((N_DEV, r, w), blk.dtype),
        in_specs=[pl.BlockSpec(memory_space=pl.ANY)],
        out_specs=pl.BlockSpec(memory_space=pl.ANY),
        scratch_shapes=[pltpu.SemaphoreType.DMA((7,)), pltpu.SemaphoreType.DMA((7,)), pltpu.SemaphoreType.DMA],
    )(blk)


N_CHIP = 4


def pair_exchange(other, name="grad_pair_exchange"):
    _, r, w = other.shape

    def body(s_ref, r_ref, send_sems, recv_sems):
        x, y, c = _mesh_pos()
        copies = []
        for s in range(N_CHIP):
            cp = pltpu.make_async_remote_copy(
                src_ref=s_ref.at[s], dst_ref=r_ref.at[s], send_sem=send_sems.at[s], recv_sem=recv_sems.at[s],
                device_id=(x, y, 1 - c), device_id_type=pl.DeviceIdType.MESH)
            cp.start()
            copies.append(cp)
        for cp in copies:
            cp.wait()

    return pl.pallas_call(
        body, name=name,
        out_shape=jax.ShapeDtypeStruct((N_CHIP, r, w), other.dtype),
        in_specs=[pl.BlockSpec(memory_space=pl.ANY)],
        out_specs=pl.BlockSpec(memory_space=pl.ANY),
        scratch_shapes=[pltpu.SemaphoreType.DMA((N_CHIP,)), pltpu.SemaphoreType.DMA((N_CHIP,))],
    )(other)


def chip_exchange(part, name="grad_chip_exchange"):
    _, r, w = part.shape

    def body(s_ref, r_ref, send_sems, recv_sems, local_sem):
        x, y, c = _mesh_pos()
        me = 2 * x + y
        mine = pltpu.make_async_copy(s_ref.at[me], r_ref.at[me], local_sem)
        mine.start()
        copies = []
        for k in range(1, N_CHIP):
            px = 1 - x if (k >> 1) & 1 else x
            py = 1 - y if k & 1 else y
            cp = pltpu.make_async_remote_copy(
                src_ref=s_ref.at[2 * px + py], dst_ref=r_ref.at[me], send_sem=send_sems.at[k - 1],
                recv_sem=recv_sems.at[k - 1], device_id=(px, py, c), device_id_type=pl.DeviceIdType.MESH)
            cp.start()
            copies.append(cp)
        for cp in copies:
            cp.wait()
        mine.wait()

    return pl.pallas_call(
        body, name=name,
        out_shape=jax.ShapeDtypeStruct((N_CHIP, r, w), part.dtype),
        in_specs=[pl.BlockSpec(memory_space=pl.ANY)],
        out_specs=pl.BlockSpec(memory_space=pl.ANY),
        scratch_shapes=[pltpu.SemaphoreType.DMA((N_CHIP - 1,)), pltpu.SemaphoreType.DMA((N_CHIP - 1,)),
                        pltpu.SemaphoreType.DMA],
    )(part)


FLAT_W = 1024
FLAT_TILE = 256


def pair_add(mine, theirs, name="grad_pair_add"):
    n, r, w = mine.shape

    def body(a_ref, b_ref, o_ref):
        o_ref[...] = (a_ref[...].astype(F32) + b_ref[...].astype(F32)).astype(o_ref.dtype)

    tr = _slab_tile(r)
    blk = pl.BlockSpec((n, tr, w), lambda i: (0, i, 0))
    return pl.pallas_call(
        body, name=name, grid=(r // tr,), in_specs=[blk, blk], out_specs=blk,
        out_shape=jax.ShapeDtypeStruct((n, r, w), mine.dtype),
        compiler_params=_cparams(("parallel",)),
    )(mine, theirs)


def sum_slots(recv, name="grad_sum"):
    n, r, w = recv.shape

    def body(r_ref, o_ref):
        acc = r_ref[0].astype(F32)
        for k in range(1, n):
            acc = acc + r_ref[k].astype(F32)
        o_ref[...] = acc

    tr = _slab_tile(r)
    return pl.pallas_call(
        body, name=name, grid=(r // tr,),
        in_specs=[pl.BlockSpec((n, tr, w), lambda i: (0, i, 0))],
        out_specs=pl.BlockSpec((tr, w), lambda i: (i, 0)),
        out_shape=jax.ShapeDtypeStruct((r, w), F32),
        compiler_params=_cparams(("parallel",)),
    )(recv)


def _slab_tile(r):
    return max(t for t in range(SUBLANE, FLAT_TILE + 1, SUBLANE) if r % t == 0)


def adamw(wf, gf, mf, vf, name="adamw"):
    r, w = wf.shape
    tr = _slab_tile(r)

    def body(w_ref, g_ref, m_ref, v_ref, d_ref, m_out, v_out):
        g = g_ref[...]
        m = ADAM_B1 * m_ref[...] + (1.0 - ADAM_B1) * g
        v = ADAM_B2 * v_ref[...] + (1.0 - ADAM_B2) * (g * g)
        m_hat = m / (1.0 - ADAM_B1 ** ADAM_STEP)
        v_hat = v / (1.0 - ADAM_B2 ** ADAM_STEP)
        d_ref[...] = -ADAM_LR * (m_hat / (jnp.sqrt(v_hat) + ADAM_EPS) + ADAM_WD * w_ref[...])
        m_out[...] = m
        v_out[...] = v

    tile = pl.BlockSpec((tr, w), lambda i: (i, 0))
    return pl.pallas_call(
        body, name=name, grid=(r // tr,), in_specs=[tile] * 4, out_specs=[tile] * 3,
        out_shape=[jax.ShapeDtypeStruct((r, w), F32)] * 3,
        compiler_params=_cparams(("parallel",)),
    )(wf, gf, mf, vf)


ROW_FORM = [("w_in_even", "T"), ("w_out_even", "R"), ("w_in_odd", "T"), ("w_out_odd", "R"), ("ffn_w_up", "T"),
            ("ffn_w_down", "R")]
SHARDED_MID = [("mla_w_uq", 2), ("mla_w_ukv", 2)]
SHARDED_SMALL = [("mix_norm_odd", 1), ("gla_w_gate_fwd", 2), ("gla_b_gate_fwd", 1), ("gla_w_gate_bwd", 2),
                 ("gla_b_gate_bwd", 1), ("gla_out_norm", 2), ("ffn_conv_w", 2)]
REPLICATED = ["mix_norm_even", "mla_q_norm", "mla_kv_norm", "mla_q_head_norm", "mla_k_head_norm", "ret_theta_fwd",
              "ret_theta_bwd", "ret_out_norm", "ffn_norm", "ffn_conv_b"]
WEIGHT_NAMES = ["mix_norm_even", "w_in_even", "mla_q_norm", "mla_kv_norm", "mla_w_uq", "mla_w_ukv",
                "mla_q_head_norm", "mla_k_head_norm", "ret_theta_fwd", "ret_theta_bwd", "ret_out_norm", "w_out_even",
                "mix_norm_odd", "w_in_odd", "gla_w_gate_fwd", "gla_b_gate_fwd", "gla_w_gate_bwd", "gla_b_gate_bwd",
                "gla_out_norm", "w_out_odd", "ffn_norm", "ffn_w_up", "ffn_conv_w", "ffn_conv_b", "ffn_w_down"]


def _round_up(n, m):
    return -(-n // m) * m


def _pack_rows(parts, rows):
    flat = jnp.concatenate(parts, axis=-1)
    pad = rows * FLAT_W - flat.shape[-1]
    flat = jnp.pad(flat, [(0, 0)] * (flat.ndim - 1) + [(0, pad)])
    return flat.reshape(*flat.shape[:-1], rows, FLAT_W)


def _row_form(v, form):
    if form == "T":
        v = jnp.swapaxes(v, 1, 2)
    return v.reshape(-1, v.shape[-1])


def _row_counts(local_shapes):
    return [local_shapes[n][0] * local_shapes[n][2 if f == "T" else 1] for n, f in ROW_FORM]


def _tail_layout(local_shapes):
    n_sh = sum(math.prod(local_shapes[n]) for n, _ in SHARDED_MID + SHARDED_SMALL)
    n_rep = sum(math.prod(local_shapes[n]) for n in REPLICATED)
    sh_rows = _round_up(-(-n_sh // FLAT_W), SUBLANE)
    rep_rows = _round_up(-(-n_rep // FLAT_W), SUBLANE)
    return sh_rows, rep_rows, _round_up(sh_rows + rep_rows, FLAT_TILE)


def pack_tail(vals, local_shapes):
    sh_rows, rep_rows, rows = _tail_layout(local_shapes)
    sh = _pack_rows([vals[n].astype(F32).reshape(-1) for n, _ in SHARDED_MID + SHARDED_SMALL], sh_rows)
    rep = _pack_rows([vals[n].astype(F32).reshape(-1) for n in REPLICATED], rep_rows)
    return jnp.concatenate([sh, rep, jnp.zeros((rows - sh_rows - rep_rows, FLAT_W), F32)], axis=0)


def unpack_tail(tail, local_shapes):
    sh_rows, rep_rows, _ = _tail_layout(local_shapes)
    out = {}
    for names, flat in (([n for n, _ in SHARDED_MID + SHARDED_SMALL], tail[:sh_rows].reshape(-1)),
                        (REPLICATED, tail[sh_rows:sh_rows + rep_rows].reshape(-1))):
        off = 0
        for n in names:
            k = math.prod(local_shapes[n])
            out[n] = flat[off:off + k].reshape(local_shapes[n])
            off += k
    return out


def unpack_rows(slab, local_shapes):
    out = {}
    off = 0
    for (n, form), rows in zip(ROW_FORM, _row_counts(local_shapes)):
        loc = local_shapes[n]
        piece = slab[off:off + rows]
        if form == "T":
            piece = jnp.swapaxes(piece.reshape(loc[0], loc[2], loc[1]), 1, 2)
        out[n] = piece.reshape(loc)
        off += rows
    return out


def pack_grad_blocks(full_grads, local_shapes):
    sh_rows, rep_rows, rows = _tail_layout(local_shapes)
    my_c = lax.axis_index("c")

    def by_core(blocks8):
        b = blocks8.reshape(N_CHIP, 2, *blocks8.shape[1:])
        return (lax.dynamic_index_in_dim(b, my_c, 1, keepdims=False),
                lax.dynamic_index_in_dim(b, 1 - my_c, 1, keepdims=False))

    blocks = []
    for n, form in ROW_FORM:
        g = full_grads[n].astype(BF16)
        layers, total = g.shape[0], g.shape[1]
        g = g.reshape(layers, N_DEV, total // N_DEV, FLAT_W)
        blocks.append(by_core(jnp.swapaxes(g, 0, 1).reshape(N_DEV, -1, FLAT_W)))
    parts = []
    for n, ax in SHARDED_MID + SHARDED_SMALL:
        g = full_grads[n].astype(F32)
        loc = local_shapes[n]
        g = g.reshape(*g.shape[:ax], N_DEV, loc[ax], *g.shape[ax + 1:])
        parts.append(jnp.moveaxis(g, ax, 0).reshape(N_DEV, -1))
    sh = _pack_rows(parts, sh_rows)
    rep = _pack_rows([full_grads[n].astype(F32).reshape(-1) for n in REPLICATED], rep_rows)
    rep = jnp.broadcast_to(rep[None], (N_DEV, rep_rows, FLAT_W))
    pad = jnp.zeros((N_DEV, rows - sh_rows - rep_rows, FLAT_W), F32)
    blocks.append(by_core(jnp.concatenate([sh, rep, pad], axis=1).astype(BF16)))
    return (jnp.concatenate([b[0] for b in blocks], axis=1), jnp.concatenate([b[1] for b in blocks], axis=1))


def pack_gather_block(vals, local_shapes):
    big = jnp.concatenate([_row_form(vals[n].astype(BF16), f) for n, f in ROW_FORM], axis=0)
    mid = [vals[n].astype(BF16).reshape(-1) for n, _ in SHARDED_MID]
    small = jnp.concatenate([vals[n].astype(F32).reshape(-1) for n, _ in SHARDED_SMALL])
    small = lax.bitcast_convert_type(small, BF16).reshape(-1)
    n = sum(v.shape[0] for v in mid) + small.shape[0]
    tail = _pack_rows(mid + [small], _round_up(-(-n // FLAT_W), 2 * SUBLANE))
    return jnp.concatenate([big, tail], axis=0)


def unpack_gathered(gathered, local_shapes):
    out = {}
    off = 0
    for (n, form), rows in zip(ROW_FORM, _row_counts(local_shapes)):
        layers = local_shapes[n][0]
        piece = gathered[:, off:off + rows].reshape(N_DEV, layers, rows // layers, FLAT_W)
        out[n] = jnp.swapaxes(piece, 0, 1).reshape(layers, N_DEV * (rows // layers), FLAT_W)
        off += rows
    flat = gathered[:, off:].reshape(N_DEV, -1)
    off = 0

    def full(piece, n, ax):
        loc = local_shapes[n]
        piece = jnp.moveaxis(piece.reshape(N_DEV, *loc), 0, ax)
        return piece.reshape(*loc[:ax], N_DEV * loc[ax], *loc[ax + 1:])

    for n, ax in SHARDED_MID:
        k = math.prod(local_shapes[n])
        out[n] = full(flat[:, off:off + k], n, ax)
        off += k
    for n, ax in SHARDED_SMALL:
        k = math.prod(local_shapes[n])
        pairs = flat[:, off:off + 2 * k].reshape(N_DEV, k, 2)
        out[n] = full(lax.bitcast_convert_type(pairs, F32), n, ax)
        off += 2 * k
    return out


def local_step(x, positions, target, fw):
    tabs_mla = rope_tables(positions, MLA_NOPE, MLA_ROPE // 2, LANE)
    tabs_ret = rope_tables(positions, 0, RET_DK // 2, RET_DK)
    layers = []
    for layer in range(DEPTH):
        i = layer // 2
        if layer % 2 == 0:
            mw = prep_even(dict(
                w_in_t=fw["w_in_even"][i], w_uq=fw["mla_w_uq"][i], w_ukv=fw["mla_w_ukv"][i], w_out=fw["w_out_even"][i],
                mix_norm=fw["mix_norm_even"][i], q_norm=fw["mla_q_norm"][i], kv_norm=fw["mla_kv_norm"][i],
                q_head_norm=fw["mla_q_head_norm"][i], k_head_norm=fw["mla_k_head_norm"][i],
                theta_fwd=fw["ret_theta_fwd"][i], theta_bwd=fw["ret_theta_bwd"][i],
                ret_out_norm=fw["ret_out_norm"][i]))
        else:
            mw = prep_odd(dict(
                w_in_t=fw["w_in_odd"][i], w_gate_fwd=fw["gla_w_gate_fwd"][i], b_gate_fwd=fw["gla_b_gate_fwd"][i],
                w_gate_bwd=fw["gla_w_gate_bwd"][i], b_gate_bwd=fw["gla_b_gate_bwd"][i],
                gla_out_norm=fw["gla_out_norm"][i], w_out=fw["w_out_odd"][i], mix_norm=fw["mix_norm_odd"][i]))
        fwt = prep_ffn(dict(norm=fw["ffn_norm"][layer], w_up_t=fw["ffn_w_up"][layer], conv_w=fw["ffn_conv_w"][layer],
                            conv_b=fw["ffn_conv_b"][layer], w_down=fw["ffn_w_down"][layer]))
        layers.append((mw, fwt))

    saved = []
    for layer, (mw, fwt) in enumerate(layers):
        if layer % 2 == 0:
            x, sm = even_fwd(x, mw, tabs_mla, tabs_ret, f"l{layer}_mix")
        else:
            x, sm = odd_fwd(x, mw, f"l{layer}_mix")
        x, sf = ffn_fwd(x, fwt, f"l{layer}_ffn")
        saved.append((sm, sf))

    dx, loss = loss_head(x, target, name="loss_head")

    per_layer = [None] * DEPTH
    for layer in reversed(range(DEPTH)):
        mw, fwt = layers[layer]
        sm, sf = saved[layer]
        dx, gf = ffn_bwd(dx, sf, fwt, f"l{layer}_ffn")
        if layer % 2 == 0:
            dx, gm = even_bwd(dx, sm, mw, tabs_mla, tabs_ret, f"l{layer}_mix")
            gm = unprep_even_grads(gm, fw["ret_theta_fwd"][layer // 2], fw["ret_theta_bwd"][layer // 2])
        else:
            dx, gm = odd_bwd(dx, sm, mw, f"l{layer}_mix")
            gm = unprep_odd_grads(gm)
        per_layer[layer] = (gm, unprep_ffn_grads(gf))

    ev = [per_layer[l][0] for l in range(0, DEPTH, 2)]
    od = [per_layer[l][0] for l in range(1, DEPTH, 2)]
    ff = [per_layer[l][1] for l in range(DEPTH)]
    st = lambda lst, key: jnp.stack([g[key] for g in lst])
    grads = {
        "mix_norm_even": st(ev, "mix_norm"), "w_in_even": st(ev, "w_in_t"), "mla_q_norm": st(ev, "q_norm"),
        "mla_kv_norm": st(ev, "kv_norm"), "mla_w_uq": st(ev, "w_uq"), "mla_w_ukv": st(ev, "w_ukv"),
        "mla_q_head_norm": st(ev, "q_head_norm"), "mla_k_head_norm": st(ev, "k_head_norm"),
        "ret_theta_fwd": st(ev, "theta_fwd"), "ret_theta_bwd": st(ev, "theta_bwd"),
        "ret_out_norm": st(ev, "ret_out_norm"), "w_out_even": st(ev, "w_out"),
        "mix_norm_odd": st(od, "mix_norm"), "w_in_odd": st(od, "w_in_t"), "gla_w_gate_fwd": st(od, "w_gate_fwd"),
        "gla_b_gate_fwd": st(od, "b_gate_fwd"), "gla_w_gate_bwd": st(od, "w_gate_bwd"),
        "gla_b_gate_bwd": st(od, "b_gate_bwd"), "gla_out_norm": st(od, "gla_out_norm"), "w_out_odd": st(od, "w_out"),
        "ffn_norm": st(ff, "norm"), "ffn_w_up": st(ff, "w_up_t"), "ffn_conv_w": st(ff, "conv_w"),
        "ffn_conv_b": st(ff, "conv_b"), "ffn_w_down": st(ff, "w_down"),
    }
    return loss, dx, grads


def kernel(x, positions, mix_norm_even, w_in_even, mla_q_norm, mla_kv_norm, mla_w_uq, mla_w_ukv, mla_q_head_norm, mla_k_head_norm, ret_theta_fwd, ret_theta_bwd, ret_out_norm, w_out_even, mix_norm_odd, w_in_odd, gla_w_gate_fwd, gla_b_gate_fwd, gla_w_gate_bwd, gla_b_gate_bwd, gla_out_norm, w_out_odd, ffn_norm, ffn_w_up, ffn_conv_w, ffn_conv_b, ffn_w_down, loss_target, m_mix_norm_even, m_w_in_even, m_mla_q_norm, m_mla_kv_norm, m_mla_w_uq, m_mla_w_ukv, m_mla_q_head_norm, m_mla_k_head_norm, m_ret_theta_fwd, m_ret_theta_bwd, m_ret_out_norm, m_w_out_even, m_mix_norm_odd, m_w_in_odd, m_gla_w_gate_fwd, m_gla_b_gate_fwd, m_gla_w_gate_bwd, m_gla_b_gate_bwd, m_gla_out_norm, m_w_out_odd, m_ffn_norm, m_ffn_w_up, m_ffn_conv_w, m_ffn_conv_b, m_ffn_w_down, v_mix_norm_even, v_w_in_even, v_mla_q_norm, v_mla_kv_norm, v_mla_w_uq, v_mla_w_ukv, v_mla_q_head_norm, v_mla_k_head_norm, v_ret_theta_fwd, v_ret_theta_bwd, v_ret_out_norm, v_w_out_even, v_mix_norm_odd, v_w_in_odd, v_gla_w_gate_fwd, v_gla_b_gate_fwd, v_gla_w_gate_bwd, v_gla_b_gate_bwd, v_gla_out_norm, v_w_out_odd, v_ffn_norm, v_ffn_w_up, v_ffn_conv_w, v_ffn_conv_b, v_ffn_w_down):
    a = dict(locals())
    wts = {n: a[n] for n in WEIGHT_NAMES}
    local_shapes = {n: tuple(wts[n].shape) for n in WEIGHT_NAMES}

    gathered = all_gather_blocks(pack_gather_block(wts, local_shapes))
    fw = unpack_gathered(gathered, local_shapes)
    for n in REPLICATED:
        fw[n] = wts[n]

    loss, grad_x, grads = local_step(x[0], positions, loss_target[0], fw)

    mine, other = pack_grad_blocks(grads, local_shapes)
    chip_part = pair_add(mine, pair_exchange(other))
    g_slab = sum_slots(chip_exchange(chip_part))
    n_big = sum(_row_counts(local_shapes))
    ms = {n: a["m_" + n] for n in WEIGHT_NAMES}
    vs = {n: a["v_" + n] for n in WEIGHT_NAMES}

    g_out = unpack_rows(g_slab[:n_big], local_shapes)
    d_out, m_out, v_out = {}, {}, {}
    for n, _ in ROW_FORM:
        loc = local_shapes[n]
        two_d = lambda t: t.reshape(-1, loc[-1])
        d, m, v = adamw(two_d(wts[n]), two_d(g_out[n]), two_d(ms[n]), two_d(vs[n]), name=f"adamw_{n}")
        d_out[n], m_out[n], v_out[n] = d.reshape(loc), m.reshape(loc), v.reshape(loc)
    g_tail = g_slab[n_big:]
    d_tail, m_tail, v_tail = adamw(pack_tail(wts, local_shapes), g_tail, pack_tail(ms, local_shapes),
                                   pack_tail(vs, local_shapes), name="adamw_small")
    g_out.update(unpack_tail(g_tail, local_shapes))
    d_out.update(unpack_tail(d_tail, local_shapes))
    m_out.update(unpack_tail(m_tail, local_shapes))
    v_out.update(unpack_tail(v_tail, local_shapes))
    total = lax.psum(loss[0, 0], MESH_AXES)
    return (total, grad_x[None], *[g_out[n] for n in WEIGHT_NAMES], *[d_out[n] for n in WEIGHT_NAMES],
            *[m_out[n] for n in WEIGHT_NAMES], *[v_out[n] for n in WEIGHT_NAMES])
```

```python
import math

import jax
import jax.numpy as jnp
from jax import lax
from jax.experimental import pallas as pl
from jax.experimental.pallas import tpu as pltpu

F32 = jnp.float32
BF16 = jnp.bfloat16

D_MODEL = 1024
DEPTH = 4
N_DEV = 8
MESH_AXES = ("x", "y", "c")

MLA_HEADS = 8
MLA_Q_RANK = 384
MLA_KV_RANK = 256
MLA_NOPE = 64
MLA_ROPE = 32
MLA_V = 64
MLA_QK = MLA_NOPE + MLA_ROPE
RET_HEADS = 8
RET_DK = 64
RET_DV = 64
RET_CHUNK = 128
GLA_HEADS = 4
GLA_DK = 128
GLA_DV = 256
GLA_GATE_RANK = 16
GLA_TAU = 16.0
GLA_CHUNK = 64
D_FF = 2816
ROPE_THETA = 10000.0
EPS = 1e-6

ADAM_LR = 0.001
ADAM_B1 = 0.9
ADAM_B2 = 0.999
ADAM_EPS = 1e-08
ADAM_WD = 0.01
ADAM_STEP = 10

LANE = 128
SUBLANE = 8
ROW_TILE = 512
VMEM_LIMIT = 56 * 1024 * 1024
WEIGHT_TILE_BYTES = 8 * 1024 * 1024

EV_CQ, EV_CKV, EV_KR, EV_RQ, EV_RK, EV_RV, EV_RG, EV_IN = 0, 384, 640, 768, 1280, 1792, 2304, 2816
OD_Q, OD_K, OD_V, OD_R, OD_GA, OD_IN = 0, 512, 1024, 2048, 3072, 3200
N_GROUPS = 4


def _cparams(sem):
    return pltpu.CompilerParams(dimension_semantics=sem, vmem_limit_bytes=VMEM_LIMIT)


def _dot(a, b):
    return jnp.dot(a.astype(BF16), b.astype(BF16), preferred_element_type=F32)


def _dot_nt(a, b):
    return lax.dot_general(a.astype(BF16), b.astype(BF16), (((1,), (1,)), ((), ())), preferred_element_type=F32)


def _dot_tn(a, b):
    return lax.dot_general(a.astype(BF16), b.astype(BF16), (((0,), (0,)), ((), ())), preferred_element_type=F32)


def _sigmoid(x):
    return 1.0 / (1.0 + jnp.exp(-x))


def _col_tile(k, n, itemsize=2):
    best = LANE
    for t in range(LANE, n + 1, LANE):
        if n % t == 0 and k * t * itemsize <= WEIGHT_TILE_BYTES:
            best = t
    return best if n % LANE == 0 else n


def _row_tile(m):
    return min(ROW_TILE, m)


def mm_nn(a, b, res=None, out_dtype=F32, name="mm_nn"):
    m, k = a.shape
    n = b.shape[1]
    tm, tn = min(2 * ROW_TILE, m), _col_tile(k, n)

    def body(*refs):
        if res is None:
            a_ref, b_ref, o_ref = refs
        else:
            a_ref, b_ref, r_ref, o_ref = refs
        acc = _dot(a_ref[...], b_ref[...])
        if res is not None:
            acc = acc + r_ref[...].astype(F32)
        o_ref[...] = acc.astype(out_dtype)

    in_specs = [pl.BlockSpec((tm, k), lambda j, i: (i, 0)), pl.BlockSpec((k, tn), lambda j, i: (0, j))]
    args = [a, b]
    if res is not None:
        in_specs.append(pl.BlockSpec((tm, tn), lambda j, i: (i, j)))
        args.append(res)
    return pl.pallas_call(
        body, name=name, grid=(n // tn, m // tm), in_specs=in_specs,
        out_specs=pl.BlockSpec((tm, tn), lambda j, i: (i, j)),
        out_shape=jax.ShapeDtypeStruct((m, n), out_dtype),
        compiler_params=_cparams(("parallel", "parallel")),
    )(*args)


def mm_tn(a, b, out_dtype=F32, name="mm_tn"):
    t, k = a.shape
    n = b.shape[1]
    tt = min(2 * ROW_TILE, t)
    tk = k if k <= 1024 else _col_tile(1024, k, 4)
    tn = n if n <= 1024 else _col_tile(1024, n, 4)

    def body(a_ref, b_ref, o_ref, acc_s):
        s = pl.program_id(2)

        @pl.when(s == 0)
        def _():
            acc_s[...] = jnp.zeros_like(acc_s)
        acc_s[...] += _dot_tn(a_ref[...], b_ref[...])

        @pl.when(s == pl.num_programs(2) - 1)
        def _():
            o_ref[...] = acc_s[...].astype(out_dtype)

    return pl.pallas_call(
        body, name=name, grid=(k // tk, n // tn, t // tt),
        in_specs=[pl.BlockSpec((tt, tk), lambda i, j, s: (s, i)), pl.BlockSpec((tt, tn), lambda i, j, s: (s, j))],
        out_specs=pl.BlockSpec((tk, tn), lambda i, j, s: (i, j)),
        out_shape=jax.ShapeDtypeStruct((k, n), out_dtype),
        scratch_shapes=[pltpu.VMEM((tk, tn), F32)],
        compiler_params=_cparams(("parallel", "parallel", "arbitrary")),
    )(a, b)


def rmsnorm_fwd(x, g, name="rmsnorm_fwd"):
    t, d = x.shape
    tm = _row_tile(t)

    def body(x_ref, g_ref, h_ref):
        xv = x_ref[...]
        r = lax.rsqrt(jnp.mean(xv * xv, axis=-1, keepdims=True) + EPS)
        h_ref[...] = (xv * r * g_ref[...]).astype(BF16)

    return pl.pallas_call(
        body, name=name, grid=(t // tm,),
        in_specs=[pl.BlockSpec((tm, d), lambda i: (i, 0)), pl.BlockSpec((1, d), lambda i: (0, 0))],
        out_specs=pl.BlockSpec((tm, d), lambda i: (i, 0)),
        out_shape=jax.ShapeDtypeStruct((t, d), BF16),
        compiler_params=_cparams(("parallel",)),
    )(x, g.reshape(1, d))


def rmsnorm_bwd(x, g, dh, dres, name="rmsnorm_bwd"):
    t, d = x.shape
    tm = _row_tile(t)

    def body(x_ref, g_ref, dh_ref, dres_ref, dx_ref, dg_ref):
        @pl.when(pl.program_id(0) == 0)
        def _():
            dg_ref[...] = jnp.zeros_like(dg_ref)
        xv = x_ref[...]
        r = lax.rsqrt(jnp.mean(xv * xv, axis=-1, keepdims=True) + EPS)
        xh = xv * r
        dhv = dh_ref[...]
        dg_ref[...] += jnp.sum(dhv * xh, axis=0, keepdims=True)
        dxh = dhv * g_ref[...]
        dx_ref[...] = dres_ref[...] + r * (dxh - xh * jnp.mean(dxh * xh, axis=-1, keepdims=True))

    row = pl.BlockSpec((tm, d), lambda i: (i, 0))
    vec = pl.BlockSpec((1, d), lambda i: (0, 0))
    return pl.pallas_call(
        body, name=name, grid=(t // tm,),
        in_specs=[row, vec, row, row], out_specs=[row, vec],
        out_shape=[jax.ShapeDtypeStruct((t, d), F32), jax.ShapeDtypeStruct((1, d), F32)],
        compiler_params=_cparams(("arbitrary",)),
    )(x, g.reshape(1, d), dh, dres)


def _rope_apply(x, cos, s1, s2, half):
    return x * cos + pltpu.roll(x, LANE - half, 1) * s1 + pltpu.roll(x, half, 1) * s2


def _rope_transpose(dy, cos, s1, s2, half):
    return dy * cos + pltpu.roll(dy * s1, half, 1) + pltpu.roll(dy * s2, LANE - half, 1)


def rope_tables(positions, lane_start, half, period):
    pos = positions.reshape(-1).astype(F32)
    inv = ROPE_THETA ** (-jnp.arange(half, dtype=F32) / half)
    ang = pos[:, None] * inv[None, :]
    cos, sin = jnp.cos(ang), jnp.sin(ang)
    t = pos.shape[0]
    pre = lane_start
    post = period - lane_start - 2 * half
    ones = lambda n: jnp.ones((t, n), F32)
    zeros = lambda n: jnp.zeros((t, n), F32)
    c = jnp.concatenate([ones(pre), cos, cos, ones(post)], axis=1)
    a = jnp.concatenate([zeros(pre), -sin, zeros(half), zeros(post)], axis=1)
    b = jnp.concatenate([zeros(pre), zeros(half), sin, zeros(post)], axis=1)
    rep = LANE // period
    return tuple(jnp.tile(v, (1, rep)) for v in (c, a, b))


def _mla_forward_tile(p, cos, s1, s2, qn_g, kvn_g, wuq, wk, wv, qhn, khn):
    cq = p[:, EV_CQ:EV_CKV]
    ckv = p[:, EV_CKV:EV_KR]
    kr = p[:, EV_KR:EV_RQ]
    rq = lax.rsqrt(jnp.mean(cq * cq, axis=-1, keepdims=True) + EPS)
    rkv = lax.rsqrt(jnp.mean(ckv * ckv, axis=-1, keepdims=True) + EPS)
    qn = cq * rq * qn_g
    kvn = ckv * rkv * kvn_g
    q_raw = _dot(qn, wuq)
    k_raw = _dot(kvn, wk)
    v = _dot(kvn, wv)
    krp = pltpu.roll(kr, MLA_NOPE, 1)
    return cq, ckv, rq, rkv, qn, kvn, q_raw, k_raw, v, krp


def _head_norm(xh, g):
    r = lax.rsqrt(jnp.sum(xh * xh, axis=-1, keepdims=True) * (1.0 / MLA_QK) + EPS)
    return xh * r * g, r


def mla_prep_fwd(p, tabs, qn_g, kvn_g, wuq, wk, wv, qhn, khn, name="mla_prep_fwd"):
    t = p.shape[0]
    tm = _row_tile(t)
    hw = MLA_HEADS * LANE

    def body(p_ref, c_ref, s1_ref, s2_ref, qn_ref, kvn_ref, wuq_ref, wk_ref, wv_ref, qhn_ref, khn_ref,
             q_out, k_out, v_out):
        cos, s1, s2 = c_ref[...], s1_ref[...], s2_ref[...]
        (_, _, _, _, _, _, q_raw, k_raw, v, krp) = _mla_forward_tile(
            p_ref[...], cos, s1, s2, qn_ref[...], kvn_ref[...], wuq_ref[...], wk_ref[...], wv_ref[...],
            qhn_ref[...], khn_ref[...])
        v_out[...] = v.astype(BF16)
        for h in range(MLA_HEADS):
            sl = slice(h * LANE, (h + 1) * LANE)
            qh, _ = _head_norm(q_raw[:, sl], qhn_ref[...])
            kh, _ = _head_norm(k_raw[:, sl] + krp, khn_ref[...])
            q_out[:, sl] = (_rope_apply(qh, cos, s1, s2, MLA_ROPE // 2) * ATTN_QSCALE).astype(BF16)
            k_out[:, sl] = _rope_apply(kh, cos, s1, s2, MLA_ROPE // 2).astype(BF16)

    row = lambda w: pl.BlockSpec((tm, w), lambda i: (i, 0))
    full = lambda a: pl.BlockSpec(a.shape, lambda i: (0,) * a.ndim)
    ws = [qn_g, kvn_g, wuq, wk, wv, qhn, khn]
    return pl.pallas_call(
        body, name=name, grid=(t // tm,),
        in_specs=[row(EV_RQ), row(LANE), row(LANE), row(LANE)] + [full(w) for w in ws],
        out_specs=[row(hw)] * 3,
        out_shape=[jax.ShapeDtypeStruct((t, hw), BF16)] * 3,
        compiler_params=_cparams(("parallel",)),
    )(p, *tabs, *ws)


def mla_prep_bwd(p, tabs, qn_g, kvn_g, wuq, wk, wv, wuq_t, wk_t, wv_t, qhn, khn, dq, dk, dv,
                 name="mla_prep_bwd"):
    t = p.shape[0]
    tm = _row_tile(t)
    hw = MLA_HEADS * LANE

    def body(p_ref, c_ref, s1_ref, s2_ref, qn_ref, kvn_ref, wuq_ref, wk_ref, wv_ref, wuqt_ref, wkt_ref, wvt_ref,
             qhn_ref, khn_ref, dq_ref, dk_ref, dv_ref,
             dp_ref, dwuq_ref, dwk_ref, dwv_ref, dqn_ref, dkvn_ref, dqhn_ref, dkhn_ref, dqraw_s, dkraw_s):
        @pl.when(pl.program_id(0) == 0)
        def _():
            for r in (dwuq_ref, dwk_ref, dwv_ref, dqn_ref, dkvn_ref, dqhn_ref, dkhn_ref):
                r[...] = jnp.zeros_like(r)
        cos, s1, s2 = c_ref[...], s1_ref[...], s2_ref[...]
        qhn_v, khn_v = qhn_ref[...], khn_ref[...]
        (cq, ckv, rq, rkv, qn, kvn, q_raw, k_raw, _, krp) = _mla_forward_tile(
            p_ref[...], cos, s1, s2, qn_ref[...], kvn_ref[...], wuq_ref[...], wk_ref[...], wv_ref[...],
            qhn_v, khn_v)
        half = MLA_ROPE // 2
        dkr_sum = jnp.zeros((tm, LANE), F32)
        dqhn_acc = jnp.zeros((1, LANE), F32)
        dkhn_acc = jnp.zeros((1, LANE), F32)
        for h in range(MLA_HEADS):
            sl = slice(h * LANE, (h + 1) * LANE)
            xq = q_raw[:, sl]
            _, r = _head_norm(xq, qhn_v)
            xh = xq * r
            dy = _rope_transpose(dq_ref[:, sl] * ATTN_SCALE, cos, s1, s2, half)
            dqhn_acc = dqhn_acc + jnp.sum(dy * xh, axis=0, keepdims=True)
            dxh = dy * qhn_v
            dqraw_s[:, sl] = r * (dxh - xh * (jnp.sum(dxh * xh, axis=-1, keepdims=True) * (1.0 / MLA_QK)))
            xk = k_raw[:, sl] + krp
            _, r = _head_norm(xk, khn_v)
            xh = xk * r
            dy = _rope_transpose(dk_ref[:, sl] * math.log(2.0), cos, s1, s2, half)
            dkhn_acc = dkhn_acc + jnp.sum(dy * xh, axis=0, keepdims=True)
            dxh = dy * khn_v
            dxk = r * (dxh - xh * (jnp.sum(dxh * xh, axis=-1, keepdims=True) * (1.0 / MLA_QK)))
            dkraw_s[:, sl] = dxk
            dkr_sum = dkr_sum + dxk
        dqhn_ref[...] += dqhn_acc
        dkhn_ref[...] += dkhn_acc
        dq_raw = dqraw_s[...]
        dk_raw = dkraw_s[...]
        dvv = dv_ref[...]
        dwuq_ref[...] += _dot_tn(qn, dq_raw)
        dwk_ref[...] += _dot_tn(kvn, dk_raw)
        dwv_ref[...] += _dot_tn(kvn, dvv)
        dqn = _dot(dq_raw, wuqt_ref[...])
        dkvn = _dot(dk_raw, wkt_ref[...]) + _dot(dvv, wvt_ref[...])
        xh = cq * rq
        dqn_ref[...] += jnp.sum(dqn * xh, axis=0, keepdims=True)
        dxh = dqn * qn_ref[...]
        dcq = rq * (dxh - xh * jnp.mean(dxh * xh, axis=-1, keepdims=True))
        dp_ref[:, EV_CQ:EV_CKV] = dcq.astype(BF16)
        xh = ckv * rkv
        dkvn_ref[...] += jnp.sum(dkvn * xh, axis=0, keepdims=True)
        dxh = dkvn * kvn_ref[...]
        dckv = rkv * (dxh - xh * jnp.mean(dxh * xh, axis=-1, keepdims=True))
        dp_ref[:, EV_CKV:EV_KR] = dckv.astype(BF16)
        lane = lax.broadcasted_iota(jnp.int32, (tm, LANE), 1)
        dkr = jnp.where(lane < MLA_ROPE, pltpu.roll(dkr_sum, LANE - MLA_NOPE, 1), 0.0)
        dp_ref[:, EV_KR:EV_RQ] = dkr.astype(BF16)

    row = lambda w: pl.BlockSpec((tm, w), lambda i: (i, 0))
    full = lambda a: pl.BlockSpec(a.shape, lambda i: (0,) * a.ndim)
    ws = [qn_g, kvn_g, wuq, wk, wv, wuq_t, wk_t, wv_t, qhn, khn]
    outs = [jax.ShapeDtypeStruct((t, EV_RQ), BF16), jax.ShapeDtypeStruct(wuq.shape, F32),
            jax.ShapeDtypeStruct(wk.shape, F32), jax.ShapeDtypeStruct(wv.shape, F32),
            jax.ShapeDtypeStruct(qn_g.shape, F32), jax.ShapeDtypeStruct(kvn_g.shape, F32),
            jax.ShapeDtypeStruct(qhn.shape, F32), jax.ShapeDtypeStruct(khn.shape, F32)]
    return pl.pallas_call(
        body, name=name, grid=(t // tm,),
        in_specs=[row(EV_RQ), row(LANE), row(LANE), row(LANE)] + [full(w) for w in ws] + [row(hw)] * 3,
        out_specs=[row(EV_RQ)] + [full(o) for o in outs[1:]],
        out_shape=outs,
        scratch_shapes=[pltpu.VMEM((tm, hw), F32), pltpu.VMEM((tm, hw), F32)],
        compiler_params=_cparams(("arbitrary",)),
    )(p, *tabs, *ws, dq, dk, dv)


ATTN_SCALE = MLA_QK ** -0.5
ATTN_QSCALE = ATTN_SCALE * math.log2(math.e)
ATTN_FWD_TQ, ATTN_FWD_TK = 512, 8192
ATTN_BWD_TQ, ATTN_BWD_TK = 512, 4096


def attn_fwd(q, k, v, name="attn_fwd"):
    t = q.shape[0]
    tq, tk = min(ATTN_FWD_TQ, _row_tile(t)), min(ATTN_FWD_TK, t)
    nh = MLA_HEADS

    def body(q_ref, k_ref, v_ref, o_ref, lse_ref, m_s, l_s, acc_s):
        j = pl.program_id(2)

        @pl.when(j == 0)
        def _():
            m_s[...] = jnp.full_like(m_s, -jnp.inf)
            l_s[...] = jnp.zeros_like(l_s)
            acc_s[...] = jnp.zeros_like(acc_s)

        s = _dot_nt(q_ref[...], k_ref[...])
        m_old = m_s[...]
        m_new = jnp.maximum(m_old, jnp.max(s, axis=-1, keepdims=True))
        pr = jnp.exp2(s - m_new)
        alpha = jnp.exp2(m_old - m_new)
        l_s[...] = alpha * l_s[...] + jnp.sum(pr, axis=-1, keepdims=True)
        acc_s[...] = alpha * acc_s[...] + _dot(pr, v_ref[...])
        m_s[...] = m_new

        @pl.when(j == pl.num_programs(2) - 1)
        def _():
            o_ref[...] = acc_s[...] / l_s[...]
            lse_ref[...] = m_s[...] + jnp.log2(l_s[...])

    return pl.pallas_call(
        body, name=name, grid=(nh, t // tq, t // tk),
        in_specs=[pl.BlockSpec((tq, LANE), lambda h, i, j: (i, h)),
                  pl.BlockSpec((tk, LANE), lambda h, i, j: (j, h)),
                  pl.BlockSpec((tk, LANE), lambda h, i, j: (j, h))],
        out_specs=[pl.BlockSpec((tq, LANE), lambda h, i, j: (i, h)),
                   pl.BlockSpec((None, tq, 1), lambda h, i, j: (h, i, 0))],
        out_shape=[jax.ShapeDtypeStruct((t, nh * LANE), F32), jax.ShapeDtypeStruct((nh, t, 1), F32)],
        scratch_shapes=[pltpu.VMEM((tq, 1), F32), pltpu.VMEM((tq, 1), F32), pltpu.VMEM((tq, LANE), F32)],
        compiler_params=_cparams(("parallel", "parallel", "arbitrary")),
    )(q, k, v)


def attn_bwd(q, k, v, o, lse, do, name="attn_bwd"):
    t = q.shape[0]
    tq, tk = min(ATTN_BWD_TQ, _row_tile(t)), min(ATTN_BWD_TK, t)
    nh = MLA_HEADS
    nq = t // tq

    def body(q_ref, k_ref, v_ref, o_ref, lse_ref, do_ref, dq_ref, dk_ref, dv_ref):
        kj, qi = pl.program_id(1), pl.program_id(2)

        @pl.when(qi == 0)
        def _():
            dk_ref[...] = jnp.zeros_like(dk_ref)
            dv_ref[...] = jnp.zeros_like(dv_ref)

        qv, kv, vv, dov = q_ref[...], k_ref[...], v_ref[...], do_ref[...]
        s = _dot_nt(qv, kv)
        pr = jnp.exp2(s - lse_ref[...])
        dp = _dot_nt(dov, vv)
        delta = jnp.sum(dov * o_ref[...], axis=-1, keepdims=True)
        ds = pr * (dp - delta)
        dv_ref[...] += _dot_tn(pr, dov)
        dk_ref[...] += _dot_tn(ds, qv)
        dq_tile = _dot(ds, kv)
        rows = pl.ds(pl.multiple_of(qi * tq, tq), tq)

        @pl.when(kj == 0)
        def _():
            dq_ref[rows, :] = dq_tile

        @pl.when(kj != 0)
        def _():
            dq_ref[rows, :] += dq_tile

    qspec = pl.BlockSpec((tq, LANE), lambda h, j, i: (i, h))
    kspec = pl.BlockSpec((tk, LANE), lambda h, j, i: (j, h))
    return pl.pallas_call(
        body, name=name, grid=(nh, t // tk, nq),
        in_specs=[qspec, kspec, kspec, qspec, pl.BlockSpec((None, tq, 1), lambda h, j, i: (h, i, 0)), qspec],
        out_specs=[pl.BlockSpec((t, LANE), lambda h, j, i: (0, h)), kspec, kspec],
        out_shape=[jax.ShapeDtypeStruct((t, nh * LANE), F32)] * 3,
        compiler_params=_cparams(("parallel", "arbitrary", "arbitrary")),
    )(q, k, v, o, lse, do)


def _scan_consts(c, reverse, inclusive):
    ii = lax.broadcasted_iota(jnp.int32, (c, c), 0)
    jj = lax.broadcasted_iota(jnp.int32, (c, c), 1)
    if reverse:
        incl = jj >= ii
        mask = incl if inclusive else jj > ii
    else:
        incl = jj <= ii
        mask = incl if inclusive else jj < ii
    mid = (c - 1 - c // 2) if reverse else c // 2
    incl_t = (jj <= ii) if reverse else (jj >= ii)
    return incl.astype(F32), incl_t.astype(F32), mask.astype(F32), mid


def _dot_split(a01, x):
    hi = x.astype(BF16)
    lo = (x - hi.astype(F32)).astype(BF16)
    a = a01.astype(BF16)
    return jnp.dot(a, hi, preferred_element_type=F32) + jnp.dot(a, lo, preferred_element_type=F32)


def _sub_masks(sub, dvg, u):
    if sub == 1:
        return None, None
    kl = lax.broadcasted_iota(jnp.int32, (1, LANE), 1)
    vl = lax.broadcasted_iota(jnp.int32, (1, dvg), 1)
    kw, vw = LANE // sub, dvg // sub
    km = (kl >= u * kw) & (kl < (u + 1) * kw)
    vm = (vl >= u * vw) & (vl < (u + 1) * vw)
    return km.astype(F32), vm.astype(F32)


def _block_incl(r, c, reverse, transposed):
    shift = c.bit_length() - 1
    ii = lax.broadcasted_iota(jnp.int32, (r, r), 0)
    jj = lax.broadcasted_iota(jnp.int32, (r, r), 1)
    same = lax.shift_right_logical(ii, shift) == lax.shift_right_logical(jj, shift)
    lower = (jj <= ii) if (reverse == transposed) else (jj >= ii)
    return (same & lower).astype(F32)


def _scan_chunk_fwd(b, la, mid):
    row = lax.broadcasted_iota(jnp.int32, b.shape, 0)
    bm = jnp.sum(jnp.where(row == mid, b, 0.0), axis=0, keepdims=True)
    tot = jnp.sum(la, axis=0, keepdims=True)
    e_qc = jnp.exp(b - bm)
    e_kc = jnp.exp(bm - b)
    e_qe = jnp.exp(b)
    e_kd = jnp.exp(tot - b)
    return e_qc, e_kc, e_qe, e_kd


def scan_fwd(q_arr, k_arr, v_arr, la_arr, *, qcb, kcb, vcb, lacb, la_row, chunk, dvg, sub, reverse, inclusive,
             qscale, kscale, rope=None, name="scan_fwd"):
    t = q_arr.shape[0]
    r = _row_tile(t)
    nb, nc = t // r, r // chunk
    c = chunk
    rb = (lambda j: nb - 1 - j) if reverse else (lambda j: j)
    order = list(range(nc))[::-1] if reverse else list(range(nc))
    half = RET_DK // 2

    def body(*refs):
        if rope is None:
            q_ref, k_ref, v_ref, la_ref, o_ref, st_ref, s_s = refs
        else:
            q_ref, k_ref, v_ref, la_ref, c_ref, s1_ref, s2_ref, o_ref, st_ref, s_s = refs

        @pl.when(pl.program_id(1) == 0)
        def _():
            s_s[...] = jnp.zeros_like(s_s)

        incl, _, mask, mid = _scan_consts(c, reverse, inclusive)
        for ci in order:
            rows = slice(ci * c, (ci + 1) * c)
            qv = q_ref[rows, :] * qscale
            kv = k_ref[rows, :] * kscale
            if rope is not None:
                cs, a1, a2 = c_ref[rows, :], s1_ref[rows, :], s2_ref[rows, :]
                qv = _rope_apply(qv, cs, a1, a2, half)
                kv = _rope_apply(kv, cs, a1, a2, half)
            la = jnp.broadcast_to(la_ref[...], (c, LANE)) if la_row else la_ref[rows, :]
            vv = v_ref[rows, :]
            e_qc, e_kc, e_qe, e_kd = _scan_chunk_fwd(_dot_split(incl, la), la, mid)
            qc, kc, qe, kd = qv * e_qc, kv * e_kc, qv * e_qe, kv * e_kd
            sg = s_s[...]
            st_ref[ci] = sg
            acc = None
            for u in range(sub):
                mu, vmu = _sub_masks(sub, dvg, u)
                qcu = qc if mu is None else qc * mu
                qeu = qe if mu is None else qe * mu
                a = _dot_nt(qcu, kc) * mask
                ou = _dot(a, vv) + _dot_nt(qeu, sg)
                ou = ou if vmu is None else ou * vmu
                acc = ou if acc is None else acc + ou
            o_ref[rows, :] = acc
            decay = jnp.exp(jnp.sum(la, axis=0, keepdims=True))
            s_s[...] = decay * sg + _dot_tn(vv, kd)

    specs = [pl.BlockSpec((r, LANE), lambda g, j: (rb(j), qcb + g)),
             pl.BlockSpec((r, LANE), lambda g, j: (rb(j), kcb + g)),
             pl.BlockSpec((r, dvg), lambda g, j: (rb(j), vcb + g)),
             pl.BlockSpec((1, LANE), lambda g, j: (0, lacb + g)) if la_row
             else pl.BlockSpec((r, LANE), lambda g, j: (rb(j), lacb + g))]
    args = [q_arr, k_arr, v_arr, la_arr]
    if rope is not None:
        specs += [pl.BlockSpec((r, LANE), lambda g, j: (rb(j), 0))] * 3
        args += list(rope)
    return pl.pallas_call(
        body, name=name, grid=(N_GROUPS, nb), in_specs=specs,
        out_specs=[pl.BlockSpec((r, dvg), lambda g, j: (rb(j), g)),
                   pl.BlockSpec((nc, dvg, LANE), lambda g, j: (rb(j), g, 0))],
        out_shape=[jax.ShapeDtypeStruct((t, N_GROUPS * dvg), F32),
                   jax.ShapeDtypeStruct((t // c, N_GROUPS * dvg, LANE), F32)],
        scratch_shapes=[pltpu.VMEM((dvg, LANE), F32)],
        compiler_params=_cparams(("parallel", "arbitrary")),
    )(*args)


def scan_bwd(q_arr, k_arr, v_arr, la_arr, st_arr, do_arr, prev, *, qcb, kcb, vcb, lacb, la_row, chunk, dvg, sub,
             reverse, inclusive, qscale, kscale, rope=None, block_rows=None, name="scan_bwd"):
    t = q_arr.shape[0]
    r = _row_tile(t) if block_rows is None else min(block_rows, t)
    nb, nc = t // r, r // chunk
    c = chunk
    rb = (lambda j: j) if reverse else (lambda j: nb - 1 - j)
    order = list(range(nc)) if reverse else list(range(nc))[::-1]
    half = RET_DK // 2
    n_in = 6 + (3 if rope is not None else 0) + (3 if prev is not None else 0)
    gdt = F32 if prev is None else BF16

    def body(*refs):
        ins, outs = refs[:n_in], refs[n_in:]
        q_ref, k_ref, v_ref, la_ref, st_ref, do_ref = ins[:6]
        pos = 6
        if rope is not None:
            c_ref, s1_ref, s2_ref = ins[pos:pos + 3]
            pos += 3
        if prev is not None:
            pq_ref, pk_ref, pv_ref = ins[pos:pos + 3]
        dq_ref, dk_ref, dv_ref, dla_ref, g_s = outs

        @pl.when(pl.program_id(1) == 0)
        def _():
            g_s[...] = jnp.zeros_like(g_s)
            if la_row:
                dla_ref[...] = jnp.zeros_like(dla_ref)

        incl, _, mask, mid = _scan_consts(c, reverse, inclusive)
        b_all = None if la_row else _dot_split(_block_incl(r, c, reverse, False), la_ref[...])
        pos = lax.broadcasted_iota(jnp.int32, (c, LANE), 0)
        cnt = ((c - pos) if reverse else (pos + 1)).astype(F32)
        dla_sum = jnp.zeros((1, LANE), F32)
        db_parts, dtot_parts = [None] * nc, [None] * nc
        for ci in order:
            rows = slice(ci * c, (ci + 1) * c)
            qv = q_ref[rows, :] * qscale
            kv = k_ref[rows, :] * kscale
            if rope is not None:
                cs, a1, a2 = c_ref[rows, :], s1_ref[rows, :], s2_ref[rows, :]
                qv = _rope_apply(qv, cs, a1, a2, half)
                kv = _rope_apply(kv, cs, a1, a2, half)
            la = jnp.broadcast_to(la_ref[...], (c, LANE)) if la_row else la_ref[rows, :]
            vv = v_ref[rows, :]
            dov = do_ref[rows, :]
            b = _dot_split(incl, la) if la_row else b_all[rows, :]
            e_qc, e_kc, e_qe, e_kd = _scan_chunk_fwd(b, la, mid)
            qc, kc, qe, kd = qv * e_qc, kv * e_kc, qv * e_qe, kv * e_kd
            sg = st_ref[ci]
            gn = g_s[...]
            dqc = jnp.zeros((c, LANE), F32)
            dkc = jnp.zeros((c, LANE), F32)
            dqe = jnp.zeros((c, LANE), F32)
            dvv = _dot_nt(kd, gn)
            ds_direct = jnp.zeros((dvg, LANE), F32)
            for u in range(sub):
                mu, vmu = _sub_masks(sub, dvg, u)
                qcu = qc if mu is None else qc * mu
                qeu = qe if mu is None else qe * mu
                dou = dov if vmu is None else dov * vmu
                a = _dot_nt(qcu, kc) * mask
                da = _dot_nt(dou, vv) * mask
                dvv = dvv + _dot_tn(a, dou)
                t1 = _dot(da, kc)
                dqc = dqc + (t1 if mu is None else t1 * mu)
                dkc = dkc + _dot_tn(da, qcu)
                t2 = _dot(dou, sg)
                dqe = dqe + (t2 if mu is None else t2 * mu)
                ds_direct = ds_direct + _dot_tn(dou, qeu)
            dkd = _dot(vv, gn)
            decay = jnp.exp(jnp.sum(la, axis=0, keepdims=True))
            dtot = jnp.sum(gn * sg, axis=0, keepdims=True) * decay + jnp.sum(dkd * kd, axis=0, keepdims=True)
            db = dqc * qc - dkc * kc + dqe * qe - dkd * kd
            if la_row:
                dla_sum = dla_sum + jnp.sum(db * cnt, axis=0, keepdims=True) + float(c) * dtot
            else:
                db_parts[ci] = db
                dtot_parts[ci] = jnp.broadcast_to(dtot, (c, LANE))
            dqv = dqc * e_qc + dqe * e_qe
            dkv = dkc * e_kc + dkd * e_kd
            if rope is not None:
                dqv = _rope_transpose(dqv, cs, a1, a2, half)
                dkv = _rope_transpose(dkv, cs, a1, a2, half)
            dqv = dqv * qscale
            dkv = dkv * kscale
            if prev is not None:
                dqv = dqv + pq_ref[rows, :]
                dkv = dkv + pk_ref[rows, :]
                dvv = dvv + pv_ref[rows, :]
            dq_ref[rows, :] = dqv.astype(gdt)
            dk_ref[rows, :] = dkv.astype(gdt)
            dv_ref[rows, :] = dvv.astype(gdt)
            g_s[...] = ds_direct + decay * gn
        if la_row:
            dla_ref[...] += dla_sum
        else:
            db_all = jnp.concatenate(db_parts, axis=0)
            dla_ref[...] = _dot_split(_block_incl(r, c, reverse, True), db_all) + jnp.concatenate(dtot_parts, axis=0)

    kblk = lambda cb: pl.BlockSpec((r, LANE), lambda g, j: (rb(j), cb + g))
    vblk = lambda cb: pl.BlockSpec((r, dvg), lambda g, j: (rb(j), cb + g))
    specs = [kblk(qcb), kblk(kcb), vblk(vcb),
             pl.BlockSpec((1, LANE), lambda g, j: (0, lacb + g)) if la_row else kblk(lacb),
             pl.BlockSpec((nc, dvg, LANE), lambda g, j: (rb(j), g, 0)), vblk(0)]
    args = [q_arr, k_arr, v_arr, la_arr, st_arr, do_arr]
    if rope is not None:
        specs += [pl.BlockSpec((r, LANE), lambda g, j: (rb(j), 0))] * 3
        args += list(rope)
    if prev is not None:
        specs += [kblk(0), kblk(0), vblk(0)]
        args += list(prev)
    wk = N_GROUPS * LANE
    outs = [jax.ShapeDtypeStruct((t, wk), gdt), jax.ShapeDtypeStruct((t, wk), gdt),
            jax.ShapeDtypeStruct((t, N_GROUPS * dvg), gdt),
            jax.ShapeDtypeStruct((1, wk) if la_row else (t, wk), F32)]
    return pl.pallas_call(
        body, name=name, grid=(N_GROUPS, nb), in_specs=specs,
        out_specs=[kblk(0), kblk(0), vblk(0),
                   pl.BlockSpec((1, LANE), lambda g, j: (0, g)) if la_row else kblk(0)],
        out_shape=outs,
        scratch_shapes=[pltpu.VMEM((dvg, LANE), F32)],
        compiler_params=_cparams(("parallel", "arbitrary")),
    )(*args)


def _seg_mean(x, seg):
    w = x.shape[1]
    if seg % LANE == 0:
        parts = []
        for s in range(0, w, seg):
            m = jnp.mean(x[:, s:s + seg], axis=-1, keepdims=True)
            parts.append(jnp.broadcast_to(m, (x.shape[0], seg)))
        return jnp.concatenate(parts, axis=1)
    shift = seg.bit_length() - 1
    ii = lax.shift_right_logical(lax.broadcasted_iota(jnp.int32, (w, w), 0), shift)
    jj = lax.shift_right_logical(lax.broadcasted_iota(jnp.int32, (w, w), 1), shift)
    e = (ii == jj).astype(BF16)
    hi = x.astype(BF16)
    lo = (x - hi.astype(F32)).astype(BF16)
    return (jnp.dot(hi, e, preferred_element_type=F32) + jnp.dot(lo, e, preferred_element_type=F32)) * (1.0 / seg)


def gated_norm_fwd(o_f, o_b, gate_arr, gcb, gn, seg, name="gated_norm_fwd"):
    t, w = o_f.shape
    tm = _row_tile(t)

    def body(of_ref, ob_ref, g_ref, gn_ref, y_ref):
        o = of_ref[...] + ob_ref[...]
        r = lax.rsqrt(_seg_mean(o * o, seg) + EPS)
        gt = g_ref[...]
        y_ref[...] = (gt * _sigmoid(gt) * (o * r * gn_ref[...])).astype(BF16)

    bw = max(seg, LANE)
    row = pl.BlockSpec((tm, bw), lambda j, i: (i, j))
    return pl.pallas_call(
        body, name=name, grid=(w // bw, t // tm),
        in_specs=[row, row, pl.BlockSpec((tm, bw), lambda j, i: (i, gcb + j)),
                  pl.BlockSpec((1, bw), lambda j, i: (0, j))],
        out_specs=row, out_shape=jax.ShapeDtypeStruct((t, w), BF16),
        compiler_params=_cparams(("parallel", "parallel")),
    )(o_f, o_b, gate_arr, gn.reshape(1, w))


def gated_norm_bwd(o_f, o_b, gate_arr, gcb, gn, seg, dy, name="gated_norm_bwd"):
    t, w = o_f.shape
    tm = _row_tile(t)

    def body(of_ref, ob_ref, g_ref, gn_ref, dy_ref, do_ref, dg_ref, dgn_ref):
        @pl.when(pl.program_id(1) == 0)
        def _():
            dgn_ref[...] = jnp.zeros_like(dgn_ref)
        o = of_ref[...] + ob_ref[...]
        r = lax.rsqrt(_seg_mean(o * o, seg) + EPS)
        xh = o * r
        gt = g_ref[...]
        sg = _sigmoid(gt)
        dyv = dy_ref[...]
        n = xh * gn_ref[...]
        dg_ref[...] = (dyv * n * (sg * (1.0 + gt * (1.0 - sg)))).astype(BF16)
        dn = dyv * (gt * sg)
        dgn_ref[...] += jnp.sum(dn * xh, axis=0, keepdims=True)
        dxh = dn * gn_ref[...]
        do_ref[...] = r * (dxh - xh * _seg_mean(dxh * xh, seg))

    bw = max(seg, LANE)
    row = pl.BlockSpec((tm, bw), lambda j, i: (i, j))
    vec = pl.BlockSpec((1, bw), lambda j, i: (0, j))
    return pl.pallas_call(
        body, name=name, grid=(w // bw, t // tm),
        in_specs=[row, row, pl.BlockSpec((tm, bw), lambda j, i: (i, gcb + j)), vec, row],
        out_specs=[row, row, vec],
        out_shape=[jax.ShapeDtypeStruct((t, w), F32), jax.ShapeDtypeStruct((t, w), BF16),
                   jax.ShapeDtypeStruct((1, w), F32)],
        compiler_params=_cparams(("parallel", "arbitrary")),
    )(o_f, o_b, gate_arr, gn.reshape(1, w), dy)


def gla_gate_fwd(p, wg, bg, name="gla_gate_fwd"):
    t = p.shape[0]
    tm = _row_tile(t)
    w = wg.shape[1]
    gcb = OD_GA // LANE

    def body(ga_ref, wg_ref, bg_ref, la_ref):
        z = _dot(ga_ref[...], wg_ref[...]) + bg_ref[...]
        la_ref[...] = (jnp.minimum(z, 0.0) - jnp.log(1.0 + jnp.exp(-jnp.abs(z)))) * (1.0 / GLA_TAU)

    return pl.pallas_call(
        body, name=name, grid=(t // tm,),
        in_specs=[pl.BlockSpec((tm, LANE), lambda i: (i, gcb)), pl.BlockSpec((LANE, w), lambda i: (0, 0)),
                  pl.BlockSpec((1, w), lambda i: (0, 0))],
        out_specs=pl.BlockSpec((tm, w), lambda i: (i, 0)),
        out_shape=jax.ShapeDtypeStruct((t, w), F32),
        compiler_params=_cparams(("parallel",)),
    )(p, wg, bg)


def gla_gate_bwd(p, wg, wg_t, bg, dla, name="gla_gate_bwd"):
    t = p.shape[0]
    tm = _row_tile(t)
    w = wg.shape[1]
    gcb = OD_GA // LANE

    def body(ga_ref, wg_ref, wgt_ref, bg_ref, dla_ref, dga_ref, dwg_ref, dbg_ref):
        @pl.when(pl.program_id(0) == 0)
        def _():
            dwg_ref[...] = jnp.zeros_like(dwg_ref)
            dbg_ref[...] = jnp.zeros_like(dbg_ref)
        ga = ga_ref[...]
        z = _dot(ga, wg_ref[...]) + bg_ref[...]
        dz = dla_ref[...] * (1.0 / GLA_TAU) * _sigmoid(-z)
        dga_ref[...] = _dot(dz, wgt_ref[...]).astype(BF16)
        dwg_ref[...] += _dot_tn(ga, dz)
        dbg_ref[...] += jnp.sum(dz, axis=0, keepdims=True)

    return pl.pallas_call(
        body, name=name, grid=(t // tm,),
        in_specs=[pl.BlockSpec((tm, LANE), lambda i: (i, gcb)), pl.BlockSpec((LANE, w), lambda i: (0, 0)),
                  pl.BlockSpec((w, LANE), lambda i: (0, 0)), pl.BlockSpec((1, w), lambda i: (0, 0)),
                  pl.BlockSpec((tm, w), lambda i: (i, 0))],
        out_specs=[pl.BlockSpec((tm, LANE), lambda i: (i, 0)), pl.BlockSpec((LANE, w), lambda i: (0, 0)),
                   pl.BlockSpec((1, w), lambda i: (0, 0))],
        out_shape=[jax.ShapeDtypeStruct((t, LANE), BF16), jax.ShapeDtypeStruct((LANE, w), F32),
                   jax.ShapeDtypeStruct((1, w), F32)],
        compiler_params=_cparams(("arbitrary",)),
    )(p, wg, wg_t, bg, dla)


FFN_COL = 1408


def _shifted(x, prev_row, next_row, first, last):
    tm = x.shape[0]
    row = lax.broadcasted_iota(jnp.int32, x.shape, 0)
    pr = jnp.where(first, 0.0, prev_row)
    nx = jnp.where(last, 0.0, next_row)
    xm1 = jnp.where(row == 0, pr, pltpu.roll(x, 1, 0))
    xp1 = jnp.where(row == tm - 1, nx, pltpu.roll(x, tm - 1, 0))
    return xm1, xp1


def _halo_rows(dtype):
    return SUBLANE * (4 // jnp.dtype(dtype).itemsize)


def _halo_specs(tm, tc, t, colmap, rowaxis, hr):
    nbh = tm // hr
    lasth = t // hr - 1

    def prev(*ids):
        i = ids[rowaxis]
        return (jnp.maximum(i * nbh - 1, 0), colmap(*ids))

    def nxt(*ids):
        i = ids[rowaxis]
        return (jnp.minimum((i + 1) * nbh, lasth), colmap(*ids))

    return pl.BlockSpec((hr, tc), prev), pl.BlockSpec((hr, tc), nxt)


def ffn_act_fwd(up, conv_w, conv_b, name="ffn_act_fwd"):
    t = up.shape[0]
    tm, tc = _row_tile(t), FFN_COL
    ncol = D_FF // tc

    hr = _halo_rows(up.dtype)

    def body(g_ref, gp_ref, gn_ref, v_ref, w_ref, b_ref, a_ref):
        i = pl.program_id(0)
        g = g_ref[...].astype(F32)
        gm1, gp1 = _shifted(g, gp_ref[hr - 1:hr, :].astype(F32), gn_ref[0:1, :].astype(F32), i == 0,
                            i == pl.num_programs(0) - 1)
        cc = w_ref[0:1, :] * gm1 + w_ref[1:2, :] * g + w_ref[2:3, :] * gp1 + b_ref[...]
        a_ref[...] = (cc * _sigmoid(cc) * v_ref[...].astype(F32)).astype(BF16)

    prev, nxt = _halo_specs(tm, tc, t, lambda i, j: j, 0, hr)
    return pl.pallas_call(
        body, name=name, grid=(t // tm, ncol),
        in_specs=[pl.BlockSpec((tm, tc), lambda i, j: (i, j)), prev, nxt,
                  pl.BlockSpec((tm, tc), lambda i, j: (i, j + ncol)),
                  pl.BlockSpec((SUBLANE, tc), lambda i, j: (0, j)), pl.BlockSpec((1, tc), lambda i, j: (0, j))],
        out_specs=pl.BlockSpec((tm, tc), lambda i, j: (i, j)),
        out_shape=jax.ShapeDtypeStruct((t, D_FF), BF16),
        compiler_params=_cparams(("parallel", "parallel")),
    )(up, up, up, up, conv_w, conv_b)


def ffn_act_bwd(up, conv_w, conv_b, dact, name="ffn_act_bwd"):
    t = up.shape[0]
    tm, tc = _row_tile(t), FFN_COL
    ncol = D_FF // tc
    hr = _halo_rows(up.dtype)

    def body(g_ref, gp_ref, gn_ref, v_ref, w_ref, b_ref, da_ref, dc_ref, dv_ref, dw_ref):
        i = pl.program_id(1)

        @pl.when(i == 0)
        def _():
            dw_ref[...] = jnp.zeros_like(dw_ref)
        g = g_ref[...].astype(F32)
        gm1, gp1 = _shifted(g, gp_ref[hr - 1:hr, :].astype(F32), gn_ref[0:1, :].astype(F32), i == 0,
                            i == pl.num_programs(1) - 1)
        cc = w_ref[0:1, :] * gm1 + w_ref[1:2, :] * g + w_ref[2:3, :] * gp1 + b_ref[...]
        sg = _sigmoid(cc)
        da = da_ref[...]
        dv_ref[...] = (da * (cc * sg)).astype(BF16)
        dc = da * v_ref[...].astype(F32) * (sg * (1.0 + cc * (1.0 - sg)))
        dc_ref[...] = dc
        dw_ref[0:1, :] += jnp.sum(dc * gm1, axis=0, keepdims=True)
        dw_ref[1:2, :] += jnp.sum(dc * g, axis=0, keepdims=True)
        dw_ref[2:3, :] += jnp.sum(dc * gp1, axis=0, keepdims=True)
        dw_ref[3:4, :] += jnp.sum(dc, axis=0, keepdims=True)

    prev, nxt = _halo_specs(tm, tc, t, lambda j, i: j, 1, hr)
    tile = pl.BlockSpec((tm, tc), lambda j, i: (i, j))
    return pl.pallas_call(
        body, name=name, grid=(ncol, t // tm),
        in_specs=[tile, prev, nxt, pl.BlockSpec((tm, tc), lambda j, i: (i, j + ncol)),
                  pl.BlockSpec((SUBLANE, tc), lambda j, i: (0, j)), pl.BlockSpec((1, tc), lambda j, i: (0, j)), tile],
        out_specs=[tile, pl.BlockSpec((tm, tc), lambda j, i: (i, j + ncol)),
                   pl.BlockSpec((SUBLANE, tc), lambda j, i: (0, j))],
        out_shape=[jax.ShapeDtypeStruct((t, D_FF), F32), jax.ShapeDtypeStruct((t, 2 * D_FF), BF16),
                   jax.ShapeDtypeStruct((SUBLANE, D_FF), F32)],
        compiler_params=_cparams(("parallel", "arbitrary")),
    )(up, up, up, up, conv_w, conv_b, dact)


def conv_transpose(dc, conv_w, dup, name="conv_transpose"):
    t = dc.shape[0]
    tm, tc = _row_tile(t), FFN_COL
    hr = _halo_rows(dc.dtype)

    def body(d_ref, dp_ref, dn_ref, w_ref, dup_ref, o_ref):
        del dup_ref
        i = pl.program_id(0)
        d = d_ref[...]
        dm1, dp1 = _shifted(d, dp_ref[hr - 1:hr, :], dn_ref[0:1, :], i == 0, i == pl.num_programs(0) - 1)
        o_ref[...] = (w_ref[0:1, :] * dp1 + w_ref[1:2, :] * d + w_ref[2:3, :] * dm1).astype(BF16)

    prev, nxt = _halo_specs(tm, tc, t, lambda i, j: j, 0, hr)
    tile = pl.BlockSpec((tm, tc), lambda i, j: (i, j))
    return pl.pallas_call(
        body, name=name, grid=(t // tm, D_FF // tc),
        in_specs=[tile, prev, nxt, pl.BlockSpec((SUBLANE, tc), lambda i, j: (0, j)),
                  pl.BlockSpec(memory_space=pl.ANY)],
        out_specs=tile, out_shape=jax.ShapeDtypeStruct((t, 2 * D_FF), BF16),
        input_output_aliases={4: 0},
        compiler_params=_cparams(("parallel", "parallel")),
    )(dc, dc, dc, conv_w, dup)


def loss_head(y, target, name="loss_head"):
    t, d = y.shape
    tm = _row_tile(t)

    def body(y_ref, t_ref, dy_ref, l_ref):
        @pl.when(pl.program_id(0) == 0)
        def _():
            l_ref[...] = jnp.zeros_like(l_ref)
        e = y_ref[...] - t_ref[...]
        dy_ref[...] = e * (1.0 / d)
        rowloss = jnp.sum(e * e, axis=-1, keepdims=True) * (0.5 / d)
        l_ref[...] += jnp.sum(rowloss, axis=0, keepdims=True)

    row = pl.BlockSpec((tm, d), lambda i: (i, 0))
    return pl.pallas_call(
        body, name=name, grid=(t // tm,), in_specs=[row, row],
        out_specs=[row, pl.BlockSpec((1, 1), lambda i: (0, 0))],
        out_shape=[jax.ShapeDtypeStruct((t, d), F32), jax.ShapeDtypeStruct((1, 1), F32)],
        compiler_params=_cparams(("arbitrary",)),
    )(y, target)


def _pad_heads(w, heads, width):
    lead = w.shape[:-1]
    w = w.reshape(*lead, heads, width)
    w = jnp.pad(w, [(0, 0)] * len(lead) + [(0, 0), (0, LANE - width)])
    return w.reshape(*lead, heads * LANE)


def _unpad_heads(w, heads, width):
    lead = w.shape[:-1]
    return w.reshape(*lead, heads, LANE)[..., :width].reshape(*lead, heads * width)


def _pad_rows_heads(w, heads, width):
    return _pad_heads(w.T, heads, width).T


def _unpad_rows_heads(w, heads, width):
    return _unpad_heads(w.T, heads, width).T


_EV_REAL = MLA_Q_RANK + MLA_KV_RANK + MLA_ROPE


def prep_even(wts, dt=BF16):
    w_in_t = wts["w_in_t"]
    w_in_tp = jnp.concatenate([w_in_t[:_EV_REAL], jnp.zeros((EV_RQ - _EV_REAL, D_MODEL), w_in_t.dtype),
                               w_in_t[_EV_REAL:]], axis=0).astype(dt)
    wuq = _pad_heads(wts["w_uq"], MLA_HEADS, MLA_QK).astype(dt)
    ukv = wts["w_ukv"].reshape(MLA_KV_RANK, MLA_HEADS, MLA_NOPE + MLA_V)
    wk = _pad_heads(ukv[..., :MLA_NOPE].reshape(MLA_KV_RANK, -1), MLA_HEADS, MLA_NOPE).astype(dt)
    wv = _pad_heads(ukv[..., MLA_NOPE:].reshape(MLA_KV_RANK, -1), MLA_HEADS, MLA_V).astype(dt)
    w_out = wts["w_out"]
    wa = _pad_rows_heads(w_out[:MLA_HEADS * MLA_V], MLA_HEADS, MLA_V).astype(dt)
    wr = w_out[MLA_HEADS * MLA_V:].astype(dt)
    pad1 = lambda v, n: jnp.pad(v.astype(F32), (0, n - v.shape[0])).reshape(1, n)
    lg = lambda th: jnp.log1p(-jnp.exp2(-th.astype(F32)))
    return dict(
        w_in=w_in_tp.T, w_in_t=w_in_tp, wuq=wuq, wuq_t=wuq.T, wk=wk, wk_t=wk.T, wv=wv, wv_t=wv.T,
        wa=wa, wa_t=wa.T, wr=wr, wr_t=wr.T,
        mix_norm=wts["mix_norm"].astype(F32), q_norm=wts["q_norm"].astype(F32).reshape(1, -1),
        kv_norm=wts["kv_norm"].astype(F32).reshape(1, -1),
        qhn=pad1(wts["q_head_norm"], LANE), khn=pad1(wts["k_head_norm"], LANE),
        la_f=jnp.repeat(lg(wts["theta_fwd"]), RET_DK).reshape(1, -1),
        la_b=jnp.repeat(lg(wts["theta_bwd"]), RET_DK).reshape(1, -1),
        out_norm=wts["ret_out_norm"].astype(F32).reshape(-1),
    )


def prep_odd(wts, dt=BF16):
    w_in_t = wts["w_in_t"]
    w_in_tp = jnp.concatenate([w_in_t, jnp.zeros((OD_IN - w_in_t.shape[0], D_MODEL), w_in_t.dtype)],
                              axis=0).astype(dt)
    hk = GLA_HEADS * GLA_DK
    wg = jnp.zeros((LANE, 2 * hk), F32)
    wg = wg.at[:GLA_GATE_RANK, :hk].set(wts["w_gate_fwd"].astype(F32))
    wg = wg.at[GLA_GATE_RANK:2 * GLA_GATE_RANK, hk:].set(wts["w_gate_bwd"].astype(F32))
    wg = wg.astype(dt)
    bg = jnp.concatenate([wts["b_gate_fwd"], wts["b_gate_bwd"]]).astype(F32).reshape(1, -1)
    w_out = wts["w_out"].astype(dt)
    return dict(w_in=w_in_tp.T, w_in_t=w_in_tp, wg=wg, wg_t=wg.T, bg=bg, w_out=w_out, w_out_t=w_out.T,
                mix_norm=wts["mix_norm"].astype(F32), out_norm=wts["gla_out_norm"].astype(F32).reshape(-1))


def prep_ffn(wts, dt=BF16):
    w_up_t = wts["w_up_t"].astype(dt)
    w_down = wts["w_down"].astype(dt)
    cw = jnp.pad(wts["conv_w"].astype(F32), ((0, SUBLANE - 3), (0, 0)))
    return dict(w_up=w_up_t.T, w_up_t=w_up_t, w_down=w_down, w_down_t=w_down.T, conv_w=cw,
                conv_b=wts["conv_b"].astype(F32).reshape(1, -1), norm=wts["norm"].astype(F32))


_RET = dict(qcb=EV_RQ // LANE, kcb=EV_RK // LANE, vcb=EV_RV // LANE, la_row=True, chunk=RET_CHUNK, dvg=LANE,
            sub=2, qscale=1.0, kscale=RET_DK ** -0.5)
_GLA = dict(qcb=OD_Q // LANE, kcb=OD_K // LANE, vcb=OD_V // GLA_DV, la_row=False, chunk=GLA_CHUNK, dvg=GLA_DV,
            sub=1, qscale=GLA_DK ** -0.5, kscale=1.0)
GLA_BWD_ROWS = 256
_FWD_DIR = dict(reverse=False, inclusive=True)
_BWD_DIR = dict(reverse=True, inclusive=False)


def even_fwd(x, w, tabs_mla, tabs_ret, tag):
    h = rmsnorm_fwd(x, w["mix_norm"], name=f"{tag}_norm")
    p = mm_nn(h, w["w_in"], name=f"{tag}_in")
    q, k, v = mla_prep_fwd(p, tabs_mla, w["q_norm"], w["kv_norm"], w["wuq"], w["wk"], w["wv"], w["qhn"], w["khn"],
                           name=f"{tag}_mla_prep")
    o, lse = attn_fwd(q, k, v, name=f"{tag}_attn")
    of, stf = scan_fwd(p, p, p, w["la_f"], lacb=0, rope=tabs_ret, name=f"{tag}_ret_f", **_RET, **_FWD_DIR)
    ob, stb = scan_fwd(p, p, p, w["la_b"], lacb=0, rope=tabs_ret, name=f"{tag}_ret_b", **_RET, **_BWD_DIR)
    r = gated_norm_fwd(of, ob, p, EV_RG // LANE, w["out_norm"], RET_DV, name=f"{tag}_ret_out")
    x1 = mm_nn(o, w["wa"], res=x, name=f"{tag}_out_a")
    x2 = mm_nn(r, w["wr"], res=x1, name=f"{tag}_out_r")
    return x2, dict(x=x, h=h, p=p, q=q, k=k, v=v, o=o, lse=lse, of=of, ob=ob, stf=stf, stb=stb, r=r)


def even_bwd(dx, s, w, tabs_mla, tabs_ret, tag):
    tag = tag + "_b"
    do = mm_nn(dx, w["wa_t"], name=f"{tag}_dout_a")
    dr = mm_nn(dx, w["wr_t"], name=f"{tag}_dout_r")
    d_wa = mm_tn(s["o"], dx, out_dtype=BF16, name=f"{tag}_dwa")
    d_wr = mm_tn(s["r"], dx, out_dtype=BF16, name=f"{tag}_dwr")
    dq, dk, dv = attn_bwd(s["q"], s["k"], s["v"], s["o"], s["lse"], do, name=f"{tag}_attn")
    (dp_mla, d_wuq, d_wk, d_wv, d_qn, d_kvn, d_qhn, d_khn) = mla_prep_bwd(
        s["p"], tabs_mla, w["q_norm"], w["kv_norm"], w["wuq"], w["wk"], w["wv"], w["wuq_t"], w["wk_t"], w["wv_t"],
        w["qhn"], w["khn"], dq, dk, dv, name=f"{tag}_mla_prep")
    d_o, d_gate, d_gn = gated_norm_bwd(s["of"], s["ob"], s["p"], EV_RG // LANE, w["out_norm"], RET_DV, dr,
                                       name=f"{tag}_ret_out")
    p = s["p"]
    g1 = scan_bwd(p, p, p, w["la_f"], s["stf"], d_o, None, lacb=0, rope=tabs_ret, name=f"{tag}_ret_f",
                  **_RET, **_FWD_DIR)
    g2 = scan_bwd(p, p, p, w["la_b"], s["stb"], d_o, g1[:3], lacb=0, rope=tabs_ret, name=f"{tag}_ret_b",
                  **_RET, **_BWD_DIR)
    dp = jnp.concatenate([dp_mla, g2[0], g2[1], g2[2], d_gate], axis=1)
    dh = mm_nn(dp, w["w_in_t"], name=f"{tag}_dh")
    d_win_t = mm_tn(dp, s["h"], out_dtype=BF16, name=f"{tag}_dwin")
    dx_in, d_mix = rmsnorm_bwd(s["x"], w["mix_norm"], dh, dx, name=f"{tag}_norm")
    grads = dict(w_in_t=d_win_t, wuq=d_wuq, wk=d_wk, wv=d_wv, wa=d_wa, wr=d_wr, mix_norm=d_mix, q_norm=d_qn,
                 kv_norm=d_kvn, qhn=d_qhn, khn=d_khn, la_f=g1[3], la_b=g2[3], out_norm=d_gn)
    return dx_in, grads


def odd_fwd(x, w, tag):
    h = rmsnorm_fwd(x, w["mix_norm"], name=f"{tag}_norm")
    p = mm_nn(h, w["w_in"], name=f"{tag}_in")
    la = gla_gate_fwd(p, w["wg"], w["bg"], name=f"{tag}_gate")
    of, stf = scan_fwd(p, p, p, la, lacb=0, name=f"{tag}_gla_f", **_GLA, **_FWD_DIR)
    ob, stb = scan_fwd(p, p, p, la, lacb=N_GROUPS, name=f"{tag}_gla_b", **_GLA, **_BWD_DIR)
    y = gated_norm_fwd(of, ob, p, OD_R // GLA_DV, w["out_norm"], GLA_DV, name=f"{tag}_gla_out")
    x1 = mm_nn(y, w["w_out"], res=x, name=f"{tag}_out")
    return x1, dict(x=x, h=h, p=p, la=la, of=of, ob=ob, stf=stf, stb=stb, y=y)


def odd_bwd(dx, s, w, tag):
    tag = tag + "_b"
    dy = mm_nn(dx, w["w_out_t"], name=f"{tag}_dout")
    d_wout = mm_tn(s["y"], dx, out_dtype=BF16, name=f"{tag}_dwout")
    d_o, d_gate, d_gn = gated_norm_bwd(s["of"], s["ob"], s["p"], OD_R // GLA_DV, w["out_norm"], GLA_DV, dy,
                                       name=f"{tag}_gla_out")
    p, la = s["p"], s["la"]
    g1 = scan_bwd(p, p, p, la, s["stf"], d_o, None, lacb=0, block_rows=GLA_BWD_ROWS, name=f"{tag}_gla_f",
                  **_GLA, **_FWD_DIR)
    g2 = scan_bwd(p, p, p, la, s["stb"], d_o, g1[:3], lacb=N_GROUPS, block_rows=GLA_BWD_ROWS, name=f"{tag}_gla_b",
                  **_GLA, **_BWD_DIR)
    dla = jnp.concatenate([g1[3], g2[3]], axis=1)
    d_ga, d_wg, d_bg = gla_gate_bwd(p, w["wg"], w["wg_t"], w["bg"], dla, name=f"{tag}_gate")
    dp = jnp.concatenate([g2[0], g2[1], g2[2], d_gate, d_ga], axis=1)
    dh = mm_nn(dp, w["w_in_t"], name=f"{tag}_dh")
    d_win_t = mm_tn(dp, s["h"], out_dtype=BF16, name=f"{tag}_dwin")
    dx_in, d_mix = rmsnorm_bwd(s["x"], w["mix_norm"], dh, dx, name=f"{tag}_norm")
    grads = dict(w_in_t=d_win_t, wg=d_wg, bg=d_bg, w_out=d_wout, mix_norm=d_mix, out_norm=d_gn)
    return dx_in, grads


def ffn_fwd(x, w, tag):
    h = rmsnorm_fwd(x, w["norm"], name=f"{tag}_norm")
    up = mm_nn(h, w["w_up"], out_dtype=BF16, name=f"{tag}_up")
    act = ffn_act_fwd(up, w["conv_w"], w["conv_b"], name=f"{tag}_act")
    x1 = mm_nn(act, w["w_down"], res=x, name=f"{tag}_down")
    return x1, dict(x=x, h=h, up=up, act=act)


def ffn_bwd(dx, s, w, tag):
    tag = tag + "_b"
    dact = mm_nn(dx, w["w_down_t"], name=f"{tag}_dact")
    d_wdown = mm_tn(s["act"], dx, out_dtype=BF16, name=f"{tag}_dwdown")
    dc, dup_v, d_conv = ffn_act_bwd(s["up"], w["conv_w"], w["conv_b"], dact, name=f"{tag}_act")
    dup = conv_transpose(dc, w["conv_w"], dup_v, name=f"{tag}_convt")
    dh = mm_nn(dup, w["w_up_t"], name=f"{tag}_dh")
    d_wup_t = mm_tn(dup, s["h"], out_dtype=BF16, name=f"{tag}_dwup")
    dx_in, d_norm = rmsnorm_bwd(s["x"], w["norm"], dh, dx, name=f"{tag}_norm")
    grads = dict(w_up_t=d_wup_t, w_down=d_wdown, conv_w=d_conv[:3], conv_b=d_conv[3], norm=d_norm)
    return dx_in, grads


def unprep_even_grads(g, theta_fwd, theta_bwd):
    d_win_t = jnp.concatenate([g["w_in_t"][:_EV_REAL], g["w_in_t"][EV_RQ:]], axis=0)
    d_uq = _unpad_heads(g["wuq"], MLA_HEADS, MLA_QK)
    dk_ = _unpad_heads(g["wk"], MLA_HEADS, MLA_NOPE).reshape(MLA_KV_RANK, MLA_HEADS, MLA_NOPE)
    dv_ = _unpad_heads(g["wv"], MLA_HEADS, MLA_V).reshape(MLA_KV_RANK, MLA_HEADS, MLA_V)
    d_ukv = jnp.concatenate([dk_, dv_], axis=-1).reshape(MLA_KV_RANK, -1)
    d_wout = jnp.concatenate([_unpad_rows_heads(g["wa"], MLA_HEADS, MLA_V), g["wr"]], axis=0)

    def dtheta(dla, th):
        dlg = dla.reshape(RET_HEADS, RET_DK).sum(axis=-1)
        e = jnp.exp2(-th.astype(F32))
        return dlg * (e * math.log(2.0)) / (1.0 - e)

    return dict(mix_norm=g["mix_norm"].reshape(-1), w_in_t=d_win_t, q_norm=g["q_norm"].reshape(-1),
                kv_norm=g["kv_norm"].reshape(-1), w_uq=d_uq, w_ukv=d_ukv, q_head_norm=g["qhn"].reshape(-1)[:MLA_QK],
                k_head_norm=g["khn"].reshape(-1)[:MLA_QK], theta_fwd=dtheta(g["la_f"], theta_fwd),
                theta_bwd=dtheta(g["la_b"], theta_bwd), ret_out_norm=g["out_norm"].reshape(RET_HEADS, RET_DV),
                w_out=d_wout)


def unprep_odd_grads(g):
    hk = GLA_HEADS * GLA_DK
    return dict(mix_norm=g["mix_norm"].reshape(-1), w_in_t=g["w_in_t"][:OD_GA + 2 * GLA_GATE_RANK],
                w_gate_fwd=g["wg"][:GLA_GATE_RANK, :hk], b_gate_fwd=g["bg"].reshape(-1)[:hk],
                w_gate_bwd=g["wg"][GLA_GATE_RANK:2 * GLA_GATE_RANK, hk:], b_gate_bwd=g["bg"].reshape(-1)[hk:],
                gla_out_norm=g["out_norm"].reshape(GLA_HEADS, GLA_DV), w_out=g["w_out"])


def unprep_ffn_grads(g):
    return dict(norm=g["norm"].reshape(-1), w_up_t=g["w_up_t"], conv_w=g["conv_w"], conv_b=g["conv_b"],
                w_down=g["w_down"])


def _mesh_pos():
    return tuple(lax.axis_index(n) for n in MESH_AXES)


def _slot(px, py, pc):
    return 4 * px + 2 * py + pc


def all_gather_blocks(blk, name="weight_all_gather"):
    r, w = blk.shape

    def body(x_ref, out_ref, send_sems, recv_sems, local_sem):
        x, y, c = _mesh_pos()
        me, sibling = (x, y, c), (x, y, 1 - c)
        chips = [(1 - x, y), (x, 1 - y), (1 - x, 1 - y)]

        def copy(k, block, to, src=None):
            dst = out_ref.at[_slot(*block)]
            return pltpu.make_async_remote_copy(
                src_ref=dst if src is None else src, dst_ref=dst, send_sem=send_sems.at[k],
                recv_sem=recv_sems.at[k], device_id=to, device_id_type=pl.DeviceIdType.MESH)

        mine = pltpu.make_async_copy(x_ref, out_ref.at[_slot(*me)], local_sem)
        mine.start()
        first = [copy(0, me, sibling, src=x_ref)]
        first += [copy(1 + j, me, (*chip, c), src=x_ref) for j, chip in enumerate(chips)]
        for cp in first:
            cp.start()
        passed = [copy(4 + j, (*chip, c), sibling) for j, chip in enumerate(chips)]
        for j, chip in enumerate(chips):
            copy(1 + j, (*chip, c), me).wait_recv()
            passed[j].start()
        copy(0, sibling, me).wait_recv()
        for j, chip in enumerate(chips):
            copy(4 + j, (*chip, 1 - c), me).wait_recv()
        for cp in first + passed:
            cp.wait_send()
        mine.wait()

    return pl.pallas_call(
        body, name=name,
        out_shape=jax.ShapeDtypeStruct((N_DEV, r, w), blk.dtype),
        in_specs=[pl.BlockSpec(memory_space=pl.ANY)],
        out_specs=pl.BlockSpec(memory_space=pl.ANY),
        scratch_shapes=[pltpu.SemaphoreType.DMA((7,)), pltpu.SemaphoreType.DMA((7,)), pltpu.SemaphoreType.DMA],
    )(blk)


N_CHIP = 4


def pair_exchange(other, name="grad_pair_exchange"):
    _, r, w = other.shape

    def body(s_ref, r_ref, send_sems, recv_sems):
        x, y, c = _mesh_pos()
        copies = []
        for s in range(N_CHIP):
            cp = pltpu.make_async_remote_copy(
                src_ref=s_ref.at[s], dst_ref=r_ref.at[s], send_sem=send_sems.at[s], recv_sem=recv_sems.at[s],
                device_id=(x, y, 1 - c), device_id_type=pl.DeviceIdType.MESH)
            cp.start()
            copies.append(cp)
        for cp in copies:
            cp.wait()

    return pl.pallas_call(
        body, name=name,
        out_shape=jax.ShapeDtypeStruct((N_CHIP, r, w), other.dtype),
        in_specs=[pl.BlockSpec(memory_space=pl.ANY)],
        out_specs=pl.BlockSpec(memory_space=pl.ANY),
        scratch_shapes=[pltpu.SemaphoreType.DMA((N_CHIP,)), pltpu.SemaphoreType.DMA((N_CHIP,))],
    )(other)


def chip_exchange(part, name="grad_chip_exchange"):
    _, r, w = part.shape

    def body(s_ref, r_ref, send_sems, recv_sems, local_sem):
        x, y, c = _mesh_pos()
        me = 2 * x + y
        mine = pltpu.make_async_copy(s_ref.at[me], r_ref.at[me], local_sem)
        mine.start()
        copies = []
        for k in range(1, N_CHIP):
            px = 1 - x if (k >> 1) & 1 else x
            py = 1 - y if k & 1 else y
            cp = pltpu.make_async_remote_copy(
                src_ref=s_ref.at[2 * px + py], dst_ref=r_ref.at[me], send_sem=send_sems.at[k - 1],
                recv_sem=recv_sems.at[k - 1], device_id=(px, py, c), device_id_type=pl.DeviceIdType.MESH)
            cp.start()
            copies.append(cp)
        for cp in copies:
            cp.wait()
        mine.wait()

    return pl.pallas_call(
        body, name=name,
        out_shape=jax.ShapeDtypeStruct((N_CHIP, r, w), part.dtype),
        in_specs=[pl.BlockSpec(memory_space=pl.ANY)],
        out_specs=pl.BlockSpec(memory_space=pl.ANY),
        scratch_shapes=[pltpu.SemaphoreType.DMA((N_CHIP - 1,)), pltpu.SemaphoreType.DMA((N_CHIP - 1,)),
                        pltpu.SemaphoreType.DMA],
    )(part)


FLAT_W = 1024
FLAT_TILE = 256


def pair_add(mine, theirs, name="grad_pair_add"):
    n, r, w = mine.shape

    def body(a_ref, b_ref, o_ref):
        o_ref[...] = (a_ref[...].astype(F32) + b_ref[...].astype(F32)).astype(o_ref.dtype)

    tr = _slab_tile(r)
    blk = pl.BlockSpec((n, tr, w), lambda i: (0, i, 0))
    return pl.pallas_call(
        body, name=name, grid=(r // tr,), in_specs=[blk, blk], out_specs=blk,
        out_shape=jax.ShapeDtypeStruct((n, r, w), mine.dtype),
        compiler_params=_cparams(("parallel",)),
    )(mine, theirs)


def sum_slots(recv, name="grad_sum"):
    n, r, w = recv.shape

    def body(r_ref, o_ref):
        acc = r_ref[0].astype(F32)
        for k in range(1, n):
            acc = acc + r_ref[k].astype(F32)
        o_ref[...] = acc

    tr = _slab_tile(r)
    return pl.pallas_call(
        body, name=name, grid=(r // tr,),
        in_specs=[pl.BlockSpec((n, tr, w), lambda i: (0, i, 0))],
        out_specs=pl.BlockSpec((tr, w), lambda i: (i, 0)),
        out_shape=jax.ShapeDtypeStruct((r, w), F32),
        compiler_params=_cparams(("parallel",)),
    )(recv)


def _slab_tile(r):
    return max(t for t in range(SUBLANE, FLAT_TILE + 1, SUBLANE) if r % t == 0)


def adamw(wf, gf, mf, vf, name="adamw"):
    r, w = wf.shape
    tr = _slab_tile(r)

    def body(w_ref, g_ref, m_ref, v_ref, d_ref, m_out, v_out):
        g = g_ref[...]
        m = ADAM_B1 * m_ref[...] + (1.0 - ADAM_B1) * g
        v = ADAM_B2 * v_ref[...] + (1.0 - ADAM_B2) * (g * g)
        m_hat = m / (1.0 - ADAM_B1 ** ADAM_STEP)
        v_hat = v / (1.0 - ADAM_B2 ** ADAM_STEP)
        d_ref[...] = -ADAM_LR * (m_hat / (jnp.sqrt(v_hat) + ADAM_EPS) + ADAM_WD * w_ref[...])
        m_out[...] = m
        v_out[...] = v

    tile = pl.BlockSpec((tr, w), lambda i: (i, 0))
    return pl.pallas_call(
        body, name=name, grid=(r // tr,), in_specs=[tile] * 4, out_specs=[tile] * 3,
        out_shape=[jax.ShapeDtypeStruct((r, w), F32)] * 3,
        compiler_params=_cparams(("parallel",)),
    )(wf, gf, mf, vf)


ROW_FORM = [("w_in_even", "T"), ("w_out_even", "R"), ("w_in_odd", "T"), ("w_out_odd", "R"), ("ffn_w_up", "T"),
            ("ffn_w_down", "R")]
SHARDED_MID = [("mla_w_uq", 2), ("mla_w_ukv", 2)]
SHARDED_SMALL = [("mix_norm_odd", 1), ("gla_w_gate_fwd", 2), ("gla_b_gate_fwd", 1), ("gla_w_gate_bwd", 2),
                 ("gla_b_gate_bwd", 1), ("gla_out_norm", 2), ("ffn_conv_w", 2)]
REPLICATED = ["mix_norm_even", "mla_q_norm", "mla_kv_norm", "mla_q_head_norm", "mla_k_head_norm", "ret_theta_fwd",
              "ret_theta_bwd", "ret_out_norm", "ffn_norm", "ffn_conv_b"]
WEIGHT_NAMES = ["mix_norm_even", "w_in_even", "mla_q_norm", "mla_kv_norm", "mla_w_uq", "mla_w_ukv",
                "mla_q_head_norm", "mla_k_head_norm", "ret_theta_fwd", "ret_theta_bwd", "ret_out_norm", "w_out_even",
                "mix_norm_odd", "w_in_odd", "gla_w_gate_fwd", "gla_b_gate_fwd", "gla_w_gate_bwd", "gla_b_gate_bwd",
                "gla_out_norm", "w_out_odd", "ffn_norm", "ffn_w_up", "ffn_conv_w", "ffn_conv_b", "ffn_w_down"]


def _round_up(n, m):
    return -(-n // m) * m


def _pack_rows(parts, rows):
    flat = jnp.concatenate(parts, axis=-1)
    pad = rows * FLAT_W - flat.shape[-1]
    flat = jnp.pad(flat, [(0, 0)] * (flat.ndim - 1) + [(0, pad)])
    return flat.reshape(*flat.shape[:-1], rows, FLAT_W)


def _row_form(v, form):
    if form == "T":
        v = jnp.swapaxes(v, 1, 2)
    return v.reshape(-1, v.shape[-1])


def _row_counts(local_shapes):
    return [local_shapes[n][0] * local_shapes[n][2 if f == "T" else 1] for n, f in ROW_FORM]


def _tail_layout(local_shapes):
    n_sh = sum(math.prod(local_shapes[n]) for n, _ in SHARDED_MID + SHARDED_SMALL)
    n_rep = sum(math.prod(local_shapes[n]) for n in REPLICATED)
    sh_rows = _round_up(-(-n_sh // FLAT_W), SUBLANE)
    rep_rows = _round_up(-(-n_rep // FLAT_W), SUBLANE)
    return sh_rows, rep_rows, _round_up(sh_rows + rep_rows, FLAT_TILE)


def pack_tail(vals, local_shapes):
    sh_rows, rep_rows, rows = _tail_layout(local_shapes)
    sh = _pack_rows([vals[n].astype(F32).reshape(-1) for n, _ in SHARDED_MID + SHARDED_SMALL], sh_rows)
    rep = _pack_rows([vals[n].astype(F32).reshape(-1) for n in REPLICATED], rep_rows)
    return jnp.concatenate([sh, rep, jnp.zeros((rows - sh_rows - rep_rows, FLAT_W), F32)], axis=0)


def unpack_tail(tail, local_shapes):
    sh_rows, rep_rows, _ = _tail_layout(local_shapes)
    out = {}
    for names, flat in (([n for n, _ in SHARDED_MID + SHARDED_SMALL], tail[:sh_rows].reshape(-1)),
                        (REPLICATED, tail[sh_rows:sh_rows + rep_rows].reshape(-1))):
        off = 0
        for n in names:
            k = math.prod(local_shapes[n])
            out[n] = flat[off:off + k].reshape(local_shapes[n])
            off += k
    return out


def unpack_rows(slab, local_shapes):
    out = {}
    off = 0
    for (n, form), rows in zip(ROW_FORM, _row_counts(local_shapes)):
        loc = local_shapes[n]
        piece = slab[off:off + rows]
        if form == "T":
            piece = jnp.swapaxes(piece.reshape(loc[0], loc[2], loc[1]), 1, 2)
        out[n] = piece.reshape(loc)
        off += rows
    return out


def pack_grad_blocks(full_grads, local_shapes):
    sh_rows, rep_rows, rows = _tail_layout(local_shapes)
    my_c = lax.axis_index("c")

    def by_core(blocks8):
        b = blocks8.reshape(N_CHIP, 2, *blocks8.shape[1:])
        return (lax.dynamic_index_in_dim(b, my_c, 1, keepdims=False),
                lax.dynamic_index_in_dim(b, 1 - my_c, 1, keepdims=False))

    blocks = []
    for n, form in ROW_FORM:
        g = full_grads[n].astype(BF16)
        layers, total = g.shape[0], g.shape[1]
        g = g.reshape(layers, N_DEV, total // N_DEV, FLAT_W)
        blocks.append(by_core(jnp.swapaxes(g, 0, 1).reshape(N_DEV, -1, FLAT_W)))
    parts = []
    for n, ax in SHARDED_MID + SHARDED_SMALL:
        g = full_grads[n].astype(F32)
        loc = local_shapes[n]
        g = g.reshape(*g.shape[:ax], N_DEV, loc[ax], *g.shape[ax + 1:])
        parts.append(jnp.moveaxis(g, ax, 0).reshape(N_DEV, -1))
    sh = _pack_rows(parts, sh_rows)
    rep = _pack_rows([full_grads[n].astype(F32).reshape(-1) for n in REPLICATED], rep_rows)
    rep = jnp.broadcast_to(rep[None], (N_DEV, rep_rows, FLAT_W))
    pad = jnp.zeros((N_DEV, rows - sh_rows - rep_rows, FLAT_W), F32)
    blocks.append(by_core(jnp.concatenate([sh, rep, pad], axis=1).astype(BF16)))
    return (jnp.concatenate([b[0] for b in blocks], axis=1), jnp.concatenate([b[1] for b in blocks], axis=1))


def pack_gather_block(vals, local_shapes):
    big = jnp.concatenate([_row_form(vals[n].astype(BF16), f) for n, f in ROW_FORM], axis=0)
    mid = [vals[n].astype(BF16).reshape(-1) for n, _ in SHARDED_MID]
    small = jnp.concatenate([vals[n].astype(F32).reshape(-1) for n, _ in SHARDED_SMALL])
    small = lax.bitcast_convert_type(small, BF16).reshape(-1)
    n = sum(v.shape[0] for v in mid) + small.shape[0]
    tail = _pack_rows(mid + [small], _round_up(-(-n // FLAT_W), 2 * SUBLANE))
    return jnp.concatenate([big, tail], axis=0)


def unpack_gathered(gathered, local_shapes):
    out = {}
    off = 0
    for (n, form), rows in zip(ROW_FORM, _row_counts(local_shapes)):
        layers = local_shapes[n][0]
        piece = gathered[:, off:off + rows].reshape(N_DEV, layers, rows // layers, FLAT_W)
        out[n] = jnp.swapaxes(piece, 0, 1).reshape(layers, N_DEV * (rows // layers), FLAT_W)
        off += rows
    flat = gathered[:, off:].reshape(N_DEV, -1)
    off = 0

    def full(piece, n, ax):
        loc = local_shapes[n]
        piece = jnp.moveaxis(piece.reshape(N_DEV, *loc), 0, ax)
        return piece.reshape(*loc[:ax], N_DEV * loc[ax], *loc[ax + 1:])

    for n, ax in SHARDED_MID:
        k = math.prod(local_shapes[n])
        out[n] = full(flat[:, off:off + k], n, ax)
        off += k
    for n, ax in SHARDED_SMALL:
        k = math.prod(local_shapes[n])
        pairs = flat[:, off:off + 2 * k].reshape(N_DEV, k, 2)
        out[n] = full(lax.bitcast_convert_type(pairs, F32), n, ax)
        off += 2 * k
    return out


def local_step(x, positions, target, fw):
    tabs_mla = rope_tables(positions, MLA_NOPE, MLA_ROPE // 2, LANE)
    tabs_ret = rope_tables(positions, 0, RET_DK // 2, RET_DK)
    layers = []
    for layer in range(DEPTH):
        i = layer // 2
        if layer % 2 == 0:
            mw = prep_even(dict(
                w_in_t=fw["w_in_even"][i], w_uq=fw["mla_w_uq"][i], w_ukv=fw["mla_w_ukv"][i], w_out=fw["w_out_even"][i],
                mix_norm=fw["mix_norm_even"][i], q_norm=fw["mla_q_norm"][i], kv_norm=fw["mla_kv_norm"][i],
                q_head_norm=fw["mla_q_head_norm"][i], k_head_norm=fw["mla_k_head_norm"][i],
                theta_fwd=fw["ret_theta_fwd"][i], theta_bwd=fw["ret_theta_bwd"][i],
                ret_out_norm=fw["ret_out_norm"][i]))
        else:
            mw = prep_odd(dict(
                w_in_t=fw["w_in_odd"][i], w_gate_fwd=fw["gla_w_gate_fwd"][i], b_gate_fwd=fw["gla_b_gate_fwd"][i],
                w_gate_bwd=fw["gla_w_gate_bwd"][i], b_gate_bwd=fw["gla_b_gate_bwd"][i],
                gla_out_norm=fw["gla_out_norm"][i], w_out=fw["w_out_odd"][i], mix_norm=fw["mix_norm_odd"][i]))
        fwt = prep_ffn(dict(norm=fw["ffn_norm"][layer], w_up_t=fw["ffn_w_up"][layer], conv_w=fw["ffn_conv_w"][layer],
                            conv_b=fw["ffn_conv_b"][layer], w_down=fw["ffn_w_down"][layer]))
        layers.append((mw, fwt))

    saved = []
    for layer, (mw, fwt) in enumerate(layers):
        if layer % 2 == 0:
            x, sm = even_fwd(x, mw, tabs_mla, tabs_ret, f"l{layer}_mix")
        else:
            x, sm = odd_fwd(x, mw, f"l{layer}_mix")
        x, sf = ffn_fwd(x, fwt, f"l{layer}_ffn")
        saved.append((sm, sf))

    dx, loss = loss_head(x, target, name="loss_head")

    per_layer = [None] * DEPTH
    for layer in reversed(range(DEPTH)):
        mw, fwt = layers[layer]
        sm, sf = saved[layer]
        dx, gf = ffn_bwd(dx, sf, fwt, f"l{layer}_ffn")
        if layer % 2 == 0:
            dx, gm = even_bwd(dx, sm, mw, tabs_mla, tabs_ret, f"l{layer}_mix")
            gm = unprep_even_grads(gm, fw["ret_theta_fwd"][layer // 2], fw["ret_theta_bwd"][layer // 2])
        else:
            dx, gm = odd_bwd(dx, sm, mw, f"l{layer}_mix")
            gm = unprep_odd_grads(gm)
        per_layer[layer] = (gm, unprep_ffn_grads(gf))

    ev = [per_layer[l][0] for l in range(0, DEPTH, 2)]
    od = [per_layer[l][0] for l in range(1, DEPTH, 2)]
    ff = [per_layer[l][1] for l in range(DEPTH)]
    st = lambda lst, key: jnp.stack([g[key] for g in lst])
    grads = {
        "mix_norm_even": st(ev, "mix_norm"), "w_in_even": st(ev, "w_in_t"), "mla_q_norm": st(ev, "q_norm"),
        "mla_kv_norm": st(ev, "kv_norm"), "mla_w_uq": st(ev, "w_uq"), "mla_w_ukv": st(ev, "w_ukv"),
        "mla_q_head_norm": st(ev, "q_head_norm"), "mla_k_head_norm": st(ev, "k_head_norm"),
        "ret_theta_fwd": st(ev, "theta_fwd"), "ret_theta_bwd": st(ev, "theta_bwd"),
        "ret_out_norm": st(ev, "ret_out_norm"), "w_out_even": st(ev, "w_out"),
        "mix_norm_odd": st(od, "mix_norm"), "w_in_odd": st(od, "w_in_t"), "gla_w_gate_fwd": st(od, "w_gate_fwd"),
        "gla_b_gate_fwd": st(od, "b_gate_fwd"), "gla_w_gate_bwd": st(od, "w_gate_bwd"),
        "gla_b_gate_bwd": st(od, "b_gate_bwd"), "gla_out_norm": st(od, "gla_out_norm"), "w_out_odd": st(od, "w_out"),
        "ffn_norm": st(ff, "norm"), "ffn_w_up": st(ff, "w_up_t"), "ffn_conv_w": st(ff, "conv_w"),
        "ffn_conv_b": st(ff, "conv_b"), "ffn_w_down": st(ff, "w_down"),
    }
    return loss, dx, grads


def kernel(x, positions, mix_norm_even, w_in_even, mla_q_norm, mla_kv_norm, mla_w_uq, mla_w_ukv, mla_q_head_norm, mla_k_head_norm, ret_theta_fwd, ret_theta_bwd, ret_out_norm, w_out_even, mix_norm_odd, w_in_odd, gla_w_gate_fwd, gla_b_gate_fwd, gla_w_gate_bwd, gla_b_gate_bwd, gla_out_norm, w_out_odd, ffn_norm, ffn_w_up, ffn_conv_w, ffn_conv_b, ffn_w_down, loss_target, m_mix_norm_even, m_w_in_even, m_mla_q_norm, m_mla_kv_norm, m_mla_w_uq, m_mla_w_ukv, m_mla_q_head_norm, m_mla_k_head_norm, m_ret_theta_fwd, m_ret_theta_bwd, m_ret_out_norm, m_w_out_even, m_mix_norm_odd, m_w_in_odd, m_gla_w_gate_fwd, m_gla_b_gate_fwd, m_gla_w_gate_bwd, m_gla_b_gate_bwd, m_gla_out_norm, m_w_out_odd, m_ffn_norm, m_ffn_w_up, m_ffn_conv_w, m_ffn_conv_b, m_ffn_w_down, v_mix_norm_even, v_w_in_even, v_mla_q_norm, v_mla_kv_norm, v_mla_w_uq, v_mla_w_ukv, v_mla_q_head_norm, v_mla_k_head_norm, v_ret_theta_fwd, v_ret_theta_bwd, v_ret_out_norm, v_w_out_even, v_mix_norm_odd, v_w_in_odd, v_gla_w_gate_fwd, v_gla_b_gate_fwd, v_gla_w_gate_bwd, v_gla_b_gate_bwd, v_gla_out_norm, v_w_out_odd, v_ffn_norm, v_ffn_w_up, v_ffn_conv_w, v_ffn_conv_b, v_ffn_w_down):
    a = dict(locals())
    wts = {n: a[n] for n in WEIGHT_NAMES}
    local_shapes = {n: tuple(wts[n].shape) for n in WEIGHT_NAMES}

    gathered = all_gather_blocks(pack_gather_block(wts, local_shapes))
    fw = unpack_gathered(gathered, local_shapes)
    for n in REPLICATED:
        fw[n] = wts[n]

    loss, grad_x, grads = local_step(x[0], positions, loss_target[0], fw)

    mine, other = pack_grad_blocks(grads, local_shapes)
    chip_part = pair_add(mine, pair_exchange(other))
    g_slab = sum_slots(chip_exchange(chip_part))
    n_big = sum(_row_counts(local_shapes))
    ms = {n: a["m_" + n] for n in WEIGHT_NAMES}
    vs = {n: a["v_" + n] for n in WEIGHT_NAMES}

    g_out = unpack_rows(g_slab[:n_big], local_shapes)
    d_out, m_out, v_out = {}, {}, {}
    for n, _ in ROW_FORM:
        loc = local_shapes[n]
        two_d = lambda t: t.reshape(-1, loc[-1])
        d, m, v = adamw(two_d(wts[n]), two_d(g_out[n]), two_d(ms[n]), two_d(vs[n]), name=f"adamw_{n}")
        d_out[n], m_out[n], v_out[n] = d.reshape(loc), m.reshape(loc), v.reshape(loc)
    g_tail = g_slab[n_big:]
    d_tail, m_tail, v_tail = adamw(pack_tail(wts, local_shapes), g_tail, pack_tail(ms, local_shapes),
                                   pack_tail(vs, local_shapes), name="adamw_small")
    g_out.update(unpack_tail(g_tail, local_shapes))
    d_out.update(unpack_tail(d_tail, local_shapes))
    m_out.update(unpack_tail(m_tail, local_shapes))
    v_out.update(unpack_tail(v_tail, local_shapes))
    total = lax.psum(loss[0, 0], MESH_AXES)
    return (total, grad_x[None], *[g_out[n] for n in WEIGHT_NAMES], *[d_out[n] for n in WEIGHT_NAMES],
            *[m_out[n] for n in WEIGHT_NAMES], *[v_out[n] for n in WEIGHT_NAMES])
```
